```python
import math
import numpy as np
import jax, jax.numpy as jnp
from jax import lax

D_MODEL = 1024
BATCH = 8
SEQ = 2048
DEPTH = 4

CHUNK = 64
N_MEM = 256
RMS_EPS = 1e-6
S5_WIDTH = D_MODEL // 2
S5_GROUP = 16
S5_GROUPS = S5_WIDTH // S5_GROUP
S5_STATE = 64
S5_DT_MIN = 0.001
S5_DT_MAX = 0.1
GDN_HEAD_DIM = 128
GDN_WIDTH = D_MODEL // 2
GDN_HEADS = GDN_WIDTH // GDN_HEAD_DIM
GDN_CONV = 4
GDN_DT_MIN = 0.001
GDN_DT_MAX = 0.1
AB_IN = S5_WIDTH + 4 * GDN_WIDTH + 2 * GDN_HEADS
CA_HEADS = 16
CA_HEAD_DIM = D_MODEL // CA_HEADS
CA_LEFT_CHUNKS = 8
CA_BAND = (CA_LEFT_CHUNKS + 1) * CHUNK
MAX_REL_DIST = 128
XA_HEADS = 4
XA_HEAD_DIM = D_MODEL // XA_HEADS
FFN_HIDDEN = -(-8 * D_MODEL // (3 * 256)) * 256
N_EVEN = (DEPTH + 1) // 2
N_ODD = DEPTH // 2

kernel_name = "hybrid_s5_deltanet_chunkattn_encoder"

F32 = jnp.float32


def rms_norm(x, g):
    x32 = x.astype(F32)
    y = x32 * lax.rsqrt(jnp.mean(x32 * x32, axis=-1, keepdims=True) + RMS_EPS)
    return (y * g.astype(F32)).astype(x.dtype)


def l2_norm(t):
    return t * lax.rsqrt(jnp.sum(t * t, axis=-1, keepdims=True) + RMS_EPS)


def s5_mixer(u, a_re, a_im, log_dt, b_re, b_im, c_re, c_im, d, w_glu, b_glu):
    bsz, seq, _ = u.shape
    u32 = u.astype(F32).reshape(bsz, seq, S5_GROUPS, S5_GROUP)
    lam = lax.complex(a_re.astype(F32), a_im.astype(F32))
    dt = jnp.exp(log_dt.astype(F32))[:, None]
    lam_bar = jnp.exp(lam * dt)
    b_bar = ((lam_bar - 1.0) / lam)[:, :, None] * lax.complex(b_re.astype(F32), b_im.astype(F32))
    c = lax.complex(c_re.astype(F32), c_im.astype(F32))
    bu = jnp.einsum('bsgc,gpc->bsgp', u32.astype(jnp.complex64), b_bar)
    lam_seq = jnp.broadcast_to(lam_bar, bu.shape)

    def combine(e_prev, e_next):
        a_p, s_p = e_prev
        a_n, s_n = e_next
        return a_n * a_p, a_n * s_p + s_n

    _, states = lax.associative_scan(combine, (lam_seq, bu), axis=1)
    y = jnp.real(jnp.einsum('bsgp,gcp->bsgc', states, c)) + d.astype(F32) * u32
    h = jax.nn.gelu(y.reshape(bsz, seq, S5_WIDTH))
    out = h * jax.nn.sigmoid(h @ w_glu.astype(F32) + b_glu.astype(F32))
    return out.astype(u.dtype)


def gated_deltanet(q, k, v, gate, a_logit, b_logit, conv_w, a_log, dt_bias, out_g):
    bsz, seq, _ = q.shape
    n = seq // CHUNK
    qkv = jnp.concatenate([q, k, v], axis=-1).astype(F32)
    qkv = lax.conv_general_dilated(
        qkv, conv_w.astype(F32)[:, None, :], window_strides=(1,),
        padding=[(GDN_CONV - 1, 0)], dimension_numbers=('NWC', 'WIO', 'NWC'),
        feature_group_count=3 * GDN_WIDTH)
    qkv = jax.nn.silu(qkv)
    q, k, v = jnp.split(qkv, 3, axis=-1)

    def heads(t):
        return t.reshape(bsz, n, CHUNK, GDN_HEADS, GDN_HEAD_DIM).transpose(0, 3, 1, 2, 4)

    def per_head(t):
        return t.reshape(bsz, n, CHUNK, GDN_HEADS).transpose(0, 3, 1, 2)

    q = l2_norm(heads(q)) * (GDN_HEAD_DIM ** -0.5)
    k = l2_norm(heads(k))
    v = heads(v)
    beta = per_head(jax.nn.sigmoid(b_logit.astype(F32)))
    g = -jnp.exp(a_log.astype(F32)) * jax.nn.softplus(a_logit.astype(F32) + dt_bias.astype(F32))
    gcum = jnp.cumsum(per_head(g), axis=-1)
    gamma = jnp.exp(gcum)
    diff = gcum[..., :, None] - gcum[..., None, :]
    idx = jnp.arange(CHUNK)
    strict = idx[:, None] > idx[None, :]
    causal = idx[:, None] >= idx[None, :]
    decay_strict = jnp.where(strict, jnp.exp(jnp.where(strict, diff, 0.0)), 0.0)
    decay_causal = jnp.where(causal, jnp.exp(jnp.where(causal, diff, 0.0)), 0.0)
    kk = jnp.einsum('bhntd,bhnjd->bhntj', k, k)
    lower = jnp.eye(CHUNK, dtype=F32) + beta[..., :, None] * kk * decay_strict
    rhs = jnp.concatenate([beta[..., None] * v, (beta * gamma)[..., None] * k], axis=-1)
    sol = lax.linalg.triangular_solve(lower, rhs, left_side=True, lower=True, unit_diagonal=True)
    u_new, w_k = jnp.split(sol, [GDN_HEAD_DIM], axis=-1)
    qk = jnp.einsum('bhntd,bhnjd->bhntj', q, k) * decay_causal
    q_g = q * gamma[..., None]
    k_tail = k * jnp.exp(gcum[..., -1:] - gcum)[..., None]
    chunk_decay = jnp.exp(gcum[..., -1])

    def step(state, inp):
        u_c, wk_c, qk_c, qg_c, kt_c, dec_c = inp
        w = u_c - jnp.einsum('bhtd,bhde->bhte', wk_c, state)
        o = jnp.einsum('bhtd,bhde->bhte', qg_c, state) + jnp.einsum('bhtj,bhje->bhte', qk_c, w)
        state = dec_c[..., None, None] * state + jnp.einsum('bhtd,bhte->bhde', kt_c, w)
        return state, o

    state0 = jnp.zeros((bsz, GDN_HEADS, GDN_HEAD_DIM, GDN_HEAD_DIM), F32)
    xs = (jnp.moveaxis(u_new, 2, 0), jnp.moveaxis(w_k, 2, 0), jnp.moveaxis(qk, 2, 0),
          jnp.moveaxis(q_g, 2, 0), jnp.moveaxis(k_tail, 2, 0), jnp.moveaxis(chunk_decay, 2, 0))
    _, o = lax.scan(step, state0, xs)
    o = o.transpose(1, 0, 3, 2, 4).reshape(bsz, seq, GDN_HEADS, GDN_HEAD_DIM)
    o = rms_norm(o, out_g) * jax.nn.silu(gate.astype(F32).reshape(bsz, seq, GDN_HEADS, GDN_HEAD_DIM))
    return o.reshape(bsz, seq, GDN_WIDTH).astype(gate.dtype)


def chunk_attention(h, w_qkv, w_out, q_g, k_g, rel_bias):
    bsz, seq, _ = h.shape
    n = seq // CHUNK
    qkv = (h @ w_qkv).reshape(bsz, seq, 3, CA_HEADS, CA_HEAD_DIM)
    q = rms_norm(qkv[:, :, 0], q_g).astype(F32)
    k = rms_norm(qkv[:, :, 1], k_g).astype(F32)
    v = qkv[:, :, 2].astype(F32)
    pad = CA_LEFT_CHUNKS * CHUNK
    k_pad = jnp.pad(k, ((0, 0), (pad, 0), (0, 0), (0, 0)))
    v_pad = jnp.pad(v, ((0, 0), (pad, 0), (0, 0), (0, 0)))
    rel = np.arange(CHUNK)[:, None] - np.arange(CA_BAND)[None, :] + pad
    rel_idx = np.clip(rel, -MAX_REL_DIST, MAX_REL_DIST) + MAX_REL_DIST
    bias = rel_bias.astype(F32)[:, rel_idx]
    band_pos = jnp.arange(CA_BAND) - pad
    scale = CA_HEAD_DIM ** -0.5

    def one_chunk(c):
        start = c * CHUNK
        q_c = lax.dynamic_slice_in_dim(q, start, CHUNK, axis=1)
        k_c = lax.dynamic_slice_in_dim(k_pad, start, CA_BAND, axis=1)
        v_c = lax.dynamic_slice_in_dim(v_pad, start, CA_BAND, axis=1)
        s = jnp.einsum('bqhd,bkhd->bhqk', q_c, k_c) * scale + bias
        s = jnp.where((band_pos + start) >= 0, s, -1e30)
        p = jax.nn.softmax(s, axis=-1)
        return jnp.einsum('bhqk,bkhd->bqhd', p, v_c)

    o = lax.map(one_chunk, jnp.arange(n))
    o = o.transpose(1, 0, 2, 3, 4).reshape(bsz, seq, D_MODEL).astype(h.dtype)
    return o @ w_out


def memory_cross_attention(h, mem_n, w_q, w_kv, w_out, q_g, k_g):
    bsz, seq, _ = h.shape
    n_mem = mem_n.shape[1]
    q = rms_norm((h @ w_q).reshape(bsz, seq, XA_HEADS, XA_HEAD_DIM), q_g).astype(F32)
    kv = (mem_n @ w_kv).reshape(bsz, n_mem, 2, XA_HEADS, XA_HEAD_DIM)
    k = rms_norm(kv[:, :, 0], k_g).astype(F32)
    v = kv[:, :, 1].astype(F32)
    s = jnp.einsum('bqhd,bkhd->bhqk', q, k) * (XA_HEAD_DIM ** -0.5)
    p = jax.nn.softmax(s, axis=-1)
    o = jnp.einsum('bhqk,bkhd->bqhd', p, v).reshape(bsz, seq, D_MODEL).astype(h.dtype)
    return o @ w_out


def swiglu(h, w_gate, w_up, w_down):
    return (jax.nn.silu(h @ w_gate) * (h @ w_up)) @ w_down


def _fwd_setup_inputs(seed: int = 0) -> dict:
    key = jax.random.key(seed)
    keys = jax.random.split(key, 64)
    counter = iter(range(64))

    def nk():
        return keys[next(counter)]

    def nrm(shape, scale):
        return scale * jax.random.normal(nk(), shape, F32)

    def gain(shape):
        return 1.0 + nrm(shape, 0.02)

    D, NE, NO = D_MODEL, N_EVEN, N_ODD
    G, P, CG = S5_GROUPS, S5_STATE, S5_GROUP
    x = nrm((BATCH, SEQ, D), 1.0)
    mem = nrm((BATCH, N_MEM, D), 1.0)
    ab_norm_g = gain((NE, D))
    ab_w_in = nrm((NE, D, AB_IN), D ** -0.5)
    ab_w_out = nrm((NE, S5_WIDTH + GDN_WIDTH, D), (S5_WIDTH + GDN_WIDTH) ** -0.5)
    s5_a_re = -0.5 + nrm((NE, G, P), 0.01)
    s5_a_im = math.pi * jnp.arange(P, dtype=F32) + nrm((NE, G, P), 0.01)
    s5_log_dt = jax.random.uniform(nk(), (NE, G), F32, math.log(S5_DT_MIN), math.log(S5_DT_MAX))
    s5_b_re = nrm((NE, G, P, CG), (2 * CG) ** -0.5)
    s5_b_im = nrm((NE, G, P, CG), (2 * CG) ** -0.5)
    s5_c_re = nrm((NE, G, CG, P), (2 * P) ** -0.5)
    s5_c_im = nrm((NE, G, CG, P), (2 * P) ** -0.5)
    s5_d = nrm((NE, G, CG), 1.0)
    s5_w_glu = nrm((NE, S5_WIDTH, S5_WIDTH), S5_WIDTH ** -0.5)
    s5_b_glu = nrm((NE, S5_WIDTH), 0.01)
    gdn_conv_w = nrm((NE, GDN_CONV, 3 * GDN_WIDTH), GDN_CONV ** -0.5)
    gdn_a_log = jnp.log(jax.random.uniform(nk(), (NE, GDN_HEADS), F32, 1.0, 16.0))
    gdn_dt = jnp.exp(jax.random.uniform(nk(), (NE, GDN_HEADS), F32, math.log(GDN_DT_MIN), math.log(GDN_DT_MAX)))
    gdn_dt_bias = gdn_dt + jnp.log(-jnp.expm1(-gdn_dt))
    gdn_out_norm_g = gain((NE, GDN_HEAD_DIM))
    c_norm_g = gain((NO, D))
    c_w_qkv = nrm((NO, D, 3 * D), D ** -0.5)
    c_w_out = nrm((NO, D, D), D ** -0.5)
    c_q_norm_g = gain((NO, CA_HEAD_DIM))
    c_k_norm_g = gain((NO, CA_HEAD_DIM))
    c_rel_bias = nrm((NO, CA_HEADS, 2 * MAX_REL_DIST + 1), 0.1)
    mem_norm_g = gain((D,))
    xa_norm_g = gain((DEPTH, D))
    xa_w_q = nrm((DEPTH, D, D), D ** -0.5)
    xa_w_kv = nrm((DEPTH, D, 2 * D), D ** -0.5)
    xa_w_out = nrm((DEPTH, D, D), D ** -0.5)
    xa_q_norm_g = gain((DEPTH, XA_HEAD_DIM))
    xa_k_norm_g = gain((DEPTH, XA_HEAD_DIM))
    f_norm_g = gain((DEPTH, D))
    f_w_gate = nrm((DEPTH, D, FFN_HIDDEN), D ** -0.5)
    f_w_up = nrm((DEPTH, D, FFN_HIDDEN), D ** -0.5)
    f_w_down = nrm((DEPTH, FFN_HIDDEN, D), FFN_HIDDEN ** -0.5)
    return {
        "x": x, "mem": mem,
        "ab_norm_g": ab_norm_g, "ab_w_in": ab_w_in, "ab_w_out": ab_w_out,
        "s5_a_re": s5_a_re, "s5_a_im": s5_a_im, "s5_log_dt": s5_log_dt,
        "s5_b_re": s5_b_re, "s5_b_im": s5_b_im, "s5_c_re": s5_c_re, "s5_c_im": s5_c_im,
        "s5_d": s5_d, "s5_w_glu": s5_w_glu, "s5_b_glu": s5_b_glu,
        "gdn_conv_w": gdn_conv_w, "gdn_a_log": gdn_a_log, "gdn_dt_bias": gdn_dt_bias,
        "gdn_out_norm_g": gdn_out_norm_g,
        "c_norm_g": c_norm_g, "c_w_qkv": c_w_qkv, "c_w_out": c_w_out,
        "c_q_norm_g": c_q_norm_g, "c_k_norm_g": c_k_norm_g, "c_rel_bias": c_rel_bias,
        "mem_norm_g": mem_norm_g, "xa_norm_g": xa_norm_g, "xa_w_q": xa_w_q,
        "xa_w_kv": xa_w_kv, "xa_w_out": xa_w_out, "xa_q_norm_g": xa_q_norm_g,
        "xa_k_norm_g": xa_k_norm_g,
        "f_norm_g": f_norm_g, "f_w_gate": f_w_gate, "f_w_up": f_w_up, "f_w_down": f_w_down,
    }


def _fwd_reference(x, mem, ab_norm_g, ab_w_in, ab_w_out, s5_a_re, s5_a_im, s5_log_dt,
              s5_b_re, s5_b_im, s5_c_re, s5_c_im, s5_d, s5_w_glu, s5_b_glu,
              gdn_conv_w, gdn_a_log, gdn_dt_bias, gdn_out_norm_g,
              c_norm_g, c_w_qkv, c_w_out, c_q_norm_g, c_k_norm_g, c_rel_bias,
              mem_norm_g, xa_norm_g, xa_w_q, xa_w_kv, xa_w_out, xa_q_norm_g, xa_k_norm_g,
              f_norm_g, f_w_gate, f_w_up, f_w_down):
    split_points = [S5_WIDTH, S5_WIDTH + GDN_WIDTH, S5_WIDTH + 2 * GDN_WIDTH,
                    S5_WIDTH + 3 * GDN_WIDTH, S5_WIDTH + 4 * GDN_WIDTH,
                    S5_WIDTH + 4 * GDN_WIDTH + GDN_HEADS]
    mem_n = rms_norm(mem, mem_norm_g)
    for layer in range(DEPTH):
        i = layer // 2
        if layer % 2 == 0:
            h = rms_norm(x, ab_norm_g[i])
            proj = h @ ab_w_in[i]
            u, q, k, v, gate, a_logit, b_logit = jnp.split(proj, split_points, axis=-1)
            a_out = s5_mixer(u, s5_a_re[i], s5_a_im[i], s5_log_dt[i], s5_b_re[i], s5_b_im[i],
                             s5_c_re[i], s5_c_im[i], s5_d[i], s5_w_glu[i], s5_b_glu[i])
            b_out = gated_deltanet(q, k, v, gate, a_logit, b_logit, gdn_conv_w[i],
                                   gdn_a_log[i], gdn_dt_bias[i], gdn_out_norm_g[i])
            x = x + jnp.concatenate([a_out, b_out], axis=-1) @ ab_w_out[i]
        else:
            x = x + chunk_attention(rms_norm(x, c_norm_g[i]), c_w_qkv[i], c_w_out[i],
                                    c_q_norm_g[i], c_k_norm_g[i], c_rel_bias[i])
        x = x + memory_cross_attention(rms_norm(x, xa_norm_g[layer]), mem_n, xa_w_q[layer],
                                       xa_w_kv[layer], xa_w_out[layer],
                                       xa_q_norm_g[layer], xa_k_norm_g[layer])
        x = x + swiglu(rms_norm(x, f_norm_g[layer]), f_w_gate[layer], f_w_up[layer], f_w_down[layer])
    return x


import jax as _jax
import jax.numpy as _jnp

TWIN_FORMAT = 'train_step'
FWD_PARAMS = ['x', 'mem', 'ab_norm_g', 'ab_w_in', 'ab_w_out', 's5_a_re', 's5_a_im', 's5_log_dt', 's5_b_re', 's5_b_im', 's5_c_re', 's5_c_im', 's5_d', 's5_w_glu', 's5_b_glu', 'gdn_conv_w', 'gdn_a_log', 'gdn_dt_bias', 'gdn_out_norm_g', 'c_norm_g', 'c_w_qkv', 'c_w_out', 'c_q_norm_g', 'c_k_norm_g', 'c_rel_bias', 'mem_norm_g', 'xa_norm_g', 'xa_w_q', 'xa_w_kv', 'xa_w_out', 'xa_q_norm_g', 'xa_k_norm_g', 'f_norm_g', 'f_w_gate', 'f_w_up', 'f_w_down']
TWIN_WEIGHTS = ['ab_norm_g', 'ab_w_in', 'ab_w_out', 's5_a_re', 's5_a_im', 's5_log_dt', 's5_b_re', 's5_b_im', 's5_c_re', 's5_c_im', 's5_d', 's5_w_glu', 's5_b_glu', 'gdn_conv_w', 'gdn_a_log', 'gdn_dt_bias', 'gdn_out_norm_g', 'c_norm_g', 'c_w_qkv', 'c_w_out', 'c_q_norm_g', 'c_k_norm_g', 'c_rel_bias', 'mem_norm_g', 'xa_norm_g', 'xa_w_q', 'xa_w_kv', 'xa_w_out', 'xa_q_norm_g', 'xa_k_norm_g', 'f_norm_g', 'f_w_gate', 'f_w_up', 'f_w_down']
TWIN_DIFF_INPUT = 'x'
TWIN_INPUTS = ['x', 'mem', 'ab_norm_g', 'ab_w_in', 'ab_w_out', 's5_a_re', 's5_a_im', 's5_log_dt', 's5_b_re', 's5_b_im', 's5_c_re', 's5_c_im', 's5_d', 's5_w_glu', 's5_b_glu', 'gdn_conv_w', 'gdn_a_log', 'gdn_dt_bias', 'gdn_out_norm_g', 'c_norm_g', 'c_w_qkv', 'c_w_out', 'c_q_norm_g', 'c_k_norm_g', 'c_rel_bias', 'mem_norm_g', 'xa_norm_g', 'xa_w_q', 'xa_w_kv', 'xa_w_out', 'xa_q_norm_g', 'xa_k_norm_g', 'f_norm_g', 'f_w_gate', 'f_w_up', 'f_w_down', 'loss_target', 'm_ab_norm_g', 'm_ab_w_in', 'm_ab_w_out', 'm_s5_a_re', 'm_s5_a_im', 'm_s5_log_dt', 'm_s5_b_re', 'm_s5_b_im', 'm_s5_c_re', 'm_s5_c_im', 'm_s5_d', 'm_s5_w_glu', 'm_s5_b_glu', 'm_gdn_conv_w', 'm_gdn_a_log', 'm_gdn_dt_bias', 'm_gdn_out_norm_g', 'm_c_norm_g', 'm_c_w_qkv', 'm_c_w_out', 'm_c_q_norm_g', 'm_c_k_norm_g', 'm_c_rel_bias', 'm_mem_norm_g', 'm_xa_norm_g', 'm_xa_w_q', 'm_xa_w_kv', 'm_xa_w_out', 'm_xa_q_norm_g', 'm_xa_k_norm_g', 'm_f_norm_g', 'm_f_w_gate', 'm_f_w_up', 'm_f_w_down', 'v_ab_norm_g', 'v_ab_w_in', 'v_ab_w_out', 'v_s5_a_re', 'v_s5_a_im', 'v_s5_log_dt', 'v_s5_b_re', 'v_s5_b_im', 'v_s5_c_re', 'v_s5_c_im', 'v_s5_d', 'v_s5_w_glu', 'v_s5_b_glu', 'v_gdn_conv_w', 'v_gdn_a_log', 'v_gdn_dt_bias', 'v_gdn_out_norm_g', 'v_c_norm_g', 'v_c_w_qkv', 'v_c_w_out', 'v_c_q_norm_g', 'v_c_k_norm_g', 'v_c_rel_bias', 'v_mem_norm_g', 'v_xa_norm_g', 'v_xa_w_q', 'v_xa_w_kv', 'v_xa_w_out', 'v_xa_q_norm_g', 'v_xa_k_norm_g', 'v_f_norm_g', 'v_f_w_gate', 'v_f_w_up', 'v_f_w_down']
TWIN_OUTPUTS = ['loss', 'grad_x', 'grad_ab_norm_g', 'grad_ab_w_in', 'grad_ab_w_out', 'grad_s5_a_re', 'grad_s5_a_im', 'grad_s5_log_dt', 'grad_s5_b_re', 'grad_s5_b_im', 'grad_s5_c_re', 'grad_s5_c_im', 'grad_s5_d', 'grad_s5_w_glu', 'grad_s5_b_glu', 'grad_gdn_conv_w', 'grad_gdn_a_log', 'grad_gdn_dt_bias', 'grad_gdn_out_norm_g', 'grad_c_norm_g', 'grad_c_w_qkv', 'grad_c_w_out', 'grad_c_q_norm_g', 'grad_c_k_norm_g', 'grad_c_rel_bias', 'grad_mem_norm_g', 'grad_xa_norm_g', 'grad_xa_w_q', 'grad_xa_w_kv', 'grad_xa_w_out', 'grad_xa_q_norm_g', 'grad_xa_k_norm_g', 'grad_f_norm_g', 'grad_f_w_gate', 'grad_f_w_up', 'grad_f_w_down', 'delta_ab_norm_g', 'delta_ab_w_in', 'delta_ab_w_out', 'delta_s5_a_re', 'delta_s5_a_im', 'delta_s5_log_dt', 'delta_s5_b_re', 'delta_s5_b_im', 'delta_s5_c_re', 'delta_s5_c_im', 'delta_s5_d', 'delta_s5_w_glu', 'delta_s5_b_glu', 'delta_gdn_conv_w', 'delta_gdn_a_log', 'delta_gdn_dt_bias', 'delta_gdn_out_norm_g', 'delta_c_norm_g', 'delta_c_w_qkv', 'delta_c_w_out', 'delta_c_q_norm_g', 'delta_c_k_norm_g', 'delta_c_rel_bias', 'delta_mem_norm_g', 'delta_xa_norm_g', 'delta_xa_w_q', 'delta_xa_w_kv', 'delta_xa_w_out', 'delta_xa_q_norm_g', 'delta_xa_k_norm_g', 'delta_f_norm_g', 'delta_f_w_gate', 'delta_f_w_up', 'delta_f_w_down', 'new_m_ab_norm_g', 'new_m_ab_w_in', 'new_m_ab_w_out', 'new_m_s5_a_re', 'new_m_s5_a_im', 'new_m_s5_log_dt', 'new_m_s5_b_re', 'new_m_s5_b_im', 'new_m_s5_c_re', 'new_m_s5_c_im', 'new_m_s5_d', 'new_m_s5_w_glu', 'new_m_s5_b_glu', 'new_m_gdn_conv_w', 'new_m_gdn_a_log', 'new_m_gdn_dt_bias', 'new_m_gdn_out_norm_g', 'new_m_c_norm_g', 'new_m_c_w_qkv', 'new_m_c_w_out', 'new_m_c_q_norm_g', 'new_m_c_k_norm_g', 'new_m_c_rel_bias', 'new_m_mem_norm_g', 'new_m_xa_norm_g', 'new_m_xa_w_q', 'new_m_xa_w_kv', 'new_m_xa_w_out', 'new_m_xa_q_norm_g', 'new_m_xa_k_norm_g', 'new_m_f_norm_g', 'new_m_f_w_gate', 'new_m_f_w_up', 'new_m_f_w_down', 'new_v_ab_norm_g', 'new_v_ab_w_in', 'new_v_ab_w_out', 'new_v_s5_a_re', 'new_v_s5_a_im', 'new_v_s5_log_dt', 'new_v_s5_b_re', 'new_v_s5_b_im', 'new_v_s5_c_re', 'new_v_s5_c_im', 'new_v_s5_d', 'new_v_s5_w_glu', 'new_v_s5_b_glu', 'new_v_gdn_conv_w', 'new_v_gdn_a_log', 'new_v_gdn_dt_bias', 'new_v_gdn_out_norm_g', 'new_v_c_norm_g', 'new_v_c_w_qkv', 'new_v_c_w_out', 'new_v_c_q_norm_g', 'new_v_c_k_norm_g', 'new_v_c_rel_bias', 'new_v_mem_norm_g', 'new_v_xa_norm_g', 'new_v_xa_w_q', 'new_v_xa_w_kv', 'new_v_xa_w_out', 'new_v_xa_q_norm_g', 'new_v_xa_k_norm_g', 'new_v_f_norm_g', 'new_v_f_w_gate', 'new_v_f_w_up', 'new_v_f_w_down']
TWIN_LEAF_KINDS = {'loss': 'loss', 'grad_x': 'grad_x', 'grad_ab_norm_g': 'grad_w', 'grad_ab_w_in': 'grad_w', 'grad_ab_w_out': 'grad_w', 'grad_s5_a_re': 'grad_w', 'grad_s5_a_im': 'grad_w', 'grad_s5_log_dt': 'grad_w', 'grad_s5_b_re': 'grad_w', 'grad_s5_b_im': 'grad_w', 'grad_s5_c_re': 'grad_w', 'grad_s5_c_im': 'grad_w', 'grad_s5_d': 'grad_w', 'grad_s5_w_glu': 'grad_w', 'grad_s5_b_glu': 'grad_w', 'grad_gdn_conv_w': 'grad_w', 'grad_gdn_a_log': 'grad_w', 'grad_gdn_dt_bias': 'grad_w', 'grad_gdn_out_norm_g': 'grad_w', 'grad_c_norm_g': 'grad_w', 'grad_c_w_qkv': 'grad_w', 'grad_c_w_out': 'grad_w', 'grad_c_q_norm_g': 'grad_w', 'grad_c_k_norm_g': 'grad_w', 'grad_c_rel_bias': 'grad_w', 'grad_mem_norm_g': 'grad_w', 'grad_xa_norm_g': 'grad_w', 'grad_xa_w_q': 'grad_w', 'grad_xa_w_kv': 'grad_w', 'grad_xa_w_out': 'grad_w', 'grad_xa_q_norm_g': 'grad_w', 'grad_xa_k_norm_g': 'grad_w', 'grad_f_norm_g': 'grad_w', 'grad_f_w_gate': 'grad_w', 'grad_f_w_up': 'grad_w', 'grad_f_w_down': 'grad_w', 'delta_ab_norm_g': 'delta_w', 'delta_ab_w_in': 'delta_w', 'delta_ab_w_out': 'delta_w', 'delta_s5_a_re': 'delta_w', 'delta_s5_a_im': 'delta_w', 'delta_s5_log_dt': 'delta_w', 'delta_s5_b_re': 'delta_w', 'delta_s5_b_im': 'delta_w', 'delta_s5_c_re': 'delta_w', 'delta_s5_c_im': 'delta_w', 'delta_s5_d': 'delta_w', 'delta_s5_w_glu': 'delta_w', 'delta_s5_b_glu': 'delta_w', 'delta_gdn_conv_w': 'delta_w', 'delta_gdn_a_log': 'delta_w', 'delta_gdn_dt_bias': 'delta_w', 'delta_gdn_out_norm_g': 'delta_w', 'delta_c_norm_g': 'delta_w', 'delta_c_w_qkv': 'delta_w', 'delta_c_w_out': 'delta_w', 'delta_c_q_norm_g': 'delta_w', 'delta_c_k_norm_g': 'delta_w', 'delta_c_rel_bias': 'delta_w', 'delta_mem_norm_g': 'delta_w', 'delta_xa_norm_g': 'delta_w', 'delta_xa_w_q': 'delta_w', 'delta_xa_w_kv': 'delta_w', 'delta_xa_w_out': 'delta_w', 'delta_xa_q_norm_g': 'delta_w', 'delta_xa_k_norm_g': 'delta_w', 'delta_f_norm_g': 'delta_w', 'delta_f_w_gate': 'delta_w', 'delta_f_w_up': 'delta_w', 'delta_f_w_down': 'delta_w', 'new_m_ab_norm_g': 'new_m', 'new_m_ab_w_in': 'new_m', 'new_m_ab_w_out': 'new_m', 'new_m_s5_a_re': 'new_m', 'new_m_s5_a_im': 'new_m', 'new_m_s5_log_dt': 'new_m', 'new_m_s5_b_re': 'new_m', 'new_m_s5_b_im': 'new_m', 'new_m_s5_c_re': 'new_m', 'new_m_s5_c_im': 'new_m', 'new_m_s5_d': 'new_m', 'new_m_s5_w_glu': 'new_m', 'new_m_s5_b_glu': 'new_m', 'new_m_gdn_conv_w': 'new_m', 'new_m_gdn_a_log': 'new_m', 'new_m_gdn_dt_bias': 'new_m', 'new_m_gdn_out_norm_g': 'new_m', 'new_m_c_norm_g': 'new_m', 'new_m_c_w_qkv': 'new_m', 'new_m_c_w_out': 'new_m', 'new_m_c_q_norm_g': 'new_m', 'new_m_c_k_norm_g': 'new_m', 'new_m_c_rel_bias': 'new_m', 'new_m_mem_norm_g': 'new_m', 'new_m_xa_norm_g': 'new_m', 'new_m_xa_w_q': 'new_m', 'new_m_xa_w_kv': 'new_m', 'new_m_xa_w_out': 'new_m', 'new_m_xa_q_norm_g': 'new_m', 'new_m_xa_k_norm_g': 'new_m', 'new_m_f_norm_g': 'new_m', 'new_m_f_w_gate': 'new_m', 'new_m_f_w_up': 'new_m', 'new_m_f_w_down': 'new_m', 'new_v_ab_norm_g': 'new_v', 'new_v_ab_w_in': 'new_v', 'new_v_ab_w_out': 'new_v', 'new_v_s5_a_re': 'new_v', 'new_v_s5_a_im': 'new_v', 'new_v_s5_log_dt': 'new_v', 'new_v_s5_b_re': 'new_v', 'new_v_s5_b_im': 'new_v', 'new_v_s5_c_re': 'new_v', 'new_v_s5_c_im': 'new_v', 'new_v_s5_d': 'new_v', 'new_v_s5_w_glu': 'new_v', 'new_v_s5_b_glu': 'new_v', 'new_v_gdn_conv_w': 'new_v', 'new_v_gdn_a_log': 'new_v', 'new_v_gdn_dt_bias': 'new_v', 'new_v_gdn_out_norm_g': 'new_v', 'new_v_c_norm_g': 'new_v', 'new_v_c_w_qkv': 'new_v', 'new_v_c_w_out': 'new_v', 'new_v_c_q_norm_g': 'new_v', 'new_v_c_k_norm_g': 'new_v', 'new_v_c_rel_bias': 'new_v', 'new_v_mem_norm_g': 'new_v', 'new_v_xa_norm_g': 'new_v', 'new_v_xa_w_q': 'new_v', 'new_v_xa_w_kv': 'new_v', 'new_v_xa_w_out': 'new_v', 'new_v_xa_q_norm_g': 'new_v', 'new_v_xa_k_norm_g': 'new_v', 'new_v_f_norm_g': 'new_v', 'new_v_f_w_gate': 'new_v', 'new_v_f_w_up': 'new_v', 'new_v_f_w_down': 'new_v'}


def _forward(args):
    return _fwd_reference(*[args[k] for k in FWD_PARAMS])


def _output_shape():
    out = _jax.eval_shape(lambda: _forward(_fwd_setup_inputs(0)))
    return out.shape, out.dtype

N_MICROBATCH = 1
ADAM_LR = 0.001
ADAM_B1 = 0.9
ADAM_B2 = 0.999
ADAM_EPS = 1e-08
ADAM_WD = 0.01
ADAM_STEP = 10
PER_EXAMPLE_BATCH_AXIS = {'x': 0, 'mem': 0, 'loss_target': 0}
SHARED_INPUTS = []
_WEIGHT_DTYPES = {'ab_norm_g': _jnp.float32, 'ab_w_in': _jnp.float32, 'ab_w_out': _jnp.float32, 's5_a_re': _jnp.float32, 's5_a_im': _jnp.float32, 's5_log_dt': _jnp.float32, 's5_b_re': _jnp.float32, 's5_b_im': _jnp.float32, 's5_c_re': _jnp.float32, 's5_c_im': _jnp.float32, 's5_d': _jnp.float32, 's5_w_glu': _jnp.float32, 's5_b_glu': _jnp.float32, 'gdn_conv_w': _jnp.float32, 'gdn_a_log': _jnp.float32, 'gdn_dt_bias': _jnp.float32, 'gdn_out_norm_g': _jnp.float32, 'c_norm_g': _jnp.float32, 'c_w_qkv': _jnp.float32, 'c_w_out': _jnp.float32, 'c_q_norm_g': _jnp.float32, 'c_k_norm_g': _jnp.float32, 'c_rel_bias': _jnp.float32, 'mem_norm_g': _jnp.float32, 'xa_norm_g': _jnp.float32, 'xa_w_q': _jnp.float32, 'xa_w_kv': _jnp.float32, 'xa_w_out': _jnp.float32, 'xa_q_norm_g': _jnp.float32, 'xa_k_norm_g': _jnp.float32, 'f_norm_g': _jnp.float32, 'f_w_gate': _jnp.float32, 'f_w_up': _jnp.float32, 'f_w_down': _jnp.float32}
MOMENT_SCALE = {'ab_norm_g': 4.681469e+00, 'ab_w_in': 4.230180e-01, 'ab_w_out': 1.484229e+00, 's5_a_re': 3.015901e-02, 's5_a_im': 2.763851e-02, 's5_log_dt': 4.919969e+00, 's5_b_re': 2.091005e-02, 's5_b_im': 2.448491e-02, 's5_c_re': 3.779086e-02, 's5_c_im': 3.376434e-02, 's5_d': 3.555027e+00, 's5_w_glu': 6.840779e-01, 's5_b_glu': 1.955260e+00, 'gdn_conv_w': 6.260332e-01, 'gdn_a_log': 1.868575e+01, 'gdn_dt_bias': 1.777327e+01, 'gdn_out_norm_g': 2.381232e+01, 'c_norm_g': 1.161038e+00, 'c_w_qkv': 5.428727e-01, 'c_w_out': 8.596002e-01, 'c_q_norm_g': 1.025378e+00, 'c_k_norm_g': 1.026313e+00, 'c_rel_bias': 2.876254e-02, 'mem_norm_g': 1.001263e+00, 'xa_norm_g': 7.446973e-02, 'xa_w_q': 7.408860e-02, 'xa_w_kv': 2.014295e-01, 'xa_w_out': 2.759236e-01, 'xa_q_norm_g': 6.285641e-01, 'xa_k_norm_g': 6.294928e-01, 'f_norm_g': 1.225469e+01, 'f_w_gate': 2.729282e-01, 'f_w_up': 2.327772e-01, 'f_w_down': 3.714890e-01}


def _to_microbatches(a, axis):
    t = _jnp.moveaxis(a, axis, 0)
    t = t.reshape((N_MICROBATCH, t.shape[0] // N_MICROBATCH) + t.shape[1:])
    return _jnp.moveaxis(t, 1, axis + 1)


def setup_inputs(seed: int = 0) -> dict:
    inp = _fwd_setup_inputs(seed)
    key = _jax.random.fold_in(_jax.random.key(seed), 7919)
    shape, _ = _output_shape()
    out = dict(inp)
    out["loss_target"] = _jax.random.normal(_jax.random.fold_in(key, 0), shape, _jnp.float32)
    for i, name in enumerate(TWIN_WEIGHTS):
        w = inp[name].astype(_jnp.float32)
        if MOMENT_SCALE is None:
            s = _jnp.sqrt(_jnp.mean(_jnp.square(w)) + 1e-30)
        else:
            s = MOMENT_SCALE[name]
        km, kv = _jax.random.split(_jax.random.fold_in(key, i + 1))
        out[name] = w
        out["m_" + name] = s * _jax.random.normal(km, w.shape, _jnp.float32)
        out["v_" + name] = (s * s) * _jax.random.uniform(kv, w.shape, _jnp.float32, 0.5, 1.5)
    if N_MICROBATCH > 1:
        for name, axis in PER_EXAMPLE_BATCH_AXIS.items():
            out[name] = _to_microbatches(out[name], axis)
    return {'x': out['x'], 'mem': out['mem'], 'ab_norm_g': out['ab_norm_g'], 'ab_w_in': out['ab_w_in'], 'ab_w_out': out['ab_w_out'], 's5_a_re': out['s5_a_re'], 's5_a_im': out['s5_a_im'], 's5_log_dt': out['s5_log_dt'], 's5_b_re': out['s5_b_re'], 's5_b_im': out['s5_b_im'], 's5_c_re': out['s5_c_re'], 's5_c_im': out['s5_c_im'], 's5_d': out['s5_d'], 's5_w_glu': out['s5_w_glu'], 's5_b_glu': out['s5_b_glu'], 'gdn_conv_w': out['gdn_conv_w'], 'gdn_a_log': out['gdn_a_log'], 'gdn_dt_bias': out['gdn_dt_bias'], 'gdn_out_norm_g': out['gdn_out_norm_g'], 'c_norm_g': out['c_norm_g'], 'c_w_qkv': out['c_w_qkv'], 'c_w_out': out['c_w_out'], 'c_q_norm_g': out['c_q_norm_g'], 'c_k_norm_g': out['c_k_norm_g'], 'c_rel_bias': out['c_rel_bias'], 'mem_norm_g': out['mem_norm_g'], 'xa_norm_g': out['xa_norm_g'], 'xa_w_q': out['xa_w_q'], 'xa_w_kv': out['xa_w_kv'], 'xa_w_out': out['xa_w_out'], 'xa_q_norm_g': out['xa_q_norm_g'], 'xa_k_norm_g': out['xa_k_norm_g'], 'f_norm_g': out['f_norm_g'], 'f_w_gate': out['f_w_gate'], 'f_w_up': out['f_w_up'], 'f_w_down': out['f_w_down'], 'loss_target': out['loss_target'], 'm_ab_norm_g': out['m_ab_norm_g'], 'm_ab_w_in': out['m_ab_w_in'], 'm_ab_w_out': out['m_ab_w_out'], 'm_s5_a_re': out['m_s5_a_re'], 'm_s5_a_im': out['m_s5_a_im'], 'm_s5_log_dt': out['m_s5_log_dt'], 'm_s5_b_re': out['m_s5_b_re'], 'm_s5_b_im': out['m_s5_b_im'], 'm_s5_c_re': out['m_s5_c_re'], 'm_s5_c_im': out['m_s5_c_im'], 'm_s5_d': out['m_s5_d'], 'm_s5_w_glu': out['m_s5_w_glu'], 'm_s5_b_glu': out['m_s5_b_glu'], 'm_gdn_conv_w': out['m_gdn_conv_w'], 'm_gdn_a_log': out['m_gdn_a_log'], 'm_gdn_dt_bias': out['m_gdn_dt_bias'], 'm_gdn_out_norm_g': out['m_gdn_out_norm_g'], 'm_c_norm_g': out['m_c_norm_g'], 'm_c_w_qkv': out['m_c_w_qkv'], 'm_c_w_out': out['m_c_w_out'], 'm_c_q_norm_g': out['m_c_q_norm_g'], 'm_c_k_norm_g': out['m_c_k_norm_g'], 'm_c_rel_bias': out['m_c_rel_bias'], 'm_mem_norm_g': out['m_mem_norm_g'], 'm_xa_norm_g': out['m_xa_norm_g'], 'm_xa_w_q': out['m_xa_w_q'], 'm_xa_w_kv': out['m_xa_w_kv'], 'm_xa_w_out': out['m_xa_w_out'], 'm_xa_q_norm_g': out['m_xa_q_norm_g'], 'm_xa_k_norm_g': out['m_xa_k_norm_g'], 'm_f_norm_g': out['m_f_norm_g'], 'm_f_w_gate': out['m_f_w_gate'], 'm_f_w_up': out['m_f_w_up'], 'm_f_w_down': out['m_f_w_down'], 'v_ab_norm_g': out['v_ab_norm_g'], 'v_ab_w_in': out['v_ab_w_in'], 'v_ab_w_out': out['v_ab_w_out'], 'v_s5_a_re': out['v_s5_a_re'], 'v_s5_a_im': out['v_s5_a_im'], 'v_s5_log_dt': out['v_s5_log_dt'], 'v_s5_b_re': out['v_s5_b_re'], 'v_s5_b_im': out['v_s5_b_im'], 'v_s5_c_re': out['v_s5_c_re'], 'v_s5_c_im': out['v_s5_c_im'], 'v_s5_d': out['v_s5_d'], 'v_s5_w_glu': out['v_s5_w_glu'], 'v_s5_b_glu': out['v_s5_b_glu'], 'v_gdn_conv_w': out['v_gdn_conv_w'], 'v_gdn_a_log': out['v_gdn_a_log'], 'v_gdn_dt_bias': out['v_gdn_dt_bias'], 'v_gdn_out_norm_g': out['v_gdn_out_norm_g'], 'v_c_norm_g': out['v_c_norm_g'], 'v_c_w_qkv': out['v_c_w_qkv'], 'v_c_w_out': out['v_c_w_out'], 'v_c_q_norm_g': out['v_c_q_norm_g'], 'v_c_k_norm_g': out['v_c_k_norm_g'], 'v_c_rel_bias': out['v_c_rel_bias'], 'v_mem_norm_g': out['v_mem_norm_g'], 'v_xa_norm_g': out['v_xa_norm_g'], 'v_xa_w_q': out['v_xa_w_q'], 'v_xa_w_kv': out['v_xa_w_kv'], 'v_xa_w_out': out['v_xa_w_out'], 'v_xa_q_norm_g': out['v_xa_q_norm_g'], 'v_xa_k_norm_g': out['v_xa_k_norm_g'], 'v_f_norm_g': out['v_f_norm_g'], 'v_f_w_gate': out['v_f_w_gate'], 'v_f_w_up': out['v_f_w_up'], 'v_f_w_down': out['v_f_w_down']}


def _loss(weights, diff, rest, loss_target):
    with _jax.named_scope("forward"):
        args = {**rest, TWIN_DIFF_INPUT: diff, **{k: w.astype(_WEIGHT_DTYPES[k]) for k, w in weights.items()}}
        y = _forward(args)
    with _jax.named_scope("loss_head"):
        err = _jnp.square(y.astype(_jnp.float32) - loss_target)
        return 0.5 * _jnp.sum(_jnp.mean(err, axis=-1)) if err.ndim else 0.5 * err


def _adamw(w, g, m, v):
    m = ADAM_B1 * m + (1.0 - ADAM_B1) * g
    v = ADAM_B2 * v + (1.0 - ADAM_B2) * _jnp.square(g)
    m_hat = m / (1.0 - ADAM_B1 ** ADAM_STEP)
    v_hat = v / (1.0 - ADAM_B2 ** ADAM_STEP)
    delta = -ADAM_LR * (m_hat / (_jnp.sqrt(v_hat) + ADAM_EPS) + ADAM_WD * w)
    return delta, m, v


def reference(x, mem, ab_norm_g, ab_w_in, ab_w_out, s5_a_re, s5_a_im, s5_log_dt, s5_b_re, s5_b_im, s5_c_re, s5_c_im, s5_d, s5_w_glu, s5_b_glu, gdn_conv_w, gdn_a_log, gdn_dt_bias, gdn_out_norm_g, c_norm_g, c_w_qkv, c_w_out, c_q_norm_g, c_k_norm_g, c_rel_bias, mem_norm_g, xa_norm_g, xa_w_q, xa_w_kv, xa_w_out, xa_q_norm_g, xa_k_norm_g, f_norm_g, f_w_gate, f_w_up, f_w_down, loss_target, m_ab_norm_g, m_ab_w_in, m_ab_w_out, m_s5_a_re, m_s5_a_im, m_s5_log_dt, m_s5_b_re, m_s5_b_im, m_s5_c_re, m_s5_c_im, m_s5_d, m_s5_w_glu, m_s5_b_glu, m_gdn_conv_w, m_gdn_a_log, m_gdn_dt_bias, m_gdn_out_norm_g, m_c_norm_g, m_c_w_qkv, m_c_w_out, m_c_q_norm_g, m_c_k_norm_g, m_c_rel_bias, m_mem_norm_g, m_xa_norm_g, m_xa_w_q, m_xa_w_kv, m_xa_w_out, m_xa_q_norm_g, m_xa_k_norm_g, m_f_norm_g, m_f_w_gate, m_f_w_up, m_f_w_down, v_ab_norm_g, v_ab_w_in, v_ab_w_out, v_s5_a_re, v_s5_a_im, v_s5_log_dt, v_s5_b_re, v_s5_b_im, v_s5_c_re, v_s5_c_im, v_s5_d, v_s5_w_glu, v_s5_b_glu, v_gdn_conv_w, v_gdn_a_log, v_gdn_dt_bias, v_gdn_out_norm_g, v_c_norm_g, v_c_w_qkv, v_c_w_out, v_c_q_norm_g, v_c_k_norm_g, v_c_rel_bias, v_mem_norm_g, v_xa_norm_g, v_xa_w_q, v_xa_w_kv, v_xa_w_out, v_xa_q_norm_g, v_xa_k_norm_g, v_f_norm_g, v_f_w_gate, v_f_w_up, v_f_w_down):
    given = dict(x=x, mem=mem, ab_norm_g=ab_norm_g, ab_w_in=ab_w_in, ab_w_out=ab_w_out, s5_a_re=s5_a_re, s5_a_im=s5_a_im, s5_log_dt=s5_log_dt, s5_b_re=s5_b_re, s5_b_im=s5_b_im, s5_c_re=s5_c_re, s5_c_im=s5_c_im, s5_d=s5_d, s5_w_glu=s5_w_glu, s5_b_glu=s5_b_glu, gdn_conv_w=gdn_conv_w, gdn_a_log=gdn_a_log, gdn_dt_bias=gdn_dt_bias, gdn_out_norm_g=gdn_out_norm_g, c_norm_g=c_norm_g, c_w_qkv=c_w_qkv, c_w_out=c_w_out, c_q_norm_g=c_q_norm_g, c_k_norm_g=c_k_norm_g, c_rel_bias=c_rel_bias, mem_norm_g=mem_norm_g, xa_norm_g=xa_norm_g, xa_w_q=xa_w_q, xa_w_kv=xa_w_kv, xa_w_out=xa_w_out, xa_q_norm_g=xa_q_norm_g, xa_k_norm_g=xa_k_norm_g, f_norm_g=f_norm_g, f_w_gate=f_w_gate, f_w_up=f_w_up, f_w_down=f_w_down, loss_target=loss_target, m_ab_norm_g=m_ab_norm_g, m_ab_w_in=m_ab_w_in, m_ab_w_out=m_ab_w_out, m_s5_a_re=m_s5_a_re, m_s5_a_im=m_s5_a_im, m_s5_log_dt=m_s5_log_dt, m_s5_b_re=m_s5_b_re, m_s5_b_im=m_s5_b_im, m_s5_c_re=m_s5_c_re, m_s5_c_im=m_s5_c_im, m_s5_d=m_s5_d, m_s5_w_glu=m_s5_w_glu, m_s5_b_glu=m_s5_b_glu, m_gdn_conv_w=m_gdn_conv_w, m_gdn_a_log=m_gdn_a_log, m_gdn_dt_bias=m_gdn_dt_bias, m_gdn_out_norm_g=m_gdn_out_norm_g, m_c_norm_g=m_c_norm_g, m_c_w_qkv=m_c_w_qkv, m_c_w_out=m_c_w_out, m_c_q_norm_g=m_c_q_norm_g, m_c_k_norm_g=m_c_k_norm_g, m_c_rel_bias=m_c_rel_bias, m_mem_norm_g=m_mem_norm_g, m_xa_norm_g=m_xa_norm_g, m_xa_w_q=m_xa_w_q, m_xa_w_kv=m_xa_w_kv, m_xa_w_out=m_xa_w_out, m_xa_q_norm_g=m_xa_q_norm_g, m_xa_k_norm_g=m_xa_k_norm_g, m_f_norm_g=m_f_norm_g, m_f_w_gate=m_f_w_gate, m_f_w_up=m_f_w_up, m_f_w_down=m_f_w_down, v_ab_norm_g=v_ab_norm_g, v_ab_w_in=v_ab_w_in, v_ab_w_out=v_ab_w_out, v_s5_a_re=v_s5_a_re, v_s5_a_im=v_s5_a_im, v_s5_log_dt=v_s5_log_dt, v_s5_b_re=v_s5_b_re, v_s5_b_im=v_s5_b_im, v_s5_c_re=v_s5_c_re, v_s5_c_im=v_s5_c_im, v_s5_d=v_s5_d, v_s5_w_glu=v_s5_w_glu, v_s5_b_glu=v_s5_b_glu, v_gdn_conv_w=v_gdn_conv_w, v_gdn_a_log=v_gdn_a_log, v_gdn_dt_bias=v_gdn_dt_bias, v_gdn_out_norm_g=v_gdn_out_norm_g, v_c_norm_g=v_c_norm_g, v_c_w_qkv=v_c_w_qkv, v_c_w_out=v_c_w_out, v_c_q_norm_g=v_c_q_norm_g, v_c_k_norm_g=v_c_k_norm_g, v_c_rel_bias=v_c_rel_bias, v_mem_norm_g=v_mem_norm_g, v_xa_norm_g=v_xa_norm_g, v_xa_w_q=v_xa_w_q, v_xa_w_kv=v_xa_w_kv, v_xa_w_out=v_xa_w_out, v_xa_q_norm_g=v_xa_q_norm_g, v_xa_k_norm_g=v_xa_k_norm_g, v_f_norm_g=v_f_norm_g, v_f_w_gate=v_f_w_gate, v_f_w_up=v_f_w_up, v_f_w_down=v_f_w_down)
    weights = {n: given[n] for n in TWIN_WEIGHTS}
    shared = {n: given[n] for n in SHARED_INPUTS}
    per_example = {n: given[n] for n in ['x', 'mem']}
    grad_fn = _jax.value_and_grad(_loss, argnums=(0, 1))

    def one_microbatch(ex, loss_target):
        ex = dict(ex)
        diff = ex.pop(TWIN_DIFF_INPUT)
        return grad_fn(weights, diff, {**shared, **ex}, loss_target)

    if N_MICROBATCH == 1:
        loss, (grad_w, grad_x) = one_microbatch(per_example, given["loss_target"])
    else:
        def body(carry, xs):
            loss_sum, grad_sum = carry
            l_k, (gw_k, gx_k) = one_microbatch(xs[0], xs[1])
            with _jax.named_scope("update"):
                return (loss_sum + l_k, _jax.tree.map(_jnp.add, grad_sum, gw_k)), gx_k

        init = (_jnp.zeros((), _jnp.float32), _jax.tree.map(_jnp.zeros_like, weights))
        (loss, grad_w), grad_x = _jax.lax.scan(body, init, (per_example, given["loss_target"]))
    with _jax.named_scope("update"):
        delta_w, new_m, new_v = {}, {}, {}
        for n in TWIN_WEIGHTS:
            delta_w[n], new_m[n], new_v[n] = _adamw(weights[n], grad_w[n], given["m_" + n], given["v_" + n])
    return (loss, grad_x, *[grad_w[n] for n in TWIN_WEIGHTS], *[delta_w[n] for n in TWIN_WEIGHTS],
            *[new_m[n] for n in TWIN_WEIGHTS], *[new_v[n] for n in TWIN_WEIGHTS])
```

```python
import functools
import math

import numpy as np
import jax
import jax.numpy as jnp
from jax import lax
from jax.experimental import pallas as pl
from jax.experimental.pallas import tpu as pltpu

F32 = jnp.float32
BF16 = jnp.bfloat16
MESH = pl.DeviceIdType.MESH

D = 1024
CHUNK = 64
N_MEM = 256
EPS = 1e-6
S5_W = 512
S5_G = 32
S5_C = 16
S5_P = 64
S5_GB = 4
GDN_H = 4
GDN_D = 128
CA_H = 16
CA_D = 64
CA_LEFT = 8
CA_BAND = (CA_LEFT + 1) * CHUNK
CA_PAD = CA_LEFT * CHUNK
MAX_REL = 128
XA_H = 4
XA_D = 256
FFN = 2816
DEPTH = 4
N_CHIPS = 4
N_DEV = 8
LR, B1, B2, AEPS, WD, STEP = 0.001, 0.9, 0.999, 1e-08, 0.01, 10

VMEM_LIMIT = 56 * 1024 * 1024
ROW_TILE = 256
HI = lax.Precision.HIGHEST


def _cp(*sem):
    return pltpu.CompilerParams(dimension_semantics=sem, vmem_limit_bytes=VMEM_LIMIT)


def _dg(a, b, ca, cb):
    return lax.dot_general(a.astype(BF16), b.astype(BF16), (((ca,), (cb,)), ((), ())),
                           preferred_element_type=F32)


@jax.custom_vjp
def mm(a, b):
    return _dg(a, b, 1, 0)


@jax.custom_vjp
def mm_nt(a, b):
    return _dg(a, b, 1, 1)


@jax.custom_vjp
def mm_tn(a, b):
    return _dg(a, b, 0, 0)


mm.defvjp(lambda a, b: (mm(a, b), (a, b)), lambda r, g: (mm_nt(g, r[1]), mm_tn(r[0], g)))
mm_nt.defvjp(lambda a, b: (mm_nt(a, b), (a, b)), lambda r, g: (mm(g, r[1]), mm_tn(g, r[0])))
mm_tn.defvjp(lambda a, b: (mm_tn(a, b), (a, b)), lambda r, g: (mm_nt(r[1], g), mm(r[0], g)))


def _bdg(a, b, ca, cb):
    return lax.dot_general(a.astype(BF16), b.astype(BF16), (((ca,), (cb,)), ((0,), (0,))),
                           preferred_element_type=F32)


@jax.custom_vjp
def bmm(a, b):
    return _bdg(a, b, 2, 1)


@jax.custom_vjp
def bmm_nt(a, b):
    return _bdg(a, b, 2, 2)


@jax.custom_vjp
def bmm_tn(a, b):
    return _bdg(a, b, 1, 1)


bmm.defvjp(lambda a, b: (bmm(a, b), (a, b)), lambda r, g: (bmm_nt(g, r[1]), bmm_tn(r[0], g)))
bmm_nt.defvjp(lambda a, b: (bmm_nt(a, b), (a, b)), lambda r, g: (bmm(g, r[1]), bmm_tn(g, r[0])))
bmm_tn.defvjp(lambda a, b: (bmm_tn(a, b), (a, b)), lambda r, g: (bmm_nt(r[1], g), bmm(r[0], g)))


def mmf(a, b):
    return jnp.dot(a, b, precision=HI, preferred_element_type=F32)


def _rms(x, g):
    return x * lax.rsqrt(jnp.mean(x * x, axis=-1, keepdims=True) + EPS) * g


def _softmax(s):
    e = jnp.exp(s - lax.stop_gradient(jnp.max(s, axis=-1, keepdims=True)))
    return e / jnp.sum(e, axis=-1, keepdims=True)


def _softplus(x):
    return jnp.maximum(x, 0.0) + jnp.log(1.0 + jnp.exp(-jnp.abs(x)))


def _tile(n, cap, align):
    if n <= cap:
        return n
    best = None
    for d in range(align, cap + 1, align):
        if n % d == 0:
            best = d
    assert best is not None, (n, cap, align)
    return best


def matmul(mode, a, b, name, out_dtype=F32, add=None):
    if mode == "nn":
        (M, K), (K2, N) = a.shape, b.shape
    elif mode == "nt":
        (M, K), (N, K2) = a.shape, b.shape
    else:
        (K, M), (K2, N) = a.shape, b.shape
    assert K == K2, (mode, a.shape, b.shape)
    tm, tn, tk = _tile(M, 512, 128), _tile(N, 512, 128), _tile(K, 2048, 128)
    nk = K // tk
    if mode == "nn":
        a_spec = pl.BlockSpec((tm, tk), lambda i, j, k: (i, k))
        b_spec = pl.BlockSpec((tk, tn), lambda i, j, k: (k, j))
        dn = (((1,), (0,)), ((), ()))
    elif mode == "nt":
        a_spec = pl.BlockSpec((tm, tk), lambda i, j, k: (i, k))
        b_spec = pl.BlockSpec((tn, tk), lambda i, j, k: (j, k))
        dn = (((1,), (1,)), ((), ()))
    else:
        a_spec = pl.BlockSpec((tk, tm), lambda i, j, k: (k, i))
        b_spec = pl.BlockSpec((tk, tn), lambda i, j, k: (k, j))
        dn = (((0,), (0,)), ((), ()))
    o_spec = pl.BlockSpec((tm, tn), lambda i, j, k: (i, j))
    has_add = add is not None

    def body(*refs):
        a_ref, b_ref = refs[0], refs[1]
        add_ref = refs[2] if has_add else None
        o_ref = refs[3] if has_add else refs[2]
        acc_ref = refs[-1]
        p = lax.dot_general(a_ref[...].astype(BF16), b_ref[...].astype(BF16), dn,
                            preferred_element_type=F32)

        def finish(total):
            if has_add:
                total = total + add_ref[...]
            o_ref[...] = total.astype(o_ref.dtype)

        if nk == 1:
            finish(p)
        else:
            k = pl.program_id(2)

            @pl.when(k == 0)
            def _():
                acc_ref[...] = p

            @pl.when(k > 0)
            def _():
                acc_ref[...] += p

            @pl.when(k == nk - 1)
            def _():
                finish(acc_ref[...])

    ins = [a, b] + ([add] if has_add else [])
    specs = [a_spec, b_spec] + ([o_spec] if has_add else [])
    return pl.pallas_call(
        body, name=name, grid=(M // tm, N // tn, nk), in_specs=specs, out_specs=o_spec,
        out_shape=jax.ShapeDtypeStruct((M, N), out_dtype),
        scratch_shapes=[pltpu.VMEM((tm, tn), F32)],
        compiler_params=_cp("parallel", "parallel", "arbitrary"),
    )(*ins)


def _row_spec(arr, tile):
    if isinstance(arr, tuple):
        a, w, cb = arr
        return a, pl.BlockSpec((tile, w), lambda i, cb=cb: (i, cb)), (tile, w)
    if arr.ndim == 3:
        d0, _, d2 = arr.shape
        return arr, pl.BlockSpec((d0, tile, d2), lambda i: (0, i, 0)), (d0, tile, d2)
    return arr, pl.BlockSpec((tile, arr.shape[1]), lambda i: (i, 0)), (tile, arr.shape[1])


def _full_spec(arr):
    nd = arr.ndim
    return pl.BlockSpec(arr.shape, lambda i, nd=nd: (0,) * nd)


def _n_rows(arr):
    a = arr[0] if isinstance(arr, tuple) else arr
    return a.shape[1] if a.ndim == 3 else a.shape[0]


def _row_out_shape(shape_tail, n, dtype):
    if isinstance(shape_tail, tuple):
        d0, d2 = shape_tail
        return (jax.ShapeDtypeStruct((d0, n, d2), dtype),
                lambda tile: pl.BlockSpec((d0, tile, d2), lambda i: (0, i, 0)))
    return (jax.ShapeDtypeStruct((n, shape_tail), dtype),
            lambda tile: pl.BlockSpec((tile, shape_tail), lambda i: (i, 0)))


def _f32(v):
    return v.astype(F32) if v.dtype == BF16 else v


def row_fwd(name, fn, rows, fulls, outs, tile=ROW_TILE):
    n = _n_rows(rows[0])
    tile = min(tile, n)
    rs = [_row_spec(r, tile) for r in rows]
    os_ = [_row_out_shape(w, n, dt) for w, dt in outs]
    nr, nf = len(rows), len(fulls)

    def body(*refs):
        vals = [_f32(r[...]) for r in refs[:nr + nf]]
        res = fn(*vals)
        for r, v in zip(refs[nr + nf:], res):
            r[...] = v.astype(r.dtype)

    out = pl.pallas_call(
        body, name=name, grid=(n // tile,),
        in_specs=[s for _, s, _ in rs] + [_full_spec(f) for f in fulls],
        out_specs=[mk(tile) for _, mk in os_], out_shape=[sh for sh, _ in os_],
        compiler_params=_cp("parallel"),
    )(*[a for a, _, _ in rs], *fulls)
    return out


def row_bwd(name, fn, rows, fulls, cts, want_rows, want_fulls, row_dtypes=None, tile=ROW_TILE):
    n = _n_rows(rows[0])
    tile = min(tile, n)
    rs = [_row_spec(r, tile) for r in rows]
    cs = [_row_spec(c, tile) for c in cts]
    nr, nf, nc = len(rows), len(fulls), len(cts)
    row_dtypes = row_dtypes or [F32] * len(want_rows)
    out_shapes, out_specs = [], []
    for k, idx in enumerate(want_rows):
        a, _, blk = rs[idx]
        if len(blk) == 3:
            sh, mk = _row_out_shape((blk[0], blk[2]), n, row_dtypes[k])
        else:
            sh, mk = _row_out_shape(blk[1], n, row_dtypes[k])
        out_shapes.append(sh)
        out_specs.append(mk(tile))
    for idx in want_fulls:
        out_shapes.append(jax.ShapeDtypeStruct(fulls[idx].shape, F32))
        out_specs.append(_full_spec(fulls[idx]))
    n_wr = len(want_rows)

    def body(*refs):
        i = pl.program_id(0)
        vals = [_f32(r[...]) for r in refs[:nr + nf]]
        ct_vals = [_f32(r[...]) for r in refs[nr + nf:nr + nf + nc]]
        outs = refs[nr + nf + nc:]
        _, vjp = jax.vjp(fn, *vals)
        grads = vjp(tuple(ct_vals))
        for k, idx in enumerate(want_rows):
            outs[k][...] = grads[idx].astype(outs[k].dtype)
        for k, idx in enumerate(want_fulls):
            o = outs[n_wr + k]
            g = grads[nr + idx]

            @pl.when(i == 0)
            def _(o=o, g=g):
                o[...] = g

            @pl.when(i > 0)
            def _(o=o, g=g):
                o[...] += g

    out = pl.pallas_call(
        body, name=name, grid=(n // tile,),
        in_specs=[s for _, s, _ in rs] + [_full_spec(f) for f in fulls] + [s for _, s, _ in cs],
        out_specs=out_specs, out_shape=out_shapes,
        compiler_params=_cp("arbitrary"),
    )(*[a for a, _, _ in rs], *fulls, *[a for a, _, _ in cs])
    return out


def whole(name, fn, args, outs):
    def body(*refs):
        res = fn(*[r[...] for r in refs[:len(args)]])
        for r, v in zip(refs[len(args):], res):
            r[...] = v.astype(r.dtype)

    return pl.pallas_call(
        body, name=name, out_shape=[jax.ShapeDtypeStruct(s, d) for s, d in outs],
        compiler_params=pltpu.CompilerParams(vmem_limit_bytes=VMEM_LIMIT),
    )(*args)


def f_norm(x, g):
    return (_rms(x, g),)


def f_norm_res(x, g):
    return _rms(x, g), x


def f_swiglu(gu):
    g, u = gu[:, :FFN], gu[:, FFN:]
    return (g * jax.nn.sigmoid(g) * u,)


def f_glu(y, w, b):
    h = jax.nn.gelu(y)
    return (h * jax.nn.sigmoid(mm(h, w) + b),)


def f_xattn(q, kv, qg, kg):
    outs = []
    for h in range(XA_H):
        sl = slice(h * XA_D, (h + 1) * XA_D)
        qn = _rms(q[:, sl], qg)
        kn = _rms(kv[:, sl], kg)
        vh = kv[:, D + h * XA_D:D + (h + 1) * XA_D]
        p = _softmax(mm_nt(qn, kn) * (XA_D ** -0.5))
        outs.append(mm(p, vh))
    return (jnp.concatenate(outs, axis=-1),)


def f_headnorm(t, g):
    return (_rms(t, g),)


def f_adam(w, g, m, v):
    m2 = B1 * m + (1.0 - B1) * g
    v2 = B2 * v + (1.0 - B2) * (g * g)
    m_hat = m2 / (1.0 - B1 ** STEP)
    v_hat = v2 / (1.0 - B2 ** STEP)
    delta = -LR * (m_hat / (jnp.sqrt(v_hat) + AEPS) + WD * w)
    return delta, m2, v2


def _s5_chunk(u, sre, sim, are, aim, ldt, b_re, b_im, c_re, c_im, dv):
    T = u.shape[0]
    dt = jnp.exp(ldt)
    ar, ai = are * dt, aim * dt
    t = lax.broadcasted_iota(jnp.int32, (T, 1), 0).astype(F32)
    row = lax.broadcasted_iota(jnp.int32, (T, T), 0)
    col = lax.broadcasted_iota(jnp.int32, (T, T), 1)
    tri = (row >= col).astype(F32)
    mag, inv = jnp.exp(t * ar), jnp.exp(-t * ar)
    cs, sn = jnp.cos(t * ai), jnp.sin(t * ai)
    e_re, e_im = mag * cs, mag * sn
    n_re, n_im = inv * cs, -inv * sn
    l_re, l_im = jnp.exp(ar) * jnp.cos(ai), jnp.exp(ar) * jnp.sin(ai)
    den = are * are + aim * aim
    k_re = ((l_re - 1.0) * are + l_im * aim) / den
    k_im = (l_im * are - (l_re - 1.0) * aim) / den
    x_re, x_im = mm(u, b_re), mm(u, b_im)
    bu_re = k_re * x_re - k_im * x_im
    bu_im = k_re * x_im + k_im * x_re
    v_re = bu_re * n_re - bu_im * n_im
    v_im = bu_re * n_im + bu_im * n_re
    p_re = l_re * sre - l_im * sim
    p_im = l_re * sim + l_im * sre
    w_re = mmf(tri, v_re) + p_re
    w_im = mmf(tri, v_im) + p_im
    s_re = e_re * w_re - e_im * w_im
    s_im = e_re * w_im + e_im * w_re
    y = mm(s_re, c_re) - mm(s_im, c_im) + dv * u
    tl = float(T - 1)
    m_re, m_im = jnp.exp(tl * ar) * jnp.cos(tl * ai), jnp.exp(tl * ar) * jnp.sin(tl * ai)
    z_re = jnp.sum(v_re, axis=0, keepdims=True) + p_re
    z_im = jnp.sum(v_im, axis=0, keepdims=True) + p_im
    return y, m_re * z_re - m_im * z_im, m_re * z_im + m_im * z_re


def _s5_specs(nc, rev):
    T = CHUNK

    def ci(c):
        return nc - 1 - c if rev else c

    u_spec = pl.BlockSpec((T, 128), lambda g, c: (ci(c), g))
    p_spec = pl.BlockSpec((1, 512), lambda g, c: (0, g))
    b_spec = pl.BlockSpec((None, 128, 512), lambda g, c: (g, 0, 0))
    c_spec = pl.BlockSpec((None, 512, 128), lambda g, c: (g, 0, 0))
    d_spec = pl.BlockSpec((1, 128), lambda g, c: (0, g))
    st_spec = pl.BlockSpec((None, 2, 512), lambda g, c: (ci(c), 0, g))
    return u_spec, p_spec, b_spec, c_spec, d_spec, st_spec


def s5_fwd(pm, are, aim, ldt, b_re, b_im, c_re, c_im, dv):
    S = pm.shape[0]
    nc = S // CHUNK
    u_spec, p_spec, b_spec, c_spec, d_spec, st_spec = _s5_specs(nc, False)

    def body(u_ref, are_ref, aim_ref, ldt_ref, bre_ref, bim_ref, cre_ref, cim_ref, dv_ref,
             y_ref, st_ref, state):
        c = pl.program_id(1)

        @pl.when(c == 0)
        def _():
            state[...] = jnp.zeros_like(state)

        st_ref[...] = state[...]
        y, e_re, e_im = _s5_chunk(u_ref[...], state[0:1, :], state[1:2, :], are_ref[...], aim_ref[...],
                                  ldt_ref[...], bre_ref[...], bim_ref[...], cre_ref[...], cim_ref[...],
                                  dv_ref[...])
        y_ref[...] = y
        state[0:1, :] = e_re
        state[1:2, :] = e_im

    return pl.pallas_call(
        body, name="s5_fwd", grid=(S5_GB, nc),
        in_specs=[u_spec, p_spec, p_spec, p_spec, b_spec, b_spec, c_spec, c_spec, d_spec],
        out_specs=[u_spec, st_spec],
        out_shape=[jax.ShapeDtypeStruct((S, S5_W), F32), jax.ShapeDtypeStruct((nc, 2, S5_G * S5_P), F32)],
        scratch_shapes=[pltpu.VMEM((2, 512), F32)],
        compiler_params=_cp("parallel", "arbitrary"),
    )(pm, are, aim, ldt, b_re, b_im, c_re, c_im, dv)


def s5_bwd(pm, st, dy, are, aim, ldt, b_re, b_im, c_re, c_im, dv):
    S = pm.shape[0]
    nc = S // CHUNK
    u_spec, p_spec, b_spec, c_spec, d_spec, st_spec = _s5_specs(nc, True)

    def body(u_ref, st_ref, dy_ref, are_ref, aim_ref, ldt_ref, bre_ref, bim_ref, cre_ref, cim_ref, dv_ref,
             du_ref, dare_ref, daim_ref, dldt_ref, dbre_ref, dbim_ref, dcre_ref, dcim_ref, ddv_ref, dstate):
        c = pl.program_id(1)

        @pl.when(c == 0)
        def _():
            dstate[...] = jnp.zeros_like(dstate)

        args = (u_ref[...], st_ref[0:1, :], st_ref[1:2, :], are_ref[...], aim_ref[...], ldt_ref[...],
                bre_ref[...], bim_ref[...], cre_ref[...], cim_ref[...], dv_ref[...])
        _, vjp = jax.vjp(_s5_chunk, *args)
        g = vjp((dy_ref[...], dstate[0:1, :], dstate[1:2, :]))
        du_ref[...] = g[0]
        dstate[0:1, :] = g[1]
        dstate[1:2, :] = g[2]
        accs = (dare_ref, daim_ref, dldt_ref, dbre_ref, dbim_ref, dcre_ref, dcim_ref, ddv_ref)
        for o, gv in zip(accs, g[3:]):
            @pl.when(c == 0)
            def _(o=o, gv=gv):
                o[...] = gv

            @pl.when(c > 0)
            def _(o=o, gv=gv):
                o[...] += gv

    n_state = S5_G * S5_P
    return pl.pallas_call(
        body, name="s5_bwd", grid=(S5_GB, nc),
        in_specs=[u_spec, st_spec, u_spec, p_spec, p_spec, p_spec, b_spec, b_spec, c_spec, c_spec, d_spec],
        out_specs=[u_spec, p_spec, p_spec, p_spec, b_spec, b_spec, c_spec, c_spec, d_spec],
        out_shape=[jax.ShapeDtypeStruct((S, S5_W), F32)] + [jax.ShapeDtypeStruct((1, n_state), F32)] * 3
        + [jax.ShapeDtypeStruct((S5_GB, 128, 512), F32)] * 2 + [jax.ShapeDtypeStruct((S5_GB, 512, 128), F32)] * 2
        + [jax.ShapeDtypeStruct((1, S5_W), F32)],
        scratch_shapes=[pltpu.VMEM((2, 512), F32)],
        compiler_params=_cp("parallel", "arbitrary"),
    )(pm, st, dy, are, aim, ldt, b_re, b_im, c_re, c_im, dv)


def conv_fwd(pm, w):
    S = pm.shape[0]

    def body(x_ref, w_ref, o_ref, pad):
        x = x_ref[...]
        pad[0:8, :] = jnp.zeros((8, 128), F32)
        pad[8:, :] = x
        y = (w_ref[3:4, :] * x + w_ref[2:3, :] * pad[7:7 + S, :] + w_ref[1:2, :] * pad[6:6 + S, :]
             + w_ref[0:1, :] * pad[5:5 + S, :])
        o_ref[...] = y * jax.nn.sigmoid(y)

    return pl.pallas_call(
        body, name="conv_fwd", grid=(12,),
        in_specs=[pl.BlockSpec((S, 128), lambda j: (0, 4 + j)), pl.BlockSpec((4, 128), lambda j: (0, j))],
        out_specs=pl.BlockSpec((S, 128), lambda j: (0, j)),
        out_shape=jax.ShapeDtypeStruct((S, 1536), F32),
        scratch_shapes=[pltpu.VMEM((S + 8, 128), F32)],
        compiler_params=_cp("parallel"),
    )(pm, w)


def conv_bwd(pm, w, dout):
    S = pm.shape[0]

    def body(x_ref, w_ref, do_ref, dx_ref, dw_ref, pad, dpad):
        x = x_ref[...]
        pad[0:8, :] = jnp.zeros((8, 128), F32)
        pad[8:, :] = x
        xs = [pad[5:5 + S, :], pad[6:6 + S, :], pad[7:7 + S, :], x]
        y = w_ref[0:1, :] * xs[0] + w_ref[1:2, :] * xs[1] + w_ref[2:3, :] * xs[2] + w_ref[3:4, :] * xs[3]
        sg = jax.nn.sigmoid(y)
        dy = do_ref[...] * (sg + y * sg * (1.0 - sg))
        dpad[0:S, :] = dy
        dpad[S:, :] = jnp.zeros((8, 128), F32)
        dx_ref[...] = (w_ref[3:4, :] * dy + w_ref[2:3, :] * dpad[1:1 + S, :] + w_ref[1:2, :] * dpad[2:2 + S, :]
                       + w_ref[0:1, :] * dpad[3:3 + S, :])
        for i in range(4):
            dw_ref[i:i + 1, :] = jnp.sum(dy * xs[i], axis=0, keepdims=True)

    return pl.pallas_call(
        body, name="conv_bwd", grid=(12,),
        in_specs=[pl.BlockSpec((S, 128), lambda j: (0, 4 + j)), pl.BlockSpec((4, 128), lambda j: (0, j)),
                  pl.BlockSpec((S, 128), lambda j: (0, j))],
        out_specs=[pl.BlockSpec((S, 128), lambda j: (0, j)), pl.BlockSpec((4, 128), lambda j: (0, j))],
        out_shape=[jax.ShapeDtypeStruct((S, 1536), F32), jax.ShapeDtypeStruct((4, 1536), F32)],
        scratch_shapes=[pltpu.VMEM((S + 8, 128), F32), pltpu.VMEM((S + 8, 128), F32)],
        compiler_params=_cp("parallel"),
    )(pm, w, dout)


def _gdn_chunk(q, k, v, gate, al, bl, alog, dtb, og, state):
    C = CHUNK
    r = lax.broadcasted_iota(jnp.int32, (C, C), 0)
    c = lax.broadcasted_iota(jnp.int32, (C, C), 1)
    eye = (r == c).astype(F32)
    strict, causal, upper = r > c, r >= c, r <= c
    qn = q * lax.rsqrt(jnp.sum(q * q, axis=-1, keepdims=True) + EPS) * (GDN_D ** -0.5)
    kn = k * lax.rsqrt(jnp.sum(k * k, axis=-1, keepdims=True) + EPS)
    beta = jax.nn.sigmoid(bl)
    g = -jnp.exp(alog) * _softplus(al + dtb)
    g_row = jnp.sum(eye * g, axis=0, keepdims=True)
    gc_col = jnp.sum(jnp.where(causal, g_row, 0.0), axis=1, keepdims=True)
    gc_row = jnp.sum(jnp.where(upper, g, 0.0), axis=0, keepdims=True)
    gtot = jnp.sum(g, axis=0, keepdims=True)
    gamma = jnp.exp(gc_col)
    diff = gc_col - gc_row
    d_strict = jnp.where(strict, jnp.exp(jnp.where(strict, diff, 0.0)), 0.0)
    d_causal = jnp.where(causal, jnp.exp(jnp.where(causal, diff, 0.0)), 0.0)
    a = beta * mm_nt(kn, kn) * d_strict
    p = -a
    x = eye + p
    for _ in range(5):
        p = mmf(p, p)
        x = x + mmf(x, p)
    u_new = mmf(x, beta * v)
    w_k = mmf(x, (beta * gamma) * kn)
    qk = mm_nt(qn, kn) * d_causal
    q_g = qn * gamma
    k_tail = kn * jnp.exp(gtot - gc_col)
    w = u_new - mm(w_k, state)
    o = mm(q_g, state) + mm(qk, w)
    new_state = jnp.exp(gtot) * state + mm_tn(k_tail, w)
    out = _rms(o, og) * (gate * jax.nn.sigmoid(gate))
    return out, new_state


def _gdn_specs(nc, rev):
    def ci(c):
        return nc - 1 - c if rev else c

    def blk(off):
        return pl.BlockSpec((CHUNK, 128), lambda h, c: (ci(c), off + h))

    col = lambda off: pl.BlockSpec((None, CHUNK, 1), lambda h, c: (off + h, ci(c), 0))
    sc = pl.BlockSpec((None, 1, 1), lambda h, c: (h, 0, 0))
    og = pl.BlockSpec((1, 128), lambda h, c: (0, 0))
    st = pl.BlockSpec((None, None, 128, 128), lambda h, c: (h, ci(c), 0, 0))
    return blk, col, sc, og, st


def gdn_fwd(qkvc, pm, abt, alog, dtb, og):
    S = qkvc.shape[0]
    nc = S // CHUNK
    blk, col, sc, ogs, st = _gdn_specs(nc, False)

    def body(q_ref, k_ref, v_ref, gate_ref, al_ref, bl_ref, alog_ref, dtb_ref, og_ref, o_ref, st_ref, state):
        c = pl.program_id(1)

        @pl.when(c == 0)
        def _():
            state[...] = jnp.zeros_like(state)

        st_ref[...] = state[...]
        out, new_state = _gdn_chunk(q_ref[...], k_ref[...], v_ref[...], gate_ref[...], al_ref[...], bl_ref[...],
                                    alog_ref[...], dtb_ref[...], og_ref[...], state[...])
        o_ref[...] = out
        state[...] = new_state

    return pl.pallas_call(
        body, name="gdn_fwd", grid=(GDN_H, nc),
        in_specs=[blk(0), blk(4), blk(8), blk(16), col(0), col(4), sc, sc, ogs],
        out_specs=[blk(0), st],
        out_shape=[jax.ShapeDtypeStruct((S, 512), F32), jax.ShapeDtypeStruct((GDN_H, nc, 128, 128), F32)],
        scratch_shapes=[pltpu.VMEM((128, 128), F32)],
        compiler_params=_cp("parallel", "arbitrary"),
    )(qkvc, qkvc, qkvc, pm, abt, abt, alog, dtb, og)


def gdn_bwd(qkvc, pm, abt, alog, dtb, og, st, dout):
    S = qkvc.shape[0]
    nc = S // CHUNK
    blk, col, sc, ogs, sts = _gdn_specs(nc, True)

    def body(q_ref, k_ref, v_ref, gate_ref, al_ref, bl_ref, alog_ref, dtb_ref, og_ref, st_ref, do_ref,
             dq_ref, dk_ref, dv_ref, dgate_ref, dal_ref, dbl_ref, dalog_ref, ddtb_ref, dog_ref, dstate):
        c = pl.program_id(1)

        @pl.when(c == 0)
        def _():
            dstate[...] = jnp.zeros_like(dstate)

        args = (q_ref[...], k_ref[...], v_ref[...], gate_ref[...], al_ref[...], bl_ref[...],
                alog_ref[...], dtb_ref[...], og_ref[...], st_ref[...])
        _, vjp = jax.vjp(_gdn_chunk, *args)
        g = vjp((do_ref[...], dstate[...]))
        for o, gv in zip((dq_ref, dk_ref, dv_ref, dgate_ref, dal_ref, dbl_ref), g[:6]):
            o[...] = gv
        dstate[...] = g[9]
        for o, gv in zip((dalog_ref, ddtb_ref, dog_ref), g[6:9]):
            @pl.when(c == 0)
            def _(o=o, gv=gv):
                o[...] = gv

            @pl.when(c > 0)
            def _(o=o, gv=gv):
                o[...] += gv

    colo = pl.BlockSpec((None, CHUNK, 1), lambda h, c: (h, nc - 1 - c, 0))
    ogo = pl.BlockSpec((None, 1, 128), lambda h, c: (h, 0, 0))
    sd = jax.ShapeDtypeStruct
    return pl.pallas_call(
        body, name="gdn_bwd", grid=(GDN_H, nc),
        in_specs=[blk(0), blk(4), blk(8), blk(16), col(0), col(4), sc, sc, ogs, sts, blk(0)],
        out_specs=[blk(0), blk(0), blk(0), blk(0), colo, colo, sc, sc, ogo],
        out_shape=[sd((S, 512), F32)] * 4 + [sd((GDN_H, S, 1), F32)] * 2 + [sd((GDN_H, 1, 1), F32)] * 2
        + [sd((GDN_H, 1, 128), F32)],
        scratch_shapes=[pltpu.VMEM((128, 128), F32)],
        compiler_params=_cp("parallel", "arbitrary"),
    )(qkvc, qkvc, qkvc, pm, abt, abt, alog, dtb, og, st, dout)


HG = 4


def _cattn_chunk(q, kb, vb, bias, valid):
    s = bmm_nt(q, kb) * (CA_D ** -0.5) + bias
    s = jnp.where(valid, s, -1e30)
    return bmm(_softmax(s), vb)


def _cattn_valid(c):
    pos = lax.broadcasted_iota(jnp.int32, (1, 1, CA_BAND), 2) + c * CHUNK
    return pos >= CA_PAD


def cattn_fwd(qn, kp, vp, bias):
    S = qn.shape[1]
    nc = S // CHUNK
    q_spec = pl.BlockSpec((HG, CHUNK, CA_D), lambda h, c: (h, c, 0))
    kv_spec = pl.BlockSpec((HG, S + CA_PAD, CA_D), lambda h, c: (h, 0, 0))
    b_spec = pl.BlockSpec((HG, CHUNK, CA_BAND), lambda h, c: (h, 0, 0))

    def body(q_ref, k_ref, v_ref, b_ref, o_ref):
        c = pl.program_id(1)
        start = pl.multiple_of(c * CHUNK, CHUNK)
        kb = k_ref[:, pl.ds(start, CA_BAND), :]
        vb = v_ref[:, pl.ds(start, CA_BAND), :]
        o_ref[...] = _cattn_chunk(q_ref[...], kb, vb, b_ref[...], _cattn_valid(c))

    return pl.pallas_call(
        body, name="cattn_fwd", grid=(CA_H // HG, nc), in_specs=[q_spec, kv_spec, kv_spec, b_spec],
        out_specs=q_spec, out_shape=jax.ShapeDtypeStruct((CA_H, S, CA_D), F32),
        compiler_params=_cp("parallel", "arbitrary"),
    )(qn, kp, vp, bias)


def cattn_bwd(qn, kp, vp, bias, do):
    S = qn.shape[1]
    nc = S // CHUNK
    q_spec = pl.BlockSpec((HG, CHUNK, CA_D), lambda h, c: (h, c, 0))
    kv_spec = pl.BlockSpec((HG, S + CA_PAD, CA_D), lambda h, c: (h, 0, 0))
    b_spec = pl.BlockSpec((HG, CHUNK, CA_BAND), lambda h, c: (h, 0, 0))

    def body(q_ref, k_ref, v_ref, b_ref, do_ref, dq_ref, dk_ref, dv_ref, db_ref):
        c = pl.program_id(1)

        @pl.when(c == 0)
        def _():
            dk_ref[...] = jnp.zeros_like(dk_ref)
            dv_ref[...] = jnp.zeros_like(dv_ref)
            db_ref[...] = jnp.zeros_like(db_ref)

        start = pl.multiple_of(c * CHUNK, CHUNK)
        kb = k_ref[:, pl.ds(start, CA_BAND), :].astype(F32)
        vb = v_ref[:, pl.ds(start, CA_BAND), :].astype(F32)
        valid = _cattn_valid(c)
        _, vjp = jax.vjp(lambda q, k, v, b: _cattn_chunk(q, k, v, b, valid), q_ref[...], kb, vb, b_ref[...])
        dq, dk, dv, db = vjp(do_ref[...])
        dq_ref[...] = dq
        dk_ref[:, pl.ds(start, CA_BAND), :] += dk
        dv_ref[:, pl.ds(start, CA_BAND), :] += dv
        db_ref[...] += db

    sd = jax.ShapeDtypeStruct
    return pl.pallas_call(
        body, name="cattn_bwd", grid=(CA_H // HG, nc), in_specs=[q_spec, kv_spec, kv_spec, b_spec, q_spec],
        out_specs=[q_spec, kv_spec, kv_spec, b_spec],
        out_shape=[sd((CA_H, S, CA_D), F32), sd((CA_H, S + CA_PAD, CA_D), F32), sd((CA_H, S + CA_PAD, CA_D), F32),
                   sd((CA_H, CHUNK, CA_BAND), F32)],
        compiler_params=_cp("parallel", "arbitrary"),
    )(qn, kp, vp, bias, do)


_REL_IDX = np.clip(np.arange(CHUNK)[:, None] - np.arange(CA_BAND)[None, :] + CA_PAD, -MAX_REL, MAX_REL) + MAX_REL
SKEW_W = CA_BAND + CHUNK


def rel_bias_grad(dbias):
    padded = jnp.pad(dbias, ((0, 0), (0, 0), (CHUNK, 0)))
    flat = jnp.pad(padded.reshape(CA_H, CHUNK * SKEW_W), ((0, 0), (0, CHUNK)))
    skew = flat.reshape(CA_H, CHUNK, SKEW_W + 1)

    def fn(t):
        colsum = jnp.sum(t, axis=1, keepdims=True)
        j = lax.broadcasted_iota(jnp.int32, colsum.shape, 2)
        far = jnp.sum(jnp.where(j < SKEW_W - CHUNK - MAX_REL, colsum, 0.0), axis=2, keepdims=True)
        return colsum, far

    colsum, far = whole("relbias_sum", fn, [skew], [((CA_H, 1, SKEW_W + 1), F32), ((CA_H, 1, 1), F32)])
    near = colsum[:, 0, SKEW_W - CHUNK - MAX_REL:SKEW_W][:, ::-1]
    g = jnp.zeros((CA_H, 2 * MAX_REL + 1), F32)
    g = g.at[:, CHUNK + 1:].set(near)
    return g.at[:, 2 * MAX_REL].add(far[:, 0, 0])


def loss_head(y, target):
    S = y.shape[0]
    tile = min(ROW_TILE, S)

    def body(y_ref, t_ref, dy_ref, acc_ref):
        i = pl.program_id(0)
        e = y_ref[...] - t_ref[...]
        dy_ref[...] = e * (1.0 / D)
        part = jnp.sum(e * e, axis=0, keepdims=True) * (0.5 / D)

        @pl.when(i == 0)
        def _():
            acc_ref[...] = part

        @pl.when(i > 0)
        def _():
            acc_ref[...] += part

    row = pl.BlockSpec((tile, D), lambda i: (i, 0))
    return pl.pallas_call(
        body, name="loss_head", grid=(S // tile,), in_specs=[row, row],
        out_specs=[row, pl.BlockSpec((1, D), lambda i: (0, 0))],
        out_shape=[jax.ShapeDtypeStruct((S, D), F32), jax.ShapeDtypeStruct((1, D), F32)],
        compiler_params=_cp("arbitrary"),
    )(y, target)


ANY = pl.BlockSpec(memory_space=pl.ANY)


def chip_exchange(src, name, per_dest):
    shape = src.shape[1:] if per_dest else src.shape

    def body(src_ref, out_ref, send_sems, recv_sems, local_sem):
        x, y, c = lax.axis_index("x"), lax.axis_index("y"), lax.axis_index("c")
        me = 2 * x + y
        peers = [(1 - x, y), (x, 1 - y), (1 - x, 1 - y)]

        def piece(dest):
            return src_ref.at[dest] if per_dest else src_ref

        local = pltpu.make_async_copy(piece(me), out_ref.at[me], local_sem)
        local.start()
        sends = []
        for j, (px, py) in enumerate(peers):
            cp = pltpu.make_async_remote_copy(
                src_ref=piece(2 * px + py), dst_ref=out_ref.at[me], send_sem=send_sems.at[j],
                recv_sem=recv_sems.at[j], device_id=(px, py, c), device_id_type=MESH)
            cp.start()
            sends.append(cp)
        for j, (px, py) in enumerate(peers):
            pltpu.make_async_remote_copy(
                src_ref=piece(me), dst_ref=out_ref.at[2 * px + py], send_sem=send_sems.at[j],
                recv_sem=recv_sems.at[j], device_id=(px, py, c), device_id_type=MESH).wait_recv()
        for cp in sends:
            cp.wait_send()
        local.wait()

    return pl.pallas_call(
        body, name=name, in_specs=[ANY], out_specs=ANY,
        out_shape=jax.ShapeDtypeStruct((N_CHIPS,) + tuple(shape), src.dtype),
        scratch_shapes=[pltpu.SemaphoreType.DMA((3,)), pltpu.SemaphoreType.DMA((3,)), pltpu.SemaphoreType.DMA],
    )(src)


def sibling_exchange(src, name):
    def body(src_ref, out_ref, send_sem, recv_sem):
        x, y, c = lax.axis_index("x"), lax.axis_index("y"), lax.axis_index("c")
        cp = pltpu.make_async_remote_copy(src_ref=src_ref, dst_ref=out_ref, send_sem=send_sem, recv_sem=recv_sem,
                                          device_id=(x, y, 1 - c), device_id_type=MESH)
        cp.start()
        cp.wait()

    return pl.pallas_call(
        body, name=name, in_specs=[ANY], out_specs=ANY, out_shape=jax.ShapeDtypeStruct(src.shape, src.dtype),
        scratch_shapes=[pltpu.SemaphoreType.DMA, pltpu.SemaphoreType.DMA],
    )(src)


def all_exchange(src, name):
    def body(src_ref, out_ref, send_sems, recv_sems, local_sem):
        x, y, c = lax.axis_index("x"), lax.axis_index("y"), lax.axis_index("c")
        me = 4 * x + 2 * y + c
        local = pltpu.make_async_copy(src_ref, out_ref.at[me], local_sem)
        local.start()
        sends = []
        peers = []
        for k in range(1, N_DEV):
            bx, by, bc = (k >> 2) & 1, (k >> 1) & 1, k & 1
            px = 1 - x if bx else x
            py = 1 - y if by else y
            pc = 1 - c if bc else c
            peers.append((px, py, pc))
        for k, peer in enumerate(peers):
            cp = pltpu.make_async_remote_copy(src_ref=src_ref, dst_ref=out_ref.at[me], send_sem=send_sems.at[k],
                                              recv_sem=recv_sems.at[k], device_id=peer, device_id_type=MESH)
            cp.start()
            sends.append(cp)
        for k, (px, py, pc) in enumerate(peers):
            pltpu.make_async_remote_copy(src_ref=src_ref, dst_ref=out_ref.at[4 * px + 2 * py + pc],
                                         send_sem=send_sems.at[k], recv_sem=recv_sems.at[k],
                                         device_id=(px, py, pc), device_id_type=MESH).wait_recv()
        for cp in sends:
            cp.wait_send()
        local.wait()

    return pl.pallas_call(
        body, name=name, in_specs=[ANY], out_specs=ANY,
        out_shape=jax.ShapeDtypeStruct((N_DEV,) + tuple(src.shape), src.dtype),
        scratch_shapes=[pltpu.SemaphoreType.DMA((N_DEV - 1,)), pltpu.SemaphoreType.DMA((N_DEV - 1,)),
                        pltpu.SemaphoreType.DMA],
    )(src)


def sum_slabs(t, name, tile=512):
    n, R, C = t.shape
    tile = _tile(R, tile, 16)

    def body(t_ref, o_ref):
        acc = t_ref[0].astype(F32)
        for s in range(1, n):
            acc = acc + t_ref[s].astype(F32)
        o_ref[...] = acc

    return pl.pallas_call(
        body, name=name, grid=(R // tile,), in_specs=[pl.BlockSpec((n, tile, C), lambda i: (0, i, 0))],
        out_specs=pl.BlockSpec((tile, C), lambda i: (i, 0)), out_shape=jax.ShapeDtypeStruct((R, C), F32),
        compiler_params=_cp("parallel"),
    )(t)


PACK_ROW_MULT = 512


def _pad_rows(a, mult=16):
    r = (-a.shape[0]) % mult
    return jnp.pad(a, ((0, r), (0, 0))) if r else a


BIG = [
    ("ab_w_in", True, 2, 642), ("c_w_qkv", True, 2, 768), ("xa_w_kv", True, 4, 512),
    ("f_w_gate", True, 4, 704), ("f_w_up", True, 4, 704),
    ("ab_w_out", False, 2, 256), ("c_w_out", False, 2, 256), ("xa_w_q", False, 4, 256),
    ("xa_w_out", False, 4, 256), ("f_w_down", False, 4, 704), ("s5_w_glu", False, 2, 64),
]


def _seg_rows(layers, rows):
    n = layers * rows
    return n + ((-n) % 16)


def pack_shards(p):
    segs = []
    for name, transposed, layers, rows in BIG:
        w = p[name]
        if transposed:
            w = jnp.swapaxes(w, 1, 2)
        segs.append(_pad_rows(w.astype(BF16).reshape(-1, D)))
    small = jnp.concatenate([p["gdn_conv_w"].reshape(-1), p["c_norm_g"].reshape(-1)])
    bits = lax.bitcast_convert_type(small, BF16).reshape(-1)
    segs.append(_pad_rows(jnp.pad(bits, (0, 8 * D - bits.shape[0])).reshape(8, D)))
    return _pad_rows(jnp.concatenate(segs, axis=0), PACK_ROW_MULT)


def unpack_gathered(g):
    out, off = {}, 0
    for name, transposed, layers, rows in BIG:
        n = layers * rows
        seg = g[:, off:off + n].reshape(N_CHIPS, layers, rows, -1)
        full = jnp.swapaxes(seg, 0, 1)
        if name == "s5_w_glu":
            full = g[:, off:off + n].reshape(N_CHIPS, 2, 128, 512)
            full = jnp.swapaxes(full, 0, 1).reshape(2, 512, 512)
        else:
            full = full.reshape(layers, N_CHIPS * rows, D)
        out[name] = full
        off += _seg_rows(layers, rows)
    bits = g[:, off:off + 8].reshape(N_CHIPS, -1)[:, :7168].reshape(N_CHIPS, 3584, 2)
    small = lax.bitcast_convert_type(bits, F32)
    conv = small[:, :3072].reshape(N_CHIPS, 2, 4, 384)
    out["gdn_conv_w"] = jnp.moveaxis(conv, 0, 2).reshape(2, 4, 1536)
    cn = small[:, 3072:].reshape(N_CHIPS, 2, 256)
    out["c_norm_g"] = jnp.moveaxis(cn, 0, 1).reshape(2, D)
    return out


def pack_grads(gr):
    segs = []
    for name, transposed, layers, rows in BIG:
        w = gr[name].astype(BF16)
        if name == "s5_w_glu":
            seg = jnp.swapaxes(w.reshape(2, N_CHIPS, 128, 512), 0, 1).reshape(N_CHIPS, 128, D)
        else:
            seg = jnp.swapaxes(w.reshape(layers, N_CHIPS, rows, D), 0, 1).reshape(N_CHIPS, layers * rows, D)
        r = (-seg.shape[1]) % 16
        if r:
            seg = jnp.pad(seg, ((0, 0), (0, r), (0, 0)))
        segs.append(seg)
    out = jnp.concatenate(segs, axis=1)
    return jnp.pad(out, ((0, 0), (0, (-out.shape[1]) % PACK_ROW_MULT), (0, 0)))


def unpack_reduced(g):
    out, off = {}, 0
    for name, transposed, layers, rows in BIG:
        n = layers * rows
        seg = g[off:off + n]
        if name == "s5_w_glu":
            w = seg.reshape(2, 128, 512)
        else:
            w = seg.reshape(layers, rows, D)
            if transposed:
                w = jnp.swapaxes(w, 1, 2)
        out[name] = w
        off += _seg_rows(layers, rows)
    return out


SMALL = ["ab_norm_g", "s5_a_re", "s5_a_im", "s5_log_dt", "s5_b_re", "s5_b_im", "s5_c_re", "s5_c_im", "s5_d",
         "s5_b_glu", "gdn_conv_w", "gdn_a_log", "gdn_dt_bias", "gdn_out_norm_g", "c_norm_g", "c_q_norm_g",
         "c_k_norm_g", "c_rel_bias", "mem_norm_g", "xa_norm_g", "xa_q_norm_g", "xa_k_norm_g", "f_norm_g"]


def pack_small(d, extra=None):
    parts = [d[n].reshape(-1).astype(F32) for n in SMALL]
    if extra is not None:
        parts.append(extra.reshape(-1))
    flat = jnp.concatenate(parts)
    flat = jnp.pad(flat, (0, (-flat.shape[0]) % (128 * 128)))
    return flat.reshape(-1, 128)


def unpack_small(flat2d, shapes):
    flat = flat2d.reshape(-1)
    out, off = {}, 0
    for n in SMALL:
        sz = int(np.prod(shapes[n]))
        out[n] = flat[off:off + sz].reshape(shapes[n])
        off += sz
    return out, flat[off:]


def _s5_blockdiag_b(b):
    bt = jnp.swapaxes(b, 1, 2).reshape(S5_GB, 8, S5_C, S5_P)
    eye = jnp.eye(8, dtype=b.dtype)
    return jnp.einsum("bgcp,gh->bgchp", bt, eye).reshape(S5_GB, 8 * S5_C, 8 * S5_P)


def _s5_blockdiag_c(c):
    ct = jnp.swapaxes(c, 1, 2).reshape(S5_GB, 8, S5_P, S5_C)
    eye = jnp.eye(8, dtype=c.dtype)
    return jnp.einsum("bgpc,gh->bgphc", ct, eye).reshape(S5_GB, 8 * S5_P, 8 * S5_C)


def _s5_diag_b(db):
    t = db.reshape(S5_GB, 8, S5_C, 8, S5_P)
    idx = np.arange(8)
    d = t[:, idx, :, idx, :]
    d = jnp.moveaxis(d, 0, 1).reshape(S5_G, S5_C, S5_P)
    return jnp.swapaxes(d, 1, 2)


def _s5_diag_c(dc):
    t = dc.reshape(S5_GB, 8, S5_P, 8, S5_C)
    idx = np.arange(8)
    d = t[:, idx, :, idx, :]
    d = jnp.moveaxis(d, 0, 1).reshape(S5_G, S5_P, S5_C)
    return jnp.swapaxes(d, 1, 2)


def _heads(t):
    return jnp.swapaxes(t.reshape(t.shape[0], CA_H, CA_D), 0, 1)


def _unheads(t):
    return jnp.swapaxes(t, 0, 1).reshape(t.shape[1], D)


def local_step(x, mem, target, p, w):
    S = x.shape[0]
    row2 = lambda a: a.reshape(1, -1)
    saved = []
    (mem_n,) = row_fwd("mem_norm", f_norm, [mem], [row2(p["mem_norm_g"])], [(D, BF16)])
    gw = {n: [None] * w[n].shape[0] for n in w}
    gs = {}

    for layer in range(DEPTH):
        i = layer // 2
        sv = {"x0": x}
        if layer % 2 == 0:
            (h,) = row_fwd("norm", f_norm, [x], [row2(p["ab_norm_g"][i])], [(D, BF16)])
            w_in = w["ab_w_in"][i]
            pm = matmul("nt", h, w_in[:2560], "proj_main")
            pab = matmul("nt", h, w_in[2560:], "proj_ab")
            s5p = dict(
                are=p["s5_a_re"][i].reshape(1, -1), aim=p["s5_a_im"][i].reshape(1, -1),
                ldt=jnp.broadcast_to(p["s5_log_dt"][i][:, None], (S5_G, S5_P)).reshape(1, -1),
                b_re=_s5_blockdiag_b(p["s5_b_re"][i]), b_im=_s5_blockdiag_b(p["s5_b_im"][i]),
                c_re=_s5_blockdiag_c(p["s5_c_re"][i]), c_im=_s5_blockdiag_c(p["s5_c_im"][i]),
                dv=p["s5_d"][i].reshape(1, -1))
            y5, st5 = s5_fwd(pm, **s5p)
            (a_out,) = row_fwd("glu", f_glu, [y5], [w["s5_w_glu"][i], row2(p["s5_b_glu"][i])], [(S5_W, F32)])
            conv_w = w["gdn_conv_w"][i]
            qkvc = conv_fwd(pm, conv_w)
            abt = jnp.swapaxes(pab, 0, 1)[:, :, None]
            alog = p["gdn_a_log"][i].reshape(GDN_H, 1, 1)
            dtb = p["gdn_dt_bias"][i].reshape(GDN_H, 1, 1)
            og = row2(p["gdn_out_norm_g"][i])
            b_out, stg = gdn_fwd(qkvc, pm, abt, alog, dtb, og)
            cat = jnp.concatenate([a_out, b_out], axis=1)
            x = matmul("nn", cat, w["ab_w_out"][i], "mix_out", add=x)
            sv.update(h=h, pm=pm, s5p=s5p, y5=y5, st5=st5, qkvc=qkvc, abt=abt, alog=alog, dtb=dtb, og=og,
                      stg=stg, cat=cat, conv_w=conv_w)
        else:
            (h,) = row_fwd("norm", f_norm, [x], [row2(w["c_norm_g"][i])], [(D, BF16)])
            qkv = matmul("nt", h, w["c_w_qkv"][i], "proj_qkv")
            q3, k3, v3 = _heads(qkv[:, :D]), _heads(qkv[:, D:2 * D]), _heads(qkv[:, 2 * D:])
            (qn,) = row_fwd("headnorm_q", f_headnorm, [q3], [row2(p["c_q_norm_g"][i])], [((CA_H, CA_D), F32)])
            (kn,) = row_fwd("headnorm_k", f_headnorm, [k3], [row2(p["c_k_norm_g"][i])], [((CA_H, CA_D), BF16)])
            kp = jnp.pad(kn, ((0, 0), (CA_PAD, 0), (0, 0)))
            vp = jnp.pad(v3.astype(BF16), ((0, 0), (CA_PAD, 0), (0, 0)))
            bias = p["c_rel_bias"][i][:, _REL_IDX]
            o3 = cattn_fwd(qn, kp, vp, bias)
            o = _unheads(o3).astype(BF16)
            x = matmul("nn", o, w["c_w_out"][i], "mix_out", add=x)
            sv.update(h=h, q3=q3, k3=k3, qn=qn, kp=kp, vp=vp, bias=bias, o=o)
        sv["x1"] = x
        (hq,) = row_fwd("norm", f_norm, [x], [row2(p["xa_norm_g"][layer])], [(D, BF16)])
        qx = matmul("nn", hq, w["xa_w_q"][layer], "xa_q")
        kv = matmul("nt", mem_n, w["xa_w_kv"][layer], "xa_kv")
        xqg, xkg = row2(p["xa_q_norm_g"][layer]), row2(p["xa_k_norm_g"][layer])
        (ox,) = row_fwd("xattn", f_xattn, [qx], [kv, xqg, xkg], [(D, BF16)])
        x = matmul("nn", ox, w["xa_w_out"][layer], "xa_out", add=x)
        sv.update(hq=hq, qx=qx, kv=kv, ox=ox)
        sv["x2"] = x
        (hf,) = row_fwd("norm", f_norm, [x], [row2(p["f_norm_g"][layer])], [(D, BF16)])
        w_gu = jnp.concatenate([w["f_w_gate"][layer], w["f_w_up"][layer]], axis=0)
        gu = matmul("nt", hf, w_gu, "ffn_gu")
        (act,) = row_fwd("swiglu", f_swiglu, [gu], [], [(FFN, BF16)])
        x = matmul("nn", act, w["f_w_down"][layer], "ffn_down", add=x)
        sv.update(hf=hf, gu=gu, act=act, w_gu=w_gu)
        saved.append(sv)

    dx, loss_vec = loss_head(x, target)

    dmem_n = None
    for layer in reversed(range(DEPTH)):
        i = layer // 2
        sv = saved[layer]
        dact = matmul("nt", dx, w["f_w_down"][layer], "ffn_dact")
        gw["f_w_down"][layer] = matmul("tn", sv["act"], dx, "ffn_dwd", out_dtype=BF16)
        (dgu,) = row_bwd("swiglu_bwd", f_swiglu, [sv["gu"]], [], [dact], [0], [], row_dtypes=[BF16])
        dwgu = matmul("tn", dgu, sv["hf"], "ffn_dwgu", out_dtype=BF16)
        gw["f_w_gate"][layer], gw["f_w_up"][layer] = dwgu[:FFN], dwgu[FFN:]
        dh = matmul("nn", dgu, sv["w_gu"], "ffn_dh")
        dx, dg = row_bwd("norm_bwd", f_norm_res, [sv["x2"]], [row2(p["f_norm_g"][layer])], [dh, dx], [0], [0])
        gs.setdefault("f_norm_g", [None] * DEPTH)[layer] = dg[0]
        do = matmul("nt", dx, w["xa_w_out"][layer], "xa_do")
        gw["xa_w_out"][layer] = matmul("tn", sv["ox"], dx, "xa_dwo", out_dtype=BF16)
        xqg, xkg = row2(p["xa_q_norm_g"][layer]), row2(p["xa_k_norm_g"][layer])
        dqx, dkv, dqg, dkg = row_bwd("xattn_bwd", f_xattn, [sv["qx"]], [sv["kv"], xqg, xkg], [do],
                                     [0], [0, 1, 2])
        gs.setdefault("xa_q_norm_g", [None] * DEPTH)[layer] = dqg[0]
        gs.setdefault("xa_k_norm_g", [None] * DEPTH)[layer] = dkg[0]
        gw["xa_w_q"][layer] = matmul("tn", sv["hq"], dqx, "xa_dwq", out_dtype=BF16)
        dhq = matmul("nt", dqx, w["xa_w_q"][layer], "xa_dhq")
        gw["xa_w_kv"][layer] = matmul("tn", dkv, mem_n, "xa_dwkv", out_dtype=BF16)
        dmem_n = matmul("nn", dkv, w["xa_w_kv"][layer], "xa_dmem", add=dmem_n)
        dx, dg = row_bwd("norm_bwd", f_norm_res, [sv["x1"]], [row2(p["xa_norm_g"][layer])], [dhq, dx], [0], [0])
        gs.setdefault("xa_norm_g", [None] * DEPTH)[layer] = dg[0]
        if layer % 2 == 0:
            dcat = matmul("nt", dx, w["ab_w_out"][i], "mix_dcat")
            gw["ab_w_out"][i] = matmul("tn", sv["cat"], dx, "mix_dwo", out_dtype=BF16)
            dy5, dwglu, dbglu = row_bwd("glu_bwd", f_glu, [sv["y5"]], [w["s5_w_glu"][i], row2(p["s5_b_glu"][i])],
                                        [(dcat, S5_W, 0)], [0], [0, 1])
            gw["s5_w_glu"][i] = dwglu
            gs.setdefault("s5_b_glu", [None] * 2)[i] = dbglu[0]
            s5p = sv["s5p"]
            du, dare, daim, dldt, dbre, dbim, dcre, dcim, ddv = s5_bwd(sv["pm"], sv["st5"], dy5, **s5p)
            (dldt_g,) = whole("s5_dt_sum", lambda t: (jnp.sum(t, axis=1, keepdims=True),),
                              [dldt.reshape(S5_G, S5_P)], [((S5_G, 1), F32)])
            for nme, val in (("s5_a_re", dare.reshape(S5_G, S5_P)), ("s5_a_im", daim.reshape(S5_G, S5_P)),
                             ("s5_log_dt", dldt_g[:, 0]), ("s5_b_re", _s5_diag_b(dbre)),
                             ("s5_b_im", _s5_diag_b(dbim)), ("s5_c_re", _s5_diag_c(dcre)),
                             ("s5_c_im", _s5_diag_c(dcim)), ("s5_d", ddv.reshape(S5_G, S5_C))):
                gs.setdefault(nme, [None] * 2)[i] = val
            dq, dk, dv, dgate, dal, dbl, dalog, ddtb, dog = gdn_bwd(
                sv["qkvc"], sv["pm"], sv["abt"], sv["alog"], sv["dtb"], sv["og"], sv["stg"], dcat[:, S5_W:])
            (dog_s,) = whole("gdn_og_sum", lambda t: (jnp.sum(t, axis=0, keepdims=True),),
                             [dog.reshape(GDN_H, GDN_D)], [((1, GDN_D), F32)])
            gs.setdefault("gdn_out_norm_g", [None] * 2)[i] = dog_s[0]
            gs.setdefault("gdn_a_log", [None] * 2)[i] = dalog.reshape(GDN_H)
            gs.setdefault("gdn_dt_bias", [None] * 2)[i] = ddtb.reshape(GDN_H)
            dqkvc = jnp.concatenate([dq, dk, dv], axis=1)
            dqkv, dconv = conv_bwd(sv["pm"], sv["conv_w"], dqkvc)
            gs.setdefault("gdn_conv_w", [None] * 2)[i] = dconv
            dpm = jnp.concatenate([du, dqkv, dgate], axis=1).astype(BF16)
            dpab = jnp.swapaxes(jnp.concatenate([dal, dbl], axis=0)[:, :, 0], 0, 1)
            dw_main = matmul("tn", dpm, sv["h"], "proj_dw", out_dtype=BF16)
            dw_ab = matmul("tn", dpab, sv["h"], "proj_ab_dw", out_dtype=BF16)
            gw["ab_w_in"][i] = jnp.concatenate([dw_main, dw_ab], axis=0)
            w_in = w["ab_w_in"][i]
            dh = matmul("nn", dpm, w_in[:2560], "proj_dh")
            dh = matmul("nn", dpab, w_in[2560:], "proj_ab_dh", add=dh)
            dx, dg = row_bwd("norm_bwd", f_norm_res, [sv["x0"]], [row2(p["ab_norm_g"][i])], [dh, dx], [0], [0])
            gs.setdefault("ab_norm_g", [None] * 2)[i] = dg[0]
        else:
            do = matmul("nt", dx, w["c_w_out"][i], "mix_dcat")
            gw["c_w_out"][i] = matmul("tn", sv["o"], dx, "mix_dwo", out_dtype=BF16)
            dqn, dkp, dvp, dbias = cattn_bwd(sv["qn"], sv["kp"], sv["vp"], sv["bias"], _heads(do))
            gs.setdefault("c_rel_bias", [None] * 2)[i] = rel_bias_grad(dbias)
            dq3, dqg = row_bwd("headnorm_bwd", f_headnorm, [sv["q3"]], [row2(p["c_q_norm_g"][i])], [dqn], [0], [0])
            dk3, dkg = row_bwd("headnorm_bwd", f_headnorm, [sv["k3"]], [row2(p["c_k_norm_g"][i])],
                               [dkp[:, CA_PAD:]], [0], [0])
            gs.setdefault("c_q_norm_g", [None] * 2)[i] = dqg[0]
            gs.setdefault("c_k_norm_g", [None] * 2)[i] = dkg[0]
            dqkv = jnp.concatenate([_unheads(dq3), _unheads(dk3), _unheads(dvp[:, CA_PAD:])], axis=1).astype(BF16)
            gw["c_w_qkv"][i] = matmul("tn", dqkv, sv["h"], "proj_qkv_dw", out_dtype=BF16)
            dh = matmul("nn", dqkv, w["c_w_qkv"][i], "proj_qkv_dh")
            dx, dg = row_bwd("norm_bwd", f_norm_res, [sv["x0"]], [row2(w["c_norm_g"][i])], [dh, dx], [0], [0])
            gs.setdefault("c_norm_g", [None] * 2)[i] = dg[0]
    (dmg,) = row_bwd("mem_norm_bwd", f_norm, [mem], [row2(p["mem_norm_g"])], [dmem_n], [], [0])
    small = {n: jnp.stack(v) for n, v in gs.items()}
    small["mem_norm_g"] = dmg[0]
    big = {n: jnp.stack(v) for n, v in gw.items() if n not in ("gdn_conv_w", "c_norm_g")}
    return loss_vec, dx, small, big


def adam(w, g, m, v, name):
    shape = w.shape
    cols = shape[-1]
    w2, g2, m2, v2 = (t.reshape(-1, cols) for t in (w, g, m, v))
    rows = w2.shape[0]
    tile = rows if rows <= 512 else _tile(rows, 512, 8)
    outs = row_fwd(name, f_adam, [w2, g2, m2, v2], [], [(cols, F32)] * 3, tile=tile)
    return tuple(o.reshape(shape) for o in outs)


WEIGHTS = ['ab_norm_g', 'ab_w_in', 'ab_w_out', 's5_a_re', 's5_a_im', 's5_log_dt', 's5_b_re', 's5_b_im', 's5_c_re',
           's5_c_im', 's5_d', 's5_w_glu', 's5_b_glu', 'gdn_conv_w', 'gdn_a_log', 'gdn_dt_bias', 'gdn_out_norm_g',
           'c_norm_g', 'c_w_qkv', 'c_w_out', 'c_q_norm_g', 'c_k_norm_g', 'c_rel_bias', 'mem_norm_g', 'xa_norm_g',
           'xa_w_q', 'xa_w_kv', 'xa_w_out', 'xa_q_norm_g', 'xa_k_norm_g', 'f_norm_g', 'f_w_gate', 'f_w_up',
           'f_w_down']
SHARDED_SMALL = {"gdn_conv_w": (2, 384), "c_norm_g": (1, 256)}


def kernel(x, mem, ab_norm_g, ab_w_in, ab_w_out, s5_a_re, s5_a_im, s5_log_dt, s5_b_re, s5_b_im, s5_c_re, s5_c_im, s5_d, s5_w_glu, s5_b_glu, gdn_conv_w, gdn_a_log, gdn_dt_bias, gdn_out_norm_g, c_norm_g, c_w_qkv, c_w_out, c_q_norm_g, c_k_norm_g, c_rel_bias, mem_norm_g, xa_norm_g, xa_w_q, xa_w_kv, xa_w_out, xa_q_norm_g, xa_k_norm_g, f_norm_g, f_w_gate, f_w_up, f_w_down, loss_target, m_ab_norm_g, m_ab_w_in, m_ab_w_out, m_s5_a_re, m_s5_a_im, m_s5_log_dt, m_s5_b_re, m_s5_b_im, m_s5_c_re, m_s5_c_im, m_s5_d, m_s5_w_glu, m_s5_b_glu, m_gdn_conv_w, m_gdn_a_log, m_gdn_dt_bias, m_gdn_out_norm_g, m_c_norm_g, m_c_w_qkv, m_c_w_out, m_c_q_norm_g, m_c_k_norm_g, m_c_rel_bias, m_mem_norm_g, m_xa_norm_g, m_xa_w_q, m_xa_w_kv, m_xa_w_out, m_xa_q_norm_g, m_xa_k_norm_g, m_f_norm_g, m_f_w_gate, m_f_w_up, m_f_w_down, v_ab_norm_g, v_ab_w_in, v_ab_w_out, v_s5_a_re, v_s5_a_im, v_s5_log_dt, v_s5_b_re, v_s5_b_im, v_s5_c_re, v_s5_c_im, v_s5_d, v_s5_w_glu, v_s5_b_glu, v_gdn_conv_w, v_gdn_a_log, v_gdn_dt_bias, v_gdn_out_norm_g, v_c_norm_g, v_c_w_qkv, v_c_w_out, v_c_q_norm_g, v_c_k_norm_g, v_c_rel_bias, v_mem_norm_g, v_xa_norm_g, v_xa_w_q, v_xa_w_kv, v_xa_w_out, v_xa_q_norm_g, v_xa_k_norm_g, v_f_norm_g, v_f_w_gate, v_f_w_up, v_f_w_down):
    args = locals()
    p = {n: args[n] for n in WEIGHTS}
    m = {n: args["m_" + n] for n in WEIGHTS}
    v = {n: args["v_" + n] for n in WEIGHTS}
    chip = 2 * lax.axis_index("x") + lax.axis_index("y")

    gathered = chip_exchange(pack_shards(p), "gather_weights", per_dest=False)
    w = unpack_gathered(gathered)

    loss_vec, dx, g_small, g_big = local_step(x[0], mem[0], loss_target[0], p, w)

    slabs = chip_exchange(pack_grads(g_big), "scatter_grads", per_dest=True)
    part = sum_slabs(slabs, "sum_chips")
    other = sibling_exchange(part, "sibling_grads")
    (total,) = row_fwd("sum_cores", lambda a, b: (a + b,), [part, other], [], [(D, F32)], tile=PACK_ROW_MULT)
    grads = unpack_reduced(total)

    full_shapes = {n: ((2, 4, 1536) if n == "gdn_conv_w" else (2, D) if n == "c_norm_g" else p[n].shape)
                   for n in SMALL}
    small_all = all_exchange(pack_small(g_small, extra=loss_vec), "gather_small")
    small_sum = sum_slabs(small_all, "sum_small")
    g_s, rest = unpack_small(small_sum, full_shapes)
    (loss11,) = whole("loss_sum", lambda t: (jnp.sum(t, axis=1, keepdims=True),), [rest[:D].reshape(1, D)],
                      [((1, 1), F32)])
    for n, (axis, width) in SHARDED_SMALL.items():
        g_s[n] = lax.dynamic_slice_in_dim(g_s[n], chip * width, width, axis=axis)
    grads.update(g_s)

    shard_shapes = {n: p[n].shape for n in SMALL}
    packs = [pack_small(d) for d in (p, grads, m, v)]
    d_s, m_s, v_s = adam(*packs, name="adam_small")
    delta, new_m, new_v = {}, {}, {}
    for dst, src in ((delta, d_s), (new_m, m_s), (new_v, v_s)):
        dst.update(unpack_small(src, shard_shapes)[0])
    for name, _, _, _ in BIG:
        delta[name], new_m[name], new_v[name] = adam(p[name], grads[name], m[name], v[name], "adam_" + name)

    return (loss11[0, 0], dx[None], *[grads[n] for n in WEIGHTS], *[delta[n] for n in WEIGHTS],
            *[new_m[n] for n in WEIGHTS], *[new_v[n] for n in WEIGHTS])
```

```python
import functools
import math

import numpy as np
import jax
import jax.numpy as jnp
from jax import lax
from jax.experimental import pallas as pl
from jax.experimental.pallas import tpu as pltpu

F32 = jnp.float32
BF16 = jnp.bfloat16
MESH = pl.DeviceIdType.MESH

D = 1024
CHUNK = 64
N_MEM = 256
EPS = 1e-6
S5_W = 512
S5_G = 32
S5_C = 16
S5_P = 64
S5_GB = 4
GDN_H = 4
GDN_D = 128
CA_H = 16
CA_D = 64
CA_LEFT = 8
CA_BAND = (CA_LEFT + 1) * CHUNK
CA_PAD = CA_LEFT * CHUNK
MAX_REL = 128
XA_H = 4
XA_D = 256
FFN = 2816
DEPTH = 4
N_CHIPS = 4
N_DEV = 8
LR, B1, B2, AEPS, WD, STEP = 0.001, 0.9, 0.999, 1e-08, 0.01, 10

VMEM_LIMIT = 56 * 1024 * 1024
ROW_TILE = 256
HI = lax.Precision.HIGHEST


def _cp(*sem):
    return pltpu.CompilerParams(dimension_semantics=sem, vmem_limit_bytes=VMEM_LIMIT)


def _dg(a, b, ca, cb):
    return lax.dot_general(a.astype(BF16), b.astype(BF16), (((ca,), (cb,)), ((), ())),
                           preferred_element_type=F32)


@jax.custom_vjp
def mm(a, b):
    return _dg(a, b, 1, 0)


@jax.custom_vjp
def mm_nt(a, b):
    return _dg(a, b, 1, 1)


@jax.custom_vjp
def mm_tn(a, b):
    return _dg(a, b, 0, 0)


mm.defvjp(lambda a, b: (mm(a, b), (a, b)), lambda r, g: (mm_nt(g, r[1]), mm_tn(r[0], g)))
mm_nt.defvjp(lambda a, b: (mm_nt(a, b), (a, b)), lambda r, g: (mm(g, r[1]), mm_tn(g, r[0])))
mm_tn.defvjp(lambda a, b: (mm_tn(a, b), (a, b)), lambda r, g: (mm_nt(r[1], g), mm(r[0], g)))


def _bdg(a, b, ca, cb):
    return lax.dot_general(a.astype(BF16), b.astype(BF16), (((ca,), (cb,)), ((0,), (0,))),
                           preferred_element_type=F32)


@jax.custom_vjp
def bmm(a, b):
    return _bdg(a, b, 2, 1)


@jax.custom_vjp
def bmm_nt(a, b):
    return _bdg(a, b, 2, 2)


@jax.custom_vjp
def bmm_tn(a, b):
    return _bdg(a, b, 1, 1)


bmm.defvjp(lambda a, b: (bmm(a, b), (a, b)), lambda r, g: (bmm_nt(g, r[1]), bmm_tn(r[0], g)))
bmm_nt.defvjp(lambda a, b: (bmm_nt(a, b), (a, b)), lambda r, g: (bmm(g, r[1]), bmm_tn(g, r[0])))
bmm_tn.defvjp(lambda a, b: (bmm_tn(a, b), (a, b)), lambda r, g: (bmm_nt(r[1], g), bmm(r[0], g)))


def mmf(a, b):
    return jnp.dot(a, b, precision=HI, preferred_element_type=F32)


def _rms(x, g):
    return x * lax.rsqrt(jnp.mean(x * x, axis=-1, keepdims=True) + EPS) * g


def _softmax(s):
    e = jnp.exp(s - lax.stop_gradient(jnp.max(s, axis=-1, keepdims=True)))
    return e / jnp.sum(e, axis=-1, keepdims=True)


def _softplus(x):
    return jnp.maximum(x, 0.0) + jnp.log(1.0 + jnp.exp(-jnp.abs(x)))


def _tile(n, cap, align):
    if n <= cap:
        return n
    best = None
    for d in range(align, cap + 1, align):
        if n % d == 0:
            best = d
    assert best is not None, (n, cap, align)
    return best


def matmul(mode, a, b, name, out_dtype=F32, add=None):
    if mode == "nn":
        (M, K), (K2, N) = a.shape, b.shape
    elif mode == "nt":
        (M, K), (N, K2) = a.shape, b.shape
    else:
        (K, M), (K2, N) = a.shape, b.shape
    assert K == K2, (mode, a.shape, b.shape)
    tm, tn, tk = _tile(M, 512, 128), _tile(N, 512, 128), _tile(K, 2048, 128)
    nk = K // tk
    if mode == "nn":
        a_spec = pl.BlockSpec((tm, tk), lambda i, j, k: (i, k))
        b_spec = pl.BlockSpec((tk, tn), lambda i, j, k: (k, j))
        dn = (((1,), (0,)), ((), ()))
    elif mode == "nt":
        a_spec = pl.BlockSpec((tm, tk), lambda i, j, k: (i, k))
        b_spec = pl.BlockSpec((tn, tk), lambda i, j, k: (j, k))
        dn = (((1,), (1,)), ((), ()))
    else:
        a_spec = pl.BlockSpec((tk, tm), lambda i, j, k: (k, i))
        b_spec = pl.BlockSpec((tk, tn), lambda i, j, k: (k, j))
        dn = (((0,), (0,)), ((), ()))
    o_spec = pl.BlockSpec((tm, tn), lambda i, j, k: (i, j))
    has_add = add is not None

    def body(*refs):
        a_ref, b_ref = refs[0], refs[1]
        add_ref = refs[2] if has_add else None
        o_ref = refs[3] if has_add else refs[2]
        acc_ref = refs[-1]
        p = lax.dot_general(a_ref[...].astype(BF16), b_ref[...].astype(BF16), dn,
                            preferred_element_type=F32)

        def finish(total):
            if has_add:
                total = total + add_ref[...]
            o_ref[...] = total.astype(o_ref.dtype)

        if nk == 1:
            finish(p)
        else:
            k = pl.program_id(2)

            @pl.when(k == 0)
            def _():
                acc_ref[...] = p

            @pl.when(k > 0)
            def _():
                acc_ref[...] += p

            @pl.when(k == nk - 1)
            def _():
                finish(acc_ref[...])

    ins = [a, b] + ([add] if has_add else [])
    specs = [a_spec, b_spec] + ([o_spec] if has_add else [])
    return pl.pallas_call(
        body, name=name, grid=(M // tm, N // tn, nk), in_specs=specs, out_specs=o_spec,
        out_shape=jax.ShapeDtypeStruct((M, N), out_dtype),
        scratch_shapes=[pltpu.VMEM((tm, tn), F32)],
        compiler_params=_cp("parallel", "parallel", "arbitrary"),
    )(*ins)


def _row_spec(arr, tile):
    if isinstance(arr, tuple):
        a, w, cb = arr
        return a, pl.BlockSpec((tile, w), lambda i, cb=cb: (i, cb)), (tile, w)
    if arr.ndim == 3:
        d0, _, d2 = arr.shape
        return arr, pl.BlockSpec((d0, tile, d2), lambda i: (0, i, 0)), (d0, tile, d2)
    return arr, pl.BlockSpec((tile, arr.shape[1]), lambda i: (i, 0)), (tile, arr.shape[1])


def _full_spec(arr):
    nd = arr.ndim
    return pl.BlockSpec(arr.shape, lambda i, nd=nd: (0,) * nd)


def _n_rows(arr):
    a = arr[0] if isinstance(arr, tuple) else arr
    return a.shape[1] if a.ndim == 3 else a.shape[0]


def _row_out_shape(shape_tail, n, dtype):
    if isinstance(shape_tail, tuple):
        d0, d2 = shape_tail
        return (jax.ShapeDtypeStruct((d0, n, d2), dtype),
                lambda tile: pl.BlockSpec((d0, tile, d2), lambda i: (0, i, 0)))
    return (jax.ShapeDtypeStruct((n, shape_tail), dtype),
            lambda tile: pl.BlockSpec((tile, shape_tail), lambda i: (i, 0)))


def _f32(v):
    return v.astype(F32) if v.dtype == BF16 else v


def row_fwd(name, fn, rows, fulls, outs, tile=ROW_TILE):
    n = _n_rows(rows[0])
    tile = min(tile, n)
    assert n % tile == 0, (name, n, tile)
    rs = [_row_spec(r, tile) for r in rows]
    os_ = [_row_out_shape(w, n, dt) for w, dt in outs]
    nr, nf = len(rows), len(fulls)

    def body(*refs):
        vals = [_f32(r[...]) for r in refs[:nr + nf]]
        res = fn(*vals)
        for r, v in zip(refs[nr + nf:], res):
            r[...] = v.astype(r.dtype)

    out = pl.pallas_call(
        body, name=name, grid=(n // tile,),
        in_specs=[s for _, s, _ in rs] + [_full_spec(f) for f in fulls],
        out_specs=[mk(tile) for _, mk in os_], out_shape=[sh for sh, _ in os_],
        compiler_params=_cp("parallel"),
    )(*[a for a, _, _ in rs], *fulls)
    return out


def row_bwd(name, fn, rows, fulls, cts, want_rows, want_fulls, row_dtypes=None, tile=ROW_TILE):
    n = _n_rows(rows[0])
    tile = min(tile, n)
    assert n % tile == 0, (name, n, tile)
    rs = [_row_spec(r, tile) for r in rows]
    cs = [_row_spec(c, tile) for c in cts]
    nr, nf, nc = len(rows), len(fulls), len(cts)
    row_dtypes = row_dtypes or [F32] * len(want_rows)
    out_shapes, out_specs = [], []
    for k, idx in enumerate(want_rows):
        a, _, blk = rs[idx]
        if len(blk) == 3:
            sh, mk = _row_out_shape((blk[0], blk[2]), n, row_dtypes[k])
        else:
            sh, mk = _row_out_shape(blk[1], n, row_dtypes[k])
        out_shapes.append(sh)
        out_specs.append(mk(tile))
    for idx in want_fulls:
        out_shapes.append(jax.ShapeDtypeStruct(fulls[idx].shape, F32))
        out_specs.append(_full_spec(fulls[idx]))
    n_wr = len(want_rows)

    def body(*refs):
        i = pl.program_id(0)
        vals = [_f32(r[...]) for r in refs[:nr + nf]]
        ct_vals = [_f32(r[...]) for r in refs[nr + nf:nr + nf + nc]]
        outs = refs[nr + nf + nc:]
        _, vjp = jax.vjp(fn, *vals)
        grads = vjp(tuple(ct_vals))
        for k, idx in enumerate(want_rows):
            outs[k][...] = grads[idx].astype(outs[k].dtype)
        for k, idx in enumerate(want_fulls):
            o = outs[n_wr + k]
            g = grads[nr + idx]

            @pl.when(i == 0)
            def _(o=o, g=g):
                o[...] = g

            @pl.when(i > 0)
            def _(o=o, g=g):
                o[...] += g

    out = pl.pallas_call(
        body, name=name, grid=(n // tile,),
        in_specs=[s for _, s, _ in rs] + [_full_spec(f) for f in fulls] + [s for _, s, _ in cs],
        out_specs=out_specs, out_shape=out_shapes,
        compiler_params=_cp("arbitrary"),
    )(*[a for a, _, _ in rs], *fulls, *[a for a, _, _ in cs])
    return out


def whole(name, fn, args, outs):
    def body(*refs):
        res = fn(*[r[...] for r in refs[:len(args)]])
        for r, v in zip(refs[len(args):], res):
            r[...] = v.astype(r.dtype)

    return pl.pallas_call(
        body, name=name, out_shape=[jax.ShapeDtypeStruct(s, d) for s, d in outs],
        compiler_params=pltpu.CompilerParams(vmem_limit_bytes=VMEM_LIMIT),
    )(*args)


def f_norm(x, g):
    return (_rms(x, g),)


def f_norm_res(x, g):
    return _rms(x, g), x


def f_swiglu(gu):
    g, u = gu[:, :FFN], gu[:, FFN:]
    return (g * jax.nn.sigmoid(g) * u,)


def f_glu(y, w, b):
    h = jax.nn.gelu(y)
    return (h * jax.nn.sigmoid(mm(h, w) + b),)


def f_xattn(q, kv, qg, kg):
    outs = []
    for h in range(XA_H):
        sl = slice(h * XA_D, (h + 1) * XA_D)
        qn = _rms(q[:, sl], qg)
        kn = _rms(kv[:, sl], kg)
        vh = kv[:, D + h * XA_D:D + (h + 1) * XA_D]
        p = _softmax(mm_nt(qn, kn) * (XA_D ** -0.5))
        outs.append(mm(p, vh))
    return (jnp.concatenate(outs, axis=-1),)


def f_headnorm(t, g):
    return (_rms(t, g),)


def f_adam(w, g, m, v):
    m2 = B1 * m + (1.0 - B1) * g
    v2 = B2 * v + (1.0 - B2) * (g * g)
    m_hat = m2 / (1.0 - B1 ** STEP)
    v_hat = v2 / (1.0 - B2 ** STEP)
    delta = -LR * (m_hat / (jnp.sqrt(v_hat) + AEPS) + WD * w)
    return delta, m2, v2


def _s5_chunk(u, sre, sim, are, aim, ldt, b_re, b_im, c_re, c_im, dv):
    T = u.shape[0]
    dt = jnp.exp(ldt)
    ar, ai = are * dt, aim * dt
    t = lax.broadcasted_iota(jnp.int32, (T, 1), 0).astype(F32)
    row = lax.broadcasted_iota(jnp.int32, (T, T), 0)
    col = lax.broadcasted_iota(jnp.int32, (T, T), 1)
    tri = (row >= col).astype(F32)
    mag, inv = jnp.exp(t * ar), jnp.exp(-t * ar)
    cs, sn = jnp.cos(t * ai), jnp.sin(t * ai)
    e_re, e_im = mag * cs, mag * sn
    n_re, n_im = inv * cs, -inv * sn
    l_re, l_im = jnp.exp(ar) * jnp.cos(ai), jnp.exp(ar) * jnp.sin(ai)
    den = are * are + aim * aim
    k_re = ((l_re - 1.0) * are + l_im * aim) / den
    k_im = (l_im * are - (l_re - 1.0) * aim) / den
    x_re, x_im = mm(u, b_re), mm(u, b_im)
    bu_re = k_re * x_re - k_im * x_im
    bu_im = k_re * x_im + k_im * x_re
    v_re = bu_re * n_re - bu_im * n_im
    v_im = bu_re * n_im + bu_im * n_re
    p_re = l_re * sre - l_im * sim
    p_im = l_re * sim + l_im * sre
    w_re = mmf(tri, v_re) + p_re
    w_im = mmf(tri, v_im) + p_im
    s_re = e_re * w_re - e_im * w_im
    s_im = e_re * w_im + e_im * w_re
    y = mm(s_re, c_re) - mm(s_im, c_im) + dv * u
    tl = float(T - 1)
    m_re, m_im = jnp.exp(tl * ar) * jnp.cos(tl * ai), jnp.exp(tl * ar) * jnp.sin(tl * ai)
    z_re = jnp.sum(v_re, axis=0, keepdims=True) + p_re
    z_im = jnp.sum(v_im, axis=0, keepdims=True) + p_im
    return y, m_re * z_re - m_im * z_im, m_re * z_im + m_im * z_re


def _s5_specs(nc, rev):
    T = CHUNK

    def ci(c):
        return nc - 1 - c if rev else c

    u_spec = pl.BlockSpec((T, 128), lambda g, c: (ci(c), g))
    p_spec = pl.BlockSpec((1, 512), lambda g, c: (0, g))
    b_spec = pl.BlockSpec((None, 128, 512), lambda g, c: (g, 0, 0))
    c_spec = pl.BlockSpec((None, 512, 128), lambda g, c: (g, 0, 0))
    d_spec = pl.BlockSpec((1, 128), lambda g, c: (0, g))
    st_spec = pl.BlockSpec((None, 2, 512), lambda g, c: (ci(c), 0, g))
    return u_spec, p_spec, b_spec, c_spec, d_spec, st_spec


def s5_fwd(pm, are, aim, ldt, b_re, b_im, c_re, c_im, dv):
    S = pm.shape[0]
    nc = S // CHUNK
    u_spec, p_spec, b_spec, c_spec, d_spec, st_spec = _s5_specs(nc, False)

    def body(u_ref, are_ref, aim_ref, ldt_ref, bre_ref, bim_ref, cre_ref, cim_ref, dv_ref,
             y_ref, st_ref, state):
        c = pl.program_id(1)

        @pl.when(c == 0)
        def _():
            state[...] = jnp.zeros_like(state)

        st_ref[...] = state[...]
        y, e_re, e_im = _s5_chunk(u_ref[...], state[0:1, :], state[1:2, :], are_ref[...], aim_ref[...],
                                  ldt_ref[...], bre_ref[...], bim_ref[...], cre_ref[...], cim_ref[...],
                                  dv_ref[...])
        y_ref[...] = y
        state[0:1, :] = e_re
        state[1:2, :] = e_im

    return pl.pallas_call(
        body, name="s5_fwd", grid=(S5_GB, nc),
        in_specs=[u_spec, p_spec, p_spec, p_spec, b_spec, b_spec, c_spec, c_spec, d_spec],
        out_specs=[u_spec, st_spec],
        out_shape=[jax.ShapeDtypeStruct((S, S5_W), F32), jax.ShapeDtypeStruct((nc, 2, S5_G * S5_P), F32)],
        scratch_shapes=[pltpu.VMEM((2, 512), F32)],
        compiler_params=_cp("parallel", "arbitrary"),
    )(pm, are, aim, ldt, b_re, b_im, c_re, c_im, dv)


def s5_bwd(pm, st, dy, are, aim, ldt, b_re, b_im, c_re, c_im, dv):
    S = pm.shape[0]
    nc = S // CHUNK
    u_spec, p_spec, b_spec, c_spec, d_spec, st_spec = _s5_specs(nc, True)

    def body(u_ref, st_ref, dy_ref, are_ref, aim_ref, ldt_ref, bre_ref, bim_ref, cre_ref, cim_ref, dv_ref,
             du_ref, dare_ref, daim_ref, dldt_ref, dbre_ref, dbim_ref, dcre_ref, dcim_ref, ddv_ref, dstate):
        c = pl.program_id(1)

        @pl.when(c == 0)
        def _():
            dstate[...] = jnp.zeros_like(dstate)

        args = (u_ref[...], st_ref[0:1, :], st_ref[1:2, :], are_ref[...], aim_ref[...], ldt_ref[...],
                bre_ref[...], bim_ref[...], cre_ref[...], cim_ref[...], dv_ref[...])
        _, vjp = jax.vjp(_s5_chunk, *args)
        g = vjp((dy_ref[...], dstate[0:1, :], dstate[1:2, :]))
        du_ref[...] = g[0]
        dstate[0:1, :] = g[1]
        dstate[1:2, :] = g[2]
        accs = (dare_ref, daim_ref, dldt_ref, dbre_ref, dbim_ref, dcre_ref, dcim_ref, ddv_ref)
        for o, gv in zip(accs, g[3:]):
            @pl.when(c == 0)
            def _(o=o, gv=gv):
                o[...] = gv

            @pl.when(c > 0)
            def _(o=o, gv=gv):
                o[...] += gv

    n_state = S5_G * S5_P
    return pl.pallas_call(
        body, name="s5_bwd", grid=(S5_GB, nc),
        in_specs=[u_spec, st_spec, u_spec, p_spec, p_spec, p_spec, b_spec, b_spec, c_spec, c_spec, d_spec],
        out_specs=[u_spec, p_spec, p_spec, p_spec, b_spec, b_spec, c_spec, c_spec, d_spec],
        out_shape=[jax.ShapeDtypeStruct((S, S5_W), F32)] + [jax.ShapeDtypeStruct((1, n_state), F32)] * 3
        + [jax.ShapeDtypeStruct((S5_GB, 128, 512), F32)] * 2 + [jax.ShapeDtypeStruct((S5_GB, 512, 128), F32)] * 2
        + [jax.ShapeDtypeStruct((1, S5_W), F32)],
        scratch_shapes=[pltpu.VMEM((2, 512), F32)],
        compiler_params=_cp("parallel", "arbitrary"),
    )(pm, st, dy, are, aim, ldt, b_re, b_im, c_re, c_im, dv)


def conv_fwd(pm, w):
    S = pm.shape[0]

    def body(x_ref, w_ref, o_ref, pad):
        x = x_ref[...]
        pad[0:8, :] = jnp.zeros((8, 128), F32)
        pad[8:, :] = x
        y = (w_ref[3:4, :] * x + w_ref[2:3, :] * pad[7:7 + S, :] + w_ref[1:2, :] * pad[6:6 + S, :]
             + w_ref[0:1, :] * pad[5:5 + S, :])
        o_ref[...] = y * jax.nn.sigmoid(y)

    return pl.pallas_call(
        body, name="conv_fwd", grid=(12,),
        in_specs=[pl.BlockSpec((S, 128), lambda j: (0, 4 + j)), pl.BlockSpec((4, 128), lambda j: (0, j))],
        out_specs=pl.BlockSpec((S, 128), lambda j: (0, j)),
        out_shape=jax.ShapeDtypeStruct((S, 1536), F32),
        scratch_shapes=[pltpu.VMEM((S + 8, 128), F32)],
        compiler_params=_cp("parallel"),
    )(pm, w)


def conv_bwd(pm, w, dout):
    S = pm.shape[0]

    def body(x_ref, w_ref, do_ref, dx_ref, dw_ref, pad, dpad):
        x = x_ref[...]
        pad[0:8, :] = jnp.zeros((8, 128), F32)
        pad[8:, :] = x
        xs = [pad[5:5 + S, :], pad[6:6 + S, :], pad[7:7 + S, :], x]
        y = w_ref[0:1, :] * xs[0] + w_ref[1:2, :] * xs[1] + w_ref[2:3, :] * xs[2] + w_ref[3:4, :] * xs[3]
        sg = jax.nn.sigmoid(y)
        dy = do_ref[...] * (sg + y * sg * (1.0 - sg))
        dpad[0:S, :] = dy
        dpad[S:, :] = jnp.zeros((8, 128), F32)
        dx_ref[...] = (w_ref[3:4, :] * dy + w_ref[2:3, :] * dpad[1:1 + S, :] + w_ref[1:2, :] * dpad[2:2 + S, :]
                       + w_ref[0:1, :] * dpad[3:3 + S, :])
        for i in range(4):
            dw_ref[i:i + 1, :] = jnp.sum(dy * xs[i], axis=0, keepdims=True)

    return pl.pallas_call(
        body, name="conv_bwd", grid=(12,),
        in_specs=[pl.BlockSpec((S, 128), lambda j: (0, 4 + j)), pl.BlockSpec((4, 128), lambda j: (0, j)),
                  pl.BlockSpec((S, 128), lambda j: (0, j))],
        out_specs=[pl.BlockSpec((S, 128), lambda j: (0, j)), pl.BlockSpec((4, 128), lambda j: (0, j))],
        out_shape=[jax.ShapeDtypeStruct((S, 1536), F32), jax.ShapeDtypeStruct((4, 1536), F32)],
        scratch_shapes=[pltpu.VMEM((S + 8, 128), F32), pltpu.VMEM((S + 8, 128), F32)],
        compiler_params=_cp("parallel"),
    )(pm, w, dout)


def _gdn_chunk(q, k, v, gate, al, bl, alog, dtb, og, state):
    C = CHUNK
    r = lax.broadcasted_iota(jnp.int32, (C, C), 0)
    c = lax.broadcasted_iota(jnp.int32, (C, C), 1)
    eye = (r == c).astype(F32)
    strict, causal, upper = r > c, r >= c, r <= c
    qn = q * lax.rsqrt(jnp.sum(q * q, axis=-1, keepdims=True) + EPS) * (GDN_D ** -0.5)
    kn = k * lax.rsqrt(jnp.sum(k * k, axis=-1, keepdims=True) + EPS)
    beta = jax.nn.sigmoid(bl)
    g = -jnp.exp(alog) * _softplus(al + dtb)
    g_row = jnp.sum(eye * g, axis=0, keepdims=True)
    gc_col = jnp.sum(jnp.where(causal, g_row, 0.0), axis=1, keepdims=True)
    gc_row = jnp.sum(jnp.where(upper, g, 0.0), axis=0, keepdims=True)
    gtot = jnp.sum(g, axis=0, keepdims=True)
    gamma = jnp.exp(gc_col)
    diff = gc_col - gc_row
    d_strict = jnp.where(strict, jnp.exp(jnp.where(strict, diff, 0.0)), 0.0)
    d_causal = jnp.where(causal, jnp.exp(jnp.where(causal, diff, 0.0)), 0.0)
    a = beta * mm_nt(kn, kn) * d_strict
    p = -a
    x = eye + p
    for _ in range(5):
        p = mmf(p, p)
        x = x + mmf(x, p)
    u_new = mmf(x, beta * v)
    w_k = mmf(x, (beta * gamma) * kn)
    qk = mm_nt(qn, kn) * d_causal
    q_g = qn * gamma
    k_tail = kn * jnp.exp(gtot - gc_col)
    w = u_new - mm(w_k, state)
    o = mm(q_g, state) + mm(qk, w)
    new_state = jnp.exp(gtot) * state + mm_tn(k_tail, w)
    out = _rms(o, og) * (gate * jax.nn.sigmoid(gate))
    return out, new_state


def _gdn_specs(nc, rev):
    def ci(c):
        return nc - 1 - c if rev else c

    def blk(off):
        return pl.BlockSpec((CHUNK, 128), lambda h, c: (ci(c), off + h))

    col = lambda off: pl.BlockSpec((None, CHUNK, 1), lambda h, c: (off + h, ci(c), 0))
    sc = pl.BlockSpec((None, 1, 1), lambda h, c: (h, 0, 0))
    og = pl.BlockSpec((1, 128), lambda h, c: (0, 0))
    st = pl.BlockSpec((None, None, 128, 128), lambda h, c: (h, ci(c), 0, 0))
    return blk, col, sc, og, st


def gdn_fwd(qkvc, pm, abt, alog, dtb, og):
    S = qkvc.shape[0]
    nc = S // CHUNK
    blk, col, sc, ogs, st = _gdn_specs(nc, False)

    def body(q_ref, k_ref, v_ref, gate_ref, al_ref, bl_ref, alog_ref, dtb_ref, og_ref, o_ref, st_ref, state):
        c = pl.program_id(1)

        @pl.when(c == 0)
        def _():
            state[...] = jnp.zeros_like(state)

        st_ref[...] = state[...]
        out, new_state = _gdn_chunk(q_ref[...], k_ref[...], v_ref[...], gate_ref[...], al_ref[...], bl_ref[...],
                                    alog_ref[...], dtb_ref[...], og_ref[...], state[...])
        o_ref[...] = out
        state[...] = new_state

    return pl.pallas_call(
        body, name="gdn_fwd", grid=(GDN_H, nc),
        in_specs=[blk(0), blk(4), blk(8), blk(16), col(0), col(4), sc, sc, ogs],
        out_specs=[blk(0), st],
        out_shape=[jax.ShapeDtypeStruct((S, 512), F32), jax.ShapeDtypeStruct((GDN_H, nc, 128, 128), F32)],
        scratch_shapes=[pltpu.VMEM((128, 128), F32)],
        compiler_params=_cp("parallel", "arbitrary"),
    )(qkvc, qkvc, qkvc, pm, abt, abt, alog, dtb, og)


def gdn_bwd(qkvc, pm, abt, alog, dtb, og, st, dout):
    S = qkvc.shape[0]
    nc = S // CHUNK
    blk, col, sc, ogs, sts = _gdn_specs(nc, True)

    def body(q_ref, k_ref, v_ref, gate_ref, al_ref, bl_ref, alog_ref, dtb_ref, og_ref, st_ref, do_ref,
             dq_ref, dk_ref, dv_ref, dgate_ref, dal_ref, dbl_ref, dalog_ref, ddtb_ref, dog_ref, dstate):
        c = pl.program_id(1)

        @pl.when(c == 0)
        def _():
            dstate[...] = jnp.zeros_like(dstate)

        args = (q_ref[...], k_ref[...], v_ref[...], gate_ref[...], al_ref[...], bl_ref[...],
                alog_ref[...], dtb_ref[...], og_ref[...], st_ref[...])
        _, vjp = jax.vjp(_gdn_chunk, *args)
        g = vjp((do_ref[...], dstate[...]))
        for o, gv in zip((dq_ref, dk_ref, dv_ref, dgate_ref, dal_ref, dbl_ref), g[:6]):
            o[...] = gv
        dstate[...] = g[9]
        for o, gv in zip((dalog_ref, ddtb_ref, dog_ref), g[6:9]):
            @pl.when(c == 0)
            def _(o=o, gv=gv):
                o[...] = gv

            @pl.when(c > 0)
            def _(o=o, gv=gv):
                o[...] += gv

    colo = pl.BlockSpec((None, CHUNK, 1), lambda h, c: (h, nc - 1 - c, 0))
    ogo = pl.BlockSpec((None, 1, 128), lambda h, c: (h, 0, 0))
    sd = jax.ShapeDtypeStruct
    return pl.pallas_call(
        body, name="gdn_bwd", grid=(GDN_H, nc),
        in_specs=[blk(0), blk(4), blk(8), blk(16), col(0), col(4), sc, sc, ogs, sts, blk(0)],
        out_specs=[blk(0), blk(0), blk(0), blk(0), colo, colo, sc, sc, ogo],
        out_shape=[sd((S, 512), F32)] * 4 + [sd((GDN_H, S, 1), F32)] * 2 + [sd((GDN_H, 1, 1), F32)] * 2
        + [sd((GDN_H, 1, 128), F32)],
        scratch_shapes=[pltpu.VMEM((128, 128), F32)],
        compiler_params=_cp("parallel", "arbitrary"),
    )(qkvc, qkvc, qkvc, pm, abt, abt, alog, dtb, og, st, dout)


HG = 4


def _cattn_chunk(q, kb, vb, bias, valid):
    s = bmm_nt(q, kb) * (CA_D ** -0.5) + bias
    s = jnp.where(valid, s, -1e30)
    return bmm(_softmax(s), vb)


def _cattn_valid(c):
    pos = lax.broadcasted_iota(jnp.int32, (1, 1, CA_BAND), 2) + c * CHUNK
    return pos >= CA_PAD


def cattn_fwd(qn, kp, vp, bias):
    S = qn.shape[1]
    nc = S // CHUNK
    q_spec = pl.BlockSpec((HG, CHUNK, CA_D), lambda h, c: (h, c, 0))
    kv_spec = pl.BlockSpec((HG, S + CA_PAD, CA_D), lambda h, c: (h, 0, 0))
    b_spec = pl.BlockSpec((HG, CHUNK, CA_BAND), lambda h, c: (h, 0, 0))

    def body(q_ref, k_ref, v_ref, b_ref, o_ref):
        c = pl.program_id(1)
        start = pl.multiple_of(c * CHUNK, CHUNK)
        kb = k_ref[:, pl.ds(start, CA_BAND), :]
        vb = v_ref[:, pl.ds(start, CA_BAND), :]
        o_ref[...] = _cattn_chunk(q_ref[...], kb, vb, b_ref[...], _cattn_valid(c))

    return pl.pallas_call(
        body, name="cattn_fwd", grid=(CA_H // HG, nc), in_specs=[q_spec, kv_spec, kv_spec, b_spec],
        out_specs=q_spec, out_shape=jax.ShapeDtypeStruct((CA_H, S, CA_D), F32),
        compiler_params=_cp("parallel", "arbitrary"),
    )(qn, kp, vp, bias)


def cattn_bwd(qn, kp, vp, bias, do):
    S = qn.shape[1]
    nc = S // CHUNK
    q_spec = pl.BlockSpec((HG, CHUNK, CA_D), lambda h, c: (h, c, 0))
    kv_spec = pl.BlockSpec((HG, S + CA_PAD, CA_D), lambda h, c: (h, 0, 0))
    b_spec = pl.BlockSpec((HG, CHUNK, CA_BAND), lambda h, c: (h, 0, 0))

    def body(q_ref, k_ref, v_ref, b_ref, do_ref, dq_ref, dk_ref, dv_ref, db_ref):
        c = pl.program_id(1)

        @pl.when(c == 0)
        def _():
            dk_ref[...] = jnp.zeros_like(dk_ref)
            dv_ref[...] = jnp.zeros_like(dv_ref)
            db_ref[...] = jnp.zeros_like(db_ref)

        start = pl.multiple_of(c * CHUNK, CHUNK)
        kb = k_ref[:, pl.ds(start, CA_BAND), :].astype(F32)
        vb = v_ref[:, pl.ds(start, CA_BAND), :].astype(F32)
        valid = _cattn_valid(c)
        _, vjp = jax.vjp(lambda q, k, v, b: _cattn_chunk(q, k, v, b, valid), q_ref[...], kb, vb, b_ref[...])
        dq, dk, dv, db = vjp(do_ref[...])
        dq_ref[...] = dq
        dk_ref[:, pl.ds(start, CA_BAND), :] += dk
        dv_ref[:, pl.ds(start, CA_BAND), :] += dv
        db_ref[...] += db

    sd = jax.ShapeDtypeStruct
    return pl.pallas_call(
        body, name="cattn_bwd", grid=(CA_H // HG, nc), in_specs=[q_spec, kv_spec, kv_spec, b_spec, q_spec],
        out_specs=[q_spec, kv_spec, kv_spec, b_spec],
        out_shape=[sd((CA_H, S, CA_D), F32), sd((CA_H, S + CA_PAD, CA_D), F32), sd((CA_H, S + CA_PAD, CA_D), F32),
                   sd((CA_H, CHUNK, CA_BAND), F32)],
        compiler_params=_cp("parallel", "arbitrary"),
    )(qn, kp, vp, bias, do)


_REL_IDX = np.clip(np.arange(CHUNK)[:, None] - np.arange(CA_BAND)[None, :] + CA_PAD, -MAX_REL, MAX_REL) + MAX_REL
SKEW_W = CA_BAND + CHUNK


def rel_bias_grad(dbias):
    padded = jnp.pad(dbias, ((0, 0), (0, 0), (CHUNK, 0)))
    flat = jnp.pad(padded.reshape(CA_H, CHUNK * SKEW_W), ((0, 0), (0, CHUNK)))
    skew = flat.reshape(CA_H, CHUNK, SKEW_W + 1)

    first_near = SKEW_W - CHUNK - MAX_REL

    def fn(t):
        colsum = jnp.sum(t, axis=1, keepdims=True)
        j = lax.broadcasted_iota(jnp.int32, colsum.shape, 2)
        far = jnp.sum(jnp.where(j < first_near, colsum, 0.0), axis=2, keepdims=True)
        return (colsum + jnp.where(j == first_near, far, 0.0),)

    (colsum,) = whole("relbias_sum", fn, [skew], [((CA_H, 1, SKEW_W + 1), F32)])
    near = colsum[:, 0, first_near:SKEW_W][:, ::-1]
    return jnp.concatenate([jnp.zeros((CA_H, CHUNK + 1), F32), near], axis=1)


def rel_bias_expand(rb):
    near = rb[:, CHUNK + 1:][:, ::-1]
    far = jnp.broadcast_to(rb[:, 2 * MAX_REL:], (CA_H, SKEW_W - CHUNK - MAX_REL))
    t = jnp.concatenate([far, near, jnp.zeros((CA_H, 1), rb.dtype)], axis=1)
    rows = jnp.tile(t, (1, CHUNK))[:, :CHUNK * SKEW_W].reshape(CA_H, CHUNK, SKEW_W)
    return rows[:, :, CHUNK:]


def loss_head(y, target):
    S = y.shape[0]
    tile = min(ROW_TILE, S)

    def body(y_ref, t_ref, dy_ref, acc_ref):
        i = pl.program_id(0)
        e = y_ref[...] - t_ref[...]
        dy_ref[...] = e * (1.0 / D)
        part = jnp.sum(e * e, axis=0, keepdims=True) * (0.5 / D)

        @pl.when(i == 0)
        def _():
            acc_ref[...] = part

        @pl.when(i > 0)
        def _():
            acc_ref[...] += part

    row = pl.BlockSpec((tile, D), lambda i: (i, 0))
    return pl.pallas_call(
        body, name="loss_head", grid=(S // tile,), in_specs=[row, row],
        out_specs=[row, pl.BlockSpec((1, D), lambda i: (0, 0))],
        out_shape=[jax.ShapeDtypeStruct((S, D), F32), jax.ShapeDtypeStruct((1, D), F32)],
        compiler_params=_cp("arbitrary"),
    )(y, target)


ANY = pl.BlockSpec(memory_space=pl.ANY)


HBM = pl.BlockSpec(memory_space=pltpu.HBM)
SEM = pl.BlockSpec(memory_space=pltpu.SEMAPHORE)
EFFECT = pltpu.SideEffectType.DATAFLOW_SIDE_EFFECTING


def _chip_copies(src_ref, land_ref, send_sems, recv_sems, per_dest):
    x, y, c = lax.axis_index("x"), lax.axis_index("y"), lax.axis_index("c")
    me = 2 * x + y
    out = []
    for j, (px, py) in enumerate([(1 - x, y), (x, 1 - y), (1 - x, 1 - y)]):
        peer = 2 * px + py
        if per_dest:
            send = (src_ref.at[peer], land_ref.at[j])
            recv = (src_ref.at[me], land_ref.at[j])
        else:
            send = (src_ref, land_ref.at[me])
            recv = (src_ref, land_ref.at[peer])
        mk = lambda s, d, j=j, px=px, py=py: pltpu.make_async_remote_copy(
            src_ref=s, dst_ref=d, send_sem=send_sems.at[j], recv_sem=recv_sems.at[j],
            device_id=(px, py, c), device_id_type=MESH)
        out.append((mk(*send), mk(*recv)))
    return out


def exchange_start(src, land, carry, name, per_dest):
    def body(src_ref, land_ref, carry_ref, send_sems, recv_sems, src_out, land_out, carry_out):
        for send, _ in _chip_copies(src_ref, land_ref, send_sems, recv_sems, per_dest):
            send.start()

    hbm = lambda a: pltpu.HBM(a.shape, a.dtype)
    return pl.pallas_call(
        body, name=name,
        out_shape=(pltpu.SemaphoreType.DMA((3,)), pltpu.SemaphoreType.DMA((3,)), hbm(src), hbm(land), hbm(carry)),
        in_specs=(HBM, HBM, HBM), out_specs=(SEM, SEM, HBM, HBM, HBM),
        input_output_aliases={0: 2, 1: 3, 2: 4},
        compiler_params=pltpu.CompilerParams(has_side_effects=EFFECT),
    )(pltpu.with_memory_space_constraint(src, pltpu.HBM), pltpu.with_memory_space_constraint(land, pltpu.HBM),
      pltpu.with_memory_space_constraint(carry, pltpu.HBM))


def exchange_wait(src, land, send_sems, recv_sems, after, name, per_dest):
    def body(src_ref, land_ref, send_sems_ref, recv_sems_ref, after_ref, src_out, land_out):
        for send, recv in _chip_copies(src_ref, land_ref, send_sems_ref, recv_sems_ref, per_dest):
            send.wait_send()
            recv.wait_recv()

    hbm = lambda a: pltpu.HBM(a.shape, a.dtype)
    return pl.pallas_call(
        body, name=name, out_shape=(hbm(src), hbm(land)),
        in_specs=(HBM, HBM, SEM, SEM, ANY), out_specs=(HBM, HBM), input_output_aliases={0: 0, 1: 1},
        compiler_params=pltpu.CompilerParams(has_side_effects=EFFECT),
    )(src, land, send_sems, recv_sems, after)


def sibling_exchange(srcs, name):
    n = len(srcs)

    def body(*refs):
        src_refs, out_refs, send_sems, recv_sems = refs[:n], refs[n:2 * n], refs[2 * n], refs[2 * n + 1]
        x, y, c = lax.axis_index("x"), lax.axis_index("y"), lax.axis_index("c")
        copies = [pltpu.make_async_remote_copy(src_ref=s, dst_ref=o, send_sem=send_sems.at[k], recv_sem=recv_sems.at[k],
                                               device_id=(x, y, 1 - c), device_id_type=MESH)
                  for k, (s, o) in enumerate(zip(src_refs, out_refs))]
        for cp in copies:
            cp.start()
        for cp in copies:
            cp.wait()

    return pl.pallas_call(
        body, name=name, in_specs=[ANY] * n, out_specs=[ANY] * n,
        out_shape=[jax.ShapeDtypeStruct(s.shape, s.dtype) for s in srcs],
        scratch_shapes=[pltpu.SemaphoreType.DMA((n,)), pltpu.SemaphoreType.DMA((n,))],
    )(*srcs)


def all_exchange(src, name):
    def body(src_ref, out_ref, send_sems, recv_sems, local_sem):
        x, y, c = lax.axis_index("x"), lax.axis_index("y"), lax.axis_index("c")
        me = 4 * x + 2 * y + c
        local = pltpu.make_async_copy(src_ref, out_ref.at[me], local_sem)
        local.start()
        sends = []
        peers = []
        for k in range(1, N_DEV):
            bx, by, bc = (k >> 2) & 1, (k >> 1) & 1, k & 1
            px = 1 - x if bx else x
            py = 1 - y if by else y
            pc = 1 - c if bc else c
            peers.append((px, py, pc))
        for k, peer in enumerate(peers):
            cp = pltpu.make_async_remote_copy(src_ref=src_ref, dst_ref=out_ref.at[me], send_sem=send_sems.at[k],
                                              recv_sem=recv_sems.at[k], device_id=peer, device_id_type=MESH)
            cp.start()
            sends.append(cp)
        for k, (px, py, pc) in enumerate(peers):
            pltpu.make_async_remote_copy(src_ref=src_ref, dst_ref=out_ref.at[4 * px + 2 * py + pc],
                                         send_sem=send_sems.at[k], recv_sem=recv_sems.at[k],
                                         device_id=(px, py, pc), device_id_type=MESH).wait_recv()
        for cp in sends:
            cp.wait_send()
        local.wait()

    return pl.pallas_call(
        body, name=name, in_specs=[ANY], out_specs=ANY,
        out_shape=jax.ShapeDtypeStruct((N_DEV,) + tuple(src.shape), src.dtype),
        scratch_shapes=[pltpu.SemaphoreType.DMA((N_DEV - 1,)), pltpu.SemaphoreType.DMA((N_DEV - 1,)),
                        pltpu.SemaphoreType.DMA],
    )(src)


def sum_own_slabs(own, land, name, tile=512):
    R, C = own.shape
    n = land.shape[0]
    tile = _tile(R, tile, 16)

    def body(o_ref, t_ref, out_ref):
        acc = o_ref[...].astype(F32)
        for s in range(n):
            acc = acc + t_ref[s].astype(F32)
        out_ref[...] = acc

    return pl.pallas_call(
        body, name=name, grid=(R // tile,),
        in_specs=[pl.BlockSpec((tile, C), lambda i: (i, 0)), pl.BlockSpec((n, tile, C), lambda i: (0, i, 0))],
        out_specs=pl.BlockSpec((tile, C), lambda i: (i, 0)), out_shape=jax.ShapeDtypeStruct((R, C), F32),
        compiler_params=_cp("parallel"),
    )(own, land)


def sum_slabs(t, name, tile=512):
    n, R, C = t.shape
    tile = _tile(R, tile, 16)

    def body(t_ref, o_ref):
        acc = t_ref[0].astype(F32)
        for s in range(1, n):
            acc = acc + t_ref[s].astype(F32)
        o_ref[...] = acc

    return pl.pallas_call(
        body, name=name, grid=(R // tile,), in_specs=[pl.BlockSpec((n, tile, C), lambda i: (0, i, 0))],
        out_specs=pl.BlockSpec((tile, C), lambda i: (i, 0)), out_shape=jax.ShapeDtypeStruct((R, C), F32),
        compiler_params=_cp("parallel"),
    )(t)


PACK_ROW_MULT = 512


def _pad_rows(a, mult=16):
    r = (-a.shape[0]) % mult
    return jnp.pad(a, ((0, r), (0, 0))) if r else a


BIG = [
    ("ab_w_in", True, 2, 642), ("c_w_qkv", True, 2, 768), ("xa_w_kv", True, 4, 512),
    ("f_w_gate", True, 4, 704), ("f_w_up", True, 4, 704),
    ("ab_w_out", False, 2, 256), ("c_w_out", False, 2, 256), ("xa_w_q", False, 4, 256),
    ("xa_w_out", False, 4, 256), ("f_w_down", False, 4, 704), ("s5_w_glu", False, 2, 64),
]


GROUPS = ("B", "A")
GROUP_ROW_MULT = 64


def group_spec(layer, grp):
    i = layer // 2
    if grp == "A":
        return [("xa_w_kv", layer, True, 512), ("xa_w_q", layer, False, 256), ("xa_w_out", layer, False, 256),
                ("f_w_gate", layer, True, 704), ("f_w_up", layer, True, 704), ("f_w_down", layer, False, 704)]
    if layer % 2 == 0:
        return [("ab_w_in", i, True, 642), ("ab_w_out", i, False, 256), ("s5_w_glu", i, False, 64)]
    return [("c_w_qkv", i, True, 768), ("c_w_out", i, False, 256)]


def _seg_rows(rows):
    return rows + ((-rows) % 16)


def _f32_rows(a):
    bits = lax.bitcast_convert_type(a.reshape(-1), BF16).reshape(-1)
    return jnp.pad(bits, (0, 16 * D - bits.shape[0])).reshape(16, D)


def pack_group_shards(p, layer, grp):
    segs = []
    for name, idx, transposed, rows in group_spec(layer, grp):
        w = p[name][idx]
        if transposed:
            w = w.T
        segs.append(_pad_rows(w.astype(BF16).reshape(-1, D)))
    if grp == "B":
        small = p["gdn_conv_w"] if layer % 2 == 0 else p["c_norm_g"]
        segs.append(_f32_rows(small[layer // 2]))
    return _pad_rows(jnp.concatenate(segs, axis=0), GROUP_ROW_MULT)


def unpack_group_gathered(g, layer, grp):
    out, off = {}, 0
    for name, idx, transposed, rows in group_spec(layer, grp):
        seg = g[:, off:off + rows]
        if name == "s5_w_glu":
            out[name] = seg.reshape(N_CHIPS * 128, 512)
        else:
            out[name] = seg.reshape(N_CHIPS * rows, D)
        off += _seg_rows(rows)
    if grp == "B":
        n = 4 * 384 if layer % 2 == 0 else 256
        bits = g[:, off:off + 16].reshape(N_CHIPS, -1)[:, :2 * n].reshape(N_CHIPS, n, 2)
        small = lax.bitcast_convert_type(bits, F32)
        if layer % 2 == 0:
            out["gdn_conv_w"] = jnp.swapaxes(small.reshape(N_CHIPS, 4, 384), 0, 1).reshape(4, 1536)
        else:
            out["c_norm_g"] = small.reshape(D)
    return out


def pack_group_grads(gr, layer, grp):
    segs = []
    for name, idx, transposed, rows in group_spec(layer, grp):
        w = gr[name].astype(BF16)
        seg = w.reshape(N_CHIPS, rows, D)
        r = (-rows) % 16
        if r:
            seg = jnp.pad(seg, ((0, 0), (0, r), (0, 0)))
        segs.append(seg)
    out = jnp.concatenate(segs, axis=1)
    return jnp.pad(out, ((0, 0), (0, (-out.shape[1]) % GROUP_ROW_MULT), (0, 0)))


def unpack_group_reduced(g, layer, grp):
    out, off = {}, 0
    for name, idx, transposed, rows in group_spec(layer, grp):
        seg = g[off:off + rows]
        if name == "s5_w_glu":
            out[name] = seg.reshape(128, 512)
        else:
            out[name] = seg.T if transposed else seg
        off += _seg_rows(rows)
    return out


SMALL = ["ab_norm_g", "s5_a_re", "s5_a_im", "s5_log_dt", "s5_b_re", "s5_b_im", "s5_c_re", "s5_c_im", "s5_d",
         "s5_b_glu", "gdn_conv_w", "gdn_a_log", "gdn_dt_bias", "gdn_out_norm_g", "c_norm_g", "c_q_norm_g",
         "c_k_norm_g", "c_rel_bias", "mem_norm_g", "xa_norm_g", "xa_q_norm_g", "xa_k_norm_g", "f_norm_g"]


def pack_small(d, extra=None):
    parts = [d[n].reshape(-1).astype(F32) for n in SMALL]
    if extra is not None:
        parts.append(extra.reshape(-1))
    flat = jnp.concatenate(parts)
    flat = jnp.pad(flat, (0, (-flat.shape[0]) % (128 * 128)))
    return flat.reshape(-1, 128)


def unpack_small(flat2d, shapes):
    flat = flat2d.reshape(-1)
    out, off = {}, 0
    for n in SMALL:
        sz = int(np.prod(shapes[n]))
        out[n] = flat[off:off + sz].reshape(shapes[n])
        off += sz
    return out, flat[off:]


def _s5_blockdiag_b(b):
    bt = jnp.swapaxes(b, 1, 2).reshape(S5_GB, 8, S5_C, S5_P)
    eye = jnp.eye(8, dtype=b.dtype)
    return jnp.einsum("bgcp,gh->bgchp", bt, eye).reshape(S5_GB, 8 * S5_C, 8 * S5_P)


def _s5_blockdiag_c(c):
    ct = jnp.swapaxes(c, 1, 2).reshape(S5_GB, 8, S5_P, S5_C)
    eye = jnp.eye(8, dtype=c.dtype)
    return jnp.einsum("bgpc,gh->bgphc", ct, eye).reshape(S5_GB, 8 * S5_P, 8 * S5_C)


def _s5_diag_b(db):
    t = db.reshape(S5_GB, 8, S5_C, 8, S5_P)
    t = jnp.transpose(t, (0, 2, 4, 1, 3)).reshape(S5_GB, S5_C, S5_P, 64)
    d = t[..., ::9]
    return jnp.transpose(d, (0, 3, 2, 1)).reshape(S5_G, S5_P, S5_C)


def _s5_diag_c(dc):
    t = dc.reshape(S5_GB, 8, S5_P, 8, S5_C)
    t = jnp.transpose(t, (0, 2, 4, 1, 3)).reshape(S5_GB, S5_P, S5_C, 64)
    d = t[..., ::9]
    return jnp.transpose(d, (0, 3, 2, 1)).reshape(S5_G, S5_C, S5_P)


def _heads(t):
    return jnp.swapaxes(t.reshape(t.shape[0], CA_H, CA_D), 0, 1)


def _unheads(t):
    return jnp.swapaxes(t, 0, 1).reshape(t.shape[1], D)


def local_step(x, mem, target, p, wsrc, gsink):
    S = x.shape[0]
    row2 = lambda a: a.reshape(1, -1)
    saved = []
    (mem_n,) = row_fwd("mem_norm", f_norm, [mem], [row2(p["mem_norm_g"])], [(D, BF16)])
    gs = {}

    for layer in range(DEPTH):
        i = layer // 2
        w = wsrc(layer, "B", x)
        sv = {"x0": x, "wB": w}
        if layer % 2 == 0:
            (h,) = row_fwd("norm", f_norm, [x], [row2(p["ab_norm_g"][i])], [(D, BF16)])
            w_in = w["ab_w_in"]
            pm = matmul("nt", h, w_in[:2560], "proj_main")
            pab = matmul("nt", h, w_in[2560:], "proj_ab")
            s5p = dict(
                are=p["s5_a_re"][i].reshape(1, -1), aim=p["s5_a_im"][i].reshape(1, -1),
                ldt=jnp.broadcast_to(p["s5_log_dt"][i][:, None], (S5_G, S5_P)).reshape(1, -1),
                b_re=_s5_blockdiag_b(p["s5_b_re"][i]), b_im=_s5_blockdiag_b(p["s5_b_im"][i]),
                c_re=_s5_blockdiag_c(p["s5_c_re"][i]), c_im=_s5_blockdiag_c(p["s5_c_im"][i]),
                dv=p["s5_d"][i].reshape(1, -1))
            y5, st5 = s5_fwd(pm, **s5p)
            (a_out,) = row_fwd("glu", f_glu, [y5], [w["s5_w_glu"], row2(p["s5_b_glu"][i])], [(S5_W, F32)])
            conv_w = w["gdn_conv_w"]
            qkvc = conv_fwd(pm, conv_w)
            abt = jnp.swapaxes(pab, 0, 1)[:, :, None]
            alog = p["gdn_a_log"][i].reshape(GDN_H, 1, 1)
            dtb = p["gdn_dt_bias"][i].reshape(GDN_H, 1, 1)
            og = row2(p["gdn_out_norm_g"][i])
            b_out, stg = gdn_fwd(qkvc, pm, abt, alog, dtb, og)
            cat = jnp.concatenate([a_out, b_out], axis=1)
            x = matmul("nn", cat, w["ab_w_out"], "mix_out", add=x)
            sv.update(h=h, pm=pm, s5p=s5p, y5=y5, st5=st5, qkvc=qkvc, abt=abt, alog=alog, dtb=dtb, og=og,
                      stg=stg, cat=cat, conv_w=conv_w)
        else:
            (h,) = row_fwd("norm", f_norm, [x], [row2(w["c_norm_g"])], [(D, BF16)])
            qkv = matmul("nt", h, w["c_w_qkv"], "proj_qkv")
            q3, k3, v3 = _heads(qkv[:, :D]), _heads(qkv[:, D:2 * D]), _heads(qkv[:, 2 * D:])
            (qn,) = row_fwd("headnorm_q", f_headnorm, [q3], [row2(p["c_q_norm_g"][i])], [((CA_H, CA_D), F32)])
            (kn,) = row_fwd("headnorm_k", f_headnorm, [k3], [row2(p["c_k_norm_g"][i])], [((CA_H, CA_D), BF16)])
            kp = jnp.pad(kn, ((0, 0), (CA_PAD, 0), (0, 0)))
            vp = jnp.pad(v3.astype(BF16), ((0, 0), (CA_PAD, 0), (0, 0)))
            bias = rel_bias_expand(p["c_rel_bias"][i])
            o3 = cattn_fwd(qn, kp, vp, bias)
            o = _unheads(o3).astype(BF16)
            x = matmul("nn", o, w["c_w_out"], "mix_out", add=x)
            sv.update(h=h, q3=q3, k3=k3, qn=qn, kp=kp, vp=vp, bias=bias, o=o)
        sv["x1"] = x
        w = wsrc(layer, "A", x)
        sv["wA"] = w
        (hq,) = row_fwd("norm", f_norm, [x], [row2(p["xa_norm_g"][layer])], [(D, BF16)])
        qx = matmul("nn", hq, w["xa_w_q"], "xa_q")
        kv = matmul("nt", mem_n, w["xa_w_kv"], "xa_kv")
        xqg, xkg = row2(p["xa_q_norm_g"][layer]), row2(p["xa_k_norm_g"][layer])
        (ox,) = row_fwd("xattn", f_xattn, [qx], [kv, xqg, xkg], [(D, BF16)])
        x = matmul("nn", ox, w["xa_w_out"], "xa_out", add=x)
        sv.update(hq=hq, qx=qx, kv=kv, ox=ox)
        sv["x2"] = x
        (hf,) = row_fwd("norm", f_norm, [x], [row2(p["f_norm_g"][layer])], [(D, BF16)])
        w_gu = jnp.concatenate([w["f_w_gate"], w["f_w_up"]], axis=0)
        gu = matmul("nt", hf, w_gu, "ffn_gu")
        (act,) = row_fwd("swiglu", f_swiglu, [gu], [], [(FFN, BF16)])
        x = matmul("nn", act, w["f_w_down"], "ffn_down", add=x)
        sv.update(hf=hf, gu=gu, act=act, w_gu=w_gu)
        saved.append(sv)

    dx, loss_vec = loss_head(x, target)

    dmem_n = None
    for layer in reversed(range(DEPTH)):
        i = layer // 2
        sv = saved[layer]
        w, gw = sv["wA"], {}
        dact = matmul("nt", dx, w["f_w_down"], "ffn_dact")
        gw["f_w_down"] = matmul("tn", sv["act"], dx, "ffn_dwd", out_dtype=BF16)
        (dgu,) = row_bwd("swiglu_bwd", f_swiglu, [sv["gu"]], [], [dact], [0], [], row_dtypes=[BF16])
        dwgu = matmul("tn", dgu, sv["hf"], "ffn_dwgu", out_dtype=BF16)
        gw["f_w_gate"], gw["f_w_up"] = dwgu[:FFN], dwgu[FFN:]
        dh = matmul("nn", dgu, sv["w_gu"], "ffn_dh")
        dx, dg = row_bwd("norm_bwd", f_norm_res, [sv["x2"]], [row2(p["f_norm_g"][layer])], [dh, dx], [0], [0])
        gs.setdefault("f_norm_g", [None] * DEPTH)[layer] = dg[0]
        do = matmul("nt", dx, w["xa_w_out"], "xa_do")
        gw["xa_w_out"] = matmul("tn", sv["ox"], dx, "xa_dwo", out_dtype=BF16)
        xqg, xkg = row2(p["xa_q_norm_g"][layer]), row2(p["xa_k_norm_g"][layer])
        dqx, dkv, dqg, dkg = row_bwd("xattn_bwd", f_xattn, [sv["qx"]], [sv["kv"], xqg, xkg], [do],
                                     [0], [0, 1, 2])
        gs.setdefault("xa_q_norm_g", [None] * DEPTH)[layer] = dqg[0]
        gs.setdefault("xa_k_norm_g", [None] * DEPTH)[layer] = dkg[0]
        gw["xa_w_q"] = matmul("tn", sv["hq"], dqx, "xa_dwq", out_dtype=BF16)
        dhq = matmul("nt", dqx, w["xa_w_q"], "xa_dhq")
        gw["xa_w_kv"] = matmul("tn", dkv, mem_n, "xa_dwkv", out_dtype=BF16)
        dmem_n = matmul("nn", dkv, w["xa_w_kv"], "xa_dmem", add=dmem_n)
        dx, dg = row_bwd("norm_bwd", f_norm_res, [sv["x1"]], [row2(p["xa_norm_g"][layer])], [dhq, dx], [0], [0])
        gs.setdefault("xa_norm_g", [None] * DEPTH)[layer] = dg[0]
        dx = gsink(layer, "A", gw, dx)
        w, gw = sv["wB"], {}
        if layer % 2 == 0:
            dcat = matmul("nt", dx, w["ab_w_out"], "mix_dcat")
            gw["ab_w_out"] = matmul("tn", sv["cat"], dx, "mix_dwo", out_dtype=BF16)
            dy5, dwglu, dbglu = row_bwd("glu_bwd", f_glu, [sv["y5"]], [w["s5_w_glu"], row2(p["s5_b_glu"][i])],
                                        [(dcat, S5_W, 0)], [0], [0, 1])
            gw["s5_w_glu"] = dwglu
            gs.setdefault("s5_b_glu", [None] * 2)[i] = dbglu[0]
            s5p = sv["s5p"]
            du, dare, daim, dldt, dbre, dbim, dcre, dcim, ddv = s5_bwd(sv["pm"], sv["st5"], dy5, **s5p)
            (dldt_g,) = whole("s5_dt_sum", lambda t: (jnp.sum(t, axis=1, keepdims=True),),
                              [dldt.reshape(S5_G, S5_P)], [((S5_G, 1), F32)])
            for nme, val in (("s5_a_re", dare.reshape(S5_G, S5_P)), ("s5_a_im", daim.reshape(S5_G, S5_P)),
                             ("s5_log_dt", dldt_g[:, 0]), ("s5_b_re", _s5_diag_b(dbre)),
                             ("s5_b_im", _s5_diag_b(dbim)), ("s5_c_re", _s5_diag_c(dcre)),
                             ("s5_c_im", _s5_diag_c(dcim)), ("s5_d", ddv.reshape(S5_G, S5_C))):
                gs.setdefault(nme, [None] * 2)[i] = val
            dq, dk, dv, dgate, dal, dbl, dalog, ddtb, dog = gdn_bwd(
                sv["qkvc"], sv["pm"], sv["abt"], sv["alog"], sv["dtb"], sv["og"], sv["stg"], dcat[:, S5_W:])
            (dog_s,) = whole("gdn_og_sum", lambda t: (jnp.sum(t, axis=0, keepdims=True),),
                             [dog.reshape(GDN_H, GDN_D)], [((1, GDN_D), F32)])
            gs.setdefault("gdn_out_norm_g", [None] * 2)[i] = dog_s[0]
            gs.setdefault("gdn_a_log", [None] * 2)[i] = dalog.reshape(GDN_H)
            gs.setdefault("gdn_dt_bias", [None] * 2)[i] = ddtb.reshape(GDN_H)
            dqkvc = jnp.concatenate([dq, dk, dv], axis=1)
            dqkv, dconv = conv_bwd(sv["pm"], sv["conv_w"], dqkvc)
            gs.setdefault("gdn_conv_w", [None] * 2)[i] = dconv
            dpm = jnp.concatenate([du, dqkv, dgate], axis=1).astype(BF16)
            dpab = jnp.swapaxes(jnp.concatenate([dal, dbl], axis=0)[:, :, 0], 0, 1)
            dw_main = matmul("tn", dpm, sv["h"], "proj_dw", out_dtype=BF16)
            dw_ab = matmul("tn", dpab, sv["h"], "proj_ab_dw", out_dtype=BF16)
            gw["ab_w_in"] = jnp.concatenate([dw_main, dw_ab], axis=0)
            w_in = w["ab_w_in"]
            dh = matmul("nn", dpm, w_in[:2560], "proj_dh")
            dh = matmul("nn", dpab, w_in[2560:], "proj_ab_dh", add=dh)
            dx, dg = row_bwd("norm_bwd", f_norm_res, [sv["x0"]], [row2(p["ab_norm_g"][i])], [dh, dx], [0], [0])
            gs.setdefault("ab_norm_g", [None] * 2)[i] = dg[0]
        else:
            do = matmul("nt", dx, w["c_w_out"], "mix_dcat")
            gw["c_w_out"] = matmul("tn", sv["o"], dx, "mix_dwo", out_dtype=BF16)
            dqn, dkp, dvp, dbias = cattn_bwd(sv["qn"], sv["kp"], sv["vp"], sv["bias"], _heads(do))
            gs.setdefault("c_rel_bias", [None] * 2)[i] = rel_bias_grad(dbias)
            dq3, dqg = row_bwd("headnorm_bwd", f_headnorm, [sv["q3"]], [row2(p["c_q_norm_g"][i])], [dqn], [0], [0])
            dk3, dkg = row_bwd("headnorm_bwd", f_headnorm, [sv["k3"]], [row2(p["c_k_norm_g"][i])],
                               [dkp[:, CA_PAD:]], [0], [0])
            gs.setdefault("c_q_norm_g", [None] * 2)[i] = dqg[0]
            gs.setdefault("c_k_norm_g", [None] * 2)[i] = dkg[0]
            dqkv = jnp.concatenate([_unheads(dq3), _unheads(dk3), _unheads(dvp[:, CA_PAD:])], axis=1).astype(BF16)
            gw["c_w_qkv"] = matmul("tn", dqkv, sv["h"], "proj_qkv_dw", out_dtype=BF16)
            dh = matmul("nn", dqkv, w["c_w_qkv"], "proj_qkv_dh")
            dx, dg = row_bwd("norm_bwd", f_norm_res, [sv["x0"]], [row2(w["c_norm_g"])], [dh, dx], [0], [0])
            gs.setdefault("c_norm_g", [None] * 2)[i] = dg[0]
        dx = gsink(layer, "B", gw, dx)
    (dmg,) = row_bwd("mem_norm_bwd", f_norm, [mem], [row2(p["mem_norm_g"])], [dmem_n], [], [0])
    small = {n: jnp.stack(v) for n, v in gs.items()}
    small["mem_norm_g"] = dmg[0]
    return loss_vec, dx, small


def adam(w, g, m, v, name):
    shape = w.shape
    cols = shape[-1]
    w2, g2, m2, v2 = (t.reshape(-1, cols) for t in (w, g, m, v))
    rows = w2.shape[0]
    tile = rows if rows <= 512 else _tile(rows, 512, 8)
    outs = row_fwd(name, f_adam, [w2, g2, m2, v2], [], [(cols, F32)] * 3, tile=tile)
    return tuple(o.reshape(shape) for o in outs)


WEIGHTS = ['ab_norm_g', 'ab_w_in', 'ab_w_out', 's5_a_re', 's5_a_im', 's5_log_dt', 's5_b_re', 's5_b_im', 's5_c_re',
           's5_c_im', 's5_d', 's5_w_glu', 's5_b_glu', 'gdn_conv_w', 'gdn_a_log', 'gdn_dt_bias', 'gdn_out_norm_g',
           'c_norm_g', 'c_w_qkv', 'c_w_out', 'c_q_norm_g', 'c_k_norm_g', 'c_rel_bias', 'mem_norm_g', 'xa_norm_g',
           'xa_w_q', 'xa_w_kv', 'xa_w_out', 'xa_q_norm_g', 'xa_k_norm_g', 'f_norm_g', 'f_w_gate', 'f_w_up',
           'f_w_down']
SHARDED_SMALL = {"gdn_conv_w": (2, 384), "c_norm_g": (1, 256)}


def kernel(x, mem, ab_norm_g, ab_w_in, ab_w_out, s5_a_re, s5_a_im, s5_log_dt, s5_b_re, s5_b_im, s5_c_re, s5_c_im, s5_d, s5_w_glu, s5_b_glu, gdn_conv_w, gdn_a_log, gdn_dt_bias, gdn_out_norm_g, c_norm_g, c_w_qkv, c_w_out, c_q_norm_g, c_k_norm_g, c_rel_bias, mem_norm_g, xa_norm_g, xa_w_q, xa_w_kv, xa_w_out, xa_q_norm_g, xa_k_norm_g, f_norm_g, f_w_gate, f_w_up, f_w_down, loss_target, m_ab_norm_g, m_ab_w_in, m_ab_w_out, m_s5_a_re, m_s5_a_im, m_s5_log_dt, m_s5_b_re, m_s5_b_im, m_s5_c_re, m_s5_c_im, m_s5_d, m_s5_w_glu, m_s5_b_glu, m_gdn_conv_w, m_gdn_a_log, m_gdn_dt_bias, m_gdn_out_norm_g, m_c_norm_g, m_c_w_qkv, m_c_w_out, m_c_q_norm_g, m_c_k_norm_g, m_c_rel_bias, m_mem_norm_g, m_xa_norm_g, m_xa_w_q, m_xa_w_kv, m_xa_w_out, m_xa_q_norm_g, m_xa_k_norm_g, m_f_norm_g, m_f_w_gate, m_f_w_up, m_f_w_down, v_ab_norm_g, v_ab_w_in, v_ab_w_out, v_s5_a_re, v_s5_a_im, v_s5_log_dt, v_s5_b_re, v_s5_b_im, v_s5_c_re, v_s5_c_im, v_s5_d, v_s5_w_glu, v_s5_b_glu, v_gdn_conv_w, v_gdn_a_log, v_gdn_dt_bias, v_gdn_out_norm_g, v_c_norm_g, v_c_w_qkv, v_c_w_out, v_c_q_norm_g, v_c_k_norm_g, v_c_rel_bias, v_mem_norm_g, v_xa_norm_g, v_xa_w_q, v_xa_w_kv, v_xa_w_out, v_xa_q_norm_g, v_xa_k_norm_g, v_f_norm_g, v_f_w_gate, v_f_w_up, v_f_w_down):
    args = locals()
    p = {n: args[n] for n in WEIGHTS}
    m = {n: args["m_" + n] for n in WEIGHTS}
    v = {n: args["v_" + n] for n in WEIGHTS}
    chip = 2 * lax.axis_index("x") + lax.axis_index("y")

    carry = x[0]
    gathers = {}
    for layer in range(DEPTH):
        for grp in GROUPS:
            src = pack_group_shards(p, layer, grp)
            land = lax.dynamic_update_slice(lax.empty((N_CHIPS,) + src.shape, BF16), src[None], (chip, 0, 0))
            send_sems, recv_sems, src, land, carry = exchange_start(
                src, land, carry, f"gather_start_{layer}{grp}", per_dest=False)
            gathers[layer, grp] = (src, land, send_sems, recv_sems)

    def wsrc(layer, grp, after):
        src, land, send_sems, recv_sems = gathers[layer, grp]
        _, land = exchange_wait(src, land, send_sems, recv_sems, after, f"gather_wait_{layer}{grp}", per_dest=False)
        return unpack_group_gathered(land, layer, grp)

    scatters = []

    def gsink(layer, grp, gw, carry):
        src = pack_group_grads(gw, layer, grp)
        land = lax.empty((3,) + src.shape[1:], BF16)
        send_sems, recv_sems, src, land, carry = exchange_start(
            src, land, carry, f"scatter_start_{layer}{grp}", per_dest=True)
        scatters.append((layer, grp, src, land, send_sems, recv_sems))
        return carry

    loss_vec, dx, g_small = local_step(carry, mem[0], loss_target[0], p, wsrc, gsink)

    parts = []
    for layer, grp, src, land, send_sems, recv_sems in scatters:
        src, land = exchange_wait(src, land, send_sems, recv_sems, dx, f"scatter_wait_{layer}{grp}", per_dest=True)
        own = lax.dynamic_index_in_dim(src, chip, axis=0, keepdims=False)
        parts.append(sum_own_slabs(own, land, "sum_chips"))
    others = sibling_exchange(parts, "sibling_grads")
    per_layer = {}
    for (layer, grp, *_), part, other in zip(scatters, parts, others):
        (total,) = row_fwd("sum_cores", lambda a, b: (a + b,), [part, other], [], [(D, F32)],
                           tile=_tile(part.shape[0], 512, 16))
        for name, g in unpack_group_reduced(total, layer, grp).items():
            per_layer.setdefault(name, {})[layer] = g
    grads = {name: jnp.stack([d[k] for k in sorted(d)]) for name, d in per_layer.items()}

    full_shapes = {n: ((2, 4, 1536) if n == "gdn_conv_w" else (2, D) if n == "c_norm_g" else p[n].shape)
                   for n in SMALL}
    small_all = all_exchange(pack_small(g_small, extra=loss_vec), "gather_small")
    small_sum = sum_slabs(small_all, "sum_small")
    g_s, rest = unpack_small(small_sum, full_shapes)
    (loss11,) = whole("loss_sum", lambda t: (jnp.sum(t, axis=1, keepdims=True),), [rest[:D].reshape(1, D)],
                      [((1, 1), F32)])
    for n, (axis, width) in SHARDED_SMALL.items():
        g_s[n] = lax.dynamic_slice_in_dim(g_s[n], chip * width, width, axis=axis)
    grads.update(g_s)

    shard_shapes = {n: p[n].shape for n in SMALL}
    packs = [pack_small(d) for d in (p, grads, m, v)]
    d_s, m_s, v_s = adam(*packs, name="adam_small")
    delta, new_m, new_v = {}, {}, {}
    for dst, src in ((delta, d_s), (new_m, m_s), (new_v, v_s)):
        dst.update(unpack_small(src, shard_shapes)[0])
    for name, _, _, _ in BIG:
        delta[name], new_m[name], new_v[name] = adam(p[name], grads[name], m[name], v[name], "adam_" + name)

    return (loss11[0, 0], dx[None], *[grads[n] for n in WEIGHTS], *[delta[n] for n in WEIGHTS],
            *[new_m[n] for n in WEIGHTS], *[new_v[n] for n in WEIGHTS])
```

```python
import functools
import math

import numpy as np
import jax
import jax.numpy as jnp
from jax import lax
from jax.experimental import pallas as pl
from jax.experimental.pallas import tpu as pltpu

F32 = jnp.float32
BF16 = jnp.bfloat16
MESH = pl.DeviceIdType.MESH

D = 1024
CHUNK = 64
N_MEM = 256
EPS = 1e-6
S5_W = 512
S5_G = 32
S5_C = 16
S5_P = 64
S5_GB = 4
GDN_H = 4
GDN_D = 128
CA_H = 16
CA_D = 64
CA_LEFT = 8
CA_BAND = (CA_LEFT + 1) * CHUNK
CA_PAD = CA_LEFT * CHUNK
MAX_REL = 128
XA_H = 4
XA_D = 256
FFN = 2816
DEPTH = 4
N_CHIPS = 4
N_DEV = 8
LR, B1, B2, AEPS, WD, STEP = 0.001, 0.9, 0.999, 1e-08, 0.01, 10

VMEM_LIMIT = 56 * 1024 * 1024
ROW_TILE = 256
HI = lax.Precision.HIGHEST


def _cp(*sem):
    return pltpu.CompilerParams(dimension_semantics=sem, vmem_limit_bytes=VMEM_LIMIT)


def _dg(a, b, ca, cb):
    return lax.dot_general(a.astype(BF16), b.astype(BF16), (((ca,), (cb,)), ((), ())),
                           preferred_element_type=F32)


@jax.custom_vjp
def mm(a, b):
    return _dg(a, b, 1, 0)


@jax.custom_vjp
def mm_nt(a, b):
    return _dg(a, b, 1, 1)


@jax.custom_vjp
def mm_tn(a, b):
    return _dg(a, b, 0, 0)


mm.defvjp(lambda a, b: (mm(a, b), (a, b)), lambda r, g: (mm_nt(g, r[1]), mm_tn(r[0], g)))
mm_nt.defvjp(lambda a, b: (mm_nt(a, b), (a, b)), lambda r, g: (mm(g, r[1]), mm_tn(g, r[0])))
mm_tn.defvjp(lambda a, b: (mm_tn(a, b), (a, b)), lambda r, g: (mm_nt(r[1], g), mm(r[0], g)))


def _bdg(a, b, ca, cb):
    return lax.dot_general(a.astype(BF16), b.astype(BF16), (((ca,), (cb,)), ((0,), (0,))),
                           preferred_element_type=F32)


@jax.custom_vjp
def bmm(a, b):
    return _bdg(a, b, 2, 1)


@jax.custom_vjp
def bmm_nt(a, b):
    return _bdg(a, b, 2, 2)


@jax.custom_vjp
def bmm_tn(a, b):
    return _bdg(a, b, 1, 1)


bmm.defvjp(lambda a, b: (bmm(a, b), (a, b)), lambda r, g: (bmm_nt(g, r[1]), bmm_tn(r[0], g)))
bmm_nt.defvjp(lambda a, b: (bmm_nt(a, b), (a, b)), lambda r, g: (bmm(g, r[1]), bmm_tn(g, r[0])))
bmm_tn.defvjp(lambda a, b: (bmm_tn(a, b), (a, b)), lambda r, g: (bmm_nt(r[1], g), bmm(r[0], g)))


def _split(a):
    hi = a.astype(BF16)
    return hi, (a - hi.astype(F32)).astype(BF16)


def _dg3(a, b, ca, cb):
    (ah, al), (bh, bl) = _split(a), _split(b)
    d = lambda u, v: lax.dot_general(u, v, (((ca,), (cb,)), ((), ())), preferred_element_type=F32)
    return d(ah, bh) + (d(ah, bl) + d(al, bh))


@jax.custom_vjp
def mm3(a, b):
    return _dg3(a, b, 1, 0)


@jax.custom_vjp
def mm3_nt(a, b):
    return _dg3(a, b, 1, 1)


@jax.custom_vjp
def mm3_tn(a, b):
    return _dg3(a, b, 0, 0)


mm3.defvjp(lambda a, b: (mm3(a, b), (a, b)), lambda r, g: (mm3_nt(g, r[1]), mm3_tn(r[0], g)))
mm3_nt.defvjp(lambda a, b: (mm3_nt(a, b), (a, b)), lambda r, g: (mm3(g, r[1]), mm3_tn(g, r[0])))
mm3_tn.defvjp(lambda a, b: (mm3_tn(a, b), (a, b)), lambda r, g: (mm3_nt(r[1], g), mm3(r[0], g)))


def _tri_mm(v, upper):
    T = v.shape[0]
    r = lax.broadcasted_iota(jnp.int32, (T, T), 0)
    c = lax.broadcasted_iota(jnp.int32, (T, T), 1)
    m = ((c >= r) if upper else (r >= c)).astype(BF16)
    hi, lo = _split(v)
    d = lambda u: lax.dot_general(m, u, (((1,), (0,)), ((), ())), preferred_element_type=F32)
    return d(hi) + d(lo)


@jax.custom_vjp
def cumsum_rows(v):
    return _tri_mm(v, False)


cumsum_rows.defvjp(lambda v: (_tri_mm(v, False), None), lambda _, g: (_tri_mm(g, True),))


def _rms(x, g):
    return x * lax.rsqrt(jnp.mean(x * x, axis=-1, keepdims=True) + EPS) * g


def _softmax(s):
    e = jnp.exp(s - lax.stop_gradient(jnp.max(s, axis=-1, keepdims=True)))
    return e / jnp.sum(e, axis=-1, keepdims=True)


def _softplus(x):
    return jnp.maximum(x, 0.0) + jnp.log(1.0 + jnp.exp(-jnp.abs(x)))


def _tile(n, cap, align):
    if n <= cap:
        return n
    best = None
    for d in range(align, cap + 1, align):
        if n % d == 0:
            best = d
    assert best is not None, (n, cap, align)
    return best


def matmul(mode, a, b, name, out_dtype=F32, add=None):
    if mode == "nn":
        (M, K), (K2, N) = a.shape, b.shape
    elif mode == "nt":
        (M, K), (N, K2) = a.shape, b.shape
    else:
        (K, M), (K2, N) = a.shape, b.shape
    assert K == K2, (mode, a.shape, b.shape)
    tm, tn, tk = _tile(M, 512, 128), _tile(N, 512, 128), _tile(K, 2048, 128)
    nk = K // tk
    if mode == "nn":
        a_spec = pl.BlockSpec((tm, tk), lambda i, j, k: (i, k))
        b_spec = pl.BlockSpec((tk, tn), lambda i, j, k: (k, j))
        dn = (((1,), (0,)), ((), ()))
    elif mode == "nt":
        a_spec = pl.BlockSpec((tm, tk), lambda i, j, k: (i, k))
        b_spec = pl.BlockSpec((tn, tk), lambda i, j, k: (j, k))
        dn = (((1,), (1,)), ((), ()))
    else:
        a_spec = pl.BlockSpec((tk, tm), lambda i, j, k: (k, i))
        b_spec = pl.BlockSpec((tk, tn), lambda i, j, k: (k, j))
        dn = (((0,), (0,)), ((), ()))
    o_spec = pl.BlockSpec((tm, tn), lambda i, j, k: (i, j))
    has_add = add is not None

    def body(*refs):
        a_ref, b_ref = refs[0], refs[1]
        add_ref = refs[2] if has_add else None
        o_ref = refs[3] if has_add else refs[2]
        acc_ref = refs[-1]
        p = lax.dot_general(a_ref[...].astype(BF16), b_ref[...].astype(BF16), dn,
                            preferred_element_type=F32)

        def finish(total):
            if has_add:
                total = total + add_ref[...]
            o_ref[...] = total.astype(o_ref.dtype)

        if nk == 1:
            finish(p)
        else:
            k = pl.program_id(2)

            @pl.when(k == 0)
            def _():
                acc_ref[...] = p

            @pl.when(k > 0)
            def _():
                acc_ref[...] += p

            @pl.when(k == nk - 1)
            def _():
                finish(acc_ref[...])

    ins = [a, b] + ([add] if has_add else [])
    specs = [a_spec, b_spec] + ([o_spec] if has_add else [])
    return pl.pallas_call(
        body, name=name, grid=(M // tm, N // tn, nk), in_specs=specs, out_specs=o_spec,
        out_shape=jax.ShapeDtypeStruct((M, N), out_dtype),
        scratch_shapes=[pltpu.VMEM((tm, tn), F32)],
        compiler_params=_cp("parallel", "parallel", "arbitrary"),
    )(*ins)


def _row_spec(arr, tile):
    if isinstance(arr, tuple):
        a, w, cb = arr
        return a, pl.BlockSpec((tile, w), lambda i, cb=cb: (i, cb)), (tile, w)
    if arr.ndim == 3:
        d0, _, d2 = arr.shape
        return arr, pl.BlockSpec((d0, tile, d2), lambda i: (0, i, 0)), (d0, tile, d2)
    return arr, pl.BlockSpec((tile, arr.shape[1]), lambda i: (i, 0)), (tile, arr.shape[1])


def _full_spec(arr):
    nd = arr.ndim
    return pl.BlockSpec(arr.shape, lambda i, nd=nd: (0,) * nd)


def _n_rows(arr):
    a = arr[0] if isinstance(arr, tuple) else arr
    return a.shape[1] if a.ndim == 3 else a.shape[0]


def _row_out_shape(shape_tail, n, dtype):
    if isinstance(shape_tail, tuple):
        d0, d2 = shape_tail
        return (jax.ShapeDtypeStruct((d0, n, d2), dtype),
                lambda tile: pl.BlockSpec((d0, tile, d2), lambda i: (0, i, 0)))
    return (jax.ShapeDtypeStruct((n, shape_tail), dtype),
            lambda tile: pl.BlockSpec((tile, shape_tail), lambda i: (i, 0)))


def _f32(v):
    return v.astype(F32) if v.dtype == BF16 else v


def row_fwd(name, fn, rows, fulls, outs, tile=ROW_TILE):
    n = _n_rows(rows[0])
    tile = min(tile, n)
    assert n % tile == 0, (name, n, tile)
    rs = [_row_spec(r, tile) for r in rows]
    os_ = [_row_out_shape(w, n, dt) for w, dt in outs]
    nr, nf = len(rows), len(fulls)

    def body(*refs):
        vals = [_f32(r[...]) for r in refs[:nr + nf]]
        res = fn(*vals)
        for r, v in zip(refs[nr + nf:], res):
            r[...] = v.astype(r.dtype)

    out = pl.pallas_call(
        body, name=name, grid=(n // tile,),
        in_specs=[s for _, s, _ in rs] + [_full_spec(f) for f in fulls],
        out_specs=[mk(tile) for _, mk in os_], out_shape=[sh for sh, _ in os_],
        compiler_params=_cp("parallel"),
    )(*[a for a, _, _ in rs], *fulls)
    return out


def row_bwd(name, fn, rows, fulls, cts, want_rows, want_fulls, row_dtypes=None, tile=ROW_TILE):
    n = _n_rows(rows[0])
    tile = min(tile, n)
    assert n % tile == 0, (name, n, tile)
    rs = [_row_spec(r, tile) for r in rows]
    cs = [_row_spec(c, tile) for c in cts]
    nr, nf, nc = len(rows), len(fulls), len(cts)
    row_dtypes = row_dtypes or [F32] * len(want_rows)
    out_shapes, out_specs = [], []
    for k, idx in enumerate(want_rows):
        a, _, blk = rs[idx]
        if len(blk) == 3:
            sh, mk = _row_out_shape((blk[0], blk[2]), n, row_dtypes[k])
        else:
            sh, mk = _row_out_shape(blk[1], n, row_dtypes[k])
        out_shapes.append(sh)
        out_specs.append(mk(tile))
    for idx in want_fulls:
        out_shapes.append(jax.ShapeDtypeStruct(fulls[idx].shape, F32))
        out_specs.append(_full_spec(fulls[idx]))
    n_wr = len(want_rows)

    def body(*refs):
        i = pl.program_id(0)
        vals = [_f32(r[...]) for r in refs[:nr + nf]]
        ct_vals = [_f32(r[...]) for r in refs[nr + nf:nr + nf + nc]]
        outs = refs[nr + nf + nc:]
        _, vjp = jax.vjp(fn, *vals)
        grads = vjp(tuple(ct_vals))
        for k, idx in enumerate(want_rows):
            outs[k][...] = grads[idx].astype(outs[k].dtype)
        for k, idx in enumerate(want_fulls):
            o = outs[n_wr + k]
            g = grads[nr + idx]

            @pl.when(i == 0)
            def _(o=o, g=g):
                o[...] = g

            @pl.when(i > 0)
            def _(o=o, g=g):
                o[...] += g

    out = pl.pallas_call(
        body, name=name, grid=(n // tile,),
        in_specs=[s for _, s, _ in rs] + [_full_spec(f) for f in fulls] + [s for _, s, _ in cs],
        out_specs=out_specs, out_shape=out_shapes,
        compiler_params=_cp("arbitrary"),
    )(*[a for a, _, _ in rs], *fulls, *[a for a, _, _ in cs])
    return out


def whole(name, fn, args, outs):
    def body(*refs):
        res = fn(*[r[...] for r in refs[:len(args)]])
        for r, v in zip(refs[len(args):], res):
            r[...] = v.astype(r.dtype)

    return pl.pallas_call(
        body, name=name, out_shape=[jax.ShapeDtypeStruct(s, d) for s, d in outs],
        compiler_params=pltpu.CompilerParams(vmem_limit_bytes=VMEM_LIMIT),
    )(*args)


def f_norm(x, g):
    return (_rms(x, g),)


def f_norm_res(x, g):
    return _rms(x, g), x


def f_swiglu(gu):
    g, u = gu[:, :FFN], gu[:, FFN:]
    return (g * jax.nn.sigmoid(g) * u,)


def f_glu(y, w, b):
    h = jax.nn.gelu(y)
    return (h * jax.nn.sigmoid(mm(h, w) + b),)


def f_xattn(q, kv, qg, kg):
    outs = []
    for h in range(XA_H):
        sl = slice(h * XA_D, (h + 1) * XA_D)
        qn = _rms(q[:, sl], qg)
        kn = _rms(kv[:, sl], kg)
        vh = kv[:, D + h * XA_D:D + (h + 1) * XA_D]
        p = _softmax(mm_nt(qn, kn) * (XA_D ** -0.5))
        outs.append(mm(p, vh))
    return (jnp.concatenate(outs, axis=-1),)


def f_headnorm(t, g):
    return (_rms(t, g),)


def f_adam(w, g, m, v):
    m2 = B1 * m + (1.0 - B1) * g
    v2 = B2 * v + (1.0 - B2) * (g * g)
    m_hat = m2 / (1.0 - B1 ** STEP)
    v_hat = v2 / (1.0 - B2 ** STEP)
    delta = -LR * (m_hat / (jnp.sqrt(v_hat) + AEPS) + WD * w)
    return delta, m2, v2


S5_NTAB, S5_NROW = 4, 6


def _s5_tables(are, aim, ldt):
    T = CHUNK
    dt = jnp.exp(ldt)
    ar, ai = are * dt, aim * dt
    t = lax.broadcasted_iota(jnp.int32, (T, 1), 0).astype(F32)
    mag, inv = jnp.exp(t * ar), jnp.exp(-t * ar)
    cs, sn = jnp.cos(t * ai), jnp.sin(t * ai)
    e_re, e_im = mag * cs, mag * sn
    n_re, n_im = inv * cs, -inv * sn
    l_re, l_im = jnp.exp(ar) * jnp.cos(ai), jnp.exp(ar) * jnp.sin(ai)
    den = are * are + aim * aim
    k_re = ((l_re - 1.0) * are + l_im * aim) / den
    k_im = (l_im * are - (l_re - 1.0) * aim) / den
    tl = float(T - 1)
    m_re, m_im = jnp.exp(tl * ar) * jnp.cos(tl * ai), jnp.exp(tl * ar) * jnp.sin(tl * ai)
    return (e_re, e_im, n_re, n_im), (l_re, l_im, k_re, k_im, m_re, m_im)


def _s5_chunk(u, sre, sim, tabs, rows, b_re, b_im, c_re, c_im, dv):
    e_re, e_im, n_re, n_im = tabs
    l_re, l_im, k_re, k_im, m_re, m_im = rows
    x_re, x_im = mm(u, b_re), mm(u, b_im)
    bu_re = k_re * x_re - k_im * x_im
    bu_im = k_re * x_im + k_im * x_re
    v_re = bu_re * n_re - bu_im * n_im
    v_im = bu_re * n_im + bu_im * n_re
    p_re = l_re * sre - l_im * sim
    p_im = l_re * sim + l_im * sre
    w_re = cumsum_rows(v_re) + p_re
    w_im = cumsum_rows(v_im) + p_im
    s_re = e_re * w_re - e_im * w_im
    s_im = e_re * w_im + e_im * w_re
    y = mm(s_re, c_re) - mm(s_im, c_im) + dv * u
    z_re = jnp.sum(v_re, axis=0, keepdims=True) + p_re
    z_im = jnp.sum(v_im, axis=0, keepdims=True) + p_im
    return y, m_re * z_re - m_im * z_im, m_re * z_im + m_im * z_re


def _s5_fill_tables(are_ref, aim_ref, ldt_ref, tab, row):
    for g in range(S5_GB):
        ls = slice(512 * g, 512 * (g + 1))
        tabs, rows = _s5_tables(are_ref[:, ls], aim_ref[:, ls], ldt_ref[:, ls])
        for k, t in enumerate(tabs):
            tab[k, :, ls] = t
        for k, r in enumerate(rows):
            row[k:k + 1, ls] = r


def _s5_read_tables(tab, row, ls):
    return (tuple(tab[k, :, ls] for k in range(S5_NTAB)), tuple(row[k:k + 1, ls] for k in range(S5_NROW)))


def _s5_specs(nc, rev):
    T = CHUNK

    def ci(c):
        return nc - 1 - c if rev else c

    u_spec = pl.BlockSpec((T, S5_W), lambda c: (ci(c), 0))
    p_spec = pl.BlockSpec((1, S5_G * S5_P), lambda c: (0, 0))
    b_spec = pl.BlockSpec((S5_GB, 128, 512), lambda c: (0, 0, 0))
    c_spec = pl.BlockSpec((S5_GB, 512, 128), lambda c: (0, 0, 0))
    d_spec = pl.BlockSpec((1, S5_W), lambda c: (0, 0))
    st_spec = pl.BlockSpec((None, 2, S5_G * S5_P), lambda c: (ci(c), 0, 0))
    return u_spec, p_spec, b_spec, c_spec, d_spec, st_spec


def s5_fwd(pm, are, aim, ldt, b_re, b_im, c_re, c_im, dv):
    S = pm.shape[0]
    nc = S // CHUNK
    u_spec, p_spec, b_spec, c_spec, d_spec, st_spec = _s5_specs(nc, False)

    def body(u_ref, are_ref, aim_ref, ldt_ref, bre_ref, bim_ref, cre_ref, cim_ref, dv_ref,
             y_ref, st_ref, state, tab, row):
        c = pl.program_id(0)

        @pl.when(c == 0)
        def _():
            state[...] = jnp.zeros_like(state)
            _s5_fill_tables(are_ref, aim_ref, ldt_ref, tab, row)

        st_ref[...] = state[...]
        for g in range(S5_GB):
            lu, ls = slice(128 * g, 128 * (g + 1)), slice(512 * g, 512 * (g + 1))
            tabs, rows = _s5_read_tables(tab, row, ls)
            y, e_re, e_im = _s5_chunk(u_ref[:, lu], state[0:1, ls], state[1:2, ls], tabs, rows,
                                      bre_ref[g], bim_ref[g], cre_ref[g], cim_ref[g], dv_ref[:, lu])
            y_ref[:, lu] = y
            state[0:1, ls] = e_re
            state[1:2, ls] = e_im

    n_state = S5_G * S5_P
    return pl.pallas_call(
        body, name="s5_fwd", grid=(nc,),
        in_specs=[u_spec, p_spec, p_spec, p_spec, b_spec, b_spec, c_spec, c_spec, d_spec],
        out_specs=[u_spec, st_spec],
        out_shape=[jax.ShapeDtypeStruct((S, S5_W), F32), jax.ShapeDtypeStruct((nc, 2, n_state), F32)],
        scratch_shapes=[pltpu.VMEM((2, n_state), F32), pltpu.VMEM((S5_NTAB, CHUNK, n_state), F32),
                        pltpu.VMEM((8, n_state), F32)],
        compiler_params=_cp("arbitrary"),
    )(pm, are, aim, ldt, b_re, b_im, c_re, c_im, dv)


def s5_bwd(pm, st, dy, are, aim, ldt, b_re, b_im, c_re, c_im, dv):
    S = pm.shape[0]
    nc = S // CHUNK
    u_spec, p_spec, b_spec, c_spec, d_spec, st_spec = _s5_specs(nc, True)

    def body(u_ref, st_ref, dy_ref, are_ref, aim_ref, ldt_ref, bre_ref, bim_ref, cre_ref, cim_ref, dv_ref,
             du_ref, dare_ref, daim_ref, dldt_ref, dbre_ref, dbim_ref, dcre_ref, dcim_ref, ddv_ref,
             dstate, tab, row, dtab, drow):
        c = pl.program_id(0)

        @pl.when(c == 0)
        def _():
            dstate[...] = jnp.zeros_like(dstate)
            dtab[...] = jnp.zeros_like(dtab)
            drow[...] = jnp.zeros_like(drow)
            _s5_fill_tables(are_ref, aim_ref, ldt_ref, tab, row)

        for g in range(S5_GB):
            lu, ls = slice(128 * g, 128 * (g + 1)), slice(512 * g, 512 * (g + 1))
            every = slice(None)
            tabs, rows = _s5_read_tables(tab, row, ls)
            args = (u_ref[:, lu], st_ref[0:1, ls], st_ref[1:2, ls], tabs, rows,
                    bre_ref[g], bim_ref[g], cre_ref[g], cim_ref[g], dv_ref[:, lu])
            _, vjp = jax.vjp(_s5_chunk, *args)
            gr = vjp((dy_ref[:, lu], dstate[0:1, ls], dstate[1:2, ls]))
            du_ref[:, lu] = gr[0]
            dstate[0:1, ls] = gr[1]
            dstate[1:2, ls] = gr[2]
            for k, t in enumerate(gr[3]):
                dtab[k, :, ls] += t
            for k, r in enumerate(gr[4]):
                drow[k:k + 1, ls] += r
            accs = ((dbre_ref, (g,)), (dbim_ref, (g,)), (dcre_ref, (g,)), (dcim_ref, (g,)), (ddv_ref, (every, lu)))
            for (o, idx), gv in zip(accs, gr[5:]):
                @pl.when(c == 0)
                def _(o=o, idx=idx, gv=gv):
                    o[idx] = gv

                @pl.when(c > 0)
                def _(o=o, idx=idx, gv=gv):
                    o[idx] += gv

        @pl.when(c == nc - 1)
        def _():
            for g in range(S5_GB):
                ls = slice(512 * g, 512 * (g + 1))
                _, vjp = jax.vjp(_s5_tables, are_ref[:, ls], aim_ref[:, ls], ldt_ref[:, ls])
                dtabs, drows = _s5_read_tables(dtab, drow, ls)
                ga, gi, gl = vjp((dtabs, drows))
                dare_ref[:, ls] = ga
                daim_ref[:, ls] = gi
                dldt_ref[:, ls] = gl

    n_state = S5_G * S5_P
    return pl.pallas_call(
        body, name="s5_bwd", grid=(nc,),
        in_specs=[u_spec, st_spec, u_spec, p_spec, p_spec, p_spec, b_spec, b_spec, c_spec, c_spec, d_spec],
        out_specs=[u_spec, p_spec, p_spec, p_spec, b_spec, b_spec, c_spec, c_spec, d_spec],
        out_shape=[jax.ShapeDtypeStruct((S, S5_W), F32)] + [jax.ShapeDtypeStruct((1, n_state), F32)] * 3
        + [jax.ShapeDtypeStruct((S5_GB, 128, 512), F32)] * 2 + [jax.ShapeDtypeStruct((S5_GB, 512, 128), F32)] * 2
        + [jax.ShapeDtypeStruct((1, S5_W), F32)],
        scratch_shapes=[pltpu.VMEM((2, n_state), F32), pltpu.VMEM((S5_NTAB, CHUNK, n_state), F32),
                        pltpu.VMEM((8, n_state), F32), pltpu.VMEM((S5_NTAB, CHUNK, n_state), F32),
                        pltpu.VMEM((8, n_state), F32)],
        compiler_params=_cp("arbitrary"),
    )(pm, st, dy, are, aim, ldt, b_re, b_im, c_re, c_im, dv)


def conv_fwd(pm, w):
    S = pm.shape[0]

    def body(x_ref, w_ref, o_ref, pad):
        x = x_ref[...]
        pad[0:8, :] = jnp.zeros((8, 128), F32)
        pad[8:, :] = x
        y = (w_ref[3:4, :] * x + w_ref[2:3, :] * pad[7:7 + S, :] + w_ref[1:2, :] * pad[6:6 + S, :]
             + w_ref[0:1, :] * pad[5:5 + S, :])
        o_ref[...] = y * jax.nn.sigmoid(y)

    return pl.pallas_call(
        body, name="conv_fwd", grid=(12,),
        in_specs=[pl.BlockSpec((S, 128), lambda j: (0, 4 + j)), pl.BlockSpec((4, 128), lambda j: (0, j))],
        out_specs=pl.BlockSpec((S, 128), lambda j: (0, j)),
        out_shape=jax.ShapeDtypeStruct((S, 1536), F32),
        scratch_shapes=[pltpu.VMEM((S + 8, 128), F32)],
        compiler_params=_cp("parallel"),
    )(pm, w)


def conv_bwd(pm, w, dout):
    S = pm.shape[0]

    def body(x_ref, w_ref, do_ref, dx_ref, dw_ref, pad, dpad):
        x = x_ref[...]
        pad[0:8, :] = jnp.zeros((8, 128), F32)
        pad[8:, :] = x
        xs = [pad[5:5 + S, :], pad[6:6 + S, :], pad[7:7 + S, :], x]
        y = w_ref[0:1, :] * xs[0] + w_ref[1:2, :] * xs[1] + w_ref[2:3, :] * xs[2] + w_ref[3:4, :] * xs[3]
        sg = jax.nn.sigmoid(y)
        dy = do_ref[...] * (sg + y * sg * (1.0 - sg))
        dpad[0:S, :] = dy
        dpad[S:, :] = jnp.zeros((8, 128), F32)
        dx_ref[...] = (w_ref[3:4, :] * dy + w_ref[2:3, :] * dpad[1:1 + S, :] + w_ref[1:2, :] * dpad[2:2 + S, :]
                       + w_ref[0:1, :] * dpad[3:3 + S, :])
        for i in range(4):
            dw_ref[i:i + 1, :] = jnp.sum(dy * xs[i], axis=0, keepdims=True)

    return pl.pallas_call(
        body, name="conv_bwd", grid=(12,),
        in_specs=[pl.BlockSpec((S, 128), lambda j: (0, 4 + j)), pl.BlockSpec((4, 128), lambda j: (0, j)),
                  pl.BlockSpec((S, 128), lambda j: (0, j))],
        out_specs=[pl.BlockSpec((S, 128), lambda j: (0, j)), pl.BlockSpec((4, 128), lambda j: (0, j))],
        out_shape=[jax.ShapeDtypeStruct((S, 1536), F32), jax.ShapeDtypeStruct((4, 1536), F32)],
        scratch_shapes=[pltpu.VMEM((S + 8, 128), F32), pltpu.VMEM((S + 8, 128), F32)],
        compiler_params=_cp("parallel"),
    )(pm, w, dout)


GDN_SUP = 4
GDN_ROWS = GDN_SUP * CHUNK


def _gdn_chunk(q, k, v, gate, al, bl, alog, dtb, og, state):
    R = q.shape[0]
    r = lax.broadcasted_iota(jnp.int32, (R, R), 0)
    c = lax.broadcasted_iota(jnp.int32, (R, R), 1)
    same = (r // CHUNK) == (c // CHUNK)
    eye = (r == c).astype(F32)
    strict, causal, upper = same & (r > c), same & (r >= c), same & (r <= c)
    qn = q * lax.rsqrt(jnp.sum(q * q, axis=-1, keepdims=True) + EPS) * (GDN_D ** -0.5)
    kn = k * lax.rsqrt(jnp.sum(k * k, axis=-1, keepdims=True) + EPS)
    beta = jax.nn.sigmoid(bl)
    g = -jnp.exp(alog) * _softplus(al + dtb)
    g_row = jnp.sum(eye * g, axis=0, keepdims=True)
    gc_col = jnp.sum(jnp.where(causal, g_row, 0.0), axis=1, keepdims=True)
    gc_row = jnp.sum(jnp.where(upper, g, 0.0), axis=0, keepdims=True)
    gtot = jnp.sum(jnp.where(same, g_row, 0.0), axis=1, keepdims=True)
    gamma = jnp.exp(gc_col)
    diff = gc_col - gc_row
    d_strict = jnp.where(strict, jnp.exp(jnp.where(strict, diff, 0.0)), 0.0)
    d_causal = jnp.where(causal, jnp.exp(jnp.where(causal, diff, 0.0)), 0.0)
    a = beta * mm_nt(kn, kn) * d_strict
    p = -a
    x = eye + p
    for _ in range(5):
        p = mm(p, p)
        x = x + mm(x, p)
    u_new = mm(x, beta * v)
    w_k = mm(x, (beta * gamma) * kn)
    qk = mm_nt(qn, kn) * d_causal
    q_g = qn * gamma
    k_tail = kn * jnp.exp(gtot - gc_col)
    ws, os_ = [], []
    for i in range(R // CHUNK):
        rows = slice(CHUNK * i, CHUNK * (i + 1))
        w_i = u_new[rows] - mm(w_k[rows], state)
        os_.append(mm(q_g[rows], state))
        decay = jnp.exp(jnp.sum(g[rows], axis=0, keepdims=True))
        state = decay * state + mm_tn(k_tail[rows], w_i)
        ws.append(w_i)
    o = jnp.concatenate(os_, axis=0) + mm(qk, jnp.concatenate(ws, axis=0))
    out = _rms(o, og) * (gate * jax.nn.sigmoid(gate))
    return out, state


def _gdn_specs(nc, rev):
    def ci(c):
        return nc - 1 - c if rev else c

    def blk(cb):
        return pl.BlockSpec((GDN_ROWS, 512), lambda c: (ci(c), cb))

    col = lambda n: pl.BlockSpec((n, GDN_ROWS, 1), lambda c: (0, ci(c), 0))
    sc = pl.BlockSpec((GDN_H, 1, 1), lambda c: (0, 0, 0))
    og = pl.BlockSpec((1, 128), lambda c: (0, 0))
    st = pl.BlockSpec((GDN_H, None, 128, 128), lambda c: (0, ci(c), 0, 0))
    return blk, col, sc, og, st


def gdn_fwd(qkvc, pm, abt, alog, dtb, og):
    S = qkvc.shape[0]
    nc = S // GDN_ROWS
    blk, col, sc, ogs, st = _gdn_specs(nc, False)

    def body(q_ref, k_ref, v_ref, gate_ref, ab_ref, alog_ref, dtb_ref, og_ref, o_ref, st_ref, state):
        c = pl.program_id(0)

        @pl.when(c == 0)
        def _():
            state[...] = jnp.zeros_like(state)

        st_ref[...] = state[...]
        for h in range(GDN_H):
            sl = slice(GDN_D * h, GDN_D * (h + 1))
            out, new_state = _gdn_chunk(q_ref[:, sl], k_ref[:, sl], v_ref[:, sl], gate_ref[:, sl], ab_ref[h],
                                        ab_ref[GDN_H + h], alog_ref[h], dtb_ref[h], og_ref[...], state[h])
            o_ref[:, sl] = out
            state[h] = new_state

    return pl.pallas_call(
        body, name="gdn_fwd", grid=(nc,),
        in_specs=[blk(0), blk(1), blk(2), blk(4), col(2 * GDN_H), sc, sc, ogs],
        out_specs=[blk(0), st],
        out_shape=[jax.ShapeDtypeStruct((S, 512), F32), jax.ShapeDtypeStruct((GDN_H, nc, 128, 128), F32)],
        scratch_shapes=[pltpu.VMEM((GDN_H, 128, 128), F32)],
        compiler_params=_cp("arbitrary"),
    )(qkvc, qkvc, qkvc, pm, abt, alog, dtb, og)


def gdn_bwd(qkvc, pm, abt, alog, dtb, og, st, dout):
    S = qkvc.shape[0]
    nc = S // GDN_ROWS
    blk, col, sc, ogs, sts = _gdn_specs(nc, True)

    def body(q_ref, k_ref, v_ref, gate_ref, ab_ref, alog_ref, dtb_ref, og_ref, st_ref, do_ref,
             dq_ref, dk_ref, dv_ref, dgate_ref, dal_ref, dbl_ref, dalog_ref, ddtb_ref, dog_ref, dstate):
        c = pl.program_id(0)

        @pl.when(c == 0)
        def _():
            dstate[...] = jnp.zeros_like(dstate)

        for h in range(GDN_H):
            sl = slice(GDN_D * h, GDN_D * (h + 1))
            args = (q_ref[:, sl], k_ref[:, sl], v_ref[:, sl], gate_ref[:, sl], ab_ref[h], ab_ref[GDN_H + h],
                    alog_ref[h], dtb_ref[h], og_ref[...], st_ref[h])
            _, vjp = jax.vjp(_gdn_chunk, *args)
            g = vjp((do_ref[:, sl], dstate[h]))
            for o, gv in zip((dq_ref, dk_ref, dv_ref, dgate_ref), g[:4]):
                o[:, sl] = gv
            dal_ref[h] = g[4]
            dbl_ref[h] = g[5]
            dstate[h] = g[9]
            for o, gv in zip((dalog_ref, ddtb_ref, dog_ref), g[6:9]):
                @pl.when(c == 0)
                def _(o=o, gv=gv, h=h):
                    o[h] = gv

                @pl.when(c > 0)
                def _(o=o, gv=gv, h=h):
                    o[h] += gv

    ogo = pl.BlockSpec((GDN_H, 1, 128), lambda c: (0, 0, 0))
    sd = jax.ShapeDtypeStruct
    return pl.pallas_call(
        body, name="gdn_bwd", grid=(nc,),
        in_specs=[blk(0), blk(1), blk(2), blk(4), col(2 * GDN_H), sc, sc, ogs, sts, blk(0)],
        out_specs=[blk(0), blk(0), blk(0), blk(0), col(GDN_H), col(GDN_H), sc, sc, ogo],
        out_shape=[sd((S, 512), F32)] * 4 + [sd((GDN_H, S, 1), F32)] * 2 + [sd((GDN_H, 1, 1), F32)] * 2
        + [sd((GDN_H, 1, 128), F32)],
        scratch_shapes=[pltpu.VMEM((GDN_H, 128, 128), F32)],
        compiler_params=_cp("arbitrary"),
    )(qkvc, qkvc, qkvc, pm, abt, alog, dtb, og, st, dout)


HG = 4


def _cattn_chunk(q, kb, vb, bias, valid):
    s = bmm_nt(q, kb) * (CA_D ** -0.5) + bias
    s = jnp.where(valid, s, -1e30)
    return bmm(_softmax(s), vb)


def _cattn_valid(c):
    pos = lax.broadcasted_iota(jnp.int32, (1, 1, CA_BAND), 2) + c * CHUNK
    return pos >= CA_PAD


def cattn_fwd(qn, kp, vp, bias):
    S = qn.shape[1]
    nc = S // CHUNK
    q_spec = pl.BlockSpec((HG, CHUNK, CA_D), lambda h, c: (h, c, 0))
    kv_spec = pl.BlockSpec((HG, S + CA_PAD, CA_D), lambda h, c: (h, 0, 0))
    b_spec = pl.BlockSpec((HG, CHUNK, CA_BAND), lambda h, c: (h, 0, 0))

    def body(q_ref, k_ref, v_ref, b_ref, o_ref):
        c = pl.program_id(1)
        start = pl.multiple_of(c * CHUNK, CHUNK)
        kb = k_ref[:, pl.ds(start, CA_BAND), :]
        vb = v_ref[:, pl.ds(start, CA_BAND), :]
        o_ref[...] = _cattn_chunk(q_ref[...], kb, vb, b_ref[...], _cattn_valid(c))

    return pl.pallas_call(
        body, name="cattn_fwd", grid=(CA_H // HG, nc), in_specs=[q_spec, kv_spec, kv_spec, b_spec],
        out_specs=q_spec, out_shape=jax.ShapeDtypeStruct((CA_H, S, CA_D), F32),
        compiler_params=_cp("parallel", "arbitrary"),
    )(qn, kp, vp, bias)


def cattn_bwd(qn, kp, vp, bias, do):
    S = qn.shape[1]
    nc = S // CHUNK
    q_spec = pl.BlockSpec((HG, CHUNK, CA_D), lambda h, c: (h, c, 0))
    kv_spec = pl.BlockSpec((HG, S + CA_PAD, CA_D), lambda h, c: (h, 0, 0))
    b_spec = pl.BlockSpec((HG, CHUNK, CA_BAND), lambda h, c: (h, 0, 0))

    def body(q_ref, k_ref, v_ref, b_ref, do_ref, dq_ref, dk_ref, dv_ref, db_ref):
        c = pl.program_id(1)

        @pl.when(c == 0)
        def _():
            dk_ref[...] = jnp.zeros_like(dk_ref)
            dv_ref[...] = jnp.zeros_like(dv_ref)
            db_ref[...] = jnp.zeros_like(db_ref)

        start = pl.multiple_of(c * CHUNK, CHUNK)
        kb = k_ref[:, pl.ds(start, CA_BAND), :].astype(F32)
        vb = v_ref[:, pl.ds(start, CA_BAND), :].astype(F32)
        valid = _cattn_valid(c)
        _, vjp = jax.vjp(lambda q, k, v, b: _cattn_chunk(q, k, v, b, valid), q_ref[...], kb, vb, b_ref[...])
        dq, dk, dv, db = vjp(do_ref[...])
        dq_ref[...] = dq
        dk_ref[:, pl.ds(start, CA_BAND), :] += dk
        dv_ref[:, pl.ds(start, CA_BAND), :] += dv
        db_ref[...] += db

    sd = jax.ShapeDtypeStruct
    return pl.pallas_call(
        body, name="cattn_bwd", grid=(CA_H // HG, nc), in_specs=[q_spec, kv_spec, kv_spec, b_spec, q_spec],
        out_specs=[q_spec, kv_spec, kv_spec, b_spec],
        out_shape=[sd((CA_H, S, CA_D), F32), sd((CA_H, S + CA_PAD, CA_D), F32), sd((CA_H, S + CA_PAD, CA_D), F32),
                   sd((CA_H, CHUNK, CA_BAND), F32)],
        compiler_params=_cp("parallel", "arbitrary"),
    )(qn, kp, vp, bias, do)


_REL_IDX = np.clip(np.arange(CHUNK)[:, None] - np.arange(CA_BAND)[None, :] + CA_PAD, -MAX_REL, MAX_REL) + MAX_REL
SKEW_W = CA_BAND + CHUNK


def rel_bias_grad(dbias):
    padded = jnp.pad(dbias, ((0, 0), (0, 0), (CHUNK, 0)))
    flat = jnp.pad(padded.reshape(CA_H, CHUNK * SKEW_W), ((0, 0), (0, CHUNK)))
    skew = flat.reshape(CA_H, CHUNK, SKEW_W + 1)

    first_near = SKEW_W - CHUNK - MAX_REL

    def fn(t):
        colsum = jnp.sum(t, axis=1, keepdims=True)
        j = lax.broadcasted_iota(jnp.int32, colsum.shape, 2)
        far = jnp.sum(jnp.where(j < first_near, colsum, 0.0), axis=2, keepdims=True)
        return (colsum + jnp.where(j == first_near, far, 0.0),)

    (colsum,) = whole("relbias_sum", fn, [skew], [((CA_H, 1, SKEW_W + 1), F32)])
    near = colsum[:, 0, first_near:SKEW_W][:, ::-1]
    return jnp.concatenate([jnp.zeros((CA_H, CHUNK + 1), F32), near], axis=1)


def rel_bias_expand(rb):
    near = rb[:, CHUNK + 1:][:, ::-1]
    far = jnp.broadcast_to(rb[:, 2 * MAX_REL:], (CA_H, SKEW_W - CHUNK - MAX_REL))
    t = jnp.concatenate([far, near, jnp.zeros((CA_H, 1), rb.dtype)], axis=1)
    rows = jnp.tile(t, (1, CHUNK))[:, :CHUNK * SKEW_W].reshape(CA_H, CHUNK, SKEW_W)
    return rows[:, :, CHUNK:]


def loss_head(y, target):
    S = y.shape[0]
    tile = min(ROW_TILE, S)

    def body(y_ref, t_ref, dy_ref, acc_ref):
        i = pl.program_id(0)
        e = y_ref[...] - t_ref[...]
        dy_ref[...] = e * (1.0 / D)
        part = jnp.sum(e * e, axis=0, keepdims=True) * (0.5 / D)

        @pl.when(i == 0)
        def _():
            acc_ref[...] = part

        @pl.when(i > 0)
        def _():
            acc_ref[...] += part

    row = pl.BlockSpec((tile, D), lambda i: (i, 0))
    return pl.pallas_call(
        body, name="loss_head", grid=(S // tile,), in_specs=[row, row],
        out_specs=[row, pl.BlockSpec((1, D), lambda i: (0, 0))],
        out_shape=[jax.ShapeDtypeStruct((S, D), F32), jax.ShapeDtypeStruct((1, D), F32)],
        compiler_params=_cp("arbitrary"),
    )(y, target)


ANY = pl.BlockSpec(memory_space=pl.ANY)


HBM = pl.BlockSpec(memory_space=pltpu.HBM)
SEM = pl.BlockSpec(memory_space=pltpu.SEMAPHORE)
EFFECT = pltpu.SideEffectType.DATAFLOW_SIDE_EFFECTING


def _chip_copies(src_ref, land_ref, send_sems, recv_sems, per_dest):
    x, y, c = lax.axis_index("x"), lax.axis_index("y"), lax.axis_index("c")
    me = 2 * x + y
    out = []
    for j, (px, py) in enumerate([(1 - x, y), (x, 1 - y), (1 - x, 1 - y)]):
        peer = 2 * px + py
        if per_dest:
            send = (src_ref.at[peer], land_ref.at[j])
            recv = (src_ref.at[me], land_ref.at[j])
        else:
            send = (src_ref, land_ref.at[me])
            recv = (src_ref, land_ref.at[peer])
        mk = lambda s, d, j=j, px=px, py=py: pltpu.make_async_remote_copy(
            src_ref=s, dst_ref=d, send_sem=send_sems.at[j], recv_sem=recv_sems.at[j],
            device_id=(px, py, c), device_id_type=MESH)
        out.append((mk(*send), mk(*recv)))
    return out


def exchange_start(src, land, carry, name, per_dest):
    def body(src_ref, land_ref, carry_ref, send_sems, recv_sems, src_out, land_out, carry_out):
        for send, _ in _chip_copies(src_ref, land_ref, send_sems, recv_sems, per_dest):
            send.start()

    hbm = lambda a: pltpu.HBM(a.shape, a.dtype)
    return pl.pallas_call(
        body, name=name,
        out_shape=(pltpu.SemaphoreType.DMA((3,)), pltpu.SemaphoreType.DMA((3,)), hbm(src), hbm(land), hbm(carry)),
        in_specs=(HBM, HBM, HBM), out_specs=(SEM, SEM, HBM, HBM, HBM),
        input_output_aliases={0: 2, 1: 3, 2: 4},
        compiler_params=pltpu.CompilerParams(has_side_effects=EFFECT),
    )(pltpu.with_memory_space_constraint(src, pltpu.HBM), pltpu.with_memory_space_constraint(land, pltpu.HBM),
      pltpu.with_memory_space_constraint(carry, pltpu.HBM))


def exchange_wait(src, land, send_sems, recv_sems, after, name, per_dest):
    def body(src_ref, land_ref, send_sems_ref, recv_sems_ref, after_ref, src_out, land_out):
        for send, recv in _chip_copies(src_ref, land_ref, send_sems_ref, recv_sems_ref, per_dest):
            send.wait_send()
            recv.wait_recv()

    hbm = lambda a: pltpu.HBM(a.shape, a.dtype)
    return pl.pallas_call(
        body, name=name, out_shape=(hbm(src), hbm(land)),
        in_specs=(HBM, HBM, SEM, SEM, ANY), out_specs=(HBM, HBM), input_output_aliases={0: 0, 1: 1},
        compiler_params=pltpu.CompilerParams(has_side_effects=EFFECT),
    )(src, land, send_sems, recv_sems, after)


def sibling_exchange(srcs, name):
    n = len(srcs)

    def body(*refs):
        src_refs, out_refs, send_sems, recv_sems = refs[:n], refs[n:2 * n], refs[2 * n], refs[2 * n + 1]
        x, y, c = lax.axis_index("x"), lax.axis_index("y"), lax.axis_index("c")
        copies = [pltpu.make_async_remote_copy(src_ref=s, dst_ref=o, send_sem=send_sems.at[k], recv_sem=recv_sems.at[k],
                                               device_id=(x, y, 1 - c), device_id_type=MESH)
                  for k, (s, o) in enumerate(zip(src_refs, out_refs))]
        for cp in copies:
            cp.start()
        for cp in copies:
            cp.wait()

    return pl.pallas_call(
        body, name=name, in_specs=[ANY] * n, out_specs=[ANY] * n,
        out_shape=[jax.ShapeDtypeStruct(s.shape, s.dtype) for s in srcs],
        scratch_shapes=[pltpu.SemaphoreType.DMA((n,)), pltpu.SemaphoreType.DMA((n,))],
    )(*srcs)


def all_exchange(src, name):
    def body(src_ref, out_ref, send_sems, recv_sems, local_sem):
        x, y, c = lax.axis_index("x"), lax.axis_index("y"), lax.axis_index("c")
        me = 4 * x + 2 * y + c
        local = pltpu.make_async_copy(src_ref, out_ref.at[me], local_sem)
        local.start()
        sends = []
        peers = []
        for k in range(1, N_DEV):
            bx, by, bc = (k >> 2) & 1, (k >> 1) & 1, k & 1
            px = 1 - x if bx else x
            py = 1 - y if by else y
            pc = 1 - c if bc else c
            peers.append((px, py, pc))
        for k, peer in enumerate(peers):
            cp = pltpu.make_async_remote_copy(src_ref=src_ref, dst_ref=out_ref.at[me], send_sem=send_sems.at[k],
                                              recv_sem=recv_sems.at[k], device_id=peer, device_id_type=MESH)
            cp.start()
            sends.append(cp)
        for k, (px, py, pc) in enumerate(peers):
            pltpu.make_async_remote_copy(src_ref=src_ref, dst_ref=out_ref.at[4 * px + 2 * py + pc],
                                         send_sem=send_sems.at[k], recv_sem=recv_sems.at[k],
                                         device_id=(px, py, pc), device_id_type=MESH).wait_recv()
        for cp in sends:
            cp.wait_send()
        local.wait()

    return pl.pallas_call(
        body, name=name, in_specs=[ANY], out_specs=ANY,
        out_shape=jax.ShapeDtypeStruct((N_DEV,) + tuple(src.shape), src.dtype),
        scratch_shapes=[pltpu.SemaphoreType.DMA((N_DEV - 1,)), pltpu.SemaphoreType.DMA((N_DEV - 1,)),
                        pltpu.SemaphoreType.DMA],
    )(src)


def sum_own_slabs(own, land, name, tile=512):
    R, C = own.shape
    n = land.shape[0]
    tile = _tile(R, tile, 16)

    def body(o_ref, t_ref, out_ref):
        acc = o_ref[...].astype(F32)
        for s in range(n):
            acc = acc + t_ref[s].astype(F32)
        out_ref[...] = acc

    return pl.pallas_call(
        body, name=name, grid=(R // tile,),
        in_specs=[pl.BlockSpec((tile, C), lambda i: (i, 0)), pl.BlockSpec((n, tile, C), lambda i: (0, i, 0))],
        out_specs=pl.BlockSpec((tile, C), lambda i: (i, 0)), out_shape=jax.ShapeDtypeStruct((R, C), F32),
        compiler_params=_cp("parallel"),
    )(own, land)


def sum_slabs(t, name, tile=512):
    n, R, C = t.shape
    tile = _tile(R, tile, 16)

    def body(t_ref, o_ref):
        acc = t_ref[0].astype(F32)
        for s in range(1, n):
            acc = acc + t_ref[s].astype(F32)
        o_ref[...] = acc

    return pl.pallas_call(
        body, name=name, grid=(R // tile,), in_specs=[pl.BlockSpec((n, tile, C), lambda i: (0, i, 0))],
        out_specs=pl.BlockSpec((tile, C), lambda i: (i, 0)), out_shape=jax.ShapeDtypeStruct((R, C), F32),
        compiler_params=_cp("parallel"),
    )(t)


PACK_ROW_MULT = 512


def _pad_rows(a, mult=16):
    r = (-a.shape[0]) % mult
    return jnp.pad(a, ((0, r), (0, 0))) if r else a


BIG = [
    ("ab_w_in", True, 2, 642), ("c_w_qkv", True, 2, 768), ("xa_w_kv", True, 4, 512),
    ("f_w_gate", True, 4, 704), ("f_w_up", True, 4, 704),
    ("ab_w_out", False, 2, 256), ("c_w_out", False, 2, 256), ("xa_w_q", False, 4, 256),
    ("xa_w_out", False, 4, 256), ("f_w_down", False, 4, 704), ("s5_w_glu", False, 2, 64),
]


GROUPS = ("B", "A")
GROUP_ROW_MULT = 64


def group_spec(layer, grp):
    i = layer // 2
    if grp == "A":
        return [("xa_w_kv", layer, True, 512), ("xa_w_q", layer, False, 256), ("xa_w_out", layer, False, 256),
                ("f_w_gate", layer, True, 704), ("f_w_up", layer, True, 704), ("f_w_down", layer, False, 704)]
    if layer % 2 == 0:
        return [("ab_w_in", i, True, 642), ("ab_w_out", i, False, 256), ("s5_w_glu", i, False, 64)]
    return [("c_w_qkv", i, True, 768), ("c_w_out", i, False, 256)]


def _seg_rows(rows):
    return rows + ((-rows) % 16)


def _f32_rows(a):
    bits = lax.bitcast_convert_type(a.reshape(-1), BF16).reshape(-1)
    return jnp.pad(bits, (0, 16 * D - bits.shape[0])).reshape(16, D)


def pack_group_shards(p, layer, grp):
    segs = []
    for name, idx, transposed, rows in group_spec(layer, grp):
        w = p[name][idx]
        if transposed:
            w = w.T
        segs.append(_pad_rows(w.astype(BF16).reshape(-1, D)))
    if grp == "B":
        small = p["gdn_conv_w"] if layer % 2 == 0 else p["c_norm_g"]
        segs.append(_f32_rows(small[layer // 2]))
    return _pad_rows(jnp.concatenate(segs, axis=0), GROUP_ROW_MULT)


def unpack_group_gathered(g, layer, grp):
    out, off = {}, 0
    for name, idx, transposed, rows in group_spec(layer, grp):
        seg = g[:, off:off + rows]
        if name == "s5_w_glu":
            out[name] = seg.reshape(N_CHIPS * 128, 512)
        else:
            out[name] = seg.reshape(N_CHIPS * rows, D)
        off += _seg_rows(rows)
    if grp == "B":
        n = 4 * 384 if layer % 2 == 0 else 256
        bits = g[:, off:off + 16].reshape(N_CHIPS, -1)[:, :2 * n].reshape(N_CHIPS, n, 2)
        small = lax.bitcast_convert_type(bits, F32)
        if layer % 2 == 0:
            out["gdn_conv_w"] = jnp.swapaxes(small.reshape(N_CHIPS, 4, 384), 0, 1).reshape(4, 1536)
        else:
            out["c_norm_g"] = small.reshape(D)
    return out


def pack_group_grads(gr, layer, grp):
    segs = []
    for name, idx, transposed, rows in group_spec(layer, grp):
        w = gr[name].astype(BF16)
        seg = w.reshape(N_CHIPS, rows, D)
        r = (-rows) % 16
        if r:
            seg = jnp.pad(seg, ((0, 0), (0, r), (0, 0)))
        segs.append(seg)
    out = jnp.concatenate(segs, axis=1)
    return jnp.pad(out, ((0, 0), (0, (-out.shape[1]) % GROUP_ROW_MULT), (0, 0)))


def unpack_group_reduced(g, layer, grp):
    out, off = {}, 0
    for name, idx, transposed, rows in group_spec(layer, grp):
        seg = g[off:off + rows]
        if name == "s5_w_glu":
            out[name] = seg.reshape(128, 512)
        else:
            out[name] = seg.T if transposed else seg
        off += _seg_rows(rows)
    return out


SMALL = ["ab_norm_g", "s5_a_re", "s5_a_im", "s5_log_dt", "s5_b_re", "s5_b_im", "s5_c_re", "s5_c_im", "s5_d",
         "s5_b_glu", "gdn_conv_w", "gdn_a_log", "gdn_dt_bias", "gdn_out_norm_g", "c_norm_g", "c_q_norm_g",
         "c_k_norm_g", "c_rel_bias", "mem_norm_g", "xa_norm_g", "xa_q_norm_g", "xa_k_norm_g", "f_norm_g"]


def _lane_rows(a):
    flat = a.reshape(-1).astype(F32)
    return jnp.pad(flat, (0, (-flat.shape[0]) % 128)).reshape(-1, 128)


def pack_small(d, extra=None):
    parts = [_lane_rows(d[n]) for n in SMALL]
    if extra is not None:
        parts.append(_lane_rows(extra))
    rows = jnp.concatenate(parts, axis=0)
    return jnp.pad(rows, ((0, (-rows.shape[0]) % 128), (0, 0)))


def unpack_small(rows, shapes):
    out, off = {}, 0
    for n in SMALL:
        sz = int(np.prod(shapes[n]))
        k = -(-sz // 128)
        out[n] = rows[off:off + k].reshape(-1)[:sz].reshape(shapes[n])
        off += k
    return out, rows[off:]


def _s5_blockdiag_b(b):
    bt = jnp.swapaxes(b, 1, 2).reshape(S5_GB, 8, S5_C, S5_P)
    eye = jnp.eye(8, dtype=b.dtype)
    return jnp.einsum("bgcp,gh->bgchp", bt, eye).reshape(S5_GB, 8 * S5_C, 8 * S5_P)


def _s5_blockdiag_c(c):
    ct = jnp.swapaxes(c, 1, 2).reshape(S5_GB, 8, S5_P, S5_C)
    eye = jnp.eye(8, dtype=c.dtype)
    return jnp.einsum("bgpc,gh->bgphc", ct, eye).reshape(S5_GB, 8 * S5_P, 8 * S5_C)


def _s5_diag_b(db):
    t = db.reshape(S5_GB, 8, S5_C, 8, S5_P)
    t = jnp.transpose(t, (0, 2, 4, 1, 3)).reshape(S5_GB, S5_C, S5_P, 64)
    d = t[..., ::9]
    return jnp.transpose(d, (0, 3, 2, 1)).reshape(S5_G, S5_P, S5_C)


def _s5_diag_c(dc):
    t = dc.reshape(S5_GB, 8, S5_P, 8, S5_C)
    t = jnp.transpose(t, (0, 2, 4, 1, 3)).reshape(S5_GB, S5_P, S5_C, 64)
    d = t[..., ::9]
    return jnp.transpose(d, (0, 3, 2, 1)).reshape(S5_G, S5_C, S5_P)


def _heads(t):
    return jnp.swapaxes(t.reshape(t.shape[0], CA_H, CA_D), 0, 1)


def _unheads(t):
    return jnp.swapaxes(t, 0, 1).reshape(t.shape[1], D)


def local_step(x, mem, target, p, wsrc, gsink):
    S = x.shape[0]
    row2 = lambda a: a.reshape(1, -1)
    saved = []
    (mem_n,) = row_fwd("mem_norm", f_norm, [mem], [row2(p["mem_norm_g"])], [(D, BF16)])
    gs = {}

    for layer in range(DEPTH):
        i = layer // 2
        w = wsrc(layer, "B", x)
        sv = {"x0": x, "wB": w}
        if layer % 2 == 0:
            (h,) = row_fwd("norm", f_norm, [x], [row2(p["ab_norm_g"][i])], [(D, BF16)])
            w_in = w["ab_w_in"]
            pm = matmul("nt", h, w_in[:2560], "proj_main")
            pab = matmul("nt", h, w_in[2560:], "proj_ab")
            s5p = dict(
                are=p["s5_a_re"][i].reshape(1, -1), aim=p["s5_a_im"][i].reshape(1, -1),
                ldt=jnp.broadcast_to(p["s5_log_dt"][i][:, None], (S5_G, S5_P)).reshape(1, -1),
                b_re=_s5_blockdiag_b(p["s5_b_re"][i]), b_im=_s5_blockdiag_b(p["s5_b_im"][i]),
                c_re=_s5_blockdiag_c(p["s5_c_re"][i]), c_im=_s5_blockdiag_c(p["s5_c_im"][i]),
                dv=p["s5_d"][i].reshape(1, -1))
            y5, st5 = s5_fwd(pm, **s5p)
            (a_out,) = row_fwd("glu", f_glu, [y5], [w["s5_w_glu"], row2(p["s5_b_glu"][i])], [(S5_W, F32)])
            conv_w = w["gdn_conv_w"]
            qkvc = conv_fwd(pm, conv_w)
            abt = jnp.swapaxes(pab, 0, 1)[:, :, None]
            alog = p["gdn_a_log"][i].reshape(GDN_H, 1, 1)
            dtb = p["gdn_dt_bias"][i].reshape(GDN_H, 1, 1)
            og = row2(p["gdn_out_norm_g"][i])
            b_out, stg = gdn_fwd(qkvc, pm, abt, alog, dtb, og)
            cat = jnp.concatenate([a_out, b_out], axis=1)
            x = matmul("nn", cat, w["ab_w_out"], "mix_out", add=x)
            sv.update(h=h, pm=pm, s5p=s5p, y5=y5, st5=st5, qkvc=qkvc, abt=abt, alog=alog, dtb=dtb, og=og,
                      stg=stg, cat=cat, conv_w=conv_w)
        else:
            (h,) = row_fwd("norm", f_norm, [x], [row2(w["c_norm_g"])], [(D, BF16)])
            qkv = matmul("nt", h, w["c_w_qkv"], "proj_qkv")
            q3, k3, v3 = _heads(qkv[:, :D]), _heads(qkv[:, D:2 * D]), _heads(qkv[:, 2 * D:])
            (qn,) = row_fwd("headnorm_q", f_headnorm, [q3], [row2(p["c_q_norm_g"][i])], [((CA_H, CA_D), F32)])
            (kn,) = row_fwd("headnorm_k", f_headnorm, [k3], [row2(p["c_k_norm_g"][i])], [((CA_H, CA_D), BF16)])
            kp = jnp.pad(kn, ((0, 0), (CA_PAD, 0), (0, 0)))
            vp = jnp.pad(v3.astype(BF16), ((0, 0), (CA_PAD, 0), (0, 0)))
            bias = rel_bias_expand(p["c_rel_bias"][i])
            o3 = cattn_fwd(qn, kp, vp, bias)
            o = _unheads(o3).astype(BF16)
            x = matmul("nn", o, w["c_w_out"], "mix_out", add=x)
            sv.update(h=h, q3=q3, k3=k3, qn=qn, kp=kp, vp=vp, bias=bias, o=o)
        sv["x1"] = x
        w = wsrc(layer, "A", x)
        sv["wA"] = w
        (hq,) = row_fwd("norm", f_norm, [x], [row2(p["xa_norm_g"][layer])], [(D, BF16)])
        qx = matmul("nn", hq, w["xa_w_q"], "xa_q")
        kv = matmul("nt", mem_n, w["xa_w_kv"], "xa_kv")
        xqg, xkg = row2(p["xa_q_norm_g"][layer]), row2(p["xa_k_norm_g"][layer])
        (ox,) = row_fwd("xattn", f_xattn, [qx], [kv, xqg, xkg], [(D, BF16)])
        x = matmul("nn", ox, w["xa_w_out"], "xa_out", add=x)
        sv.update(hq=hq, qx=qx, kv=kv, ox=ox)
        sv["x2"] = x
        (hf,) = row_fwd("norm", f_norm, [x], [row2(p["f_norm_g"][layer])], [(D, BF16)])
        w_gu = jnp.concatenate([w["f_w_gate"], w["f_w_up"]], axis=0)
        gu = matmul("nt", hf, w_gu, "ffn_gu")
        (act,) = row_fwd("swiglu", f_swiglu, [gu], [], [(FFN, BF16)])
        x = matmul("nn", act, w["f_w_down"], "ffn_down", add=x)
        sv.update(hf=hf, gu=gu, act=act, w_gu=w_gu)
        saved.append(sv)

    dx, loss_vec = loss_head(x, target)

    dmem_n = None
    for layer in reversed(range(DEPTH)):
        i = layer // 2
        sv = saved[layer]
        w, gw = sv["wA"], {}
        dact = matmul("nt", dx, w["f_w_down"], "ffn_dact")
        gw["f_w_down"] = matmul("tn", sv["act"], dx, "ffn_dwd", out_dtype=BF16)
        (dgu,) = row_bwd("swiglu_bwd", f_swiglu, [sv["gu"]], [], [dact], [0], [], row_dtypes=[BF16])
        dwgu = matmul("tn", dgu, sv["hf"], "ffn_dwgu", out_dtype=BF16)
        gw["f_w_gate"], gw["f_w_up"] = dwgu[:FFN], dwgu[FFN:]
        dh = matmul("nn", dgu, sv["w_gu"], "ffn_dh")
        dx, dg = row_bwd("norm_bwd", f_norm_res, [sv["x2"]], [row2(p["f_norm_g"][layer])], [dh, dx], [0], [0])
        gs.setdefault("f_norm_g", [None] * DEPTH)[layer] = dg[0]
        do = matmul("nt", dx, w["xa_w_out"], "xa_do")
        gw["xa_w_out"] = matmul("tn", sv["ox"], dx, "xa_dwo", out_dtype=BF16)
        xqg, xkg = row2(p["xa_q_norm_g"][layer]), row2(p["xa_k_norm_g"][layer])
        dqx, dkv, dqg, dkg = row_bwd("xattn_bwd", f_xattn, [sv["qx"]], [sv["kv"], xqg, xkg], [do],
                                     [0], [0, 1, 2])
        gs.setdefault("xa_q_norm_g", [None] * DEPTH)[layer] = dqg[0]
        gs.setdefault("xa_k_norm_g", [None] * DEPTH)[layer] = dkg[0]
        gw["xa_w_q"] = matmul("tn", sv["hq"], dqx, "xa_dwq", out_dtype=BF16)
        dhq = matmul("nt", dqx, w["xa_w_q"], "xa_dhq")
        gw["xa_w_kv"] = matmul("tn", dkv, mem_n, "xa_dwkv", out_dtype=BF16)
        dmem_n = matmul("nn", dkv, w["xa_w_kv"], "xa_dmem", add=dmem_n)
        dx, dg = row_bwd("norm_bwd", f_norm_res, [sv["x1"]], [row2(p["xa_norm_g"][layer])], [dhq, dx], [0], [0])
        gs.setdefault("xa_norm_g", [None] * DEPTH)[layer] = dg[0]
        dx = gsink(layer, "A", gw, dx)
        w, gw = sv["wB"], {}
        if layer % 2 == 0:
            dcat = matmul("nt", dx, w["ab_w_out"], "mix_dcat")
            gw["ab_w_out"] = matmul("tn", sv["cat"], dx, "mix_dwo", out_dtype=BF16)
            dy5, dwglu, dbglu = row_bwd("glu_bwd", f_glu, [sv["y5"]], [w["s5_w_glu"], row2(p["s5_b_glu"][i])],
                                        [(dcat, S5_W, 0)], [0], [0, 1])
            gw["s5_w_glu"] = dwglu
            gs.setdefault("s5_b_glu", [None] * 2)[i] = dbglu[0]
            s5p = sv["s5p"]
            du, dare, daim, dldt, dbre, dbim, dcre, dcim, ddv = s5_bwd(sv["pm"], sv["st5"], dy5, **s5p)
            (dldt_g,) = whole("s5_dt_sum", lambda t: (jnp.sum(t, axis=1, keepdims=True),),
                              [dldt.reshape(S5_G, S5_P)], [((S5_G, 1), F32)])
            for nme, val in (("s5_a_re", dare.reshape(S5_G, S5_P)), ("s5_a_im", daim.reshape(S5_G, S5_P)),
                             ("s5_log_dt", dldt_g[:, 0]), ("s5_b_re", _s5_diag_b(dbre)),
                             ("s5_b_im", _s5_diag_b(dbim)), ("s5_c_re", _s5_diag_c(dcre)),
                             ("s5_c_im", _s5_diag_c(dcim)), ("s5_d", ddv.reshape(S5_G, S5_C))):
                gs.setdefault(nme, [None] * 2)[i] = val
            dq, dk, dv, dgate, dal, dbl, dalog, ddtb, dog = gdn_bwd(
                sv["qkvc"], sv["pm"], sv["abt"], sv["alog"], sv["dtb"], sv["og"], sv["stg"], dcat[:, S5_W:])
            (dog_s,) = whole("gdn_og_sum", lambda t: (jnp.sum(t, axis=0, keepdims=True),),
                             [dog.reshape(GDN_H, GDN_D)], [((1, GDN_D), F32)])
            gs.setdefault("gdn_out_norm_g", [None] * 2)[i] = dog_s[0]
            gs.setdefault("gdn_a_log", [None] * 2)[i] = dalog.reshape(GDN_H)
            gs.setdefault("gdn_dt_bias", [None] * 2)[i] = ddtb.reshape(GDN_H)
            dqkvc = jnp.concatenate([dq, dk, dv], axis=1)
            dqkv, dconv = conv_bwd(sv["pm"], sv["conv_w"], dqkvc)
            gs.setdefault("gdn_conv_w", [None] * 2)[i] = dconv
            dpm = jnp.concatenate([du, dqkv, dgate], axis=1).astype(BF16)
            dpab = jnp.swapaxes(jnp.concatenate([dal, dbl], axis=0)[:, :, 0], 0, 1)
            dw_main = matmul("tn", dpm, sv["h"], "proj_dw", out_dtype=BF16)
            dw_ab = matmul("tn", dpab, sv["h"], "proj_ab_dw", out_dtype=BF16)
            gw["ab_w_in"] = jnp.concatenate([dw_main, dw_ab], axis=0)
            w_in = w["ab_w_in"]
            dh = matmul("nn", dpm, w_in[:2560], "proj_dh")
            dh = matmul("nn", dpab, w_in[2560:], "proj_ab_dh", add=dh)
            dx, dg = row_bwd("norm_bwd", f_norm_res, [sv["x0"]], [row2(p["ab_norm_g"][i])], [dh, dx], [0], [0])
            gs.setdefault("ab_norm_g", [None] * 2)[i] = dg[0]
        else:
            do = matmul("nt", dx, w["c_w_out"], "mix_dcat")
            gw["c_w_out"] = matmul("tn", sv["o"], dx, "mix_dwo", out_dtype=BF16)
            dqn, dkp, dvp, dbias = cattn_bwd(sv["qn"], sv["kp"], sv["vp"], sv["bias"], _heads(do))
            gs.setdefault("c_rel_bias", [None] * 2)[i] = rel_bias_grad(dbias)
            dq3, dqg = row_bwd("headnorm_bwd", f_headnorm, [sv["q3"]], [row2(p["c_q_norm_g"][i])], [dqn], [0], [0])
            dk3, dkg = row_bwd("headnorm_bwd", f_headnorm, [sv["k3"]], [row2(p["c_k_norm_g"][i])],
                               [dkp[:, CA_PAD:]], [0], [0])
            gs.setdefault("c_q_norm_g", [None] * 2)[i] = dqg[0]
            gs.setdefault("c_k_norm_g", [None] * 2)[i] = dkg[0]
            dqkv = jnp.concatenate([_unheads(dq3), _unheads(dk3), _unheads(dvp[:, CA_PAD:])], axis=1).astype(BF16)
            gw["c_w_qkv"] = matmul("tn", dqkv, sv["h"], "proj_qkv_dw", out_dtype=BF16)
            dh = matmul("nn", dqkv, w["c_w_qkv"], "proj_qkv_dh")
            dx, dg = row_bwd("norm_bwd", f_norm_res, [sv["x0"]], [row2(w["c_norm_g"])], [dh, dx], [0], [0])
            gs.setdefault("c_norm_g", [None] * 2)[i] = dg[0]
        dx = gsink(layer, "B", gw, dx)
    (dmg,) = row_bwd("mem_norm_bwd", f_norm, [mem], [row2(p["mem_norm_g"])], [dmem_n], [], [0])
    small = {n: jnp.stack(v) for n, v in gs.items()}
    small["mem_norm_g"] = dmg[0]
    return loss_vec, dx, small


def adam(w, g, m, v, name):
    shape = w.shape
    cols = shape[-1]
    w2, g2, m2, v2 = (t.reshape(-1, cols) for t in (w, g, m, v))
    rows = w2.shape[0]
    tile = rows if rows <= 512 else _tile(rows, 512, 8)
    outs = row_fwd(name, f_adam, [w2, g2, m2, v2], [], [(cols, F32)] * 3, tile=tile)
    return tuple(o.reshape(shape) for o in outs)


WEIGHTS = ['ab_norm_g', 'ab_w_in', 'ab_w_out', 's5_a_re', 's5_a_im', 's5_log_dt', 's5_b_re', 's5_b_im', 's5_c_re',
           's5_c_im', 's5_d', 's5_w_glu', 's5_b_glu', 'gdn_conv_w', 'gdn_a_log', 'gdn_dt_bias', 'gdn_out_norm_g',
           'c_norm_g', 'c_w_qkv', 'c_w_out', 'c_q_norm_g', 'c_k_norm_g', 'c_rel_bias', 'mem_norm_g', 'xa_norm_g',
           'xa_w_q', 'xa_w_kv', 'xa_w_out', 'xa_q_norm_g', 'xa_k_norm_g', 'f_norm_g', 'f_w_gate', 'f_w_up',
           'f_w_down']
SHARDED_SMALL = {"gdn_conv_w": (2, 384), "c_norm_g": (1, 256)}


def kernel(x, mem, ab_norm_g, ab_w_in, ab_w_out, s5_a_re, s5_a_im, s5_log_dt, s5_b_re, s5_b_im, s5_c_re, s5_c_im, s5_d, s5_w_glu, s5_b_glu, gdn_conv_w, gdn_a_log, gdn_dt_bias, gdn_out_norm_g, c_norm_g, c_w_qkv, c_w_out, c_q_norm_g, c_k_norm_g, c_rel_bias, mem_norm_g, xa_norm_g, xa_w_q, xa_w_kv, xa_w_out, xa_q_norm_g, xa_k_norm_g, f_norm_g, f_w_gate, f_w_up, f_w_down, loss_target, m_ab_norm_g, m_ab_w_in, m_ab_w_out, m_s5_a_re, m_s5_a_im, m_s5_log_dt, m_s5_b_re, m_s5_b_im, m_s5_c_re, m_s5_c_im, m_s5_d, m_s5_w_glu, m_s5_b_glu, m_gdn_conv_w, m_gdn_a_log, m_gdn_dt_bias, m_gdn_out_norm_g, m_c_norm_g, m_c_w_qkv, m_c_w_out, m_c_q_norm_g, m_c_k_norm_g, m_c_rel_bias, m_mem_norm_g, m_xa_norm_g, m_xa_w_q, m_xa_w_kv, m_xa_w_out, m_xa_q_norm_g, m_xa_k_norm_g, m_f_norm_g, m_f_w_gate, m_f_w_up, m_f_w_down, v_ab_norm_g, v_ab_w_in, v_ab_w_out, v_s5_a_re, v_s5_a_im, v_s5_log_dt, v_s5_b_re, v_s5_b_im, v_s5_c_re, v_s5_c_im, v_s5_d, v_s5_w_glu, v_s5_b_glu, v_gdn_conv_w, v_gdn_a_log, v_gdn_dt_bias, v_gdn_out_norm_g, v_c_norm_g, v_c_w_qkv, v_c_w_out, v_c_q_norm_g, v_c_k_norm_g, v_c_rel_bias, v_mem_norm_g, v_xa_norm_g, v_xa_w_q, v_xa_w_kv, v_xa_w_out, v_xa_q_norm_g, v_xa_k_norm_g, v_f_norm_g, v_f_w_gate, v_f_w_up, v_f_w_down):
    args = locals()
    p = {n: args[n] for n in WEIGHTS}
    m = {n: args["m_" + n] for n in WEIGHTS}
    v = {n: args["v_" + n] for n in WEIGHTS}
    chip = 2 * lax.axis_index("x") + lax.axis_index("y")

    carry = x[0]
    gathers = {}
    for layer in range(DEPTH):
        for grp in GROUPS:
            src = pack_group_shards(p, layer, grp)
            land = lax.dynamic_update_slice(lax.empty((N_CHIPS,) + src.shape, BF16), src[None], (chip, 0, 0))
            send_sems, recv_sems, src, land, carry = exchange_start(
                src, land, carry, f"gather_start_{layer}{grp}", per_dest=False)
            gathers[layer, grp] = (src, land, send_sems, recv_sems)

    def wsrc(layer, grp, after):
        src, land, send_sems, recv_sems = gathers[layer, grp]
        _, land = exchange_wait(src, land, send_sems, recv_sems, after, f"gather_wait_{layer}{grp}", per_dest=False)
        return unpack_group_gathered(land, layer, grp)

    scatters = []

    def gsink(layer, grp, gw, carry):
        src = pack_group_grads(gw, layer, grp)
        land = lax.empty((3,) + src.shape[1:], BF16)
        send_sems, recv_sems, src, land, carry = exchange_start(
            src, land, carry, f"scatter_start_{layer}{grp}", per_dest=True)
        scatters.append((layer, grp, src, land, send_sems, recv_sems))
        return carry

    loss_vec, dx, g_small = local_step(carry, mem[0], loss_target[0], p, wsrc, gsink)

    parts = []
    for layer, grp, src, land, send_sems, recv_sems in scatters:
        src, land = exchange_wait(src, land, send_sems, recv_sems, dx, f"scatter_wait_{layer}{grp}", per_dest=True)
        own = lax.dynamic_index_in_dim(src, chip, axis=0, keepdims=False)
        parts.append(sum_own_slabs(own, land, "sum_chips"))
    others = sibling_exchange(parts, "sibling_grads")
    per_layer = {}
    for (layer, grp, *_), part, other in zip(scatters, parts, others):
        (total,) = row_fwd("sum_cores", lambda a, b: (a + b,), [part, other], [], [(D, F32)],
                           tile=_tile(part.shape[0], 512, 16))
        for name, g in unpack_group_reduced(total, layer, grp).items():
            per_layer.setdefault(name, {})[layer] = g
    grads = {name: jnp.stack([d[k] for k in sorted(d)]) for name, d in per_layer.items()}

    full_shapes = {n: ((2, 4, 1536) if n == "gdn_conv_w" else (2, D) if n == "c_norm_g" else p[n].shape)
                   for n in SMALL}
    small_all = all_exchange(pack_small(g_small, extra=loss_vec), "gather_small")
    small_sum = sum_slabs(small_all, "sum_small")
    g_s, rest = unpack_small(small_sum, full_shapes)
    (loss11,) = whole("loss_sum", lambda t: (jnp.sum(jnp.sum(t, axis=1, keepdims=True), axis=0, keepdims=True),),
                      [rest[:D // 128]], [((1, 1), F32)])
    for n, (axis, width) in SHARDED_SMALL.items():
        g_s[n] = lax.dynamic_slice_in_dim(g_s[n], chip * width, width, axis=axis)
    grads.update(g_s)

    shard_shapes = {n: p[n].shape for n in SMALL}
    packs = [pack_small(d) for d in (p, grads, m, v)]
    d_s, m_s, v_s = adam(*packs, name="adam_small")
    delta, new_m, new_v = {}, {}, {}
    for dst, src in ((delta, d_s), (new_m, m_s), (new_v, v_s)):
        dst.update(unpack_small(src, shard_shapes)[0])
    for name, _, _, _ in BIG:
        delta[name], new_m[name], new_v[name] = adam(p[name], grads[name], m[name], v[name], "adam_" + name)

    return (loss11[0, 0], dx[None], *[grads[n] for n in WEIGHTS], *[delta[n] for n in WEIGHTS],
            *[new_m[n] for n in WEIGHTS], *[new_v[n] for n in WEIGHTS])
```

```python
import functools
import math

import numpy as np
import jax
import jax.numpy as jnp
from jax import lax
from jax.experimental import pallas as pl
from jax.experimental.pallas import tpu as pltpu

F32 = jnp.float32
BF16 = jnp.bfloat16
MESH = pl.DeviceIdType.MESH

D = 1024
CHUNK = 64
N_MEM = 256
EPS = 1e-6
S5_W = 512
S5_G = 32
S5_C = 16
S5_P = 64
S5_GB = 4
GDN_H = 4
GDN_D = 128
CA_H = 16
CA_D = 64
CA_LEFT = 8
CA_BAND = (CA_LEFT + 1) * CHUNK
CA_PAD = CA_LEFT * CHUNK
MAX_REL = 128
XA_H = 4
XA_D = 256
FFN = 2816
DEPTH = 4
N_CHIPS = 4
N_DEV = 8
LR, B1, B2, AEPS, WD, STEP = 0.001, 0.9, 0.999, 1e-08, 0.01, 10

VMEM_LIMIT = 56 * 1024 * 1024
ROW_TILE = 256
HI = lax.Precision.HIGHEST


def _cp(*sem):
    return pltpu.CompilerParams(dimension_semantics=sem, vmem_limit_bytes=VMEM_LIMIT)


def _dg(a, b, ca, cb):
    return lax.dot_general(a.astype(BF16), b.astype(BF16), (((ca,), (cb,)), ((), ())),
                           preferred_element_type=F32)


@jax.custom_vjp
def mm(a, b):
    return _dg(a, b, 1, 0)


@jax.custom_vjp
def mm_nt(a, b):
    return _dg(a, b, 1, 1)


@jax.custom_vjp
def mm_tn(a, b):
    return _dg(a, b, 0, 0)


mm.defvjp(lambda a, b: (mm(a, b), (a, b)), lambda r, g: (mm_nt(g, r[1]), mm_tn(r[0], g)))
mm_nt.defvjp(lambda a, b: (mm_nt(a, b), (a, b)), lambda r, g: (mm(g, r[1]), mm_tn(g, r[0])))
mm_tn.defvjp(lambda a, b: (mm_tn(a, b), (a, b)), lambda r, g: (mm_nt(r[1], g), mm(r[0], g)))


def _bdg(a, b, ca, cb):
    return lax.dot_general(a.astype(BF16), b.astype(BF16), (((ca,), (cb,)), ((0,), (0,))),
                           preferred_element_type=F32)


@jax.custom_vjp
def bmm(a, b):
    return _bdg(a, b, 2, 1)


@jax.custom_vjp
def bmm_nt(a, b):
    return _bdg(a, b, 2, 2)


@jax.custom_vjp
def bmm_tn(a, b):
    return _bdg(a, b, 1, 1)


bmm.defvjp(lambda a, b: (bmm(a, b), (a, b)), lambda r, g: (bmm_nt(g, r[1]), bmm_tn(r[0], g)))
bmm_nt.defvjp(lambda a, b: (bmm_nt(a, b), (a, b)), lambda r, g: (bmm(g, r[1]), bmm_tn(g, r[0])))
bmm_tn.defvjp(lambda a, b: (bmm_tn(a, b), (a, b)), lambda r, g: (bmm_nt(r[1], g), bmm(r[0], g)))


def _split(a):
    hi = a.astype(BF16)
    return hi, (a - hi.astype(F32)).astype(BF16)


def _dg3(a, b, ca, cb):
    (ah, al), (bh, bl) = _split(a), _split(b)
    d = lambda u, v: lax.dot_general(u, v, (((ca,), (cb,)), ((), ())), preferred_element_type=F32)
    return d(ah, bh) + (d(ah, bl) + d(al, bh))


@jax.custom_vjp
def mm3(a, b):
    return _dg3(a, b, 1, 0)


@jax.custom_vjp
def mm3_nt(a, b):
    return _dg3(a, b, 1, 1)


@jax.custom_vjp
def mm3_tn(a, b):
    return _dg3(a, b, 0, 0)


mm3.defvjp(lambda a, b: (mm3(a, b), (a, b)), lambda r, g: (mm3_nt(g, r[1]), mm3_tn(r[0], g)))
mm3_nt.defvjp(lambda a, b: (mm3_nt(a, b), (a, b)), lambda r, g: (mm3(g, r[1]), mm3_tn(g, r[0])))
mm3_tn.defvjp(lambda a, b: (mm3_tn(a, b), (a, b)), lambda r, g: (mm3_nt(r[1], g), mm3(r[0], g)))


def _tri_mm(v, upper):
    T = v.shape[0]
    r = lax.broadcasted_iota(jnp.int32, (T, T), 0)
    c = lax.broadcasted_iota(jnp.int32, (T, T), 1)
    m = ((c >= r) if upper else (r >= c)).astype(BF16)
    hi, lo = _split(v)
    d = lambda u: lax.dot_general(m, u, (((1,), (0,)), ((), ())), preferred_element_type=F32)
    return d(hi) + d(lo)


@jax.custom_vjp
def cumsum_rows(v):
    return _tri_mm(v, False)


cumsum_rows.defvjp(lambda v: (_tri_mm(v, False), None), lambda _, g: (_tri_mm(g, True),))


def _rms(x, g):
    return x * lax.rsqrt(jnp.mean(x * x, axis=-1, keepdims=True) + EPS) * g


def _softmax(s):
    e = jnp.exp(s - lax.stop_gradient(jnp.max(s, axis=-1, keepdims=True)))
    return e / jnp.sum(e, axis=-1, keepdims=True)


def _softplus(x):
    return jnp.maximum(x, 0.0) + jnp.log(1.0 + jnp.exp(-jnp.abs(x)))


def _tile(n, cap, align):
    if n <= cap:
        return n
    best = None
    for d in range(align, cap + 1, align):
        if n % d == 0:
            best = d
    assert best is not None, (n, cap, align)
    return best


def matmul(mode, a, b, name, out_dtype=F32, add=None):
    if mode == "nn":
        (M, K), (K2, N) = a.shape, b.shape
    elif mode == "nt":
        (M, K), (N, K2) = a.shape, b.shape
    else:
        (K, M), (K2, N) = a.shape, b.shape
    assert K == K2, (mode, a.shape, b.shape)
    tm, tn, tk = _tile(M, 512, 128), _tile(N, 1536, 128), _tile(K, 2048, 128)
    nk = K // tk
    if mode == "nn":
        a_spec = pl.BlockSpec((tm, tk), lambda i, j, k: (i, k))
        b_spec = pl.BlockSpec((tk, tn), lambda i, j, k: (k, j))
        dn = (((1,), (0,)), ((), ()))
    elif mode == "nt":
        a_spec = pl.BlockSpec((tm, tk), lambda i, j, k: (i, k))
        b_spec = pl.BlockSpec((tn, tk), lambda i, j, k: (j, k))
        dn = (((1,), (1,)), ((), ()))
    else:
        a_spec = pl.BlockSpec((tk, tm), lambda i, j, k: (k, i))
        b_spec = pl.BlockSpec((tk, tn), lambda i, j, k: (k, j))
        dn = (((0,), (0,)), ((), ()))
    o_spec = pl.BlockSpec((tm, tn), lambda i, j, k: (i, j))
    has_add = add is not None

    def body(*refs):
        a_ref, b_ref = refs[0], refs[1]
        add_ref = refs[2] if has_add else None
        o_ref = refs[3] if has_add else refs[2]
        acc_ref = refs[-1]
        p = lax.dot_general(a_ref[...].astype(BF16), b_ref[...].astype(BF16), dn,
                            preferred_element_type=F32)

        def finish(total):
            if has_add:
                total = total + add_ref[...]
            o_ref[...] = total.astype(o_ref.dtype)

        if nk == 1:
            finish(p)
        else:
            k = pl.program_id(2)

            @pl.when(k == 0)
            def _():
                acc_ref[...] = p

            @pl.when(k > 0)
            def _():
                acc_ref[...] += p

            @pl.when(k == nk - 1)
            def _():
                finish(acc_ref[...])

    ins = [a, b] + ([add] if has_add else [])
    specs = [a_spec, b_spec] + ([o_spec] if has_add else [])
    return pl.pallas_call(
        body, name=name, grid=(M // tm, N // tn, nk), in_specs=specs, out_specs=o_spec,
        out_shape=jax.ShapeDtypeStruct((M, N), out_dtype),
        scratch_shapes=[pltpu.VMEM((tm, tn), F32)],
        compiler_params=_cp("parallel", "parallel", "arbitrary"),
    )(*ins)


def _row_spec(arr, tile):
    if isinstance(arr, tuple):
        a, w, cb = arr[:3]
        ro = (arr[3] // tile) if len(arr) > 3 else 0
        assert len(arr) < 4 or arr[3] % tile == 0
        return a, pl.BlockSpec((tile, w), lambda i, cb=cb, ro=ro: (i + ro, cb)), (tile, w)
    if arr.ndim == 3:
        d0, _, d2 = arr.shape
        return arr, pl.BlockSpec((d0, tile, d2), lambda i: (0, i, 0)), (d0, tile, d2)
    return arr, pl.BlockSpec((tile, arr.shape[1]), lambda i: (i, 0)), (tile, arr.shape[1])


def _full_spec(arr):
    nd = arr.ndim
    return pl.BlockSpec(arr.shape, lambda i, nd=nd: (0,) * nd)


def _n_rows(arr):
    a = arr[0] if isinstance(arr, tuple) else arr
    return a.shape[1] if a.ndim == 3 else a.shape[0]


def _row_out_shape(shape_tail, n, dtype):
    if isinstance(shape_tail, tuple):
        d0, d2 = shape_tail
        return (jax.ShapeDtypeStruct((d0, n, d2), dtype),
                lambda tile: pl.BlockSpec((d0, tile, d2), lambda i: (0, i, 0)))
    return (jax.ShapeDtypeStruct((n, shape_tail), dtype),
            lambda tile: pl.BlockSpec((tile, shape_tail), lambda i: (i, 0)))


def _f32(v):
    return v.astype(F32) if v.dtype == BF16 else v


def row_fwd(name, fn, rows, fulls, outs, tile=ROW_TILE):
    n = _n_rows(rows[0])
    tile = min(tile, n)
    assert n % tile == 0, (name, n, tile)
    rs = [_row_spec(r, tile) for r in rows]
    os_ = [_row_out_shape(w, n, dt) for w, dt in outs]
    nr, nf = len(rows), len(fulls)

    def body(*refs):
        vals = [_f32(r[...]) for r in refs[:nr + nf]]
        res = fn(*vals)
        for r, v in zip(refs[nr + nf:], res):
            r[...] = v.astype(r.dtype)

    out = pl.pallas_call(
        body, name=name, grid=(n // tile,),
        in_specs=[s for _, s, _ in rs] + [_full_spec(f) for f in fulls],
        out_specs=[mk(tile) for _, mk in os_], out_shape=[sh for sh, _ in os_],
        compiler_params=_cp("parallel"),
    )(*[a for a, _, _ in rs], *fulls)
    return out


def row_bwd(name, fn, rows, fulls, cts, want_rows, want_fulls, row_dtypes=None, tile=ROW_TILE):
    n = _n_rows(rows[0])
    tile = min(tile, n)
    assert n % tile == 0, (name, n, tile)
    rs = [_row_spec(r, tile) for r in rows]
    cs = [_row_spec(c, tile) for c in cts]
    nr, nf, nc = len(rows), len(fulls), len(cts)
    row_dtypes = row_dtypes or [F32] * len(want_rows)
    out_shapes, out_specs = [], []
    for k, idx in enumerate(want_rows):
        a, _, blk = rs[idx]
        if len(blk) == 3:
            sh, mk = _row_out_shape((blk[0], blk[2]), n, row_dtypes[k])
        else:
            sh, mk = _row_out_shape(blk[1], n, row_dtypes[k])
        out_shapes.append(sh)
        out_specs.append(mk(tile))
    for idx in want_fulls:
        out_shapes.append(jax.ShapeDtypeStruct(fulls[idx].shape, F32))
        out_specs.append(_full_spec(fulls[idx]))
    n_wr = len(want_rows)

    def body(*refs):
        i = pl.program_id(0)
        vals = [_f32(r[...]) for r in refs[:nr + nf]]
        ct_vals = [_f32(r[...]) for r in refs[nr + nf:nr + nf + nc]]
        outs = refs[nr + nf + nc:]
        _, vjp = jax.vjp(fn, *vals)
        grads = vjp(tuple(ct_vals))
        for k, idx in enumerate(want_rows):
            outs[k][...] = grads[idx].astype(outs[k].dtype)
        for k, idx in enumerate(want_fulls):
            o = outs[n_wr + k]
            g = grads[nr + idx]

            @pl.when(i == 0)
            def _(o=o, g=g):
                o[...] = g

            @pl.when(i > 0)
            def _(o=o, g=g):
                o[...] += g

    out = pl.pallas_call(
        body, name=name, grid=(n // tile,),
        in_specs=[s for _, s, _ in rs] + [_full_spec(f) for f in fulls] + [s for _, s, _ in cs],
        out_specs=out_specs, out_shape=out_shapes,
        compiler_params=_cp("arbitrary"),
    )(*[a for a, _, _ in rs], *fulls, *[a for a, _, _ in cs])
    return out


def whole(name, fn, args, outs):
    def body(*refs):
        res = fn(*[r[...] for r in refs[:len(args)]])
        for r, v in zip(refs[len(args):], res):
            r[...] = v.astype(r.dtype)

    return pl.pallas_call(
        body, name=name, out_shape=[jax.ShapeDtypeStruct(s, d) for s, d in outs],
        compiler_params=pltpu.CompilerParams(vmem_limit_bytes=VMEM_LIMIT),
    )(*args)


def f_norm(x, g):
    return (_rms(x, g),)


def f_norm_res(x, g):
    return _rms(x, g), x


def f_swiglu(g, u):
    return (g * jax.nn.sigmoid(g) * u,)


def f_glu(y, w, b):
    h = jax.nn.gelu(y)
    return (h * jax.nn.sigmoid(mm(h, w) + b),)


def f_xattn(q, kv, qg, kg):
    outs = []
    for h in range(XA_H):
        sl = slice(h * XA_D, (h + 1) * XA_D)
        qn = _rms(q[:, sl], qg)
        kn = _rms(kv[:, sl], kg)
        vh = kv[:, D + h * XA_D:D + (h + 1) * XA_D]
        p = _softmax(mm_nt(qn, kn) * (XA_D ** -0.5))
        outs.append(mm(p, vh))
    return (jnp.concatenate(outs, axis=-1),)


def f_adam(w, g, m, v):
    m2 = B1 * m + (1.0 - B1) * g
    v2 = B2 * v + (1.0 - B2) * (g * g)
    m_hat = m2 / (1.0 - B1 ** STEP)
    v_hat = v2 / (1.0 - B2 ** STEP)
    delta = -LR * (m_hat / (jnp.sqrt(v_hat) + AEPS) + WD * w)
    return delta, m2, v2


S5_NTAB, S5_NROW = 4, 6


def _s5_tables(are, aim, ldt):
    T = CHUNK
    dt = jnp.exp(ldt)
    ar, ai = are * dt, aim * dt
    t = lax.broadcasted_iota(jnp.int32, (T, 1), 0).astype(F32)
    mag, inv = jnp.exp(t * ar), jnp.exp(-t * ar)
    cs, sn = jnp.cos(t * ai), jnp.sin(t * ai)
    e_re, e_im = mag * cs, mag * sn
    n_re, n_im = inv * cs, -inv * sn
    l_re, l_im = jnp.exp(ar) * jnp.cos(ai), jnp.exp(ar) * jnp.sin(ai)
    den = are * are + aim * aim
    k_re = ((l_re - 1.0) * are + l_im * aim) / den
    k_im = (l_im * are - (l_re - 1.0) * aim) / den
    tl = float(T - 1)
    m_re, m_im = jnp.exp(tl * ar) * jnp.cos(tl * ai), jnp.exp(tl * ar) * jnp.sin(tl * ai)
    return (e_re, e_im, n_re, n_im), (l_re, l_im, k_re, k_im, m_re, m_im)


def _s5_chunk(u, sre, sim, tabs, rows, b_re, b_im, c_re, c_im, dv):
    e_re, e_im, n_re, n_im = tabs
    l_re, l_im, k_re, k_im, m_re, m_im = rows
    x_re, x_im = mm(u, b_re), mm(u, b_im)
    bu_re = k_re * x_re - k_im * x_im
    bu_im = k_re * x_im + k_im * x_re
    v_re = bu_re * n_re - bu_im * n_im
    v_im = bu_re * n_im + bu_im * n_re
    p_re = l_re * sre - l_im * sim
    p_im = l_re * sim + l_im * sre
    w_re = cumsum_rows(v_re) + p_re
    w_im = cumsum_rows(v_im) + p_im
    s_re = e_re * w_re - e_im * w_im
    s_im = e_re * w_im + e_im * w_re
    y = mm(s_re, c_re) - mm(s_im, c_im) + dv * u
    z_re = jnp.sum(v_re, axis=0, keepdims=True) + p_re
    z_im = jnp.sum(v_im, axis=0, keepdims=True) + p_im
    return y, m_re * z_re - m_im * z_im, m_re * z_im + m_im * z_re


def _s5_fill_tables(are_ref, aim_ref, ldt_ref, tab, row):
    for g in range(S5_GB):
        ls = slice(512 * g, 512 * (g + 1))
        tabs, rows = _s5_tables(are_ref[:, ls], aim_ref[:, ls], ldt_ref[:, ls])
        for k, t in enumerate(tabs):
            tab[k, :, ls] = t
        for k, r in enumerate(rows):
            row[k:k + 1, ls] = r


def _s5_read_tables(tab, row, ls):
    return (tuple(tab[k, :, ls] for k in range(S5_NTAB)), tuple(row[k:k + 1, ls] for k in range(S5_NROW)))


def _s5_specs(nc, rev):
    T = CHUNK

    def ci(c):
        return nc - 1 - c if rev else c

    u_spec = pl.BlockSpec((T, S5_W), lambda c: (ci(c), 0))
    p_spec = pl.BlockSpec((1, S5_G * S5_P), lambda c: (0, 0))
    b_spec = pl.BlockSpec((S5_GB, 128, 512), lambda c: (0, 0, 0))
    c_spec = pl.BlockSpec((S5_GB, 512, 128), lambda c: (0, 0, 0))
    d_spec = pl.BlockSpec((1, S5_W), lambda c: (0, 0))
    st_spec = pl.BlockSpec((None, 2, S5_G * S5_P), lambda c: (ci(c), 0, 0))
    return u_spec, p_spec, b_spec, c_spec, d_spec, st_spec


def s5_fwd(pm, are, aim, ldt, b_re, b_im, c_re, c_im, dv):
    S = pm.shape[0]
    nc = S // CHUNK
    u_spec, p_spec, b_spec, c_spec, d_spec, st_spec = _s5_specs(nc, False)

    def body(u_ref, are_ref, aim_ref, ldt_ref, bre_ref, bim_ref, cre_ref, cim_ref, dv_ref,
             y_ref, st_ref, state, tab, row):
        c = pl.program_id(0)

        @pl.when(c == 0)
        def _():
            state[...] = jnp.zeros_like(state)
            _s5_fill_tables(are_ref, aim_ref, ldt_ref, tab, row)

        st_ref[...] = state[...]
        for g in range(S5_GB):
            lu, ls = slice(128 * g, 128 * (g + 1)), slice(512 * g, 512 * (g + 1))
            tabs, rows = _s5_read_tables(tab, row, ls)
            y, e_re, e_im = _s5_chunk(u_ref[:, lu], state[0:1, ls], state[1:2, ls], tabs, rows,
                                      bre_ref[g], bim_ref[g], cre_ref[g], cim_ref[g], dv_ref[:, lu])
            y_ref[:, lu] = y
            state[0:1, ls] = e_re
            state[1:2, ls] = e_im

    n_state = S5_G * S5_P
    return pl.pallas_call(
        body, name="s5_fwd", grid=(nc,),
        in_specs=[u_spec, p_spec, p_spec, p_spec, b_spec, b_spec, c_spec, c_spec, d_spec],
        out_specs=[u_spec, st_spec],
        out_shape=[jax.ShapeDtypeStruct((S, S5_W), F32), jax.ShapeDtypeStruct((nc, 2, n_state), F32)],
        scratch_shapes=[pltpu.VMEM((2, n_state), F32), pltpu.VMEM((S5_NTAB, CHUNK, n_state), F32),
                        pltpu.VMEM((8, n_state), F32)],
        compiler_params=_cp("arbitrary"),
    )(pm, are, aim, ldt, b_re, b_im, c_re, c_im, dv)


def s5_bwd(pm, st, dy, are, aim, ldt, b_re, b_im, c_re, c_im, dv):
    S = pm.shape[0]
    nc = S // CHUNK
    u_spec, p_spec, b_spec, c_spec, d_spec, st_spec = _s5_specs(nc, True)

    def body(u_ref, st_ref, dy_ref, are_ref, aim_ref, ldt_ref, bre_ref, bim_ref, cre_ref, cim_ref, dv_ref,
             du_ref, dare_ref, daim_ref, dldt_ref, dbre_ref, dbim_ref, dcre_ref, dcim_ref, ddv_ref,
             dstate, tab, row, dtab, drow):
        c = pl.program_id(0)

        @pl.when(c == 0)
        def _():
            dstate[...] = jnp.zeros_like(dstate)
            dtab[...] = jnp.zeros_like(dtab)
            drow[...] = jnp.zeros_like(drow)
            _s5_fill_tables(are_ref, aim_ref, ldt_ref, tab, row)

        for g in range(S5_GB):
            lu, ls = slice(128 * g, 128 * (g + 1)), slice(512 * g, 512 * (g + 1))
            every = slice(None)
            tabs, rows = _s5_read_tables(tab, row, ls)
            args = (u_ref[:, lu], st_ref[0:1, ls], st_ref[1:2, ls], tabs, rows,
                    bre_ref[g], bim_ref[g], cre_ref[g], cim_ref[g], dv_ref[:, lu])
            _, vjp = jax.vjp(_s5_chunk, *args)
            gr = vjp((dy_ref[:, lu], dstate[0:1, ls], dstate[1:2, ls]))
            du_ref[:, lu] = gr[0]
            dstate[0:1, ls] = gr[1]
            dstate[1:2, ls] = gr[2]
            for k, t in enumerate(gr[3]):
                dtab[k, :, ls] += t
            for k, r in enumerate(gr[4]):
                drow[k:k + 1, ls] += r
            accs = ((dbre_ref, (g,)), (dbim_ref, (g,)), (dcre_ref, (g,)), (dcim_ref, (g,)), (ddv_ref, (every, lu)))
            for (o, idx), gv in zip(accs, gr[5:]):
                @pl.when(c == 0)
                def _(o=o, idx=idx, gv=gv):
                    o[idx] = gv

                @pl.when(c > 0)
                def _(o=o, idx=idx, gv=gv):
                    o[idx] += gv

        @pl.when(c == nc - 1)
        def _():
            for g in range(S5_GB):
                ls = slice(512 * g, 512 * (g + 1))
                _, vjp = jax.vjp(_s5_tables, are_ref[:, ls], aim_ref[:, ls], ldt_ref[:, ls])
                dtabs, drows = _s5_read_tables(dtab, drow, ls)
                ga, gi, gl = vjp((dtabs, drows))
                dare_ref[:, ls] = ga
                daim_ref[:, ls] = gi
                dldt_ref[:, ls] = gl

    n_state = S5_G * S5_P
    return pl.pallas_call(
        body, name="s5_bwd", grid=(nc,),
        in_specs=[u_spec, st_spec, u_spec, p_spec, p_spec, p_spec, b_spec, b_spec, c_spec, c_spec, d_spec],
        out_specs=[u_spec, p_spec, p_spec, p_spec, b_spec, b_spec, c_spec, c_spec, d_spec],
        out_shape=[jax.ShapeDtypeStruct((S, S5_W), F32)] + [jax.ShapeDtypeStruct((1, n_state), F32)] * 3
        + [jax.ShapeDtypeStruct((S5_GB, 128, 512), F32)] * 2 + [jax.ShapeDtypeStruct((S5_GB, 512, 128), F32)] * 2
        + [jax.ShapeDtypeStruct((1, S5_W), F32)],
        scratch_shapes=[pltpu.VMEM((2, n_state), F32), pltpu.VMEM((S5_NTAB, CHUNK, n_state), F32),
                        pltpu.VMEM((8, n_state), F32), pltpu.VMEM((S5_NTAB, CHUNK, n_state), F32),
                        pltpu.VMEM((8, n_state), F32)],
        compiler_params=_cp("arbitrary"),
    )(pm, st, dy, are, aim, ldt, b_re, b_im, c_re, c_im, dv)


def conv_fwd(pm, w):
    S = pm.shape[0]

    def body(x_ref, w_ref, o_ref, pad):
        x = x_ref[...]
        pad[0:8, :] = jnp.zeros((8, 128), F32)
        pad[8:, :] = x
        y = (w_ref[3:4, :] * x + w_ref[2:3, :] * pad[7:7 + S, :] + w_ref[1:2, :] * pad[6:6 + S, :]
             + w_ref[0:1, :] * pad[5:5 + S, :])
        o_ref[...] = y * jax.nn.sigmoid(y)

    return pl.pallas_call(
        body, name="conv_fwd", grid=(12,),
        in_specs=[pl.BlockSpec((S, 128), lambda j: (0, 4 + j)), pl.BlockSpec((4, 128), lambda j: (0, j))],
        out_specs=pl.BlockSpec((S, 128), lambda j: (0, j)),
        out_shape=jax.ShapeDtypeStruct((S, 1536), F32),
        scratch_shapes=[pltpu.VMEM((S + 8, 128), F32)],
        compiler_params=_cp("parallel"),
    )(pm, w)


def conv_bwd(pm, w, dout):
    S = pm.shape[0]

    def body(x_ref, w_ref, do_ref, dx_ref, dw_ref, pad, dpad):
        x = x_ref[...]
        pad[0:8, :] = jnp.zeros((8, 128), F32)
        pad[8:, :] = x
        xs = [pad[5:5 + S, :], pad[6:6 + S, :], pad[7:7 + S, :], x]
        y = w_ref[0:1, :] * xs[0] + w_ref[1:2, :] * xs[1] + w_ref[2:3, :] * xs[2] + w_ref[3:4, :] * xs[3]
        sg = jax.nn.sigmoid(y)
        dy = do_ref[...] * (sg + y * sg * (1.0 - sg))
        dpad[0:S, :] = dy
        dpad[S:, :] = jnp.zeros((8, 128), F32)
        dx_ref[...] = (w_ref[3:4, :] * dy + w_ref[2:3, :] * dpad[1:1 + S, :] + w_ref[1:2, :] * dpad[2:2 + S, :]
                       + w_ref[0:1, :] * dpad[3:3 + S, :])
        for i in range(4):
            dw_ref[i:i + 1, :] = jnp.sum(dy * xs[i], axis=0, keepdims=True)

    return pl.pallas_call(
        body, name="conv_bwd", grid=(12,),
        in_specs=[pl.BlockSpec((S, 128), lambda j: (0, 4 + j)), pl.BlockSpec((4, 128), lambda j: (0, j)),
                  pl.BlockSpec((S, 128), lambda j: (0, j))],
        out_specs=[pl.BlockSpec((S, 128), lambda j: (0, j)), pl.BlockSpec((4, 128), lambda j: (0, j))],
        out_shape=[jax.ShapeDtypeStruct((S, 1536), F32), jax.ShapeDtypeStruct((4, 1536), F32)],
        scratch_shapes=[pltpu.VMEM((S + 8, 128), F32), pltpu.VMEM((S + 8, 128), F32)],
        compiler_params=_cp("parallel"),
    )(pm, w, dout)


GDN_SUP = 4
GDN_ROWS = GDN_SUP * CHUNK


def _gdn_chunk(q, k, v, gate, al, bl, alog, dtb, og, state):
    R = q.shape[0]
    r = lax.broadcasted_iota(jnp.int32, (R, R), 0)
    c = lax.broadcasted_iota(jnp.int32, (R, R), 1)
    same = (r // CHUNK) == (c // CHUNK)
    eye = (r == c).astype(F32)
    strict, causal, upper = same & (r > c), same & (r >= c), same & (r <= c)
    qn = q * lax.rsqrt(jnp.sum(q * q, axis=-1, keepdims=True) + EPS) * (GDN_D ** -0.5)
    kn = k * lax.rsqrt(jnp.sum(k * k, axis=-1, keepdims=True) + EPS)
    beta = jax.nn.sigmoid(bl)
    g = -jnp.exp(alog) * _softplus(al + dtb)
    g_row = jnp.sum(eye * g, axis=0, keepdims=True)
    gc_col = jnp.sum(jnp.where(causal, g_row, 0.0), axis=1, keepdims=True)
    gc_row = jnp.sum(jnp.where(upper, g, 0.0), axis=0, keepdims=True)
    gtot = jnp.sum(jnp.where(same, g_row, 0.0), axis=1, keepdims=True)
    gamma = jnp.exp(gc_col)
    diff = gc_col - gc_row
    d_strict = jnp.where(strict, jnp.exp(jnp.where(strict, diff, 0.0)), 0.0)
    d_causal = jnp.where(causal, jnp.exp(jnp.where(causal, diff, 0.0)), 0.0)
    a = beta * mm_nt(kn, kn) * d_strict
    p = -a
    x = eye + p
    for _ in range(5):
        p = mm(p, p)
        x = x + mm(x, p)
    u_new = mm(x, beta * v)
    w_k = mm(x, (beta * gamma) * kn)
    qk = mm_nt(qn, kn) * d_causal
    q_g = qn * gamma
    k_tail = kn * jnp.exp(gtot - gc_col)
    ws, os_ = [], []
    for i in range(R // CHUNK):
        rows = slice(CHUNK * i, CHUNK * (i + 1))
        w_i = u_new[rows] - mm(w_k[rows], state)
        os_.append(mm(q_g[rows], state))
        decay = jnp.exp(jnp.sum(g[rows], axis=0, keepdims=True))
        state = decay * state + mm_tn(k_tail[rows], w_i)
        ws.append(w_i)
    o = jnp.concatenate(os_, axis=0) + mm(qk, jnp.concatenate(ws, axis=0))
    out = _rms(o, og) * (gate * jax.nn.sigmoid(gate))
    return out, state


def _gdn_specs(nc, rev):
    def ci(c):
        return nc - 1 - c if rev else c

    def blk(cb):
        return pl.BlockSpec((GDN_ROWS, 512), lambda c: (ci(c), cb))

    col = lambda n: pl.BlockSpec((n, GDN_ROWS, 1), lambda c: (0, ci(c), 0))
    sc = pl.BlockSpec((GDN_H, 1, 1), lambda c: (0, 0, 0))
    og = pl.BlockSpec((1, 128), lambda c: (0, 0))
    st = pl.BlockSpec((GDN_H, None, 128, 128), lambda c: (0, ci(c), 0, 0))
    return blk, col, sc, og, st


def gdn_fwd(qkvc, pm, abt, alog, dtb, og):
    S = qkvc.shape[0]
    nc = S // GDN_ROWS
    blk, col, sc, ogs, st = _gdn_specs(nc, False)

    def body(q_ref, k_ref, v_ref, gate_ref, ab_ref, alog_ref, dtb_ref, og_ref, o_ref, st_ref, state):
        c = pl.program_id(0)

        @pl.when(c == 0)
        def _():
            state[...] = jnp.zeros_like(state)

        st_ref[...] = state[...]
        for h in range(GDN_H):
            sl = slice(GDN_D * h, GDN_D * (h + 1))
            out, new_state = _gdn_chunk(q_ref[:, sl], k_ref[:, sl], v_ref[:, sl], gate_ref[:, sl], ab_ref[h],
                                        ab_ref[GDN_H + h], alog_ref[h], dtb_ref[h], og_ref[...], state[h])
            o_ref[:, sl] = out
            state[h] = new_state

    return pl.pallas_call(
        body, name="gdn_fwd", grid=(nc,),
        in_specs=[blk(0), blk(1), blk(2), blk(4), col(2 * GDN_H), sc, sc, ogs],
        out_specs=[blk(0), st],
        out_shape=[jax.ShapeDtypeStruct((S, 512), F32), jax.ShapeDtypeStruct((GDN_H, nc, 128, 128), F32)],
        scratch_shapes=[pltpu.VMEM((GDN_H, 128, 128), F32)],
        compiler_params=_cp("arbitrary"),
    )(qkvc, qkvc, qkvc, pm, abt, alog, dtb, og)


def gdn_bwd(qkvc, pm, abt, alog, dtb, og, st, dout):
    S = qkvc.shape[0]
    nc = S // GDN_ROWS
    blk, col, sc, ogs, sts = _gdn_specs(nc, True)

    def body(q_ref, k_ref, v_ref, gate_ref, ab_ref, alog_ref, dtb_ref, og_ref, st_ref, do_ref,
             dq_ref, dk_ref, dv_ref, dgate_ref, dal_ref, dbl_ref, dalog_ref, ddtb_ref, dog_ref, dstate):
        c = pl.program_id(0)

        @pl.when(c == 0)
        def _():
            dstate[...] = jnp.zeros_like(dstate)

        for h in range(GDN_H):
            sl = slice(GDN_D * h, GDN_D * (h + 1))
            args = (q_ref[:, sl], k_ref[:, sl], v_ref[:, sl], gate_ref[:, sl], ab_ref[h], ab_ref[GDN_H + h],
                    alog_ref[h], dtb_ref[h], og_ref[...], st_ref[h])
            _, vjp = jax.vjp(_gdn_chunk, *args)
            g = vjp((do_ref[:, sl], dstate[h]))
            for o, gv in zip((dq_ref, dk_ref, dv_ref, dgate_ref), g[:4]):
                o[:, sl] = gv
            dal_ref[h] = g[4]
            dbl_ref[h] = g[5]
            dstate[h] = g[9]
            for o, gv in zip((dalog_ref, ddtb_ref, dog_ref), g[6:9]):
                @pl.when(c == 0)
                def _(o=o, gv=gv, h=h):
                    o[h] = gv

                @pl.when(c > 0)
                def _(o=o, gv=gv, h=h):
                    o[h] += gv

    ogo = pl.BlockSpec((GDN_H, 1, 128), lambda c: (0, 0, 0))
    sd = jax.ShapeDtypeStruct
    return pl.pallas_call(
        body, name="gdn_bwd", grid=(nc,),
        in_specs=[blk(0), blk(1), blk(2), blk(4), col(2 * GDN_H), sc, sc, ogs, sts, blk(0)],
        out_specs=[blk(0), blk(0), blk(0), blk(0), col(GDN_H), col(GDN_H), sc, sc, ogo],
        out_shape=[sd((S, 512), F32)] * 4 + [sd((GDN_H, S, 1), F32)] * 2 + [sd((GDN_H, 1, 1), F32)] * 2
        + [sd((GDN_H, 1, 128), F32)],
        scratch_shapes=[pltpu.VMEM((GDN_H, 128, 128), F32)],
        compiler_params=_cp("arbitrary"),
    )(qkvc, qkvc, qkvc, pm, abt, alog, dtb, og, st, dout)


HG = 4
HG_LANES = HG * CA_D


def _group_mean_raw(y):
    r = lax.broadcasted_iota(jnp.int32, (128, 128), 0)
    c = lax.broadcasted_iota(jnp.int32, (128, 128), 1)
    g = jnp.where((r // CA_D) == (c // CA_D), 1.0 / CA_D, 0.0).astype(BF16)
    d = lambda u: lax.dot_general(u, g, (((1,), (0,)), ((), ())), preferred_element_type=F32)
    outs = []
    for j in range(y.shape[1] // 128):
        hi, lo = _split(y[:, 128 * j:128 * (j + 1)])
        outs.append(d(hi) + d(lo))
    return jnp.concatenate(outs, axis=1)


@jax.custom_vjp
def group_mean(y):
    return _group_mean_raw(y)


group_mean.defvjp(lambda y: (_group_mean_raw(y), None), lambda _, g: (_group_mean_raw(g),))


def f_headnorm(t, g):
    return (t * lax.rsqrt(group_mean(t * t) + EPS) * g,)


def _cattn_chunk(q, kb, vb, bias, valid):
    lane = lax.broadcasted_iota(jnp.int32, (1, 128), 1)
    m0 = (lane < CA_D).astype(F32)
    m1 = 1.0 - m0
    pairs = range(HG // 2)
    sl = [slice(128 * p, 128 * (p + 1)) for p in pairs]
    q2 = [jnp.concatenate([q[:, s] * m0, q[:, s] * m1], axis=0) for s in sl]
    sc = [mm_nt(q2[p], kb[:, sl[p]]) * (CA_D ** -0.5) + bias[sl[p]] for p in pairs]
    pr = [_softmax(jnp.where(valid, s, -1e30)) for s in sc]
    o2 = [mm(pr[p], vb[:, sl[p]]) for p in pairs]
    return jnp.concatenate([o[:CHUNK] * m0 + o[CHUNK:] * m1 for o in o2], axis=1)


def _cattn_valid(c):
    pos = lax.broadcasted_iota(jnp.int32, (1, CA_BAND), 1) + c * CHUNK
    return pos >= CA_PAD


def _cattn_specs(S):
    q_spec = pl.BlockSpec((CHUNK, HG_LANES), lambda h, c: (c, h))
    kv_spec = pl.BlockSpec((S + CA_PAD, HG_LANES), lambda h, c: (0, h))
    b_spec = pl.BlockSpec((HG * CHUNK, CA_BAND), lambda h, c: (h, 0))
    return q_spec, kv_spec, b_spec


def cattn_fwd(qn, kp, vp, bias):
    S = qn.shape[0]
    nc = S // CHUNK
    q_spec, kv_spec, b_spec = _cattn_specs(S)

    def body(q_ref, k_ref, v_ref, b_ref, o_ref):
        c = pl.program_id(1)
        start = pl.multiple_of(c * CHUNK, CHUNK)
        kb = k_ref[pl.ds(start, CA_BAND), :]
        vb = v_ref[pl.ds(start, CA_BAND), :]
        o_ref[...] = _cattn_chunk(q_ref[...], kb, vb, b_ref[...], _cattn_valid(c)).astype(o_ref.dtype)

    return pl.pallas_call(
        body, name="cattn_fwd", grid=(CA_H // HG, nc), in_specs=[q_spec, kv_spec, kv_spec, b_spec],
        out_specs=q_spec, out_shape=jax.ShapeDtypeStruct((S, D), BF16),
        compiler_params=_cp("parallel", "arbitrary"),
    )(qn, kp, vp, bias)


def kv_prep(qkv, kg):
    S = qkv.shape[0]
    tile = ROW_TILE
    lead = CA_PAD // tile

    def body(k_ref, v_ref, g_ref, kp_ref, vp_ref):
        i = pl.program_id(0)

        @pl.when(i < lead)
        def _():
            kp_ref[...] = jnp.zeros_like(kp_ref)
            vp_ref[...] = jnp.zeros_like(vp_ref)

        @pl.when(i >= lead)
        def _():
            kp_ref[...] = f_headnorm(k_ref[...], g_ref[...])[0].astype(BF16)
            vp_ref[...] = v_ref[...].astype(BF16)

    src = lambda cb: pl.BlockSpec((tile, D), lambda i, cb=cb: (jnp.maximum(i - lead, 0), cb))
    out = pl.BlockSpec((tile, D), lambda i: (i, 0))
    return pl.pallas_call(
        body, name="kv_prep", grid=((S + CA_PAD) // tile,),
        in_specs=[src(1), src(2), pl.BlockSpec((1, D), lambda i: (0, 0))], out_specs=[out, out],
        out_shape=[jax.ShapeDtypeStruct((S + CA_PAD, D), BF16)] * 2,
        compiler_params=_cp("parallel"),
    )(qkv, qkv, kg)


def cattn_bwd(qn, kp, vp, bias, do):
    S = qn.shape[0]
    nc = S // CHUNK
    q_spec, kv_spec, b_spec = _cattn_specs(S)

    def body(q_ref, k_ref, v_ref, b_ref, do_ref, dq_ref, dk_ref, dv_ref, db_ref):
        c = pl.program_id(1)

        @pl.when(c == 0)
        def _():
            dk_ref[...] = jnp.zeros_like(dk_ref)
            dv_ref[...] = jnp.zeros_like(dv_ref)
            db_ref[...] = jnp.zeros_like(db_ref)

        start = pl.multiple_of(c * CHUNK, CHUNK)
        kb = k_ref[pl.ds(start, CA_BAND), :].astype(F32)
        vb = v_ref[pl.ds(start, CA_BAND), :].astype(F32)
        valid = _cattn_valid(c)
        _, vjp = jax.vjp(lambda q, k, v, b: _cattn_chunk(q, k, v, b, valid), q_ref[...], kb, vb, b_ref[...])
        dq, dk, dv, db = vjp(do_ref[...])
        dq_ref[...] = dq
        dk_ref[pl.ds(start, CA_BAND), :] += dk
        dv_ref[pl.ds(start, CA_BAND), :] += dv
        db_ref[...] += db

    sd = jax.ShapeDtypeStruct
    return pl.pallas_call(
        body, name="cattn_bwd", grid=(CA_H // HG, nc), in_specs=[q_spec, kv_spec, kv_spec, b_spec, q_spec],
        out_specs=[q_spec, kv_spec, kv_spec, b_spec],
        out_shape=[sd((S, D), F32), sd((S + CA_PAD, D), F32), sd((S + CA_PAD, D), F32),
                   sd((CA_H * CHUNK, CA_BAND), F32)],
        compiler_params=_cp("parallel", "arbitrary"),
    )(qn, kp, vp, bias, do)


_REL_IDX = np.clip(np.arange(CHUNK)[:, None] - np.arange(CA_BAND)[None, :] + CA_PAD, -MAX_REL, MAX_REL) + MAX_REL
SKEW_W = CA_BAND + CHUNK


def rel_bias_grad(dbias):
    padded = jnp.pad(dbias, ((0, 0), (0, 0), (CHUNK, 0)))
    flat = jnp.pad(padded.reshape(CA_H, CHUNK * SKEW_W), ((0, 0), (0, CHUNK)))
    skew = flat.reshape(CA_H, CHUNK, SKEW_W + 1)

    first_near = SKEW_W - CHUNK - MAX_REL

    def fn(t):
        colsum = jnp.sum(t, axis=1, keepdims=True)
        j = lax.broadcasted_iota(jnp.int32, colsum.shape, 2)
        far = jnp.sum(jnp.where(j < first_near, colsum, 0.0), axis=2, keepdims=True)
        return (colsum + jnp.where(j == first_near, far, 0.0),)

    (colsum,) = whole("relbias_sum", fn, [skew], [((CA_H, 1, SKEW_W + 1), F32)])
    near = colsum[:, 0, first_near:SKEW_W][:, ::-1]
    return jnp.concatenate([jnp.zeros((CA_H, CHUNK + 1), F32), near], axis=1)


def rel_bias_expand(rb):
    near = rb[:, CHUNK + 1:][:, ::-1]
    far = jnp.broadcast_to(rb[:, 2 * MAX_REL:], (CA_H, SKEW_W - CHUNK - MAX_REL))
    t = jnp.concatenate([far, near, jnp.zeros((CA_H, 1), rb.dtype)], axis=1)
    rows = jnp.tile(t, (1, CHUNK))[:, :CHUNK * SKEW_W].reshape(CA_H, CHUNK, SKEW_W)
    return rows[:, :, CHUNK:]


def loss_head(y, target):
    S = y.shape[0]
    tile = min(ROW_TILE, S)

    def body(y_ref, t_ref, dy_ref, acc_ref):
        i = pl.program_id(0)
        e = y_ref[...] - t_ref[...]
        dy_ref[...] = e * (1.0 / D)
        part = jnp.sum(e * e, axis=0, keepdims=True) * (0.5 / D)

        @pl.when(i == 0)
        def _():
            acc_ref[...] = part

        @pl.when(i > 0)
        def _():
            acc_ref[...] += part

    row = pl.BlockSpec((tile, D), lambda i: (i, 0))
    return pl.pallas_call(
        body, name="loss_head", grid=(S // tile,), in_specs=[row, row],
        out_specs=[row, pl.BlockSpec((1, D), lambda i: (0, 0))],
        out_shape=[jax.ShapeDtypeStruct((S, D), F32), jax.ShapeDtypeStruct((1, D), F32)],
        compiler_params=_cp("arbitrary"),
    )(y, target)


ANY = pl.BlockSpec(memory_space=pl.ANY)


HBM = pl.BlockSpec(memory_space=pltpu.HBM)
SEM = pl.BlockSpec(memory_space=pltpu.SEMAPHORE)
EFFECT = pltpu.SideEffectType.DATAFLOW_SIDE_EFFECTING


def _chip_copies(src_ref, land_ref, send_sems, recv_sems, per_dest):
    x, y, c = lax.axis_index("x"), lax.axis_index("y"), lax.axis_index("c")
    me = 2 * x + y
    out = []
    for j, (px, py) in enumerate([(1 - x, y), (x, 1 - y), (1 - x, 1 - y)]):
        peer = 2 * px + py
        if per_dest:
            send = (src_ref.at[peer], land_ref.at[j])
            recv = (src_ref.at[me], land_ref.at[j])
        else:
            send = (src_ref, land_ref.at[me])
            recv = (src_ref, land_ref.at[peer])
        mk = lambda s, d, j=j, px=px, py=py: pltpu.make_async_remote_copy(
            src_ref=s, dst_ref=d, send_sem=send_sems.at[j], recv_sem=recv_sems.at[j],
            device_id=(px, py, c), device_id_type=MESH)
        out.append((mk(*send), mk(*recv)))
    return out


def exchange_start(src, land, carry, name, per_dest):
    def body(src_ref, land_ref, carry_ref, send_sems, recv_sems, src_out, land_out, carry_out):
        for send, _ in _chip_copies(src_ref, land_ref, send_sems, recv_sems, per_dest):
            send.start()

    hbm = lambda a: pltpu.HBM(a.shape, a.dtype)
    return pl.pallas_call(
        body, name=name,
        out_shape=(pltpu.SemaphoreType.DMA((3,)), pltpu.SemaphoreType.DMA((3,)), hbm(src), hbm(land), hbm(carry)),
        in_specs=(HBM, HBM, HBM), out_specs=(SEM, SEM, HBM, HBM, HBM),
        input_output_aliases={0: 2, 1: 3, 2: 4},
        compiler_params=pltpu.CompilerParams(has_side_effects=EFFECT),
    )(pltpu.with_memory_space_constraint(src, pltpu.HBM), pltpu.with_memory_space_constraint(land, pltpu.HBM),
      pltpu.with_memory_space_constraint(carry, pltpu.HBM))


def exchange_wait(src, land, send_sems, recv_sems, after, name, per_dest):
    def body(src_ref, land_ref, send_sems_ref, recv_sems_ref, after_ref, src_out, land_out):
        for send, recv in _chip_copies(src_ref, land_ref, send_sems_ref, recv_sems_ref, per_dest):
            send.wait_send()
            recv.wait_recv()

    hbm = lambda a: pltpu.HBM(a.shape, a.dtype)
    return pl.pallas_call(
        body, name=name, out_shape=(hbm(src), hbm(land)),
        in_specs=(HBM, HBM, SEM, SEM, ANY), out_specs=(HBM, HBM), input_output_aliases={0: 0, 1: 1},
        compiler_params=pltpu.CompilerParams(has_side_effects=EFFECT),
    )(src, land, send_sems, recv_sems, after)


def sibling_exchange(srcs, name):
    n = len(srcs)

    def body(*refs):
        src_refs, out_refs, send_sems, recv_sems = refs[:n], refs[n:2 * n], refs[2 * n], refs[2 * n + 1]
        x, y, c = lax.axis_index("x"), lax.axis_index("y"), lax.axis_index("c")
        copies = [pltpu.make_async_remote_copy(src_ref=s, dst_ref=o, send_sem=send_sems.at[k], recv_sem=recv_sems.at[k],
                                               device_id=(x, y, 1 - c), device_id_type=MESH)
                  for k, (s, o) in enumerate(zip(src_refs, out_refs))]
        for cp in copies:
            cp.start()
        for cp in copies:
            cp.wait()

    return pl.pallas_call(
        body, name=name, in_specs=[ANY] * n, out_specs=[ANY] * n,
        out_shape=[jax.ShapeDtypeStruct(s.shape, s.dtype) for s in srcs],
        scratch_shapes=[pltpu.SemaphoreType.DMA((n,)), pltpu.SemaphoreType.DMA((n,))],
    )(*srcs)


def all_exchange(src, name):
    def body(src_ref, out_ref, send_sems, recv_sems, local_sem):
        x, y, c = lax.axis_index("x"), lax.axis_index("y"), lax.axis_index("c")
        me = 4 * x + 2 * y + c
        local = pltpu.make_async_copy(src_ref, out_ref.at[me], local_sem)
        local.start()
        sends = []
        peers = []
        for k in range(1, N_DEV):
            bx, by, bc = (k >> 2) & 1, (k >> 1) & 1, k & 1
            px = 1 - x if bx else x
            py = 1 - y if by else y
            pc = 1 - c if bc else c
            peers.append((px, py, pc))
        for k, peer in enumerate(peers):
            cp = pltpu.make_async_remote_copy(src_ref=src_ref, dst_ref=out_ref.at[me], send_sem=send_sems.at[k],
                                              recv_sem=recv_sems.at[k], device_id=peer, device_id_type=MESH)
            cp.start()
            sends.append(cp)
        for k, (px, py, pc) in enumerate(peers):
            pltpu.make_async_remote_copy(src_ref=src_ref, dst_ref=out_ref.at[4 * px + 2 * py + pc],
                                         send_sem=send_sems.at[k], recv_sem=recv_sems.at[k],
                                         device_id=(px, py, pc), device_id_type=MESH).wait_recv()
        for cp in sends:
            cp.wait_send()
        local.wait()

    return pl.pallas_call(
        body, name=name, in_specs=[ANY], out_specs=ANY,
        out_shape=jax.ShapeDtypeStruct((N_DEV,) + tuple(src.shape), src.dtype),
        scratch_shapes=[pltpu.SemaphoreType.DMA((N_DEV - 1,)), pltpu.SemaphoreType.DMA((N_DEV - 1,)),
                        pltpu.SemaphoreType.DMA],
    )(src)


def sum_own_slabs(own, land, name, tile=512):
    R, C = own.shape
    n = land.shape[0]
    tile = _tile(R, tile, 16)

    def body(o_ref, t_ref, out_ref):
        acc = o_ref[...].astype(F32)
        for s in range(n):
            acc = acc + t_ref[s].astype(F32)
        out_ref[...] = acc

    return pl.pallas_call(
        body, name=name, grid=(R // tile,),
        in_specs=[pl.BlockSpec((tile, C), lambda i: (i, 0)), pl.BlockSpec((n, tile, C), lambda i: (0, i, 0))],
        out_specs=pl.BlockSpec((tile, C), lambda i: (i, 0)), out_shape=jax.ShapeDtypeStruct((R, C), F32),
        compiler_params=_cp("parallel"),
    )(own, land)


def sum_slabs(t, name, tile=512):
    n, R, C = t.shape
    tile = _tile(R, tile, 16)

    def body(t_ref, o_ref):
        acc = t_ref[0].astype(F32)
        for s in range(1, n):
            acc = acc + t_ref[s].astype(F32)
        o_ref[...] = acc

    return pl.pallas_call(
        body, name=name, grid=(R // tile,), in_specs=[pl.BlockSpec((n, tile, C), lambda i: (0, i, 0))],
        out_specs=pl.BlockSpec((tile, C), lambda i: (i, 0)), out_shape=jax.ShapeDtypeStruct((R, C), F32),
        compiler_params=_cp("parallel"),
    )(t)


PACK_ROW_MULT = 512


def _pad_rows(a, mult=16):
    r = (-a.shape[0]) % mult
    return jnp.pad(a, ((0, r), (0, 0))) if r else a


BIG = [
    ("ab_w_in", True, 2, 642), ("c_w_qkv", True, 2, 768), ("xa_w_kv", True, 4, 512),
    ("f_w_gate", True, 4, 704), ("f_w_up", True, 4, 704),
    ("ab_w_out", False, 2, 256), ("c_w_out", False, 2, 256), ("xa_w_q", False, 4, 256),
    ("xa_w_out", False, 4, 256), ("f_w_down", False, 4, 704), ("s5_w_glu", False, 2, 64),
]


GROUPS = ("B", "A")
GROUP_ROW_MULT = 64


def group_spec(layer, grp):
    i = layer // 2
    if grp == "A":
        return [("xa_w_kv", layer, True, 512), ("xa_w_q", layer, False, 256), ("xa_w_out", layer, False, 256),
                ("f_w_gate", layer, True, 704), ("f_w_up", layer, True, 704), ("f_w_down", layer, False, 704)]
    if layer % 2 == 0:
        return [("ab_w_in", i, True, 642), ("ab_w_out", i, False, 256), ("s5_w_glu", i, False, 64)]
    return [("c_w_qkv", i, True, 768), ("c_w_out", i, False, 256)]


def _seg_rows(rows):
    return rows + ((-rows) % 16)


def _f32_rows(a):
    bits = lax.bitcast_convert_type(a.reshape(-1), BF16).reshape(-1)
    return jnp.pad(bits, (0, 16 * D - bits.shape[0])).reshape(16, D)


def pack_group_shards(p, layer, grp):
    segs = []
    for name, idx, transposed, rows in group_spec(layer, grp):
        w = p[name][idx]
        if transposed:
            w = w.T
        segs.append(_pad_rows(w.astype(BF16).reshape(-1, D)))
    if grp == "B":
        small = p["gdn_conv_w"] if layer % 2 == 0 else p["c_norm_g"]
        segs.append(_f32_rows(small[layer // 2]))
    return _pad_rows(jnp.concatenate(segs, axis=0), GROUP_ROW_MULT)


def unpack_group_gathered(g, layer, grp):
    out, off = {}, 0
    for name, idx, transposed, rows in group_spec(layer, grp):
        seg = g[:, off:off + rows]
        if name == "s5_w_glu":
            out[name] = seg.reshape(N_CHIPS * 128, 512)
        else:
            out[name] = seg.reshape(N_CHIPS * rows, D)
        off += _seg_rows(rows)
    if grp == "B":
        n = 4 * 384 if layer % 2 == 0 else 256
        bits = g[:, off:off + 16].reshape(N_CHIPS, -1)[:, :2 * n].reshape(N_CHIPS, n, 2)
        small = lax.bitcast_convert_type(bits, F32)
        if layer % 2 == 0:
            out["gdn_conv_w"] = jnp.swapaxes(small.reshape(N_CHIPS, 4, 384), 0, 1).reshape(4, 1536)
        else:
            out["c_norm_g"] = small.reshape(D)
    return out


def pack_group_grads(gr, layer, grp):
    segs = []
    for name, idx, transposed, rows in group_spec(layer, grp):
        w = gr[name].astype(BF16)
        seg = w.reshape(N_CHIPS, rows, D)
        r = (-rows) % 16
        if r:
            seg = jnp.pad(seg, ((0, 0), (0, r), (0, 0)))
        segs.append(seg)
    out = jnp.concatenate(segs, axis=1)
    return jnp.pad(out, ((0, 0), (0, (-out.shape[1]) % GROUP_ROW_MULT), (0, 0)))


def unpack_group_reduced(g, layer, grp):
    out, off = {}, 0
    for name, idx, transposed, rows in group_spec(layer, grp):
        seg = g[off:off + rows]
        if name == "s5_w_glu":
            out[name] = seg.reshape(128, 512)
        else:
            out[name] = seg.T if transposed else seg
        off += _seg_rows(rows)
    return out


SMALL = ["ab_norm_g", "s5_a_re", "s5_a_im", "s5_log_dt", "s5_b_re", "s5_b_im", "s5_c_re", "s5_c_im", "s5_d",
         "s5_b_glu", "gdn_conv_w", "gdn_a_log", "gdn_dt_bias", "gdn_out_norm_g", "c_norm_g", "c_q_norm_g",
         "c_k_norm_g", "c_rel_bias", "mem_norm_g", "xa_norm_g", "xa_q_norm_g", "xa_k_norm_g", "f_norm_g"]


def _lane_rows(a):
    flat = a.reshape(-1).astype(F32)
    return jnp.pad(flat, (0, (-flat.shape[0]) % 128)).reshape(-1, 128)


def pack_small(d, extra=None):
    parts = [_lane_rows(d[n]) for n in SMALL]
    if extra is not None:
        parts.append(_lane_rows(extra))
    rows = jnp.concatenate(parts, axis=0)
    return jnp.pad(rows, ((0, (-rows.shape[0]) % 128), (0, 0)))


def unpack_small(rows, shapes):
    out, off = {}, 0
    for n in SMALL:
        sz = int(np.prod(shapes[n]))
        k = -(-sz // 128)
        out[n] = rows[off:off + k].reshape(-1)[:sz].reshape(shapes[n])
        off += k
    return out, rows[off:]


def _s5_blockdiag_b(b):
    bt = jnp.swapaxes(b, 1, 2).reshape(S5_GB, 8, S5_C, S5_P)
    eye = jnp.eye(8, dtype=b.dtype)
    return jnp.einsum("bgcp,gh->bgchp", bt, eye).reshape(S5_GB, 8 * S5_C, 8 * S5_P)


def _s5_blockdiag_c(c):
    ct = jnp.swapaxes(c, 1, 2).reshape(S5_GB, 8, S5_P, S5_C)
    eye = jnp.eye(8, dtype=c.dtype)
    return jnp.einsum("bgpc,gh->bgphc", ct, eye).reshape(S5_GB, 8 * S5_P, 8 * S5_C)


def _s5_diag_b(db):
    t = db.reshape(S5_GB, 8, S5_C, 8, S5_P)
    t = jnp.transpose(t, (0, 2, 4, 1, 3)).reshape(S5_GB, S5_C, S5_P, 64)
    d = t[..., ::9]
    return jnp.transpose(d, (0, 3, 2, 1)).reshape(S5_G, S5_P, S5_C)


def _s5_diag_c(dc):
    t = dc.reshape(S5_GB, 8, S5_P, 8, S5_C)
    t = jnp.transpose(t, (0, 2, 4, 1, 3)).reshape(S5_GB, S5_P, S5_C, 64)
    d = t[..., ::9]
    return jnp.transpose(d, (0, 3, 2, 1)).reshape(S5_G, S5_C, S5_P)


def _heads(t):
    return jnp.swapaxes(t.reshape(t.shape[0], CA_H, CA_D), 0, 1)


def _unheads(t):
    return jnp.swapaxes(t, 0, 1).reshape(t.shape[1], D)


def local_step(x, mem, target, p, wsrc, gsink):
    S = x.shape[0]
    row2 = lambda a: a.reshape(1, -1)
    saved = []
    (mem_n,) = row_fwd("mem_norm", f_norm, [mem], [row2(p["mem_norm_g"])], [(D, BF16)])
    gs = {}

    for layer in range(DEPTH):
        i = layer // 2
        w = wsrc(layer, "B", x)
        sv = {"x0": x, "wB": w}
        if layer % 2 == 0:
            (h,) = row_fwd("norm", f_norm, [x], [row2(p["ab_norm_g"][i])], [(D, BF16)])
            w_in = w["ab_w_in"]
            pm = matmul("nt", h, w_in[:2560], "proj_main")
            pab = matmul("nt", h, w_in[2560:], "proj_ab")
            s5p = dict(
                are=p["s5_a_re"][i].reshape(1, -1), aim=p["s5_a_im"][i].reshape(1, -1),
                ldt=jnp.broadcast_to(p["s5_log_dt"][i][:, None], (S5_G, S5_P)).reshape(1, -1),
                b_re=_s5_blockdiag_b(p["s5_b_re"][i]), b_im=_s5_blockdiag_b(p["s5_b_im"][i]),
                c_re=_s5_blockdiag_c(p["s5_c_re"][i]), c_im=_s5_blockdiag_c(p["s5_c_im"][i]),
                dv=p["s5_d"][i].reshape(1, -1))
            y5, st5 = s5_fwd(pm, **s5p)
            (a_out,) = row_fwd("glu", f_glu, [y5], [w["s5_w_glu"], row2(p["s5_b_glu"][i])], [(S5_W, F32)])
            conv_w = w["gdn_conv_w"]
            qkvc = conv_fwd(pm, conv_w)
            abt = jnp.swapaxes(pab, 0, 1)[:, :, None]
            alog = p["gdn_a_log"][i].reshape(GDN_H, 1, 1)
            dtb = p["gdn_dt_bias"][i].reshape(GDN_H, 1, 1)
            og = row2(p["gdn_out_norm_g"][i])
            b_out, stg = gdn_fwd(qkvc, pm, abt, alog, dtb, og)
            cat = jnp.concatenate([a_out, b_out], axis=1)
            x = matmul("nn", cat, w["ab_w_out"], "mix_out", add=x)
            sv.update(h=h, pm=pm, s5p=s5p, y5=y5, st5=st5, qkvc=qkvc, abt=abt, alog=alog, dtb=dtb, og=og,
                      stg=stg, cat=cat, conv_w=conv_w)
        else:
            (h,) = row_fwd("norm", f_norm, [x], [row2(w["c_norm_g"])], [(D, BF16)])
            qkv = matmul("nt", h, w["c_w_qkv"], "proj_qkv")
            qg = jnp.tile(row2(p["c_q_norm_g"][i]), (1, CA_H))
            kg = jnp.tile(row2(p["c_k_norm_g"][i]), (1, CA_H))
            (qn,) = row_fwd("headnorm_q", f_headnorm, [(qkv, D, 0)], [qg], [(D, F32)])
            kp, vp = kv_prep(qkv, kg)
            bias = rel_bias_expand(p["c_rel_bias"][i]).reshape(CA_H * CHUNK, CA_BAND)
            o = cattn_fwd(qn, kp, vp, bias)
            x = matmul("nn", o, w["c_w_out"], "mix_out", add=x)
            sv.update(h=h, qkv=qkv, qg=qg, kg=kg, qn=qn, kp=kp, vp=vp, bias=bias, o=o)
        sv["x1"] = x
        w = wsrc(layer, "A", x)
        sv["wA"] = w
        (hq,) = row_fwd("norm", f_norm, [x], [row2(p["xa_norm_g"][layer])], [(D, BF16)])
        qx = matmul("nn", hq, w["xa_w_q"], "xa_q")
        kv = matmul("nt", mem_n, w["xa_w_kv"], "xa_kv")
        xqg, xkg = row2(p["xa_q_norm_g"][layer]), row2(p["xa_k_norm_g"][layer])
        (ox,) = row_fwd("xattn", f_xattn, [qx], [kv, xqg, xkg], [(D, BF16)])
        x = matmul("nn", ox, w["xa_w_out"], "xa_out", add=x)
        sv.update(hq=hq, qx=qx, kv=kv, ox=ox)
        sv["x2"] = x
        (hf,) = row_fwd("norm", f_norm, [x], [row2(p["f_norm_g"][layer])], [(D, BF16)])
        gate = matmul("nt", hf, w["f_w_gate"], "ffn_gate")
        up = matmul("nt", hf, w["f_w_up"], "ffn_up")
        (act,) = row_fwd("swiglu", f_swiglu, [gate, up], [], [(FFN, BF16)])
        x = matmul("nn", act, w["f_w_down"], "ffn_down", add=x)
        sv.update(hf=hf, gate=gate, up=up, act=act)
        saved.append(sv)

    dx, loss_vec = loss_head(x, target)

    dmem_n = None
    for layer in reversed(range(DEPTH)):
        i = layer // 2
        sv = saved[layer]
        w, gw = sv["wA"], {}
        dact = matmul("nt", dx, w["f_w_down"], "ffn_dact")
        gw["f_w_down"] = matmul("tn", sv["act"], dx, "ffn_dwd", out_dtype=BF16)
        dgate, dup = row_bwd("swiglu_bwd", f_swiglu, [sv["gate"], sv["up"]], [], [dact], [0, 1], [],
                             row_dtypes=[BF16, BF16])
        gw["f_w_gate"] = matmul("tn", dgate, sv["hf"], "ffn_dwg", out_dtype=BF16)
        gw["f_w_up"] = matmul("tn", dup, sv["hf"], "ffn_dwu", out_dtype=BF16)
        dh = matmul("nn", dgate, w["f_w_gate"], "ffn_dhg")
        dh = matmul("nn", dup, w["f_w_up"], "ffn_dhu", add=dh)
        dx, dg = row_bwd("norm_bwd", f_norm_res, [sv["x2"]], [row2(p["f_norm_g"][layer])], [dh, dx], [0], [0])
        gs.setdefault("f_norm_g", [None] * DEPTH)[layer] = dg[0]
        do = matmul("nt", dx, w["xa_w_out"], "xa_do")
        gw["xa_w_out"] = matmul("tn", sv["ox"], dx, "xa_dwo", out_dtype=BF16)
        xqg, xkg = row2(p["xa_q_norm_g"][layer]), row2(p["xa_k_norm_g"][layer])
        dqx, dkv, dqg, dkg = row_bwd("xattn_bwd", f_xattn, [sv["qx"]], [sv["kv"], xqg, xkg], [do],
                                     [0], [0, 1, 2])
        gs.setdefault("xa_q_norm_g", [None] * DEPTH)[layer] = dqg[0]
        gs.setdefault("xa_k_norm_g", [None] * DEPTH)[layer] = dkg[0]
        gw["xa_w_q"] = matmul("tn", sv["hq"], dqx, "xa_dwq", out_dtype=BF16)
        dhq = matmul("nt", dqx, w["xa_w_q"], "xa_dhq")
        gw["xa_w_kv"] = matmul("tn", dkv, mem_n, "xa_dwkv", out_dtype=BF16)
        dmem_n = matmul("nn", dkv, w["xa_w_kv"], "xa_dmem", add=dmem_n)
        dx, dg = row_bwd("norm_bwd", f_norm_res, [sv["x1"]], [row2(p["xa_norm_g"][layer])], [dhq, dx], [0], [0])
        gs.setdefault("xa_norm_g", [None] * DEPTH)[layer] = dg[0]
        dx = gsink(layer, "A", gw, dx)
        w, gw = sv["wB"], {}
        if layer % 2 == 0:
            dcat = matmul("nt", dx, w["ab_w_out"], "mix_dcat")
            gw["ab_w_out"] = matmul("tn", sv["cat"], dx, "mix_dwo", out_dtype=BF16)
            dy5, dwglu, dbglu = row_bwd("glu_bwd", f_glu, [sv["y5"]], [w["s5_w_glu"], row2(p["s5_b_glu"][i])],
                                        [(dcat, S5_W, 0)], [0], [0, 1])
            gw["s5_w_glu"] = dwglu
            gs.setdefault("s5_b_glu", [None] * 2)[i] = dbglu[0]
            s5p = sv["s5p"]
            du, dare, daim, dldt, dbre, dbim, dcre, dcim, ddv = s5_bwd(sv["pm"], sv["st5"], dy5, **s5p)
            (dldt_g,) = whole("s5_dt_sum", lambda t: (jnp.sum(t, axis=1, keepdims=True),),
                              [dldt.reshape(S5_G, S5_P)], [((S5_G, 1), F32)])
            for nme, val in (("s5_a_re", dare.reshape(S5_G, S5_P)), ("s5_a_im", daim.reshape(S5_G, S5_P)),
                             ("s5_log_dt", dldt_g[:, 0]), ("s5_b_re", _s5_diag_b(dbre)),
                             ("s5_b_im", _s5_diag_b(dbim)), ("s5_c_re", _s5_diag_c(dcre)),
                             ("s5_c_im", _s5_diag_c(dcim)), ("s5_d", ddv.reshape(S5_G, S5_C))):
                gs.setdefault(nme, [None] * 2)[i] = val
            dq, dk, dv, dgate, dal, dbl, dalog, ddtb, dog = gdn_bwd(
                sv["qkvc"], sv["pm"], sv["abt"], sv["alog"], sv["dtb"], sv["og"], sv["stg"], dcat[:, S5_W:])
            (dog_s,) = whole("gdn_og_sum", lambda t: (jnp.sum(t, axis=0, keepdims=True),),
                             [dog.reshape(GDN_H, GDN_D)], [((1, GDN_D), F32)])
            gs.setdefault("gdn_out_norm_g", [None] * 2)[i] = dog_s[0]
            gs.setdefault("gdn_a_log", [None] * 2)[i] = dalog.reshape(GDN_H)
            gs.setdefault("gdn_dt_bias", [None] * 2)[i] = ddtb.reshape(GDN_H)
            dqkvc = jnp.concatenate([dq, dk, dv], axis=1)
            dqkv, dconv = conv_bwd(sv["pm"], sv["conv_w"], dqkvc)
            gs.setdefault("gdn_conv_w", [None] * 2)[i] = dconv
            dpm = jnp.concatenate([du, dqkv, dgate], axis=1).astype(BF16)
            dpab = jnp.swapaxes(jnp.concatenate([dal, dbl], axis=0)[:, :, 0], 0, 1)
            dw_main = matmul("tn", dpm, sv["h"], "proj_dw", out_dtype=BF16)
            dw_ab = matmul("tn", dpab, sv["h"], "proj_ab_dw", out_dtype=BF16)
            gw["ab_w_in"] = jnp.concatenate([dw_main, dw_ab], axis=0)
            w_in = w["ab_w_in"]
            dh = matmul("nn", dpm, w_in[:2560], "proj_dh")
            dh = matmul("nn", dpab, w_in[2560:], "proj_ab_dh", add=dh)
            dx, dg = row_bwd("norm_bwd", f_norm_res, [sv["x0"]], [row2(p["ab_norm_g"][i])], [dh, dx], [0], [0])
            gs.setdefault("ab_norm_g", [None] * 2)[i] = dg[0]
        else:
            do = matmul("nt", dx, w["c_w_out"], "mix_dcat")
            gw["c_w_out"] = matmul("tn", sv["o"], dx, "mix_dwo", out_dtype=BF16)
            dqn, dkp, dvp, dbias = cattn_bwd(sv["qn"], sv["kp"], sv["vp"], sv["bias"], do)
            gs.setdefault("c_rel_bias", [None] * 2)[i] = rel_bias_grad(dbias.reshape(CA_H, CHUNK, CA_BAND))
            dq, dqg = row_bwd("headnorm_bwd", f_headnorm, [(sv["qkv"], D, 0)], [sv["qg"]], [dqn], [0], [0])
            dk, dkg = row_bwd("headnorm_bwd", f_headnorm, [(sv["qkv"], D, 1)], [sv["kg"]],
                              [(dkp, D, 0, CA_PAD)], [0], [0])
            head_sum = lambda t: (jnp.sum(t, axis=0, keepdims=True),)
            (dqg,) = whole("headgain_sum", head_sum, [dqg.reshape(CA_H, CA_D)], [((1, CA_D), F32)])
            (dkg,) = whole("headgain_sum", head_sum, [dkg.reshape(CA_H, CA_D)], [((1, CA_D), F32)])
            gs.setdefault("c_q_norm_g", [None] * 2)[i] = dqg[0]
            gs.setdefault("c_k_norm_g", [None] * 2)[i] = dkg[0]
            dqkv = jnp.concatenate([dq, dk, dvp[CA_PAD:]], axis=1).astype(BF16)
            gw["c_w_qkv"] = matmul("tn", dqkv, sv["h"], "proj_qkv_dw", out_dtype=BF16)
            dh = matmul("nn", dqkv, w["c_w_qkv"], "proj_qkv_dh")
            dx, dg = row_bwd("norm_bwd", f_norm_res, [sv["x0"]], [row2(w["c_norm_g"])], [dh, dx], [0], [0])
            gs.setdefault("c_norm_g", [None] * 2)[i] = dg[0]
        dx = gsink(layer, "B", gw, dx)
    (dmg,) = row_bwd("mem_norm_bwd", f_norm, [mem], [row2(p["mem_norm_g"])], [dmem_n], [], [0])
    small = {n: jnp.stack(v) for n, v in gs.items()}
    small["mem_norm_g"] = dmg[0]
    return loss_vec, dx, small


ADAM_BLOCK_BYTES = 3 << 19


def adam(w, g, m, v, name):
    shape = w.shape
    if w.ndim == 3:
        d0, n, d2 = shape
        fits = [t for t in range(8, n + 1, 8) if n % t == 0 and d0 * t * d2 * 4 <= ADAM_BLOCK_BYTES]
        return tuple(row_fwd(name, f_adam, [w, g, m, v], [], [((d0, d2), F32)] * 3, tile=max(fits)))
    cols = shape[-1]
    w2, g2, m2, v2 = (t.reshape(-1, cols) for t in (w, g, m, v))
    rows = w2.shape[0]
    tile = rows if rows <= 512 else _tile(rows, 512, 8)
    outs = row_fwd(name, f_adam, [w2, g2, m2, v2], [], [(cols, F32)] * 3, tile=tile)
    return tuple(o.reshape(shape) for o in outs)


WEIGHTS = ['ab_norm_g', 'ab_w_in', 'ab_w_out', 's5_a_re', 's5_a_im', 's5_log_dt', 's5_b_re', 's5_b_im', 's5_c_re',
           's5_c_im', 's5_d', 's5_w_glu', 's5_b_glu', 'gdn_conv_w', 'gdn_a_log', 'gdn_dt_bias', 'gdn_out_norm_g',
           'c_norm_g', 'c_w_qkv', 'c_w_out', 'c_q_norm_g', 'c_k_norm_g', 'c_rel_bias', 'mem_norm_g', 'xa_norm_g',
           'xa_w_q', 'xa_w_kv', 'xa_w_out', 'xa_q_norm_g', 'xa_k_norm_g', 'f_norm_g', 'f_w_gate', 'f_w_up',
           'f_w_down']
SHARDED_SMALL = {"gdn_conv_w": (2, 384), "c_norm_g": (1, 256)}


def kernel(x, mem, ab_norm_g, ab_w_in, ab_w_out, s5_a_re, s5_a_im, s5_log_dt, s5_b_re, s5_b_im, s5_c_re, s5_c_im, s5_d, s5_w_glu, s5_b_glu, gdn_conv_w, gdn_a_log, gdn_dt_bias, gdn_out_norm_g, c_norm_g, c_w_qkv, c_w_out, c_q_norm_g, c_k_norm_g, c_rel_bias, mem_norm_g, xa_norm_g, xa_w_q, xa_w_kv, xa_w_out, xa_q_norm_g, xa_k_norm_g, f_norm_g, f_w_gate, f_w_up, f_w_down, loss_target, m_ab_norm_g, m_ab_w_in, m_ab_w_out, m_s5_a_re, m_s5_a_im, m_s5_log_dt, m_s5_b_re, m_s5_b_im, m_s5_c_re, m_s5_c_im, m_s5_d, m_s5_w_glu, m_s5_b_glu, m_gdn_conv_w, m_gdn_a_log, m_gdn_dt_bias, m_gdn_out_norm_g, m_c_norm_g, m_c_w_qkv, m_c_w_out, m_c_q_norm_g, m_c_k_norm_g, m_c_rel_bias, m_mem_norm_g, m_xa_norm_g, m_xa_w_q, m_xa_w_kv, m_xa_w_out, m_xa_q_norm_g, m_xa_k_norm_g, m_f_norm_g, m_f_w_gate, m_f_w_up, m_f_w_down, v_ab_norm_g, v_ab_w_in, v_ab_w_out, v_s5_a_re, v_s5_a_im, v_s5_log_dt, v_s5_b_re, v_s5_b_im, v_s5_c_re, v_s5_c_im, v_s5_d, v_s5_w_glu, v_s5_b_glu, v_gdn_conv_w, v_gdn_a_log, v_gdn_dt_bias, v_gdn_out_norm_g, v_c_norm_g, v_c_w_qkv, v_c_w_out, v_c_q_norm_g, v_c_k_norm_g, v_c_rel_bias, v_mem_norm_g, v_xa_norm_g, v_xa_w_q, v_xa_w_kv, v_xa_w_out, v_xa_q_norm_g, v_xa_k_norm_g, v_f_norm_g, v_f_w_gate, v_f_w_up, v_f_w_down):
    args = locals()
    p = {n: args[n] for n in WEIGHTS}
    m = {n: args["m_" + n] for n in WEIGHTS}
    v = {n: args["v_" + n] for n in WEIGHTS}
    chip = 2 * lax.axis_index("x") + lax.axis_index("y")

    carry = x[0]
    gathers = {}
    for layer in range(DEPTH):
        for grp in GROUPS:
            src = pack_group_shards(p, layer, grp)
            land = lax.dynamic_update_slice(lax.empty((N_CHIPS,) + src.shape, BF16), src[None], (chip, 0, 0))
            send_sems, recv_sems, src, land, carry = exchange_start(
                src, land, carry, f"gather_start_{layer}{grp}", per_dest=False)
            gathers[layer, grp] = (src, land, send_sems, recv_sems)

    def wsrc(layer, grp, after):
        src, land, send_sems, recv_sems = gathers[layer, grp]
        _, land = exchange_wait(src, land, send_sems, recv_sems, after, f"gather_wait_{layer}{grp}", per_dest=False)
        return unpack_group_gathered(land, layer, grp)

    scatters = []

    def gsink(layer, grp, gw, carry):
        src = pack_group_grads(gw, layer, grp)
        land = lax.empty((3,) + src.shape[1:], BF16)
        send_sems, recv_sems, src, land, carry = exchange_start(
            src, land, carry, f"scatter_start_{layer}{grp}", per_dest=True)
        scatters.append((layer, grp, src, land, send_sems, recv_sems))
        return carry

    loss_vec, dx, g_small = local_step(carry, mem[0], loss_target[0], p, wsrc, gsink)

    parts = []
    for layer, grp, src, land, send_sems, recv_sems in scatters:
        src, land = exchange_wait(src, land, send_sems, recv_sems, dx, f"scatter_wait_{layer}{grp}", per_dest=True)
        own = lax.dynamic_index_in_dim(src, chip, axis=0, keepdims=False)
        parts.append(sum_own_slabs(own, land, "sum_chips"))
    others = sibling_exchange(parts, "sibling_grads")
    per_layer = {}
    for (layer, grp, *_), part, other in zip(scatters, parts, others):
        (total,) = row_fwd("sum_cores", lambda a, b: (a + b,), [part, other], [], [(D, F32)],
                           tile=_tile(part.shape[0], 512, 16))
        for name, g in unpack_group_reduced(total, layer, grp).items():
            per_layer.setdefault(name, {})[layer] = g
    grads = {name: jnp.stack([d[k] for k in sorted(d)]) for name, d in per_layer.items()}

    full_shapes = {n: ((2, 4, 1536) if n == "gdn_conv_w" else (2, D) if n == "c_norm_g" else p[n].shape)
                   for n in SMALL}
    small_all = all_exchange(pack_small(g_small, extra=loss_vec), "gather_small")
    small_sum = sum_slabs(small_all, "sum_small")
    g_s, rest = unpack_small(small_sum, full_shapes)
    (loss11,) = whole("loss_sum", lambda t: (jnp.sum(jnp.sum(t, axis=1, keepdims=True), axis=0, keepdims=True),),
                      [rest[:D // 128]], [((1, 1), F32)])
    for n, (axis, width) in SHARDED_SMALL.items():
        g_s[n] = lax.dynamic_slice_in_dim(g_s[n], chip * width, width, axis=axis)
    grads.update(g_s)

    shard_shapes = {n: p[n].shape for n in SMALL}
    packs = [pack_small(d) for d in (p, grads, m, v)]
    d_s, m_s, v_s = adam(*packs, name="adam_small")
    delta, new_m, new_v = {}, {}, {}
    for dst, src in ((delta, d_s), (new_m, m_s), (new_v, v_s)):
        dst.update(unpack_small(src, shard_shapes)[0])
    for name, _, _, _ in BIG:
        delta[name], new_m[name], new_v[name] = adam(p[name], grads[name], m[name], v[name], "adam_" + name)

    return (loss11[0, 0], dx[None], *[grads[n] for n in WEIGHTS], *[delta[n] for n in WEIGHTS],
            *[new_m[n] for n in WEIGHTS], *[new_v[n] for n in WEIGHTS])
```

```python
import functools
import math

import numpy as np
import jax
import jax.numpy as jnp
from jax import lax
from jax.experimental import pallas as pl
from jax.experimental.pallas import tpu as pltpu

F32 = jnp.float32
BF16 = jnp.bfloat16
MESH = pl.DeviceIdType.MESH

D = 1024
CHUNK = 64
N_MEM = 256
EPS = 1e-6
S5_W = 512
S5_G = 32
S5_C = 16
S5_P = 64
S5_GB = 4
GDN_H = 4
GDN_D = 128
CA_H = 16
CA_D = 64
CA_LEFT = 8
CA_BAND = (CA_LEFT + 1) * CHUNK
CA_PAD = CA_LEFT * CHUNK
MAX_REL = 128
XA_H = 4
XA_D = 256
FFN = 2816
DEPTH = 4
N_CHIPS = 4
N_DEV = 8
LR, B1, B2, AEPS, WD, STEP = 0.001, 0.9, 0.999, 1e-08, 0.01, 10

VMEM_LIMIT = 56 * 1024 * 1024
ROW_TILE = 256
HI = lax.Precision.HIGHEST


def _cp(*sem):
    return pltpu.CompilerParams(dimension_semantics=sem, vmem_limit_bytes=VMEM_LIMIT)


def _dg(a, b, ca, cb):
    return lax.dot_general(a.astype(BF16), b.astype(BF16), (((ca,), (cb,)), ((), ())),
                           preferred_element_type=F32)


@jax.custom_vjp
def mm(a, b):
    return _dg(a, b, 1, 0)


@jax.custom_vjp
def mm_nt(a, b):
    return _dg(a, b, 1, 1)


@jax.custom_vjp
def mm_tn(a, b):
    return _dg(a, b, 0, 0)


mm.defvjp(lambda a, b: (mm(a, b), (a, b)), lambda r, g: (mm_nt(g, r[1]), mm_tn(r[0], g)))
mm_nt.defvjp(lambda a, b: (mm_nt(a, b), (a, b)), lambda r, g: (mm(g, r[1]), mm_tn(g, r[0])))
mm_tn.defvjp(lambda a, b: (mm_tn(a, b), (a, b)), lambda r, g: (mm_nt(r[1], g), mm(r[0], g)))


def _bdg(a, b, ca, cb):
    return lax.dot_general(a.astype(BF16), b.astype(BF16), (((ca,), (cb,)), ((0,), (0,))),
                           preferred_element_type=F32)


@jax.custom_vjp
def bmm(a, b):
    return _bdg(a, b, 2, 1)


@jax.custom_vjp
def bmm_nt(a, b):
    return _bdg(a, b, 2, 2)


@jax.custom_vjp
def bmm_tn(a, b):
    return _bdg(a, b, 1, 1)


bmm.defvjp(lambda a, b: (bmm(a, b), (a, b)), lambda r, g: (bmm_nt(g, r[1]), bmm_tn(r[0], g)))
bmm_nt.defvjp(lambda a, b: (bmm_nt(a, b), (a, b)), lambda r, g: (bmm(g, r[1]), bmm_tn(g, r[0])))
bmm_tn.defvjp(lambda a, b: (bmm_tn(a, b), (a, b)), lambda r, g: (bmm_nt(r[1], g), bmm(r[0], g)))


def _split(a):
    hi = a.astype(BF16)
    return hi, (a - hi.astype(F32)).astype(BF16)


def _dg3(a, b, ca, cb):
    (ah, al), (bh, bl) = _split(a), _split(b)
    d = lambda u, v: lax.dot_general(u, v, (((ca,), (cb,)), ((), ())), preferred_element_type=F32)
    return d(ah, bh) + (d(ah, bl) + d(al, bh))


@jax.custom_vjp
def mm3(a, b):
    return _dg3(a, b, 1, 0)


@jax.custom_vjp
def mm3_nt(a, b):
    return _dg3(a, b, 1, 1)


@jax.custom_vjp
def mm3_tn(a, b):
    return _dg3(a, b, 0, 0)


mm3.defvjp(lambda a, b: (mm3(a, b), (a, b)), lambda r, g: (mm3_nt(g, r[1]), mm3_tn(r[0], g)))
mm3_nt.defvjp(lambda a, b: (mm3_nt(a, b), (a, b)), lambda r, g: (mm3(g, r[1]), mm3_tn(g, r[0])))
mm3_tn.defvjp(lambda a, b: (mm3_tn(a, b), (a, b)), lambda r, g: (mm3_nt(r[1], g), mm3(r[0], g)))


def _tri_mm(v, upper):
    T = v.shape[0]
    r = lax.broadcasted_iota(jnp.int32, (T, T), 0)
    c = lax.broadcasted_iota(jnp.int32, (T, T), 1)
    m = ((c >= r) if upper else (r >= c)).astype(BF16)
    hi, lo = _split(v)
    d = lambda u: lax.dot_general(m, u, (((1,), (0,)), ((), ())), preferred_element_type=F32)
    return d(hi) + d(lo)


@jax.custom_vjp
def cumsum_rows(v):
    return _tri_mm(v, False)


cumsum_rows.defvjp(lambda v: (_tri_mm(v, False), None), lambda _, g: (_tri_mm(g, True),))


def _rms(x, g):
    return x * lax.rsqrt(jnp.mean(x * x, axis=-1, keepdims=True) + EPS) * g


def _softmax(s):
    e = jnp.exp(s - lax.stop_gradient(jnp.max(s, axis=-1, keepdims=True)))
    return e / jnp.sum(e, axis=-1, keepdims=True)


def _softplus(x):
    return jnp.maximum(x, 0.0) + jnp.log(1.0 + jnp.exp(-jnp.abs(x)))


def _tile(n, cap, align):
    if n <= cap:
        return n
    best = None
    for d in range(align, cap + 1, align):
        if n % d == 0:
            best = d
    assert best is not None, (n, cap, align)
    return best


def matmul(mode, a, b, name, out_dtype=F32, add=None):
    if mode == "nn":
        (M, K), (K2, N) = a.shape, b.shape
    elif mode == "nt":
        (M, K), (N, K2) = a.shape, b.shape
    else:
        (K, M), (K2, N) = a.shape, b.shape
    assert K == K2, (mode, a.shape, b.shape)
    tm, tn, tk = _tile(M, 512, 128), _tile(N, 1536, 128), _tile(K, 2048, 128)
    nk = K // tk
    if mode == "nn":
        a_spec = pl.BlockSpec((tm, tk), lambda i, j, k: (i, k))
        b_spec = pl.BlockSpec((tk, tn), lambda i, j, k: (k, j))
        dn = (((1,), (0,)), ((), ()))
    elif mode == "nt":
        a_spec = pl.BlockSpec((tm, tk), lambda i, j, k: (i, k))
        b_spec = pl.BlockSpec((tn, tk), lambda i, j, k: (j, k))
        dn = (((1,), (1,)), ((), ()))
    else:
        a_spec = pl.BlockSpec((tk, tm), lambda i, j, k: (k, i))
        b_spec = pl.BlockSpec((tk, tn), lambda i, j, k: (k, j))
        dn = (((0,), (0,)), ((), ()))
    o_spec = pl.BlockSpec((tm, tn), lambda i, j, k: (i, j))
    has_add = add is not None

    def body(*refs):
        a_ref, b_ref = refs[0], refs[1]
        add_ref = refs[2] if has_add else None
        o_ref = refs[3] if has_add else refs[2]
        acc_ref = refs[-1]
        p = lax.dot_general(a_ref[...].astype(BF16), b_ref[...].astype(BF16), dn,
                            preferred_element_type=F32)

        def finish(total):
            if has_add:
                total = total + add_ref[...]
            o_ref[...] = total.astype(o_ref.dtype)

        if nk == 1:
            finish(p)
        else:
            k = pl.program_id(2)

            @pl.when(k == 0)
            def _():
                acc_ref[...] = p

            @pl.when(k > 0)
            def _():
                acc_ref[...] += p

            @pl.when(k == nk - 1)
            def _():
                finish(acc_ref[...])

    ins = [a, b] + ([add] if has_add else [])
    specs = [a_spec, b_spec] + ([o_spec] if has_add else [])
    return pl.pallas_call(
        body, name=name, grid=(M // tm, N // tn, nk), in_specs=specs, out_specs=o_spec,
        out_shape=jax.ShapeDtypeStruct((M, N), out_dtype),
        scratch_shapes=[pltpu.VMEM((tm, tn), F32)],
        compiler_params=_cp("parallel", "parallel", "arbitrary"),
    )(*ins)


def _row_spec(arr, tile):
    if isinstance(arr, tuple):
        a, w, cb = arr[:3]
        ro = (arr[3] // tile) if len(arr) > 3 else 0
        assert len(arr) < 4 or arr[3] % tile == 0
        return a, pl.BlockSpec((tile, w), lambda i, cb=cb, ro=ro: (i + ro, cb)), (tile, w)
    if arr.ndim == 3:
        d0, _, d2 = arr.shape
        return arr, pl.BlockSpec((d0, tile, d2), lambda i: (0, i, 0)), (d0, tile, d2)
    return arr, pl.BlockSpec((tile, arr.shape[1]), lambda i: (i, 0)), (tile, arr.shape[1])


def _full_spec(arr):
    nd = arr.ndim
    return pl.BlockSpec(arr.shape, lambda i, nd=nd: (0,) * nd)


def _n_rows(arr):
    a = arr[0] if isinstance(arr, tuple) else arr
    return a.shape[1] if a.ndim == 3 else a.shape[0]


def _row_out_shape(shape_tail, n, dtype):
    if isinstance(shape_tail, tuple):
        d0, d2 = shape_tail
        return (jax.ShapeDtypeStruct((d0, n, d2), dtype),
                lambda tile: pl.BlockSpec((d0, tile, d2), lambda i: (0, i, 0)))
    return (jax.ShapeDtypeStruct((n, shape_tail), dtype),
            lambda tile: pl.BlockSpec((tile, shape_tail), lambda i: (i, 0)))


def _f32(v):
    return v.astype(F32) if v.dtype == BF16 else v


def row_fwd(name, fn, rows, fulls, outs, tile=ROW_TILE):
    n = _n_rows(rows[0])
    tile = min(tile, n)
    assert n % tile == 0, (name, n, tile)
    rs = [_row_spec(r, tile) for r in rows]
    os_ = [_row_out_shape(w, n, dt) for w, dt in outs]
    nr, nf = len(rows), len(fulls)

    def body(*refs):
        vals = [_f32(r[...]) for r in refs[:nr + nf]]
        res = fn(*vals)
        for r, v in zip(refs[nr + nf:], res):
            r[...] = v.astype(r.dtype)

    out = pl.pallas_call(
        body, name=name, grid=(n // tile,),
        in_specs=[s for _, s, _ in rs] + [_full_spec(f) for f in fulls],
        out_specs=[mk(tile) for _, mk in os_], out_shape=[sh for sh, _ in os_],
        compiler_params=_cp("parallel"),
    )(*[a for a, _, _ in rs], *fulls)
    return out


def row_bwd(name, fn, rows, fulls, cts, want_rows, want_fulls, row_dtypes=None, tile=ROW_TILE):
    n = _n_rows(rows[0])
    tile = min(tile, n)
    assert n % tile == 0, (name, n, tile)
    rs = [_row_spec(r, tile) for r in rows]
    cs = [_row_spec(c, tile) for c in cts]
    nr, nf, nc = len(rows), len(fulls), len(cts)
    row_dtypes = row_dtypes or [F32] * len(want_rows)
    out_shapes, out_specs = [], []
    for k, idx in enumerate(want_rows):
        a, _, blk = rs[idx]
        if len(blk) == 3:
            sh, mk = _row_out_shape((blk[0], blk[2]), n, row_dtypes[k])
        else:
            sh, mk = _row_out_shape(blk[1], n, row_dtypes[k])
        out_shapes.append(sh)
        out_specs.append(mk(tile))
    for idx in want_fulls:
        out_shapes.append(jax.ShapeDtypeStruct(fulls[idx].shape, F32))
        out_specs.append(_full_spec(fulls[idx]))
    n_wr = len(want_rows)

    def body(*refs):
        i = pl.program_id(0)
        vals = [_f32(r[...]) for r in refs[:nr + nf]]
        ct_vals = [_f32(r[...]) for r in refs[nr + nf:nr + nf + nc]]
        outs = refs[nr + nf + nc:]
        _, vjp = jax.vjp(fn, *vals)
        grads = vjp(tuple(ct_vals))
        for k, idx in enumerate(want_rows):
            outs[k][...] = grads[idx].astype(outs[k].dtype)
        for k, idx in enumerate(want_fulls):
            o = outs[n_wr + k]
            g = grads[nr + idx]

            @pl.when(i == 0)
            def _(o=o, g=g):
                o[...] = g

            @pl.when(i > 0)
            def _(o=o, g=g):
                o[...] += g

    out = pl.pallas_call(
        body, name=name, grid=(n // tile,),
        in_specs=[s for _, s, _ in rs] + [_full_spec(f) for f in fulls] + [s for _, s, _ in cs],
        out_specs=out_specs, out_shape=out_shapes,
        compiler_params=_cp("arbitrary"),
    )(*[a for a, _, _ in rs], *fulls, *[a for a, _, _ in cs])
    return out


def whole(name, fn, args, outs):
    def body(*refs):
        res = fn(*[r[...] for r in refs[:len(args)]])
        for r, v in zip(refs[len(args):], res):
            r[...] = v.astype(r.dtype)

    return pl.pallas_call(
        body, name=name, out_shape=[jax.ShapeDtypeStruct(s, d) for s, d in outs],
        compiler_params=pltpu.CompilerParams(vmem_limit_bytes=VMEM_LIMIT),
    )(*args)


def f_norm(x, g):
    return (_rms(x, g),)


def f_norm_res(x, g):
    return _rms(x, g), x


def f_swiglu(g, u):
    return (g * jax.nn.sigmoid(g) * u,)


def f_glu(y, w, b):
    h = jax.nn.gelu(y)
    return (h * jax.nn.sigmoid(mm(h, w) + b),)


def f_xattn(q, kv, qg, kg):
    outs = []
    for h in range(XA_H):
        sl = slice(h * XA_D, (h + 1) * XA_D)
        qn = _rms(q[:, sl], qg)
        kn = _rms(kv[:, sl], kg)
        vh = kv[:, D + h * XA_D:D + (h + 1) * XA_D]
        p = _softmax(mm_nt(qn, kn) * (XA_D ** -0.5))
        outs.append(mm(p, vh))
    return (jnp.concatenate(outs, axis=-1),)


def f_adam(w, g, m, v):
    m2 = B1 * m + (1.0 - B1) * g
    v2 = B2 * v + (1.0 - B2) * (g * g)
    m_hat = m2 / (1.0 - B1 ** STEP)
    v_hat = v2 / (1.0 - B2 ** STEP)
    delta = -LR * (m_hat / (jnp.sqrt(v_hat) + AEPS) + WD * w)
    return delta, m2, v2


S5_NTAB, S5_NROW = 4, 6


def _s5_tables(are, aim, ldt):
    T = CHUNK
    dt = jnp.exp(ldt)
    ar, ai = are * dt, aim * dt
    t = lax.broadcasted_iota(jnp.int32, (T, 1), 0).astype(F32)
    mag, inv = jnp.exp(t * ar), jnp.exp(-t * ar)
    cs, sn = jnp.cos(t * ai), jnp.sin(t * ai)
    e_re, e_im = mag * cs, mag * sn
    n_re, n_im = inv * cs, -inv * sn
    l_re, l_im = jnp.exp(ar) * jnp.cos(ai), jnp.exp(ar) * jnp.sin(ai)
    den = are * are + aim * aim
    k_re = ((l_re - 1.0) * are + l_im * aim) / den
    k_im = (l_im * are - (l_re - 1.0) * aim) / den
    tl = float(T - 1)
    m_re, m_im = jnp.exp(tl * ar) * jnp.cos(tl * ai), jnp.exp(tl * ar) * jnp.sin(tl * ai)
    return (e_re, e_im, n_re, n_im), (l_re, l_im, k_re, k_im, m_re, m_im)


def _s5_chunk(u, sre, sim, tabs, rows, b_re, b_im, c_re, c_im, dv):
    e_re, e_im, n_re, n_im = tabs
    l_re, l_im, k_re, k_im, m_re, m_im = rows
    x_re, x_im = mm(u, b_re), mm(u, b_im)
    bu_re = k_re * x_re - k_im * x_im
    bu_im = k_re * x_im + k_im * x_re
    v_re = bu_re * n_re - bu_im * n_im
    v_im = bu_re * n_im + bu_im * n_re
    p_re = l_re * sre - l_im * sim
    p_im = l_re * sim + l_im * sre
    w_re = cumsum_rows(v_re) + p_re
    w_im = cumsum_rows(v_im) + p_im
    s_re = e_re * w_re - e_im * w_im
    s_im = e_re * w_im + e_im * w_re
    y = mm(s_re, c_re) - mm(s_im, c_im) + dv * u
    z_re = jnp.sum(v_re, axis=0, keepdims=True) + p_re
    z_im = jnp.sum(v_im, axis=0, keepdims=True) + p_im
    return y, m_re * z_re - m_im * z_im, m_re * z_im + m_im * z_re


def _s5_fill_tables(are_ref, aim_ref, ldt_ref, tab, row):
    for g in range(S5_GB):
        ls = slice(512 * g, 512 * (g + 1))
        tabs, rows = _s5_tables(are_ref[:, ls], aim_ref[:, ls], ldt_ref[:, ls])
        for k, t in enumerate(tabs):
            tab[k, :, ls] = t
        for k, r in enumerate(rows):
            row[k:k + 1, ls] = r


def _s5_read_tables(tab, row, ls):
    return (tuple(tab[k, :, ls] for k in range(S5_NTAB)), tuple(row[k:k + 1, ls] for k in range(S5_NROW)))


def _s5_specs(nc, rev):
    T = CHUNK

    def ci(c):
        return nc - 1 - c if rev else c

    u_spec = pl.BlockSpec((T, S5_W), lambda c: (ci(c), 0))
    p_spec = pl.BlockSpec((1, S5_G * S5_P), lambda c: (0, 0))
    b_spec = pl.BlockSpec((S5_GB, 128, 512), lambda c: (0, 0, 0))
    c_spec = pl.BlockSpec((S5_GB, 512, 128), lambda c: (0, 0, 0))
    d_spec = pl.BlockSpec((1, S5_W), lambda c: (0, 0))
    st_spec = pl.BlockSpec((None, 2, S5_G * S5_P), lambda c: (ci(c), 0, 0))
    return u_spec, p_spec, b_spec, c_spec, d_spec, st_spec


def s5_fwd(pm, are, aim, ldt, b_re, b_im, c_re, c_im, dv):
    S = pm.shape[0]
    nc = S // CHUNK
    u_spec, p_spec, b_spec, c_spec, d_spec, st_spec = _s5_specs(nc, False)

    def body(u_ref, are_ref, aim_ref, ldt_ref, bre_ref, bim_ref, cre_ref, cim_ref, dv_ref,
             y_ref, st_ref, state, tab, row):
        c = pl.program_id(0)

        @pl.when(c == 0)
        def _():
            state[...] = jnp.zeros_like(state)
            _s5_fill_tables(are_ref, aim_ref, ldt_ref, tab, row)

        st_ref[...] = state[...]
        for g in range(S5_GB):
            lu, ls = slice(128 * g, 128 * (g + 1)), slice(512 * g, 512 * (g + 1))
            tabs, rows = _s5_read_tables(tab, row, ls)
            y, e_re, e_im = _s5_chunk(u_ref[:, lu], state[0:1, ls], state[1:2, ls], tabs, rows,
                                      bre_ref[g], bim_ref[g], cre_ref[g], cim_ref[g], dv_ref[:, lu])
            y_ref[:, lu] = y
            state[0:1, ls] = e_re
            state[1:2, ls] = e_im

    n_state = S5_G * S5_P
    return pl.pallas_call(
        body, name="s5_fwd", grid=(nc,),
        in_specs=[u_spec, p_spec, p_spec, p_spec, b_spec, b_spec, c_spec, c_spec, d_spec],
        out_specs=[u_spec, st_spec],
        out_shape=[jax.ShapeDtypeStruct((S, S5_W), F32), jax.ShapeDtypeStruct((nc, 2, n_state), F32)],
        scratch_shapes=[pltpu.VMEM((2, n_state), F32), pltpu.VMEM((S5_NTAB, CHUNK, n_state), F32),
                        pltpu.VMEM((8, n_state), F32)],
        compiler_params=_cp("arbitrary"),
    )(pm, are, aim, ldt, b_re, b_im, c_re, c_im, dv)


def s5_bwd(pm, st, dy, are, aim, ldt, b_re, b_im, c_re, c_im, dv):
    S = pm.shape[0]
    nc = S // CHUNK
    u_spec, p_spec, b_spec, c_spec, d_spec, st_spec = _s5_specs(nc, True)

    def body(u_ref, st_ref, dy_ref, are_ref, aim_ref, ldt_ref, bre_ref, bim_ref, cre_ref, cim_ref, dv_ref,
             du_ref, dare_ref, daim_ref, dldt_ref, dbre_ref, dbim_ref, dcre_ref, dcim_ref, ddv_ref,
             dstate, tab, row, dtab, drow):
        c = pl.program_id(0)

        @pl.when(c == 0)
        def _():
            dstate[...] = jnp.zeros_like(dstate)
            dtab[...] = jnp.zeros_like(dtab)
            drow[...] = jnp.zeros_like(drow)
            _s5_fill_tables(are_ref, aim_ref, ldt_ref, tab, row)

        for g in range(S5_GB):
            lu, ls = slice(128 * g, 128 * (g + 1)), slice(512 * g, 512 * (g + 1))
            every = slice(None)
            tabs, rows = _s5_read_tables(tab, row, ls)
            args = (u_ref[:, lu], st_ref[0:1, ls], st_ref[1:2, ls], tabs, rows,
                    bre_ref[g], bim_ref[g], cre_ref[g], cim_ref[g], dv_ref[:, lu])
            _, vjp = jax.vjp(_s5_chunk, *args)
            gr = vjp((dy_ref[:, lu], dstate[0:1, ls], dstate[1:2, ls]))
            du_ref[:, lu] = gr[0]
            dstate[0:1, ls] = gr[1]
            dstate[1:2, ls] = gr[2]
            for k, t in enumerate(gr[3]):
                dtab[k, :, ls] += t
            for k, r in enumerate(gr[4]):
                drow[k:k + 1, ls] += r
            accs = ((dbre_ref, (g,)), (dbim_ref, (g,)), (dcre_ref, (g,)), (dcim_ref, (g,)), (ddv_ref, (every, lu)))
            for (o, idx), gv in zip(accs, gr[5:]):
                @pl.when(c == 0)
                def _(o=o, idx=idx, gv=gv):
                    o[idx] = gv

                @pl.when(c > 0)
                def _(o=o, idx=idx, gv=gv):
                    o[idx] += gv

        @pl.when(c == nc - 1)
        def _():
            for g in range(S5_GB):
                ls = slice(512 * g, 512 * (g + 1))
                _, vjp = jax.vjp(_s5_tables, are_ref[:, ls], aim_ref[:, ls], ldt_ref[:, ls])
                dtabs, drows = _s5_read_tables(dtab, drow, ls)
                ga, gi, gl = vjp((dtabs, drows))
                dare_ref[:, ls] = ga
                daim_ref[:, ls] = gi
                dldt_ref[:, ls] = gl

    n_state = S5_G * S5_P
    return pl.pallas_call(
        body, name="s5_bwd", grid=(nc,),
        in_specs=[u_spec, st_spec, u_spec, p_spec, p_spec, p_spec, b_spec, b_spec, c_spec, c_spec, d_spec],
        out_specs=[u_spec, p_spec, p_spec, p_spec, b_spec, b_spec, c_spec, c_spec, d_spec],
        out_shape=[jax.ShapeDtypeStruct((S, S5_W), F32)] + [jax.ShapeDtypeStruct((1, n_state), F32)] * 3
        + [jax.ShapeDtypeStruct((S5_GB, 128, 512), F32)] * 2 + [jax.ShapeDtypeStruct((S5_GB, 512, 128), F32)] * 2
        + [jax.ShapeDtypeStruct((1, S5_W), F32)],
        scratch_shapes=[pltpu.VMEM((2, n_state), F32), pltpu.VMEM((S5_NTAB, CHUNK, n_state), F32),
                        pltpu.VMEM((8, n_state), F32), pltpu.VMEM((S5_NTAB, CHUNK, n_state), F32),
                        pltpu.VMEM((8, n_state), F32)],
        compiler_params=_cp("arbitrary"),
    )(pm, st, dy, are, aim, ldt, b_re, b_im, c_re, c_im, dv)


def conv_fwd(pm, w):
    S = pm.shape[0]

    def body(x_ref, w_ref, o_ref, pad):
        x = x_ref[...]
        pad[0:8, :] = jnp.zeros((8, 128), F32)
        pad[8:, :] = x
        y = (w_ref[3:4, :] * x + w_ref[2:3, :] * pad[7:7 + S, :] + w_ref[1:2, :] * pad[6:6 + S, :]
             + w_ref[0:1, :] * pad[5:5 + S, :])
        o_ref[...] = y * jax.nn.sigmoid(y)

    return pl.pallas_call(
        body, name="conv_fwd", grid=(12,),
        in_specs=[pl.BlockSpec((S, 128), lambda j: (0, 4 + j)), pl.BlockSpec((4, 128), lambda j: (0, j))],
        out_specs=pl.BlockSpec((S, 128), lambda j: (0, j)),
        out_shape=jax.ShapeDtypeStruct((S, 1536), F32),
        scratch_shapes=[pltpu.VMEM((S + 8, 128), F32)],
        compiler_params=_cp("parallel"),
    )(pm, w)


def conv_bwd(pm, w, dout):
    S = pm.shape[0]

    def body(x_ref, w_ref, do_ref, dx_ref, dw_ref, pad, dpad):
        x = x_ref[...]
        pad[0:8, :] = jnp.zeros((8, 128), F32)
        pad[8:, :] = x
        xs = [pad[5:5 + S, :], pad[6:6 + S, :], pad[7:7 + S, :], x]
        y = w_ref[0:1, :] * xs[0] + w_ref[1:2, :] * xs[1] + w_ref[2:3, :] * xs[2] + w_ref[3:4, :] * xs[3]
        sg = jax.nn.sigmoid(y)
        dy = do_ref[...] * (sg + y * sg * (1.0 - sg))
        dpad[0:S, :] = dy
        dpad[S:, :] = jnp.zeros((8, 128), F32)
        dx_ref[...] = (w_ref[3:4, :] * dy + w_ref[2:3, :] * dpad[1:1 + S, :] + w_ref[1:2, :] * dpad[2:2 + S, :]
                       + w_ref[0:1, :] * dpad[3:3 + S, :])
        for i in range(4):
            dw_ref[i:i + 1, :] = jnp.sum(dy * xs[i], axis=0, keepdims=True)

    return pl.pallas_call(
        body, name="conv_bwd", grid=(12,),
        in_specs=[pl.BlockSpec((S, 128), lambda j: (0, 4 + j)), pl.BlockSpec((4, 128), lambda j: (0, j)),
                  pl.BlockSpec((S, 128), lambda j: (0, j))],
        out_specs=[pl.BlockSpec((S, 128), lambda j: (0, j)), pl.BlockSpec((4, 128), lambda j: (0, j))],
        out_shape=[jax.ShapeDtypeStruct((S, 1536), F32), jax.ShapeDtypeStruct((4, 1536), F32)],
        scratch_shapes=[pltpu.VMEM((S + 8, 128), F32), pltpu.VMEM((S + 8, 128), F32)],
        compiler_params=_cp("parallel"),
    )(pm, w, dout)


GDN_SUP = 4
GDN_ROWS = GDN_SUP * CHUNK


def _gdn_chunk(q, k, v, gate, al, bl, alog, dtb, og, state):
    R = q.shape[0]
    r = lax.broadcasted_iota(jnp.int32, (R, R), 0)
    c = lax.broadcasted_iota(jnp.int32, (R, R), 1)
    same = (r // CHUNK) == (c // CHUNK)
    eye = (r == c).astype(F32)
    strict, causal, upper = same & (r > c), same & (r >= c), same & (r <= c)
    qn = q * lax.rsqrt(jnp.sum(q * q, axis=-1, keepdims=True) + EPS) * (GDN_D ** -0.5)
    kn = k * lax.rsqrt(jnp.sum(k * k, axis=-1, keepdims=True) + EPS)
    beta = jax.nn.sigmoid(bl)
    g = -jnp.exp(alog) * _softplus(al + dtb)
    g_row = jnp.sum(eye * g, axis=0, keepdims=True)
    gc_col = jnp.sum(jnp.where(causal, g_row, 0.0), axis=1, keepdims=True)
    gc_row = jnp.sum(jnp.where(upper, g, 0.0), axis=0, keepdims=True)
    gtot = jnp.sum(jnp.where(same, g_row, 0.0), axis=1, keepdims=True)
    gamma = jnp.exp(gc_col)
    diff = gc_col - gc_row
    d_strict = jnp.where(strict, jnp.exp(jnp.where(strict, diff, 0.0)), 0.0)
    d_causal = jnp.where(causal, jnp.exp(jnp.where(causal, diff, 0.0)), 0.0)
    a = beta * mm_nt(kn, kn) * d_strict
    p = -a
    x = eye + p
    for _ in range(5):
        p = mm(p, p)
        x = x + mm(x, p)
    u_new = mm(x, beta * v)
    w_k = mm(x, (beta * gamma) * kn)
    qk = mm_nt(qn, kn) * d_causal
    q_g = qn * gamma
    k_tail = kn * jnp.exp(gtot - gc_col)
    ws, os_ = [], []
    for i in range(R // CHUNK):
        rows = slice(CHUNK * i, CHUNK * (i + 1))
        w_i = u_new[rows] - mm(w_k[rows], state)
        os_.append(mm(q_g[rows], state))
        decay = jnp.exp(jnp.sum(g[rows], axis=0, keepdims=True))
        state = decay * state + mm_tn(k_tail[rows], w_i)
        ws.append(w_i)
    o = jnp.concatenate(os_, axis=0) + mm(qk, jnp.concatenate(ws, axis=0))
    out = _rms(o, og) * (gate * jax.nn.sigmoid(gate))
    return out, state


def _gdn_specs(nc, rev):
    def ci(c):
        return nc - 1 - c if rev else c

    def blk(cb):
        return pl.BlockSpec((GDN_ROWS, 512), lambda c: (ci(c), cb))

    col = lambda n: pl.BlockSpec((n, GDN_ROWS, 1), lambda c: (0, ci(c), 0))
    sc = pl.BlockSpec((GDN_H, 1, 1), lambda c: (0, 0, 0))
    og = pl.BlockSpec((1, 128), lambda c: (0, 0))
    st = pl.BlockSpec((GDN_H, None, 128, 128), lambda c: (0, ci(c), 0, 0))
    return blk, col, sc, og, st


def gdn_fwd(qkvc, pm, abt, alog, dtb, og):
    S = qkvc.shape[0]
    nc = S // GDN_ROWS
    blk, col, sc, ogs, st = _gdn_specs(nc, False)

    def body(q_ref, k_ref, v_ref, gate_ref, ab_ref, alog_ref, dtb_ref, og_ref, o_ref, st_ref, state):
        c = pl.program_id(0)

        @pl.when(c == 0)
        def _():
            state[...] = jnp.zeros_like(state)

        st_ref[...] = state[...]
        for h in range(GDN_H):
            sl = slice(GDN_D * h, GDN_D * (h + 1))
            out, new_state = _gdn_chunk(q_ref[:, sl], k_ref[:, sl], v_ref[:, sl], gate_ref[:, sl], ab_ref[h],
                                        ab_ref[GDN_H + h], alog_ref[h], dtb_ref[h], og_ref[...], state[h])
            o_ref[:, sl] = out
            state[h] = new_state

    return pl.pallas_call(
        body, name="gdn_fwd", grid=(nc,),
        in_specs=[blk(0), blk(1), blk(2), blk(4), col(2 * GDN_H), sc, sc, ogs],
        out_specs=[blk(0), st],
        out_shape=[jax.ShapeDtypeStruct((S, 512), F32), jax.ShapeDtypeStruct((GDN_H, nc, 128, 128), F32)],
        scratch_shapes=[pltpu.VMEM((GDN_H, 128, 128), F32)],
        compiler_params=_cp("arbitrary"),
    )(qkvc, qkvc, qkvc, pm, abt, alog, dtb, og)


def gdn_bwd(qkvc, pm, abt, alog, dtb, og, st, dout):
    S = qkvc.shape[0]
    nc = S // GDN_ROWS
    blk, col, sc, ogs, sts = _gdn_specs(nc, True)

    def body(q_ref, k_ref, v_ref, gate_ref, ab_ref, alog_ref, dtb_ref, og_ref, st_ref, do_ref,
             dq_ref, dk_ref, dv_ref, dgate_ref, dal_ref, dbl_ref, dalog_ref, ddtb_ref, dog_ref, dstate):
        c = pl.program_id(0)

        @pl.when(c == 0)
        def _():
            dstate[...] = jnp.zeros_like(dstate)

        for h in range(GDN_H):
            sl = slice(GDN_D * h, GDN_D * (h + 1))
            args = (q_ref[:, sl], k_ref[:, sl], v_ref[:, sl], gate_ref[:, sl], ab_ref[h], ab_ref[GDN_H + h],
                    alog_ref[h], dtb_ref[h], og_ref[...], st_ref[h])
            _, vjp = jax.vjp(_gdn_chunk, *args)
            g = vjp((do_ref[:, sl], dstate[h]))
            for o, gv in zip((dq_ref, dk_ref, dv_ref, dgate_ref), g[:4]):
                o[:, sl] = gv
            dal_ref[h] = g[4]
            dbl_ref[h] = g[5]
            dstate[h] = g[9]
            for o, gv in zip((dalog_ref, ddtb_ref, dog_ref), g[6:9]):
                @pl.when(c == 0)
                def _(o=o, gv=gv, h=h):
                    o[h] = gv

                @pl.when(c > 0)
                def _(o=o, gv=gv, h=h):
                    o[h] += gv

    ogo = pl.BlockSpec((GDN_H, 1, 128), lambda c: (0, 0, 0))
    sd = jax.ShapeDtypeStruct
    return pl.pallas_call(
        body, name="gdn_bwd", grid=(nc,),
        in_specs=[blk(0), blk(1), blk(2), blk(4), col(2 * GDN_H), sc, sc, ogs, sts, blk(0)],
        out_specs=[blk(0), blk(0), blk(0), blk(0), col(GDN_H), col(GDN_H), sc, sc, ogo],
        out_shape=[sd((S, 512), F32)] * 4 + [sd((GDN_H, S, 1), F32)] * 2 + [sd((GDN_H, 1, 1), F32)] * 2
        + [sd((GDN_H, 1, 128), F32)],
        scratch_shapes=[pltpu.VMEM((GDN_H, 128, 128), F32)],
        compiler_params=_cp("arbitrary"),
    )(qkvc, qkvc, qkvc, pm, abt, alog, dtb, og, st, dout)


HG = 8
HG_LANES = HG * CA_D


def _group_mean_raw(y):
    r = lax.broadcasted_iota(jnp.int32, (128, 128), 0)
    c = lax.broadcasted_iota(jnp.int32, (128, 128), 1)
    g = jnp.where((r // CA_D) == (c // CA_D), 1.0 / CA_D, 0.0).astype(BF16)
    d = lambda u: lax.dot_general(u, g, (((1,), (0,)), ((), ())), preferred_element_type=F32)
    outs = []
    for j in range(y.shape[1] // 128):
        hi, lo = _split(y[:, 128 * j:128 * (j + 1)])
        outs.append(d(hi) + d(lo))
    return jnp.concatenate(outs, axis=1)


@jax.custom_vjp
def group_mean(y):
    return _group_mean_raw(y)


group_mean.defvjp(lambda y: (_group_mean_raw(y), None), lambda _, g: (_group_mean_raw(g),))


def f_headnorm(t, g):
    return (t * lax.rsqrt(group_mean(t * t) + EPS) * g,)


def _cattn_chunk(q, kb, vb, bias, valid):
    lane = lax.broadcasted_iota(jnp.int32, (1, 128), 1)
    m0 = (lane < CA_D).astype(F32)
    m1 = 1.0 - m0
    pairs = range(HG // 2)
    sl = [slice(128 * p, 128 * (p + 1)) for p in pairs]
    q2 = [jnp.concatenate([q[:, s] * m0, q[:, s] * m1], axis=0) for s in sl]
    sc = [mm_nt(q2[p], kb[:, sl[p]]) * (CA_D ** -0.5) + bias[sl[p]] for p in pairs]
    pr = [_softmax(jnp.where(valid, s, -1e30)) for s in sc]
    o2 = [mm(pr[p], vb[:, sl[p]]) for p in pairs]
    return jnp.concatenate([o[:CHUNK] * m0 + o[CHUNK:] * m1 for o in o2], axis=1)


def _cattn_valid(c):
    pos = lax.broadcasted_iota(jnp.int32, (1, CA_BAND), 1) + c * CHUNK
    return pos >= CA_PAD


def _cattn_specs(S):
    q_spec = pl.BlockSpec((CHUNK, HG_LANES), lambda h, c: (c, h))
    kv_spec = pl.BlockSpec((S + CA_PAD, HG_LANES), lambda h, c: (0, h))
    b_spec = pl.BlockSpec((HG * CHUNK, CA_BAND), lambda h, c: (h, 0))
    return q_spec, kv_spec, b_spec


def cattn_fwd(qn, kp, vp, bias):
    S = qn.shape[0]
    nc = S // CHUNK
    q_spec, kv_spec, b_spec = _cattn_specs(S)

    def body(q_ref, k_ref, v_ref, b_ref, o_ref):
        c = pl.program_id(1)
        start = pl.multiple_of(c * CHUNK, CHUNK)
        kb = k_ref[pl.ds(start, CA_BAND), :]
        vb = v_ref[pl.ds(start, CA_BAND), :]
        o_ref[...] = _cattn_chunk(q_ref[...], kb, vb, b_ref[...], _cattn_valid(c)).astype(o_ref.dtype)

    return pl.pallas_call(
        body, name="cattn_fwd", grid=(CA_H // HG, nc), in_specs=[q_spec, kv_spec, kv_spec, b_spec],
        out_specs=q_spec, out_shape=jax.ShapeDtypeStruct((S, D), BF16),
        compiler_params=_cp("parallel", "arbitrary"),
    )(qn, kp, vp, bias)


def kv_prep(qkv, kg):
    S = qkv.shape[0]
    tile = ROW_TILE
    lead = CA_PAD // tile

    def body(k_ref, v_ref, g_ref, kp_ref, vp_ref):
        i = pl.program_id(0)

        @pl.when(i < lead)
        def _():
            kp_ref[...] = jnp.zeros_like(kp_ref)
            vp_ref[...] = jnp.zeros_like(vp_ref)

        @pl.when(i >= lead)
        def _():
            kp_ref[...] = f_headnorm(k_ref[...], g_ref[...])[0].astype(BF16)
            vp_ref[...] = v_ref[...].astype(BF16)

    src = lambda cb: pl.BlockSpec((tile, D), lambda i, cb=cb: (jnp.maximum(i - lead, 0), cb))
    out = pl.BlockSpec((tile, D), lambda i: (i, 0))
    return pl.pallas_call(
        body, name="kv_prep", grid=((S + CA_PAD) // tile,),
        in_specs=[src(1), src(2), pl.BlockSpec((1, D), lambda i: (0, 0))], out_specs=[out, out],
        out_shape=[jax.ShapeDtypeStruct((S + CA_PAD, D), BF16)] * 2,
        compiler_params=_cp("parallel"),
    )(qkv, qkv, kg)


def cattn_bwd(qn, kp, vp, bias, do):
    S = qn.shape[0]
    nc = S // CHUNK
    q_spec, kv_spec, b_spec = _cattn_specs(S)

    def body(q_ref, k_ref, v_ref, b_ref, do_ref, dq_ref, dk_ref, dv_ref, db_ref):
        c = pl.program_id(1)

        @pl.when(c == 0)
        def _():
            dk_ref[...] = jnp.zeros_like(dk_ref)
            dv_ref[...] = jnp.zeros_like(dv_ref)
            db_ref[...] = jnp.zeros_like(db_ref)

        start = pl.multiple_of(c * CHUNK, CHUNK)
        kb = k_ref[pl.ds(start, CA_BAND), :].astype(F32)
        vb = v_ref[pl.ds(start, CA_BAND), :].astype(F32)
        valid = _cattn_valid(c)
        _, vjp = jax.vjp(lambda q, k, v, b: _cattn_chunk(q, k, v, b, valid), q_ref[...], kb, vb, b_ref[...])
        dq, dk, dv, db = vjp(do_ref[...])
        dq_ref[...] = dq
        dk_ref[pl.ds(start, CA_BAND), :] += dk
        dv_ref[pl.ds(start, CA_BAND), :] += dv
        db_ref[...] += db

    sd = jax.ShapeDtypeStruct
    return pl.pallas_call(
        body, name="cattn_bwd", grid=(CA_H // HG, nc), in_specs=[q_spec, kv_spec, kv_spec, b_spec, q_spec],
        out_specs=[q_spec, kv_spec, kv_spec, b_spec],
        out_shape=[sd((S, D), F32), sd((S + CA_PAD, D), F32), sd((S + CA_PAD, D), F32),
                   sd((CA_H * CHUNK, CA_BAND), F32)],
        compiler_params=_cp("parallel", "arbitrary"),
    )(qn, kp, vp, bias, do)


_REL_IDX = np.clip(np.arange(CHUNK)[:, None] - np.arange(CA_BAND)[None, :] + CA_PAD, -MAX_REL, MAX_REL) + MAX_REL
SKEW_W = CA_BAND + CHUNK


def rel_bias_grad(dbias):
    padded = jnp.pad(dbias, ((0, 0), (0, 0), (CHUNK, 0)))
    flat = jnp.pad(padded.reshape(CA_H, CHUNK * SKEW_W), ((0, 0), (0, CHUNK)))
    skew = flat.reshape(CA_H, CHUNK, SKEW_W + 1)

    first_near = SKEW_W - CHUNK - MAX_REL

    def fn(t):
        colsum = jnp.sum(t, axis=1, keepdims=True)
        j = lax.broadcasted_iota(jnp.int32, colsum.shape, 2)
        far = jnp.sum(jnp.where(j < first_near, colsum, 0.0), axis=2, keepdims=True)
        return (colsum + jnp.where(j == first_near, far, 0.0),)

    (colsum,) = whole("relbias_sum", fn, [skew], [((CA_H, 1, SKEW_W + 1), F32)])
    near = colsum[:, 0, first_near:SKEW_W][:, ::-1]
    return jnp.concatenate([jnp.zeros((CA_H, CHUNK + 1), F32), near], axis=1)


def rel_bias_expand(rb):
    near = rb[:, CHUNK + 1:][:, ::-1]
    far = jnp.broadcast_to(rb[:, 2 * MAX_REL:], (CA_H, SKEW_W - CHUNK - MAX_REL))
    t = jnp.concatenate([far, near, jnp.zeros((CA_H, 1), rb.dtype)], axis=1)
    rows = jnp.tile(t, (1, CHUNK))[:, :CHUNK * SKEW_W].reshape(CA_H, CHUNK, SKEW_W)
    return rows[:, :, CHUNK:]


def loss_head(y, target):
    S = y.shape[0]
    tile = min(ROW_TILE, S)

    def body(y_ref, t_ref, dy_ref, acc_ref):
        i = pl.program_id(0)
        e = y_ref[...] - t_ref[...]
        dy_ref[...] = e * (1.0 / D)
        part = jnp.sum(e * e, axis=0, keepdims=True) * (0.5 / D)

        @pl.when(i == 0)
        def _():
            acc_ref[...] = part

        @pl.when(i > 0)
        def _():
            acc_ref[...] += part

    row = pl.BlockSpec((tile, D), lambda i: (i, 0))
    return pl.pallas_call(
        body, name="loss_head", grid=(S // tile,), in_specs=[row, row],
        out_specs=[row, pl.BlockSpec((1, D), lambda i: (0, 0))],
        out_shape=[jax.ShapeDtypeStruct((S, D), F32), jax.ShapeDtypeStruct((1, D), F32)],
        compiler_params=_cp("arbitrary"),
    )(y, target)


ANY = pl.BlockSpec(memory_space=pl.ANY)


HBM = pl.BlockSpec(memory_space=pltpu.HBM)
SEM = pl.BlockSpec(memory_space=pltpu.SEMAPHORE)
EFFECT = pltpu.SideEffectType.DATAFLOW_SIDE_EFFECTING


def _chip_copies(src_ref, land_ref, send_sems, recv_sems, per_dest):
    x, y, c = lax.axis_index("x"), lax.axis_index("y"), lax.axis_index("c")
    me = 2 * x + y
    if per_dest == "sibling":
        cp = pltpu.make_async_remote_copy(src_ref=src_ref, dst_ref=land_ref, send_sem=send_sems.at[0],
                                          recv_sem=recv_sems.at[0], device_id=(x, y, 1 - c), device_id_type=MESH)
        return [(cp, cp)]
    out = []
    for j, (px, py) in enumerate([(1 - x, y), (x, 1 - y), (1 - x, 1 - y)]):
        peer = 2 * px + py
        if per_dest:
            send = (src_ref.at[peer], land_ref.at[j])
            recv = (src_ref.at[me], land_ref.at[j])
        else:
            send = (src_ref, land_ref.at[me])
            recv = (src_ref, land_ref.at[peer])
        mk = lambda s, d, j=j, px=px, py=py: pltpu.make_async_remote_copy(
            src_ref=s, dst_ref=d, send_sem=send_sems.at[j], recv_sem=recv_sems.at[j],
            device_id=(px, py, c), device_id_type=MESH)
        out.append((mk(*send), mk(*recv)))
    return out


def exchange_start(src, land, carry, name, per_dest):
    def body(src_ref, land_ref, carry_ref, send_sems, recv_sems, src_out, land_out, carry_out):
        for send, _ in _chip_copies(src_ref, land_ref, send_sems, recv_sems, per_dest):
            send.start()

    hbm = lambda a: pltpu.HBM(a.shape, a.dtype)
    n = 1 if per_dest == "sibling" else 3
    return pl.pallas_call(
        body, name=name,
        out_shape=(pltpu.SemaphoreType.DMA((n,)), pltpu.SemaphoreType.DMA((n,)), hbm(src), hbm(land), hbm(carry)),
        in_specs=(HBM, HBM, HBM), out_specs=(SEM, SEM, HBM, HBM, HBM),
        input_output_aliases={0: 2, 1: 3, 2: 4},
        compiler_params=pltpu.CompilerParams(has_side_effects=EFFECT),
    )(pltpu.with_memory_space_constraint(src, pltpu.HBM), pltpu.with_memory_space_constraint(land, pltpu.HBM),
      pltpu.with_memory_space_constraint(carry, pltpu.HBM))


def exchange_wait(src, land, send_sems, recv_sems, after, name, per_dest):
    def body(src_ref, land_ref, send_sems_ref, recv_sems_ref, after_ref, src_out, land_out):
        for send, recv in _chip_copies(src_ref, land_ref, send_sems_ref, recv_sems_ref, per_dest):
            send.wait_send()
            recv.wait_recv()

    hbm = lambda a: pltpu.HBM(a.shape, a.dtype)
    return pl.pallas_call(
        body, name=name, out_shape=(hbm(src), hbm(land)),
        in_specs=(HBM, HBM, SEM, SEM, ANY), out_specs=(HBM, HBM), input_output_aliases={0: 0, 1: 1},
        compiler_params=pltpu.CompilerParams(has_side_effects=EFFECT),
    )(src, land, send_sems, recv_sems, after)


def sibling_exchange(srcs, name):
    n = len(srcs)

    def body(*refs):
        src_refs, out_refs, send_sems, recv_sems = refs[:n], refs[n:2 * n], refs[2 * n], refs[2 * n + 1]
        x, y, c = lax.axis_index("x"), lax.axis_index("y"), lax.axis_index("c")
        copies = [pltpu.make_async_remote_copy(src_ref=s, dst_ref=o, send_sem=send_sems.at[k], recv_sem=recv_sems.at[k],
                                               device_id=(x, y, 1 - c), device_id_type=MESH)
                  for k, (s, o) in enumerate(zip(src_refs, out_refs))]
        for cp in copies:
            cp.start()
        for cp in copies:
            cp.wait()

    return pl.pallas_call(
        body, name=name, in_specs=[ANY] * n, out_specs=[ANY] * n,
        out_shape=[jax.ShapeDtypeStruct(s.shape, s.dtype) for s in srcs],
        scratch_shapes=[pltpu.SemaphoreType.DMA((n,)), pltpu.SemaphoreType.DMA((n,))],
    )(*srcs)


def all_exchange(src, name):
    def body(src_ref, out_ref, send_sems, recv_sems, local_sem):
        x, y, c = lax.axis_index("x"), lax.axis_index("y"), lax.axis_index("c")
        me = 4 * x + 2 * y + c
        local = pltpu.make_async_copy(src_ref, out_ref.at[me], local_sem)
        local.start()
        sends = []
        peers = []
        for k in range(1, N_DEV):
            bx, by, bc = (k >> 2) & 1, (k >> 1) & 1, k & 1
            px = 1 - x if bx else x
            py = 1 - y if by else y
            pc = 1 - c if bc else c
            peers.append((px, py, pc))
        for k, peer in enumerate(peers):
            cp = pltpu.make_async_remote_copy(src_ref=src_ref, dst_ref=out_ref.at[me], send_sem=send_sems.at[k],
                                              recv_sem=recv_sems.at[k], device_id=peer, device_id_type=MESH)
            cp.start()
            sends.append(cp)
        for k, (px, py, pc) in enumerate(peers):
            pltpu.make_async_remote_copy(src_ref=src_ref, dst_ref=out_ref.at[4 * px + 2 * py + pc],
                                         send_sem=send_sems.at[k], recv_sem=recv_sems.at[k],
                                         device_id=(px, py, pc), device_id_type=MESH).wait_recv()
        for cp in sends:
            cp.wait_send()
        local.wait()

    return pl.pallas_call(
        body, name=name, in_specs=[ANY], out_specs=ANY,
        out_shape=jax.ShapeDtypeStruct((N_DEV,) + tuple(src.shape), src.dtype),
        scratch_shapes=[pltpu.SemaphoreType.DMA((N_DEV - 1,)), pltpu.SemaphoreType.DMA((N_DEV - 1,)),
                        pltpu.SemaphoreType.DMA],
    )(src)


def sum_own_slabs(own, land, name, tile=512):
    R, C = own.shape
    n = land.shape[0]
    tile = _tile(R, tile, 16)

    def body(o_ref, t_ref, out_ref):
        acc = o_ref[...].astype(F32)
        for s in range(n):
            acc = acc + t_ref[s].astype(F32)
        out_ref[...] = acc

    return pl.pallas_call(
        body, name=name, grid=(R // tile,),
        in_specs=[pl.BlockSpec((tile, C), lambda i: (i, 0)), pl.BlockSpec((n, tile, C), lambda i: (0, i, 0))],
        out_specs=pl.BlockSpec((tile, C), lambda i: (i, 0)), out_shape=jax.ShapeDtypeStruct((R, C), F32),
        compiler_params=_cp("parallel"),
    )(own, land)


def sum_slabs(t, name, tile=512):
    n, R, C = t.shape
    tile = _tile(R, tile, 16)

    def body(t_ref, o_ref):
        acc = t_ref[0].astype(F32)
        for s in range(1, n):
            acc = acc + t_ref[s].astype(F32)
        o_ref[...] = acc

    return pl.pallas_call(
        body, name=name, grid=(R // tile,), in_specs=[pl.BlockSpec((n, tile, C), lambda i: (0, i, 0))],
        out_specs=pl.BlockSpec((tile, C), lambda i: (i, 0)), out_shape=jax.ShapeDtypeStruct((R, C), F32),
        compiler_params=_cp("parallel"),
    )(t)


PACK_ROW_MULT = 512


def _pad_rows(a, mult=16):
    r = (-a.shape[0]) % mult
    return jnp.pad(a, ((0, r), (0, 0))) if r else a


BIG = [
    ("ab_w_in", True, 2, 642), ("c_w_qkv", True, 2, 768), ("xa_w_kv", True, 4, 512),
    ("f_w_gate", True, 4, 704), ("f_w_up", True, 4, 704),
    ("ab_w_out", False, 2, 256), ("c_w_out", False, 2, 256), ("xa_w_q", False, 4, 256),
    ("xa_w_out", False, 4, 256), ("f_w_down", False, 4, 704), ("s5_w_glu", False, 2, 64),
]


GROUPS = ("B", "A")
GROUP_ROW_MULT = 64


def group_spec(layer, grp):
    i = layer // 2
    if grp == "A":
        return [("xa_w_kv", layer, True, 512), ("xa_w_q", layer, False, 256), ("xa_w_out", layer, False, 256),
                ("f_w_gate", layer, True, 704), ("f_w_up", layer, True, 704), ("f_w_down", layer, False, 704)]
    if layer % 2 == 0:
        return [("ab_w_in", i, True, 642), ("ab_w_out", i, False, 256), ("s5_w_glu", i, False, 64)]
    return [("c_w_qkv", i, True, 768), ("c_w_out", i, False, 256)]


def _seg_rows(rows):
    return rows + ((-rows) % 16)


def _f32_rows(a):
    bits = lax.bitcast_convert_type(a.reshape(-1), BF16).reshape(-1)
    return jnp.pad(bits, (0, 16 * D - bits.shape[0])).reshape(16, D)


def pack_group_shards(p, layer, grp):
    segs = []
    for name, idx, transposed, rows in group_spec(layer, grp):
        w = p[name][idx]
        if transposed:
            w = w.T
        segs.append(_pad_rows(w.astype(BF16).reshape(-1, D)))
    if grp == "B":
        small = p["gdn_conv_w"] if layer % 2 == 0 else p["c_norm_g"]
        segs.append(_f32_rows(small[layer // 2]))
    return _pad_rows(jnp.concatenate(segs, axis=0), GROUP_ROW_MULT)


def unpack_group_gathered(g, layer, grp):
    out, off = {}, 0
    for name, idx, transposed, rows in group_spec(layer, grp):
        seg = g[:, off:off + rows]
        if name == "s5_w_glu":
            out[name] = seg.reshape(N_CHIPS * 128, 512)
        else:
            out[name] = seg.reshape(N_CHIPS * rows, D)
        off += _seg_rows(rows)
    if grp == "B":
        n = 4 * 384 if layer % 2 == 0 else 256
        bits = g[:, off:off + 16].reshape(N_CHIPS, -1)[:, :2 * n].reshape(N_CHIPS, n, 2)
        small = lax.bitcast_convert_type(bits, F32)
        if layer % 2 == 0:
            out["gdn_conv_w"] = jnp.swapaxes(small.reshape(N_CHIPS, 4, 384), 0, 1).reshape(4, 1536)
        else:
            out["c_norm_g"] = small.reshape(D)
    return out


def pack_group_grads(gr, layer, grp):
    segs = []
    for name, idx, transposed, rows in group_spec(layer, grp):
        w = gr[name].astype(BF16)
        seg = w.reshape(N_CHIPS, rows, D)
        r = (-rows) % 16
        if r:
            seg = jnp.pad(seg, ((0, 0), (0, r), (0, 0)))
        segs.append(seg)
    out = jnp.concatenate(segs, axis=1)
    return jnp.pad(out, ((0, 0), (0, (-out.shape[1]) % GROUP_ROW_MULT), (0, 0)))


def unpack_group_reduced(g, layer, grp):
    out, off = {}, 0
    for name, idx, transposed, rows in group_spec(layer, grp):
        seg = g[off:off + rows]
        if name == "s5_w_glu":
            out[name] = seg.reshape(128, 512)
        else:
            out[name] = seg.T if (transposed and name not in ADAM_TRANSPOSED) else seg
        off += _seg_rows(rows)
    return out


ADAM_TRANSPOSED = ("f_w_gate", "f_w_up")


SMALL = ["ab_norm_g", "s5_a_re", "s5_a_im", "s5_log_dt", "s5_b_re", "s5_b_im", "s5_c_re", "s5_c_im", "s5_d",
         "s5_b_glu", "gdn_conv_w", "gdn_a_log", "gdn_dt_bias", "gdn_out_norm_g", "c_norm_g", "c_q_norm_g",
         "c_k_norm_g", "c_rel_bias", "mem_norm_g", "xa_norm_g", "xa_q_norm_g", "xa_k_norm_g", "f_norm_g"]


def _lane_rows(a):
    flat = a.reshape(-1).astype(F32)
    return jnp.pad(flat, (0, (-flat.shape[0]) % 1024)).reshape(-1, 128)


def pack_small(d, extra=None):
    parts = [_lane_rows(d[n]) for n in SMALL]
    if extra is not None:
        parts.append(_lane_rows(extra))
    rows = jnp.concatenate(parts, axis=0)
    return jnp.pad(rows, ((0, (-rows.shape[0]) % 128), (0, 0)))


def unpack_small(rows, shapes):
    out, off = {}, 0
    for n in SMALL:
        sz = int(np.prod(shapes[n]))
        k = 8 * -(-sz // 1024)
        out[n] = rows[off:off + k].reshape(-1)[:sz].reshape(shapes[n])
        off += k
    return out, rows[off:]


def _s5_blockdiag_b(b):
    bt = jnp.swapaxes(b, 1, 2).reshape(S5_GB, 8, S5_C, S5_P)
    eye = jnp.eye(8, dtype=b.dtype)
    return jnp.einsum("bgcp,gh->bgchp", bt, eye).reshape(S5_GB, 8 * S5_C, 8 * S5_P)


def _s5_blockdiag_c(c):
    ct = jnp.swapaxes(c, 1, 2).reshape(S5_GB, 8, S5_P, S5_C)
    eye = jnp.eye(8, dtype=c.dtype)
    return jnp.einsum("bgpc,gh->bgphc", ct, eye).reshape(S5_GB, 8 * S5_P, 8 * S5_C)


def _s5_diag_b(db):
    t = db.reshape(S5_GB, 8, S5_C, 8, S5_P)
    t = jnp.transpose(t, (0, 2, 4, 1, 3)).reshape(S5_GB, S5_C, S5_P, 64)
    d = t[..., ::9]
    return jnp.transpose(d, (0, 3, 2, 1)).reshape(S5_G, S5_P, S5_C)


def _s5_diag_c(dc):
    t = dc.reshape(S5_GB, 8, S5_P, 8, S5_C)
    t = jnp.transpose(t, (0, 2, 4, 1, 3)).reshape(S5_GB, S5_P, S5_C, 64)
    d = t[..., ::9]
    return jnp.transpose(d, (0, 3, 2, 1)).reshape(S5_G, S5_C, S5_P)


def _heads(t):
    return jnp.swapaxes(t.reshape(t.shape[0], CA_H, CA_D), 0, 1)


def _unheads(t):
    return jnp.swapaxes(t, 0, 1).reshape(t.shape[1], D)


def local_step(x, mem, target, p, wsrc, gsink):
    S = x.shape[0]
    row2 = lambda a: a.reshape(1, -1)
    saved = []
    (mem_n,) = row_fwd("mem_norm", f_norm, [mem], [row2(p["mem_norm_g"])], [(D, BF16)])
    gs = {}

    for layer in range(DEPTH):
        i = layer // 2
        w = wsrc(layer, "B", x)
        sv = {"x0": x, "wB": w}
        if layer % 2 == 0:
            (h,) = row_fwd("norm", f_norm, [x], [row2(p["ab_norm_g"][i])], [(D, BF16)])
            w_in = w["ab_w_in"]
            pm = matmul("nt", h, w_in[:2560], "proj_main")
            pab = matmul("nt", h, w_in[2560:], "proj_ab")
            s5p = dict(
                are=p["s5_a_re"][i].reshape(1, -1), aim=p["s5_a_im"][i].reshape(1, -1),
                ldt=jnp.broadcast_to(p["s5_log_dt"][i][:, None], (S5_G, S5_P)).reshape(1, -1),
                b_re=_s5_blockdiag_b(p["s5_b_re"][i]), b_im=_s5_blockdiag_b(p["s5_b_im"][i]),
                c_re=_s5_blockdiag_c(p["s5_c_re"][i]), c_im=_s5_blockdiag_c(p["s5_c_im"][i]),
                dv=p["s5_d"][i].reshape(1, -1))
            y5, st5 = s5_fwd(pm, **s5p)
            (a_out,) = row_fwd("glu", f_glu, [y5], [w["s5_w_glu"], row2(p["s5_b_glu"][i])], [(S5_W, F32)])
            conv_w = w["gdn_conv_w"]
            qkvc = conv_fwd(pm, conv_w)
            abt = jnp.swapaxes(pab, 0, 1)[:, :, None]
            alog = p["gdn_a_log"][i].reshape(GDN_H, 1, 1)
            dtb = p["gdn_dt_bias"][i].reshape(GDN_H, 1, 1)
            og = row2(p["gdn_out_norm_g"][i])
            b_out, stg = gdn_fwd(qkvc, pm, abt, alog, dtb, og)
            cat = jnp.concatenate([a_out, b_out], axis=1)
            x = matmul("nn", cat, w["ab_w_out"], "mix_out", add=x)
            sv.update(h=h, pm=pm, s5p=s5p, y5=y5, st5=st5, qkvc=qkvc, abt=abt, alog=alog, dtb=dtb, og=og,
                      stg=stg, cat=cat, conv_w=conv_w)
        else:
            (h,) = row_fwd("norm", f_norm, [x], [row2(w["c_norm_g"])], [(D, BF16)])
            qkv = matmul("nt", h, w["c_w_qkv"], "proj_qkv")
            qg = jnp.tile(row2(p["c_q_norm_g"][i]), (1, CA_H))
            kg = jnp.tile(row2(p["c_k_norm_g"][i]), (1, CA_H))
            (qn,) = row_fwd("headnorm_q", f_headnorm, [(qkv, D, 0)], [qg], [(D, F32)])
            kp, vp = kv_prep(qkv, kg)
            bias = rel_bias_expand(p["c_rel_bias"][i]).reshape(CA_H * CHUNK, CA_BAND)
            o = cattn_fwd(qn, kp, vp, bias)
            x = matmul("nn", o, w["c_w_out"], "mix_out", add=x)
            sv.update(h=h, qkv=qkv, qg=qg, kg=kg, qn=qn, kp=kp, vp=vp, bias=bias, o=o)
        sv["x1"] = x
        w = wsrc(layer, "A", x)
        sv["wA"] = w
        (hq,) = row_fwd("norm", f_norm, [x], [row2(p["xa_norm_g"][layer])], [(D, BF16)])
        qx = matmul("nn", hq, w["xa_w_q"], "xa_q")
        kv = matmul("nt", mem_n, w["xa_w_kv"], "xa_kv")
        xqg, xkg = row2(p["xa_q_norm_g"][layer]), row2(p["xa_k_norm_g"][layer])
        (ox,) = row_fwd("xattn", f_xattn, [qx], [kv, xqg, xkg], [(D, BF16)])
        x = matmul("nn", ox, w["xa_w_out"], "xa_out", add=x)
        sv.update(hq=hq, qx=qx, kv=kv, ox=ox)
        sv["x2"] = x
        (hf,) = row_fwd("norm", f_norm, [x], [row2(p["f_norm_g"][layer])], [(D, BF16)])
        gate = matmul("nt", hf, w["f_w_gate"], "ffn_gate")
        up = matmul("nt", hf, w["f_w_up"], "ffn_up")
        (act,) = row_fwd("swiglu", f_swiglu, [gate, up], [], [(FFN, BF16)])
        x = matmul("nn", act, w["f_w_down"], "ffn_down", add=x)
        sv.update(hf=hf, gate=gate, up=up, act=act)
        saved.append(sv)

    dx, loss_vec = loss_head(x, target)

    dmem_n = None
    for layer in reversed(range(DEPTH)):
        i = layer // 2
        sv = saved[layer]
        w, gw = sv["wA"], {}
        dact = matmul("nt", dx, w["f_w_down"], "ffn_dact")
        gw["f_w_down"] = matmul("tn", sv["act"], dx, "ffn_dwd", out_dtype=BF16)
        dgate, dup = row_bwd("swiglu_bwd", f_swiglu, [sv["gate"], sv["up"]], [], [dact], [0, 1], [],
                             row_dtypes=[BF16, BF16])
        gw["f_w_gate"] = matmul("tn", dgate, sv["hf"], "ffn_dwg", out_dtype=BF16)
        gw["f_w_up"] = matmul("tn", dup, sv["hf"], "ffn_dwu", out_dtype=BF16)
        dh = matmul("nn", dgate, w["f_w_gate"], "ffn_dhg")
        dh = matmul("nn", dup, w["f_w_up"], "ffn_dhu", add=dh)
        dx, dg = row_bwd("norm_bwd", f_norm_res, [sv["x2"]], [row2(p["f_norm_g"][layer])], [dh, dx], [0], [0])
        gs.setdefault("f_norm_g", [None] * DEPTH)[layer] = dg[0]
        do = matmul("nt", dx, w["xa_w_out"], "xa_do")
        gw["xa_w_out"] = matmul("tn", sv["ox"], dx, "xa_dwo", out_dtype=BF16)
        xqg, xkg = row2(p["xa_q_norm_g"][layer]), row2(p["xa_k_norm_g"][layer])
        dqx, dkv, dqg, dkg = row_bwd("xattn_bwd", f_xattn, [sv["qx"]], [sv["kv"], xqg, xkg], [do],
                                     [0], [0, 1, 2])
        gs.setdefault("xa_q_norm_g", [None] * DEPTH)[layer] = dqg[0]
        gs.setdefault("xa_k_norm_g", [None] * DEPTH)[layer] = dkg[0]
        gw["xa_w_q"] = matmul("tn", sv["hq"], dqx, "xa_dwq", out_dtype=BF16)
        dhq = matmul("nt", dqx, w["xa_w_q"], "xa_dhq")
        gw["xa_w_kv"] = matmul("tn", dkv, mem_n, "xa_dwkv", out_dtype=BF16)
        dmem_n = matmul("nn", dkv, w["xa_w_kv"], "xa_dmem", add=dmem_n)
        dx, dg = row_bwd("norm_bwd", f_norm_res, [sv["x1"]], [row2(p["xa_norm_g"][layer])], [dhq, dx], [0], [0])
        gs.setdefault("xa_norm_g", [None] * DEPTH)[layer] = dg[0]
        dx = gsink(layer, "A", gw, dx)
        w, gw = sv["wB"], {}
        if layer % 2 == 0:
            dcat = matmul("nt", dx, w["ab_w_out"], "mix_dcat")
            gw["ab_w_out"] = matmul("tn", sv["cat"], dx, "mix_dwo", out_dtype=BF16)
            dy5, dwglu, dbglu = row_bwd("glu_bwd", f_glu, [sv["y5"]], [w["s5_w_glu"], row2(p["s5_b_glu"][i])],
                                        [(dcat, S5_W, 0)], [0], [0, 1])
            gw["s5_w_glu"] = dwglu
            gs.setdefault("s5_b_glu", [None] * 2)[i] = dbglu[0]
            s5p = sv["s5p"]
            du, dare, daim, dldt, dbre, dbim, dcre, dcim, ddv = s5_bwd(sv["pm"], sv["st5"], dy5, **s5p)
            (dldt_g,) = whole("s5_dt_sum", lambda t: (jnp.sum(t, axis=1, keepdims=True),),
                              [dldt.reshape(S5_G, S5_P)], [((S5_G, 1), F32)])
            for nme, val in (("s5_a_re", dare.reshape(S5_G, S5_P)), ("s5_a_im", daim.reshape(S5_G, S5_P)),
                             ("s5_log_dt", dldt_g[:, 0]), ("s5_b_re", _s5_diag_b(dbre)),
                             ("s5_b_im", _s5_diag_b(dbim)), ("s5_c_re", _s5_diag_c(dcre)),
                             ("s5_c_im", _s5_diag_c(dcim)), ("s5_d", ddv.reshape(S5_G, S5_C))):
                gs.setdefault(nme, [None] * 2)[i] = val
            dq, dk, dv, dgate, dal, dbl, dalog, ddtb, dog = gdn_bwd(
                sv["qkvc"], sv["pm"], sv["abt"], sv["alog"], sv["dtb"], sv["og"], sv["stg"], dcat[:, S5_W:])
            (dog_s,) = whole("gdn_og_sum", lambda t: (jnp.sum(t, axis=0, keepdims=True),),
                             [dog.reshape(GDN_H, GDN_D)], [((1, GDN_D), F32)])
            gs.setdefault("gdn_out_norm_g", [None] * 2)[i] = dog_s[0]
            gs.setdefault("gdn_a_log", [None] * 2)[i] = dalog.reshape(GDN_H)
            gs.setdefault("gdn_dt_bias", [None] * 2)[i] = ddtb.reshape(GDN_H)
            dqkvc = jnp.concatenate([dq, dk, dv], axis=1)
            dqkv, dconv = conv_bwd(sv["pm"], sv["conv_w"], dqkvc)
            gs.setdefault("gdn_conv_w", [None] * 2)[i] = dconv
            dpm = jnp.concatenate([du, dqkv, dgate], axis=1).astype(BF16)
            dpab = jnp.swapaxes(jnp.concatenate([dal, dbl], axis=0)[:, :, 0], 0, 1)
            dw_main = matmul("tn", dpm, sv["h"], "proj_dw", out_dtype=BF16)
            dw_ab = matmul("tn", dpab, sv["h"], "proj_ab_dw", out_dtype=BF16)
            gw["ab_w_in"] = jnp.concatenate([dw_main, dw_ab], axis=0)
            w_in = w["ab_w_in"]
            dh = matmul("nn", dpm, w_in[:2560], "proj_dh")
            dh = matmul("nn", dpab, w_in[2560:], "proj_ab_dh", add=dh)
            dx, dg = row_bwd("norm_bwd", f_norm_res, [sv["x0"]], [row2(p["ab_norm_g"][i])], [dh, dx], [0], [0])
            gs.setdefault("ab_norm_g", [None] * 2)[i] = dg[0]
        else:
            do = matmul("nt", dx, w["c_w_out"], "mix_dcat")
            gw["c_w_out"] = matmul("tn", sv["o"], dx, "mix_dwo", out_dtype=BF16)
            dqn, dkp, dvp, dbias = cattn_bwd(sv["qn"], sv["kp"], sv["vp"], sv["bias"], do)
            gs.setdefault("c_rel_bias", [None] * 2)[i] = rel_bias_grad(dbias.reshape(CA_H, CHUNK, CA_BAND))
            dq, dqg = row_bwd("headnorm_bwd", f_headnorm, [(sv["qkv"], D, 0)], [sv["qg"]], [dqn], [0], [0])
            dk, dkg = row_bwd("headnorm_bwd", f_headnorm, [(sv["qkv"], D, 1)], [sv["kg"]],
                              [(dkp, D, 0, CA_PAD)], [0], [0])
            head_sum = lambda t: (jnp.sum(t, axis=0, keepdims=True),)
            (dqg,) = whole("headgain_sum", head_sum, [dqg.reshape(CA_H, CA_D)], [((1, CA_D), F32)])
            (dkg,) = whole("headgain_sum", head_sum, [dkg.reshape(CA_H, CA_D)], [((1, CA_D), F32)])
            gs.setdefault("c_q_norm_g", [None] * 2)[i] = dqg[0]
            gs.setdefault("c_k_norm_g", [None] * 2)[i] = dkg[0]
            dqkv = jnp.concatenate([dq, dk, dvp[CA_PAD:]], axis=1).astype(BF16)
            gw["c_w_qkv"] = matmul("tn", dqkv, sv["h"], "proj_qkv_dw", out_dtype=BF16)
            dh = matmul("nn", dqkv, w["c_w_qkv"], "proj_qkv_dh")
            dx, dg = row_bwd("norm_bwd", f_norm_res, [sv["x0"]], [row2(w["c_norm_g"])], [dh, dx], [0], [0])
            gs.setdefault("c_norm_g", [None] * 2)[i] = dg[0]
        dx = gsink(layer, "B", gw, dx)
    (dmg,) = row_bwd("mem_norm_bwd", f_norm, [mem], [row2(p["mem_norm_g"])], [dmem_n], [], [0])
    small = {n: jnp.stack(v) for n, v in gs.items()}
    small["mem_norm_g"] = dmg[0]
    return loss_vec, dx, small


ADAM_BLOCK_BYTES = 3 << 19


def adam(w, g, m, v, name):
    shape = w.shape
    if w.ndim == 3:
        d0, n, d2 = shape
        fits = [t for t in range(8, n + 1, 8) if n % t == 0 and d0 * t * d2 * 4 <= ADAM_BLOCK_BYTES]
        return tuple(row_fwd(name, f_adam, [w, g, m, v], [], [((d0, d2), F32)] * 3, tile=max(fits)))
    cols = shape[-1]
    w2, g2, m2, v2 = (t.reshape(-1, cols) for t in (w, g, m, v))
    rows = w2.shape[0]
    tile = rows if rows <= 512 else _tile(rows, 512, 8)
    outs = row_fwd(name, f_adam, [w2, g2, m2, v2], [], [(cols, F32)] * 3, tile=tile)
    return tuple(o.reshape(shape) for o in outs)


WEIGHTS = ['ab_norm_g', 'ab_w_in', 'ab_w_out', 's5_a_re', 's5_a_im', 's5_log_dt', 's5_b_re', 's5_b_im', 's5_c_re',
           's5_c_im', 's5_d', 's5_w_glu', 's5_b_glu', 'gdn_conv_w', 'gdn_a_log', 'gdn_dt_bias', 'gdn_out_norm_g',
           'c_norm_g', 'c_w_qkv', 'c_w_out', 'c_q_norm_g', 'c_k_norm_g', 'c_rel_bias', 'mem_norm_g', 'xa_norm_g',
           'xa_w_q', 'xa_w_kv', 'xa_w_out', 'xa_q_norm_g', 'xa_k_norm_g', 'f_norm_g', 'f_w_gate', 'f_w_up',
           'f_w_down']
SHARDED_SMALL = {"gdn_conv_w": (2, 384), "c_norm_g": (1, 256)}


def kernel(x, mem, ab_norm_g, ab_w_in, ab_w_out, s5_a_re, s5_a_im, s5_log_dt, s5_b_re, s5_b_im, s5_c_re, s5_c_im, s5_d, s5_w_glu, s5_b_glu, gdn_conv_w, gdn_a_log, gdn_dt_bias, gdn_out_norm_g, c_norm_g, c_w_qkv, c_w_out, c_q_norm_g, c_k_norm_g, c_rel_bias, mem_norm_g, xa_norm_g, xa_w_q, xa_w_kv, xa_w_out, xa_q_norm_g, xa_k_norm_g, f_norm_g, f_w_gate, f_w_up, f_w_down, loss_target, m_ab_norm_g, m_ab_w_in, m_ab_w_out, m_s5_a_re, m_s5_a_im, m_s5_log_dt, m_s5_b_re, m_s5_b_im, m_s5_c_re, m_s5_c_im, m_s5_d, m_s5_w_glu, m_s5_b_glu, m_gdn_conv_w, m_gdn_a_log, m_gdn_dt_bias, m_gdn_out_norm_g, m_c_norm_g, m_c_w_qkv, m_c_w_out, m_c_q_norm_g, m_c_k_norm_g, m_c_rel_bias, m_mem_norm_g, m_xa_norm_g, m_xa_w_q, m_xa_w_kv, m_xa_w_out, m_xa_q_norm_g, m_xa_k_norm_g, m_f_norm_g, m_f_w_gate, m_f_w_up, m_f_w_down, v_ab_norm_g, v_ab_w_in, v_ab_w_out, v_s5_a_re, v_s5_a_im, v_s5_log_dt, v_s5_b_re, v_s5_b_im, v_s5_c_re, v_s5_c_im, v_s5_d, v_s5_w_glu, v_s5_b_glu, v_gdn_conv_w, v_gdn_a_log, v_gdn_dt_bias, v_gdn_out_norm_g, v_c_norm_g, v_c_w_qkv, v_c_w_out, v_c_q_norm_g, v_c_k_norm_g, v_c_rel_bias, v_mem_norm_g, v_xa_norm_g, v_xa_w_q, v_xa_w_kv, v_xa_w_out, v_xa_q_norm_g, v_xa_k_norm_g, v_f_norm_g, v_f_w_gate, v_f_w_up, v_f_w_down):
    args = locals()
    p = {n: args[n] for n in WEIGHTS}
    m = {n: args["m_" + n] for n in WEIGHTS}
    v = {n: args["v_" + n] for n in WEIGHTS}
    chip = 2 * lax.axis_index("x") + lax.axis_index("y")

    carry = x[0]
    gathers = {}
    for layer in range(DEPTH):
        for grp in GROUPS:
            src = pack_group_shards(p, layer, grp)
            land = lax.dynamic_update_slice(lax.empty((N_CHIPS,) + src.shape, BF16), src[None], (chip, 0, 0))
            send_sems, recv_sems, src, land, carry = exchange_start(
                src, land, carry, f"gather_start_{layer}{grp}", per_dest=False)
            gathers[layer, grp] = (src, land, send_sems, recv_sems)

    def wsrc(layer, grp, after):
        src, land, send_sems, recv_sems = gathers[layer, grp]
        _, land = exchange_wait(src, land, send_sems, recv_sems, after, f"gather_wait_{layer}{grp}", per_dest=False)
        return unpack_group_gathered(land, layer, grp)

    scatters, siblings = [], []
    LAG = 2

    def finish(carry):
        layer, grp, src, land, send_sems, recv_sems = scatters[len(siblings)]
        src, land = exchange_wait(src, land, send_sems, recv_sems, carry, f"scatter_wait_{layer}{grp}", per_dest=True)
        own = lax.dynamic_index_in_dim(src, chip, axis=0, keepdims=False)
        part = sum_own_slabs(own, land, "sum_chips")
        send_sems, recv_sems, part, other, carry = exchange_start(
            part, lax.empty(part.shape, F32), carry, f"sibling_start_{layer}{grp}", per_dest="sibling")
        siblings.append((layer, grp, part, other, send_sems, recv_sems))
        return carry

    def gsink(layer, grp, gw, carry):
        src = pack_group_grads(gw, layer, grp)
        land = lax.empty((3,) + src.shape[1:], BF16)
        send_sems, recv_sems, src, land, carry = exchange_start(
            src, land, carry, f"scatter_start_{layer}{grp}", per_dest=True)
        scatters.append((layer, grp, src, land, send_sems, recv_sems))
        if len(scatters) > LAG:
            carry = finish(carry)
        return carry

    loss_vec, dx, g_small = local_step(carry, mem[0], loss_target[0], p, wsrc, gsink)
    while len(siblings) < len(scatters):
        dx = finish(dx)

    per_layer = {}
    for layer, grp, part, other, send_sems, recv_sems in siblings:
        part, other = exchange_wait(part, other, send_sems, recv_sems, dx, f"sibling_wait_{layer}{grp}",
                                    per_dest="sibling")
        (total,) = row_fwd("sum_cores", lambda a, b: (a + b,), [part, other], [], [(D, F32)],
                           tile=_tile(part.shape[0], 512, 16))
        for name, g in unpack_group_reduced(total, layer, grp).items():
            per_layer.setdefault(name, {})[layer] = g
    grads = {name: jnp.stack([d[k] for k in sorted(d)]) for name, d in per_layer.items()}

    full_shapes = {n: ((2, 4, 1536) if n == "gdn_conv_w" else (2, D) if n == "c_norm_g" else p[n].shape)
                   for n in SMALL}
    small_all = all_exchange(pack_small(g_small, extra=loss_vec), "gather_small")
    small_sum = sum_slabs(small_all, "sum_small")
    g_s, rest = unpack_small(small_sum, full_shapes)
    (loss11,) = whole("loss_sum", lambda t: (jnp.sum(jnp.sum(t, axis=1, keepdims=True), axis=0, keepdims=True),),
                      [rest[:D // 128]], [((1, 1), F32)])
    for n, (axis, width) in SHARDED_SMALL.items():
        g_s[n] = lax.dynamic_slice_in_dim(g_s[n], chip * width, width, axis=axis)
    grads.update(g_s)

    shard_shapes = {n: p[n].shape for n in SMALL}
    packs = [pack_small(d) for d in (p, grads, m, v)]
    d_s, m_s, v_s = adam(*packs, name="adam_small")
    delta, new_m, new_v = {}, {}, {}
    for dst, src in ((delta, d_s), (new_m, m_s), (new_v, v_s)):
        dst.update(unpack_small(src, shard_shapes)[0])
    for name, _, _, _ in BIG:
        if name in ADAM_TRANSPOSED:
            t = lambda a: jnp.swapaxes(a, 1, 2)
            outs = adam(t(p[name]), grads[name], t(m[name]), t(v[name]), "adam_" + name)
            delta[name], new_m[name], new_v[name] = (t(o) for o in outs)
            grads[name] = t(grads[name])
        else:
            delta[name], new_m[name], new_v[name] = adam(p[name], grads[name], m[name], v[name], "adam_" + name)

    return (loss11[0, 0], dx[None], *[grads[n] for n in WEIGHTS], *[delta[n] for n in WEIGHTS],
            *[new_m[n] for n in WEIGHTS], *[new_v[n] for n in WEIGHTS])
```

```python
import functools
import math

import numpy as np
import jax
import jax.numpy as jnp
from jax import lax
from jax.experimental import pallas as pl
from jax.experimental.pallas import tpu as pltpu

F32 = jnp.float32
BF16 = jnp.bfloat16
MESH = pl.DeviceIdType.MESH

D = 1024
CHUNK = 64
N_MEM = 256
EPS = 1e-6
S5_W = 512
S5_G = 32
S5_C = 16
S5_P = 64
S5_GB = 4
GDN_H = 4
GDN_D = 128
CA_H = 16
CA_D = 64
CA_LEFT = 8
CA_BAND = (CA_LEFT + 1) * CHUNK
CA_PAD = CA_LEFT * CHUNK
MAX_REL = 128
XA_H = 4
XA_D = 256
FFN = 2816
DEPTH = 4
N_CHIPS = 4
N_DEV = 8
LR, B1, B2, AEPS, WD, STEP = 0.001, 0.9, 0.999, 1e-08, 0.01, 10

VMEM_LIMIT = 56 * 1024 * 1024
ROW_TILE = 256
HI = lax.Precision.HIGHEST


def _cp(*sem):
    return pltpu.CompilerParams(dimension_semantics=sem, vmem_limit_bytes=VMEM_LIMIT)


def _dg(a, b, ca, cb):
    return lax.dot_general(a.astype(BF16), b.astype(BF16), (((ca,), (cb,)), ((), ())),
                           preferred_element_type=F32)


@jax.custom_vjp
def mm(a, b):
    return _dg(a, b, 1, 0)


@jax.custom_vjp
def mm_nt(a, b):
    return _dg(a, b, 1, 1)


@jax.custom_vjp
def mm_tn(a, b):
    return _dg(a, b, 0, 0)


mm.defvjp(lambda a, b: (mm(a, b), (a, b)), lambda r, g: (mm_nt(g, r[1]), mm_tn(r[0], g)))
mm_nt.defvjp(lambda a, b: (mm_nt(a, b), (a, b)), lambda r, g: (mm(g, r[1]), mm_tn(g, r[0])))
mm_tn.defvjp(lambda a, b: (mm_tn(a, b), (a, b)), lambda r, g: (mm_nt(r[1], g), mm(r[0], g)))


def _bdg(a, b, ca, cb):
    return lax.dot_general(a.astype(BF16), b.astype(BF16), (((ca,), (cb,)), ((0,), (0,))),
                           preferred_element_type=F32)


@jax.custom_vjp
def bmm(a, b):
    return _bdg(a, b, 2, 1)


@jax.custom_vjp
def bmm_nt(a, b):
    return _bdg(a, b, 2, 2)


@jax.custom_vjp
def bmm_tn(a, b):
    return _bdg(a, b, 1, 1)


bmm.defvjp(lambda a, b: (bmm(a, b), (a, b)), lambda r, g: (bmm_nt(g, r[1]), bmm_tn(r[0], g)))
bmm_nt.defvjp(lambda a, b: (bmm_nt(a, b), (a, b)), lambda r, g: (bmm(g, r[1]), bmm_tn(g, r[0])))
bmm_tn.defvjp(lambda a, b: (bmm_tn(a, b), (a, b)), lambda r, g: (bmm_nt(r[1], g), bmm(r[0], g)))


def _split(a):
    hi = a.astype(BF16)
    return hi, (a - hi.astype(F32)).astype(BF16)


def _dg3(a, b, ca, cb):
    (ah, al), (bh, bl) = _split(a), _split(b)
    d = lambda u, v: lax.dot_general(u, v, (((ca,), (cb,)), ((), ())), preferred_element_type=F32)
    return d(ah, bh) + (d(ah, bl) + d(al, bh))


@jax.custom_vjp
def mm3(a, b):
    return _dg3(a, b, 1, 0)


@jax.custom_vjp
def mm3_nt(a, b):
    return _dg3(a, b, 1, 1)


@jax.custom_vjp
def mm3_tn(a, b):
    return _dg3(a, b, 0, 0)


mm3.defvjp(lambda a, b: (mm3(a, b), (a, b)), lambda r, g: (mm3_nt(g, r[1]), mm3_tn(r[0], g)))
mm3_nt.defvjp(lambda a, b: (mm3_nt(a, b), (a, b)), lambda r, g: (mm3(g, r[1]), mm3_tn(g, r[0])))
mm3_tn.defvjp(lambda a, b: (mm3_tn(a, b), (a, b)), lambda r, g: (mm3_nt(r[1], g), mm3(r[0], g)))


def _tri_mm(v, upper):
    T = v.shape[0]
    r = lax.broadcasted_iota(jnp.int32, (T, T), 0)
    c = lax.broadcasted_iota(jnp.int32, (T, T), 1)
    m = ((c >= r) if upper else (r >= c)).astype(BF16)
    hi, lo = _split(v)
    d = lambda u: lax.dot_general(m, u, (((1,), (0,)), ((), ())), preferred_element_type=F32)
    return d(hi) + d(lo)


@jax.custom_vjp
def cumsum_rows(v):
    return _tri_mm(v, False)


cumsum_rows.defvjp(lambda v: (_tri_mm(v, False), None), lambda _, g: (_tri_mm(g, True),))


def _rms(x, g):
    return x * lax.rsqrt(jnp.mean(x * x, axis=-1, keepdims=True) + EPS) * g


def _softmax(s):
    e = jnp.exp(s - lax.stop_gradient(jnp.max(s, axis=-1, keepdims=True)))
    return e / jnp.sum(e, axis=-1, keepdims=True)


def _softplus(x):
    return jnp.maximum(x, 0.0) + jnp.log(1.0 + jnp.exp(-jnp.abs(x)))


def _tile(n, cap, align):
    if n <= cap:
        return n
    best = None
    for d in range(align, cap + 1, align):
        if n % d == 0:
            best = d
    assert best is not None, (n, cap, align)
    return best


def matmul(mode, a, b, name, out_dtype=F32, add=None, norm_out=None, norm_bwd=None):
    if mode == "nn":
        (M, K), (K2, N) = a.shape, b.shape
    elif mode == "nt":
        (M, K), (N, K2) = a.shape, b.shape
    else:
        (K, M), (K2, N) = a.shape, b.shape
    assert K == K2, (mode, a.shape, b.shape)
    tm, tn, tk = _tile(M, 512, 128), _tile(N, 1536, 128), _tile(K, 2048, 128)
    nk = K // tk
    if mode == "nn":
        a_spec = pl.BlockSpec((tm, tk), lambda i, j, k: (i, k))
        b_spec = pl.BlockSpec((tk, tn), lambda i, j, k: (k, j))
        dn = (((1,), (0,)), ((), ()))
    elif mode == "nt":
        a_spec = pl.BlockSpec((tm, tk), lambda i, j, k: (i, k))
        b_spec = pl.BlockSpec((tn, tk), lambda i, j, k: (j, k))
        dn = (((1,), (1,)), ((), ()))
    else:
        a_spec = pl.BlockSpec((tk, tm), lambda i, j, k: (k, i))
        b_spec = pl.BlockSpec((tk, tn), lambda i, j, k: (k, j))
        dn = (((0,), (0,)), ((), ()))
    o_spec = pl.BlockSpec((tm, tn), lambda i, j, k: (i, j))
    has_add = add is not None
    g_spec = pl.BlockSpec((1, tn), lambda i, j, k: (0, j))
    extra, extra_specs, out_shapes, out_specs = [], [], [jax.ShapeDtypeStruct((M, N), out_dtype)], [o_spec]
    sem = ("parallel", "parallel", "arbitrary")
    if norm_out is not None:
        assert tn == N
        extra, extra_specs = [norm_out], [g_spec]
        out_shapes.append(jax.ShapeDtypeStruct((M, N), BF16))
        out_specs.append(o_spec)
    if norm_bwd is not None:
        assert tn == N
        x_in, g_in, res_in = norm_bwd
        extra, extra_specs = [x_in, g_in, res_in], [o_spec, g_spec, o_spec]
        out_shapes.append(jax.ShapeDtypeStruct((1, N), F32))
        out_specs.append(g_spec)
        sem = ("arbitrary", "arbitrary", "arbitrary")
    n_in = 2 + int(has_add) + len(extra)
    n_out = len(out_shapes)

    def body(*refs):
        a_ref, b_ref = refs[0], refs[1]
        add_ref = refs[2] if has_add else None
        extra_refs = refs[2 + int(has_add):n_in]
        o_ref = refs[n_in]
        acc_ref = refs[-1]
        i = pl.program_id(0)
        p = lax.dot_general(a_ref[...].astype(BF16), b_ref[...].astype(BF16), dn,
                            preferred_element_type=F32)

        def finish(total):
            if has_add:
                total = total + add_ref[...]
            if norm_out is not None:
                o_ref[...] = total.astype(o_ref.dtype)
                refs[n_in + 1][...] = _rms(total, extra_refs[0][...]).astype(BF16)
            elif norm_bwd is not None:
                x_ref, g_ref, res_ref = extra_refs
                _, vjp = jax.vjp(f_norm_res, x_ref[...], g_ref[...])
                dx, dg = vjp((total, res_ref[...]))
                o_ref[...] = dx
                dg_ref = refs[n_in + 1]

                @pl.when(i == 0)
                def _():
                    dg_ref[...] = dg

                @pl.when(i > 0)
                def _():
                    dg_ref[...] += dg
            else:
                o_ref[...] = total.astype(o_ref.dtype)

        if nk == 1:
            finish(p)
        else:
            k = pl.program_id(2)

            @pl.when(k == 0)
            def _():
                acc_ref[...] = p

            @pl.when(k > 0)
            def _():
                acc_ref[...] += p

            @pl.when(k == nk - 1)
            def _():
                finish(acc_ref[...])

    ins = [a, b] + ([add] if has_add else []) + extra
    specs = [a_spec, b_spec] + ([o_spec] if has_add else []) + extra_specs
    out = pl.pallas_call(
        body, name=name, grid=(M // tm, N // tn, nk), in_specs=specs, out_specs=out_specs,
        out_shape=out_shapes, scratch_shapes=[pltpu.VMEM((tm, tn), F32)],
        compiler_params=_cp(*sem),
    )(*ins)
    return out[0] if n_out == 1 else out


def _row_spec(arr, tile):
    if isinstance(arr, tuple):
        a, w, cb = arr[:3]
        ro = (arr[3] // tile) if len(arr) > 3 else 0
        assert len(arr) < 4 or arr[3] % tile == 0
        return a, pl.BlockSpec((tile, w), lambda i, cb=cb, ro=ro: (i + ro, cb)), (tile, w)
    if arr.ndim == 3:
        d0, _, d2 = arr.shape
        return arr, pl.BlockSpec((d0, tile, d2), lambda i: (0, i, 0)), (d0, tile, d2)
    return arr, pl.BlockSpec((tile, arr.shape[1]), lambda i: (i, 0)), (tile, arr.shape[1])


def _full_spec(arr):
    nd = arr.ndim
    return pl.BlockSpec(arr.shape, lambda i, nd=nd: (0,) * nd)


def _n_rows(arr):
    a = arr[0] if isinstance(arr, tuple) else arr
    return a.shape[1] if a.ndim == 3 else a.shape[0]


def _row_out_shape(shape_tail, n, dtype):
    if isinstance(shape_tail, tuple):
        d0, d2 = shape_tail
        return (jax.ShapeDtypeStruct((d0, n, d2), dtype),
                lambda tile: pl.BlockSpec((d0, tile, d2), lambda i: (0, i, 0)))
    return (jax.ShapeDtypeStruct((n, shape_tail), dtype),
            lambda tile: pl.BlockSpec((tile, shape_tail), lambda i: (i, 0)))


def _f32(v):
    return v.astype(F32) if v.dtype == BF16 else v


def row_fwd(name, fn, rows, fulls, outs, tile=ROW_TILE):
    n = _n_rows(rows[0])
    tile = min(tile, n)
    assert n % tile == 0, (name, n, tile)
    rs = [_row_spec(r, tile) for r in rows]
    os_ = [_row_out_shape(w, n, dt) for w, dt in outs]
    nr, nf = len(rows), len(fulls)

    def body(*refs):
        vals = [_f32(r[...]) for r in refs[:nr + nf]]
        res = fn(*vals)
        for r, v in zip(refs[nr + nf:], res):
            r[...] = v.astype(r.dtype)

    out = pl.pallas_call(
        body, name=name, grid=(n // tile,),
        in_specs=[s for _, s, _ in rs] + [_full_spec(f) for f in fulls],
        out_specs=[mk(tile) for _, mk in os_], out_shape=[sh for sh, _ in os_],
        compiler_params=_cp("parallel"),
    )(*[a for a, _, _ in rs], *fulls)
    return out


def row_bwd(name, fn, rows, fulls, cts, want_rows, want_fulls, row_dtypes=None, tile=ROW_TILE):
    n = _n_rows(rows[0])
    tile = min(tile, n)
    assert n % tile == 0, (name, n, tile)
    rs = [_row_spec(r, tile) for r in rows]
    cs = [_row_spec(c, tile) for c in cts]
    nr, nf, nc = len(rows), len(fulls), len(cts)
    row_dtypes = row_dtypes or [F32] * len(want_rows)
    out_shapes, out_specs = [], []
    for k, idx in enumerate(want_rows):
        a, _, blk = rs[idx]
        if len(blk) == 3:
            sh, mk = _row_out_shape((blk[0], blk[2]), n, row_dtypes[k])
        else:
            sh, mk = _row_out_shape(blk[1], n, row_dtypes[k])
        out_shapes.append(sh)
        out_specs.append(mk(tile))
    for idx in want_fulls:
        out_shapes.append(jax.ShapeDtypeStruct(fulls[idx].shape, F32))
        out_specs.append(_full_spec(fulls[idx]))
    n_wr = len(want_rows)

    def body(*refs):
        i = pl.program_id(0)
        vals = [_f32(r[...]) for r in refs[:nr + nf]]
        ct_vals = [_f32(r[...]) for r in refs[nr + nf:nr + nf + nc]]
        outs = refs[nr + nf + nc:]
        _, vjp = jax.vjp(fn, *vals)
        grads = vjp(tuple(ct_vals))
        for k, idx in enumerate(want_rows):
            outs[k][...] = grads[idx].astype(outs[k].dtype)
        for k, idx in enumerate(want_fulls):
            o = outs[n_wr + k]
            g = grads[nr + idx]

            @pl.when(i == 0)
            def _(o=o, g=g):
                o[...] = g

            @pl.when(i > 0)
            def _(o=o, g=g):
                o[...] += g

    out = pl.pallas_call(
        body, name=name, grid=(n // tile,),
        in_specs=[s for _, s, _ in rs] + [_full_spec(f) for f in fulls] + [s for _, s, _ in cs],
        out_specs=out_specs, out_shape=out_shapes,
        compiler_params=_cp("arbitrary"),
    )(*[a for a, _, _ in rs], *fulls, *[a for a, _, _ in cs])
    return out


def whole(name, fn, args, outs):
    def body(*refs):
        res = fn(*[r[...] for r in refs[:len(args)]])
        for r, v in zip(refs[len(args):], res):
            r[...] = v.astype(r.dtype)

    return pl.pallas_call(
        body, name=name, out_shape=[jax.ShapeDtypeStruct(s, d) for s, d in outs],
        compiler_params=pltpu.CompilerParams(vmem_limit_bytes=VMEM_LIMIT),
    )(*args)


def f_norm(x, g):
    return (_rms(x, g),)


def f_norm_res(x, g):
    return _rms(x, g), x


def f_swiglu(g, u):
    return (g * jax.nn.sigmoid(g) * u,)


def f_glu(y, w, b):
    h = jax.nn.gelu(y)
    return (h * jax.nn.sigmoid(mm(h, w) + b),)


def f_xattn(q, kv, qg, kg):
    outs = []
    for h in range(XA_H):
        sl = slice(h * XA_D, (h + 1) * XA_D)
        qn = _rms(q[:, sl], qg)
        kn = _rms(kv[:, sl], kg)
        vh = kv[:, D + h * XA_D:D + (h + 1) * XA_D]
        p = _softmax(mm_nt(qn, kn) * (XA_D ** -0.5))
        outs.append(mm(p, vh))
    return (jnp.concatenate(outs, axis=-1),)


def f_adam(w, g, m, v):
    m2 = B1 * m + (1.0 - B1) * g
    v2 = B2 * v + (1.0 - B2) * (g * g)
    m_hat = m2 / (1.0 - B1 ** STEP)
    v_hat = v2 / (1.0 - B2 ** STEP)
    delta = -LR * (m_hat / (jnp.sqrt(v_hat) + AEPS) + WD * w)
    return delta, m2, v2


S5_NTAB, S5_NROW = 4, 6


def _s5_tables(are, aim, ldt):
    T = CHUNK
    dt = jnp.exp(ldt)
    ar, ai = are * dt, aim * dt
    t = lax.broadcasted_iota(jnp.int32, (T, 1), 0).astype(F32)
    mag, inv = jnp.exp(t * ar), jnp.exp(-t * ar)
    cs, sn = jnp.cos(t * ai), jnp.sin(t * ai)
    e_re, e_im = mag * cs, mag * sn
    n_re, n_im = inv * cs, -inv * sn
    l_re, l_im = jnp.exp(ar) * jnp.cos(ai), jnp.exp(ar) * jnp.sin(ai)
    den = are * are + aim * aim
    k_re = ((l_re - 1.0) * are + l_im * aim) / den
    k_im = (l_im * are - (l_re - 1.0) * aim) / den
    tl = float(T - 1)
    m_re, m_im = jnp.exp(tl * ar) * jnp.cos(tl * ai), jnp.exp(tl * ar) * jnp.sin(tl * ai)
    return (e_re, e_im, n_re, n_im), (l_re, l_im, k_re, k_im, m_re, m_im)


def _s5_chunk(u, sre, sim, tabs, rows, b_re, b_im, c_re, c_im, dv):
    e_re, e_im, n_re, n_im = tabs
    l_re, l_im, k_re, k_im, m_re, m_im = rows
    x_re, x_im = mm(u, b_re), mm(u, b_im)
    bu_re = k_re * x_re - k_im * x_im
    bu_im = k_re * x_im + k_im * x_re
    v_re = bu_re * n_re - bu_im * n_im
    v_im = bu_re * n_im + bu_im * n_re
    p_re = l_re * sre - l_im * sim
    p_im = l_re * sim + l_im * sre
    w_re = cumsum_rows(v_re) + p_re
    w_im = cumsum_rows(v_im) + p_im
    s_re = e_re * w_re - e_im * w_im
    s_im = e_re * w_im + e_im * w_re
    y = mm(s_re, c_re) - mm(s_im, c_im) + dv * u
    z_re = jnp.sum(v_re, axis=0, keepdims=True) + p_re
    z_im = jnp.sum(v_im, axis=0, keepdims=True) + p_im
    return y, m_re * z_re - m_im * z_im, m_re * z_im + m_im * z_re


def _s5_fill_tables(are_ref, aim_ref, ldt_ref, tab, row):
    for g in range(S5_GB):
        ls = slice(512 * g, 512 * (g + 1))
        tabs, rows = _s5_tables(are_ref[:, ls], aim_ref[:, ls], ldt_ref[:, ls])
        for k, t in enumerate(tabs):
            tab[k, :, ls] = t
        for k, r in enumerate(rows):
            row[k:k + 1, ls] = r


def _s5_read_tables(tab, row, ls):
    return (tuple(tab[k, :, ls] for k in range(S5_NTAB)), tuple(row[k:k + 1, ls] for k in range(S5_NROW)))


def _s5_specs(nc, rev):
    T = CHUNK

    def ci(c):
        return nc - 1 - c if rev else c

    u_spec = pl.BlockSpec((T, S5_W), lambda c: (ci(c), 0))
    p_spec = pl.BlockSpec((1, S5_G * S5_P), lambda c: (0, 0))
    b_spec = pl.BlockSpec((S5_GB, 128, 512), lambda c: (0, 0, 0))
    c_spec = pl.BlockSpec((S5_GB, 512, 128), lambda c: (0, 0, 0))
    d_spec = pl.BlockSpec((1, S5_W), lambda c: (0, 0))
    st_spec = pl.BlockSpec((None, 2, S5_G * S5_P), lambda c: (ci(c), 0, 0))
    return u_spec, p_spec, b_spec, c_spec, d_spec, st_spec


def s5_fwd(pm, are, aim, ldt, b_re, b_im, c_re, c_im, dv):
    S = pm.shape[0]
    nc = S // CHUNK
    u_spec, p_spec, b_spec, c_spec, d_spec, st_spec = _s5_specs(nc, False)

    def body(u_ref, are_ref, aim_ref, ldt_ref, bre_ref, bim_ref, cre_ref, cim_ref, dv_ref,
             y_ref, st_ref, state, tab, row):
        c = pl.program_id(0)

        @pl.when(c == 0)
        def _():
            state[...] = jnp.zeros_like(state)
            _s5_fill_tables(are_ref, aim_ref, ldt_ref, tab, row)

        st_ref[...] = state[...]
        for g in range(S5_GB):
            lu, ls = slice(128 * g, 128 * (g + 1)), slice(512 * g, 512 * (g + 1))
            tabs, rows = _s5_read_tables(tab, row, ls)
            y, e_re, e_im = _s5_chunk(u_ref[:, lu], state[0:1, ls], state[1:2, ls], tabs, rows,
                                      bre_ref[g], bim_ref[g], cre_ref[g], cim_ref[g], dv_ref[:, lu])
            y_ref[:, lu] = y
            state[0:1, ls] = e_re
            state[1:2, ls] = e_im

    n_state = S5_G * S5_P
    return pl.pallas_call(
        body, name="s5_fwd", grid=(nc,),
        in_specs=[u_spec, p_spec, p_spec, p_spec, b_spec, b_spec, c_spec, c_spec, d_spec],
        out_specs=[u_spec, st_spec],
        out_shape=[jax.ShapeDtypeStruct((S, S5_W), F32), jax.ShapeDtypeStruct((nc, 2, n_state), F32)],
        scratch_shapes=[pltpu.VMEM((2, n_state), F32), pltpu.VMEM((S5_NTAB, CHUNK, n_state), F32),
                        pltpu.VMEM((8, n_state), F32)],
        compiler_params=_cp("arbitrary"),
    )(pm, are, aim, ldt, b_re, b_im, c_re, c_im, dv)


def s5_bwd(pm, st, dy, are, aim, ldt, b_re, b_im, c_re, c_im, dv):
    S = pm.shape[0]
    nc = S // CHUNK
    u_spec, p_spec, b_spec, c_spec, d_spec, st_spec = _s5_specs(nc, True)

    def body(u_ref, st_ref, dy_ref, are_ref, aim_ref, ldt_ref, bre_ref, bim_ref, cre_ref, cim_ref, dv_ref,
             du_ref, dare_ref, daim_ref, dldt_ref, dbre_ref, dbim_ref, dcre_ref, dcim_ref, ddv_ref,
             dstate, tab, row, dtab, drow):
        c = pl.program_id(0)

        @pl.when(c == 0)
        def _():
            dstate[...] = jnp.zeros_like(dstate)
            dtab[...] = jnp.zeros_like(dtab)
            drow[...] = jnp.zeros_like(drow)
            _s5_fill_tables(are_ref, aim_ref, ldt_ref, tab, row)

        for g in range(S5_GB):
            lu, ls = slice(128 * g, 128 * (g + 1)), slice(512 * g, 512 * (g + 1))
            every = slice(None)
            tabs, rows = _s5_read_tables(tab, row, ls)
            args = (u_ref[:, lu], st_ref[0:1, ls], st_ref[1:2, ls], tabs, rows,
                    bre_ref[g], bim_ref[g], cre_ref[g], cim_ref[g], dv_ref[:, lu])
            _, vjp = jax.vjp(_s5_chunk, *args)
            gr = vjp((dy_ref[:, lu], dstate[0:1, ls], dstate[1:2, ls]))
            du_ref[:, lu] = gr[0]
            dstate[0:1, ls] = gr[1]
            dstate[1:2, ls] = gr[2]
            for k, t in enumerate(gr[3]):
                dtab[k, :, ls] += t
            for k, r in enumerate(gr[4]):
                drow[k:k + 1, ls] += r
            accs = ((dbre_ref, (g,)), (dbim_ref, (g,)), (dcre_ref, (g,)), (dcim_ref, (g,)), (ddv_ref, (every, lu)))
            for (o, idx), gv in zip(accs, gr[5:]):
                @pl.when(c == 0)
                def _(o=o, idx=idx, gv=gv):
                    o[idx] = gv

                @pl.when(c > 0)
                def _(o=o, idx=idx, gv=gv):
                    o[idx] += gv

        @pl.when(c == nc - 1)
        def _():
            for g in range(S5_GB):
                ls = slice(512 * g, 512 * (g + 1))
                _, vjp = jax.vjp(_s5_tables, are_ref[:, ls], aim_ref[:, ls], ldt_ref[:, ls])
                dtabs, drows = _s5_read_tables(dtab, drow, ls)
                ga, gi, gl = vjp((dtabs, drows))
                dare_ref[:, ls] = ga
                daim_ref[:, ls] = gi
                dldt_ref[:, ls] = gl

    n_state = S5_G * S5_P
    return pl.pallas_call(
        body, name="s5_bwd", grid=(nc,),
        in_specs=[u_spec, st_spec, u_spec, p_spec, p_spec, p_spec, b_spec, b_spec, c_spec, c_spec, d_spec],
        out_specs=[u_spec, p_spec, p_spec, p_spec, b_spec, b_spec, c_spec, c_spec, d_spec],
        out_shape=[jax.ShapeDtypeStruct((S, S5_W), F32)] + [jax.ShapeDtypeStruct((1, n_state), F32)] * 3
        + [jax.ShapeDtypeStruct((S5_GB, 128, 512), F32)] * 2 + [jax.ShapeDtypeStruct((S5_GB, 512, 128), F32)] * 2
        + [jax.ShapeDtypeStruct((1, S5_W), F32)],
        scratch_shapes=[pltpu.VMEM((2, n_state), F32), pltpu.VMEM((S5_NTAB, CHUNK, n_state), F32),
                        pltpu.VMEM((8, n_state), F32), pltpu.VMEM((S5_NTAB, CHUNK, n_state), F32),
                        pltpu.VMEM((8, n_state), F32)],
        compiler_params=_cp("arbitrary"),
    )(pm, st, dy, are, aim, ldt, b_re, b_im, c_re, c_im, dv)


def conv_fwd(pm, w):
    S = pm.shape[0]

    def body(x_ref, w_ref, o_ref, pad):
        x = x_ref[...]
        pad[0:8, :] = jnp.zeros((8, 128), F32)
        pad[8:, :] = x
        y = (w_ref[3:4, :] * x + w_ref[2:3, :] * pad[7:7 + S, :] + w_ref[1:2, :] * pad[6:6 + S, :]
             + w_ref[0:1, :] * pad[5:5 + S, :])
        o_ref[...] = y * jax.nn.sigmoid(y)

    return pl.pallas_call(
        body, name="conv_fwd", grid=(12,),
        in_specs=[pl.BlockSpec((S, 128), lambda j: (0, 4 + j)), pl.BlockSpec((4, 128), lambda j: (0, j))],
        out_specs=pl.BlockSpec((S, 128), lambda j: (0, j)),
        out_shape=jax.ShapeDtypeStruct((S, 1536), F32),
        scratch_shapes=[pltpu.VMEM((S + 8, 128), F32)],
        compiler_params=_cp("parallel"),
    )(pm, w)


def conv_bwd(pm, w, dout):
    S = pm.shape[0]

    def body(x_ref, w_ref, do_ref, dx_ref, dw_ref, pad, dpad):
        x = x_ref[...]
        pad[0:8, :] = jnp.zeros((8, 128), F32)
        pad[8:, :] = x
        xs = [pad[5:5 + S, :], pad[6:6 + S, :], pad[7:7 + S, :], x]
        y = w_ref[0:1, :] * xs[0] + w_ref[1:2, :] * xs[1] + w_ref[2:3, :] * xs[2] + w_ref[3:4, :] * xs[3]
        sg = jax.nn.sigmoid(y)
        dy = do_ref[...] * (sg + y * sg * (1.0 - sg))
        dpad[0:S, :] = dy
        dpad[S:, :] = jnp.zeros((8, 128), F32)
        dx_ref[...] = (w_ref[3:4, :] * dy + w_ref[2:3, :] * dpad[1:1 + S, :] + w_ref[1:2, :] * dpad[2:2 + S, :]
                       + w_ref[0:1, :] * dpad[3:3 + S, :])
        for i in range(4):
            dw_ref[i:i + 1, :] = jnp.sum(dy * xs[i], axis=0, keepdims=True)

    return pl.pallas_call(
        body, name="conv_bwd", grid=(12,),
        in_specs=[pl.BlockSpec((S, 128), lambda j: (0, 4 + j)), pl.BlockSpec((4, 128), lambda j: (0, j)),
                  pl.BlockSpec((S, 128), lambda j: (0, j))],
        out_specs=[pl.BlockSpec((S, 128), lambda j: (0, j)), pl.BlockSpec((4, 128), lambda j: (0, j))],
        out_shape=[jax.ShapeDtypeStruct((S, 1536), F32), jax.ShapeDtypeStruct((4, 1536), F32)],
        scratch_shapes=[pltpu.VMEM((S + 8, 128), F32), pltpu.VMEM((S + 8, 128), F32)],
        compiler_params=_cp("parallel"),
    )(pm, w, dout)


GDN_SUP = 4
GDN_ROWS = GDN_SUP * CHUNK


def _gdn_chunk(q, k, v, gate, al, bl, alog, dtb, og, state):
    R = q.shape[0]
    r = lax.broadcasted_iota(jnp.int32, (R, R), 0)
    c = lax.broadcasted_iota(jnp.int32, (R, R), 1)
    same = (r // CHUNK) == (c // CHUNK)
    eye = (r == c).astype(F32)
    strict, causal, upper = same & (r > c), same & (r >= c), same & (r <= c)
    qn = q * lax.rsqrt(jnp.sum(q * q, axis=-1, keepdims=True) + EPS) * (GDN_D ** -0.5)
    kn = k * lax.rsqrt(jnp.sum(k * k, axis=-1, keepdims=True) + EPS)
    beta = jax.nn.sigmoid(bl)
    g = -jnp.exp(alog) * _softplus(al + dtb)
    g_row = jnp.sum(eye * g, axis=0, keepdims=True)
    gc_col = jnp.sum(jnp.where(causal, g_row, 0.0), axis=1, keepdims=True)
    gc_row = jnp.sum(jnp.where(upper, g, 0.0), axis=0, keepdims=True)
    gtot = jnp.sum(jnp.where(same, g_row, 0.0), axis=1, keepdims=True)
    gamma = jnp.exp(gc_col)
    diff = gc_col - gc_row
    d_strict = jnp.where(strict, jnp.exp(jnp.where(strict, diff, 0.0)), 0.0)
    d_causal = jnp.where(causal, jnp.exp(jnp.where(causal, diff, 0.0)), 0.0)
    a = beta * mm_nt(kn, kn) * d_strict
    p = -a
    x = eye + p
    for _ in range(5):
        p = mm(p, p)
        x = x + mm(x, p)
    u_new = mm(x, beta * v)
    w_k = mm(x, (beta * gamma) * kn)
    qk = mm_nt(qn, kn) * d_causal
    q_g = qn * gamma
    k_tail = kn * jnp.exp(gtot - gc_col)
    ws, os_ = [], []
    for i in range(R // CHUNK):
        rows = slice(CHUNK * i, CHUNK * (i + 1))
        w_i = u_new[rows] - mm(w_k[rows], state)
        os_.append(mm(q_g[rows], state))
        decay = jnp.exp(jnp.sum(g[rows], axis=0, keepdims=True))
        state = decay * state + mm_tn(k_tail[rows], w_i)
        ws.append(w_i)
    o = jnp.concatenate(os_, axis=0) + mm(qk, jnp.concatenate(ws, axis=0))
    out = _rms(o, og) * (gate * jax.nn.sigmoid(gate))
    return out, state


def _gdn_specs(nc, rev):
    def ci(c):
        return nc - 1 - c if rev else c

    def blk(cb):
        return pl.BlockSpec((GDN_ROWS, 512), lambda c: (ci(c), cb))

    col = lambda n: pl.BlockSpec((n, GDN_ROWS, 1), lambda c: (0, ci(c), 0))
    sc = pl.BlockSpec((GDN_H, 1, 1), lambda c: (0, 0, 0))
    og = pl.BlockSpec((1, 128), lambda c: (0, 0))
    st = pl.BlockSpec((GDN_H, None, 128, 128), lambda c: (0, ci(c), 0, 0))
    return blk, col, sc, og, st


def gdn_fwd(qkvc, pm, abt, alog, dtb, og):
    S = qkvc.shape[0]
    nc = S // GDN_ROWS
    blk, col, sc, ogs, st = _gdn_specs(nc, False)

    def body(q_ref, k_ref, v_ref, gate_ref, ab_ref, alog_ref, dtb_ref, og_ref, o_ref, st_ref, state):
        c = pl.program_id(0)

        @pl.when(c == 0)
        def _():
            state[...] = jnp.zeros_like(state)

        st_ref[...] = state[...]
        for h in range(GDN_H):
            sl = slice(GDN_D * h, GDN_D * (h + 1))
            out, new_state = _gdn_chunk(q_ref[:, sl], k_ref[:, sl], v_ref[:, sl], gate_ref[:, sl], ab_ref[h],
                                        ab_ref[GDN_H + h], alog_ref[h], dtb_ref[h], og_ref[...], state[h])
            o_ref[:, sl] = out
            state[h] = new_state

    return pl.pallas_call(
        body, name="gdn_fwd", grid=(nc,),
        in_specs=[blk(0), blk(1), blk(2), blk(4), col(2 * GDN_H), sc, sc, ogs],
        out_specs=[blk(0), st],
        out_shape=[jax.ShapeDtypeStruct((S, 512), F32), jax.ShapeDtypeStruct((GDN_H, nc, 128, 128), F32)],
        scratch_shapes=[pltpu.VMEM((GDN_H, 128, 128), F32)],
        compiler_params=_cp("arbitrary"),
    )(qkvc, qkvc, qkvc, pm, abt, alog, dtb, og)


def gdn_bwd(qkvc, pm, abt, alog, dtb, og, st, dout):
    S = qkvc.shape[0]
    nc = S // GDN_ROWS
    blk, col, sc, ogs, sts = _gdn_specs(nc, True)

    def body(q_ref, k_ref, v_ref, gate_ref, ab_ref, alog_ref, dtb_ref, og_ref, st_ref, do_ref,
             dq_ref, dk_ref, dv_ref, dgate_ref, dal_ref, dbl_ref, dalog_ref, ddtb_ref, dog_ref, dstate):
        c = pl.program_id(0)

        @pl.when(c == 0)
        def _():
            dstate[...] = jnp.zeros_like(dstate)

        for h in range(GDN_H):
            sl = slice(GDN_D * h, GDN_D * (h + 1))
            args = (q_ref[:, sl], k_ref[:, sl], v_ref[:, sl], gate_ref[:, sl], ab_ref[h], ab_ref[GDN_H + h],
                    alog_ref[h], dtb_ref[h], og_ref[...], st_ref[h])
            _, vjp = jax.vjp(_gdn_chunk, *args)
            g = vjp((do_ref[:, sl], dstate[h]))
            for o, gv in zip((dq_ref, dk_ref, dv_ref, dgate_ref), g[:4]):
                o[:, sl] = gv
            dal_ref[h] = g[4]
            dbl_ref[h] = g[5]
            dstate[h] = g[9]
            for o, gv in zip((dalog_ref, ddtb_ref, dog_ref), g[6:9]):
                @pl.when(c == 0)
                def _(o=o, gv=gv, h=h):
                    o[h] = gv

                @pl.when(c > 0)
                def _(o=o, gv=gv, h=h):
                    o[h] += gv

    ogo = pl.BlockSpec((GDN_H, 1, 128), lambda c: (0, 0, 0))
    sd = jax.ShapeDtypeStruct
    return pl.pallas_call(
        body, name="gdn_bwd", grid=(nc,),
        in_specs=[blk(0), blk(1), blk(2), blk(4), col(2 * GDN_H), sc, sc, ogs, sts, blk(0)],
        out_specs=[blk(0), blk(0), blk(0), blk(0), col(GDN_H), col(GDN_H), sc, sc, ogo],
        out_shape=[sd((S, 512), F32)] * 4 + [sd((GDN_H, S, 1), F32)] * 2 + [sd((GDN_H, 1, 1), F32)] * 2
        + [sd((GDN_H, 1, 128), F32)],
        scratch_shapes=[pltpu.VMEM((GDN_H, 128, 128), F32)],
        compiler_params=_cp("arbitrary"),
    )(qkvc, qkvc, qkvc, pm, abt, alog, dtb, og, st, dout)


HG = 8
HG_LANES = HG * CA_D


def _group_mean_raw(y):
    r = lax.broadcasted_iota(jnp.int32, (128, 128), 0)
    c = lax.broadcasted_iota(jnp.int32, (128, 128), 1)
    g = jnp.where((r // CA_D) == (c // CA_D), 1.0 / CA_D, 0.0).astype(BF16)
    d = lambda u: lax.dot_general(u, g, (((1,), (0,)), ((), ())), preferred_element_type=F32)
    outs = []
    for j in range(y.shape[1] // 128):
        hi, lo = _split(y[:, 128 * j:128 * (j + 1)])
        outs.append(d(hi) + d(lo))
    return jnp.concatenate(outs, axis=1)


@jax.custom_vjp
def group_mean(y):
    return _group_mean_raw(y)


group_mean.defvjp(lambda y: (_group_mean_raw(y), None), lambda _, g: (_group_mean_raw(g),))


def f_headnorm(t, g):
    return (t * lax.rsqrt(group_mean(t * t) + EPS) * g,)


def _cattn_chunk(q, kb, vb, bias, valid):
    lane = lax.broadcasted_iota(jnp.int32, (1, 128), 1)
    m0 = (lane < CA_D).astype(F32)
    m1 = 1.0 - m0
    pairs = range(HG // 2)
    sl = [slice(128 * p, 128 * (p + 1)) for p in pairs]
    q2 = [jnp.concatenate([q[:, s] * m0, q[:, s] * m1], axis=0) for s in sl]
    sc = [mm_nt(q2[p], kb[:, sl[p]]) * (CA_D ** -0.5) + bias[sl[p]] for p in pairs]
    pr = [_softmax(jnp.where(valid, s, -1e30)) for s in sc]
    o2 = [mm(pr[p], vb[:, sl[p]]) for p in pairs]
    return jnp.concatenate([o[:CHUNK] * m0 + o[CHUNK:] * m1 for o in o2], axis=1)


def _cattn_valid(c):
    pos = lax.broadcasted_iota(jnp.int32, (1, CA_BAND), 1) + c * CHUNK
    return pos >= CA_PAD


def _cattn_specs(S):
    q_spec = pl.BlockSpec((CHUNK, HG_LANES), lambda h, c: (c, h))
    kv_spec = pl.BlockSpec((S + CA_PAD, HG_LANES), lambda h, c: (0, h))
    b_spec = pl.BlockSpec((HG * CHUNK, CA_BAND), lambda h, c: (h, 0))
    return q_spec, kv_spec, b_spec


def cattn_fwd(qn, kp, vp, bias):
    S = qn.shape[0]
    nc = S // CHUNK
    q_spec, kv_spec, b_spec = _cattn_specs(S)

    def body(q_ref, k_ref, v_ref, b_ref, o_ref):
        c = pl.program_id(1)
        start = pl.multiple_of(c * CHUNK, CHUNK)
        kb = k_ref[pl.ds(start, CA_BAND), :]
        vb = v_ref[pl.ds(start, CA_BAND), :]
        o_ref[...] = _cattn_chunk(q_ref[...], kb, vb, b_ref[...], _cattn_valid(c)).astype(o_ref.dtype)

    return pl.pallas_call(
        body, name="cattn_fwd", grid=(CA_H // HG, nc), in_specs=[q_spec, kv_spec, kv_spec, b_spec],
        out_specs=q_spec, out_shape=jax.ShapeDtypeStruct((S, D), BF16),
        compiler_params=_cp("parallel", "arbitrary"),
    )(qn, kp, vp, bias)


def kv_prep(qkv, kg):
    S = qkv.shape[0]
    tile = ROW_TILE
    lead = CA_PAD // tile

    def body(k_ref, v_ref, g_ref, kp_ref, vp_ref):
        i = pl.program_id(0)

        @pl.when(i < lead)
        def _():
            kp_ref[...] = jnp.zeros_like(kp_ref)
            vp_ref[...] = jnp.zeros_like(vp_ref)

        @pl.when(i >= lead)
        def _():
            kp_ref[...] = f_headnorm(k_ref[...], g_ref[...])[0].astype(BF16)
            vp_ref[...] = v_ref[...].astype(BF16)

    src = lambda cb: pl.BlockSpec((tile, D), lambda i, cb=cb: (jnp.maximum(i - lead, 0), cb))
    out = pl.BlockSpec((tile, D), lambda i: (i, 0))
    return pl.pallas_call(
        body, name="kv_prep", grid=((S + CA_PAD) // tile,),
        in_specs=[src(1), src(2), pl.BlockSpec((1, D), lambda i: (0, 0))], out_specs=[out, out],
        out_shape=[jax.ShapeDtypeStruct((S + CA_PAD, D), BF16)] * 2,
        compiler_params=_cp("parallel"),
    )(qkv, qkv, kg)


def cattn_bwd(qn, kp, vp, bias, do):
    S = qn.shape[0]
    nc = S // CHUNK
    q_spec, kv_spec, b_spec = _cattn_specs(S)

    def body(q_ref, k_ref, v_ref, b_ref, do_ref, dq_ref, dk_ref, dv_ref, db_ref):
        c = pl.program_id(1)

        @pl.when(c == 0)
        def _():
            dk_ref[...] = jnp.zeros_like(dk_ref)
            dv_ref[...] = jnp.zeros_like(dv_ref)
            db_ref[...] = jnp.zeros_like(db_ref)

        start = pl.multiple_of(c * CHUNK, CHUNK)
        kb = k_ref[pl.ds(start, CA_BAND), :].astype(F32)
        vb = v_ref[pl.ds(start, CA_BAND), :].astype(F32)
        valid = _cattn_valid(c)
        _, vjp = jax.vjp(lambda q, k, v, b: _cattn_chunk(q, k, v, b, valid), q_ref[...], kb, vb, b_ref[...])
        dq, dk, dv, db = vjp(do_ref[...])
        dq_ref[...] = dq
        dk_ref[pl.ds(start, CA_BAND), :] += dk
        dv_ref[pl.ds(start, CA_BAND), :] += dv
        db_ref[...] += db

    sd = jax.ShapeDtypeStruct
    return pl.pallas_call(
        body, name="cattn_bwd", grid=(CA_H // HG, nc), in_specs=[q_spec, kv_spec, kv_spec, b_spec, q_spec],
        out_specs=[q_spec, kv_spec, kv_spec, b_spec],
        out_shape=[sd((S, D), F32), sd((S + CA_PAD, D), F32), sd((S + CA_PAD, D), F32),
                   sd((CA_H * CHUNK, CA_BAND), F32)],
        compiler_params=_cp("parallel", "arbitrary"),
    )(qn, kp, vp, bias, do)


_REL_IDX = np.clip(np.arange(CHUNK)[:, None] - np.arange(CA_BAND)[None, :] + CA_PAD, -MAX_REL, MAX_REL) + MAX_REL
SKEW_W = CA_BAND + CHUNK


def rel_bias_grad(dbias):
    padded = jnp.pad(dbias, ((0, 0), (0, 0), (CHUNK, 0)))
    flat = jnp.pad(padded.reshape(CA_H, CHUNK * SKEW_W), ((0, 0), (0, CHUNK)))
    skew = flat.reshape(CA_H, CHUNK, SKEW_W + 1)

    first_near = SKEW_W - CHUNK - MAX_REL

    def fn(t):
        colsum = jnp.sum(t, axis=1, keepdims=True)
        j = lax.broadcasted_iota(jnp.int32, colsum.shape, 2)
        far = jnp.sum(jnp.where(j < first_near, colsum, 0.0), axis=2, keepdims=True)
        return (colsum + jnp.where(j == first_near, far, 0.0),)

    (colsum,) = whole("relbias_sum", fn, [skew], [((CA_H, 1, SKEW_W + 1), F32)])
    near = colsum[:, 0, first_near:SKEW_W][:, ::-1]
    return jnp.concatenate([jnp.zeros((CA_H, CHUNK + 1), F32), near], axis=1)


def rel_bias_expand(rb):
    near = rb[:, CHUNK + 1:][:, ::-1]
    far = jnp.broadcast_to(rb[:, 2 * MAX_REL:], (CA_H, SKEW_W - CHUNK - MAX_REL))
    t = jnp.concatenate([far, near, jnp.zeros((CA_H, 1), rb.dtype)], axis=1)
    rows = jnp.tile(t, (1, CHUNK))[:, :CHUNK * SKEW_W].reshape(CA_H, CHUNK, SKEW_W)
    return rows[:, :, CHUNK:]


def loss_head(y, target):
    S = y.shape[0]
    tile = min(ROW_TILE, S)

    def body(y_ref, t_ref, dy_ref, acc_ref):
        i = pl.program_id(0)
        e = y_ref[...] - t_ref[...]
        dy_ref[...] = e * (1.0 / D)
        part = jnp.sum(e * e, axis=0, keepdims=True) * (0.5 / D)

        @pl.when(i == 0)
        def _():
            acc_ref[...] = part

        @pl.when(i > 0)
        def _():
            acc_ref[...] += part

    row = pl.BlockSpec((tile, D), lambda i: (i, 0))
    return pl.pallas_call(
        body, name="loss_head", grid=(S // tile,), in_specs=[row, row],
        out_specs=[row, pl.BlockSpec((1, D), lambda i: (0, 0))],
        out_shape=[jax.ShapeDtypeStruct((S, D), F32), jax.ShapeDtypeStruct((1, D), F32)],
        compiler_params=_cp("arbitrary"),
    )(y, target)


ANY = pl.BlockSpec(memory_space=pl.ANY)


HBM = pl.BlockSpec(memory_space=pltpu.HBM)
SEM = pl.BlockSpec(memory_space=pltpu.SEMAPHORE)
EFFECT = pltpu.SideEffectType.DATAFLOW_SIDE_EFFECTING


def _chip_copies(src_ref, land_ref, send_sems, recv_sems, per_dest):
    x, y, c = lax.axis_index("x"), lax.axis_index("y"), lax.axis_index("c")
    me = 2 * x + y
    if per_dest == "sibling":
        cp = pltpu.make_async_remote_copy(src_ref=src_ref, dst_ref=land_ref, send_sem=send_sems.at[0],
                                          recv_sem=recv_sems.at[0], device_id=(x, y, 1 - c), device_id_type=MESH)
        return [(cp, cp)]
    out = []
    for j, (px, py) in enumerate([(1 - x, y), (x, 1 - y), (1 - x, 1 - y)]):
        peer = 2 * px + py
        if per_dest:
            send = (src_ref.at[peer], land_ref.at[j])
            recv = (src_ref.at[me], land_ref.at[j])
        else:
            send = (src_ref, land_ref.at[me])
            recv = (src_ref, land_ref.at[peer])
        mk = lambda s, d, j=j, px=px, py=py: pltpu.make_async_remote_copy(
            src_ref=s, dst_ref=d, send_sem=send_sems.at[j], recv_sem=recv_sems.at[j],
            device_id=(px, py, c), device_id_type=MESH)
        out.append((mk(*send), mk(*recv)))
    return out


def exchange_start(src, land, carry, name, per_dest):
    def body(src_ref, land_ref, carry_ref, send_sems, recv_sems, src_out, land_out, carry_out):
        for send, _ in _chip_copies(src_ref, land_ref, send_sems, recv_sems, per_dest):
            send.start()

    hbm = lambda a: pltpu.HBM(a.shape, a.dtype)
    n = 1 if per_dest == "sibling" else 3
    return pl.pallas_call(
        body, name=name,
        out_shape=(pltpu.SemaphoreType.DMA((n,)), pltpu.SemaphoreType.DMA((n,)), hbm(src), hbm(land), hbm(carry)),
        in_specs=(HBM, HBM, HBM), out_specs=(SEM, SEM, HBM, HBM, HBM),
        input_output_aliases={0: 2, 1: 3, 2: 4},
        compiler_params=pltpu.CompilerParams(has_side_effects=EFFECT),
    )(pltpu.with_memory_space_constraint(src, pltpu.HBM), pltpu.with_memory_space_constraint(land, pltpu.HBM),
      pltpu.with_memory_space_constraint(carry, pltpu.HBM))


def exchange_wait(src, land, send_sems, recv_sems, after, name, per_dest):
    def body(src_ref, land_ref, send_sems_ref, recv_sems_ref, after_ref, src_out, land_out):
        for send, recv in _chip_copies(src_ref, land_ref, send_sems_ref, recv_sems_ref, per_dest):
            send.wait_send()
            recv.wait_recv()

    hbm = lambda a: pltpu.HBM(a.shape, a.dtype)
    return pl.pallas_call(
        body, name=name, out_shape=(hbm(src), hbm(land)),
        in_specs=(HBM, HBM, SEM, SEM, ANY), out_specs=(HBM, HBM), input_output_aliases={0: 0, 1: 1},
        compiler_params=pltpu.CompilerParams(has_side_effects=EFFECT),
    )(src, land, send_sems, recv_sems, after)


def sibling_exchange(srcs, name):
    n = len(srcs)

    def body(*refs):
        src_refs, out_refs, send_sems, recv_sems = refs[:n], refs[n:2 * n], refs[2 * n], refs[2 * n + 1]
        x, y, c = lax.axis_index("x"), lax.axis_index("y"), lax.axis_index("c")
        copies = [pltpu.make_async_remote_copy(src_ref=s, dst_ref=o, send_sem=send_sems.at[k], recv_sem=recv_sems.at[k],
                                               device_id=(x, y, 1 - c), device_id_type=MESH)
                  for k, (s, o) in enumerate(zip(src_refs, out_refs))]
        for cp in copies:
            cp.start()
        for cp in copies:
            cp.wait()

    return pl.pallas_call(
        body, name=name, in_specs=[ANY] * n, out_specs=[ANY] * n,
        out_shape=[jax.ShapeDtypeStruct(s.shape, s.dtype) for s in srcs],
        scratch_shapes=[pltpu.SemaphoreType.DMA((n,)), pltpu.SemaphoreType.DMA((n,))],
    )(*srcs)


def all_exchange(src, name):
    def body(src_ref, out_ref, send_sems, recv_sems, local_sem):
        x, y, c = lax.axis_index("x"), lax.axis_index("y"), lax.axis_index("c")
        me = 4 * x + 2 * y + c
        local = pltpu.make_async_copy(src_ref, out_ref.at[me], local_sem)
        local.start()
        sends = []
        peers = []
        for k in range(1, N_DEV):
            bx, by, bc = (k >> 2) & 1, (k >> 1) & 1, k & 1
            px = 1 - x if bx else x
            py = 1 - y if by else y
            pc = 1 - c if bc else c
            peers.append((px, py, pc))
        for k, peer in enumerate(peers):
            cp = pltpu.make_async_remote_copy(src_ref=src_ref, dst_ref=out_ref.at[me], send_sem=send_sems.at[k],
                                              recv_sem=recv_sems.at[k], device_id=peer, device_id_type=MESH)
            cp.start()
            sends.append(cp)
        for k, (px, py, pc) in enumerate(peers):
            pltpu.make_async_remote_copy(src_ref=src_ref, dst_ref=out_ref.at[4 * px + 2 * py + pc],
                                         send_sem=send_sems.at[k], recv_sem=recv_sems.at[k],
                                         device_id=(px, py, pc), device_id_type=MESH).wait_recv()
        for cp in sends:
            cp.wait_send()
        local.wait()

    return pl.pallas_call(
        body, name=name, in_specs=[ANY], out_specs=ANY,
        out_shape=jax.ShapeDtypeStruct((N_DEV,) + tuple(src.shape), src.dtype),
        scratch_shapes=[pltpu.SemaphoreType.DMA((N_DEV - 1,)), pltpu.SemaphoreType.DMA((N_DEV - 1,)),
                        pltpu.SemaphoreType.DMA],
    )(src)


def sum_own_slabs(own, land, name, tile=512):
    R, C = own.shape
    n = land.shape[0]
    tile = _tile(R, tile, 16)

    def body(o_ref, t_ref, out_ref):
        acc = o_ref[...].astype(F32)
        for s in range(n):
            acc = acc + t_ref[s].astype(F32)
        out_ref[...] = acc

    return pl.pallas_call(
        body, name=name, grid=(R // tile,),
        in_specs=[pl.BlockSpec((tile, C), lambda i: (i, 0)), pl.BlockSpec((n, tile, C), lambda i: (0, i, 0))],
        out_specs=pl.BlockSpec((tile, C), lambda i: (i, 0)), out_shape=jax.ShapeDtypeStruct((R, C), F32),
        compiler_params=_cp("parallel"),
    )(own, land)


def sum_slabs(t, name, tile=512):
    n, R, C = t.shape
    tile = _tile(R, tile, 16)

    def body(t_ref, o_ref):
        acc = t_ref[0].astype(F32)
        for s in range(1, n):
            acc = acc + t_ref[s].astype(F32)
        o_ref[...] = acc

    return pl.pallas_call(
        body, name=name, grid=(R // tile,), in_specs=[pl.BlockSpec((n, tile, C), lambda i: (0, i, 0))],
        out_specs=pl.BlockSpec((tile, C), lambda i: (i, 0)), out_shape=jax.ShapeDtypeStruct((R, C), F32),
        compiler_params=_cp("parallel"),
    )(t)


PACK_ROW_MULT = 512


def _pad_rows(a, mult=16):
    r = (-a.shape[0]) % mult
    return jnp.pad(a, ((0, r), (0, 0))) if r else a


BIG = [
    ("ab_w_in", True, 2, 642), ("c_w_qkv", True, 2, 768), ("xa_w_kv", True, 4, 512),
    ("f_w_gate", True, 4, 704), ("f_w_up", True, 4, 704),
    ("ab_w_out", False, 2, 256), ("c_w_out", False, 2, 256), ("xa_w_q", False, 4, 256),
    ("xa_w_out", False, 4, 256), ("f_w_down", False, 4, 704), ("s5_w_glu", False, 2, 64),
]


GROUPS = ("B", "A")
GROUP_ROW_MULT = 64


def group_spec(layer, grp):
    i = layer // 2
    if grp == "A":
        return [("xa_w_kv", layer, True, 512), ("xa_w_q", layer, False, 256), ("xa_w_out", layer, False, 256),
                ("f_w_gate", layer, True, 704), ("f_w_up", layer, True, 704), ("f_w_down", layer, False, 704)]
    if layer % 2 == 0:
        return [("ab_w_in", i, True, 642), ("ab_w_out", i, False, 256), ("s5_w_glu", i, False, 64)]
    return [("c_w_qkv", i, True, 768), ("c_w_out", i, False, 256)]


def _seg_rows(rows):
    return rows + ((-rows) % 16)


def _f32_rows(a):
    bits = lax.bitcast_convert_type(a.reshape(-1), BF16).reshape(-1)
    return jnp.pad(bits, (0, 16 * D - bits.shape[0])).reshape(16, D)


def pack_group_shards(p, layer, grp):
    segs = []
    for name, idx, transposed, rows in group_spec(layer, grp):
        w = p[name][idx]
        if transposed:
            w = w.T
        segs.append(_pad_rows(w.astype(BF16).reshape(-1, D)))
    if grp == "B":
        small = p["gdn_conv_w"] if layer % 2 == 0 else p["c_norm_g"]
        segs.append(_f32_rows(small[layer // 2]))
    return _pad_rows(jnp.concatenate(segs, axis=0), GROUP_ROW_MULT)


def unpack_group_gathered(g, layer, grp):
    out, off = {}, 0
    for name, idx, transposed, rows in group_spec(layer, grp):
        seg = g[:, off:off + rows]
        if name == "s5_w_glu":
            out[name] = seg.reshape(N_CHIPS * 128, 512)
        else:
            out[name] = seg.reshape(N_CHIPS * rows, D)
        off += _seg_rows(rows)
    if grp == "B":
        n = 4 * 384 if layer % 2 == 0 else 256
        bits = g[:, off:off + 16].reshape(N_CHIPS, -1)[:, :2 * n].reshape(N_CHIPS, n, 2)
        small = lax.bitcast_convert_type(bits, F32)
        if layer % 2 == 0:
            out["gdn_conv_w"] = jnp.swapaxes(small.reshape(N_CHIPS, 4, 384), 0, 1).reshape(4, 1536)
        else:
            out["c_norm_g"] = small.reshape(D)
    return out


def pack_group_grads(gr, layer, grp):
    segs = []
    for name, idx, transposed, rows in group_spec(layer, grp):
        w = gr[name].astype(BF16)
        seg = w.reshape(N_CHIPS, rows, D)
        r = (-rows) % 16
        if r:
            seg = jnp.pad(seg, ((0, 0), (0, r), (0, 0)))
        segs.append(seg)
    out = jnp.concatenate(segs, axis=1)
    return jnp.pad(out, ((0, 0), (0, (-out.shape[1]) % GROUP_ROW_MULT), (0, 0)))


def unpack_group_reduced(g, layer, grp):
    out, off = {}, 0
    for name, idx, transposed, rows in group_spec(layer, grp):
        seg = g[off:off + rows]
        if name == "s5_w_glu":
            out[name] = seg.reshape(128, 512)
        else:
            out[name] = seg.T if (transposed and name not in ADAM_TRANSPOSED) else seg
        off += _seg_rows(rows)
    return out


ADAM_TRANSPOSED = ("f_w_gate", "f_w_up")


SMALL = ["ab_norm_g", "s5_a_re", "s5_a_im", "s5_log_dt", "s5_b_re", "s5_b_im", "s5_c_re", "s5_c_im", "s5_d",
         "s5_b_glu", "gdn_conv_w", "gdn_a_log", "gdn_dt_bias", "gdn_out_norm_g", "c_norm_g", "c_q_norm_g",
         "c_k_norm_g", "c_rel_bias", "mem_norm_g", "xa_norm_g", "xa_q_norm_g", "xa_k_norm_g", "f_norm_g"]


def _lane_rows(a):
    flat = a.reshape(-1).astype(F32)
    return jnp.pad(flat, (0, (-flat.shape[0]) % 1024)).reshape(-1, 128)


def pack_small(d, extra=None):
    parts = [_lane_rows(d[n]) for n in SMALL]
    if extra is not None:
        parts.append(_lane_rows(extra))
    rows = jnp.concatenate(parts, axis=0)
    return jnp.pad(rows, ((0, (-rows.shape[0]) % 128), (0, 0)))


def unpack_small(rows, shapes):
    out, off = {}, 0
    for n in SMALL:
        sz = int(np.prod(shapes[n]))
        k = 8 * -(-sz // 1024)
        out[n] = rows[off:off + k].reshape(-1)[:sz].reshape(shapes[n])
        off += k
    return out, rows[off:]


def _s5_blockdiag_b(b):
    bt = jnp.swapaxes(b, 1, 2).reshape(S5_GB, 8, S5_C, S5_P)
    eye = jnp.eye(8, dtype=b.dtype)
    return jnp.einsum("bgcp,gh->bgchp", bt, eye).reshape(S5_GB, 8 * S5_C, 8 * S5_P)


def _s5_blockdiag_c(c):
    ct = jnp.swapaxes(c, 1, 2).reshape(S5_GB, 8, S5_P, S5_C)
    eye = jnp.eye(8, dtype=c.dtype)
    return jnp.einsum("bgpc,gh->bgphc", ct, eye).reshape(S5_GB, 8 * S5_P, 8 * S5_C)


def _s5_diag_b(db):
    t = db.reshape(S5_GB, 8, S5_C, 8, S5_P)
    t = jnp.transpose(t, (0, 2, 4, 1, 3)).reshape(S5_GB, S5_C, S5_P, 64)
    d = t[..., ::9]
    return jnp.transpose(d, (0, 3, 2, 1)).reshape(S5_G, S5_P, S5_C)


def _s5_diag_c(dc):
    t = dc.reshape(S5_GB, 8, S5_P, 8, S5_C)
    t = jnp.transpose(t, (0, 2, 4, 1, 3)).reshape(S5_GB, S5_P, S5_C, 64)
    d = t[..., ::9]
    return jnp.transpose(d, (0, 3, 2, 1)).reshape(S5_G, S5_C, S5_P)


def _heads(t):
    return jnp.swapaxes(t.reshape(t.shape[0], CA_H, CA_D), 0, 1)


def _unheads(t):
    return jnp.swapaxes(t, 0, 1).reshape(t.shape[1], D)


def local_step(x, mem, target, p, wsrc, gsink):
    S = x.shape[0]
    row2 = lambda a: a.reshape(1, -1)
    saved = []
    (mem_n,) = row_fwd("mem_norm", f_norm, [mem], [row2(p["mem_norm_g"])], [(D, BF16)])
    gs = {}

    for layer in range(DEPTH):
        i = layer // 2
        w = wsrc(layer, "B", x)
        sv = {"x0": x, "wB": w}
        if layer % 2 == 0:
            (h,) = row_fwd("norm", f_norm, [x], [row2(p["ab_norm_g"][i])], [(D, BF16)])
            w_in = w["ab_w_in"]
            pm = matmul("nt", h, w_in[:2560], "proj_main")
            pab = matmul("nt", h, w_in[2560:], "proj_ab")
            s5p = dict(
                are=p["s5_a_re"][i].reshape(1, -1), aim=p["s5_a_im"][i].reshape(1, -1),
                ldt=jnp.broadcast_to(p["s5_log_dt"][i][:, None], (S5_G, S5_P)).reshape(1, -1),
                b_re=_s5_blockdiag_b(p["s5_b_re"][i]), b_im=_s5_blockdiag_b(p["s5_b_im"][i]),
                c_re=_s5_blockdiag_c(p["s5_c_re"][i]), c_im=_s5_blockdiag_c(p["s5_c_im"][i]),
                dv=p["s5_d"][i].reshape(1, -1))
            y5, st5 = s5_fwd(pm, **s5p)
            (a_out,) = row_fwd("glu", f_glu, [y5], [w["s5_w_glu"], row2(p["s5_b_glu"][i])], [(S5_W, F32)])
            conv_w = w["gdn_conv_w"]
            qkvc = conv_fwd(pm, conv_w)
            abt = jnp.swapaxes(pab, 0, 1)[:, :, None]
            alog = p["gdn_a_log"][i].reshape(GDN_H, 1, 1)
            dtb = p["gdn_dt_bias"][i].reshape(GDN_H, 1, 1)
            og = row2(p["gdn_out_norm_g"][i])
            b_out, stg = gdn_fwd(qkvc, pm, abt, alog, dtb, og)
            cat = jnp.concatenate([a_out, b_out], axis=1)
            x, hq = matmul("nn", cat, w["ab_w_out"], "mix_out", add=x, norm_out=row2(p["xa_norm_g"][layer]))
            sv.update(h=h, pm=pm, s5p=s5p, y5=y5, st5=st5, qkvc=qkvc, abt=abt, alog=alog, dtb=dtb, og=og,
                      stg=stg, cat=cat, conv_w=conv_w)
        else:
            (h,) = row_fwd("norm", f_norm, [x], [row2(w["c_norm_g"])], [(D, BF16)])
            qkv = matmul("nt", h, w["c_w_qkv"], "proj_qkv")
            qg = jnp.tile(row2(p["c_q_norm_g"][i]), (1, CA_H))
            kg = jnp.tile(row2(p["c_k_norm_g"][i]), (1, CA_H))
            (qn,) = row_fwd("headnorm_q", f_headnorm, [(qkv, D, 0)], [qg], [(D, F32)])
            kp, vp = kv_prep(qkv, kg)
            bias = rel_bias_expand(p["c_rel_bias"][i]).reshape(CA_H * CHUNK, CA_BAND)
            o = cattn_fwd(qn, kp, vp, bias)
            x, hq = matmul("nn", o, w["c_w_out"], "mix_out", add=x, norm_out=row2(p["xa_norm_g"][layer]))
            sv.update(h=h, qkv=qkv, qg=qg, kg=kg, qn=qn, kp=kp, vp=vp, bias=bias, o=o)
        sv["x1"] = x
        w = wsrc(layer, "A", x)
        sv["wA"] = w
        qx = matmul("nn", hq, w["xa_w_q"], "xa_q")
        kv = matmul("nt", mem_n, w["xa_w_kv"], "xa_kv")
        xqg, xkg = row2(p["xa_q_norm_g"][layer]), row2(p["xa_k_norm_g"][layer])
        (ox,) = row_fwd("xattn", f_xattn, [qx], [kv, xqg, xkg], [(D, BF16)])
        x, hf = matmul("nn", ox, w["xa_w_out"], "xa_out", add=x, norm_out=row2(p["f_norm_g"][layer]))
        sv.update(hq=hq, qx=qx, kv=kv, ox=ox)
        sv["x2"] = x
        gate = matmul("nt", hf, w["f_w_gate"], "ffn_gate")
        up = matmul("nt", hf, w["f_w_up"], "ffn_up")
        (act,) = row_fwd("swiglu", f_swiglu, [gate, up], [], [(FFN, BF16)])
        x = matmul("nn", act, w["f_w_down"], "ffn_down", add=x)
        sv.update(hf=hf, gate=gate, up=up, act=act)
        saved.append(sv)

    dx, loss_vec = loss_head(x, target)

    dmem_n = None
    for layer in reversed(range(DEPTH)):
        i = layer // 2
        sv = saved[layer]
        w, gw = sv["wA"], {}
        dact = matmul("nt", dx, w["f_w_down"], "ffn_dact")
        gw["f_w_down"] = matmul("tn", sv["act"], dx, "ffn_dwd", out_dtype=BF16)
        dgate, dup = row_bwd("swiglu_bwd", f_swiglu, [sv["gate"], sv["up"]], [], [dact], [0, 1], [],
                             row_dtypes=[BF16, BF16])
        gw["f_w_gate"] = matmul("tn", dgate, sv["hf"], "ffn_dwg", out_dtype=BF16)
        gw["f_w_up"] = matmul("tn", dup, sv["hf"], "ffn_dwu", out_dtype=BF16)
        dh = matmul("nn", dgate, w["f_w_gate"], "ffn_dhg")
        dx, dg = matmul("nn", dup, w["f_w_up"], "ffn_dhu", add=dh,
                        norm_bwd=(sv["x2"], row2(p["f_norm_g"][layer]), dx))
        gs.setdefault("f_norm_g", [None] * DEPTH)[layer] = dg[0]
        do = matmul("nt", dx, w["xa_w_out"], "xa_do")
        gw["xa_w_out"] = matmul("tn", sv["ox"], dx, "xa_dwo", out_dtype=BF16)
        xqg, xkg = row2(p["xa_q_norm_g"][layer]), row2(p["xa_k_norm_g"][layer])
        dqx, dkv, dqg, dkg = row_bwd("xattn_bwd", f_xattn, [sv["qx"]], [sv["kv"], xqg, xkg], [do],
                                     [0], [0, 1, 2])
        gs.setdefault("xa_q_norm_g", [None] * DEPTH)[layer] = dqg[0]
        gs.setdefault("xa_k_norm_g", [None] * DEPTH)[layer] = dkg[0]
        gw["xa_w_q"] = matmul("tn", sv["hq"], dqx, "xa_dwq", out_dtype=BF16)
        gw["xa_w_kv"] = matmul("tn", dkv, mem_n, "xa_dwkv", out_dtype=BF16)
        dmem_n = matmul("nn", dkv, w["xa_w_kv"], "xa_dmem", add=dmem_n)
        dx, dg = matmul("nt", dqx, w["xa_w_q"], "xa_dhq", norm_bwd=(sv["x1"], row2(p["xa_norm_g"][layer]), dx))
        gs.setdefault("xa_norm_g", [None] * DEPTH)[layer] = dg[0]
        dx = gsink(layer, "A", gw, dx)
        w, gw = sv["wB"], {}
        if layer % 2 == 0:
            dcat = matmul("nt", dx, w["ab_w_out"], "mix_dcat")
            gw["ab_w_out"] = matmul("tn", sv["cat"], dx, "mix_dwo", out_dtype=BF16)
            dy5, dwglu, dbglu = row_bwd("glu_bwd", f_glu, [sv["y5"]], [w["s5_w_glu"], row2(p["s5_b_glu"][i])],
                                        [(dcat, S5_W, 0)], [0], [0, 1])
            gw["s5_w_glu"] = dwglu
            gs.setdefault("s5_b_glu", [None] * 2)[i] = dbglu[0]
            s5p = sv["s5p"]
            du, dare, daim, dldt, dbre, dbim, dcre, dcim, ddv = s5_bwd(sv["pm"], sv["st5"], dy5, **s5p)
            (dldt_g,) = whole("s5_dt_sum", lambda t: (jnp.sum(t, axis=1, keepdims=True),),
                              [dldt.reshape(S5_G, S5_P)], [((S5_G, 1), F32)])
            for nme, val in (("s5_a_re", dare.reshape(S5_G, S5_P)), ("s5_a_im", daim.reshape(S5_G, S5_P)),
                             ("s5_log_dt", dldt_g[:, 0]), ("s5_b_re", _s5_diag_b(dbre)),
                             ("s5_b_im", _s5_diag_b(dbim)), ("s5_c_re", _s5_diag_c(dcre)),
                             ("s5_c_im", _s5_diag_c(dcim)), ("s5_d", ddv.reshape(S5_G, S5_C))):
                gs.setdefault(nme, [None] * 2)[i] = val
            dq, dk, dv, dgate, dal, dbl, dalog, ddtb, dog = gdn_bwd(
                sv["qkvc"], sv["pm"], sv["abt"], sv["alog"], sv["dtb"], sv["og"], sv["stg"], dcat[:, S5_W:])
            (dog_s,) = whole("gdn_og_sum", lambda t: (jnp.sum(t, axis=0, keepdims=True),),
                             [dog.reshape(GDN_H, GDN_D)], [((1, GDN_D), F32)])
            gs.setdefault("gdn_out_norm_g", [None] * 2)[i] = dog_s[0]
            gs.setdefault("gdn_a_log", [None] * 2)[i] = dalog.reshape(GDN_H)
            gs.setdefault("gdn_dt_bias", [None] * 2)[i] = ddtb.reshape(GDN_H)
            dqkvc = jnp.concatenate([dq, dk, dv], axis=1)
            dqkv, dconv = conv_bwd(sv["pm"], sv["conv_w"], dqkvc)
            gs.setdefault("gdn_conv_w", [None] * 2)[i] = dconv
            dpm = jnp.concatenate([du, dqkv, dgate], axis=1).astype(BF16)
            dpab = jnp.swapaxes(jnp.concatenate([dal, dbl], axis=0)[:, :, 0], 0, 1)
            dw_main = matmul("tn", dpm, sv["h"], "proj_dw", out_dtype=BF16)
            dw_ab = matmul("tn", dpab, sv["h"], "proj_ab_dw", out_dtype=BF16)
            gw["ab_w_in"] = jnp.concatenate([dw_main, dw_ab], axis=0)
            w_in = w["ab_w_in"]
            dh = matmul("nn", dpm, w_in[:2560], "proj_dh")
            dx, dg = matmul("nn", dpab, w_in[2560:], "proj_ab_dh", add=dh,
                            norm_bwd=(sv["x0"], row2(p["ab_norm_g"][i]), dx))
            gs.setdefault("ab_norm_g", [None] * 2)[i] = dg[0]
        else:
            do = matmul("nt", dx, w["c_w_out"], "mix_dcat")
            gw["c_w_out"] = matmul("tn", sv["o"], dx, "mix_dwo", out_dtype=BF16)
            dqn, dkp, dvp, dbias = cattn_bwd(sv["qn"], sv["kp"], sv["vp"], sv["bias"], do)
            gs.setdefault("c_rel_bias", [None] * 2)[i] = rel_bias_grad(dbias.reshape(CA_H, CHUNK, CA_BAND))
            dq, dqg = row_bwd("headnorm_bwd", f_headnorm, [(sv["qkv"], D, 0)], [sv["qg"]], [dqn], [0], [0])
            dk, dkg = row_bwd("headnorm_bwd", f_headnorm, [(sv["qkv"], D, 1)], [sv["kg"]],
                              [(dkp, D, 0, CA_PAD)], [0], [0])
            head_sum = lambda t: (jnp.sum(t, axis=0, keepdims=True),)
            (dqg,) = whole("headgain_sum", head_sum, [dqg.reshape(CA_H, CA_D)], [((1, CA_D), F32)])
            (dkg,) = whole("headgain_sum", head_sum, [dkg.reshape(CA_H, CA_D)], [((1, CA_D), F32)])
            gs.setdefault("c_q_norm_g", [None] * 2)[i] = dqg[0]
            gs.setdefault("c_k_norm_g", [None] * 2)[i] = dkg[0]
            dqkv = jnp.concatenate([dq, dk, dvp[CA_PAD:]], axis=1).astype(BF16)
            gw["c_w_qkv"] = matmul("tn", dqkv, sv["h"], "proj_qkv_dw", out_dtype=BF16)
            dx, dg = matmul("nn", dqkv, w["c_w_qkv"], "proj_qkv_dh",
                            norm_bwd=(sv["x0"], row2(w["c_norm_g"]), dx))
            gs.setdefault("c_norm_g", [None] * 2)[i] = dg[0]
        dx = gsink(layer, "B", gw, dx)
    (dmg,) = row_bwd("mem_norm_bwd", f_norm, [mem], [row2(p["mem_norm_g"])], [dmem_n], [], [0])
    small = {n: jnp.stack(v) for n, v in gs.items()}
    small["mem_norm_g"] = dmg[0]
    return loss_vec, dx, small


ADAM_BLOCK_BYTES = 3 << 19


def adam(w, g, m, v, name):
    shape = w.shape
    if w.ndim == 3:
        d0, n, d2 = shape
        fits = [t for t in range(8, n + 1, 8) if n % t == 0 and d0 * t * d2 * 4 <= ADAM_BLOCK_BYTES]
        return tuple(row_fwd(name, f_adam, [w, g, m, v], [], [((d0, d2), F32)] * 3, tile=max(fits)))
    cols = shape[-1]
    w2, g2, m2, v2 = (t.reshape(-1, cols) for t in (w, g, m, v))
    rows = w2.shape[0]
    tile = rows if rows <= 512 else _tile(rows, 512, 8)
    outs = row_fwd(name, f_adam, [w2, g2, m2, v2], [], [(cols, F32)] * 3, tile=tile)
    return tuple(o.reshape(shape) for o in outs)


WEIGHTS = ['ab_norm_g', 'ab_w_in', 'ab_w_out', 's5_a_re', 's5_a_im', 's5_log_dt', 's5_b_re', 's5_b_im', 's5_c_re',
           's5_c_im', 's5_d', 's5_w_glu', 's5_b_glu', 'gdn_conv_w', 'gdn_a_log', 'gdn_dt_bias', 'gdn_out_norm_g',
           'c_norm_g', 'c_w_qkv', 'c_w_out', 'c_q_norm_g', 'c_k_norm_g', 'c_rel_bias', 'mem_norm_g', 'xa_norm_g',
           'xa_w_q', 'xa_w_kv', 'xa_w_out', 'xa_q_norm_g', 'xa_k_norm_g', 'f_norm_g', 'f_w_gate', 'f_w_up',
           'f_w_down']
SHARDED_SMALL = {"gdn_conv_w": (2, 384), "c_norm_g": (1, 256)}


def kernel(x, mem, ab_norm_g, ab_w_in, ab_w_out, s5_a_re, s5_a_im, s5_log_dt, s5_b_re, s5_b_im, s5_c_re, s5_c_im, s5_d, s5_w_glu, s5_b_glu, gdn_conv_w, gdn_a_log, gdn_dt_bias, gdn_out_norm_g, c_norm_g, c_w_qkv, c_w_out, c_q_norm_g, c_k_norm_g, c_rel_bias, mem_norm_g, xa_norm_g, xa_w_q, xa_w_kv, xa_w_out, xa_q_norm_g, xa_k_norm_g, f_norm_g, f_w_gate, f_w_up, f_w_down, loss_target, m_ab_norm_g, m_ab_w_in, m_ab_w_out, m_s5_a_re, m_s5_a_im, m_s5_log_dt, m_s5_b_re, m_s5_b_im, m_s5_c_re, m_s5_c_im, m_s5_d, m_s5_w_glu, m_s5_b_glu, m_gdn_conv_w, m_gdn_a_log, m_gdn_dt_bias, m_gdn_out_norm_g, m_c_norm_g, m_c_w_qkv, m_c_w_out, m_c_q_norm_g, m_c_k_norm_g, m_c_rel_bias, m_mem_norm_g, m_xa_norm_g, m_xa_w_q, m_xa_w_kv, m_xa_w_out, m_xa_q_norm_g, m_xa_k_norm_g, m_f_norm_g, m_f_w_gate, m_f_w_up, m_f_w_down, v_ab_norm_g, v_ab_w_in, v_ab_w_out, v_s5_a_re, v_s5_a_im, v_s5_log_dt, v_s5_b_re, v_s5_b_im, v_s5_c_re, v_s5_c_im, v_s5_d, v_s5_w_glu, v_s5_b_glu, v_gdn_conv_w, v_gdn_a_log, v_gdn_dt_bias, v_gdn_out_norm_g, v_c_norm_g, v_c_w_qkv, v_c_w_out, v_c_q_norm_g, v_c_k_norm_g, v_c_rel_bias, v_mem_norm_g, v_xa_norm_g, v_xa_w_q, v_xa_w_kv, v_xa_w_out, v_xa_q_norm_g, v_xa_k_norm_g, v_f_norm_g, v_f_w_gate, v_f_w_up, v_f_w_down):
    args = locals()
    p = {n: args[n] for n in WEIGHTS}
    m = {n: args["m_" + n] for n in WEIGHTS}
    v = {n: args["v_" + n] for n in WEIGHTS}
    chip = 2 * lax.axis_index("x") + lax.axis_index("y")

    carry = x[0]
    gathers = {}
    for layer in range(DEPTH):
        for grp in GROUPS:
            src = pack_group_shards(p, layer, grp)
            land = lax.dynamic_update_slice(lax.empty((N_CHIPS,) + src.shape, BF16), src[None], (chip, 0, 0))
            send_sems, recv_sems, src, land, carry = exchange_start(
                src, land, carry, f"gather_start_{layer}{grp}", per_dest=False)
            gathers[layer, grp] = (src, land, send_sems, recv_sems)

    def wsrc(layer, grp, after):
        src, land, send_sems, recv_sems = gathers[layer, grp]
        _, land = exchange_wait(src, land, send_sems, recv_sems, after, f"gather_wait_{layer}{grp}", per_dest=False)
        return unpack_group_gathered(land, layer, grp)

    scatters, siblings = [], []
    LAG = 2

    def finish(carry):
        layer, grp, src, land, send_sems, recv_sems = scatters[len(siblings)]
        src, land = exchange_wait(src, land, send_sems, recv_sems, carry, f"scatter_wait_{layer}{grp}", per_dest=True)
        own = lax.dynamic_index_in_dim(src, chip, axis=0, keepdims=False)
        part = sum_own_slabs(own, land, "sum_chips")
        send_sems, recv_sems, part, other, carry = exchange_start(
            part, lax.empty(part.shape, F32), carry, f"sibling_start_{layer}{grp}", per_dest="sibling")
        siblings.append((layer, grp, part, other, send_sems, recv_sems))
        return carry

    def gsink(layer, grp, gw, carry):
        src = pack_group_grads(gw, layer, grp)
        land = lax.empty((3,) + src.shape[1:], BF16)
        send_sems, recv_sems, src, land, carry = exchange_start(
            src, land, carry, f"scatter_start_{layer}{grp}", per_dest=True)
        scatters.append((layer, grp, src, land, send_sems, recv_sems))
        if len(scatters) > LAG:
            carry = finish(carry)
        return carry

    loss_vec, dx, g_small = local_step(carry, mem[0], loss_target[0], p, wsrc, gsink)
    while len(siblings) < len(scatters):
        dx = finish(dx)

    per_layer = {}
    for layer, grp, part, other, send_sems, recv_sems in siblings:
        part, other = exchange_wait(part, other, send_sems, recv_sems, dx, f"sibling_wait_{layer}{grp}",
                                    per_dest="sibling")
        (total,) = row_fwd("sum_cores", lambda a, b: (a + b,), [part, other], [], [(D, F32)],
                           tile=_tile(part.shape[0], 512, 16))
        for name, g in unpack_group_reduced(total, layer, grp).items():
            per_layer.setdefault(name, {})[layer] = g
    grads = {name: jnp.stack([d[k] for k in sorted(d)]) for name, d in per_layer.items()}

    full_shapes = {n: ((2, 4, 1536) if n == "gdn_conv_w" else (2, D) if n == "c_norm_g" else p[n].shape)
                   for n in SMALL}
    small_all = all_exchange(pack_small(g_small, extra=loss_vec), "gather_small")
    small_sum = sum_slabs(small_all, "sum_small")
    g_s, rest = unpack_small(small_sum, full_shapes)
    (loss11,) = whole("loss_sum", lambda t: (jnp.sum(jnp.sum(t, axis=1, keepdims=True), axis=0, keepdims=True),),
                      [rest[:D // 128]], [((1, 1), F32)])
    for n, (axis, width) in SHARDED_SMALL.items():
        g_s[n] = lax.dynamic_slice_in_dim(g_s[n], chip * width, width, axis=axis)
    grads.update(g_s)

    delta, new_m, new_v = {}, {}, {}
    for n in SMALL:
        shape = p[n].shape
        two = (1, shape[0]) if len(shape) == 1 else (int(np.prod(shape[:-1])), shape[-1])
        outs = whole("adam_" + n, f_adam, [t.reshape(two) for t in (p[n], grads[n], m[n], v[n])], [(two, F32)] * 3)
        delta[n], new_m[n], new_v[n] = (o.reshape(shape) for o in outs)
    for name, _, _, _ in BIG:
        if name in ADAM_TRANSPOSED:
            t = lambda a: jnp.swapaxes(a, 1, 2)
            outs = adam(t(p[name]), grads[name], t(m[name]), t(v[name]), "adam_" + name)
            delta[name], new_m[name], new_v[name] = (t(o) for o in outs)
            grads[name] = t(grads[name])
        else:
            delta[name], new_m[name], new_v[name] = adam(p[name], grads[name], m[name], v[name], "adam_" + name)

    return (loss11[0, 0], dx[None], *[grads[n] for n in WEIGHTS], *[delta[n] for n in WEIGHTS],
            *[new_m[n] for n in WEIGHTS], *[new_v[n] for n in WEIGHTS])
```

```python
import functools
import math

import numpy as np
import jax
import jax.numpy as jnp
from jax import lax
from jax.experimental import pallas as pl
from jax.experimental.pallas import tpu as pltpu

F32 = jnp.float32
BF16 = jnp.bfloat16
MESH = pl.DeviceIdType.MESH

D = 1024
CHUNK = 64
N_MEM = 256
EPS = 1e-6
S5_W = 512
S5_G = 32
S5_C = 16
S5_P = 64
S5_GB = 4
GDN_H = 4
GDN_D = 128
CA_H = 16
CA_D = 64
CA_LEFT = 8
CA_BAND = (CA_LEFT + 1) * CHUNK
CA_PAD = CA_LEFT * CHUNK
MAX_REL = 128
XA_H = 4
XA_D = 256
FFN = 2816
DEPTH = 4
N_CHIPS = 4
N_DEV = 8
LR, B1, B2, AEPS, WD, STEP = 0.001, 0.9, 0.999, 1e-08, 0.01, 10

VMEM_LIMIT = 56 * 1024 * 1024
ROW_TILE = 256
HI = lax.Precision.HIGHEST


def _cp(*sem):
    return pltpu.CompilerParams(dimension_semantics=sem, vmem_limit_bytes=VMEM_LIMIT)


def _dg(a, b, ca, cb):
    return lax.dot_general(a.astype(BF16), b.astype(BF16), (((ca,), (cb,)), ((), ())),
                           preferred_element_type=F32)


@jax.custom_vjp
def mm(a, b):
    return _dg(a, b, 1, 0)


@jax.custom_vjp
def mm_nt(a, b):
    return _dg(a, b, 1, 1)


@jax.custom_vjp
def mm_tn(a, b):
    return _dg(a, b, 0, 0)


mm.defvjp(lambda a, b: (mm(a, b), (a, b)), lambda r, g: (mm_nt(g, r[1]), mm_tn(r[0], g)))
mm_nt.defvjp(lambda a, b: (mm_nt(a, b), (a, b)), lambda r, g: (mm(g, r[1]), mm_tn(g, r[0])))
mm_tn.defvjp(lambda a, b: (mm_tn(a, b), (a, b)), lambda r, g: (mm_nt(r[1], g), mm(r[0], g)))


def _bdg(a, b, ca, cb):
    return lax.dot_general(a.astype(BF16), b.astype(BF16), (((ca,), (cb,)), ((0,), (0,))),
                           preferred_element_type=F32)


@jax.custom_vjp
def bmm(a, b):
    return _bdg(a, b, 2, 1)


@jax.custom_vjp
def bmm_nt(a, b):
    return _bdg(a, b, 2, 2)


@jax.custom_vjp
def bmm_tn(a, b):
    return _bdg(a, b, 1, 1)


bmm.defvjp(lambda a, b: (bmm(a, b), (a, b)), lambda r, g: (bmm_nt(g, r[1]), bmm_tn(r[0], g)))
bmm_nt.defvjp(lambda a, b: (bmm_nt(a, b), (a, b)), lambda r, g: (bmm(g, r[1]), bmm_tn(g, r[0])))
bmm_tn.defvjp(lambda a, b: (bmm_tn(a, b), (a, b)), lambda r, g: (bmm_nt(r[1], g), bmm(r[0], g)))


def _split(a):
    hi = a.astype(BF16)
    return hi, (a - hi.astype(F32)).astype(BF16)


def _dg3(a, b, ca, cb):
    (ah, al), (bh, bl) = _split(a), _split(b)
    d = lambda u, v: lax.dot_general(u, v, (((ca,), (cb,)), ((), ())), preferred_element_type=F32)
    return d(ah, bh) + (d(ah, bl) + d(al, bh))


@jax.custom_vjp
def mm3(a, b):
    return _dg3(a, b, 1, 0)


@jax.custom_vjp
def mm3_nt(a, b):
    return _dg3(a, b, 1, 1)


@jax.custom_vjp
def mm3_tn(a, b):
    return _dg3(a, b, 0, 0)


mm3.defvjp(lambda a, b: (mm3(a, b), (a, b)), lambda r, g: (mm3_nt(g, r[1]), mm3_tn(r[0], g)))
mm3_nt.defvjp(lambda a, b: (mm3_nt(a, b), (a, b)), lambda r, g: (mm3(g, r[1]), mm3_tn(g, r[0])))
mm3_tn.defvjp(lambda a, b: (mm3_tn(a, b), (a, b)), lambda r, g: (mm3_nt(r[1], g), mm3(r[0], g)))


def _tri_mm(v, upper):
    T = v.shape[0]
    r = lax.broadcasted_iota(jnp.int32, (T, T), 0)
    c = lax.broadcasted_iota(jnp.int32, (T, T), 1)
    m = ((c >= r) if upper else (r >= c)).astype(BF16)
    hi, lo = _split(v)
    d = lambda u: lax.dot_general(m, u, (((1,), (0,)), ((), ())), preferred_element_type=F32)
    return d(hi) + d(lo)


@jax.custom_vjp
def cumsum_rows(v):
    return _tri_mm(v, False)


cumsum_rows.defvjp(lambda v: (_tri_mm(v, False), None), lambda _, g: (_tri_mm(g, True),))


def _rms(x, g):
    return x * lax.rsqrt(jnp.mean(x * x, axis=-1, keepdims=True) + EPS) * g


def _softmax(s):
    e = jnp.exp(s - lax.stop_gradient(jnp.max(s, axis=-1, keepdims=True)))
    return e / jnp.sum(e, axis=-1, keepdims=True)


def _softplus(x):
    return jnp.maximum(x, 0.0) + jnp.log(1.0 + jnp.exp(-jnp.abs(x)))


def _tile(n, cap, align):
    if n <= cap:
        return n
    best = None
    for d in range(align, cap + 1, align):
        if n % d == 0:
            best = d
    assert best is not None, (n, cap, align)
    return best


def matmul(mode, a, b, name, out_dtype=F32, add=None, norm_out=None, norm_bwd=None):
    if mode == "nn":
        (M, K), (K2, N) = a.shape, b.shape
    elif mode == "nt":
        (M, K), (N, K2) = a.shape, b.shape
    else:
        (K, M), (K2, N) = a.shape, b.shape
    assert K == K2, (mode, a.shape, b.shape)
    if mode != "tn" and norm_out is None and norm_bwd is None:
        tm, tn, tk = _tile(M, 1024, 128), _tile(N, 512, 128), _tile(K, 2048, 128)
    else:
        tm, tn, tk = _tile(M, 512, 128), _tile(N, 1536, 128), _tile(K, 2048, 128)
    nk = K // tk
    if mode == "nn":
        a_spec = pl.BlockSpec((tm, tk), lambda i, j, k: (i, k))
        b_spec = pl.BlockSpec((tk, tn), lambda i, j, k: (k, j))
        dn = (((1,), (0,)), ((), ()))
    elif mode == "nt":
        a_spec = pl.BlockSpec((tm, tk), lambda i, j, k: (i, k))
        b_spec = pl.BlockSpec((tn, tk), lambda i, j, k: (j, k))
        dn = (((1,), (1,)), ((), ()))
    else:
        a_spec = pl.BlockSpec((tk, tm), lambda i, j, k: (k, i))
        b_spec = pl.BlockSpec((tk, tn), lambda i, j, k: (k, j))
        dn = (((0,), (0,)), ((), ()))
    o_spec = pl.BlockSpec((tm, tn), lambda i, j, k: (i, j))
    has_add = add is not None
    g_spec = pl.BlockSpec((1, tn), lambda i, j, k: (0, j))
    extra, extra_specs, out_shapes, out_specs = [], [], [jax.ShapeDtypeStruct((M, N), out_dtype)], [o_spec]
    sem = ("parallel", "parallel", "arbitrary")
    if norm_out is not None:
        assert tn == N
        extra, extra_specs = [norm_out], [g_spec]
        out_shapes.append(jax.ShapeDtypeStruct((M, N), BF16))
        out_specs.append(o_spec)
    if norm_bwd is not None:
        assert tn == N
        x_in, g_in, res_in = norm_bwd
        extra, extra_specs = [x_in, g_in, res_in], [o_spec, g_spec, o_spec]
        out_shapes.append(jax.ShapeDtypeStruct((1, N), F32))
        out_specs.append(g_spec)
        sem = ("arbitrary", "arbitrary", "arbitrary")
    n_in = 2 + int(has_add) + len(extra)
    n_out = len(out_shapes)

    def body(*refs):
        a_ref, b_ref = refs[0], refs[1]
        add_ref = refs[2] if has_add else None
        extra_refs = refs[2 + int(has_add):n_in]
        o_ref = refs[n_in]
        acc_ref = refs[-1]
        i = pl.program_id(0)
        p = lax.dot_general(a_ref[...].astype(BF16), b_ref[...].astype(BF16), dn,
                            preferred_element_type=F32)

        def finish(total):
            if has_add:
                total = total + add_ref[...]
            if norm_out is not None:
                o_ref[...] = total.astype(o_ref.dtype)
                refs[n_in + 1][...] = _rms(total, extra_refs[0][...]).astype(BF16)
            elif norm_bwd is not None:
                x_ref, g_ref, res_ref = extra_refs
                _, vjp = jax.vjp(f_norm_res, x_ref[...], g_ref[...])
                dx, dg = vjp((total, res_ref[...]))
                o_ref[...] = dx
                dg_ref = refs[n_in + 1]

                @pl.when(i == 0)
                def _():
                    dg_ref[...] = dg

                @pl.when(i > 0)
                def _():
                    dg_ref[...] += dg
            else:
                o_ref[...] = total.astype(o_ref.dtype)

        if nk == 1:
            finish(p)
        else:
            k = pl.program_id(2)

            @pl.when(k == 0)
            def _():
                acc_ref[...] = p

            @pl.when(k > 0)
            def _():
                acc_ref[...] += p

            @pl.when(k == nk - 1)
            def _():
                finish(acc_ref[...])

    ins = [a, b] + ([add] if has_add else []) + extra
    specs = [a_spec, b_spec] + ([o_spec] if has_add else []) + extra_specs
    out = pl.pallas_call(
        body, name=name, grid=(M // tm, N // tn, nk), in_specs=specs, out_specs=out_specs,
        out_shape=out_shapes, scratch_shapes=[pltpu.VMEM((tm, tn), F32)],
        compiler_params=_cp(*sem),
    )(*ins)
    return out[0] if n_out == 1 else out


def ffn_in(h, wg, wu):
    (S, K), F = h.shape, wg.shape[0]
    tm, tn = _tile(S, 1024, 128), _tile(F, 512, 128)
    dn = (((1,), (1,)), ((), ()))

    def body(h_ref, wg_ref, wu_ref, g_ref, u_ref, act_ref):
        a = h_ref[...].astype(BF16)
        g = lax.dot_general(a, wg_ref[...].astype(BF16), dn, preferred_element_type=F32)
        u = lax.dot_general(a, wu_ref[...].astype(BF16), dn, preferred_element_type=F32)
        g_ref[...] = g
        u_ref[...] = u
        act_ref[...] = f_swiglu(g, u)[0].astype(BF16)

    w_spec = pl.BlockSpec((tn, K), lambda i, j: (j, 0))
    o_spec = pl.BlockSpec((tm, tn), lambda i, j: (i, j))
    sd = jax.ShapeDtypeStruct
    return pl.pallas_call(
        body, name="ffn_in", grid=(S // tm, F // tn),
        in_specs=[pl.BlockSpec((tm, K), lambda i, j: (i, 0)), w_spec, w_spec], out_specs=[o_spec] * 3,
        out_shape=[sd((S, F), F32), sd((S, F), F32), sd((S, F), BF16)],
        compiler_params=_cp("parallel", "parallel"),
    )(h, wg, wu)


def ffn_dact(dy, wd, gate, up):
    (S, K), F = dy.shape, wd.shape[0]
    tm, tn = _tile(S, 1024, 128), _tile(F, 512, 128)
    dn = (((1,), (1,)), ((), ()))

    def body(dy_ref, wd_ref, g_ref, u_ref, dg_ref, du_ref):
        dact = lax.dot_general(dy_ref[...].astype(BF16), wd_ref[...].astype(BF16), dn, preferred_element_type=F32)
        _, vjp = jax.vjp(f_swiglu, g_ref[...], u_ref[...])
        dg, du = vjp((dact,))
        dg_ref[...] = dg.astype(BF16)
        du_ref[...] = du.astype(BF16)

    o_spec = pl.BlockSpec((tm, tn), lambda i, j: (i, j))
    sd = jax.ShapeDtypeStruct
    return pl.pallas_call(
        body, name="ffn_dact", grid=(S // tm, F // tn),
        in_specs=[pl.BlockSpec((tm, K), lambda i, j: (i, 0)), pl.BlockSpec((tn, K), lambda i, j: (j, 0)),
                  o_spec, o_spec],
        out_specs=[o_spec] * 2, out_shape=[sd((S, F), BF16)] * 2,
        compiler_params=_cp("parallel", "parallel"),
    )(dy, wd, gate, up)


def _row_spec(arr, tile):
    if isinstance(arr, tuple):
        a, w, cb = arr[:3]
        ro = (arr[3] // tile) if len(arr) > 3 else 0
        assert len(arr) < 4 or arr[3] % tile == 0
        return a, pl.BlockSpec((tile, w), lambda i, cb=cb, ro=ro: (i + ro, cb)), (tile, w)
    if arr.ndim == 3:
        d0, _, d2 = arr.shape
        return arr, pl.BlockSpec((d0, tile, d2), lambda i: (0, i, 0)), (d0, tile, d2)
    return arr, pl.BlockSpec((tile, arr.shape[1]), lambda i: (i, 0)), (tile, arr.shape[1])


def _full_spec(arr):
    nd = arr.ndim
    return pl.BlockSpec(arr.shape, lambda i, nd=nd: (0,) * nd)


def _n_rows(arr):
    a = arr[0] if isinstance(arr, tuple) else arr
    return a.shape[1] if a.ndim == 3 else a.shape[0]


def _row_out_shape(shape_tail, n, dtype):
    if isinstance(shape_tail, tuple):
        d0, d2 = shape_tail
        return (jax.ShapeDtypeStruct((d0, n, d2), dtype),
                lambda tile: pl.BlockSpec((d0, tile, d2), lambda i: (0, i, 0)))
    return (jax.ShapeDtypeStruct((n, shape_tail), dtype),
            lambda tile: pl.BlockSpec((tile, shape_tail), lambda i: (i, 0)))


def _f32(v):
    return v.astype(F32) if v.dtype == BF16 else v


def row_fwd(name, fn, rows, fulls, outs, tile=ROW_TILE):
    n = _n_rows(rows[0])
    tile = min(tile, n)
    assert n % tile == 0, (name, n, tile)
    rs = [_row_spec(r, tile) for r in rows]
    os_ = [_row_out_shape(w, n, dt) for w, dt in outs]
    nr, nf = len(rows), len(fulls)

    def body(*refs):
        vals = [_f32(r[...]) for r in refs[:nr + nf]]
        res = fn(*vals)
        for r, v in zip(refs[nr + nf:], res):
            r[...] = v.astype(r.dtype)

    out = pl.pallas_call(
        body, name=name, grid=(n // tile,),
        in_specs=[s for _, s, _ in rs] + [_full_spec(f) for f in fulls],
        out_specs=[mk(tile) for _, mk in os_], out_shape=[sh for sh, _ in os_],
        compiler_params=_cp("parallel"),
    )(*[a for a, _, _ in rs], *fulls)
    return out


def row_bwd(name, fn, rows, fulls, cts, want_rows, want_fulls, row_dtypes=None, tile=ROW_TILE):
    n = _n_rows(rows[0])
    tile = min(tile, n)
    assert n % tile == 0, (name, n, tile)
    rs = [_row_spec(r, tile) for r in rows]
    cs = [_row_spec(c, tile) for c in cts]
    nr, nf, nc = len(rows), len(fulls), len(cts)
    row_dtypes = row_dtypes or [F32] * len(want_rows)
    out_shapes, out_specs = [], []
    for k, idx in enumerate(want_rows):
        a, _, blk = rs[idx]
        if len(blk) == 3:
            sh, mk = _row_out_shape((blk[0], blk[2]), n, row_dtypes[k])
        else:
            sh, mk = _row_out_shape(blk[1], n, row_dtypes[k])
        out_shapes.append(sh)
        out_specs.append(mk(tile))
    for idx in want_fulls:
        out_shapes.append(jax.ShapeDtypeStruct(fulls[idx].shape, F32))
        out_specs.append(_full_spec(fulls[idx]))
    n_wr = len(want_rows)

    def body(*refs):
        i = pl.program_id(0)
        vals = [_f32(r[...]) for r in refs[:nr + nf]]
        ct_vals = [_f32(r[...]) for r in refs[nr + nf:nr + nf + nc]]
        outs = refs[nr + nf + nc:]
        _, vjp = jax.vjp(fn, *vals)
        grads = vjp(tuple(ct_vals))
        for k, idx in enumerate(want_rows):
            outs[k][...] = grads[idx].astype(outs[k].dtype)
        for k, idx in enumerate(want_fulls):
            o = outs[n_wr + k]
            g = grads[nr + idx]

            @pl.when(i == 0)
            def _(o=o, g=g):
                o[...] = g

            @pl.when(i > 0)
            def _(o=o, g=g):
                o[...] += g

    out = pl.pallas_call(
        body, name=name, grid=(n // tile,),
        in_specs=[s for _, s, _ in rs] + [_full_spec(f) for f in fulls] + [s for _, s, _ in cs],
        out_specs=out_specs, out_shape=out_shapes,
        compiler_params=_cp("arbitrary"),
    )(*[a for a, _, _ in rs], *fulls, *[a for a, _, _ in cs])
    return out


def whole(name, fn, args, outs):
    def body(*refs):
        res = fn(*[r[...] for r in refs[:len(args)]])
        for r, v in zip(refs[len(args):], res):
            r[...] = v.astype(r.dtype)

    return pl.pallas_call(
        body, name=name, out_shape=[jax.ShapeDtypeStruct(s, d) for s, d in outs],
        compiler_params=pltpu.CompilerParams(vmem_limit_bytes=VMEM_LIMIT),
    )(*args)


def f_norm(x, g):
    return (_rms(x, g),)


def f_norm_res(x, g):
    return _rms(x, g), x


def f_swiglu(g, u):
    return (g * jax.nn.sigmoid(g) * u,)


def f_glu(y, w, b):
    h = jax.nn.gelu(y)
    return (h * jax.nn.sigmoid(mm(h, w) + b),)


def f_xattn(q, kv, qg, kg):
    outs = []
    for h in range(XA_H):
        sl = slice(h * XA_D, (h + 1) * XA_D)
        qn = _rms(q[:, sl], qg)
        kn = _rms(kv[:, sl], kg)
        vh = kv[:, D + h * XA_D:D + (h + 1) * XA_D]
        p = _softmax(mm_nt(qn, kn) * (XA_D ** -0.5))
        outs.append(mm(p, vh))
    return (jnp.concatenate(outs, axis=-1),)


def f_adam(w, g, m, v):
    m2 = B1 * m + (1.0 - B1) * g
    v2 = B2 * v + (1.0 - B2) * (g * g)
    m_hat = m2 / (1.0 - B1 ** STEP)
    v_hat = v2 / (1.0 - B2 ** STEP)
    delta = -LR * (m_hat / (jnp.sqrt(v_hat) + AEPS) + WD * w)
    return delta, m2, v2


S5_NTAB, S5_NROW = 4, 6


def _s5_tables(are, aim, ldt):
    T = CHUNK
    dt = jnp.exp(ldt)
    ar, ai = are * dt, aim * dt
    t = lax.broadcasted_iota(jnp.int32, (T, 1), 0).astype(F32)
    mag, inv = jnp.exp(t * ar), jnp.exp(-t * ar)
    cs, sn = jnp.cos(t * ai), jnp.sin(t * ai)
    e_re, e_im = mag * cs, mag * sn
    n_re, n_im = inv * cs, -inv * sn
    l_re, l_im = jnp.exp(ar) * jnp.cos(ai), jnp.exp(ar) * jnp.sin(ai)
    den = are * are + aim * aim
    k_re = ((l_re - 1.0) * are + l_im * aim) / den
    k_im = (l_im * are - (l_re - 1.0) * aim) / den
    tl = float(T - 1)
    m_re, m_im = jnp.exp(tl * ar) * jnp.cos(tl * ai), jnp.exp(tl * ar) * jnp.sin(tl * ai)
    return (e_re, e_im, n_re, n_im), (l_re, l_im, k_re, k_im, m_re, m_im)


def _s5_chunk(u, sre, sim, tabs, rows, b_re, b_im, c_re, c_im, dv):
    e_re, e_im, n_re, n_im = tabs
    l_re, l_im, k_re, k_im, m_re, m_im = rows
    x_re, x_im = mm(u, b_re), mm(u, b_im)
    bu_re = k_re * x_re - k_im * x_im
    bu_im = k_re * x_im + k_im * x_re
    v_re = bu_re * n_re - bu_im * n_im
    v_im = bu_re * n_im + bu_im * n_re
    p_re = l_re * sre - l_im * sim
    p_im = l_re * sim + l_im * sre
    w_re = cumsum_rows(v_re) + p_re
    w_im = cumsum_rows(v_im) + p_im
    s_re = e_re * w_re - e_im * w_im
    s_im = e_re * w_im + e_im * w_re
    y = mm(s_re, c_re) - mm(s_im, c_im) + dv * u
    z_re = jnp.sum(v_re, axis=0, keepdims=True) + p_re
    z_im = jnp.sum(v_im, axis=0, keepdims=True) + p_im
    return y, m_re * z_re - m_im * z_im, m_re * z_im + m_im * z_re


def _s5_fill_tables(are_ref, aim_ref, ldt_ref, tab, row):
    for g in range(S5_GB):
        ls = slice(512 * g, 512 * (g + 1))
        tabs, rows = _s5_tables(are_ref[:, ls], aim_ref[:, ls], ldt_ref[:, ls])
        for k, t in enumerate(tabs):
            tab[k, :, ls] = t
        for k, r in enumerate(rows):
            row[k:k + 1, ls] = r


def _s5_read_tables(tab, row, ls):
    return (tuple(tab[k, :, ls] for k in range(S5_NTAB)), tuple(row[k:k + 1, ls] for k in range(S5_NROW)))


def _s5_specs(nc, rev):
    T = CHUNK

    def ci(c):
        return nc - 1 - c if rev else c

    u_spec = pl.BlockSpec((T, S5_W), lambda c: (ci(c), 0))
    p_spec = pl.BlockSpec((1, S5_G * S5_P), lambda c: (0, 0))
    b_spec = pl.BlockSpec((S5_GB, 128, 512), lambda c: (0, 0, 0))
    c_spec = pl.BlockSpec((S5_GB, 512, 128), lambda c: (0, 0, 0))
    d_spec = pl.BlockSpec((1, S5_W), lambda c: (0, 0))
    st_spec = pl.BlockSpec((None, 2, S5_G * S5_P), lambda c: (ci(c), 0, 0))
    return u_spec, p_spec, b_spec, c_spec, d_spec, st_spec


def s5_fwd(pm, are, aim, ldt, b_re, b_im, c_re, c_im, dv):
    S = pm.shape[0]
    nc = S // CHUNK
    u_spec, p_spec, b_spec, c_spec, d_spec, st_spec = _s5_specs(nc, False)

    def body(u_ref, are_ref, aim_ref, ldt_ref, bre_ref, bim_ref, cre_ref, cim_ref, dv_ref,
             y_ref, st_ref, state, tab, row):
        c = pl.program_id(0)

        @pl.when(c == 0)
        def _():
            state[...] = jnp.zeros_like(state)
            _s5_fill_tables(are_ref, aim_ref, ldt_ref, tab, row)

        st_ref[...] = state[...]
        for g in range(S5_GB):
            lu, ls = slice(128 * g, 128 * (g + 1)), slice(512 * g, 512 * (g + 1))
            tabs, rows = _s5_read_tables(tab, row, ls)
            y, e_re, e_im = _s5_chunk(u_ref[:, lu], state[0:1, ls], state[1:2, ls], tabs, rows,
                                      bre_ref[g], bim_ref[g], cre_ref[g], cim_ref[g], dv_ref[:, lu])
            y_ref[:, lu] = y
            state[0:1, ls] = e_re
            state[1:2, ls] = e_im

    n_state = S5_G * S5_P
    return pl.pallas_call(
        body, name="s5_fwd", grid=(nc,),
        in_specs=[u_spec, p_spec, p_spec, p_spec, b_spec, b_spec, c_spec, c_spec, d_spec],
        out_specs=[u_spec, st_spec],
        out_shape=[jax.ShapeDtypeStruct((S, S5_W), F32), jax.ShapeDtypeStruct((nc, 2, n_state), F32)],
        scratch_shapes=[pltpu.VMEM((2, n_state), F32), pltpu.VMEM((S5_NTAB, CHUNK, n_state), F32),
                        pltpu.VMEM((8, n_state), F32)],
        compiler_params=_cp("arbitrary"),
    )(pm, are, aim, ldt, b_re, b_im, c_re, c_im, dv)


def s5_bwd(pm, st, dy, are, aim, ldt, b_re, b_im, c_re, c_im, dv):
    S = pm.shape[0]
    nc = S // CHUNK
    u_spec, p_spec, b_spec, c_spec, d_spec, st_spec = _s5_specs(nc, True)

    def body(u_ref, st_ref, dy_ref, are_ref, aim_ref, ldt_ref, bre_ref, bim_ref, cre_ref, cim_ref, dv_ref,
             du_ref, dare_ref, daim_ref, dldt_ref, dbre_ref, dbim_ref, dcre_ref, dcim_ref, ddv_ref,
             dstate, tab, row, dtab, drow):
        c = pl.program_id(0)

        @pl.when(c == 0)
        def _():
            dstate[...] = jnp.zeros_like(dstate)
            dtab[...] = jnp.zeros_like(dtab)
            drow[...] = jnp.zeros_like(drow)
            _s5_fill_tables(are_ref, aim_ref, ldt_ref, tab, row)

        for g in range(S5_GB):
            lu, ls = slice(128 * g, 128 * (g + 1)), slice(512 * g, 512 * (g + 1))
            every = slice(None)
            tabs, rows = _s5_read_tables(tab, row, ls)
            args = (u_ref[:, lu], st_ref[0:1, ls], st_ref[1:2, ls], tabs, rows,
                    bre_ref[g], bim_ref[g], cre_ref[g], cim_ref[g], dv_ref[:, lu])
            _, vjp = jax.vjp(_s5_chunk, *args)
            gr = vjp((dy_ref[:, lu], dstate[0:1, ls], dstate[1:2, ls]))
            du_ref[:, lu] = gr[0]
            dstate[0:1, ls] = gr[1]
            dstate[1:2, ls] = gr[2]
            for k, t in enumerate(gr[3]):
                dtab[k, :, ls] += t
            for k, r in enumerate(gr[4]):
                drow[k:k + 1, ls] += r
            accs = ((dbre_ref, (g,)), (dbim_ref, (g,)), (dcre_ref, (g,)), (dcim_ref, (g,)), (ddv_ref, (every, lu)))
            for (o, idx), gv in zip(accs, gr[5:]):
                @pl.when(c == 0)
                def _(o=o, idx=idx, gv=gv):
                    o[idx] = gv

                @pl.when(c > 0)
                def _(o=o, idx=idx, gv=gv):
                    o[idx] += gv

        @pl.when(c == nc - 1)
        def _():
            for g in range(S5_GB):
                ls = slice(512 * g, 512 * (g + 1))
                _, vjp = jax.vjp(_s5_tables, are_ref[:, ls], aim_ref[:, ls], ldt_ref[:, ls])
                dtabs, drows = _s5_read_tables(dtab, drow, ls)
                ga, gi, gl = vjp((dtabs, drows))
                dare_ref[:, ls] = ga
                daim_ref[:, ls] = gi
                dldt_ref[:, ls] = gl

    n_state = S5_G * S5_P
    return pl.pallas_call(
        body, name="s5_bwd", grid=(nc,),
        in_specs=[u_spec, st_spec, u_spec, p_spec, p_spec, p_spec, b_spec, b_spec, c_spec, c_spec, d_spec],
        out_specs=[u_spec, p_spec, p_spec, p_spec, b_spec, b_spec, c_spec, c_spec, d_spec],
        out_shape=[jax.ShapeDtypeStruct((S, S5_W), F32)] + [jax.ShapeDtypeStruct((1, n_state), F32)] * 3
        + [jax.ShapeDtypeStruct((S5_GB, 128, 512), F32)] * 2 + [jax.ShapeDtypeStruct((S5_GB, 512, 128), F32)] * 2
        + [jax.ShapeDtypeStruct((1, S5_W), F32)],
        scratch_shapes=[pltpu.VMEM((2, n_state), F32), pltpu.VMEM((S5_NTAB, CHUNK, n_state), F32),
                        pltpu.VMEM((8, n_state), F32), pltpu.VMEM((S5_NTAB, CHUNK, n_state), F32),
                        pltpu.VMEM((8, n_state), F32)],
        compiler_params=_cp("arbitrary"),
    )(pm, st, dy, are, aim, ldt, b_re, b_im, c_re, c_im, dv)


def conv_fwd(pm, w):
    S = pm.shape[0]

    def body(x_ref, w_ref, o_ref, pad):
        x = x_ref[...]
        pad[0:8, :] = jnp.zeros((8, 128), F32)
        pad[8:, :] = x
        y = (w_ref[3:4, :] * x + w_ref[2:3, :] * pad[7:7 + S, :] + w_ref[1:2, :] * pad[6:6 + S, :]
             + w_ref[0:1, :] * pad[5:5 + S, :])
        o_ref[...] = y * jax.nn.sigmoid(y)

    return pl.pallas_call(
        body, name="conv_fwd", grid=(12,),
        in_specs=[pl.BlockSpec((S, 128), lambda j: (0, 4 + j)), pl.BlockSpec((4, 128), lambda j: (0, j))],
        out_specs=pl.BlockSpec((S, 128), lambda j: (0, j)),
        out_shape=jax.ShapeDtypeStruct((S, 1536), F32),
        scratch_shapes=[pltpu.VMEM((S + 8, 128), F32)],
        compiler_params=_cp("parallel"),
    )(pm, w)


def conv_bwd(pm, w, dout):
    S = pm.shape[0]

    def body(x_ref, w_ref, do_ref, dx_ref, dw_ref, pad, dpad):
        x = x_ref[...]
        pad[0:8, :] = jnp.zeros((8, 128), F32)
        pad[8:, :] = x
        xs = [pad[5:5 + S, :], pad[6:6 + S, :], pad[7:7 + S, :], x]
        y = w_ref[0:1, :] * xs[0] + w_ref[1:2, :] * xs[1] + w_ref[2:3, :] * xs[2] + w_ref[3:4, :] * xs[3]
        sg = jax.nn.sigmoid(y)
        dy = do_ref[...] * (sg + y * sg * (1.0 - sg))
        dpad[0:S, :] = dy
        dpad[S:, :] = jnp.zeros((8, 128), F32)
        dx_ref[...] = (w_ref[3:4, :] * dy + w_ref[2:3, :] * dpad[1:1 + S, :] + w_ref[1:2, :] * dpad[2:2 + S, :]
                       + w_ref[0:1, :] * dpad[3:3 + S, :])
        for i in range(4):
            dw_ref[i:i + 1, :] = jnp.sum(dy * xs[i], axis=0, keepdims=True)

    return pl.pallas_call(
        body, name="conv_bwd", grid=(12,),
        in_specs=[pl.BlockSpec((S, 128), lambda j: (0, 4 + j)), pl.BlockSpec((4, 128), lambda j: (0, j)),
                  pl.BlockSpec((S, 128), lambda j: (0, j))],
        out_specs=[pl.BlockSpec((S, 128), lambda j: (0, j)), pl.BlockSpec((4, 128), lambda j: (0, j))],
        out_shape=[jax.ShapeDtypeStruct((S, 1536), F32), jax.ShapeDtypeStruct((4, 1536), F32)],
        scratch_shapes=[pltpu.VMEM((S + 8, 128), F32), pltpu.VMEM((S + 8, 128), F32)],
        compiler_params=_cp("parallel"),
    )(pm, w, dout)


GDN_SUP = 4
GDN_ROWS = GDN_SUP * CHUNK


def _gdn_chunk(q, k, v, gate, al, bl, alog, dtb, og, state):
    R = q.shape[0]
    r = lax.broadcasted_iota(jnp.int32, (R, R), 0)
    c = lax.broadcasted_iota(jnp.int32, (R, R), 1)
    same = (r // CHUNK) == (c // CHUNK)
    eye = (r == c).astype(F32)
    strict, causal, upper = same & (r > c), same & (r >= c), same & (r <= c)
    qn = q * lax.rsqrt(jnp.sum(q * q, axis=-1, keepdims=True) + EPS) * (GDN_D ** -0.5)
    kn = k * lax.rsqrt(jnp.sum(k * k, axis=-1, keepdims=True) + EPS)
    beta = jax.nn.sigmoid(bl)
    g = -jnp.exp(alog) * _softplus(al + dtb)
    g_row = jnp.sum(eye * g, axis=0, keepdims=True)
    gc_col = jnp.sum(jnp.where(causal, g_row, 0.0), axis=1, keepdims=True)
    gc_row = jnp.sum(jnp.where(upper, g, 0.0), axis=0, keepdims=True)
    gtot = jnp.sum(jnp.where(same, g_row, 0.0), axis=1, keepdims=True)
    gamma = jnp.exp(gc_col)
    diff = gc_col - gc_row
    d_strict = jnp.where(strict, jnp.exp(jnp.where(strict, diff, 0.0)), 0.0)
    d_causal = jnp.where(causal, jnp.exp(jnp.where(causal, diff, 0.0)), 0.0)
    a = beta * mm_nt(kn, kn) * d_strict
    p = -a
    x = eye + p
    for _ in range(5):
        p = mm(p, p)
        x = x + mm(x, p)
    u_new = mm(x, beta * v)
    w_k = mm(x, (beta * gamma) * kn)
    qk = mm_nt(qn, kn) * d_causal
    q_g = qn * gamma
    k_tail = kn * jnp.exp(gtot - gc_col)
    ws, os_ = [], []
    for i in range(R // CHUNK):
        rows = slice(CHUNK * i, CHUNK * (i + 1))
        w_i = u_new[rows] - mm(w_k[rows], state)
        os_.append(mm(q_g[rows], state))
        decay = jnp.exp(jnp.sum(g[rows], axis=0, keepdims=True))
        state = decay * state + mm_tn(k_tail[rows], w_i)
        ws.append(w_i)
    o = jnp.concatenate(os_, axis=0) + mm(qk, jnp.concatenate(ws, axis=0))
    out = _rms(o, og) * (gate * jax.nn.sigmoid(gate))
    return out, state


def _gdn_specs(nc, rev):
    def ci(c):
        return nc - 1 - c if rev else c

    def blk(cb):
        return pl.BlockSpec((GDN_ROWS, 512), lambda c: (ci(c), cb))

    col = lambda n: pl.BlockSpec((n, GDN_ROWS, 1), lambda c: (0, ci(c), 0))
    sc = pl.BlockSpec((GDN_H, 1, 1), lambda c: (0, 0, 0))
    og = pl.BlockSpec((1, 128), lambda c: (0, 0))
    st = pl.BlockSpec((GDN_H, None, 128, 128), lambda c: (0, ci(c), 0, 0))
    return blk, col, sc, og, st


def gdn_fwd(qkvc, pm, abt, alog, dtb, og):
    S = qkvc.shape[0]
    nc = S // GDN_ROWS
    blk, col, sc, ogs, st = _gdn_specs(nc, False)

    def body(q_ref, k_ref, v_ref, gate_ref, ab_ref, alog_ref, dtb_ref, og_ref, o_ref, st_ref, state):
        c = pl.program_id(0)

        @pl.when(c == 0)
        def _():
            state[...] = jnp.zeros_like(state)

        st_ref[...] = state[...]
        for h in range(GDN_H):
            sl = slice(GDN_D * h, GDN_D * (h + 1))
            out, new_state = _gdn_chunk(q_ref[:, sl], k_ref[:, sl], v_ref[:, sl], gate_ref[:, sl], ab_ref[h],
                                        ab_ref[GDN_H + h], alog_ref[h], dtb_ref[h], og_ref[...], state[h])
            o_ref[:, sl] = out
            state[h] = new_state

    return pl.pallas_call(
        body, name="gdn_fwd", grid=(nc,),
        in_specs=[blk(0), blk(1), blk(2), blk(4), col(2 * GDN_H), sc, sc, ogs],
        out_specs=[blk(0), st],
        out_shape=[jax.ShapeDtypeStruct((S, 512), F32), jax.ShapeDtypeStruct((GDN_H, nc, 128, 128), F32)],
        scratch_shapes=[pltpu.VMEM((GDN_H, 128, 128), F32)],
        compiler_params=_cp("arbitrary"),
    )(qkvc, qkvc, qkvc, pm, abt, alog, dtb, og)


def gdn_bwd(qkvc, pm, abt, alog, dtb, og, st, dout):
    S = qkvc.shape[0]
    nc = S // GDN_ROWS
    blk, col, sc, ogs, sts = _gdn_specs(nc, True)

    def body(q_ref, k_ref, v_ref, gate_ref, ab_ref, alog_ref, dtb_ref, og_ref, st_ref, do_ref,
             dq_ref, dk_ref, dv_ref, dgate_ref, dal_ref, dbl_ref, dalog_ref, ddtb_ref, dog_ref, dstate):
        c = pl.program_id(0)

        @pl.when(c == 0)
        def _():
            dstate[...] = jnp.zeros_like(dstate)

        for h in range(GDN_H):
            sl = slice(GDN_D * h, GDN_D * (h + 1))
            args = (q_ref[:, sl], k_ref[:, sl], v_ref[:, sl], gate_ref[:, sl], ab_ref[h], ab_ref[GDN_H + h],
                    alog_ref[h], dtb_ref[h], og_ref[...], st_ref[h])
            _, vjp = jax.vjp(_gdn_chunk, *args)
            g = vjp((do_ref[:, sl], dstate[h]))
            for o, gv in zip((dq_ref, dk_ref, dv_ref, dgate_ref), g[:4]):
                o[:, sl] = gv
            dal_ref[h] = g[4]
            dbl_ref[h] = g[5]
            dstate[h] = g[9]
            for o, gv in zip((dalog_ref, ddtb_ref, dog_ref), g[6:9]):
                @pl.when(c == 0)
                def _(o=o, gv=gv, h=h):
                    o[h] = gv

                @pl.when(c > 0)
                def _(o=o, gv=gv, h=h):
                    o[h] += gv

    ogo = pl.BlockSpec((GDN_H, 1, 128), lambda c: (0, 0, 0))
    sd = jax.ShapeDtypeStruct
    return pl.pallas_call(
        body, name="gdn_bwd", grid=(nc,),
        in_specs=[blk(0), blk(1), blk(2), blk(4), col(2 * GDN_H), sc, sc, ogs, sts, blk(0)],
        out_specs=[blk(0), blk(0), blk(0), blk(0), col(GDN_H), col(GDN_H), sc, sc, ogo],
        out_shape=[sd((S, 512), F32)] * 4 + [sd((GDN_H, S, 1), F32)] * 2 + [sd((GDN_H, 1, 1), F32)] * 2
        + [sd((GDN_H, 1, 128), F32)],
        scratch_shapes=[pltpu.VMEM((GDN_H, 128, 128), F32)],
        compiler_params=_cp("arbitrary"),
    )(qkvc, qkvc, qkvc, pm, abt, alog, dtb, og, st, dout)


HG = 8
HG_LANES = HG * CA_D


def _group_mean_raw(y):
    r = lax.broadcasted_iota(jnp.int32, (128, 128), 0)
    c = lax.broadcasted_iota(jnp.int32, (128, 128), 1)
    g = jnp.where((r // CA_D) == (c // CA_D), 1.0 / CA_D, 0.0).astype(BF16)
    d = lambda u: lax.dot_general(u, g, (((1,), (0,)), ((), ())), preferred_element_type=F32)
    outs = []
    for j in range(y.shape[1] // 128):
        hi, lo = _split(y[:, 128 * j:128 * (j + 1)])
        outs.append(d(hi) + d(lo))
    return jnp.concatenate(outs, axis=1)


@jax.custom_vjp
def group_mean(y):
    return _group_mean_raw(y)


group_mean.defvjp(lambda y: (_group_mean_raw(y), None), lambda _, g: (_group_mean_raw(g),))


def f_headnorm(t, g):
    return (t * lax.rsqrt(group_mean(t * t) + EPS) * g,)


def _cattn_chunk(q, kb, vb, bias, valid):
    lane = lax.broadcasted_iota(jnp.int32, (1, 128), 1)
    m0 = (lane < CA_D).astype(F32)
    m1 = 1.0 - m0
    pairs = range(HG // 2)
    sl = [slice(128 * p, 128 * (p + 1)) for p in pairs]
    q2 = [jnp.concatenate([q[:, s] * m0, q[:, s] * m1], axis=0) for s in sl]
    sc = [mm_nt(q2[p], kb[:, sl[p]]) * (CA_D ** -0.5) + bias[sl[p]] for p in pairs]
    pr = [_softmax(jnp.where(valid, s, -1e30)) for s in sc]
    o2 = [mm(pr[p], vb[:, sl[p]]) for p in pairs]
    return jnp.concatenate([o[:CHUNK] * m0 + o[CHUNK:] * m1 for o in o2], axis=1)


def _cattn_valid(c):
    pos = lax.broadcasted_iota(jnp.int32, (1, CA_BAND), 1) + c * CHUNK
    return pos >= CA_PAD


def _cattn_specs(S):
    q_spec = pl.BlockSpec((CHUNK, HG_LANES), lambda h, c: (c, h))
    kv_spec = pl.BlockSpec((S + CA_PAD, HG_LANES), lambda h, c: (0, h))
    b_spec = pl.BlockSpec((HG * CHUNK, CA_BAND), lambda h, c: (h, 0))
    return q_spec, kv_spec, b_spec


def cattn_fwd(qn, kp, vp, bias):
    S = qn.shape[0]
    nc = S // CHUNK
    q_spec, kv_spec, b_spec = _cattn_specs(S)

    def body(q_ref, k_ref, v_ref, b_ref, o_ref):
        c = pl.program_id(1)
        start = pl.multiple_of(c * CHUNK, CHUNK)
        kb = k_ref[pl.ds(start, CA_BAND), :]
        vb = v_ref[pl.ds(start, CA_BAND), :]
        o_ref[...] = _cattn_chunk(q_ref[...], kb, vb, b_ref[...], _cattn_valid(c)).astype(o_ref.dtype)

    return pl.pallas_call(
        body, name="cattn_fwd", grid=(CA_H // HG, nc), in_specs=[q_spec, kv_spec, kv_spec, b_spec],
        out_specs=q_spec, out_shape=jax.ShapeDtypeStruct((S, D), BF16),
        compiler_params=_cp("parallel", "arbitrary"),
    )(qn, kp, vp, bias)


def kv_prep(qkv, kg):
    S = qkv.shape[0]
    tile = ROW_TILE
    lead = CA_PAD // tile

    def body(k_ref, v_ref, g_ref, kp_ref, vp_ref):
        i = pl.program_id(0)

        @pl.when(i < lead)
        def _():
            kp_ref[...] = jnp.zeros_like(kp_ref)
            vp_ref[...] = jnp.zeros_like(vp_ref)

        @pl.when(i >= lead)
        def _():
            kp_ref[...] = f_headnorm(k_ref[...], g_ref[...])[0].astype(BF16)
            vp_ref[...] = v_ref[...].astype(BF16)

    src = lambda cb: pl.BlockSpec((tile, D), lambda i, cb=cb: (jnp.maximum(i - lead, 0), cb))
    out = pl.BlockSpec((tile, D), lambda i: (i, 0))
    return pl.pallas_call(
        body, name="kv_prep", grid=((S + CA_PAD) // tile,),
        in_specs=[src(1), src(2), pl.BlockSpec((1, D), lambda i: (0, 0))], out_specs=[out, out],
        out_shape=[jax.ShapeDtypeStruct((S + CA_PAD, D), BF16)] * 2,
        compiler_params=_cp("parallel"),
    )(qkv, qkv, kg)


def cattn_bwd(qn, kp, vp, bias, do):
    S = qn.shape[0]
    nc = S // CHUNK
    q_spec, kv_spec, b_spec = _cattn_specs(S)

    def body(q_ref, k_ref, v_ref, b_ref, do_ref, dq_ref, dk_ref, dv_ref, db_ref):
        c = pl.program_id(1)

        @pl.when(c == 0)
        def _():
            dk_ref[...] = jnp.zeros_like(dk_ref)
            dv_ref[...] = jnp.zeros_like(dv_ref)
            db_ref[...] = jnp.zeros_like(db_ref)

        start = pl.multiple_of(c * CHUNK, CHUNK)
        kb = k_ref[pl.ds(start, CA_BAND), :].astype(F32)
        vb = v_ref[pl.ds(start, CA_BAND), :].astype(F32)
        valid = _cattn_valid(c)
        _, vjp = jax.vjp(lambda q, k, v, b: _cattn_chunk(q, k, v, b, valid), q_ref[...], kb, vb, b_ref[...])
        dq, dk, dv, db = vjp(do_ref[...])
        dq_ref[...] = dq
        dk_ref[pl.ds(start, CA_BAND), :] += dk
        dv_ref[pl.ds(start, CA_BAND), :] += dv
        db_ref[...] += db

    sd = jax.ShapeDtypeStruct
    return pl.pallas_call(
        body, name="cattn_bwd", grid=(CA_H // HG, nc), in_specs=[q_spec, kv_spec, kv_spec, b_spec, q_spec],
        out_specs=[q_spec, kv_spec, kv_spec, b_spec],
        out_shape=[sd((S, D), F32), sd((S + CA_PAD, D), F32), sd((S + CA_PAD, D), F32),
                   sd((CA_H * CHUNK, CA_BAND), F32)],
        compiler_params=_cp("parallel", "arbitrary"),
    )(qn, kp, vp, bias, do)


_REL_IDX = np.clip(np.arange(CHUNK)[:, None] - np.arange(CA_BAND)[None, :] + CA_PAD, -MAX_REL, MAX_REL) + MAX_REL
SKEW_W = CA_BAND + CHUNK


def rel_bias_grad(dbias):
    padded = jnp.pad(dbias, ((0, 0), (0, 0), (CHUNK, 0)))
    flat = jnp.pad(padded.reshape(CA_H, CHUNK * SKEW_W), ((0, 0), (0, CHUNK)))
    skew = flat.reshape(CA_H, CHUNK, SKEW_W + 1)

    first_near = SKEW_W - CHUNK - MAX_REL

    def fn(t):
        colsum = jnp.sum(t, axis=1, keepdims=True)
        j = lax.broadcasted_iota(jnp.int32, colsum.shape, 2)
        far = jnp.sum(jnp.where(j < first_near, colsum, 0.0), axis=2, keepdims=True)
        return (colsum + jnp.where(j == first_near, far, 0.0),)

    (colsum,) = whole("relbias_sum", fn, [skew], [((CA_H, 1, SKEW_W + 1), F32)])
    near = colsum[:, 0, first_near:SKEW_W][:, ::-1]
    return jnp.concatenate([jnp.zeros((CA_H, CHUNK + 1), F32), near], axis=1)


def rel_bias_expand(rb):
    near = rb[:, CHUNK + 1:][:, ::-1]
    far = jnp.broadcast_to(rb[:, 2 * MAX_REL:], (CA_H, SKEW_W - CHUNK - MAX_REL))
    t = jnp.concatenate([far, near, jnp.zeros((CA_H, 1), rb.dtype)], axis=1)
    rows = jnp.tile(t, (1, CHUNK))[:, :CHUNK * SKEW_W].reshape(CA_H, CHUNK, SKEW_W)
    return rows[:, :, CHUNK:]


def loss_head(y, target):
    S = y.shape[0]
    tile = min(ROW_TILE, S)

    def body(y_ref, t_ref, dy_ref, acc_ref):
        i = pl.program_id(0)
        e = y_ref[...] - t_ref[...]
        dy_ref[...] = e * (1.0 / D)
        part = jnp.sum(e * e, axis=0, keepdims=True) * (0.5 / D)

        @pl.when(i == 0)
        def _():
            acc_ref[...] = part

        @pl.when(i > 0)
        def _():
            acc_ref[...] += part

    row = pl.BlockSpec((tile, D), lambda i: (i, 0))
    return pl.pallas_call(
        body, name="loss_head", grid=(S // tile,), in_specs=[row, row],
        out_specs=[row, pl.BlockSpec((1, D), lambda i: (0, 0))],
        out_shape=[jax.ShapeDtypeStruct((S, D), F32), jax.ShapeDtypeStruct((1, D), F32)],
        compiler_params=_cp("arbitrary"),
    )(y, target)


ANY = pl.BlockSpec(memory_space=pl.ANY)


HBM = pl.BlockSpec(memory_space=pltpu.HBM)
SEM = pl.BlockSpec(memory_space=pltpu.SEMAPHORE)
EFFECT = pltpu.SideEffectType.DATAFLOW_SIDE_EFFECTING


def _chip_copies(src_ref, land_ref, send_sems, recv_sems, per_dest):
    x, y, c = lax.axis_index("x"), lax.axis_index("y"), lax.axis_index("c")
    me = 2 * x + y
    if per_dest == "sibling":
        cp = pltpu.make_async_remote_copy(src_ref=src_ref, dst_ref=land_ref, send_sem=send_sems.at[0],
                                          recv_sem=recv_sems.at[0], device_id=(x, y, 1 - c), device_id_type=MESH)
        return [(cp, cp)]
    out = []
    for j, (px, py) in enumerate([(1 - x, y), (x, 1 - y), (1 - x, 1 - y)]):
        peer = 2 * px + py
        if per_dest:
            send = (src_ref.at[peer], land_ref.at[j])
            recv = (src_ref.at[me], land_ref.at[j])
        else:
            send = (src_ref, land_ref.at[me])
            recv = (src_ref, land_ref.at[peer])
        mk = lambda s, d, j=j, px=px, py=py: pltpu.make_async_remote_copy(
            src_ref=s, dst_ref=d, send_sem=send_sems.at[j], recv_sem=recv_sems.at[j],
            device_id=(px, py, c), device_id_type=MESH)
        out.append((mk(*send), mk(*recv)))
    return out


def exchange_start(src, land, carry, name, per_dest):
    def body(src_ref, land_ref, carry_ref, send_sems, recv_sems, src_out, land_out, carry_out):
        for send, _ in _chip_copies(src_ref, land_ref, send_sems, recv_sems, per_dest):
            send.start()

    hbm = lambda a: pltpu.HBM(a.shape, a.dtype)
    n = 1 if per_dest == "sibling" else 3
    return pl.pallas_call(
        body, name=name,
        out_shape=(pltpu.SemaphoreType.DMA((n,)), pltpu.SemaphoreType.DMA((n,)), hbm(src), hbm(land), hbm(carry)),
        in_specs=(HBM, HBM, HBM), out_specs=(SEM, SEM, HBM, HBM, HBM),
        input_output_aliases={0: 2, 1: 3, 2: 4},
        compiler_params=pltpu.CompilerParams(has_side_effects=EFFECT),
    )(pltpu.with_memory_space_constraint(src, pltpu.HBM), pltpu.with_memory_space_constraint(land, pltpu.HBM),
      pltpu.with_memory_space_constraint(carry, pltpu.HBM))


def exchange_wait(src, land, send_sems, recv_sems, after, name, per_dest):
    def body(src_ref, land_ref, send_sems_ref, recv_sems_ref, after_ref, src_out, land_out):
        for send, recv in _chip_copies(src_ref, land_ref, send_sems_ref, recv_sems_ref, per_dest):
            send.wait_send()
            recv.wait_recv()

    hbm = lambda a: pltpu.HBM(a.shape, a.dtype)
    return pl.pallas_call(
        body, name=name, out_shape=(hbm(src), hbm(land)),
        in_specs=(HBM, HBM, SEM, SEM, ANY), out_specs=(HBM, HBM), input_output_aliases={0: 0, 1: 1},
        compiler_params=pltpu.CompilerParams(has_side_effects=EFFECT),
    )(src, land, send_sems, recv_sems, after)


def sibling_exchange(srcs, name):
    n = len(srcs)

    def body(*refs):
        src_refs, out_refs, send_sems, recv_sems = refs[:n], refs[n:2 * n], refs[2 * n], refs[2 * n + 1]
        x, y, c = lax.axis_index("x"), lax.axis_index("y"), lax.axis_index("c")
        copies = [pltpu.make_async_remote_copy(src_ref=s, dst_ref=o, send_sem=send_sems.at[k], recv_sem=recv_sems.at[k],
                                               device_id=(x, y, 1 - c), device_id_type=MESH)
                  for k, (s, o) in enumerate(zip(src_refs, out_refs))]
        for cp in copies:
            cp.start()
        for cp in copies:
            cp.wait()

    return pl.pallas_call(
        body, name=name, in_specs=[ANY] * n, out_specs=[ANY] * n,
        out_shape=[jax.ShapeDtypeStruct(s.shape, s.dtype) for s in srcs],
        scratch_shapes=[pltpu.SemaphoreType.DMA((n,)), pltpu.SemaphoreType.DMA((n,))],
    )(*srcs)


def chip_gather(src, name):
    def body(src_ref, out_ref, send_sems, recv_sems, local_sem):
        x, y, c = lax.axis_index("x"), lax.axis_index("y"), lax.axis_index("c")
        me = 2 * x + y
        local = pltpu.make_async_copy(src_ref, out_ref.at[me], local_sem)
        local.start()
        peers = [(1 - x, y), (x, 1 - y), (1 - x, 1 - y)]
        mk = lambda j, px, py, slab: pltpu.make_async_remote_copy(
            src_ref=src_ref, dst_ref=out_ref.at[slab], send_sem=send_sems.at[j], recv_sem=recv_sems.at[j],
            device_id=(px, py, c), device_id_type=MESH)
        sends = [mk(j, px, py, me) for j, (px, py) in enumerate(peers)]
        for cp in sends:
            cp.start()
        for j, (px, py) in enumerate(peers):
            mk(j, px, py, 2 * px + py).wait_recv()
        for cp in sends:
            cp.wait_send()
        local.wait()

    return pl.pallas_call(
        body, name=name, in_specs=[ANY], out_specs=ANY,
        out_shape=jax.ShapeDtypeStruct((N_CHIPS,) + tuple(src.shape), src.dtype),
        scratch_shapes=[pltpu.SemaphoreType.DMA((3,)), pltpu.SemaphoreType.DMA((3,)), pltpu.SemaphoreType.DMA],
    )(src)


def all_exchange(src, name):
    def body(src_ref, out_ref, send_sems, recv_sems, local_sem):
        x, y, c = lax.axis_index("x"), lax.axis_index("y"), lax.axis_index("c")
        me = 4 * x + 2 * y + c
        local = pltpu.make_async_copy(src_ref, out_ref.at[me], local_sem)
        local.start()
        sends = []
        peers = []
        for k in range(1, N_DEV):
            bx, by, bc = (k >> 2) & 1, (k >> 1) & 1, k & 1
            px = 1 - x if bx else x
            py = 1 - y if by else y
            pc = 1 - c if bc else c
            peers.append((px, py, pc))
        for k, peer in enumerate(peers):
            cp = pltpu.make_async_remote_copy(src_ref=src_ref, dst_ref=out_ref.at[me], send_sem=send_sems.at[k],
                                              recv_sem=recv_sems.at[k], device_id=peer, device_id_type=MESH)
            cp.start()
            sends.append(cp)
        for k, (px, py, pc) in enumerate(peers):
            pltpu.make_async_remote_copy(src_ref=src_ref, dst_ref=out_ref.at[4 * px + 2 * py + pc],
                                         send_sem=send_sems.at[k], recv_sem=recv_sems.at[k],
                                         device_id=(px, py, pc), device_id_type=MESH).wait_recv()
        for cp in sends:
            cp.wait_send()
        local.wait()

    return pl.pallas_call(
        body, name=name, in_specs=[ANY], out_specs=ANY,
        out_shape=jax.ShapeDtypeStruct((N_DEV,) + tuple(src.shape), src.dtype),
        scratch_shapes=[pltpu.SemaphoreType.DMA((N_DEV - 1,)), pltpu.SemaphoreType.DMA((N_DEV - 1,)),
                        pltpu.SemaphoreType.DMA],
    )(src)


def sum_own_slabs(own, land, name, tile=512):
    R, C = own.shape
    n = land.shape[0]
    tile = _tile(R, tile, 16)

    def body(o_ref, t_ref, out_ref):
        acc = o_ref[...].astype(F32)
        for s in range(n):
            acc = acc + t_ref[s].astype(F32)
        out_ref[...] = acc

    return pl.pallas_call(
        body, name=name, grid=(R // tile,),
        in_specs=[pl.BlockSpec((tile, C), lambda i: (i, 0)), pl.BlockSpec((n, tile, C), lambda i: (0, i, 0))],
        out_specs=pl.BlockSpec((tile, C), lambda i: (i, 0)), out_shape=jax.ShapeDtypeStruct((R, C), F32),
        compiler_params=_cp("parallel"),
    )(own, land)


def sum_slabs(t, name, tile=512):
    n, R, C = t.shape
    tile = _tile(R, tile, 16)

    def body(t_ref, o_ref):
        acc = t_ref[0].astype(F32)
        for s in range(1, n):
            acc = acc + t_ref[s].astype(F32)
        o_ref[...] = acc

    return pl.pallas_call(
        body, name=name, grid=(R // tile,), in_specs=[pl.BlockSpec((n, tile, C), lambda i: (0, i, 0))],
        out_specs=pl.BlockSpec((tile, C), lambda i: (i, 0)), out_shape=jax.ShapeDtypeStruct((R, C), F32),
        compiler_params=_cp("parallel"),
    )(t)


PACK_ROW_MULT = 512


def _pad_rows(a, mult=16):
    r = (-a.shape[0]) % mult
    return jnp.pad(a, ((0, r), (0, 0))) if r else a


BIG = [
    ("ab_w_in", True, 2, 642), ("c_w_qkv", True, 2, 768), ("xa_w_kv", True, 4, 512),
    ("f_w_gate", True, 4, 704), ("f_w_up", True, 4, 704),
    ("ab_w_out", False, 2, 256), ("c_w_out", False, 2, 256), ("xa_w_q", False, 4, 256),
    ("xa_w_out", False, 4, 256), ("f_w_down", False, 4, 704), ("s5_w_glu", False, 2, 64),
]


GROUPS = ("B", "A")
GROUP_ROW_MULT = 64


def group_spec(layer, grp):
    i = layer // 2
    if grp == "A":
        return [("xa_w_kv", layer, True, 512), ("xa_w_q", layer, False, 256), ("xa_w_out", layer, False, 256),
                ("f_w_gate", layer, True, 704), ("f_w_up", layer, True, 704), ("f_w_down", layer, False, 704)]
    if layer % 2 == 0:
        return [("ab_w_in", i, True, 642), ("ab_w_out", i, False, 256), ("s5_w_glu", i, False, 64)]
    return [("c_w_qkv", i, True, 768), ("c_w_out", i, False, 256)]


def _seg_rows(rows):
    return rows + ((-rows) % 16)


def _f32_rows(a):
    bits = lax.bitcast_convert_type(a.reshape(-1), BF16).reshape(-1)
    return jnp.pad(bits, (0, 16 * D - bits.shape[0])).reshape(16, D)


def pack_group_shards(p, layer, grp):
    segs = []
    for name, idx, transposed, rows in group_spec(layer, grp):
        w = p[name][idx]
        if transposed:
            w = w.T
        segs.append(_pad_rows(w.astype(BF16).reshape(-1, D)))
    if grp == "B":
        small = p["gdn_conv_w"] if layer % 2 == 0 else p["c_norm_g"]
        segs.append(_f32_rows(small[layer // 2]))
    return _pad_rows(jnp.concatenate(segs, axis=0), GROUP_ROW_MULT)


def unpack_group_gathered(g, layer, grp):
    out, off = {}, 0
    for name, idx, transposed, rows in group_spec(layer, grp):
        seg = g[:, off:off + rows]
        if name == "s5_w_glu":
            out[name] = seg.reshape(N_CHIPS * 128, 512)
        else:
            out[name] = seg.reshape(N_CHIPS * rows, D)
        off += _seg_rows(rows)
    if grp == "B":
        n = 4 * 384 if layer % 2 == 0 else 256
        bits = g[:, off:off + 16].reshape(N_CHIPS, -1)[:, :2 * n].reshape(N_CHIPS, n, 2)
        small = lax.bitcast_convert_type(bits, F32)
        if layer % 2 == 0:
            out["gdn_conv_w"] = jnp.swapaxes(small.reshape(N_CHIPS, 4, 384), 0, 1).reshape(4, 1536)
        else:
            out["c_norm_g"] = small.reshape(D)
    return out


def pack_group_grads(gr, layer, grp):
    segs = []
    for name, idx, transposed, rows in group_spec(layer, grp):
        w = gr[name].astype(BF16)
        seg = w.reshape(N_CHIPS, rows, D)
        r = (-rows) % 16
        if r:
            seg = jnp.pad(seg, ((0, 0), (0, r), (0, 0)))
        segs.append(seg)
    out = jnp.concatenate(segs, axis=1)
    return jnp.pad(out, ((0, 0), (0, (-out.shape[1]) % GROUP_ROW_MULT), (0, 0)))


def unpack_group_reduced(g, layer, grp):
    out, off = {}, 0
    for name, idx, transposed, rows in group_spec(layer, grp):
        seg = g[off:off + rows]
        if name == "s5_w_glu":
            out[name] = seg.reshape(128, 512)
        else:
            out[name] = seg.T if (transposed and name not in ADAM_TRANSPOSED) else seg
        off += _seg_rows(rows)
    return out


ADAM_TRANSPOSED = ("f_w_gate", "f_w_up")


SMALL = ["ab_norm_g", "s5_a_re", "s5_a_im", "s5_log_dt", "s5_b_re", "s5_b_im", "s5_c_re", "s5_c_im", "s5_d",
         "s5_b_glu", "gdn_conv_w", "gdn_a_log", "gdn_dt_bias", "gdn_out_norm_g", "c_norm_g", "c_q_norm_g",
         "c_k_norm_g", "c_rel_bias", "mem_norm_g", "xa_norm_g", "xa_q_norm_g", "xa_k_norm_g", "f_norm_g"]


def _lane_rows(a):
    flat = a.reshape(-1).astype(F32)
    return jnp.pad(flat, (0, (-flat.shape[0]) % 1024)).reshape(-1, 128)


def pack_small(d, extra=None):
    parts = [_lane_rows(d[n]) for n in SMALL]
    if extra is not None:
        parts.append(_lane_rows(extra))
    rows = jnp.concatenate(parts, axis=0)
    return jnp.pad(rows, ((0, (-rows.shape[0]) % 128), (0, 0)))


def unpack_small(rows, shapes):
    out, off = {}, 0
    for n in SMALL:
        sz = int(np.prod(shapes[n]))
        k = 8 * -(-sz // 1024)
        out[n] = rows[off:off + k].reshape(-1)[:sz].reshape(shapes[n])
        off += k
    return out, rows[off:]


def _s5_blockdiag_b(b):
    bt = jnp.swapaxes(b, 1, 2).reshape(S5_GB, 8, S5_C, S5_P)
    eye = jnp.eye(8, dtype=b.dtype)
    return jnp.einsum("bgcp,gh->bgchp", bt, eye).reshape(S5_GB, 8 * S5_C, 8 * S5_P)


def _s5_blockdiag_c(c):
    ct = jnp.swapaxes(c, 1, 2).reshape(S5_GB, 8, S5_P, S5_C)
    eye = jnp.eye(8, dtype=c.dtype)
    return jnp.einsum("bgpc,gh->bgphc", ct, eye).reshape(S5_GB, 8 * S5_P, 8 * S5_C)


def _s5_diag_b(db):
    t = db.reshape(S5_GB, 8, S5_C, 8, S5_P)
    t = jnp.transpose(t, (0, 2, 4, 1, 3)).reshape(S5_GB, S5_C, S5_P, 64)
    d = t[..., ::9]
    return jnp.transpose(d, (0, 3, 2, 1)).reshape(S5_G, S5_P, S5_C)


def _s5_diag_c(dc):
    t = dc.reshape(S5_GB, 8, S5_P, 8, S5_C)
    t = jnp.transpose(t, (0, 2, 4, 1, 3)).reshape(S5_GB, S5_P, S5_C, 64)
    d = t[..., ::9]
    return jnp.transpose(d, (0, 3, 2, 1)).reshape(S5_G, S5_C, S5_P)


def _heads(t):
    return jnp.swapaxes(t.reshape(t.shape[0], CA_H, CA_D), 0, 1)


def _unheads(t):
    return jnp.swapaxes(t, 0, 1).reshape(t.shape[1], D)


def local_step(x, mem, target, p, wsrc, gsink):
    S = x.shape[0]
    row2 = lambda a: a.reshape(1, -1)
    saved = []
    (mem_n,) = row_fwd("mem_norm", f_norm, [mem], [row2(p["mem_norm_g"])], [(D, BF16)])
    gs = {}

    for layer in range(DEPTH):
        i = layer // 2
        w = wsrc(layer, "B", x)
        sv = {"x0": x, "wB": w}
        if layer % 2 == 0:
            (h,) = row_fwd("norm", f_norm, [x], [row2(p["ab_norm_g"][i])], [(D, BF16)])
            w_in = w["ab_w_in"]
            pm = matmul("nt", h, w_in[:2560], "proj_main")
            pab = matmul("nt", h, w_in[2560:], "proj_ab")
            s5p = dict(
                are=p["s5_a_re"][i].reshape(1, -1), aim=p["s5_a_im"][i].reshape(1, -1),
                ldt=jnp.broadcast_to(p["s5_log_dt"][i][:, None], (S5_G, S5_P)).reshape(1, -1),
                b_re=_s5_blockdiag_b(p["s5_b_re"][i]), b_im=_s5_blockdiag_b(p["s5_b_im"][i]),
                c_re=_s5_blockdiag_c(p["s5_c_re"][i]), c_im=_s5_blockdiag_c(p["s5_c_im"][i]),
                dv=p["s5_d"][i].reshape(1, -1))
            y5, st5 = s5_fwd(pm, **s5p)
            (a_out,) = row_fwd("glu", f_glu, [y5], [w["s5_w_glu"], row2(p["s5_b_glu"][i])], [(S5_W, F32)])
            conv_w = w["gdn_conv_w"]
            qkvc = conv_fwd(pm, conv_w)
            abt = jnp.swapaxes(pab, 0, 1)[:, :, None]
            alog = p["gdn_a_log"][i].reshape(GDN_H, 1, 1)
            dtb = p["gdn_dt_bias"][i].reshape(GDN_H, 1, 1)
            og = row2(p["gdn_out_norm_g"][i])
            b_out, stg = gdn_fwd(qkvc, pm, abt, alog, dtb, og)
            cat = jnp.concatenate([a_out, b_out], axis=1)
            x, hq = matmul("nn", cat, w["ab_w_out"], "mix_out", add=x, norm_out=row2(p["xa_norm_g"][layer]))
            sv.update(h=h, pm=pm, s5p=s5p, y5=y5, st5=st5, qkvc=qkvc, abt=abt, alog=alog, dtb=dtb, og=og,
                      stg=stg, cat=cat, conv_w=conv_w)
        else:
            (h,) = row_fwd("norm", f_norm, [x], [row2(w["c_norm_g"])], [(D, BF16)])
            qkv = matmul("nt", h, w["c_w_qkv"], "proj_qkv")
            qg = jnp.tile(row2(p["c_q_norm_g"][i]), (1, CA_H))
            kg = jnp.tile(row2(p["c_k_norm_g"][i]), (1, CA_H))
            (qn,) = row_fwd("headnorm_q", f_headnorm, [(qkv, D, 0)], [qg], [(D, F32)])
            kp, vp = kv_prep(qkv, kg)
            bias = rel_bias_expand(p["c_rel_bias"][i]).reshape(CA_H * CHUNK, CA_BAND)
            o = cattn_fwd(qn, kp, vp, bias)
            x, hq = matmul("nn", o, w["c_w_out"], "mix_out", add=x, norm_out=row2(p["xa_norm_g"][layer]))
            sv.update(h=h, qkv=qkv, qg=qg, kg=kg, qn=qn, kp=kp, vp=vp, bias=bias, o=o)
        sv["x1"] = x
        w = wsrc(layer, "A", x)
        sv["wA"] = w
        qx = matmul("nn", hq, w["xa_w_q"], "xa_q")
        kv = matmul("nt", mem_n, w["xa_w_kv"], "xa_kv")
        xqg, xkg = row2(p["xa_q_norm_g"][layer]), row2(p["xa_k_norm_g"][layer])
        (ox,) = row_fwd("xattn", f_xattn, [qx], [kv, xqg, xkg], [(D, BF16)])
        x, hf = matmul("nn", ox, w["xa_w_out"], "xa_out", add=x, norm_out=row2(p["f_norm_g"][layer]))
        sv.update(hq=hq, qx=qx, kv=kv, ox=ox)
        sv["x2"] = x
        gate, up, act = ffn_in(hf, w["f_w_gate"], w["f_w_up"])
        x = matmul("nn", act, w["f_w_down"], "ffn_down", add=x)
        sv.update(hf=hf, gate=gate, up=up, act=act)
        saved.append(sv)

    dx, loss_vec = loss_head(x, target)

    dmem_n = None
    for layer in reversed(range(DEPTH)):
        i = layer // 2
        sv = saved[layer]
        w, gw = sv["wA"], {}
        dgate, dup = ffn_dact(dx, w["f_w_down"], sv["gate"], sv["up"])
        gw["f_w_down"] = matmul("tn", sv["act"], dx, "ffn_dwd", out_dtype=BF16)
        gw["f_w_gate"] = matmul("tn", dgate, sv["hf"], "ffn_dwg", out_dtype=BF16)
        gw["f_w_up"] = matmul("tn", dup, sv["hf"], "ffn_dwu", out_dtype=BF16)
        dh = matmul("nn", dgate, w["f_w_gate"], "ffn_dhg")
        dx, dg = matmul("nn", dup, w["f_w_up"], "ffn_dhu", add=dh,
                        norm_bwd=(sv["x2"], row2(p["f_norm_g"][layer]), dx))
        gs.setdefault("f_norm_g", [None] * DEPTH)[layer] = dg[0]
        do = matmul("nt", dx, w["xa_w_out"], "xa_do")
        gw["xa_w_out"] = matmul("tn", sv["ox"], dx, "xa_dwo", out_dtype=BF16)
        xqg, xkg = row2(p["xa_q_norm_g"][layer]), row2(p["xa_k_norm_g"][layer])
        dqx, dkv, dqg, dkg = row_bwd("xattn_bwd", f_xattn, [sv["qx"]], [sv["kv"], xqg, xkg], [do],
                                     [0], [0, 1, 2])
        gs.setdefault("xa_q_norm_g", [None] * DEPTH)[layer] = dqg[0]
        gs.setdefault("xa_k_norm_g", [None] * DEPTH)[layer] = dkg[0]
        gw["xa_w_q"] = matmul("tn", sv["hq"], dqx, "xa_dwq", out_dtype=BF16)
        gw["xa_w_kv"] = matmul("tn", dkv, mem_n, "xa_dwkv", out_dtype=BF16)
        dmem_n = matmul("nn", dkv, w["xa_w_kv"], "xa_dmem", add=dmem_n)
        dx, dg = matmul("nt", dqx, w["xa_w_q"], "xa_dhq", norm_bwd=(sv["x1"], row2(p["xa_norm_g"][layer]), dx))
        gs.setdefault("xa_norm_g", [None] * DEPTH)[layer] = dg[0]
        dx = gsink(layer, "A", gw, dx)
        w, gw = sv["wB"], {}
        if layer % 2 == 0:
            dcat = matmul("nt", dx, w["ab_w_out"], "mix_dcat")
            gw["ab_w_out"] = matmul("tn", sv["cat"], dx, "mix_dwo", out_dtype=BF16)
            dy5, dwglu, dbglu = row_bwd("glu_bwd", f_glu, [sv["y5"]], [w["s5_w_glu"], row2(p["s5_b_glu"][i])],
                                        [(dcat, S5_W, 0)], [0], [0, 1])
            gw["s5_w_glu"] = dwglu
            gs.setdefault("s5_b_glu", [None] * 2)[i] = dbglu[0]
            s5p = sv["s5p"]
            du, dare, daim, dldt, dbre, dbim, dcre, dcim, ddv = s5_bwd(sv["pm"], sv["st5"], dy5, **s5p)
            (dldt_g,) = whole("s5_dt_sum", lambda t: (jnp.sum(t, axis=1, keepdims=True),),
                              [dldt.reshape(S5_G, S5_P)], [((S5_G, 1), F32)])
            for nme, val in (("s5_a_re", dare.reshape(S5_G, S5_P)), ("s5_a_im", daim.reshape(S5_G, S5_P)),
                             ("s5_log_dt", dldt_g[:, 0]), ("s5_b_re", _s5_diag_b(dbre)),
                             ("s5_b_im", _s5_diag_b(dbim)), ("s5_c_re", _s5_diag_c(dcre)),
                             ("s5_c_im", _s5_diag_c(dcim)), ("s5_d", ddv.reshape(S5_G, S5_C))):
                gs.setdefault(nme, [None] * 2)[i] = val
            dq, dk, dv, dgate, dal, dbl, dalog, ddtb, dog = gdn_bwd(
                sv["qkvc"], sv["pm"], sv["abt"], sv["alog"], sv["dtb"], sv["og"], sv["stg"], dcat[:, S5_W:])
            (dog_s,) = whole("gdn_og_sum", lambda t: (jnp.sum(t, axis=0, keepdims=True),),
                             [dog.reshape(GDN_H, GDN_D)], [((1, GDN_D), F32)])
            gs.setdefault("gdn_out_norm_g", [None] * 2)[i] = dog_s[0]
            gs.setdefault("gdn_a_log", [None] * 2)[i] = dalog.reshape(GDN_H)
            gs.setdefault("gdn_dt_bias", [None] * 2)[i] = ddtb.reshape(GDN_H)
            dqkvc = jnp.concatenate([dq, dk, dv], axis=1)
            dqkv, dconv = conv_bwd(sv["pm"], sv["conv_w"], dqkvc)
            gs.setdefault("gdn_conv_w", [None] * 2)[i] = dconv
            dpm = jnp.concatenate([du, dqkv, dgate], axis=1).astype(BF16)
            dpab = jnp.swapaxes(jnp.concatenate([dal, dbl], axis=0)[:, :, 0], 0, 1)
            dw_main = matmul("tn", dpm, sv["h"], "proj_dw", out_dtype=BF16)
            dw_ab = matmul("tn", dpab, sv["h"], "proj_ab_dw", out_dtype=BF16)
            gw["ab_w_in"] = jnp.concatenate([dw_main, dw_ab], axis=0)
            w_in = w["ab_w_in"]
            dh = matmul("nn", dpm, w_in[:2560], "proj_dh")
            dx, dg = matmul("nn", dpab, w_in[2560:], "proj_ab_dh", add=dh,
                            norm_bwd=(sv["x0"], row2(p["ab_norm_g"][i]), dx))
            gs.setdefault("ab_norm_g", [None] * 2)[i] = dg[0]
        else:
            do = matmul("nt", dx, w["c_w_out"], "mix_dcat")
            gw["c_w_out"] = matmul("tn", sv["o"], dx, "mix_dwo", out_dtype=BF16)
            dqn, dkp, dvp, dbias = cattn_bwd(sv["qn"], sv["kp"], sv["vp"], sv["bias"], do)
            gs.setdefault("c_rel_bias", [None] * 2)[i] = rel_bias_grad(dbias.reshape(CA_H, CHUNK, CA_BAND))
            dq, dqg = row_bwd("headnorm_bwd", f_headnorm, [(sv["qkv"], D, 0)], [sv["qg"]], [dqn], [0], [0])
            dk, dkg = row_bwd("headnorm_bwd", f_headnorm, [(sv["qkv"], D, 1)], [sv["kg"]],
                              [(dkp, D, 0, CA_PAD)], [0], [0])
            head_sum = lambda t: (jnp.sum(t, axis=0, keepdims=True),)
            (dqg,) = whole("headgain_sum", head_sum, [dqg.reshape(CA_H, CA_D)], [((1, CA_D), F32)])
            (dkg,) = whole("headgain_sum", head_sum, [dkg.reshape(CA_H, CA_D)], [((1, CA_D), F32)])
            gs.setdefault("c_q_norm_g", [None] * 2)[i] = dqg[0]
            gs.setdefault("c_k_norm_g", [None] * 2)[i] = dkg[0]
            dqkv = jnp.concatenate([dq, dk, dvp[CA_PAD:]], axis=1).astype(BF16)
            gw["c_w_qkv"] = matmul("tn", dqkv, sv["h"], "proj_qkv_dw", out_dtype=BF16)
            dx, dg = matmul("nn", dqkv, w["c_w_qkv"], "proj_qkv_dh",
                            norm_bwd=(sv["x0"], row2(w["c_norm_g"]), dx))
            gs.setdefault("c_norm_g", [None] * 2)[i] = dg[0]
        dx = gsink(layer, "B", gw, dx)
    (dmg,) = row_bwd("mem_norm_bwd", f_norm, [mem], [row2(p["mem_norm_g"])], [dmem_n], [], [0])
    small = {n: jnp.stack(v) for n, v in gs.items()}
    small["mem_norm_g"] = dmg[0]
    return loss_vec, dx, small


ADAM_BLOCK_BYTES = 3 << 19


def adam(w, g, m, v, name):
    shape = w.shape
    if w.ndim == 3:
        d0, n, d2 = shape
        fits = [t for t in range(8, n + 1, 8) if n % t == 0 and d0 * t * d2 * 4 <= ADAM_BLOCK_BYTES]
        return tuple(row_fwd(name, f_adam, [w, g, m, v], [], [((d0, d2), F32)] * 3, tile=max(fits)))
    cols = shape[-1]
    w2, g2, m2, v2 = (t.reshape(-1, cols) for t in (w, g, m, v))
    rows = w2.shape[0]
    tile = rows if rows <= 512 else _tile(rows, 512, 8)
    outs = row_fwd(name, f_adam, [w2, g2, m2, v2], [], [(cols, F32)] * 3, tile=tile)
    return tuple(o.reshape(shape) for o in outs)


WEIGHTS = ['ab_norm_g', 'ab_w_in', 'ab_w_out', 's5_a_re', 's5_a_im', 's5_log_dt', 's5_b_re', 's5_b_im', 's5_c_re',
           's5_c_im', 's5_d', 's5_w_glu', 's5_b_glu', 'gdn_conv_w', 'gdn_a_log', 'gdn_dt_bias', 'gdn_out_norm_g',
           'c_norm_g', 'c_w_qkv', 'c_w_out', 'c_q_norm_g', 'c_k_norm_g', 'c_rel_bias', 'mem_norm_g', 'xa_norm_g',
           'xa_w_q', 'xa_w_kv', 'xa_w_out', 'xa_q_norm_g', 'xa_k_norm_g', 'f_norm_g', 'f_w_gate', 'f_w_up',
           'f_w_down']
SHARDED_SMALL = {"gdn_conv_w": (2, 384), "c_norm_g": (1, 256)}


def kernel(x, mem, ab_norm_g, ab_w_in, ab_w_out, s5_a_re, s5_a_im, s5_log_dt, s5_b_re, s5_b_im, s5_c_re, s5_c_im, s5_d, s5_w_glu, s5_b_glu, gdn_conv_w, gdn_a_log, gdn_dt_bias, gdn_out_norm_g, c_norm_g, c_w_qkv, c_w_out, c_q_norm_g, c_k_norm_g, c_rel_bias, mem_norm_g, xa_norm_g, xa_w_q, xa_w_kv, xa_w_out, xa_q_norm_g, xa_k_norm_g, f_norm_g, f_w_gate, f_w_up, f_w_down, loss_target, m_ab_norm_g, m_ab_w_in, m_ab_w_out, m_s5_a_re, m_s5_a_im, m_s5_log_dt, m_s5_b_re, m_s5_b_im, m_s5_c_re, m_s5_c_im, m_s5_d, m_s5_w_glu, m_s5_b_glu, m_gdn_conv_w, m_gdn_a_log, m_gdn_dt_bias, m_gdn_out_norm_g, m_c_norm_g, m_c_w_qkv, m_c_w_out, m_c_q_norm_g, m_c_k_norm_g, m_c_rel_bias, m_mem_norm_g, m_xa_norm_g, m_xa_w_q, m_xa_w_kv, m_xa_w_out, m_xa_q_norm_g, m_xa_k_norm_g, m_f_norm_g, m_f_w_gate, m_f_w_up, m_f_w_down, v_ab_norm_g, v_ab_w_in, v_ab_w_out, v_s5_a_re, v_s5_a_im, v_s5_log_dt, v_s5_b_re, v_s5_b_im, v_s5_c_re, v_s5_c_im, v_s5_d, v_s5_w_glu, v_s5_b_glu, v_gdn_conv_w, v_gdn_a_log, v_gdn_dt_bias, v_gdn_out_norm_g, v_c_norm_g, v_c_w_qkv, v_c_w_out, v_c_q_norm_g, v_c_k_norm_g, v_c_rel_bias, v_mem_norm_g, v_xa_norm_g, v_xa_w_q, v_xa_w_kv, v_xa_w_out, v_xa_q_norm_g, v_xa_k_norm_g, v_f_norm_g, v_f_w_gate, v_f_w_up, v_f_w_down):
    args = locals()
    p = {n: args[n] for n in WEIGHTS}
    m = {n: args["m_" + n] for n in WEIGHTS}
    v = {n: args["v_" + n] for n in WEIGHTS}
    chip = 2 * lax.axis_index("x") + lax.axis_index("y")

    carry = x[0]
    gathers = {}
    for layer in range(DEPTH):
        for grp in GROUPS:
            src = pack_group_shards(p, layer, grp)
            land = lax.dynamic_update_slice(lax.empty((N_CHIPS,) + src.shape, BF16), src[None], (chip, 0, 0))
            send_sems, recv_sems, src, land, carry = exchange_start(
                src, land, carry, f"gather_start_{layer}{grp}", per_dest=False)
            gathers[layer, grp] = (src, land, send_sems, recv_sems)

    def wsrc(layer, grp, after):
        src, land, send_sems, recv_sems = gathers[layer, grp]
        _, land = exchange_wait(src, land, send_sems, recv_sems, after, f"gather_wait_{layer}{grp}", per_dest=False)
        return unpack_group_gathered(land, layer, grp)

    scatters, siblings = [], []
    LAG = 2

    def finish(carry):
        layer, grp, src, land, send_sems, recv_sems = scatters[len(siblings)]
        src, land = exchange_wait(src, land, send_sems, recv_sems, carry, f"scatter_wait_{layer}{grp}", per_dest=True)
        own = lax.dynamic_index_in_dim(src, chip, axis=0, keepdims=False)
        part = sum_own_slabs(own, land, "sum_chips")
        send_sems, recv_sems, part, other, carry = exchange_start(
            part, lax.empty(part.shape, F32), carry, f"sibling_start_{layer}{grp}", per_dest="sibling")
        siblings.append((layer, grp, part, other, send_sems, recv_sems))
        return carry

    def gsink(layer, grp, gw, carry):
        src = pack_group_grads(gw, layer, grp)
        land = lax.empty((3,) + src.shape[1:], BF16)
        send_sems, recv_sems, src, land, carry = exchange_start(
            src, land, carry, f"scatter_start_{layer}{grp}", per_dest=True)
        scatters.append((layer, grp, src, land, send_sems, recv_sems))
        if len(scatters) > LAG:
            carry = finish(carry)
        return carry

    loss_vec, dx, g_small = local_step(carry, mem[0], loss_target[0], p, wsrc, gsink)
    while len(siblings) < len(scatters):
        dx = finish(dx)

    per_layer = {}
    for layer, grp, part, other, send_sems, recv_sems in siblings:
        part, other = exchange_wait(part, other, send_sems, recv_sems, dx, f"sibling_wait_{layer}{grp}",
                                    per_dest="sibling")
        (total,) = row_fwd("sum_cores", lambda a, b: (a + b,), [part, other], [], [(D, F32)],
                           tile=_tile(part.shape[0], 512, 16))
        for name, g in unpack_group_reduced(total, layer, grp).items():
            per_layer.setdefault(name, {})[layer] = g
    grads = {name: jnp.stack([d[k] for k in sorted(d)]) for name, d in per_layer.items()}

    full_shapes = {n: ((2, 4, 1536) if n == "gdn_conv_w" else (2, D) if n == "c_norm_g" else p[n].shape)
                   for n in SMALL}
    small_mine = pack_small(g_small, extra=loss_vec)
    (small_other,) = sibling_exchange([small_mine], "sibling_small")
    (small_chip,) = row_fwd("sum_small_cores", lambda a, b: (a + b,), [small_mine, small_other], [], [(128, F32)],
                            tile=128)
    small_sum = sum_slabs(chip_gather(small_chip, "gather_small"), "sum_small")
    g_s, rest = unpack_small(small_sum, full_shapes)
    (loss11,) = whole("loss_sum", lambda t: (jnp.sum(jnp.sum(t, axis=1, keepdims=True), axis=0, keepdims=True),),
                      [rest[:D // 128]], [((1, 1), F32)])
    for n, (axis, width) in SHARDED_SMALL.items():
        g_s[n] = lax.dynamic_slice_in_dim(g_s[n], chip * width, width, axis=axis)
    grads.update(g_s)

    delta, new_m, new_v = {}, {}, {}
    for n in SMALL:
        shape = p[n].shape
        two = (1, shape[0]) if len(shape) == 1 else (int(np.prod(shape[:-1])), shape[-1])
        outs = whole("adam_" + n, f_adam, [t.reshape(two) for t in (p[n], grads[n], m[n], v[n])], [(two, F32)] * 3)
        delta[n], new_m[n], new_v[n] = (o.reshape(shape) for o in outs)
    for name, _, _, _ in BIG:
        if name in ADAM_TRANSPOSED:
            t = lambda a: jnp.swapaxes(a, 1, 2)
            outs = adam(t(p[name]), grads[name], t(m[name]), t(v[name]), "adam_" + name)
            delta[name], new_m[name], new_v[name] = (t(o) for o in outs)
            grads[name] = t(grads[name])
        else:
            delta[name], new_m[name], new_v[name] = adam(p[name], grads[name], m[name], v[name], "adam_" + name)

    return (loss11[0, 0], dx[None], *[grads[n] for n in WEIGHTS], *[delta[n] for n in WEIGHTS],
            *[new_m[n] for n in WEIGHTS], *[new_v[n] for n in WEIGHTS])
```

```python
import functools
import math

import numpy as np
import jax
import jax.numpy as jnp
from jax import lax
from jax.experimental import pallas as pl
from jax.experimental.pallas import tpu as pltpu

F32 = jnp.float32
BF16 = jnp.bfloat16
MESH = pl.DeviceIdType.MESH

D = 1024
CHUNK = 64
N_MEM = 256
EPS = 1e-6
S5_W = 512
S5_G = 32
S5_C = 16
S5_P = 64
S5_GB = 4
GDN_H = 4
GDN_D = 128
CA_H = 16
CA_D = 64
CA_LEFT = 8
CA_BAND = (CA_LEFT + 1) * CHUNK
CA_PAD = CA_LEFT * CHUNK
MAX_REL = 128
XA_H = 4
XA_D = 256
FFN = 2816
DEPTH = 4
N_CHIPS = 4
N_DEV = 8
LR, B1, B2, AEPS, WD, STEP = 0.001, 0.9, 0.999, 1e-08, 0.01, 10

VMEM_LIMIT = 56 * 1024 * 1024
ROW_TILE = 256
HI = lax.Precision.HIGHEST


def _cp(*sem):
    return pltpu.CompilerParams(dimension_semantics=sem, vmem_limit_bytes=VMEM_LIMIT)


def _dg(a, b, ca, cb):
    return lax.dot_general(a.astype(BF16), b.astype(BF16), (((ca,), (cb,)), ((), ())),
                           preferred_element_type=F32)


@jax.custom_vjp
def mm(a, b):
    return _dg(a, b, 1, 0)


@jax.custom_vjp
def mm_nt(a, b):
    return _dg(a, b, 1, 1)


@jax.custom_vjp
def mm_tn(a, b):
    return _dg(a, b, 0, 0)


mm.defvjp(lambda a, b: (mm(a, b), (a, b)), lambda r, g: (mm_nt(g, r[1]), mm_tn(r[0], g)))
mm_nt.defvjp(lambda a, b: (mm_nt(a, b), (a, b)), lambda r, g: (mm(g, r[1]), mm_tn(g, r[0])))
mm_tn.defvjp(lambda a, b: (mm_tn(a, b), (a, b)), lambda r, g: (mm_nt(r[1], g), mm(r[0], g)))


def _bdg(a, b, ca, cb):
    return lax.dot_general(a.astype(BF16), b.astype(BF16), (((ca,), (cb,)), ((0,), (0,))),
                           preferred_element_type=F32)


@jax.custom_vjp
def bmm(a, b):
    return _bdg(a, b, 2, 1)


@jax.custom_vjp
def bmm_nt(a, b):
    return _bdg(a, b, 2, 2)


@jax.custom_vjp
def bmm_tn(a, b):
    return _bdg(a, b, 1, 1)


bmm.defvjp(lambda a, b: (bmm(a, b), (a, b)), lambda r, g: (bmm_nt(g, r[1]), bmm_tn(r[0], g)))
bmm_nt.defvjp(lambda a, b: (bmm_nt(a, b), (a, b)), lambda r, g: (bmm(g, r[1]), bmm_tn(g, r[0])))
bmm_tn.defvjp(lambda a, b: (bmm_tn(a, b), (a, b)), lambda r, g: (bmm_nt(r[1], g), bmm(r[0], g)))


def _split(a):
    hi = a.astype(BF16)
    return hi, (a - hi.astype(F32)).astype(BF16)


def _dg3(a, b, ca, cb):
    (ah, al), (bh, bl) = _split(a), _split(b)
    d = lambda u, v: lax.dot_general(u, v, (((ca,), (cb,)), ((), ())), preferred_element_type=F32)
    return d(ah, bh) + (d(ah, bl) + d(al, bh))


@jax.custom_vjp
def mm3(a, b):
    return _dg3(a, b, 1, 0)


@jax.custom_vjp
def mm3_nt(a, b):
    return _dg3(a, b, 1, 1)


@jax.custom_vjp
def mm3_tn(a, b):
    return _dg3(a, b, 0, 0)


mm3.defvjp(lambda a, b: (mm3(a, b), (a, b)), lambda r, g: (mm3_nt(g, r[1]), mm3_tn(r[0], g)))
mm3_nt.defvjp(lambda a, b: (mm3_nt(a, b), (a, b)), lambda r, g: (mm3(g, r[1]), mm3_tn(g, r[0])))
mm3_tn.defvjp(lambda a, b: (mm3_tn(a, b), (a, b)), lambda r, g: (mm3_nt(r[1], g), mm3(r[0], g)))


def _tri_mm(v, upper):
    T = v.shape[0]
    r = lax.broadcasted_iota(jnp.int32, (T, T), 0)
    c = lax.broadcasted_iota(jnp.int32, (T, T), 1)
    m = ((c >= r) if upper else (r >= c)).astype(BF16)
    hi, lo = _split(v)
    d = lambda u: lax.dot_general(m, u, (((1,), (0,)), ((), ())), preferred_element_type=F32)
    return d(hi) + d(lo)


@jax.custom_vjp
def cumsum_rows(v):
    return _tri_mm(v, False)


cumsum_rows.defvjp(lambda v: (_tri_mm(v, False), None), lambda _, g: (_tri_mm(g, True),))


def _rms(x, g):
    return x * lax.rsqrt(jnp.mean(x * x, axis=-1, keepdims=True) + EPS) * g


def _softmax(s):
    e = jnp.exp(s - lax.stop_gradient(jnp.max(s, axis=-1, keepdims=True)))
    return e / jnp.sum(e, axis=-1, keepdims=True)


def _softplus(x):
    return jnp.maximum(x, 0.0) + jnp.log(1.0 + jnp.exp(-jnp.abs(x)))


def _tile(n, cap, align):
    if n <= cap:
        return n
    best = None
    for d in range(align, cap + 1, align):
        if n % d == 0:
            best = d
    assert best is not None, (n, cap, align)
    return best


def matmul(mode, a, b, name, out_dtype=F32, add=None, norm_out=None, norm_bwd=None):
    if mode == "nn":
        (M, K), (K2, N) = a.shape, b.shape
    elif mode == "nt":
        (M, K), (N, K2) = a.shape, b.shape
    else:
        (K, M), (K2, N) = a.shape, b.shape
    assert K == K2, (mode, a.shape, b.shape)
    if mode != "tn" and norm_out is None and norm_bwd is None:
        tm, tn, tk = _tile(M, 1024, 128), _tile(N, 512, 128), _tile(K, 2048, 128)
    else:
        tm, tn, tk = _tile(M, 512, 128), _tile(N, 1536, 128), _tile(K, 2048, 128)
    nk = K // tk
    if mode == "nn":
        a_spec = pl.BlockSpec((tm, tk), lambda i, j, k: (i, k))
        b_spec = pl.BlockSpec((tk, tn), lambda i, j, k: (k, j))
        dn = (((1,), (0,)), ((), ()))
    elif mode == "nt":
        a_spec = pl.BlockSpec((tm, tk), lambda i, j, k: (i, k))
        b_spec = pl.BlockSpec((tn, tk), lambda i, j, k: (j, k))
        dn = (((1,), (1,)), ((), ()))
    else:
        a_spec = pl.BlockSpec((tk, tm), lambda i, j, k: (k, i))
        b_spec = pl.BlockSpec((tk, tn), lambda i, j, k: (k, j))
        dn = (((0,), (0,)), ((), ()))
    o_spec = pl.BlockSpec((tm, tn), lambda i, j, k: (i, j))
    has_add = add is not None
    g_spec = pl.BlockSpec((1, tn), lambda i, j, k: (0, j))
    extra, extra_specs, out_shapes, out_specs = [], [], [jax.ShapeDtypeStruct((M, N), out_dtype)], [o_spec]
    sem = ("parallel", "parallel", "arbitrary")
    if norm_out is not None:
        assert tn == N
        extra, extra_specs = [norm_out], [g_spec]
        out_shapes.append(jax.ShapeDtypeStruct((M, N), BF16))
        out_specs.append(o_spec)
    if norm_bwd is not None:
        assert tn == N
        x_in, g_in, res_in = norm_bwd
        extra, extra_specs = [x_in, g_in, res_in], [o_spec, g_spec, o_spec]
        out_shapes.append(jax.ShapeDtypeStruct((1, N), F32))
        out_specs.append(g_spec)
        sem = ("arbitrary", "arbitrary", "arbitrary")
    n_in = 2 + int(has_add) + len(extra)
    n_out = len(out_shapes)

    def body(*refs):
        a_ref, b_ref = refs[0], refs[1]
        add_ref = refs[2] if has_add else None
        extra_refs = refs[2 + int(has_add):n_in]
        o_ref = refs[n_in]
        acc_ref = refs[-1]
        i = pl.program_id(0)
        p = lax.dot_general(a_ref[...].astype(BF16), b_ref[...].astype(BF16), dn,
                            preferred_element_type=F32)

        def finish(total):
            if has_add:
                total = total + add_ref[...]
            if norm_out is not None:
                o_ref[...] = total.astype(o_ref.dtype)
                refs[n_in + 1][...] = _rms(total, extra_refs[0][...]).astype(BF16)
            elif norm_bwd is not None:
                x_ref, g_ref, res_ref = extra_refs
                _, vjp = jax.vjp(f_norm_res, x_ref[...], g_ref[...])
                dx, dg = vjp((total, res_ref[...]))
                o_ref[...] = dx
                dg_ref = refs[n_in + 1]

                @pl.when(i == 0)
                def _():
                    dg_ref[...] = dg

                @pl.when(i > 0)
                def _():
                    dg_ref[...] += dg
            else:
                o_ref[...] = total.astype(o_ref.dtype)

        if nk == 1:
            finish(p)
        else:
            k = pl.program_id(2)

            @pl.when(k == 0)
            def _():
                acc_ref[...] = p

            @pl.when(k > 0)
            def _():
                acc_ref[...] += p

            @pl.when(k == nk - 1)
            def _():
                finish(acc_ref[...])

    ins = [a, b] + ([add] if has_add else []) + extra
    specs = [a_spec, b_spec] + ([o_spec] if has_add else []) + extra_specs
    out = pl.pallas_call(
        body, name=name, grid=(M // tm, N // tn, nk), in_specs=specs, out_specs=out_specs,
        out_shape=out_shapes, scratch_shapes=[pltpu.VMEM((tm, tn), F32)],
        compiler_params=_cp(*sem),
    )(*ins)
    return out[0] if n_out == 1 else out


def ffn_in(h, wg, wu):
    (S, K), F = h.shape, wg.shape[0]
    tm, tn = _tile(S, 1024, 128), _tile(F, 512, 128)
    dn = (((1,), (1,)), ((), ()))

    def body(h_ref, wg_ref, wu_ref, g_ref, u_ref, act_ref):
        a = h_ref[...].astype(BF16)
        g = lax.dot_general(a, wg_ref[...].astype(BF16), dn, preferred_element_type=F32)
        u = lax.dot_general(a, wu_ref[...].astype(BF16), dn, preferred_element_type=F32)
        g_ref[...] = g.astype(BF16)
        u_ref[...] = u.astype(BF16)
        act_ref[...] = f_swiglu(g, u)[0].astype(BF16)

    w_spec = pl.BlockSpec((tn, K), lambda i, j: (j, 0))
    o_spec = pl.BlockSpec((tm, tn), lambda i, j: (i, j))
    sd = jax.ShapeDtypeStruct
    return pl.pallas_call(
        body, name="ffn_in", grid=(S // tm, F // tn),
        in_specs=[pl.BlockSpec((tm, K), lambda i, j: (i, 0)), w_spec, w_spec], out_specs=[o_spec] * 3,
        out_shape=[sd((S, F), BF16), sd((S, F), BF16), sd((S, F), BF16)],
        compiler_params=_cp("parallel", "parallel"),
    )(h, wg, wu)


def ffn_dact(dy, wd, gate, up):
    (S, K), F = dy.shape, wd.shape[0]
    tm, tn = _tile(S, 1024, 128), _tile(F, 512, 128)
    dn = (((1,), (1,)), ((), ()))

    def body(dy_ref, wd_ref, g_ref, u_ref, dg_ref, du_ref):
        dact = lax.dot_general(dy_ref[...].astype(BF16), wd_ref[...].astype(BF16), dn, preferred_element_type=F32)
        _, vjp = jax.vjp(f_swiglu, g_ref[...].astype(F32), u_ref[...].astype(F32))
        dg, du = vjp((dact,))
        dg_ref[...] = dg.astype(BF16)
        du_ref[...] = du.astype(BF16)

    o_spec = pl.BlockSpec((tm, tn), lambda i, j: (i, j))
    sd = jax.ShapeDtypeStruct
    return pl.pallas_call(
        body, name="ffn_dact", grid=(S // tm, F // tn),
        in_specs=[pl.BlockSpec((tm, K), lambda i, j: (i, 0)), pl.BlockSpec((tn, K), lambda i, j: (j, 0)),
                  o_spec, o_spec],
        out_specs=[o_spec] * 2, out_shape=[sd((S, F), BF16)] * 2,
        compiler_params=_cp("parallel", "parallel"),
    )(dy, wd, gate, up)


def _row_spec(arr, tile):
    if isinstance(arr, tuple):
        a, w, cb = arr[:3]
        ro = (arr[3] // tile) if len(arr) > 3 else 0
        assert len(arr) < 4 or arr[3] % tile == 0
        return a, pl.BlockSpec((tile, w), lambda i, cb=cb, ro=ro: (i + ro, cb)), (tile, w)
    if arr.ndim == 3:
        d0, _, d2 = arr.shape
        return arr, pl.BlockSpec((d0, tile, d2), lambda i: (0, i, 0)), (d0, tile, d2)
    return arr, pl.BlockSpec((tile, arr.shape[1]), lambda i: (i, 0)), (tile, arr.shape[1])


def _full_spec(arr):
    nd = arr.ndim
    return pl.BlockSpec(arr.shape, lambda i, nd=nd: (0,) * nd)


def _n_rows(arr):
    a = arr[0] if isinstance(arr, tuple) else arr
    return a.shape[1] if a.ndim == 3 else a.shape[0]


def _row_out_shape(shape_tail, n, dtype):
    if isinstance(shape_tail, tuple):
        d0, d2 = shape_tail
        return (jax.ShapeDtypeStruct((d0, n, d2), dtype),
                lambda tile: pl.BlockSpec((d0, tile, d2), lambda i: (0, i, 0)))
    return (jax.ShapeDtypeStruct((n, shape_tail), dtype),
            lambda tile: pl.BlockSpec((tile, shape_tail), lambda i: (i, 0)))


def _f32(v):
    return v.astype(F32) if v.dtype == BF16 else v


def row_fwd(name, fn, rows, fulls, outs, tile=ROW_TILE):
    n = _n_rows(rows[0])
    tile = min(tile, n)
    assert n % tile == 0, (name, n, tile)
    rs = [_row_spec(r, tile) for r in rows]
    os_ = [_row_out_shape(w, n, dt) for w, dt in outs]
    nr, nf = len(rows), len(fulls)

    def body(*refs):
        vals = [_f32(r[...]) for r in refs[:nr + nf]]
        res = fn(*vals)
        for r, v in zip(refs[nr + nf:], res):
            r[...] = v.astype(r.dtype)

    out = pl.pallas_call(
        body, name=name, grid=(n // tile,),
        in_specs=[s for _, s, _ in rs] + [_full_spec(f) for f in fulls],
        out_specs=[mk(tile) for _, mk in os_], out_shape=[sh for sh, _ in os_],
        compiler_params=_cp("parallel"),
    )(*[a for a, _, _ in rs], *fulls)
    return out


def row_bwd(name, fn, rows, fulls, cts, want_rows, want_fulls, row_dtypes=None, tile=ROW_TILE):
    n = _n_rows(rows[0])
    tile = min(tile, n)
    assert n % tile == 0, (name, n, tile)
    rs = [_row_spec(r, tile) for r in rows]
    cs = [_row_spec(c, tile) for c in cts]
    nr, nf, nc = len(rows), len(fulls), len(cts)
    row_dtypes = row_dtypes or [F32] * len(want_rows)
    out_shapes, out_specs = [], []
    for k, idx in enumerate(want_rows):
        a, _, blk = rs[idx]
        if len(blk) == 3:
            sh, mk = _row_out_shape((blk[0], blk[2]), n, row_dtypes[k])
        else:
            sh, mk = _row_out_shape(blk[1], n, row_dtypes[k])
        out_shapes.append(sh)
        out_specs.append(mk(tile))
    for idx in want_fulls:
        out_shapes.append(jax.ShapeDtypeStruct(fulls[idx].shape, F32))
        out_specs.append(_full_spec(fulls[idx]))
    n_wr = len(want_rows)

    def body(*refs):
        i = pl.program_id(0)
        vals = [_f32(r[...]) for r in refs[:nr + nf]]
        ct_vals = [_f32(r[...]) for r in refs[nr + nf:nr + nf + nc]]
        outs = refs[nr + nf + nc:]
        _, vjp = jax.vjp(fn, *vals)
        grads = vjp(tuple(ct_vals))
        for k, idx in enumerate(want_rows):
            outs[k][...] = grads[idx].astype(outs[k].dtype)
        for k, idx in enumerate(want_fulls):
            o = outs[n_wr + k]
            g = grads[nr + idx]

            @pl.when(i == 0)
            def _(o=o, g=g):
                o[...] = g

            @pl.when(i > 0)
            def _(o=o, g=g):
                o[...] += g

    out = pl.pallas_call(
        body, name=name, grid=(n // tile,),
        in_specs=[s for _, s, _ in rs] + [_full_spec(f) for f in fulls] + [s for _, s, _ in cs],
        out_specs=out_specs, out_shape=out_shapes,
        compiler_params=_cp("arbitrary"),
    )(*[a for a, _, _ in rs], *fulls, *[a for a, _, _ in cs])
    return out


def whole(name, fn, args, outs):
    def body(*refs):
        res = fn(*[r[...] for r in refs[:len(args)]])
        for r, v in zip(refs[len(args):], res):
            r[...] = v.astype(r.dtype)

    return pl.pallas_call(
        body, name=name, out_shape=[jax.ShapeDtypeStruct(s, d) for s, d in outs],
        compiler_params=pltpu.CompilerParams(vmem_limit_bytes=VMEM_LIMIT),
    )(*args)


def f_norm(x, g):
    return (_rms(x, g),)


def f_norm_res(x, g):
    return _rms(x, g), x


def f_swiglu(g, u):
    return (g * jax.nn.sigmoid(g) * u,)


def f_glu(y, w, b):
    h = jax.nn.gelu(y)
    return (h * jax.nn.sigmoid(mm(h, w) + b),)


def f_xattn(q, kv, qg, kg):
    outs = []
    for h in range(XA_H):
        sl = slice(h * XA_D, (h + 1) * XA_D)
        qn = _rms(q[:, sl], qg)
        kn = _rms(kv[:, sl], kg)
        vh = kv[:, D + h * XA_D:D + (h + 1) * XA_D]
        p = _softmax(mm_nt(qn, kn) * (XA_D ** -0.5))
        outs.append(mm(p, vh))
    return (jnp.concatenate(outs, axis=-1),)


def f_adam(w, g, m, v):
    m2 = B1 * m + (1.0 - B1) * g
    v2 = B2 * v + (1.0 - B2) * (g * g)
    m_hat = m2 / (1.0 - B1 ** STEP)
    v_hat = v2 / (1.0 - B2 ** STEP)
    delta = -LR * (m_hat / (jnp.sqrt(v_hat) + AEPS) + WD * w)
    return delta, m2, v2


S5_NTAB, S5_NROW = 4, 6


def _s5_tables(are, aim, ldt):
    T = CHUNK
    dt = jnp.exp(ldt)
    ar, ai = are * dt, aim * dt
    t = lax.broadcasted_iota(jnp.int32, (T, 1), 0).astype(F32)
    mag, inv = jnp.exp(t * ar), jnp.exp(-t * ar)
    cs, sn = jnp.cos(t * ai), jnp.sin(t * ai)
    e_re, e_im = mag * cs, mag * sn
    n_re, n_im = inv * cs, -inv * sn
    l_re, l_im = jnp.exp(ar) * jnp.cos(ai), jnp.exp(ar) * jnp.sin(ai)
    den = are * are + aim * aim
    k_re = ((l_re - 1.0) * are + l_im * aim) / den
    k_im = (l_im * are - (l_re - 1.0) * aim) / den
    tl = float(T - 1)
    m_re, m_im = jnp.exp(tl * ar) * jnp.cos(tl * ai), jnp.exp(tl * ar) * jnp.sin(tl * ai)
    return (e_re, e_im, n_re, n_im), (l_re, l_im, k_re, k_im, m_re, m_im)


def _s5_chunk(u, sre, sim, tabs, rows, b_re, b_im, c_re, c_im, dv):
    e_re, e_im, n_re, n_im = tabs
    l_re, l_im, k_re, k_im, m_re, m_im = rows
    x_re, x_im = mm(u, b_re), mm(u, b_im)
    bu_re = k_re * x_re - k_im * x_im
    bu_im = k_re * x_im + k_im * x_re
    v_re = bu_re * n_re - bu_im * n_im
    v_im = bu_re * n_im + bu_im * n_re
    p_re = l_re * sre - l_im * sim
    p_im = l_re * sim + l_im * sre
    w_re = cumsum_rows(v_re) + p_re
    w_im = cumsum_rows(v_im) + p_im
    s_re = e_re * w_re - e_im * w_im
    s_im = e_re * w_im + e_im * w_re
    y = mm(s_re, c_re) - mm(s_im, c_im) + dv * u
    z_re = jnp.sum(v_re, axis=0, keepdims=True) + p_re
    z_im = jnp.sum(v_im, axis=0, keepdims=True) + p_im
    return y, m_re * z_re - m_im * z_im, m_re * z_im + m_im * z_re


def _s5_fill_tables(are_ref, aim_ref, ldt_ref, tab, row):
    for g in range(S5_GB):
        ls = slice(512 * g, 512 * (g + 1))
        tabs, rows = _s5_tables(are_ref[:, ls], aim_ref[:, ls], ldt_ref[:, ls])
        for k, t in enumerate(tabs):
            tab[k, :, ls] = t
        for k, r in enumerate(rows):
            row[k:k + 1, ls] = r


def _s5_read_tables(tab, row, ls):
    return (tuple(tab[k, :, ls] for k in range(S5_NTAB)), tuple(row[k:k + 1, ls] for k in range(S5_NROW)))


def _s5_specs(nc, rev):
    T = CHUNK

    def ci(c):
        return nc - 1 - c if rev else c

    u_spec = pl.BlockSpec((T, S5_W), lambda c: (ci(c), 0))
    p_spec = pl.BlockSpec((1, S5_G * S5_P), lambda c: (0, 0))
    b_spec = pl.BlockSpec((S5_GB, 128, 512), lambda c: (0, 0, 0))
    c_spec = pl.BlockSpec((S5_GB, 512, 128), lambda c: (0, 0, 0))
    d_spec = pl.BlockSpec((1, S5_W), lambda c: (0, 0))
    st_spec = pl.BlockSpec((None, 2, S5_G * S5_P), lambda c: (ci(c), 0, 0))
    return u_spec, p_spec, b_spec, c_spec, d_spec, st_spec


def s5_fwd(pm, are, aim, ldt, b_re, b_im, c_re, c_im, dv):
    S = pm.shape[0]
    nc = S // CHUNK
    u_spec, p_spec, b_spec, c_spec, d_spec, st_spec = _s5_specs(nc, False)

    def body(u_ref, are_ref, aim_ref, ldt_ref, bre_ref, bim_ref, cre_ref, cim_ref, dv_ref,
             y_ref, st_ref, state, tab, row):
        c = pl.program_id(0)

        @pl.when(c == 0)
        def _():
            state[...] = jnp.zeros_like(state)
            _s5_fill_tables(are_ref, aim_ref, ldt_ref, tab, row)

        st_ref[...] = state[...]
        for g in range(S5_GB):
            lu, ls = slice(128 * g, 128 * (g + 1)), slice(512 * g, 512 * (g + 1))
            tabs, rows = _s5_read_tables(tab, row, ls)
            y, e_re, e_im = _s5_chunk(u_ref[:, lu], state[0:1, ls], state[1:2, ls], tabs, rows,
                                      bre_ref[g], bim_ref[g], cre_ref[g], cim_ref[g], dv_ref[:, lu])
            y_ref[:, lu] = y
            state[0:1, ls] = e_re
            state[1:2, ls] = e_im

    n_state = S5_G * S5_P
    return pl.pallas_call(
        body, name="s5_fwd", grid=(nc,),
        in_specs=[u_spec, p_spec, p_spec, p_spec, b_spec, b_spec, c_spec, c_spec, d_spec],
        out_specs=[u_spec, st_spec],
        out_shape=[jax.ShapeDtypeStruct((S, S5_W), F32), jax.ShapeDtypeStruct((nc, 2, n_state), F32)],
        scratch_shapes=[pltpu.VMEM((2, n_state), F32), pltpu.VMEM((S5_NTAB, CHUNK, n_state), F32),
                        pltpu.VMEM((8, n_state), F32)],
        compiler_params=_cp("arbitrary"),
    )(pm, are, aim, ldt, b_re, b_im, c_re, c_im, dv)


def s5_bwd(pm, st, dy, are, aim, ldt, b_re, b_im, c_re, c_im, dv):
    S = pm.shape[0]
    nc = S // CHUNK
    u_spec, p_spec, b_spec, c_spec, d_spec, st_spec = _s5_specs(nc, True)

    def body(u_ref, st_ref, dy_ref, are_ref, aim_ref, ldt_ref, bre_ref, bim_ref, cre_ref, cim_ref, dv_ref,
             du_ref, dare_ref, daim_ref, dldt_ref, dbre_ref, dbim_ref, dcre_ref, dcim_ref, ddv_ref,
             dstate, tab, row, dtab, drow):
        c = pl.program_id(0)

        @pl.when(c == 0)
        def _():
            dstate[...] = jnp.zeros_like(dstate)
            dtab[...] = jnp.zeros_like(dtab)
            drow[...] = jnp.zeros_like(drow)
            _s5_fill_tables(are_ref, aim_ref, ldt_ref, tab, row)

        for g in range(S5_GB):
            lu, ls = slice(128 * g, 128 * (g + 1)), slice(512 * g, 512 * (g + 1))
            every = slice(None)
            tabs, rows = _s5_read_tables(tab, row, ls)
            args = (u_ref[:, lu], st_ref[0:1, ls], st_ref[1:2, ls], tabs, rows,
                    bre_ref[g], bim_ref[g], cre_ref[g], cim_ref[g], dv_ref[:, lu])
            _, vjp = jax.vjp(_s5_chunk, *args)
            gr = vjp((dy_ref[:, lu], dstate[0:1, ls], dstate[1:2, ls]))
            du_ref[:, lu] = gr[0]
            dstate[0:1, ls] = gr[1]
            dstate[1:2, ls] = gr[2]
            for k, t in enumerate(gr[3]):
                dtab[k, :, ls] += t
            for k, r in enumerate(gr[4]):
                drow[k:k + 1, ls] += r
            accs = ((dbre_ref, (g,)), (dbim_ref, (g,)), (dcre_ref, (g,)), (dcim_ref, (g,)), (ddv_ref, (every, lu)))
            for (o, idx), gv in zip(accs, gr[5:]):
                @pl.when(c == 0)
                def _(o=o, idx=idx, gv=gv):
                    o[idx] = gv

                @pl.when(c > 0)
                def _(o=o, idx=idx, gv=gv):
                    o[idx] += gv

        @pl.when(c == nc - 1)
        def _():
            for g in range(S5_GB):
                ls = slice(512 * g, 512 * (g + 1))
                _, vjp = jax.vjp(_s5_tables, are_ref[:, ls], aim_ref[:, ls], ldt_ref[:, ls])
                dtabs, drows = _s5_read_tables(dtab, drow, ls)
                ga, gi, gl = vjp((dtabs, drows))
                dare_ref[:, ls] = ga
                daim_ref[:, ls] = gi
                dldt_ref[:, ls] = gl

    n_state = S5_G * S5_P
    return pl.pallas_call(
        body, name="s5_bwd", grid=(nc,),
        in_specs=[u_spec, st_spec, u_spec, p_spec, p_spec, p_spec, b_spec, b_spec, c_spec, c_spec, d_spec],
        out_specs=[u_spec, p_spec, p_spec, p_spec, b_spec, b_spec, c_spec, c_spec, d_spec],
        out_shape=[jax.ShapeDtypeStruct((S, S5_W), F32)] + [jax.ShapeDtypeStruct((1, n_state), F32)] * 3
        + [jax.ShapeDtypeStruct((S5_GB, 128, 512), F32)] * 2 + [jax.ShapeDtypeStruct((S5_GB, 512, 128), F32)] * 2
        + [jax.ShapeDtypeStruct((1, S5_W), F32)],
        scratch_shapes=[pltpu.VMEM((2, n_state), F32), pltpu.VMEM((S5_NTAB, CHUNK, n_state), F32),
                        pltpu.VMEM((8, n_state), F32), pltpu.VMEM((S5_NTAB, CHUNK, n_state), F32),
                        pltpu.VMEM((8, n_state), F32)],
        compiler_params=_cp("arbitrary"),
    )(pm, st, dy, are, aim, ldt, b_re, b_im, c_re, c_im, dv)


def conv_fwd(pm, w):
    S = pm.shape[0]

    def body(x_ref, w_ref, o_ref, pad):
        x = x_ref[...]
        pad[0:8, :] = jnp.zeros((8, 128), F32)
        pad[8:, :] = x
        y = (w_ref[3:4, :] * x + w_ref[2:3, :] * pad[7:7 + S, :] + w_ref[1:2, :] * pad[6:6 + S, :]
             + w_ref[0:1, :] * pad[5:5 + S, :])
        o_ref[...] = y * jax.nn.sigmoid(y)

    return pl.pallas_call(
        body, name="conv_fwd", grid=(12,),
        in_specs=[pl.BlockSpec((S, 128), lambda j: (0, 4 + j)), pl.BlockSpec((4, 128), lambda j: (0, j))],
        out_specs=pl.BlockSpec((S, 128), lambda j: (0, j)),
        out_shape=jax.ShapeDtypeStruct((S, 1536), F32),
        scratch_shapes=[pltpu.VMEM((S + 8, 128), F32)],
        compiler_params=_cp("parallel"),
    )(pm, w)


def conv_bwd(pm, w, dout):
    S = pm.shape[0]

    def body(x_ref, w_ref, do_ref, dx_ref, dw_ref, pad, dpad):
        x = x_ref[...]
        pad[0:8, :] = jnp.zeros((8, 128), F32)
        pad[8:, :] = x
        xs = [pad[5:5 + S, :], pad[6:6 + S, :], pad[7:7 + S, :], x]
        y = w_ref[0:1, :] * xs[0] + w_ref[1:2, :] * xs[1] + w_ref[2:3, :] * xs[2] + w_ref[3:4, :] * xs[3]
        sg = jax.nn.sigmoid(y)
        dy = do_ref[...] * (sg + y * sg * (1.0 - sg))
        dpad[0:S, :] = dy
        dpad[S:, :] = jnp.zeros((8, 128), F32)
        dx_ref[...] = (w_ref[3:4, :] * dy + w_ref[2:3, :] * dpad[1:1 + S, :] + w_ref[1:2, :] * dpad[2:2 + S, :]
                       + w_ref[0:1, :] * dpad[3:3 + S, :])
        for i in range(4):
            dw_ref[i:i + 1, :] = jnp.sum(dy * xs[i], axis=0, keepdims=True)

    return pl.pallas_call(
        body, name="conv_bwd", grid=(12,),
        in_specs=[pl.BlockSpec((S, 128), lambda j: (0, 4 + j)), pl.BlockSpec((4, 128), lambda j: (0, j)),
                  pl.BlockSpec((S, 128), lambda j: (0, j))],
        out_specs=[pl.BlockSpec((S, 128), lambda j: (0, j)), pl.BlockSpec((4, 128), lambda j: (0, j))],
        out_shape=[jax.ShapeDtypeStruct((S, 1536), F32), jax.ShapeDtypeStruct((4, 1536), F32)],
        scratch_shapes=[pltpu.VMEM((S + 8, 128), F32), pltpu.VMEM((S + 8, 128), F32)],
        compiler_params=_cp("parallel"),
    )(pm, w, dout)


GDN_SUP = 4
GDN_ROWS = GDN_SUP * CHUNK


@jax.custom_vjp
def _saved_inverse(a, x):
    return x


_saved_inverse.defvjp(lambda a, x: (x, x),
                      lambda x, g: (-mm_tn(x, mm_nt(g, x)), jnp.zeros_like(x)))


def _gdn_chunk(q, k, v, gate, al, bl, alog, dtb, og, state, inv=None, want_inv=False):
    R = q.shape[0]
    r = lax.broadcasted_iota(jnp.int32, (R, R), 0)
    c = lax.broadcasted_iota(jnp.int32, (R, R), 1)
    same = (r // CHUNK) == (c // CHUNK)
    eye = (r == c).astype(F32)
    strict, causal, upper = same & (r > c), same & (r >= c), same & (r <= c)
    qn = q * lax.rsqrt(jnp.sum(q * q, axis=-1, keepdims=True) + EPS) * (GDN_D ** -0.5)
    kn = k * lax.rsqrt(jnp.sum(k * k, axis=-1, keepdims=True) + EPS)
    beta = jax.nn.sigmoid(bl)
    g = -jnp.exp(alog) * _softplus(al + dtb)
    g_row = jnp.sum(eye * g, axis=0, keepdims=True)
    gc_col = jnp.sum(jnp.where(causal, g_row, 0.0), axis=1, keepdims=True)
    gc_row = jnp.sum(jnp.where(upper, g, 0.0), axis=0, keepdims=True)
    gtot = jnp.sum(jnp.where(same, g_row, 0.0), axis=1, keepdims=True)
    gamma = jnp.exp(gc_col)
    diff = gc_col - gc_row
    d_strict = jnp.where(strict, jnp.exp(jnp.where(strict, diff, 0.0)), 0.0)
    d_causal = jnp.where(causal, jnp.exp(jnp.where(causal, diff, 0.0)), 0.0)
    a = beta * mm_nt(kn, kn) * d_strict
    if inv is None:
        p = -a
        x = eye + p
        for _ in range(5):
            p = mm(p, p)
            x = x + mm(x, p)
    else:
        x = _saved_inverse(a, inv)
    u_new = mm(x, beta * v)
    w_k = mm(x, (beta * gamma) * kn)
    qk = mm_nt(qn, kn) * d_causal
    q_g = qn * gamma
    k_tail = kn * jnp.exp(gtot - gc_col)
    ws, os_ = [], []
    for i in range(R // CHUNK):
        rows = slice(CHUNK * i, CHUNK * (i + 1))
        w_i = u_new[rows] - mm(w_k[rows], state)
        os_.append(mm(q_g[rows], state))
        decay = jnp.exp(jnp.sum(g[rows], axis=0, keepdims=True))
        state = decay * state + mm_tn(k_tail[rows], w_i)
        ws.append(w_i)
    o = jnp.concatenate(os_, axis=0) + mm(qk, jnp.concatenate(ws, axis=0))
    out = _rms(o, og) * (gate * jax.nn.sigmoid(gate))
    return (out, state, x) if want_inv else (out, state)


def _gdn_specs(nc, rev):
    def ci(c):
        return nc - 1 - c if rev else c

    def blk(cb):
        return pl.BlockSpec((GDN_ROWS, 512), lambda c: (ci(c), cb))

    col = lambda n: pl.BlockSpec((n, GDN_ROWS, 1), lambda c: (0, ci(c), 0))
    sc = pl.BlockSpec((GDN_H, 1, 1), lambda c: (0, 0, 0))
    og = pl.BlockSpec((1, 128), lambda c: (0, 0))
    st = pl.BlockSpec((GDN_H, None, 128, 128), lambda c: (0, ci(c), 0, 0))
    return blk, col, sc, og, st


def gdn_fwd(qkvc, pm, abt, alog, dtb, og):
    S = qkvc.shape[0]
    nc = S // GDN_ROWS
    blk, col, sc, ogs, st = _gdn_specs(nc, False)

    def body(q_ref, k_ref, v_ref, gate_ref, ab_ref, alog_ref, dtb_ref, og_ref, o_ref, st_ref, inv_ref, state):
        c = pl.program_id(0)

        @pl.when(c == 0)
        def _():
            state[...] = jnp.zeros_like(state)

        st_ref[...] = state[...]
        for h in range(GDN_H):
            sl = slice(GDN_D * h, GDN_D * (h + 1))
            out, new_state, inv = _gdn_chunk(
                q_ref[:, sl], k_ref[:, sl], v_ref[:, sl], gate_ref[:, sl], ab_ref[h], ab_ref[GDN_H + h],
                alog_ref[h], dtb_ref[h], og_ref[...], state[h], want_inv=True)
            o_ref[:, sl] = out
            state[h] = new_state
            inv_ref[h] = inv

    inv_spec = pl.BlockSpec((GDN_H, None, GDN_ROWS, GDN_ROWS), lambda c: (0, c, 0, 0))
    return pl.pallas_call(
        body, name="gdn_fwd", grid=(nc,),
        in_specs=[blk(0), blk(1), blk(2), blk(4), col(2 * GDN_H), sc, sc, ogs],
        out_specs=[blk(0), st, inv_spec],
        out_shape=[jax.ShapeDtypeStruct((S, 512), F32), jax.ShapeDtypeStruct((GDN_H, nc, 128, 128), F32),
                   jax.ShapeDtypeStruct((GDN_H, nc, GDN_ROWS, GDN_ROWS), F32)],
        scratch_shapes=[pltpu.VMEM((GDN_H, 128, 128), F32)],
        compiler_params=_cp("arbitrary"),
    )(qkvc, qkvc, qkvc, pm, abt, alog, dtb, og)


def gdn_bwd(qkvc, pm, abt, alog, dtb, og, st, inv, dout):
    S = qkvc.shape[0]
    nc = S // GDN_ROWS
    blk, col, sc, ogs, sts = _gdn_specs(nc, True)

    def body(q_ref, k_ref, v_ref, gate_ref, ab_ref, alog_ref, dtb_ref, og_ref, st_ref, inv_ref, do_ref,
             dq_ref, dk_ref, dv_ref, dgate_ref, dal_ref, dbl_ref, dalog_ref, ddtb_ref, dog_ref, dstate):
        c = pl.program_id(0)

        @pl.when(c == 0)
        def _():
            dstate[...] = jnp.zeros_like(dstate)

        for h in range(GDN_H):
            sl = slice(GDN_D * h, GDN_D * (h + 1))
            args = (q_ref[:, sl], k_ref[:, sl], v_ref[:, sl], gate_ref[:, sl], ab_ref[h], ab_ref[GDN_H + h],
                    alog_ref[h], dtb_ref[h], og_ref[...], st_ref[h])
            inv_h = inv_ref[h]
            _, vjp = jax.vjp(lambda *a, inv_h=inv_h: _gdn_chunk(*a, inv=inv_h), *args)
            g = vjp((do_ref[:, sl], dstate[h]))
            for o, gv in zip((dq_ref, dk_ref, dv_ref, dgate_ref), g[:4]):
                o[:, sl] = gv
            dal_ref[h] = g[4]
            dbl_ref[h] = g[5]
            dstate[h] = g[9]
            for o, gv in zip((dalog_ref, ddtb_ref, dog_ref), g[6:9]):
                @pl.when(c == 0)
                def _(o=o, gv=gv, h=h):
                    o[h] = gv

                @pl.when(c > 0)
                def _(o=o, gv=gv, h=h):
                    o[h] += gv

    ogo = pl.BlockSpec((GDN_H, 1, 128), lambda c: (0, 0, 0))
    inv_spec = pl.BlockSpec((GDN_H, None, GDN_ROWS, GDN_ROWS), lambda c: (0, nc - 1 - c, 0, 0))
    sd = jax.ShapeDtypeStruct
    return pl.pallas_call(
        body, name="gdn_bwd", grid=(nc,),
        in_specs=[blk(0), blk(1), blk(2), blk(4), col(2 * GDN_H), sc, sc, ogs, sts, inv_spec, blk(0)],
        out_specs=[blk(0), blk(0), blk(0), blk(0), col(GDN_H), col(GDN_H), sc, sc, ogo],
        out_shape=[sd((S, 512), F32)] * 4 + [sd((GDN_H, S, 1), F32)] * 2 + [sd((GDN_H, 1, 1), F32)] * 2
        + [sd((GDN_H, 1, 128), F32)],
        scratch_shapes=[pltpu.VMEM((GDN_H, 128, 128), F32)],
        compiler_params=_cp("arbitrary"),
    )(qkvc, qkvc, qkvc, pm, abt, alog, dtb, og, st, inv, dout)


HG = 8
HG_LANES = HG * CA_D


def _group_mean_raw(y):
    r = lax.broadcasted_iota(jnp.int32, (128, 128), 0)
    c = lax.broadcasted_iota(jnp.int32, (128, 128), 1)
    g = jnp.where((r // CA_D) == (c // CA_D), 1.0 / CA_D, 0.0).astype(BF16)
    d = lambda u: lax.dot_general(u, g, (((1,), (0,)), ((), ())), preferred_element_type=F32)
    outs = []
    for j in range(y.shape[1] // 128):
        hi, lo = _split(y[:, 128 * j:128 * (j + 1)])
        outs.append(d(hi) + d(lo))
    return jnp.concatenate(outs, axis=1)


@jax.custom_vjp
def group_mean(y):
    return _group_mean_raw(y)


group_mean.defvjp(lambda y: (_group_mean_raw(y), None), lambda _, g: (_group_mean_raw(g),))


def f_headnorm(t, g):
    return (t * lax.rsqrt(group_mean(t * t) + EPS) * g,)


def _cattn_chunk(q, kb, vb, bias, valid):
    lane = lax.broadcasted_iota(jnp.int32, (1, 128), 1)
    m0 = (lane < CA_D).astype(F32)
    m1 = 1.0 - m0
    pairs = range(HG // 2)
    sl = [slice(128 * p, 128 * (p + 1)) for p in pairs]
    q2 = [jnp.concatenate([q[:, s] * m0, q[:, s] * m1], axis=0) for s in sl]
    sc = [mm_nt(q2[p], kb[:, sl[p]]) * (CA_D ** -0.5) + bias[sl[p]] for p in pairs]
    pr = [_softmax(jnp.where(valid, s, -1e30)) for s in sc]
    o2 = [mm(pr[p], vb[:, sl[p]]) for p in pairs]
    return jnp.concatenate([o[:CHUNK] * m0 + o[CHUNK:] * m1 for o in o2], axis=1)


def _cattn_valid(c):
    pos = lax.broadcasted_iota(jnp.int32, (1, CA_BAND), 1) + c * CHUNK
    return pos >= CA_PAD


def _cattn_specs(S):
    q_spec = pl.BlockSpec((CHUNK, HG_LANES), lambda h, c: (c, h))
    kv_spec = pl.BlockSpec((S + CA_PAD, HG_LANES), lambda h, c: (0, h))
    b_spec = pl.BlockSpec((HG * CHUNK, CA_BAND), lambda h, c: (h, 0))
    return q_spec, kv_spec, b_spec


def cattn_fwd(qn, kp, vp, bias):
    S = qn.shape[0]
    nc = S // CHUNK
    q_spec, kv_spec, b_spec = _cattn_specs(S)

    def body(q_ref, k_ref, v_ref, b_ref, o_ref):
        c = pl.program_id(1)
        start = pl.multiple_of(c * CHUNK, CHUNK)
        kb = k_ref[pl.ds(start, CA_BAND), :]
        vb = v_ref[pl.ds(start, CA_BAND), :]
        o_ref[...] = _cattn_chunk(q_ref[...], kb, vb, b_ref[...], _cattn_valid(c)).astype(o_ref.dtype)

    return pl.pallas_call(
        body, name="cattn_fwd", grid=(CA_H // HG, nc), in_specs=[q_spec, kv_spec, kv_spec, b_spec],
        out_specs=q_spec, out_shape=jax.ShapeDtypeStruct((S, D), BF16),
        compiler_params=_cp("parallel", "arbitrary"),
    )(qn, kp, vp, bias)


def kv_prep(qkv, kg):
    S = qkv.shape[0]
    tile = ROW_TILE
    lead = CA_PAD // tile

    def body(k_ref, v_ref, g_ref, kp_ref, vp_ref):
        i = pl.program_id(0)

        @pl.when(i < lead)
        def _():
            kp_ref[...] = jnp.zeros_like(kp_ref)
            vp_ref[...] = jnp.zeros_like(vp_ref)

        @pl.when(i >= lead)
        def _():
            kp_ref[...] = f_headnorm(k_ref[...], g_ref[...])[0].astype(BF16)
            vp_ref[...] = v_ref[...].astype(BF16)

    src = lambda cb: pl.BlockSpec((tile, D), lambda i, cb=cb: (jnp.maximum(i - lead, 0), cb))
    out = pl.BlockSpec((tile, D), lambda i: (i, 0))
    return pl.pallas_call(
        body, name="kv_prep", grid=((S + CA_PAD) // tile,),
        in_specs=[src(1), src(2), pl.BlockSpec((1, D), lambda i: (0, 0))], out_specs=[out, out],
        out_shape=[jax.ShapeDtypeStruct((S + CA_PAD, D), BF16)] * 2,
        compiler_params=_cp("parallel"),
    )(qkv, qkv, kg)


def cattn_bwd(qn, kp, vp, bias, do):
    S = qn.shape[0]
    nc = S // CHUNK
    q_spec, kv_spec, b_spec = _cattn_specs(S)

    def body(q_ref, k_ref, v_ref, b_ref, do_ref, dq_ref, dk_ref, dv_ref, db_ref):
        c = pl.program_id(1)

        @pl.when(c == 0)
        def _():
            dk_ref[...] = jnp.zeros_like(dk_ref)
            dv_ref[...] = jnp.zeros_like(dv_ref)
            db_ref[...] = jnp.zeros_like(db_ref)

        start = pl.multiple_of(c * CHUNK, CHUNK)
        kb = k_ref[pl.ds(start, CA_BAND), :].astype(F32)
        vb = v_ref[pl.ds(start, CA_BAND), :].astype(F32)
        valid = _cattn_valid(c)
        _, vjp = jax.vjp(lambda q, k, v, b: _cattn_chunk(q, k, v, b, valid), q_ref[...], kb, vb, b_ref[...])
        dq, dk, dv, db = vjp(do_ref[...])
        dq_ref[...] = dq
        dk_ref[pl.ds(start, CA_BAND), :] += dk
        dv_ref[pl.ds(start, CA_BAND), :] += dv
        db_ref[...] += db

    sd = jax.ShapeDtypeStruct
    return pl.pallas_call(
        body, name="cattn_bwd", grid=(CA_H // HG, nc), in_specs=[q_spec, kv_spec, kv_spec, b_spec, q_spec],
        out_specs=[q_spec, kv_spec, kv_spec, b_spec],
        out_shape=[sd((S, D), F32), sd((S + CA_PAD, D), F32), sd((S + CA_PAD, D), F32),
                   sd((CA_H * CHUNK, CA_BAND), F32)],
        compiler_params=_cp("parallel", "arbitrary"),
    )(qn, kp, vp, bias, do)


_REL_IDX = np.clip(np.arange(CHUNK)[:, None] - np.arange(CA_BAND)[None, :] + CA_PAD, -MAX_REL, MAX_REL) + MAX_REL
SKEW_W = CA_BAND + CHUNK


def rel_bias_grad(dbias):
    padded = jnp.pad(dbias, ((0, 0), (0, 0), (CHUNK, 0)))
    flat = jnp.pad(padded.reshape(CA_H, CHUNK * SKEW_W), ((0, 0), (0, CHUNK)))
    skew = flat.reshape(CA_H, CHUNK, SKEW_W + 1)

    first_near = SKEW_W - CHUNK - MAX_REL

    def fn(t):
        colsum = jnp.sum(t, axis=1, keepdims=True)
        j = lax.broadcasted_iota(jnp.int32, colsum.shape, 2)
        far = jnp.sum(jnp.where(j < first_near, colsum, 0.0), axis=2, keepdims=True)
        return (colsum + jnp.where(j == first_near, far, 0.0),)

    (colsum,) = whole("relbias_sum", fn, [skew], [((CA_H, 1, SKEW_W + 1), F32)])
    near = colsum[:, 0, first_near:SKEW_W][:, ::-1]
    return jnp.concatenate([jnp.zeros((CA_H, CHUNK + 1), F32), near], axis=1)


def rel_bias_expand(rb):
    near = rb[:, CHUNK + 1:][:, ::-1]
    far = jnp.broadcast_to(rb[:, 2 * MAX_REL:], (CA_H, SKEW_W - CHUNK - MAX_REL))
    t = jnp.concatenate([far, near, jnp.zeros((CA_H, 1), rb.dtype)], axis=1)
    rows = jnp.tile(t, (1, CHUNK))[:, :CHUNK * SKEW_W].reshape(CA_H, CHUNK, SKEW_W)
    return rows[:, :, CHUNK:]


def loss_head(y, target):
    S = y.shape[0]
    tile = min(ROW_TILE, S)

    def body(y_ref, t_ref, dy_ref, acc_ref):
        i = pl.program_id(0)
        e = y_ref[...] - t_ref[...]
        dy_ref[...] = e * (1.0 / D)
        part = jnp.sum(e * e, axis=0, keepdims=True) * (0.5 / D)

        @pl.when(i == 0)
        def _():
            acc_ref[...] = part

        @pl.when(i > 0)
        def _():
            acc_ref[...] += part

    row = pl.BlockSpec((tile, D), lambda i: (i, 0))
    return pl.pallas_call(
        body, name="loss_head", grid=(S // tile,), in_specs=[row, row],
        out_specs=[row, pl.BlockSpec((1, D), lambda i: (0, 0))],
        out_shape=[jax.ShapeDtypeStruct((S, D), F32), jax.ShapeDtypeStruct((1, D), F32)],
        compiler_params=_cp("arbitrary"),
    )(y, target)


ANY = pl.BlockSpec(memory_space=pl.ANY)


HBM = pl.BlockSpec(memory_space=pltpu.HBM)
SEM = pl.BlockSpec(memory_space=pltpu.SEMAPHORE)
EFFECT = pltpu.SideEffectType.DATAFLOW_SIDE_EFFECTING


def _chip_copies(src_ref, land_ref, send_sems, recv_sems, per_dest):
    x, y, c = lax.axis_index("x"), lax.axis_index("y"), lax.axis_index("c")
    me = 2 * x + y
    if per_dest == "sibling":
        cp = pltpu.make_async_remote_copy(src_ref=src_ref, dst_ref=land_ref, send_sem=send_sems.at[0],
                                          recv_sem=recv_sems.at[0], device_id=(x, y, 1 - c), device_id_type=MESH)
        return [(cp, cp)]
    out = []
    for j, (px, py) in enumerate([(1 - x, y), (x, 1 - y), (1 - x, 1 - y)]):
        peer = 2 * px + py
        if per_dest:
            send = (src_ref.at[peer], land_ref.at[j])
            recv = (src_ref.at[me], land_ref.at[j])
        else:
            send = (src_ref, land_ref.at[me])
            recv = (src_ref, land_ref.at[peer])
        mk = lambda s, d, j=j, px=px, py=py: pltpu.make_async_remote_copy(
            src_ref=s, dst_ref=d, send_sem=send_sems.at[j], recv_sem=recv_sems.at[j],
            device_id=(px, py, c), device_id_type=MESH)
        out.append((mk(*send), mk(*recv)))
    return out


def exchange_start(src, land, carry, name, per_dest):
    def body(src_ref, land_ref, carry_ref, send_sems, recv_sems, src_out, land_out, carry_out):
        for send, _ in _chip_copies(src_ref, land_ref, send_sems, recv_sems, per_dest):
            send.start()

    hbm = lambda a: pltpu.HBM(a.shape, a.dtype)
    n = 1 if per_dest == "sibling" else 3
    return pl.pallas_call(
        body, name=name,
        out_shape=(pltpu.SemaphoreType.DMA((n,)), pltpu.SemaphoreType.DMA((n,)), hbm(src), hbm(land), hbm(carry)),
        in_specs=(HBM, HBM, HBM), out_specs=(SEM, SEM, HBM, HBM, HBM),
        input_output_aliases={0: 2, 1: 3, 2: 4},
        compiler_params=pltpu.CompilerParams(has_side_effects=EFFECT),
    )(pltpu.with_memory_space_constraint(src, pltpu.HBM), pltpu.with_memory_space_constraint(land, pltpu.HBM),
      pltpu.with_memory_space_constraint(carry, pltpu.HBM))


def exchange_wait(src, land, send_sems, recv_sems, after, name, per_dest):
    def body(src_ref, land_ref, send_sems_ref, recv_sems_ref, after_ref, src_out, land_out):
        for send, recv in _chip_copies(src_ref, land_ref, send_sems_ref, recv_sems_ref, per_dest):
            send.wait_send()
            recv.wait_recv()

    hbm = lambda a: pltpu.HBM(a.shape, a.dtype)
    return pl.pallas_call(
        body, name=name, out_shape=(hbm(src), hbm(land)),
        in_specs=(HBM, HBM, SEM, SEM, ANY), out_specs=(HBM, HBM), input_output_aliases={0: 0, 1: 1},
        compiler_params=pltpu.CompilerParams(has_side_effects=EFFECT),
    )(src, land, send_sems, recv_sems, after)


def sibling_exchange(srcs, name):
    n = len(srcs)

    def body(*refs):
        src_refs, out_refs, send_sems, recv_sems = refs[:n], refs[n:2 * n], refs[2 * n], refs[2 * n + 1]
        x, y, c = lax.axis_index("x"), lax.axis_index("y"), lax.axis_index("c")
        copies = [pltpu.make_async_remote_copy(src_ref=s, dst_ref=o, send_sem=send_sems.at[k], recv_sem=recv_sems.at[k],
                                               device_id=(x, y, 1 - c), device_id_type=MESH)
                  for k, (s, o) in enumerate(zip(src_refs, out_refs))]
        for cp in copies:
            cp.start()
        for cp in copies:
            cp.wait()

    return pl.pallas_call(
        body, name=name, in_specs=[ANY] * n, out_specs=[ANY] * n,
        out_shape=[jax.ShapeDtypeStruct(s.shape, s.dtype) for s in srcs],
        scratch_shapes=[pltpu.SemaphoreType.DMA((n,)), pltpu.SemaphoreType.DMA((n,))],
    )(*srcs)


def chip_gather(src, name):
    def body(src_ref, out_ref, send_sems, recv_sems, local_sem):
        x, y, c = lax.axis_index("x"), lax.axis_index("y"), lax.axis_index("c")
        me = 2 * x + y
        local = pltpu.make_async_copy(src_ref, out_ref.at[me], local_sem)
        local.start()
        peers = [(1 - x, y), (x, 1 - y), (1 - x, 1 - y)]
        mk = lambda j, px, py, slab: pltpu.make_async_remote_copy(
            src_ref=src_ref, dst_ref=out_ref.at[slab], send_sem=send_sems.at[j], recv_sem=recv_sems.at[j],
            device_id=(px, py, c), device_id_type=MESH)
        sends = [mk(j, px, py, me) for j, (px, py) in enumerate(peers)]
        for cp in sends:
            cp.start()
        for j, (px, py) in enumerate(peers):
            mk(j, px, py, 2 * px + py).wait_recv()
        for cp in sends:
            cp.wait_send()
        local.wait()

    return pl.pallas_call(
        body, name=name, in_specs=[ANY], out_specs=ANY,
        out_shape=jax.ShapeDtypeStruct((N_CHIPS,) + tuple(src.shape), src.dtype),
        scratch_shapes=[pltpu.SemaphoreType.DMA((3,)), pltpu.SemaphoreType.DMA((3,)), pltpu.SemaphoreType.DMA],
    )(src)


def all_exchange(src, name):
    def body(src_ref, out_ref, send_sems, recv_sems, local_sem):
        x, y, c = lax.axis_index("x"), lax.axis_index("y"), lax.axis_index("c")
        me = 4 * x + 2 * y + c
        local = pltpu.make_async_copy(src_ref, out_ref.at[me], local_sem)
        local.start()
        sends = []
        peers = []
        for k in range(1, N_DEV):
            bx, by, bc = (k >> 2) & 1, (k >> 1) & 1, k & 1
            px = 1 - x if bx else x
            py = 1 - y if by else y
            pc = 1 - c if bc else c
            peers.append((px, py, pc))
        for k, peer in enumerate(peers):
            cp = pltpu.make_async_remote_copy(src_ref=src_ref, dst_ref=out_ref.at[me], send_sem=send_sems.at[k],
                                              recv_sem=recv_sems.at[k], device_id=peer, device_id_type=MESH)
            cp.start()
            sends.append(cp)
        for k, (px, py, pc) in enumerate(peers):
            pltpu.make_async_remote_copy(src_ref=src_ref, dst_ref=out_ref.at[4 * px + 2 * py + pc],
                                         send_sem=send_sems.at[k], recv_sem=recv_sems.at[k],
                                         device_id=(px, py, pc), device_id_type=MESH).wait_recv()
        for cp in sends:
            cp.wait_send()
        local.wait()

    return pl.pallas_call(
        body, name=name, in_specs=[ANY], out_specs=ANY,
        out_shape=jax.ShapeDtypeStruct((N_DEV,) + tuple(src.shape), src.dtype),
        scratch_shapes=[pltpu.SemaphoreType.DMA((N_DEV - 1,)), pltpu.SemaphoreType.DMA((N_DEV - 1,)),
                        pltpu.SemaphoreType.DMA],
    )(src)


def sum_own_slabs(own, land, name, tile=512):
    R, C = own.shape
    n = land.shape[0]
    tile = _tile(R, tile, 16)

    def body(o_ref, t_ref, out_ref):
        acc = o_ref[...].astype(F32)
        for s in range(n):
            acc = acc + t_ref[s].astype(F32)
        out_ref[...] = acc

    return pl.pallas_call(
        body, name=name, grid=(R // tile,),
        in_specs=[pl.BlockSpec((tile, C), lambda i: (i, 0)), pl.BlockSpec((n, tile, C), lambda i: (0, i, 0))],
        out_specs=pl.BlockSpec((tile, C), lambda i: (i, 0)), out_shape=jax.ShapeDtypeStruct((R, C), F32),
        compiler_params=_cp("parallel"),
    )(own, land)


def sum_slabs(t, name, tile=512):
    n, R, C = t.shape
    tile = _tile(R, tile, 16)

    def body(t_ref, o_ref):
        acc = t_ref[0].astype(F32)
        for s in range(1, n):
            acc = acc + t_ref[s].astype(F32)
        o_ref[...] = acc

    return pl.pallas_call(
        body, name=name, grid=(R // tile,), in_specs=[pl.BlockSpec((n, tile, C), lambda i: (0, i, 0))],
        out_specs=pl.BlockSpec((tile, C), lambda i: (i, 0)), out_shape=jax.ShapeDtypeStruct((R, C), F32),
        compiler_params=_cp("parallel"),
    )(t)


PACK_ROW_MULT = 512


def _pad_rows(a, mult=16):
    r = (-a.shape[0]) % mult
    return jnp.pad(a, ((0, r), (0, 0))) if r else a


BIG = [
    ("ab_w_in", True, 2, 642), ("c_w_qkv", True, 2, 768), ("xa_w_kv", True, 4, 512),
    ("f_w_gate", True, 4, 704), ("f_w_up", True, 4, 704),
    ("ab_w_out", False, 2, 256), ("c_w_out", False, 2, 256), ("xa_w_q", False, 4, 256),
    ("xa_w_out", False, 4, 256), ("f_w_down", False, 4, 704), ("s5_w_glu", False, 2, 64),
]


GROUPS = ("B", "A")
GROUP_ROW_MULT = 64


def group_spec(layer, grp):
    i = layer // 2
    if grp == "A":
        return [("xa_w_kv", layer, True, 512), ("xa_w_q", layer, False, 256), ("xa_w_out", layer, False, 256),
                ("f_w_gate", layer, True, 704), ("f_w_up", layer, True, 704), ("f_w_down", layer, False, 704)]
    if layer % 2 == 0:
        return [("ab_w_in", i, True, 642), ("ab_w_out", i, False, 256), ("s5_w_glu", i, False, 64)]
    return [("c_w_qkv", i, True, 768), ("c_w_out", i, False, 256)]


def _seg_rows(rows):
    return rows + ((-rows) % 16)


def _f32_rows(a):
    bits = lax.bitcast_convert_type(a.reshape(-1), BF16).reshape(-1)
    return jnp.pad(bits, (0, 16 * D - bits.shape[0])).reshape(16, D)


def pack_group_shards(p, layer, grp):
    segs = []
    for name, idx, transposed, rows in group_spec(layer, grp):
        w = p[name][idx]
        if transposed:
            w = w.T
        segs.append(_pad_rows(w.astype(BF16).reshape(-1, D)))
    if grp == "B":
        small = p["gdn_conv_w"] if layer % 2 == 0 else p["c_norm_g"]
        segs.append(_f32_rows(small[layer // 2]))
    return _pad_rows(jnp.concatenate(segs, axis=0), GROUP_ROW_MULT)


def unpack_group_gathered(g, layer, grp):
    out, off = {}, 0
    for name, idx, transposed, rows in group_spec(layer, grp):
        seg = g[:, off:off + rows]
        if name == "s5_w_glu":
            out[name] = seg.reshape(N_CHIPS * 128, 512)
        else:
            out[name] = seg.reshape(N_CHIPS * rows, D)
        off += _seg_rows(rows)
    if grp == "B":
        n = 4 * 384 if layer % 2 == 0 else 256
        bits = g[:, off:off + 16].reshape(N_CHIPS, -1)[:, :2 * n].reshape(N_CHIPS, n, 2)
        small = lax.bitcast_convert_type(bits, F32)
        if layer % 2 == 0:
            out["gdn_conv_w"] = jnp.swapaxes(small.reshape(N_CHIPS, 4, 384), 0, 1).reshape(4, 1536)
        else:
            out["c_norm_g"] = small.reshape(D)
    return out


def pack_group_grads(gr, layer, grp):
    segs = []
    for name, idx, transposed, rows in group_spec(layer, grp):
        w = gr[name].astype(BF16)
        seg = w.reshape(N_CHIPS, rows, D)
        r = (-rows) % 16
        if r:
            seg = jnp.pad(seg, ((0, 0), (0, r), (0, 0)))
        segs.append(seg)
    out = jnp.concatenate(segs, axis=1)
    return jnp.pad(out, ((0, 0), (0, (-out.shape[1]) % GROUP_ROW_MULT), (0, 0)))


def unpack_group_reduced(g, layer, grp):
    out, off = {}, 0
    for name, idx, transposed, rows in group_spec(layer, grp):
        seg = g[off:off + rows]
        if name == "s5_w_glu":
            out[name] = seg.reshape(128, 512)
        else:
            out[name] = seg.T if (transposed and name not in ADAM_TRANSPOSED) else seg
        off += _seg_rows(rows)
    return out


ADAM_TRANSPOSED = ("f_w_gate", "f_w_up")


SMALL = ["ab_norm_g", "s5_a_re", "s5_a_im", "s5_log_dt", "s5_b_re", "s5_b_im", "s5_c_re", "s5_c_im", "s5_d",
         "s5_b_glu", "gdn_conv_w", "gdn_a_log", "gdn_dt_bias", "gdn_out_norm_g", "c_norm_g", "c_q_norm_g",
         "c_k_norm_g", "c_rel_bias", "mem_norm_g", "xa_norm_g", "xa_q_norm_g", "xa_k_norm_g", "f_norm_g"]


def _lane_rows(a):
    flat = a.reshape(-1).astype(F32)
    return jnp.pad(flat, (0, (-flat.shape[0]) % 1024)).reshape(-1, 128)


def pack_small(d, extra=None):
    parts = [_lane_rows(d[n]) for n in SMALL]
    if extra is not None:
        parts.append(_lane_rows(extra))
    rows = jnp.concatenate(parts, axis=0)
    return jnp.pad(rows, ((0, (-rows.shape[0]) % 128), (0, 0)))


def unpack_small(rows, shapes):
    out, off = {}, 0
    for n in SMALL:
        sz = int(np.prod(shapes[n]))
        k = 8 * -(-sz // 1024)
        out[n] = rows[off:off + k].reshape(-1)[:sz].reshape(shapes[n])
        off += k
    return out, rows[off:]


def _s5_blockdiag_b(b):
    bt = jnp.swapaxes(b, 1, 2).reshape(S5_GB, 8, S5_C, S5_P)
    eye = jnp.eye(8, dtype=b.dtype)
    return jnp.einsum("bgcp,gh->bgchp", bt, eye).reshape(S5_GB, 8 * S5_C, 8 * S5_P)


def _s5_blockdiag_c(c):
    ct = jnp.swapaxes(c, 1, 2).reshape(S5_GB, 8, S5_P, S5_C)
    eye = jnp.eye(8, dtype=c.dtype)
    return jnp.einsum("bgpc,gh->bgphc", ct, eye).reshape(S5_GB, 8 * S5_P, 8 * S5_C)


def _s5_diag_b(db):
    t = db.reshape(S5_GB, 8, S5_C, 8, S5_P)
    t = jnp.transpose(t, (0, 2, 4, 1, 3)).reshape(S5_GB, S5_C, S5_P, 64)
    d = t[..., ::9]
    return jnp.transpose(d, (0, 3, 2, 1)).reshape(S5_G, S5_P, S5_C)


def _s5_diag_c(dc):
    t = dc.reshape(S5_GB, 8, S5_P, 8, S5_C)
    t = jnp.transpose(t, (0, 2, 4, 1, 3)).reshape(S5_GB, S5_P, S5_C, 64)
    d = t[..., ::9]
    return jnp.transpose(d, (0, 3, 2, 1)).reshape(S5_G, S5_C, S5_P)


def _heads(t):
    return jnp.swapaxes(t.reshape(t.shape[0], CA_H, CA_D), 0, 1)


def _unheads(t):
    return jnp.swapaxes(t, 0, 1).reshape(t.shape[1], D)


def local_step(x, mem, target, p, wsrc, gsink):
    S = x.shape[0]
    row2 = lambda a: a.reshape(1, -1)
    saved = []
    (mem_n,) = row_fwd("mem_norm", f_norm, [mem], [row2(p["mem_norm_g"])], [(D, BF16)])
    gs = {}

    for layer in range(DEPTH):
        i = layer // 2
        w = wsrc(layer, "B", x)
        sv = {"x0": x, "wB": w}
        if layer % 2 == 0:
            (h,) = row_fwd("norm", f_norm, [x], [row2(p["ab_norm_g"][i])], [(D, BF16)])
            w_in = w["ab_w_in"]
            pm = matmul("nt", h, w_in[:2560], "proj_main")
            pab = matmul("nt", h, w_in[2560:], "proj_ab")
            s5p = dict(
                are=p["s5_a_re"][i].reshape(1, -1), aim=p["s5_a_im"][i].reshape(1, -1),
                ldt=jnp.broadcast_to(p["s5_log_dt"][i][:, None], (S5_G, S5_P)).reshape(1, -1),
                b_re=_s5_blockdiag_b(p["s5_b_re"][i]), b_im=_s5_blockdiag_b(p["s5_b_im"][i]),
                c_re=_s5_blockdiag_c(p["s5_c_re"][i]), c_im=_s5_blockdiag_c(p["s5_c_im"][i]),
                dv=p["s5_d"][i].reshape(1, -1))
            y5, st5 = s5_fwd(pm, **s5p)
            (a_out,) = row_fwd("glu", f_glu, [y5], [w["s5_w_glu"], row2(p["s5_b_glu"][i])], [(S5_W, F32)])
            conv_w = w["gdn_conv_w"]
            qkvc = conv_fwd(pm, conv_w)
            abt = jnp.swapaxes(pab, 0, 1)[:, :, None]
            alog = p["gdn_a_log"][i].reshape(GDN_H, 1, 1)
            dtb = p["gdn_dt_bias"][i].reshape(GDN_H, 1, 1)
            og = row2(p["gdn_out_norm_g"][i])
            b_out, stg, inv = gdn_fwd(qkvc, pm, abt, alog, dtb, og)
            cat = jnp.concatenate([a_out, b_out], axis=1)
            x, hq = matmul("nn", cat, w["ab_w_out"], "mix_out", add=x, norm_out=row2(p["xa_norm_g"][layer]))
            sv.update(h=h, pm=pm, s5p=s5p, y5=y5, st5=st5, qkvc=qkvc, abt=abt, alog=alog, dtb=dtb, og=og,
                      stg=stg, inv=inv, cat=cat, conv_w=conv_w)
        else:
            (h,) = row_fwd("norm", f_norm, [x], [row2(w["c_norm_g"])], [(D, BF16)])
            qkv = matmul("nt", h, w["c_w_qkv"], "proj_qkv")
            qg = jnp.tile(row2(p["c_q_norm_g"][i]), (1, CA_H))
            kg = jnp.tile(row2(p["c_k_norm_g"][i]), (1, CA_H))
            (qn,) = row_fwd("headnorm_q", f_headnorm, [(qkv, D, 0)], [qg], [(D, F32)])
            kp, vp = kv_prep(qkv, kg)
            bias = rel_bias_expand(p["c_rel_bias"][i]).reshape(CA_H * CHUNK, CA_BAND)
            o = cattn_fwd(qn, kp, vp, bias)
            x, hq = matmul("nn", o, w["c_w_out"], "mix_out", add=x, norm_out=row2(p["xa_norm_g"][layer]))
            sv.update(h=h, qkv=qkv, qg=qg, kg=kg, qn=qn, kp=kp, vp=vp, bias=bias, o=o)
        sv["x1"] = x
        w = wsrc(layer, "A", x)
        sv["wA"] = w
        qx = matmul("nn", hq, w["xa_w_q"], "xa_q")
        kv = matmul("nt", mem_n, w["xa_w_kv"], "xa_kv")
        xqg, xkg = row2(p["xa_q_norm_g"][layer]), row2(p["xa_k_norm_g"][layer])
        (ox,) = row_fwd("xattn", f_xattn, [qx], [kv, xqg, xkg], [(D, BF16)])
        x, hf = matmul("nn", ox, w["xa_w_out"], "xa_out", add=x, norm_out=row2(p["f_norm_g"][layer]))
        sv.update(hq=hq, qx=qx, kv=kv, ox=ox)
        sv["x2"] = x
        gate, up, act = ffn_in(hf, w["f_w_gate"], w["f_w_up"])
        x = matmul("nn", act, w["f_w_down"], "ffn_down", add=x)
        sv.update(hf=hf, gate=gate, up=up, act=act)
        saved.append(sv)

    dx, loss_vec = loss_head(x, target)

    dmem_n = None
    for layer in reversed(range(DEPTH)):
        i = layer // 2
        sv = saved[layer]
        w, gw = sv["wA"], {}
        dgate, dup = ffn_dact(dx, w["f_w_down"], sv["gate"], sv["up"])
        gw["f_w_down"] = matmul("tn", sv["act"], dx, "ffn_dwd", out_dtype=BF16)
        gw["f_w_gate"] = matmul("tn", dgate, sv["hf"], "ffn_dwg", out_dtype=BF16)
        gw["f_w_up"] = matmul("tn", dup, sv["hf"], "ffn_dwu", out_dtype=BF16)
        dh = matmul("nn", dgate, w["f_w_gate"], "ffn_dhg")
        dx, dg = matmul("nn", dup, w["f_w_up"], "ffn_dhu", add=dh,
                        norm_bwd=(sv["x2"], row2(p["f_norm_g"][layer]), dx))
        gs.setdefault("f_norm_g", [None] * DEPTH)[layer] = dg[0]
        do = matmul("nt", dx, w["xa_w_out"], "xa_do")
        gw["xa_w_out"] = matmul("tn", sv["ox"], dx, "xa_dwo", out_dtype=BF16)
        xqg, xkg = row2(p["xa_q_norm_g"][layer]), row2(p["xa_k_norm_g"][layer])
        dqx, dkv, dqg, dkg = row_bwd("xattn_bwd", f_xattn, [sv["qx"]], [sv["kv"], xqg, xkg], [do],
                                     [0], [0, 1, 2])
        gs.setdefault("xa_q_norm_g", [None] * DEPTH)[layer] = dqg[0]
        gs.setdefault("xa_k_norm_g", [None] * DEPTH)[layer] = dkg[0]
        gw["xa_w_q"] = matmul("tn", sv["hq"], dqx, "xa_dwq", out_dtype=BF16)
        gw["xa_w_kv"] = matmul("tn", dkv, mem_n, "xa_dwkv", out_dtype=BF16)
        dmem_n = matmul("nn", dkv, w["xa_w_kv"], "xa_dmem", add=dmem_n)
        dx, dg = matmul("nt", dqx, w["xa_w_q"], "xa_dhq", norm_bwd=(sv["x1"], row2(p["xa_norm_g"][layer]), dx))
        gs.setdefault("xa_norm_g", [None] * DEPTH)[layer] = dg[0]
        dx = gsink(layer, "A", gw, dx)
        w, gw = sv["wB"], {}
        if layer % 2 == 0:
            dcat = matmul("nt", dx, w["ab_w_out"], "mix_dcat")
            gw["ab_w_out"] = matmul("tn", sv["cat"], dx, "mix_dwo", out_dtype=BF16)
            dy5, dwglu, dbglu = row_bwd("glu_bwd", f_glu, [sv["y5"]], [w["s5_w_glu"], row2(p["s5_b_glu"][i])],
                                        [(dcat, S5_W, 0)], [0], [0, 1])
            gw["s5_w_glu"] = dwglu
            gs.setdefault("s5_b_glu", [None] * 2)[i] = dbglu[0]
            s5p = sv["s5p"]
            du, dare, daim, dldt, dbre, dbim, dcre, dcim, ddv = s5_bwd(sv["pm"], sv["st5"], dy5, **s5p)
            (dldt_g,) = whole("s5_dt_sum", lambda t: (jnp.sum(t, axis=1, keepdims=True),),
                              [dldt.reshape(S5_G, S5_P)], [((S5_G, 1), F32)])
            for nme, val in (("s5_a_re", dare.reshape(S5_G, S5_P)), ("s5_a_im", daim.reshape(S5_G, S5_P)),
                             ("s5_log_dt", dldt_g[:, 0]), ("s5_b_re", _s5_diag_b(dbre)),
                             ("s5_b_im", _s5_diag_b(dbim)), ("s5_c_re", _s5_diag_c(dcre)),
                             ("s5_c_im", _s5_diag_c(dcim)), ("s5_d", ddv.reshape(S5_G, S5_C))):
                gs.setdefault(nme, [None] * 2)[i] = val
            dq, dk, dv, dgate, dal, dbl, dalog, ddtb, dog = gdn_bwd(
                sv["qkvc"], sv["pm"], sv["abt"], sv["alog"], sv["dtb"], sv["og"], sv["stg"], sv["inv"],
                dcat[:, S5_W:])
            (dog_s,) = whole("gdn_og_sum", lambda t: (jnp.sum(t, axis=0, keepdims=True),),
                             [dog.reshape(GDN_H, GDN_D)], [((1, GDN_D), F32)])
            gs.setdefault("gdn_out_norm_g", [None] * 2)[i] = dog_s[0]
            gs.setdefault("gdn_a_log", [None] * 2)[i] = dalog.reshape(GDN_H)
            gs.setdefault("gdn_dt_bias", [None] * 2)[i] = ddtb.reshape(GDN_H)
            dqkvc = jnp.concatenate([dq, dk, dv], axis=1)
            dqkv, dconv = conv_bwd(sv["pm"], sv["conv_w"], dqkvc)
            gs.setdefault("gdn_conv_w", [None] * 2)[i] = dconv
            dpm = jnp.concatenate([du, dqkv, dgate], axis=1).astype(BF16)
            dpab = jnp.swapaxes(jnp.concatenate([dal, dbl], axis=0)[:, :, 0], 0, 1)
            dw_main = matmul("tn", dpm, sv["h"], "proj_dw", out_dtype=BF16)
            dw_ab = matmul("tn", dpab, sv["h"], "proj_ab_dw", out_dtype=BF16)
            gw["ab_w_in"] = jnp.concatenate([dw_main, dw_ab], axis=0)
            w_in = w["ab_w_in"]
            dh = matmul("nn", dpm, w_in[:2560], "proj_dh")
            dx, dg = matmul("nn", dpab, w_in[2560:], "proj_ab_dh", add=dh,
                            norm_bwd=(sv["x0"], row2(p["ab_norm_g"][i]), dx))
            gs.setdefault("ab_norm_g", [None] * 2)[i] = dg[0]
        else:
            do = matmul("nt", dx, w["c_w_out"], "mix_dcat")
            gw["c_w_out"] = matmul("tn", sv["o"], dx, "mix_dwo", out_dtype=BF16)
            dqn, dkp, dvp, dbias = cattn_bwd(sv["qn"], sv["kp"], sv["vp"], sv["bias"], do)
            gs.setdefault("c_rel_bias", [None] * 2)[i] = rel_bias_grad(dbias.reshape(CA_H, CHUNK, CA_BAND))
            dq, dqg = row_bwd("headnorm_bwd", f_headnorm, [(sv["qkv"], D, 0)], [sv["qg"]], [dqn], [0], [0])
            dk, dkg = row_bwd("headnorm_bwd", f_headnorm, [(sv["qkv"], D, 1)], [sv["kg"]],
                              [(dkp, D, 0, CA_PAD)], [0], [0])
            head_sum = lambda t: (jnp.sum(t, axis=0, keepdims=True),)
            (dqg,) = whole("headgain_sum", head_sum, [dqg.reshape(CA_H, CA_D)], [((1, CA_D), F32)])
            (dkg,) = whole("headgain_sum", head_sum, [dkg.reshape(CA_H, CA_D)], [((1, CA_D), F32)])
            gs.setdefault("c_q_norm_g", [None] * 2)[i] = dqg[0]
            gs.setdefault("c_k_norm_g", [None] * 2)[i] = dkg[0]
            dqkv = jnp.concatenate([dq, dk, dvp[CA_PAD:]], axis=1).astype(BF16)
            gw["c_w_qkv"] = matmul("tn", dqkv, sv["h"], "proj_qkv_dw", out_dtype=BF16)
            dx, dg = matmul("nn", dqkv, w["c_w_qkv"], "proj_qkv_dh",
                            norm_bwd=(sv["x0"], row2(w["c_norm_g"]), dx))
            gs.setdefault("c_norm_g", [None] * 2)[i] = dg[0]
        dx = gsink(layer, "B", gw, dx)
    (dmg,) = row_bwd("mem_norm_bwd", f_norm, [mem], [row2(p["mem_norm_g"])], [dmem_n], [], [0])
    small = {n: jnp.stack(v) for n, v in gs.items()}
    small["mem_norm_g"] = dmg[0]
    return loss_vec, dx, small


ADAM_BLOCK_BYTES = 3 << 19


def adam(w, g, m, v, name):
    shape = w.shape
    if w.ndim == 3:
        d0, n, d2 = shape
        fits = [t for t in range(8, n + 1, 8) if n % t == 0 and d0 * t * d2 * 4 <= ADAM_BLOCK_BYTES]
        return tuple(row_fwd(name, f_adam, [w, g, m, v], [], [((d0, d2), F32)] * 3, tile=max(fits)))
    cols = shape[-1]
    w2, g2, m2, v2 = (t.reshape(-1, cols) for t in (w, g, m, v))
    rows = w2.shape[0]
    tile = rows if rows <= 512 else _tile(rows, 512, 8)
    outs = row_fwd(name, f_adam, [w2, g2, m2, v2], [], [(cols, F32)] * 3, tile=tile)
    return tuple(o.reshape(shape) for o in outs)


WEIGHTS = ['ab_norm_g', 'ab_w_in', 'ab_w_out', 's5_a_re', 's5_a_im', 's5_log_dt', 's5_b_re', 's5_b_im', 's5_c_re',
           's5_c_im', 's5_d', 's5_w_glu', 's5_b_glu', 'gdn_conv_w', 'gdn_a_log', 'gdn_dt_bias', 'gdn_out_norm_g',
           'c_norm_g', 'c_w_qkv', 'c_w_out', 'c_q_norm_g', 'c_k_norm_g', 'c_rel_bias', 'mem_norm_g', 'xa_norm_g',
           'xa_w_q', 'xa_w_kv', 'xa_w_out', 'xa_q_norm_g', 'xa_k_norm_g', 'f_norm_g', 'f_w_gate', 'f_w_up',
           'f_w_down']
SHARDED_SMALL = {"gdn_conv_w": (2, 384), "c_norm_g": (1, 256)}


def kernel(x, mem, ab_norm_g, ab_w_in, ab_w_out, s5_a_re, s5_a_im, s5_log_dt, s5_b_re, s5_b_im, s5_c_re, s5_c_im, s5_d, s5_w_glu, s5_b_glu, gdn_conv_w, gdn_a_log, gdn_dt_bias, gdn_out_norm_g, c_norm_g, c_w_qkv, c_w_out, c_q_norm_g, c_k_norm_g, c_rel_bias, mem_norm_g, xa_norm_g, xa_w_q, xa_w_kv, xa_w_out, xa_q_norm_g, xa_k_norm_g, f_norm_g, f_w_gate, f_w_up, f_w_down, loss_target, m_ab_norm_g, m_ab_w_in, m_ab_w_out, m_s5_a_re, m_s5_a_im, m_s5_log_dt, m_s5_b_re, m_s5_b_im, m_s5_c_re, m_s5_c_im, m_s5_d, m_s5_w_glu, m_s5_b_glu, m_gdn_conv_w, m_gdn_a_log, m_gdn_dt_bias, m_gdn_out_norm_g, m_c_norm_g, m_c_w_qkv, m_c_w_out, m_c_q_norm_g, m_c_k_norm_g, m_c_rel_bias, m_mem_norm_g, m_xa_norm_g, m_xa_w_q, m_xa_w_kv, m_xa_w_out, m_xa_q_norm_g, m_xa_k_norm_g, m_f_norm_g, m_f_w_gate, m_f_w_up, m_f_w_down, v_ab_norm_g, v_ab_w_in, v_ab_w_out, v_s5_a_re, v_s5_a_im, v_s5_log_dt, v_s5_b_re, v_s5_b_im, v_s5_c_re, v_s5_c_im, v_s5_d, v_s5_w_glu, v_s5_b_glu, v_gdn_conv_w, v_gdn_a_log, v_gdn_dt_bias, v_gdn_out_norm_g, v_c_norm_g, v_c_w_qkv, v_c_w_out, v_c_q_norm_g, v_c_k_norm_g, v_c_rel_bias, v_mem_norm_g, v_xa_norm_g, v_xa_w_q, v_xa_w_kv, v_xa_w_out, v_xa_q_norm_g, v_xa_k_norm_g, v_f_norm_g, v_f_w_gate, v_f_w_up, v_f_w_down):
    args = locals()
    p = {n: args[n] for n in WEIGHTS}
    m = {n: args["m_" + n] for n in WEIGHTS}
    v = {n: args["v_" + n] for n in WEIGHTS}
    chip = 2 * lax.axis_index("x") + lax.axis_index("y")

    carry = x[0]
    gathers = {}
    for layer in range(DEPTH):
        for grp in GROUPS:
            src = pack_group_shards(p, layer, grp)
            land = lax.dynamic_update_slice(lax.empty((N_CHIPS,) + src.shape, BF16), src[None], (chip, 0, 0))
            send_sems, recv_sems, src, land, carry = exchange_start(
                src, land, carry, f"gather_start_{layer}{grp}", per_dest=False)
            gathers[layer, grp] = (src, land, send_sems, recv_sems)

    def wsrc(layer, grp, after):
        src, land, send_sems, recv_sems = gathers[layer, grp]
        _, land = exchange_wait(src, land, send_sems, recv_sems, after, f"gather_wait_{layer}{grp}", per_dest=False)
        return unpack_group_gathered(land, layer, grp)

    scatters, siblings = [], []
    LAG = 2

    def finish(carry):
        layer, grp, src, land, send_sems, recv_sems = scatters[len(siblings)]
        src, land = exchange_wait(src, land, send_sems, recv_sems, carry, f"scatter_wait_{layer}{grp}", per_dest=True)
        own = lax.dynamic_index_in_dim(src, chip, axis=0, keepdims=False)
        part = sum_own_slabs(own, land, "sum_chips")
        send_sems, recv_sems, part, other, carry = exchange_start(
            part, lax.empty(part.shape, F32), carry, f"sibling_start_{layer}{grp}", per_dest="sibling")
        siblings.append((layer, grp, part, other, send_sems, recv_sems))
        return carry

    def gsink(layer, grp, gw, carry):
        src = pack_group_grads(gw, layer, grp)
        land = lax.empty((3,) + src.shape[1:], BF16)
        send_sems, recv_sems, src, land, carry = exchange_start(
            src, land, carry, f"scatter_start_{layer}{grp}", per_dest=True)
        scatters.append((layer, grp, src, land, send_sems, recv_sems))
        if len(scatters) > LAG:
            carry = finish(carry)
        return carry

    loss_vec, dx, g_small = local_step(carry, mem[0], loss_target[0], p, wsrc, gsink)
    while len(siblings) < len(scatters):
        dx = finish(dx)

    per_layer = {}
    for layer, grp, part, other, send_sems, recv_sems in siblings:
        part, other = exchange_wait(part, other, send_sems, recv_sems, dx, f"sibling_wait_{layer}{grp}",
                                    per_dest="sibling")
        (total,) = row_fwd("sum_cores", lambda a, b: (a + b,), [part, other], [], [(D, F32)],
                           tile=_tile(part.shape[0], 512, 16))
        for name, g in unpack_group_reduced(total, layer, grp).items():
            per_layer.setdefault(name, {})[layer] = g
    grads = {name: jnp.stack([d[k] for k in sorted(d)]) for name, d in per_layer.items()}

    full_shapes = {n: ((2, 4, 1536) if n == "gdn_conv_w" else (2, D) if n == "c_norm_g" else p[n].shape)
                   for n in SMALL}
    small_mine = pack_small(g_small, extra=loss_vec)
    (small_other,) = sibling_exchange([small_mine], "sibling_small")
    (small_chip,) = row_fwd("sum_small_cores", lambda a, b: (a + b,), [small_mine, small_other], [], [(128, F32)],
                            tile=128)
    small_sum = sum_slabs(chip_gather(small_chip, "gather_small"), "sum_small")
    g_s, rest = unpack_small(small_sum, full_shapes)
    (loss11,) = whole("loss_sum", lambda t: (jnp.sum(jnp.sum(t, axis=1, keepdims=True), axis=0, keepdims=True),),
                      [rest[:D // 128]], [((1, 1), F32)])
    for n, (axis, width) in SHARDED_SMALL.items():
        g_s[n] = lax.dynamic_slice_in_dim(g_s[n], chip * width, width, axis=axis)
    grads.update(g_s)

    delta, new_m, new_v = {}, {}, {}
    for n in SMALL:
        shape = p[n].shape
        two = (1, shape[0]) if len(shape) == 1 else (int(np.prod(shape[:-1])), shape[-1])
        outs = whole("adam_" + n, f_adam, [t.reshape(two) for t in (p[n], grads[n], m[n], v[n])], [(two, F32)] * 3)
        delta[n], new_m[n], new_v[n] = (o.reshape(shape) for o in outs)
    for name, _, _, _ in BIG:
        if name in ADAM_TRANSPOSED:
            t = lambda a: jnp.swapaxes(a, 1, 2)
            outs = adam(t(p[name]), grads[name], t(m[name]), t(v[name]), "adam_" + name)
            delta[name], new_m[name], new_v[name] = (t(o) for o in outs)
            grads[name] = t(grads[name])
        else:
            delta[name], new_m[name], new_v[name] = adam(p[name], grads[name], m[name], v[name], "adam_" + name)

    return (loss11[0, 0], dx[None], *[grads[n] for n in WEIGHTS], *[delta[n] for n in WEIGHTS],
            *[new_m[n] for n in WEIGHTS], *[new_v[n] for n in WEIGHTS])
```

```python
import functools
import math

import numpy as np
import jax
import jax.numpy as jnp
from jax import lax
from jax.experimental import pallas as pl
from jax.experimental.pallas import tpu as pltpu

F32 = jnp.float32
BF16 = jnp.bfloat16
MESH = pl.DeviceIdType.MESH

D = 1024
CHUNK = 64
N_MEM = 256
EPS = 1e-6
S5_W = 512
S5_G = 32
S5_C = 16
S5_P = 64
S5_GB = 4
GDN_H = 4
GDN_D = 128
CA_H = 16
CA_D = 64
CA_LEFT = 8
CA_BAND = (CA_LEFT + 1) * CHUNK
CA_PAD = CA_LEFT * CHUNK
MAX_REL = 128
XA_H = 4
XA_D = 256
FFN = 2816
DEPTH = 4
N_CHIPS = 4
N_DEV = 8
LR, B1, B2, AEPS, WD, STEP = 0.001, 0.9, 0.999, 1e-08, 0.01, 10

VMEM_LIMIT = 56 * 1024 * 1024
ROW_TILE = 256
HI = lax.Precision.HIGHEST


def _cp(*sem):
    return pltpu.CompilerParams(dimension_semantics=sem, vmem_limit_bytes=VMEM_LIMIT)


def _dg(a, b, ca, cb):
    return lax.dot_general(a.astype(BF16), b.astype(BF16), (((ca,), (cb,)), ((), ())),
                           preferred_element_type=F32)


@jax.custom_vjp
def mm(a, b):
    return _dg(a, b, 1, 0)


@jax.custom_vjp
def mm_nt(a, b):
    return _dg(a, b, 1, 1)


@jax.custom_vjp
def mm_tn(a, b):
    return _dg(a, b, 0, 0)


mm.defvjp(lambda a, b: (mm(a, b), (a, b)), lambda r, g: (mm_nt(g, r[1]), mm_tn(r[0], g)))
mm_nt.defvjp(lambda a, b: (mm_nt(a, b), (a, b)), lambda r, g: (mm(g, r[1]), mm_tn(g, r[0])))
mm_tn.defvjp(lambda a, b: (mm_tn(a, b), (a, b)), lambda r, g: (mm_nt(r[1], g), mm(r[0], g)))


def _bdg(a, b, ca, cb):
    return lax.dot_general(a.astype(BF16), b.astype(BF16), (((ca,), (cb,)), ((0,), (0,))),
                           preferred_element_type=F32)


@jax.custom_vjp
def bmm(a, b):
    return _bdg(a, b, 2, 1)


@jax.custom_vjp
def bmm_nt(a, b):
    return _bdg(a, b, 2, 2)


@jax.custom_vjp
def bmm_tn(a, b):
    return _bdg(a, b, 1, 1)


bmm.defvjp(lambda a, b: (bmm(a, b), (a, b)), lambda r, g: (bmm_nt(g, r[1]), bmm_tn(r[0], g)))
bmm_nt.defvjp(lambda a, b: (bmm_nt(a, b), (a, b)), lambda r, g: (bmm(g, r[1]), bmm_tn(g, r[0])))
bmm_tn.defvjp(lambda a, b: (bmm_tn(a, b), (a, b)), lambda r, g: (bmm_nt(r[1], g), bmm(r[0], g)))


def _split(a):
    hi = a.astype(BF16)
    return hi, (a - hi.astype(F32)).astype(BF16)


def _dg3(a, b, ca, cb):
    (ah, al), (bh, bl) = _split(a), _split(b)
    d = lambda u, v: lax.dot_general(u, v, (((ca,), (cb,)), ((), ())), preferred_element_type=F32)
    return d(ah, bh) + (d(ah, bl) + d(al, bh))


@jax.custom_vjp
def mm3(a, b):
    return _dg3(a, b, 1, 0)


@jax.custom_vjp
def mm3_nt(a, b):
    return _dg3(a, b, 1, 1)


@jax.custom_vjp
def mm3_tn(a, b):
    return _dg3(a, b, 0, 0)


mm3.defvjp(lambda a, b: (mm3(a, b), (a, b)), lambda r, g: (mm3_nt(g, r[1]), mm3_tn(r[0], g)))
mm3_nt.defvjp(lambda a, b: (mm3_nt(a, b), (a, b)), lambda r, g: (mm3(g, r[1]), mm3_tn(g, r[0])))
mm3_tn.defvjp(lambda a, b: (mm3_tn(a, b), (a, b)), lambda r, g: (mm3_nt(r[1], g), mm3(r[0], g)))


def _tri_mm(v, upper):
    T = v.shape[0]
    r = lax.broadcasted_iota(jnp.int32, (T, T), 0)
    c = lax.broadcasted_iota(jnp.int32, (T, T), 1)
    m = ((c >= r) if upper else (r >= c)).astype(BF16)
    hi, lo = _split(v)
    d = lambda u: lax.dot_general(m, u, (((1,), (0,)), ((), ())), preferred_element_type=F32)
    return d(hi) + d(lo)


@jax.custom_vjp
def cumsum_rows(v):
    return _tri_mm(v, False)


cumsum_rows.defvjp(lambda v: (_tri_mm(v, False), None), lambda _, g: (_tri_mm(g, True),))


def _rms(x, g):
    return x * lax.rsqrt(jnp.mean(x * x, axis=-1, keepdims=True) + EPS) * g


def _softmax(s):
    e = jnp.exp(s - lax.stop_gradient(jnp.max(s, axis=-1, keepdims=True)))
    return e / jnp.sum(e, axis=-1, keepdims=True)


def _softplus(x):
    return jnp.maximum(x, 0.0) + jnp.log(1.0 + jnp.exp(-jnp.abs(x)))


def _tile(n, cap, align):
    if n <= cap:
        return n
    best = None
    for d in range(align, cap + 1, align):
        if n % d == 0:
            best = d
    assert best is not None, (n, cap, align)
    return best


def matmul(mode, a, b, name, out_dtype=F32, add=None, norm_out=None, norm_bwd=None):
    if mode == "nn":
        (M, K), (K2, N) = a.shape, b.shape
    elif mode == "nt":
        (M, K), (N, K2) = a.shape, b.shape
    else:
        (K, M), (K2, N) = a.shape, b.shape
    assert K == K2, (mode, a.shape, b.shape)
    if mode != "tn" and norm_out is None and norm_bwd is None:
        tm, tn, tk = _tile(M, 1024, 128), _tile(N, 512, 128), _tile(K, 2048, 128)
    else:
        tm, tn, tk = _tile(M, 512, 128), _tile(N, 1536, 128), _tile(K, 2048, 128)
    nk = K // tk
    if mode == "nn":
        a_spec = pl.BlockSpec((tm, tk), lambda i, j, k: (i, k))
        b_spec = pl.BlockSpec((tk, tn), lambda i, j, k: (k, j))
        dn = (((1,), (0,)), ((), ()))
    elif mode == "nt":
        a_spec = pl.BlockSpec((tm, tk), lambda i, j, k: (i, k))
        b_spec = pl.BlockSpec((tn, tk), lambda i, j, k: (j, k))
        dn = (((1,), (1,)), ((), ()))
    else:
        a_spec = pl.BlockSpec((tk, tm), lambda i, j, k: (k, i))
        b_spec = pl.BlockSpec((tk, tn), lambda i, j, k: (k, j))
        dn = (((0,), (0,)), ((), ()))
    o_spec = pl.BlockSpec((tm, tn), lambda i, j, k: (i, j))
    has_add = add is not None
    g_spec = pl.BlockSpec((1, tn), lambda i, j, k: (0, j))
    extra, extra_specs, out_shapes, out_specs = [], [], [jax.ShapeDtypeStruct((M, N), out_dtype)], [o_spec]
    sem = ("parallel", "parallel", "arbitrary")
    if norm_out is not None:
        assert tn == N
        extra, extra_specs = [norm_out], [g_spec]
        out_shapes.append(jax.ShapeDtypeStruct((M, N), BF16))
        out_specs.append(o_spec)
    if norm_bwd is not None:
        assert tn == N
        x_in, g_in, res_in = norm_bwd
        extra, extra_specs = [x_in, g_in, res_in], [o_spec, g_spec, o_spec]
        out_shapes.append(jax.ShapeDtypeStruct((1, N), F32))
        out_specs.append(g_spec)
        sem = ("arbitrary", "arbitrary", "arbitrary")
    n_in = 2 + int(has_add) + len(extra)
    n_out = len(out_shapes)

    def body(*refs):
        a_ref, b_ref = refs[0], refs[1]
        add_ref = refs[2] if has_add else None
        extra_refs = refs[2 + int(has_add):n_in]
        o_ref = refs[n_in]
        acc_ref = refs[-1]
        i = pl.program_id(0)
        p = lax.dot_general(a_ref[...].astype(BF16), b_ref[...].astype(BF16), dn,
                            preferred_element_type=F32)

        def finish(total):
            if has_add:
                total = total + add_ref[...]
            if norm_out is not None:
                o_ref[...] = total.astype(o_ref.dtype)
                refs[n_in + 1][...] = _rms(total, extra_refs[0][...]).astype(BF16)
            elif norm_bwd is not None:
                x_ref, g_ref, res_ref = extra_refs
                _, vjp = jax.vjp(f_norm_res, x_ref[...], g_ref[...])
                dx, dg = vjp((total, res_ref[...]))
                o_ref[...] = dx
                dg_ref = refs[n_in + 1]

                @pl.when(i == 0)
                def _():
                    dg_ref[...] = dg

                @pl.when(i > 0)
                def _():
                    dg_ref[...] += dg
            else:
                o_ref[...] = total.astype(o_ref.dtype)

        if nk == 1:
            finish(p)
        else:
            k = pl.program_id(2)

            @pl.when(k == 0)
            def _():
                acc_ref[...] = p

            @pl.when(k > 0)
            def _():
                acc_ref[...] += p

            @pl.when(k == nk - 1)
            def _():
                finish(acc_ref[...])

    ins = [a, b] + ([add] if has_add else []) + extra
    specs = [a_spec, b_spec] + ([o_spec] if has_add else []) + extra_specs
    out = pl.pallas_call(
        body, name=name, grid=(M // tm, N // tn, nk), in_specs=specs, out_specs=out_specs,
        out_shape=out_shapes, scratch_shapes=[pltpu.VMEM((tm, tn), F32)],
        compiler_params=_cp(*sem),
    )(*ins)
    return out[0] if n_out == 1 else out


def ffn_in(h, wg, wu):
    (S, K), F = h.shape, wg.shape[0]
    tm, tn = _tile(S, 1024, 128), _tile(F, 512, 128)
    dn = (((1,), (1,)), ((), ()))

    def body(h_ref, wg_ref, wu_ref, g_ref, u_ref, act_ref):
        a = h_ref[...].astype(BF16)
        g = lax.dot_general(a, wg_ref[...].astype(BF16), dn, preferred_element_type=F32)
        u = lax.dot_general(a, wu_ref[...].astype(BF16), dn, preferred_element_type=F32)
        g_ref[...] = g.astype(BF16)
        u_ref[...] = u.astype(BF16)
        act_ref[...] = f_swiglu(g, u)[0].astype(BF16)

    w_spec = pl.BlockSpec((tn, K), lambda i, j: (j, 0))
    o_spec = pl.BlockSpec((tm, tn), lambda i, j: (i, j))
    sd = jax.ShapeDtypeStruct
    return pl.pallas_call(
        body, name="ffn_in", grid=(S // tm, F // tn),
        in_specs=[pl.BlockSpec((tm, K), lambda i, j: (i, 0)), w_spec, w_spec], out_specs=[o_spec] * 3,
        out_shape=[sd((S, F), BF16), sd((S, F), BF16), sd((S, F), BF16)],
        compiler_params=_cp("parallel", "parallel"),
    )(h, wg, wu)


def ffn_dact(dy, wd, gate, up):
    (S, K), F = dy.shape, wd.shape[0]
    tm, tn = _tile(S, 1024, 128), _tile(F, 512, 128)
    dn = (((1,), (1,)), ((), ()))

    def body(dy_ref, wd_ref, g_ref, u_ref, dg_ref, du_ref):
        dact = lax.dot_general(dy_ref[...].astype(BF16), wd_ref[...].astype(BF16), dn, preferred_element_type=F32)
        _, vjp = jax.vjp(f_swiglu, g_ref[...].astype(F32), u_ref[...].astype(F32))
        dg, du = vjp((dact,))
        dg_ref[...] = dg.astype(BF16)
        du_ref[...] = du.astype(BF16)

    o_spec = pl.BlockSpec((tm, tn), lambda i, j: (i, j))
    sd = jax.ShapeDtypeStruct
    return pl.pallas_call(
        body, name="ffn_dact", grid=(S // tm, F // tn),
        in_specs=[pl.BlockSpec((tm, K), lambda i, j: (i, 0)), pl.BlockSpec((tn, K), lambda i, j: (j, 0)),
                  o_spec, o_spec],
        out_specs=[o_spec] * 2, out_shape=[sd((S, F), BF16)] * 2,
        compiler_params=_cp("parallel", "parallel"),
    )(dy, wd, gate, up)


def _row_spec(arr, tile):
    if isinstance(arr, tuple):
        a, w, cb = arr[:3]
        ro = (arr[3] // tile) if len(arr) > 3 else 0
        assert len(arr) < 4 or arr[3] % tile == 0
        return a, pl.BlockSpec((tile, w), lambda i, cb=cb, ro=ro: (i + ro, cb)), (tile, w)
    if arr.ndim == 3:
        d0, _, d2 = arr.shape
        return arr, pl.BlockSpec((d0, tile, d2), lambda i: (0, i, 0)), (d0, tile, d2)
    return arr, pl.BlockSpec((tile, arr.shape[1]), lambda i: (i, 0)), (tile, arr.shape[1])


def _full_spec(arr):
    nd = arr.ndim
    return pl.BlockSpec(arr.shape, lambda i, nd=nd: (0,) * nd)


def _n_rows(arr):
    a = arr[0] if isinstance(arr, tuple) else arr
    return a.shape[1] if a.ndim == 3 else a.shape[0]


def _row_out_shape(shape_tail, n, dtype):
    if isinstance(shape_tail, tuple):
        d0, d2 = shape_tail
        return (jax.ShapeDtypeStruct((d0, n, d2), dtype),
                lambda tile: pl.BlockSpec((d0, tile, d2), lambda i: (0, i, 0)))
    return (jax.ShapeDtypeStruct((n, shape_tail), dtype),
            lambda tile: pl.BlockSpec((tile, shape_tail), lambda i: (i, 0)))


def _f32(v):
    return v.astype(F32) if v.dtype == BF16 else v


def row_fwd(name, fn, rows, fulls, outs, tile=ROW_TILE):
    n = _n_rows(rows[0])
    tile = min(tile, n)
    assert n % tile == 0, (name, n, tile)
    rs = [_row_spec(r, tile) for r in rows]
    os_ = [_row_out_shape(w, n, dt) for w, dt in outs]
    nr, nf = len(rows), len(fulls)

    def body(*refs):
        vals = [_f32(r[...]) for r in refs[:nr + nf]]
        res = fn(*vals)
        for r, v in zip(refs[nr + nf:], res):
            r[...] = v.astype(r.dtype)

    out = pl.pallas_call(
        body, name=name, grid=(n // tile,),
        in_specs=[s for _, s, _ in rs] + [_full_spec(f) for f in fulls],
        out_specs=[mk(tile) for _, mk in os_], out_shape=[sh for sh, _ in os_],
        compiler_params=_cp("parallel"),
    )(*[a for a, _, _ in rs], *fulls)
    return out


def row_bwd(name, fn, rows, fulls, cts, want_rows, want_fulls, row_dtypes=None, tile=ROW_TILE):
    n = _n_rows(rows[0])
    tile = min(tile, n)
    assert n % tile == 0, (name, n, tile)
    rs = [_row_spec(r, tile) for r in rows]
    cs = [_row_spec(c, tile) for c in cts]
    nr, nf, nc = len(rows), len(fulls), len(cts)
    row_dtypes = row_dtypes or [F32] * len(want_rows)
    out_shapes, out_specs = [], []
    for k, idx in enumerate(want_rows):
        a, _, blk = rs[idx]
        if len(blk) == 3:
            sh, mk = _row_out_shape((blk[0], blk[2]), n, row_dtypes[k])
        else:
            sh, mk = _row_out_shape(blk[1], n, row_dtypes[k])
        out_shapes.append(sh)
        out_specs.append(mk(tile))
    for idx in want_fulls:
        out_shapes.append(jax.ShapeDtypeStruct(fulls[idx].shape, F32))
        out_specs.append(_full_spec(fulls[idx]))
    n_wr = len(want_rows)

    def body(*refs):
        i = pl.program_id(0)
        vals = [_f32(r[...]) for r in refs[:nr + nf]]
        ct_vals = [_f32(r[...]) for r in refs[nr + nf:nr + nf + nc]]
        outs = refs[nr + nf + nc:]
        _, vjp = jax.vjp(fn, *vals)
        grads = vjp(tuple(ct_vals))
        for k, idx in enumerate(want_rows):
            outs[k][...] = grads[idx].astype(outs[k].dtype)
        for k, idx in enumerate(want_fulls):
            o = outs[n_wr + k]
            g = grads[nr + idx]

            @pl.when(i == 0)
            def _(o=o, g=g):
                o[...] = g

            @pl.when(i > 0)
            def _(o=o, g=g):
                o[...] += g

    out = pl.pallas_call(
        body, name=name, grid=(n // tile,),
        in_specs=[s for _, s, _ in rs] + [_full_spec(f) for f in fulls] + [s for _, s, _ in cs],
        out_specs=out_specs, out_shape=out_shapes,
        compiler_params=_cp("arbitrary"),
    )(*[a for a, _, _ in rs], *fulls, *[a for a, _, _ in cs])
    return out


def whole(name, fn, args, outs):
    def body(*refs):
        res = fn(*[r[...] for r in refs[:len(args)]])
        for r, v in zip(refs[len(args):], res):
            r[...] = v.astype(r.dtype)

    return pl.pallas_call(
        body, name=name, out_shape=[jax.ShapeDtypeStruct(s, d) for s, d in outs],
        compiler_params=pltpu.CompilerParams(vmem_limit_bytes=VMEM_LIMIT),
    )(*args)


def f_norm(x, g):
    return (_rms(x, g),)


def f_norm_res(x, g):
    return _rms(x, g), x


def f_swiglu(g, u):
    return (g * jax.nn.sigmoid(g) * u,)


def f_glu(y, w, b):
    h = jax.nn.gelu(y)
    return (h * jax.nn.sigmoid(mm(h, w) + b),)


def f_xattn(q, kv, qg, kg):
    outs = []
    for h in range(XA_H):
        sl = slice(h * XA_D, (h + 1) * XA_D)
        qn = _rms(q[:, sl], qg)
        kn = _rms(kv[:, sl], kg)
        vh = kv[:, D + h * XA_D:D + (h + 1) * XA_D]
        p = _softmax(mm_nt(qn, kn) * (XA_D ** -0.5))
        outs.append(mm(p, vh))
    return (jnp.concatenate(outs, axis=-1),)


def f_adam(w, g, m, v):
    m2 = B1 * m + (1.0 - B1) * g
    v2 = B2 * v + (1.0 - B2) * (g * g)
    m_hat = m2 / (1.0 - B1 ** STEP)
    v_hat = v2 / (1.0 - B2 ** STEP)
    delta = -LR * (m_hat / (jnp.sqrt(v_hat) + AEPS) + WD * w)
    return delta, m2, v2


S5_NTAB, S5_NROW = 4, 6


def _s5_tables(are, aim, ldt):
    T = CHUNK
    dt = jnp.exp(ldt)
    ar, ai = are * dt, aim * dt
    t = lax.broadcasted_iota(jnp.int32, (T, 1), 0).astype(F32)
    mag, inv = jnp.exp(t * ar), jnp.exp(-t * ar)
    cs, sn = jnp.cos(t * ai), jnp.sin(t * ai)
    e_re, e_im = mag * cs, mag * sn
    n_re, n_im = inv * cs, -inv * sn
    l_re, l_im = jnp.exp(ar) * jnp.cos(ai), jnp.exp(ar) * jnp.sin(ai)
    den = are * are + aim * aim
    k_re = ((l_re - 1.0) * are + l_im * aim) / den
    k_im = (l_im * are - (l_re - 1.0) * aim) / den
    tl = float(T - 1)
    m_re, m_im = jnp.exp(tl * ar) * jnp.cos(tl * ai), jnp.exp(tl * ar) * jnp.sin(tl * ai)
    return (e_re, e_im, n_re, n_im), (l_re, l_im, k_re, k_im, m_re, m_im)


def _s5_chunk(u, sre, sim, tabs, rows, b_re, b_im, c_re, c_im, dv):
    e_re, e_im, n_re, n_im = tabs
    l_re, l_im, k_re, k_im, m_re, m_im = rows
    x_re, x_im = mm(u, b_re), mm(u, b_im)
    bu_re = k_re * x_re - k_im * x_im
    bu_im = k_re * x_im + k_im * x_re
    v_re = bu_re * n_re - bu_im * n_im
    v_im = bu_re * n_im + bu_im * n_re
    p_re = l_re * sre - l_im * sim
    p_im = l_re * sim + l_im * sre
    w_re = cumsum_rows(v_re) + p_re
    w_im = cumsum_rows(v_im) + p_im
    s_re = e_re * w_re - e_im * w_im
    s_im = e_re * w_im + e_im * w_re
    y = mm(s_re, c_re) - mm(s_im, c_im) + dv * u
    z_re = jnp.sum(v_re, axis=0, keepdims=True) + p_re
    z_im = jnp.sum(v_im, axis=0, keepdims=True) + p_im
    return y, m_re * z_re - m_im * z_im, m_re * z_im + m_im * z_re


def _s5_fill_tables(are_ref, aim_ref, ldt_ref, tab, row):
    for g in range(S5_GB):
        ls = slice(512 * g, 512 * (g + 1))
        tabs, rows = _s5_tables(are_ref[:, ls], aim_ref[:, ls], ldt_ref[:, ls])
        for k, t in enumerate(tabs):
            tab[k, :, ls] = t
        for k, r in enumerate(rows):
            row[k:k + 1, ls] = r


def _s5_read_tables(tab, row, ls):
    return (tuple(tab[k, :, ls] for k in range(S5_NTAB)), tuple(row[k:k + 1, ls] for k in range(S5_NROW)))


def _s5_specs(nc, rev):
    T = CHUNK

    def ci(c):
        return nc - 1 - c if rev else c

    u_spec = pl.BlockSpec((T, S5_W), lambda c: (ci(c), 0))
    p_spec = pl.BlockSpec((1, S5_G * S5_P), lambda c: (0, 0))
    b_spec = pl.BlockSpec((S5_GB, 128, 512), lambda c: (0, 0, 0))
    c_spec = pl.BlockSpec((S5_GB, 512, 128), lambda c: (0, 0, 0))
    d_spec = pl.BlockSpec((1, S5_W), lambda c: (0, 0))
    st_spec = pl.BlockSpec((None, 2, S5_G * S5_P), lambda c: (ci(c), 0, 0))
    return u_spec, p_spec, b_spec, c_spec, d_spec, st_spec


def s5_fwd(pm, are, aim, ldt, b_re, b_im, c_re, c_im, dv):
    S = pm.shape[0]
    nc = S // CHUNK
    u_spec, p_spec, b_spec, c_spec, d_spec, st_spec = _s5_specs(nc, False)

    def body(u_ref, are_ref, aim_ref, ldt_ref, bre_ref, bim_ref, cre_ref, cim_ref, dv_ref,
             y_ref, st_ref, state, tab, row):
        c = pl.program_id(0)

        @pl.when(c == 0)
        def _():
            state[...] = jnp.zeros_like(state)
            _s5_fill_tables(are_ref, aim_ref, ldt_ref, tab, row)

        st_ref[...] = state[...]
        for g in range(S5_GB):
            lu, ls = slice(128 * g, 128 * (g + 1)), slice(512 * g, 512 * (g + 1))
            tabs, rows = _s5_read_tables(tab, row, ls)
            y, e_re, e_im = _s5_chunk(u_ref[:, lu], state[0:1, ls], state[1:2, ls], tabs, rows,
                                      bre_ref[g], bim_ref[g], cre_ref[g], cim_ref[g], dv_ref[:, lu])
            y_ref[:, lu] = y
            state[0:1, ls] = e_re
            state[1:2, ls] = e_im

    n_state = S5_G * S5_P
    return pl.pallas_call(
        body, name="s5_fwd", grid=(nc,),
        in_specs=[u_spec, p_spec, p_spec, p_spec, b_spec, b_spec, c_spec, c_spec, d_spec],
        out_specs=[u_spec, st_spec],
        out_shape=[jax.ShapeDtypeStruct((S, S5_W), F32), jax.ShapeDtypeStruct((nc, 2, n_state), F32)],
        scratch_shapes=[pltpu.VMEM((2, n_state), F32), pltpu.VMEM((S5_NTAB, CHUNK, n_state), F32),
                        pltpu.VMEM((8, n_state), F32)],
        compiler_params=_cp("arbitrary"),
    )(pm, are, aim, ldt, b_re, b_im, c_re, c_im, dv)


def s5_bwd(pm, st, dy, are, aim, ldt, b_re, b_im, c_re, c_im, dv):
    S = pm.shape[0]
    nc = S // CHUNK
    u_spec, p_spec, b_spec, c_spec, d_spec, st_spec = _s5_specs(nc, True)

    def body(u_ref, st_ref, dy_ref, are_ref, aim_ref, ldt_ref, bre_ref, bim_ref, cre_ref, cim_ref, dv_ref,
             du_ref, dare_ref, daim_ref, dldt_ref, dbre_ref, dbim_ref, dcre_ref, dcim_ref, ddv_ref,
             dstate, tab, row, dtab, drow):
        c = pl.program_id(0)

        @pl.when(c == 0)
        def _():
            dstate[...] = jnp.zeros_like(dstate)
            dtab[...] = jnp.zeros_like(dtab)
            drow[...] = jnp.zeros_like(drow)
            _s5_fill_tables(are_ref, aim_ref, ldt_ref, tab, row)

        for g in range(S5_GB):
            lu, ls = slice(128 * g, 128 * (g + 1)), slice(512 * g, 512 * (g + 1))
            every = slice(None)
            tabs, rows = _s5_read_tables(tab, row, ls)
            args = (u_ref[:, lu], st_ref[0:1, ls], st_ref[1:2, ls], tabs, rows,
                    bre_ref[g], bim_ref[g], cre_ref[g], cim_ref[g], dv_ref[:, lu])
            _, vjp = jax.vjp(_s5_chunk, *args)
            gr = vjp((dy_ref[:, lu], dstate[0:1, ls], dstate[1:2, ls]))
            du_ref[:, lu] = gr[0]
            dstate[0:1, ls] = gr[1]
            dstate[1:2, ls] = gr[2]
            for k, t in enumerate(gr[3]):
                dtab[k, :, ls] += t
            for k, r in enumerate(gr[4]):
                drow[k:k + 1, ls] += r
            accs = ((dbre_ref, (g,)), (dbim_ref, (g,)), (dcre_ref, (g,)), (dcim_ref, (g,)), (ddv_ref, (every, lu)))
            for (o, idx), gv in zip(accs, gr[5:]):
                @pl.when(c == 0)
                def _(o=o, idx=idx, gv=gv):
                    o[idx] = gv

                @pl.when(c > 0)
                def _(o=o, idx=idx, gv=gv):
                    o[idx] += gv

        @pl.when(c == nc - 1)
        def _():
            for g in range(S5_GB):
                ls = slice(512 * g, 512 * (g + 1))
                _, vjp = jax.vjp(_s5_tables, are_ref[:, ls], aim_ref[:, ls], ldt_ref[:, ls])
                dtabs, drows = _s5_read_tables(dtab, drow, ls)
                ga, gi, gl = vjp((dtabs, drows))
                dare_ref[:, ls] = ga
                daim_ref[:, ls] = gi
                dldt_ref[:, ls] = gl

    n_state = S5_G * S5_P
    return pl.pallas_call(
        body, name="s5_bwd", grid=(nc,),
        in_specs=[u_spec, st_spec, u_spec, p_spec, p_spec, p_spec, b_spec, b_spec, c_spec, c_spec, d_spec],
        out_specs=[u_spec, p_spec, p_spec, p_spec, b_spec, b_spec, c_spec, c_spec, d_spec],
        out_shape=[jax.ShapeDtypeStruct((S, S5_W), F32)] + [jax.ShapeDtypeStruct((1, n_state), F32)] * 3
        + [jax.ShapeDtypeStruct((S5_GB, 128, 512), F32)] * 2 + [jax.ShapeDtypeStruct((S5_GB, 512, 128), F32)] * 2
        + [jax.ShapeDtypeStruct((1, S5_W), F32)],
        scratch_shapes=[pltpu.VMEM((2, n_state), F32), pltpu.VMEM((S5_NTAB, CHUNK, n_state), F32),
                        pltpu.VMEM((8, n_state), F32), pltpu.VMEM((S5_NTAB, CHUNK, n_state), F32),
                        pltpu.VMEM((8, n_state), F32)],
        compiler_params=_cp("arbitrary"),
    )(pm, st, dy, are, aim, ldt, b_re, b_im, c_re, c_im, dv)


def conv_fwd(pm, w):
    S = pm.shape[0]

    def body(x_ref, w_ref, o_ref, pad):
        x = x_ref[...]
        pad[0:8, :] = jnp.zeros((8, 128), F32)
        pad[8:, :] = x
        y = (w_ref[3:4, :] * x + w_ref[2:3, :] * pad[7:7 + S, :] + w_ref[1:2, :] * pad[6:6 + S, :]
             + w_ref[0:1, :] * pad[5:5 + S, :])
        o_ref[...] = y * jax.nn.sigmoid(y)

    return pl.pallas_call(
        body, name="conv_fwd", grid=(12,),
        in_specs=[pl.BlockSpec((S, 128), lambda j: (0, 4 + j)), pl.BlockSpec((4, 128), lambda j: (0, j))],
        out_specs=pl.BlockSpec((S, 128), lambda j: (0, j)),
        out_shape=jax.ShapeDtypeStruct((S, 1536), F32),
        scratch_shapes=[pltpu.VMEM((S + 8, 128), F32)],
        compiler_params=_cp("parallel"),
    )(pm, w)


def conv_bwd(pm, w, dout):
    S = pm.shape[0]

    def body(x_ref, w_ref, do_ref, dx_ref, dw_ref, pad, dpad):
        x = x_ref[...]
        pad[0:8, :] = jnp.zeros((8, 128), F32)
        pad[8:, :] = x
        xs = [pad[5:5 + S, :], pad[6:6 + S, :], pad[7:7 + S, :], x]
        y = w_ref[0:1, :] * xs[0] + w_ref[1:2, :] * xs[1] + w_ref[2:3, :] * xs[2] + w_ref[3:4, :] * xs[3]
        sg = jax.nn.sigmoid(y)
        dy = do_ref[...] * (sg + y * sg * (1.0 - sg))
        dpad[0:S, :] = dy
        dpad[S:, :] = jnp.zeros((8, 128), F32)
        dx_ref[...] = (w_ref[3:4, :] * dy + w_ref[2:3, :] * dpad[1:1 + S, :] + w_ref[1:2, :] * dpad[2:2 + S, :]
                       + w_ref[0:1, :] * dpad[3:3 + S, :])
        for i in range(4):
            dw_ref[i:i + 1, :] = jnp.sum(dy * xs[i], axis=0, keepdims=True)

    return pl.pallas_call(
        body, name="conv_bwd", grid=(12,),
        in_specs=[pl.BlockSpec((S, 128), lambda j: (0, 4 + j)), pl.BlockSpec((4, 128), lambda j: (0, j)),
                  pl.BlockSpec((S, 128), lambda j: (0, j))],
        out_specs=[pl.BlockSpec((S, 128), lambda j: (0, j)), pl.BlockSpec((4, 128), lambda j: (0, j))],
        out_shape=[jax.ShapeDtypeStruct((S, 1536), F32), jax.ShapeDtypeStruct((4, 1536), F32)],
        scratch_shapes=[pltpu.VMEM((S + 8, 128), F32), pltpu.VMEM((S + 8, 128), F32)],
        compiler_params=_cp("parallel"),
    )(pm, w, dout)


GDN_SUP = 4
GDN_ROWS = GDN_SUP * CHUNK


@jax.custom_vjp
def _saved_inverse(a, x):
    return x


_saved_inverse.defvjp(lambda a, x: (x, x),
                      lambda x, g: (-mm_tn(x, mm_nt(g, x)), jnp.zeros_like(x)))


def _gdn_chunk(q, k, v, gate, al, bl, alog, dtb, og, state, inv=None, want_inv=False):
    R = q.shape[0]
    r = lax.broadcasted_iota(jnp.int32, (R, R), 0)
    c = lax.broadcasted_iota(jnp.int32, (R, R), 1)
    same = (r // CHUNK) == (c // CHUNK)
    eye = (r == c).astype(F32)
    strict, causal, upper = same & (r > c), same & (r >= c), same & (r <= c)
    qn = q * lax.rsqrt(jnp.sum(q * q, axis=-1, keepdims=True) + EPS) * (GDN_D ** -0.5)
    kn = k * lax.rsqrt(jnp.sum(k * k, axis=-1, keepdims=True) + EPS)
    beta = jnp.sum(eye * jax.nn.sigmoid(bl), axis=1, keepdims=True)
    g_row = -jnp.exp(alog) * _softplus(al + dtb)
    g = jnp.sum(eye * g_row, axis=1, keepdims=True)
    gc_col = jnp.sum(jnp.where(causal, g_row, 0.0), axis=1, keepdims=True)
    gc_row = jnp.sum(jnp.where(upper, g, 0.0), axis=0, keepdims=True)
    gtot = jnp.sum(jnp.where(same, g_row, 0.0), axis=1, keepdims=True)
    gamma = jnp.exp(gc_col)
    diff = gc_col - gc_row
    d_strict = jnp.where(strict, jnp.exp(jnp.where(strict, diff, 0.0)), 0.0)
    d_causal = jnp.where(causal, jnp.exp(jnp.where(causal, diff, 0.0)), 0.0)
    a = beta * mm_nt(kn, kn) * d_strict
    if inv is None:
        p = -a
        x = eye + p
        for _ in range(5):
            p = mm(p, p)
            x = x + mm(x, p)
    else:
        x = _saved_inverse(a, inv)
    u_new = mm(x, beta * v)
    w_k = mm(x, (beta * gamma) * kn)
    qk = mm_nt(qn, kn) * d_causal
    q_g = qn * gamma
    k_tail = kn * jnp.exp(gtot - gc_col)
    ws, os_ = [], []
    for i in range(R // CHUNK):
        rows = slice(CHUNK * i, CHUNK * (i + 1))
        w_i = u_new[rows] - mm(w_k[rows], state)
        os_.append(mm(q_g[rows], state))
        decay = jnp.exp(jnp.sum(g[rows], axis=0, keepdims=True))
        state = decay * state + mm_tn(k_tail[rows], w_i)
        ws.append(w_i)
    o = jnp.concatenate(os_, axis=0) + mm(qk, jnp.concatenate(ws, axis=0))
    out = _rms(o, og) * (gate * jax.nn.sigmoid(gate))
    return (out, state, x) if want_inv else (out, state)


def _gdn_specs(nc, rev):
    def ci(c):
        return nc - 1 - c if rev else c

    def blk(cb):
        return pl.BlockSpec((GDN_ROWS, 512), lambda c: (ci(c), cb))

    col = lambda n: pl.BlockSpec((n, GDN_ROWS), lambda c: (0, ci(c)))
    sc = pl.BlockSpec((GDN_H, 1, 1), lambda c: (0, 0, 0))
    og = pl.BlockSpec((1, 128), lambda c: (0, 0))
    st = pl.BlockSpec((GDN_H, None, 128, 128), lambda c: (0, ci(c), 0, 0))
    return blk, col, sc, og, st


def gdn_fwd(qkvc, pm, abt, alog, dtb, og):
    S = qkvc.shape[0]
    nc = S // GDN_ROWS
    blk, col, sc, ogs, st = _gdn_specs(nc, False)

    def body(q_ref, k_ref, v_ref, gate_ref, ab_ref, alog_ref, dtb_ref, og_ref, o_ref, st_ref, inv_ref, state):
        c = pl.program_id(0)

        @pl.when(c == 0)
        def _():
            state[...] = jnp.zeros_like(state)

        st_ref[...] = state[...]
        for h in range(GDN_H):
            sl = slice(GDN_D * h, GDN_D * (h + 1))
            out, new_state, inv = _gdn_chunk(
                q_ref[:, sl], k_ref[:, sl], v_ref[:, sl], gate_ref[:, sl], ab_ref[h:h + 1, :], ab_ref[GDN_H + h:GDN_H + h + 1, :],
                alog_ref[h], dtb_ref[h], og_ref[...], state[h], want_inv=True)
            o_ref[:, sl] = out
            state[h] = new_state
            inv_ref[h] = inv

    inv_spec = pl.BlockSpec((GDN_H, None, GDN_ROWS, GDN_ROWS), lambda c: (0, c, 0, 0))
    return pl.pallas_call(
        body, name="gdn_fwd", grid=(nc,),
        in_specs=[blk(0), blk(1), blk(2), blk(4), col(2 * GDN_H), sc, sc, ogs],
        out_specs=[blk(0), st, inv_spec],
        out_shape=[jax.ShapeDtypeStruct((S, 512), F32), jax.ShapeDtypeStruct((GDN_H, nc, 128, 128), F32),
                   jax.ShapeDtypeStruct((GDN_H, nc, GDN_ROWS, GDN_ROWS), F32)],
        scratch_shapes=[pltpu.VMEM((GDN_H, 128, 128), F32)],
        compiler_params=_cp("arbitrary"),
    )(qkvc, qkvc, qkvc, pm, abt, alog, dtb, og)


def gdn_bwd(qkvc, pm, abt, alog, dtb, og, st, inv, dout):
    S = qkvc.shape[0]
    nc = S // GDN_ROWS
    blk, col, sc, ogs, sts = _gdn_specs(nc, True)

    def body(q_ref, k_ref, v_ref, gate_ref, ab_ref, alog_ref, dtb_ref, og_ref, st_ref, inv_ref, do_ref,
             dq_ref, dk_ref, dv_ref, dgate_ref, dal_ref, dbl_ref, dalog_ref, ddtb_ref, dog_ref, dstate):
        c = pl.program_id(0)

        @pl.when(c == 0)
        def _():
            dstate[...] = jnp.zeros_like(dstate)

        for h in range(GDN_H):
            sl = slice(GDN_D * h, GDN_D * (h + 1))
            args = (q_ref[:, sl], k_ref[:, sl], v_ref[:, sl], gate_ref[:, sl], ab_ref[h:h + 1, :], ab_ref[GDN_H + h:GDN_H + h + 1, :],
                    alog_ref[h], dtb_ref[h], og_ref[...], st_ref[h])
            inv_h = inv_ref[h]
            _, vjp = jax.vjp(lambda *a, inv_h=inv_h: _gdn_chunk(*a, inv=inv_h), *args)
            g = vjp((do_ref[:, sl], dstate[h]))
            for o, gv in zip((dq_ref, dk_ref, dv_ref, dgate_ref), g[:4]):
                o[:, sl] = gv
            dal_ref[h:h + 1, :] = g[4]
            dbl_ref[h:h + 1, :] = g[5]
            dstate[h] = g[9]
            for o, gv in zip((dalog_ref, ddtb_ref, dog_ref), g[6:9]):
                @pl.when(c == 0)
                def _(o=o, gv=gv, h=h):
                    o[h] = gv

                @pl.when(c > 0)
                def _(o=o, gv=gv, h=h):
                    o[h] += gv

    ogo = pl.BlockSpec((GDN_H, 1, 128), lambda c: (0, 0, 0))
    inv_spec = pl.BlockSpec((GDN_H, None, GDN_ROWS, GDN_ROWS), lambda c: (0, nc - 1 - c, 0, 0))
    sd = jax.ShapeDtypeStruct
    return pl.pallas_call(
        body, name="gdn_bwd", grid=(nc,),
        in_specs=[blk(0), blk(1), blk(2), blk(4), col(2 * GDN_H), sc, sc, ogs, sts, inv_spec, blk(0)],
        out_specs=[blk(0), blk(0), blk(0), blk(0), col(GDN_H), col(GDN_H), sc, sc, ogo],
        out_shape=[sd((S, 512), F32)] * 4 + [sd((GDN_H, S), F32)] * 2 + [sd((GDN_H, 1, 1), F32)] * 2
        + [sd((GDN_H, 1, 128), F32)],
        scratch_shapes=[pltpu.VMEM((GDN_H, 128, 128), F32)],
        compiler_params=_cp("arbitrary"),
    )(qkvc, qkvc, qkvc, pm, abt, alog, dtb, og, st, inv, dout)


HG = 8
HG_LANES = HG * CA_D


def _group_mean_raw(y):
    r = lax.broadcasted_iota(jnp.int32, (128, 128), 0)
    c = lax.broadcasted_iota(jnp.int32, (128, 128), 1)
    g = jnp.where((r // CA_D) == (c // CA_D), 1.0 / CA_D, 0.0).astype(BF16)
    d = lambda u: lax.dot_general(u, g, (((1,), (0,)), ((), ())), preferred_element_type=F32)
    outs = []
    for j in range(y.shape[1] // 128):
        hi, lo = _split(y[:, 128 * j:128 * (j + 1)])
        outs.append(d(hi) + d(lo))
    return jnp.concatenate(outs, axis=1)


@jax.custom_vjp
def group_mean(y):
    return _group_mean_raw(y)


group_mean.defvjp(lambda y: (_group_mean_raw(y), None), lambda _, g: (_group_mean_raw(g),))


def f_headnorm(t, g):
    return (t * lax.rsqrt(group_mean(t * t) + EPS) * g,)


def _cattn_chunk(q, kb, vb, bias, valid):
    lane = lax.broadcasted_iota(jnp.int32, (1, 128), 1)
    m0 = (lane < CA_D).astype(F32)
    m1 = 1.0 - m0
    pairs = range(HG // 2)
    sl = [slice(128 * p, 128 * (p + 1)) for p in pairs]
    q2 = [jnp.concatenate([q[:, s] * m0, q[:, s] * m1], axis=0) for s in sl]
    sc = [mm_nt(q2[p], kb[:, sl[p]]) * (CA_D ** -0.5) + bias[sl[p]] for p in pairs]
    pr = [_softmax(jnp.where(valid, s, -1e30)) for s in sc]
    o2 = [mm(pr[p], vb[:, sl[p]]) for p in pairs]
    return jnp.concatenate([o[:CHUNK] * m0 + o[CHUNK:] * m1 for o in o2], axis=1)


def _cattn_valid(c):
    pos = lax.broadcasted_iota(jnp.int32, (1, CA_BAND), 1) + c * CHUNK
    return pos >= CA_PAD


def _cattn_specs(S):
    q_spec = pl.BlockSpec((CHUNK, HG_LANES), lambda h, c: (c, h))
    kv_spec = pl.BlockSpec((S + CA_PAD, HG_LANES), lambda h, c: (0, h))
    b_spec = pl.BlockSpec((HG * CHUNK, CA_BAND), lambda h, c: (h, 0))
    return q_spec, kv_spec, b_spec


def cattn_fwd(qn, kp, vp, bias):
    S = qn.shape[0]
    nc = S // CHUNK
    q_spec, kv_spec, b_spec = _cattn_specs(S)

    def body(q_ref, k_ref, v_ref, b_ref, o_ref):
        c = pl.program_id(1)
        start = pl.multiple_of(c * CHUNK, CHUNK)
        kb = k_ref[pl.ds(start, CA_BAND), :]
        vb = v_ref[pl.ds(start, CA_BAND), :]
        o_ref[...] = _cattn_chunk(q_ref[...], kb, vb, b_ref[...], _cattn_valid(c)).astype(o_ref.dtype)

    return pl.pallas_call(
        body, name="cattn_fwd", grid=(CA_H // HG, nc), in_specs=[q_spec, kv_spec, kv_spec, b_spec],
        out_specs=q_spec, out_shape=jax.ShapeDtypeStruct((S, D), BF16),
        compiler_params=_cp("parallel", "arbitrary"),
    )(qn, kp, vp, bias)


def kv_prep(qkv, kg):
    S = qkv.shape[0]
    tile = ROW_TILE
    lead = CA_PAD // tile

    def body(k_ref, v_ref, g_ref, kp_ref, vp_ref):
        i = pl.program_id(0)

        @pl.when(i < lead)
        def _():
            kp_ref[...] = jnp.zeros_like(kp_ref)
            vp_ref[...] = jnp.zeros_like(vp_ref)

        @pl.when(i >= lead)
        def _():
            kp_ref[...] = f_headnorm(k_ref[...], g_ref[...])[0].astype(BF16)
            vp_ref[...] = v_ref[...].astype(BF16)

    src = lambda cb: pl.BlockSpec((tile, D), lambda i, cb=cb: (jnp.maximum(i - lead, 0), cb))
    out = pl.BlockSpec((tile, D), lambda i: (i, 0))
    return pl.pallas_call(
        body, name="kv_prep", grid=((S + CA_PAD) // tile,),
        in_specs=[src(1), src(2), pl.BlockSpec((1, D), lambda i: (0, 0))], out_specs=[out, out],
        out_shape=[jax.ShapeDtypeStruct((S + CA_PAD, D), BF16)] * 2,
        compiler_params=_cp("parallel"),
    )(qkv, qkv, kg)


def cattn_bwd(qn, kp, vp, bias, do):
    S = qn.shape[0]
    nc = S // CHUNK
    q_spec, kv_spec, b_spec = _cattn_specs(S)

    def body(q_ref, k_ref, v_ref, b_ref, do_ref, dq_ref, dk_ref, dv_ref, db_ref):
        c = pl.program_id(1)

        @pl.when(c == 0)
        def _():
            dk_ref[...] = jnp.zeros_like(dk_ref)
            dv_ref[...] = jnp.zeros_like(dv_ref)
            db_ref[...] = jnp.zeros_like(db_ref)

        start = pl.multiple_of(c * CHUNK, CHUNK)
        kb = k_ref[pl.ds(start, CA_BAND), :].astype(F32)
        vb = v_ref[pl.ds(start, CA_BAND), :].astype(F32)
        valid = _cattn_valid(c)
        _, vjp = jax.vjp(lambda q, k, v, b: _cattn_chunk(q, k, v, b, valid), q_ref[...], kb, vb, b_ref[...])
        dq, dk, dv, db = vjp(do_ref[...])
        dq_ref[...] = dq
        dk_ref[pl.ds(start, CA_BAND), :] += dk
        dv_ref[pl.ds(start, CA_BAND), :] += dv
        db_ref[...] += db

    sd = jax.ShapeDtypeStruct
    return pl.pallas_call(
        body, name="cattn_bwd", grid=(CA_H // HG, nc), in_specs=[q_spec, kv_spec, kv_spec, b_spec, q_spec],
        out_specs=[q_spec, kv_spec, kv_spec, b_spec],
        out_shape=[sd((S, D), F32), sd((S + CA_PAD, D), F32), sd((S + CA_PAD, D), F32),
                   sd((CA_H * CHUNK, CA_BAND), F32)],
        compiler_params=_cp("parallel", "arbitrary"),
    )(qn, kp, vp, bias, do)


_REL_IDX = np.clip(np.arange(CHUNK)[:, None] - np.arange(CA_BAND)[None, :] + CA_PAD, -MAX_REL, MAX_REL) + MAX_REL
SKEW_W = CA_BAND + CHUNK


def rel_bias_grad(dbias):
    padded = jnp.pad(dbias, ((0, 0), (0, 0), (CHUNK, 0)))
    flat = jnp.pad(padded.reshape(CA_H, CHUNK * SKEW_W), ((0, 0), (0, CHUNK)))
    skew = flat.reshape(CA_H, CHUNK, SKEW_W + 1)

    first_near = SKEW_W - CHUNK - MAX_REL

    def fn(t):
        colsum = jnp.sum(t, axis=1, keepdims=True)
        j = lax.broadcasted_iota(jnp.int32, colsum.shape, 2)
        far = jnp.sum(jnp.where(j < first_near, colsum, 0.0), axis=2, keepdims=True)
        return (colsum + jnp.where(j == first_near, far, 0.0),)

    (colsum,) = whole("relbias_sum", fn, [skew], [((CA_H, 1, SKEW_W + 1), F32)])
    near = colsum[:, 0, first_near:SKEW_W][:, ::-1]
    return jnp.concatenate([jnp.zeros((CA_H, CHUNK + 1), F32), near], axis=1)


def rel_bias_expand(rb):
    near = rb[:, CHUNK + 1:][:, ::-1]
    far = jnp.broadcast_to(rb[:, 2 * MAX_REL:], (CA_H, SKEW_W - CHUNK - MAX_REL))
    t = jnp.concatenate([far, near, jnp.zeros((CA_H, 1), rb.dtype)], axis=1)
    rows = jnp.tile(t, (1, CHUNK))[:, :CHUNK * SKEW_W].reshape(CA_H, CHUNK, SKEW_W)
    return rows[:, :, CHUNK:]


def loss_head(y, target):
    S = y.shape[0]
    tile = min(ROW_TILE, S)

    def body(y_ref, t_ref, dy_ref, acc_ref):
        i = pl.program_id(0)
        e = y_ref[...] - t_ref[...]
        dy_ref[...] = e * (1.0 / D)
        part = jnp.sum(e * e, axis=0, keepdims=True) * (0.5 / D)

        @pl.when(i == 0)
        def _():
            acc_ref[...] = part

        @pl.when(i > 0)
        def _():
            acc_ref[...] += part

    row = pl.BlockSpec((tile, D), lambda i: (i, 0))
    return pl.pallas_call(
        body, name="loss_head", grid=(S // tile,), in_specs=[row, row],
        out_specs=[row, pl.BlockSpec((1, D), lambda i: (0, 0))],
        out_shape=[jax.ShapeDtypeStruct((S, D), F32), jax.ShapeDtypeStruct((1, D), F32)],
        compiler_params=_cp("arbitrary"),
    )(y, target)


ANY = pl.BlockSpec(memory_space=pl.ANY)


HBM = pl.BlockSpec(memory_space=pltpu.HBM)
SEM = pl.BlockSpec(memory_space=pltpu.SEMAPHORE)
EFFECT = pltpu.SideEffectType.DATAFLOW_SIDE_EFFECTING


def _chip_copies(src_ref, land_ref, send_sems, recv_sems, per_dest):
    x, y, c = lax.axis_index("x"), lax.axis_index("y"), lax.axis_index("c")
    me = 2 * x + y
    if per_dest == "sibling":
        cp = pltpu.make_async_remote_copy(src_ref=src_ref, dst_ref=land_ref, send_sem=send_sems.at[0],
                                          recv_sem=recv_sems.at[0], device_id=(x, y, 1 - c), device_id_type=MESH)
        return [(cp, cp)]
    out = []
    for j, (px, py) in enumerate([(1 - x, y), (x, 1 - y), (1 - x, 1 - y)]):
        peer = 2 * px + py
        if per_dest:
            send = (src_ref.at[peer], land_ref.at[j])
            recv = (src_ref.at[me], land_ref.at[j])
        else:
            send = (src_ref, land_ref.at[me])
            recv = (src_ref, land_ref.at[peer])
        mk = lambda s, d, j=j, px=px, py=py: pltpu.make_async_remote_copy(
            src_ref=s, dst_ref=d, send_sem=send_sems.at[j], recv_sem=recv_sems.at[j],
            device_id=(px, py, c), device_id_type=MESH)
        out.append((mk(*send), mk(*recv)))
    return out


def exchange_start(src, land, carry, name, per_dest):
    def body(src_ref, land_ref, carry_ref, send_sems, recv_sems, src_out, land_out, carry_out):
        for send, _ in _chip_copies(src_ref, land_ref, send_sems, recv_sems, per_dest):
            send.start()

    hbm = lambda a: pltpu.HBM(a.shape, a.dtype)
    n = 1 if per_dest == "sibling" else 3
    return pl.pallas_call(
        body, name=name,
        out_shape=(pltpu.SemaphoreType.DMA((n,)), pltpu.SemaphoreType.DMA((n,)), hbm(src), hbm(land), hbm(carry)),
        in_specs=(HBM, HBM, HBM), out_specs=(SEM, SEM, HBM, HBM, HBM),
        input_output_aliases={0: 2, 1: 3, 2: 4},
        compiler_params=pltpu.CompilerParams(has_side_effects=EFFECT),
    )(pltpu.with_memory_space_constraint(src, pltpu.HBM), pltpu.with_memory_space_constraint(land, pltpu.HBM),
      pltpu.with_memory_space_constraint(carry, pltpu.HBM))


def exchange_wait(src, land, send_sems, recv_sems, after, name, per_dest):
    def body(src_ref, land_ref, send_sems_ref, recv_sems_ref, after_ref, src_out, land_out):
        for send, recv in _chip_copies(src_ref, land_ref, send_sems_ref, recv_sems_ref, per_dest):
            send.wait_send()
            recv.wait_recv()

    hbm = lambda a: pltpu.HBM(a.shape, a.dtype)
    return pl.pallas_call(
        body, name=name, out_shape=(hbm(src), hbm(land)),
        in_specs=(HBM, HBM, SEM, SEM, ANY), out_specs=(HBM, HBM), input_output_aliases={0: 0, 1: 1},
        compiler_params=pltpu.CompilerParams(has_side_effects=EFFECT),
    )(src, land, send_sems, recv_sems, after)


def sibling_exchange(srcs, name):
    n = len(srcs)

    def body(*refs):
        src_refs, out_refs, send_sems, recv_sems = refs[:n], refs[n:2 * n], refs[2 * n], refs[2 * n + 1]
        x, y, c = lax.axis_index("x"), lax.axis_index("y"), lax.axis_index("c")
        copies = [pltpu.make_async_remote_copy(src_ref=s, dst_ref=o, send_sem=send_sems.at[k], recv_sem=recv_sems.at[k],
                                               device_id=(x, y, 1 - c), device_id_type=MESH)
                  for k, (s, o) in enumerate(zip(src_refs, out_refs))]
        for cp in copies:
            cp.start()
        for cp in copies:
            cp.wait()

    return pl.pallas_call(
        body, name=name, in_specs=[ANY] * n, out_specs=[ANY] * n,
        out_shape=[jax.ShapeDtypeStruct(s.shape, s.dtype) for s in srcs],
        scratch_shapes=[pltpu.SemaphoreType.DMA((n,)), pltpu.SemaphoreType.DMA((n,))],
    )(*srcs)


def chip_gather(src, name):
    def body(src_ref, out_ref, send_sems, recv_sems, local_sem):
        x, y, c = lax.axis_index("x"), lax.axis_index("y"), lax.axis_index("c")
        me = 2 * x + y
        local = pltpu.make_async_copy(src_ref, out_ref.at[me], local_sem)
        local.start()
        peers = [(1 - x, y), (x, 1 - y), (1 - x, 1 - y)]
        mk = lambda j, px, py, slab: pltpu.make_async_remote_copy(
            src_ref=src_ref, dst_ref=out_ref.at[slab], send_sem=send_sems.at[j], recv_sem=recv_sems.at[j],
            device_id=(px, py, c), device_id_type=MESH)
        sends = [mk(j, px, py, me) for j, (px, py) in enumerate(peers)]
        for cp in sends:
            cp.start()
        for j, (px, py) in enumerate(peers):
            mk(j, px, py, 2 * px + py).wait_recv()
        for cp in sends:
            cp.wait_send()
        local.wait()

    return pl.pallas_call(
        body, name=name, in_specs=[ANY], out_specs=ANY,
        out_shape=jax.ShapeDtypeStruct((N_CHIPS,) + tuple(src.shape), src.dtype),
        scratch_shapes=[pltpu.SemaphoreType.DMA((3,)), pltpu.SemaphoreType.DMA((3,)), pltpu.SemaphoreType.DMA],
    )(src)


def all_exchange(src, name):
    def body(src_ref, out_ref, send_sems, recv_sems, local_sem):
        x, y, c = lax.axis_index("x"), lax.axis_index("y"), lax.axis_index("c")
        me = 4 * x + 2 * y + c
        local = pltpu.make_async_copy(src_ref, out_ref.at[me], local_sem)
        local.start()
        sends = []
        peers = []
        for k in range(1, N_DEV):
            bx, by, bc = (k >> 2) & 1, (k >> 1) & 1, k & 1
            px = 1 - x if bx else x
            py = 1 - y if by else y
            pc = 1 - c if bc else c
            peers.append((px, py, pc))
        for k, peer in enumerate(peers):
            cp = pltpu.make_async_remote_copy(src_ref=src_ref, dst_ref=out_ref.at[me], send_sem=send_sems.at[k],
                                              recv_sem=recv_sems.at[k], device_id=peer, device_id_type=MESH)
            cp.start()
            sends.append(cp)
        for k, (px, py, pc) in enumerate(peers):
            pltpu.make_async_remote_copy(src_ref=src_ref, dst_ref=out_ref.at[4 * px + 2 * py + pc],
                                         send_sem=send_sems.at[k], recv_sem=recv_sems.at[k],
                                         device_id=(px, py, pc), device_id_type=MESH).wait_recv()
        for cp in sends:
            cp.wait_send()
        local.wait()

    return pl.pallas_call(
        body, name=name, in_specs=[ANY], out_specs=ANY,
        out_shape=jax.ShapeDtypeStruct((N_DEV,) + tuple(src.shape), src.dtype),
        scratch_shapes=[pltpu.SemaphoreType.DMA((N_DEV - 1,)), pltpu.SemaphoreType.DMA((N_DEV - 1,)),
                        pltpu.SemaphoreType.DMA],
    )(src)


def sum_own_slabs(own, land, name, tile=512):
    R, C = own.shape
    n = land.shape[0]
    tile = _tile(R, tile, 16)

    def body(o_ref, t_ref, out_ref):
        acc = o_ref[...].astype(F32)
        for s in range(n):
            acc = acc + t_ref[s].astype(F32)
        out_ref[...] = acc

    return pl.pallas_call(
        body, name=name, grid=(R // tile,),
        in_specs=[pl.BlockSpec((tile, C), lambda i: (i, 0)), pl.BlockSpec((n, tile, C), lambda i: (0, i, 0))],
        out_specs=pl.BlockSpec((tile, C), lambda i: (i, 0)), out_shape=jax.ShapeDtypeStruct((R, C), F32),
        compiler_params=_cp("parallel"),
    )(own, land)


def sum_slabs(t, name, tile=512):
    n, R, C = t.shape
    tile = _tile(R, tile, 16)

    def body(t_ref, o_ref):
        acc = t_ref[0].astype(F32)
        for s in range(1, n):
            acc = acc + t_ref[s].astype(F32)
        o_ref[...] = acc

    return pl.pallas_call(
        body, name=name, grid=(R // tile,), in_specs=[pl.BlockSpec((n, tile, C), lambda i: (0, i, 0))],
        out_specs=pl.BlockSpec((tile, C), lambda i: (i, 0)), out_shape=jax.ShapeDtypeStruct((R, C), F32),
        compiler_params=_cp("parallel"),
    )(t)


PACK_ROW_MULT = 512


def _pad_rows(a, mult=16):
    r = (-a.shape[0]) % mult
    return jnp.pad(a, ((0, r), (0, 0))) if r else a


BIG = [
    ("ab_w_in", True, 2, 642), ("c_w_qkv", True, 2, 768), ("xa_w_kv", True, 4, 512),
    ("f_w_gate", True, 4, 704), ("f_w_up", True, 4, 704),
    ("ab_w_out", False, 2, 256), ("c_w_out", False, 2, 256), ("xa_w_q", False, 4, 256),
    ("xa_w_out", False, 4, 256), ("f_w_down", False, 4, 704), ("s5_w_glu", False, 2, 64),
]


GROUPS = ("B", "A")
GROUP_ROW_MULT = 64


def group_spec(layer, grp):
    i = layer // 2
    if grp == "A":
        return [("xa_w_kv", layer, True, 512), ("xa_w_q", layer, False, 256), ("xa_w_out", layer, False, 256),
                ("f_w_gate", layer, True, 704), ("f_w_up", layer, True, 704), ("f_w_down", layer, False, 704)]
    if layer % 2 == 0:
        return [("ab_w_in", i, True, 642), ("ab_w_out", i, False, 256), ("s5_w_glu", i, False, 64)]
    return [("c_w_qkv", i, True, 768), ("c_w_out", i, False, 256)]


def _seg_rows(rows):
    return rows + ((-rows) % 16)


def _f32_rows(a):
    bits = lax.bitcast_convert_type(a.reshape(-1), BF16).reshape(-1)
    return jnp.pad(bits, (0, 16 * D - bits.shape[0])).reshape(16, D)


def pack_group_shards(p, layer, grp):
    segs = []
    for name, idx, transposed, rows in group_spec(layer, grp):
        w = p[name][idx]
        if transposed:
            w = w.T
        segs.append(_pad_rows(w.astype(BF16).reshape(-1, D)))
    if grp == "B":
        small = p["gdn_conv_w"] if layer % 2 == 0 else p["c_norm_g"]
        segs.append(_f32_rows(small[layer // 2]))
    return _pad_rows(jnp.concatenate(segs, axis=0), GROUP_ROW_MULT)


def unpack_group_gathered(g, layer, grp):
    out, off = {}, 0
    for name, idx, transposed, rows in group_spec(layer, grp):
        seg = g[:, off:off + rows]
        if name == "s5_w_glu":
            out[name] = seg.reshape(N_CHIPS * 128, 512)
        else:
            out[name] = seg.reshape(N_CHIPS * rows, D)
        off += _seg_rows(rows)
    if grp == "B":
        n = 4 * 384 if layer % 2 == 0 else 256
        bits = g[:, off:off + 16].reshape(N_CHIPS, -1)[:, :2 * n].reshape(N_CHIPS, n, 2)
        small = lax.bitcast_convert_type(bits, F32)
        if layer % 2 == 0:
            out["gdn_conv_w"] = jnp.swapaxes(small.reshape(N_CHIPS, 4, 384), 0, 1).reshape(4, 1536)
        else:
            out["c_norm_g"] = small.reshape(D)
    return out


def pack_group_grads(gr, layer, grp):
    segs = []
    for name, idx, transposed, rows in group_spec(layer, grp):
        w = gr[name].astype(BF16)
        seg = w.reshape(N_CHIPS, rows, D)
        r = (-rows) % 16
        if r:
            seg = jnp.pad(seg, ((0, 0), (0, r), (0, 0)))
        segs.append(seg)
    out = jnp.concatenate(segs, axis=1)
    return jnp.pad(out, ((0, 0), (0, (-out.shape[1]) % GROUP_ROW_MULT), (0, 0)))


def unpack_group_reduced(g, layer, grp):
    out, off = {}, 0
    for name, idx, transposed, rows in group_spec(layer, grp):
        seg = g[off:off + rows]
        if name == "s5_w_glu":
            out[name] = seg.reshape(128, 512)
        else:
            out[name] = seg.T if (transposed and name not in ADAM_TRANSPOSED) else seg
        off += _seg_rows(rows)
    return out


ADAM_TRANSPOSED = ("f_w_gate", "f_w_up")


SMALL = ["ab_norm_g", "s5_a_re", "s5_a_im", "s5_log_dt", "s5_b_re", "s5_b_im", "s5_c_re", "s5_c_im", "s5_d",
         "s5_b_glu", "gdn_conv_w", "gdn_a_log", "gdn_dt_bias", "gdn_out_norm_g", "c_norm_g", "c_q_norm_g",
         "c_k_norm_g", "c_rel_bias", "mem_norm_g", "xa_norm_g", "xa_q_norm_g", "xa_k_norm_g", "f_norm_g"]


def _lane_rows(a):
    flat = a.reshape(-1).astype(F32)
    return jnp.pad(flat, (0, (-flat.shape[0]) % 1024)).reshape(-1, 128)


def pack_small(d, extra=None):
    parts = [_lane_rows(d[n]) for n in SMALL]
    if extra is not None:
        parts.append(_lane_rows(extra))
    rows = jnp.concatenate(parts, axis=0)
    return jnp.pad(rows, ((0, (-rows.shape[0]) % 128), (0, 0)))


def unpack_small(rows, shapes):
    out, off = {}, 0
    for n in SMALL:
        sz = int(np.prod(shapes[n]))
        k = 8 * -(-sz // 1024)
        out[n] = rows[off:off + k].reshape(-1)[:sz].reshape(shapes[n])
        off += k
    return out, rows[off:]


def _s5_blockdiag_b(b):
    bt = jnp.swapaxes(b, 1, 2).reshape(S5_GB, 8, S5_C, S5_P)
    eye = jnp.eye(8, dtype=b.dtype)
    return jnp.einsum("bgcp,gh->bgchp", bt, eye).reshape(S5_GB, 8 * S5_C, 8 * S5_P)


def _s5_blockdiag_c(c):
    ct = jnp.swapaxes(c, 1, 2).reshape(S5_GB, 8, S5_P, S5_C)
    eye = jnp.eye(8, dtype=c.dtype)
    return jnp.einsum("bgpc,gh->bgphc", ct, eye).reshape(S5_GB, 8 * S5_P, 8 * S5_C)


def _s5_diag_b(db):
    t = db.reshape(S5_GB, 8, S5_C, 8, S5_P)
    t = jnp.transpose(t, (0, 2, 4, 1, 3)).reshape(S5_GB, S5_C, S5_P, 64)
    d = t[..., ::9]
    return jnp.transpose(d, (0, 3, 2, 1)).reshape(S5_G, S5_P, S5_C)


def _s5_diag_c(dc):
    t = dc.reshape(S5_GB, 8, S5_P, 8, S5_C)
    t = jnp.transpose(t, (0, 2, 4, 1, 3)).reshape(S5_GB, S5_P, S5_C, 64)
    d = t[..., ::9]
    return jnp.transpose(d, (0, 3, 2, 1)).reshape(S5_G, S5_C, S5_P)


def _heads(t):
    return jnp.swapaxes(t.reshape(t.shape[0], CA_H, CA_D), 0, 1)


def _unheads(t):
    return jnp.swapaxes(t, 0, 1).reshape(t.shape[1], D)


def local_step(x, mem, target, p, wsrc, gsink):
    S = x.shape[0]
    row2 = lambda a: a.reshape(1, -1)
    saved = []
    (mem_n,) = row_fwd("mem_norm", f_norm, [mem], [row2(p["mem_norm_g"])], [(D, BF16)])
    gs = {}

    for layer in range(DEPTH):
        i = layer // 2
        w = wsrc(layer, "B", x)
        sv = {"x0": x, "wB": w}
        if layer % 2 == 0:
            (h,) = row_fwd("norm", f_norm, [x], [row2(p["ab_norm_g"][i])], [(D, BF16)])
            w_in = w["ab_w_in"]
            pm = matmul("nt", h, w_in[:2560], "proj_main")
            abt = matmul("nt", w_in[2560:], h, "proj_ab")
            s5p = dict(
                are=p["s5_a_re"][i].reshape(1, -1), aim=p["s5_a_im"][i].reshape(1, -1),
                ldt=jnp.broadcast_to(p["s5_log_dt"][i][:, None], (S5_G, S5_P)).reshape(1, -1),
                b_re=_s5_blockdiag_b(p["s5_b_re"][i]), b_im=_s5_blockdiag_b(p["s5_b_im"][i]),
                c_re=_s5_blockdiag_c(p["s5_c_re"][i]), c_im=_s5_blockdiag_c(p["s5_c_im"][i]),
                dv=p["s5_d"][i].reshape(1, -1))
            y5, st5 = s5_fwd(pm, **s5p)
            (a_out,) = row_fwd("glu", f_glu, [y5], [w["s5_w_glu"], row2(p["s5_b_glu"][i])], [(S5_W, F32)])
            conv_w = w["gdn_conv_w"]
            qkvc = conv_fwd(pm, conv_w)
            alog = p["gdn_a_log"][i].reshape(GDN_H, 1, 1)
            dtb = p["gdn_dt_bias"][i].reshape(GDN_H, 1, 1)
            og = row2(p["gdn_out_norm_g"][i])
            b_out, stg, inv = gdn_fwd(qkvc, pm, abt, alog, dtb, og)
            cat = jnp.concatenate([a_out, b_out], axis=1)
            x, hq = matmul("nn", cat, w["ab_w_out"], "mix_out", add=x, norm_out=row2(p["xa_norm_g"][layer]))
            sv.update(h=h, pm=pm, s5p=s5p, y5=y5, st5=st5, qkvc=qkvc, abt=abt, alog=alog, dtb=dtb, og=og,
                      stg=stg, inv=inv, cat=cat, conv_w=conv_w)
        else:
            (h,) = row_fwd("norm", f_norm, [x], [row2(w["c_norm_g"])], [(D, BF16)])
            qkv = matmul("nt", h, w["c_w_qkv"], "proj_qkv")
            qg = jnp.tile(row2(p["c_q_norm_g"][i]), (1, CA_H))
            kg = jnp.tile(row2(p["c_k_norm_g"][i]), (1, CA_H))
            (qn,) = row_fwd("headnorm_q", f_headnorm, [(qkv, D, 0)], [qg], [(D, F32)])
            kp, vp = kv_prep(qkv, kg)
            bias = rel_bias_expand(p["c_rel_bias"][i]).reshape(CA_H * CHUNK, CA_BAND)
            o = cattn_fwd(qn, kp, vp, bias)
            x, hq = matmul("nn", o, w["c_w_out"], "mix_out", add=x, norm_out=row2(p["xa_norm_g"][layer]))
            sv.update(h=h, qkv=qkv, qg=qg, kg=kg, qn=qn, kp=kp, vp=vp, bias=bias, o=o)
        sv["x1"] = x
        w = wsrc(layer, "A", x)
        sv["wA"] = w
        qx = matmul("nn", hq, w["xa_w_q"], "xa_q")
        kv = matmul("nt", mem_n, w["xa_w_kv"], "xa_kv")
        xqg, xkg = row2(p["xa_q_norm_g"][layer]), row2(p["xa_k_norm_g"][layer])
        (ox,) = row_fwd("xattn", f_xattn, [qx], [kv, xqg, xkg], [(D, BF16)])
        x, hf = matmul("nn", ox, w["xa_w_out"], "xa_out", add=x, norm_out=row2(p["f_norm_g"][layer]))
        sv.update(hq=hq, qx=qx, kv=kv, ox=ox)
        sv["x2"] = x
        gate, up, act = ffn_in(hf, w["f_w_gate"], w["f_w_up"])
        x = matmul("nn", act, w["f_w_down"], "ffn_down", add=x)
        sv.update(hf=hf, gate=gate, up=up, act=act)
        saved.append(sv)

    dx, loss_vec = loss_head(x, target)

    dmem_n = None
    for layer in reversed(range(DEPTH)):
        i = layer // 2
        sv = saved[layer]
        w, gw = sv["wA"], {}
        dgate, dup = ffn_dact(dx, w["f_w_down"], sv["gate"], sv["up"])
        gw["f_w_down"] = matmul("tn", sv["act"], dx, "ffn_dwd", out_dtype=BF16)
        gw["f_w_gate"] = matmul("tn", dgate, sv["hf"], "ffn_dwg", out_dtype=BF16)
        gw["f_w_up"] = matmul("tn", dup, sv["hf"], "ffn_dwu", out_dtype=BF16)
        dh = matmul("nn", dgate, w["f_w_gate"], "ffn_dhg")
        dx, dg = matmul("nn", dup, w["f_w_up"], "ffn_dhu", add=dh,
                        norm_bwd=(sv["x2"], row2(p["f_norm_g"][layer]), dx))
        gs.setdefault("f_norm_g", [None] * DEPTH)[layer] = dg[0]
        do = matmul("nt", dx, w["xa_w_out"], "xa_do")
        gw["xa_w_out"] = matmul("tn", sv["ox"], dx, "xa_dwo", out_dtype=BF16)
        xqg, xkg = row2(p["xa_q_norm_g"][layer]), row2(p["xa_k_norm_g"][layer])
        dqx, dkv, dqg, dkg = row_bwd("xattn_bwd", f_xattn, [sv["qx"]], [sv["kv"], xqg, xkg], [do],
                                     [0], [0, 1, 2])
        gs.setdefault("xa_q_norm_g", [None] * DEPTH)[layer] = dqg[0]
        gs.setdefault("xa_k_norm_g", [None] * DEPTH)[layer] = dkg[0]
        gw["xa_w_q"] = matmul("tn", sv["hq"], dqx, "xa_dwq", out_dtype=BF16)
        gw["xa_w_kv"] = matmul("tn", dkv, mem_n, "xa_dwkv", out_dtype=BF16)
        dx = gsink(layer, "A", gw, dx)
        dmem_n = matmul("nn", dkv, w["xa_w_kv"], "xa_dmem", add=dmem_n)
        dx, dg = matmul("nt", dqx, w["xa_w_q"], "xa_dhq", norm_bwd=(sv["x1"], row2(p["xa_norm_g"][layer]), dx))
        gs.setdefault("xa_norm_g", [None] * DEPTH)[layer] = dg[0]
        w, gw = sv["wB"], {}
        if layer % 2 == 0:
            dcat = matmul("nt", dx, w["ab_w_out"], "mix_dcat")
            gw["ab_w_out"] = matmul("tn", sv["cat"], dx, "mix_dwo", out_dtype=BF16)
            dy5, dwglu, dbglu = row_bwd("glu_bwd", f_glu, [sv["y5"]], [w["s5_w_glu"], row2(p["s5_b_glu"][i])],
                                        [(dcat, S5_W, 0)], [0], [0, 1])
            gw["s5_w_glu"] = dwglu
            gs.setdefault("s5_b_glu", [None] * 2)[i] = dbglu[0]
            s5p = sv["s5p"]
            du, dare, daim, dldt, dbre, dbim, dcre, dcim, ddv = s5_bwd(sv["pm"], sv["st5"], dy5, **s5p)
            (dldt_g,) = whole("s5_dt_sum", lambda t: (jnp.sum(t, axis=1, keepdims=True),),
                              [dldt.reshape(S5_G, S5_P)], [((S5_G, 1), F32)])
            for nme, val in (("s5_a_re", dare.reshape(S5_G, S5_P)), ("s5_a_im", daim.reshape(S5_G, S5_P)),
                             ("s5_log_dt", dldt_g[:, 0]), ("s5_b_re", _s5_diag_b(dbre)),
                             ("s5_b_im", _s5_diag_b(dbim)), ("s5_c_re", _s5_diag_c(dcre)),
                             ("s5_c_im", _s5_diag_c(dcim)), ("s5_d", ddv.reshape(S5_G, S5_C))):
                gs.setdefault(nme, [None] * 2)[i] = val
            dq, dk, dv, dgate, dal, dbl, dalog, ddtb, dog = gdn_bwd(
                sv["qkvc"], sv["pm"], sv["abt"], sv["alog"], sv["dtb"], sv["og"], sv["stg"], sv["inv"],
                dcat[:, S5_W:])
            (dog_s,) = whole("gdn_og_sum", lambda t: (jnp.sum(t, axis=0, keepdims=True),),
                             [dog.reshape(GDN_H, GDN_D)], [((1, GDN_D), F32)])
            gs.setdefault("gdn_out_norm_g", [None] * 2)[i] = dog_s[0]
            gs.setdefault("gdn_a_log", [None] * 2)[i] = dalog.reshape(GDN_H)
            gs.setdefault("gdn_dt_bias", [None] * 2)[i] = ddtb.reshape(GDN_H)
            dqkvc = jnp.concatenate([dq, dk, dv], axis=1)
            dqkv, dconv = conv_bwd(sv["pm"], sv["conv_w"], dqkvc)
            gs.setdefault("gdn_conv_w", [None] * 2)[i] = dconv
            dpm = jnp.concatenate([du, dqkv, dgate], axis=1).astype(BF16)
            dabt = jnp.concatenate([dal, dbl], axis=0)
            dw_main = matmul("tn", dpm, sv["h"], "proj_dw", out_dtype=BF16)
            dw_ab = matmul("nn", dabt, sv["h"], "proj_ab_dw", out_dtype=BF16)
            gw["ab_w_in"] = jnp.concatenate([dw_main, dw_ab], axis=0)
            dx = gsink(layer, "B", gw, dx)
            w_in = w["ab_w_in"]
            dh = matmul("nn", dpm, w_in[:2560], "proj_dh")
            dx, dg = matmul("tn", dabt, w_in[2560:], "proj_ab_dh", add=dh,
                            norm_bwd=(sv["x0"], row2(p["ab_norm_g"][i]), dx))
            gs.setdefault("ab_norm_g", [None] * 2)[i] = dg[0]
        else:
            do = matmul("nt", dx, w["c_w_out"], "mix_dcat")
            gw["c_w_out"] = matmul("tn", sv["o"], dx, "mix_dwo", out_dtype=BF16)
            dqn, dkp, dvp, dbias = cattn_bwd(sv["qn"], sv["kp"], sv["vp"], sv["bias"], do)
            gs.setdefault("c_rel_bias", [None] * 2)[i] = rel_bias_grad(dbias.reshape(CA_H, CHUNK, CA_BAND))
            dq, dqg = row_bwd("headnorm_bwd", f_headnorm, [(sv["qkv"], D, 0)], [sv["qg"]], [dqn], [0], [0])
            dk, dkg = row_bwd("headnorm_bwd", f_headnorm, [(sv["qkv"], D, 1)], [sv["kg"]],
                              [(dkp, D, 0, CA_PAD)], [0], [0])
            head_sum = lambda t: (jnp.sum(t, axis=0, keepdims=True),)
            (dqg,) = whole("headgain_sum", head_sum, [dqg.reshape(CA_H, CA_D)], [((1, CA_D), F32)])
            (dkg,) = whole("headgain_sum", head_sum, [dkg.reshape(CA_H, CA_D)], [((1, CA_D), F32)])
            gs.setdefault("c_q_norm_g", [None] * 2)[i] = dqg[0]
            gs.setdefault("c_k_norm_g", [None] * 2)[i] = dkg[0]
            dqkv = jnp.concatenate([dq, dk, dvp[CA_PAD:]], axis=1).astype(BF16)
            gw["c_w_qkv"] = matmul("tn", dqkv, sv["h"], "proj_qkv_dw", out_dtype=BF16)
            dx = gsink(layer, "B", gw, dx)
            dx, dg = matmul("nn", dqkv, w["c_w_qkv"], "proj_qkv_dh",
                            norm_bwd=(sv["x0"], row2(w["c_norm_g"]), dx))
            gs.setdefault("c_norm_g", [None] * 2)[i] = dg[0]
    (dmg,) = row_bwd("mem_norm_bwd", f_norm, [mem], [row2(p["mem_norm_g"])], [dmem_n], [], [0])
    small = {n: jnp.stack(v) for n, v in gs.items()}
    small["mem_norm_g"] = dmg[0]
    return loss_vec, dx, small


ADAM_BLOCK_BYTES = 3 << 19


def adam(w, g, m, v, name):
    shape = w.shape
    if w.ndim == 3:
        d0, n, d2 = shape
        fits = [t for t in range(8, n + 1, 8) if n % t == 0 and d0 * t * d2 * 4 <= ADAM_BLOCK_BYTES]
        return tuple(row_fwd(name, f_adam, [w, g, m, v], [], [((d0, d2), F32)] * 3, tile=max(fits)))
    cols = shape[-1]
    w2, g2, m2, v2 = (t.reshape(-1, cols) for t in (w, g, m, v))
    rows = w2.shape[0]
    tile = rows if rows <= 512 else _tile(rows, 512, 8)
    outs = row_fwd(name, f_adam, [w2, g2, m2, v2], [], [(cols, F32)] * 3, tile=tile)
    return tuple(o.reshape(shape) for o in outs)


WEIGHTS = ['ab_norm_g', 'ab_w_in', 'ab_w_out', 's5_a_re', 's5_a_im', 's5_log_dt', 's5_b_re', 's5_b_im', 's5_c_re',
           's5_c_im', 's5_d', 's5_w_glu', 's5_b_glu', 'gdn_conv_w', 'gdn_a_log', 'gdn_dt_bias', 'gdn_out_norm_g',
           'c_norm_g', 'c_w_qkv', 'c_w_out', 'c_q_norm_g', 'c_k_norm_g', 'c_rel_bias', 'mem_norm_g', 'xa_norm_g',
           'xa_w_q', 'xa_w_kv', 'xa_w_out', 'xa_q_norm_g', 'xa_k_norm_g', 'f_norm_g', 'f_w_gate', 'f_w_up',
           'f_w_down']
SHARDED_SMALL = {"gdn_conv_w": (2, 384), "c_norm_g": (1, 256)}


def kernel(x, mem, ab_norm_g, ab_w_in, ab_w_out, s5_a_re, s5_a_im, s5_log_dt, s5_b_re, s5_b_im, s5_c_re, s5_c_im, s5_d, s5_w_glu, s5_b_glu, gdn_conv_w, gdn_a_log, gdn_dt_bias, gdn_out_norm_g, c_norm_g, c_w_qkv, c_w_out, c_q_norm_g, c_k_norm_g, c_rel_bias, mem_norm_g, xa_norm_g, xa_w_q, xa_w_kv, xa_w_out, xa_q_norm_g, xa_k_norm_g, f_norm_g, f_w_gate, f_w_up, f_w_down, loss_target, m_ab_norm_g, m_ab_w_in, m_ab_w_out, m_s5_a_re, m_s5_a_im, m_s5_log_dt, m_s5_b_re, m_s5_b_im, m_s5_c_re, m_s5_c_im, m_s5_d, m_s5_w_glu, m_s5_b_glu, m_gdn_conv_w, m_gdn_a_log, m_gdn_dt_bias, m_gdn_out_norm_g, m_c_norm_g, m_c_w_qkv, m_c_w_out, m_c_q_norm_g, m_c_k_norm_g, m_c_rel_bias, m_mem_norm_g, m_xa_norm_g, m_xa_w_q, m_xa_w_kv, m_xa_w_out, m_xa_q_norm_g, m_xa_k_norm_g, m_f_norm_g, m_f_w_gate, m_f_w_up, m_f_w_down, v_ab_norm_g, v_ab_w_in, v_ab_w_out, v_s5_a_re, v_s5_a_im, v_s5_log_dt, v_s5_b_re, v_s5_b_im, v_s5_c_re, v_s5_c_im, v_s5_d, v_s5_w_glu, v_s5_b_glu, v_gdn_conv_w, v_gdn_a_log, v_gdn_dt_bias, v_gdn_out_norm_g, v_c_norm_g, v_c_w_qkv, v_c_w_out, v_c_q_norm_g, v_c_k_norm_g, v_c_rel_bias, v_mem_norm_g, v_xa_norm_g, v_xa_w_q, v_xa_w_kv, v_xa_w_out, v_xa_q_norm_g, v_xa_k_norm_g, v_f_norm_g, v_f_w_gate, v_f_w_up, v_f_w_down):
    args = locals()
    p = {n: args[n] for n in WEIGHTS}
    m = {n: args["m_" + n] for n in WEIGHTS}
    v = {n: args["v_" + n] for n in WEIGHTS}
    chip = 2 * lax.axis_index("x") + lax.axis_index("y")

    carry = x[0]
    gathers = {}
    for layer in range(DEPTH):
        for grp in GROUPS:
            src = pack_group_shards(p, layer, grp)
            land = lax.dynamic_update_slice(lax.empty((N_CHIPS,) + src.shape, BF16), src[None], (chip, 0, 0))
            send_sems, recv_sems, src, land, carry = exchange_start(
                src, land, carry, f"gather_start_{layer}{grp}", per_dest=False)
            gathers[layer, grp] = (src, land, send_sems, recv_sems)

    def wsrc(layer, grp, after):
        src, land, send_sems, recv_sems = gathers[layer, grp]
        _, land = exchange_wait(src, land, send_sems, recv_sems, after, f"gather_wait_{layer}{grp}", per_dest=False)
        return unpack_group_gathered(land, layer, grp)

    scatters, siblings = [], []
    LAG = 2

    def finish(carry):
        layer, grp, src, land, send_sems, recv_sems = scatters[len(siblings)]
        src, land = exchange_wait(src, land, send_sems, recv_sems, carry, f"scatter_wait_{layer}{grp}", per_dest=True)
        own = lax.dynamic_index_in_dim(src, chip, axis=0, keepdims=False)
        part = sum_own_slabs(own, land, "sum_chips")
        send_sems, recv_sems, part, other, carry = exchange_start(
            part, lax.empty(part.shape, F32), carry, f"sibling_start_{layer}{grp}", per_dest="sibling")
        siblings.append((layer, grp, part, other, send_sems, recv_sems))
        return carry

    def gsink(layer, grp, gw, carry):
        src = pack_group_grads(gw, layer, grp)
        land = lax.empty((3,) + src.shape[1:], BF16)
        send_sems, recv_sems, src, land, carry = exchange_start(
            src, land, carry, f"scatter_start_{layer}{grp}", per_dest=True)
        scatters.append((layer, grp, src, land, send_sems, recv_sems))
        if len(scatters) > LAG:
            carry = finish(carry)
        return carry

    loss_vec, dx, g_small = local_step(carry, mem[0], loss_target[0], p, wsrc, gsink)

    full_shapes = {n: ((2, 4, 1536) if n == "gdn_conv_w" else (2, D) if n == "c_norm_g" else p[n].shape)
                   for n in SMALL}
    small_mine = pack_small(g_small, extra=loss_vec)
    (small_other,) = sibling_exchange([small_mine], "sibling_small")
    (small_chip,) = row_fwd("sum_small_cores", lambda a, b: (a + b,), [small_mine, small_other], [], [(128, F32)],
                            tile=128)
    small_sum = sum_slabs(chip_gather(small_chip, "gather_small"), "sum_small")
    g_s, rest = unpack_small(small_sum, full_shapes)
    (loss11,) = whole("loss_sum", lambda t: (jnp.sum(jnp.sum(t, axis=1, keepdims=True), axis=0, keepdims=True),),
                      [rest[:D // 128]], [((1, 1), F32)])
    for n, (axis, width) in SHARDED_SMALL.items():
        g_s[n] = lax.dynamic_slice_in_dim(g_s[n], chip * width, width, axis=axis)
    delta, new_m, new_v = {}, {}, {}
    for n in SMALL:
        shape = p[n].shape
        two = (1, shape[0]) if len(shape) == 1 else (int(np.prod(shape[:-1])), shape[-1])
        outs = whole("adam_" + n, f_adam, [t.reshape(two) for t in (p[n], g_s[n], m[n], v[n])], [(two, F32)] * 3)
        delta[n], new_m[n], new_v[n] = (o.reshape(shape) for o in outs)

    while len(siblings) < len(scatters):
        dx = finish(dx)

    per_layer = {}
    for layer, grp, part, other, send_sems, recv_sems in siblings:
        part, other = exchange_wait(part, other, send_sems, recv_sems, dx, f"sibling_wait_{layer}{grp}",
                                    per_dest="sibling")
        (total,) = row_fwd("sum_cores", lambda a, b: (a + b,), [part, other], [], [(D, F32)],
                           tile=_tile(part.shape[0], 512, 16))
        for name, g in unpack_group_reduced(total, layer, grp).items():
            per_layer.setdefault(name, {})[layer] = g
    grads = {name: jnp.stack([d[k] for k in sorted(d)]) for name, d in per_layer.items()}
    grads.update(g_s)
    for name, _, _, _ in BIG:
        if name in ADAM_TRANSPOSED:
            t = lambda a: jnp.swapaxes(a, 1, 2)
            outs = adam(t(p[name]), grads[name], t(m[name]), t(v[name]), "adam_" + name)
            delta[name], new_m[name], new_v[name] = (t(o) for o in outs)
            grads[name] = t(grads[name])
        else:
            delta[name], new_m[name], new_v[name] = adam(p[name], grads[name], m[name], v[name], "adam_" + name)

    return (loss11[0, 0], dx[None], *[grads[n] for n in WEIGHTS], *[delta[n] for n in WEIGHTS],
            *[new_m[n] for n in WEIGHTS], *[new_v[n] for n in WEIGHTS])
```

```python
import functools
import math

import numpy as np
import jax
import jax.numpy as jnp
from jax import lax
from jax.experimental import pallas as pl
from jax.experimental.pallas import tpu as pltpu

F32 = jnp.float32
BF16 = jnp.bfloat16
MESH = pl.DeviceIdType.MESH

D = 1024
CHUNK = 64
N_MEM = 256
EPS = 1e-6
S5_W = 512
S5_G = 32
S5_C = 16
S5_P = 64
S5_GB = 4
GDN_H = 4
GDN_D = 128
CA_H = 16
CA_D = 64
CA_LEFT = 8
CA_BAND = (CA_LEFT + 1) * CHUNK
CA_PAD = CA_LEFT * CHUNK
MAX_REL = 128
XA_H = 4
XA_D = 256
FFN = 2816
DEPTH = 4
N_CHIPS = 4
N_DEV = 8
LR, B1, B2, AEPS, WD, STEP = 0.001, 0.9, 0.999, 1e-08, 0.01, 10

VMEM_LIMIT = 56 * 1024 * 1024
ROW_TILE = 256
HI = lax.Precision.HIGHEST


def _cp(*sem):
    return pltpu.CompilerParams(dimension_semantics=sem, vmem_limit_bytes=VMEM_LIMIT)


def _dg(a, b, ca, cb):
    return lax.dot_general(a.astype(BF16), b.astype(BF16), (((ca,), (cb,)), ((), ())),
                           preferred_element_type=F32)


@jax.custom_vjp
def mm(a, b):
    return _dg(a, b, 1, 0)


@jax.custom_vjp
def mm_nt(a, b):
    return _dg(a, b, 1, 1)


@jax.custom_vjp
def mm_tn(a, b):
    return _dg(a, b, 0, 0)


mm.defvjp(lambda a, b: (mm(a, b), (a, b)), lambda r, g: (mm_nt(g, r[1]), mm_tn(r[0], g)))
mm_nt.defvjp(lambda a, b: (mm_nt(a, b), (a, b)), lambda r, g: (mm(g, r[1]), mm_tn(g, r[0])))
mm_tn.defvjp(lambda a, b: (mm_tn(a, b), (a, b)), lambda r, g: (mm_nt(r[1], g), mm(r[0], g)))


def _bdg(a, b, ca, cb):
    return lax.dot_general(a.astype(BF16), b.astype(BF16), (((ca,), (cb,)), ((0,), (0,))),
                           preferred_element_type=F32)


@jax.custom_vjp
def bmm(a, b):
    return _bdg(a, b, 2, 1)


@jax.custom_vjp
def bmm_nt(a, b):
    return _bdg(a, b, 2, 2)


@jax.custom_vjp
def bmm_tn(a, b):
    return _bdg(a, b, 1, 1)


bmm.defvjp(lambda a, b: (bmm(a, b), (a, b)), lambda r, g: (bmm_nt(g, r[1]), bmm_tn(r[0], g)))
bmm_nt.defvjp(lambda a, b: (bmm_nt(a, b), (a, b)), lambda r, g: (bmm(g, r[1]), bmm_tn(g, r[0])))
bmm_tn.defvjp(lambda a, b: (bmm_tn(a, b), (a, b)), lambda r, g: (bmm_nt(r[1], g), bmm(r[0], g)))


def _split(a):
    hi = a.astype(BF16)
    return hi, (a - hi.astype(F32)).astype(BF16)


def _dg3(a, b, ca, cb):
    (ah, al), (bh, bl) = _split(a), _split(b)
    d = lambda u, v: lax.dot_general(u, v, (((ca,), (cb,)), ((), ())), preferred_element_type=F32)
    return d(ah, bh) + (d(ah, bl) + d(al, bh))


@jax.custom_vjp
def mm3(a, b):
    return _dg3(a, b, 1, 0)


@jax.custom_vjp
def mm3_nt(a, b):
    return _dg3(a, b, 1, 1)


@jax.custom_vjp
def mm3_tn(a, b):
    return _dg3(a, b, 0, 0)


mm3.defvjp(lambda a, b: (mm3(a, b), (a, b)), lambda r, g: (mm3_nt(g, r[1]), mm3_tn(r[0], g)))
mm3_nt.defvjp(lambda a, b: (mm3_nt(a, b), (a, b)), lambda r, g: (mm3(g, r[1]), mm3_tn(g, r[0])))
mm3_tn.defvjp(lambda a, b: (mm3_tn(a, b), (a, b)), lambda r, g: (mm3_nt(r[1], g), mm3(r[0], g)))


def _tri_mm(v, upper):
    T = v.shape[0]
    r = lax.broadcasted_iota(jnp.int32, (T, T), 0)
    c = lax.broadcasted_iota(jnp.int32, (T, T), 1)
    m = ((c >= r) if upper else (r >= c)).astype(BF16)
    hi, lo = _split(v)
    d = lambda u: lax.dot_general(m, u, (((1,), (0,)), ((), ())), preferred_element_type=F32)
    return d(hi) + d(lo)


@jax.custom_vjp
def cumsum_rows(v):
    return _tri_mm(v, False)


cumsum_rows.defvjp(lambda v: (_tri_mm(v, False), None), lambda _, g: (_tri_mm(g, True),))


def _rms(x, g):
    return x * lax.rsqrt(jnp.mean(x * x, axis=-1, keepdims=True) + EPS) * g


def _softmax(s):
    e = jnp.exp(s - lax.stop_gradient(jnp.max(s, axis=-1, keepdims=True)))
    return e / jnp.sum(e, axis=-1, keepdims=True)


def _softplus(x):
    return jnp.maximum(x, 0.0) + jnp.log(1.0 + jnp.exp(-jnp.abs(x)))


def _tile(n, cap, align):
    if n <= cap:
        return n
    best = None
    for d in range(align, cap + 1, align):
        if n % d == 0:
            best = d
    assert best is not None, (n, cap, align)
    return best


def matmul(mode, a, b, name, out_dtype=F32, add=None, norm_out=None, norm_bwd=None):
    if mode == "nn":
        (M, K), (K2, N) = a.shape, b.shape
    elif mode == "nt":
        (M, K), (N, K2) = a.shape, b.shape
    else:
        (K, M), (K2, N) = a.shape, b.shape
    assert K == K2, (mode, a.shape, b.shape)
    if mode != "tn" and norm_out is None and norm_bwd is None:
        tm, tn, tk = _tile(M, 1024, 128), _tile(N, 512, 128), _tile(K, 2048, 128)
    else:
        tm, tn, tk = _tile(M, 512, 128), _tile(N, 1536, 128), _tile(K, 2048, 128)
    nk = K // tk
    if mode == "nn":
        a_spec = pl.BlockSpec((tm, tk), lambda i, j, k: (i, k))
        b_spec = pl.BlockSpec((tk, tn), lambda i, j, k: (k, j))
        dn = (((1,), (0,)), ((), ()))
    elif mode == "nt":
        a_spec = pl.BlockSpec((tm, tk), lambda i, j, k: (i, k))
        b_spec = pl.BlockSpec((tn, tk), lambda i, j, k: (j, k))
        dn = (((1,), (1,)), ((), ()))
    else:
        a_spec = pl.BlockSpec((tk, tm), lambda i, j, k: (k, i))
        b_spec = pl.BlockSpec((tk, tn), lambda i, j, k: (k, j))
        dn = (((0,), (0,)), ((), ()))
    o_spec = pl.BlockSpec((tm, tn), lambda i, j, k: (i, j))
    has_add = add is not None
    g_spec = pl.BlockSpec((1, tn), lambda i, j, k: (0, j))
    extra, extra_specs, out_shapes, out_specs = [], [], [jax.ShapeDtypeStruct((M, N), out_dtype)], [o_spec]
    sem = ("parallel", "parallel", "arbitrary")
    if norm_out is not None:
        assert tn == N
        extra, extra_specs = [norm_out], [g_spec]
        out_shapes.append(jax.ShapeDtypeStruct((M, N), BF16))
        out_specs.append(o_spec)
    if norm_bwd is not None:
        assert tn == N
        x_in, g_in, res_in = norm_bwd
        extra, extra_specs = [x_in, g_in, res_in], [o_spec, g_spec, o_spec]
        out_shapes.append(jax.ShapeDtypeStruct((1, N), F32))
        out_specs.append(g_spec)
        sem = ("arbitrary", "arbitrary", "arbitrary")
    n_in = 2 + int(has_add) + len(extra)
    n_out = len(out_shapes)

    def body(*refs):
        a_ref, b_ref = refs[0], refs[1]
        add_ref = refs[2] if has_add else None
        extra_refs = refs[2 + int(has_add):n_in]
        o_ref = refs[n_in]
        acc_ref = refs[-1]
        i = pl.program_id(0)
        p = lax.dot_general(a_ref[...].astype(BF16), b_ref[...].astype(BF16), dn,
                            preferred_element_type=F32)

        def finish(total):
            if has_add:
                total = total + add_ref[...]
            if norm_out is not None:
                o_ref[...] = total.astype(o_ref.dtype)
                refs[n_in + 1][...] = _rms(total, extra_refs[0][...]).astype(BF16)
            elif norm_bwd is not None:
                x_ref, g_ref, res_ref = extra_refs
                _, vjp = jax.vjp(f_norm_res, x_ref[...], g_ref[...])
                dx, dg = vjp((total, res_ref[...]))
                o_ref[...] = dx
                dg_ref = refs[n_in + 1]

                @pl.when(i == 0)
                def _():
                    dg_ref[...] = dg

                @pl.when(i > 0)
                def _():
                    dg_ref[...] += dg
            else:
                o_ref[...] = total.astype(o_ref.dtype)

        if nk == 1:
            finish(p)
        else:
            k = pl.program_id(2)

            @pl.when(k == 0)
            def _():
                acc_ref[...] = p

            @pl.when(k > 0)
            def _():
                acc_ref[...] += p

            @pl.when(k == nk - 1)
            def _():
                finish(acc_ref[...])

    ins = [a, b] + ([add] if has_add else []) + extra
    specs = [a_spec, b_spec] + ([o_spec] if has_add else []) + extra_specs
    out = pl.pallas_call(
        body, name=name, grid=(M // tm, N // tn, nk), in_specs=specs, out_specs=out_specs,
        out_shape=out_shapes, scratch_shapes=[pltpu.VMEM((tm, tn), F32)],
        compiler_params=_cp(*sem),
    )(*ins)
    return out[0] if n_out == 1 else out


def ffn_in(h, wg, wu):
    (S, K), F = h.shape, wg.shape[0]
    tm, tn = _tile(S, 1024, 128), _tile(F, 512, 128)
    dn = (((1,), (1,)), ((), ()))

    def body(h_ref, wg_ref, wu_ref, g_ref, u_ref, act_ref):
        a = h_ref[...].astype(BF16)
        g = lax.dot_general(a, wg_ref[...].astype(BF16), dn, preferred_element_type=F32)
        u = lax.dot_general(a, wu_ref[...].astype(BF16), dn, preferred_element_type=F32)
        g_ref[...] = g.astype(BF16)
        u_ref[...] = u.astype(BF16)
        act_ref[...] = f_swiglu(g, u)[0].astype(BF16)

    w_spec = pl.BlockSpec((tn, K), lambda i, j: (j, 0))
    o_spec = pl.BlockSpec((tm, tn), lambda i, j: (i, j))
    sd = jax.ShapeDtypeStruct
    return pl.pallas_call(
        body, name="ffn_in", grid=(S // tm, F // tn),
        in_specs=[pl.BlockSpec((tm, K), lambda i, j: (i, 0)), w_spec, w_spec], out_specs=[o_spec] * 3,
        out_shape=[sd((S, F), BF16), sd((S, F), BF16), sd((S, F), BF16)],
        compiler_params=_cp("parallel", "parallel"),
    )(h, wg, wu)


def ffn_dact(dy, wd, gate, up):
    (S, K), F = dy.shape, wd.shape[0]
    tm, tn = _tile(S, 1024, 128), _tile(F, 512, 128)
    dn = (((1,), (1,)), ((), ()))

    def body(dy_ref, wd_ref, g_ref, u_ref, dg_ref, du_ref):
        dact = lax.dot_general(dy_ref[...].astype(BF16), wd_ref[...].astype(BF16), dn, preferred_element_type=F32)
        _, vjp = jax.vjp(f_swiglu, g_ref[...].astype(F32), u_ref[...].astype(F32))
        dg, du = vjp((dact,))
        dg_ref[...] = dg.astype(BF16)
        du_ref[...] = du.astype(BF16)

    o_spec = pl.BlockSpec((tm, tn), lambda i, j: (i, j))
    sd = jax.ShapeDtypeStruct
    return pl.pallas_call(
        body, name="ffn_dact", grid=(S // tm, F // tn),
        in_specs=[pl.BlockSpec((tm, K), lambda i, j: (i, 0)), pl.BlockSpec((tn, K), lambda i, j: (j, 0)),
                  o_spec, o_spec],
        out_specs=[o_spec] * 2, out_shape=[sd((S, F), BF16)] * 2,
        compiler_params=_cp("parallel", "parallel"),
    )(dy, wd, gate, up)


def _row_spec(arr, tile):
    if isinstance(arr, tuple):
        a, w, cb = arr[:3]
        ro = (arr[3] // tile) if len(arr) > 3 else 0
        assert len(arr) < 4 or arr[3] % tile == 0
        return a, pl.BlockSpec((tile, w), lambda i, cb=cb, ro=ro: (i + ro, cb)), (tile, w)
    if arr.ndim == 3:
        d0, _, d2 = arr.shape
        return arr, pl.BlockSpec((d0, tile, d2), lambda i: (0, i, 0)), (d0, tile, d2)
    return arr, pl.BlockSpec((tile, arr.shape[1]), lambda i: (i, 0)), (tile, arr.shape[1])


def _full_spec(arr):
    nd = arr.ndim
    return pl.BlockSpec(arr.shape, lambda i, nd=nd: (0,) * nd)


def _n_rows(arr):
    a = arr[0] if isinstance(arr, tuple) else arr
    return a.shape[1] if a.ndim == 3 else a.shape[0]


def _row_out_shape(shape_tail, n, dtype):
    if isinstance(shape_tail, tuple):
        d0, d2 = shape_tail
        return (jax.ShapeDtypeStruct((d0, n, d2), dtype),
                lambda tile: pl.BlockSpec((d0, tile, d2), lambda i: (0, i, 0)))
    return (jax.ShapeDtypeStruct((n, shape_tail), dtype),
            lambda tile: pl.BlockSpec((tile, shape_tail), lambda i: (i, 0)))


def _f32(v):
    return v.astype(F32) if v.dtype == BF16 else v


def row_fwd(name, fn, rows, fulls, outs, tile=ROW_TILE):
    n = _n_rows(rows[0])
    tile = min(tile, n)
    assert n % tile == 0, (name, n, tile)
    rs = [_row_spec(r, tile) for r in rows]
    os_ = [_row_out_shape(w, n, dt) for w, dt in outs]
    nr, nf = len(rows), len(fulls)

    def body(*refs):
        vals = [_f32(r[...]) for r in refs[:nr + nf]]
        res = fn(*vals)
        for r, v in zip(refs[nr + nf:], res):
            r[...] = v.astype(r.dtype)

    out = pl.pallas_call(
        body, name=name, grid=(n // tile,),
        in_specs=[s for _, s, _ in rs] + [_full_spec(f) for f in fulls],
        out_specs=[mk(tile) for _, mk in os_], out_shape=[sh for sh, _ in os_],
        compiler_params=_cp("parallel"),
    )(*[a for a, _, _ in rs], *fulls)
    return out


def row_bwd(name, fn, rows, fulls, cts, want_rows, want_fulls, row_dtypes=None, tile=ROW_TILE):
    n = _n_rows(rows[0])
    tile = min(tile, n)
    assert n % tile == 0, (name, n, tile)
    rs = [_row_spec(r, tile) for r in rows]
    cs = [_row_spec(c, tile) for c in cts]
    nr, nf, nc = len(rows), len(fulls), len(cts)
    row_dtypes = row_dtypes or [F32] * len(want_rows)
    out_shapes, out_specs = [], []
    for k, idx in enumerate(want_rows):
        a, _, blk = rs[idx]
        if len(blk) == 3:
            sh, mk = _row_out_shape((blk[0], blk[2]), n, row_dtypes[k])
        else:
            sh, mk = _row_out_shape(blk[1], n, row_dtypes[k])
        out_shapes.append(sh)
        out_specs.append(mk(tile))
    for idx in want_fulls:
        out_shapes.append(jax.ShapeDtypeStruct(fulls[idx].shape, F32))
        out_specs.append(_full_spec(fulls[idx]))
    n_wr = len(want_rows)

    def body(*refs):
        i = pl.program_id(0)
        vals = [_f32(r[...]) for r in refs[:nr + nf]]
        ct_vals = [_f32(r[...]) for r in refs[nr + nf:nr + nf + nc]]
        outs = refs[nr + nf + nc:]
        _, vjp = jax.vjp(fn, *vals)
        grads = vjp(tuple(ct_vals))
        for k, idx in enumerate(want_rows):
            outs[k][...] = grads[idx].astype(outs[k].dtype)
        for k, idx in enumerate(want_fulls):
            o = outs[n_wr + k]
            g = grads[nr + idx]

            @pl.when(i == 0)
            def _(o=o, g=g):
                o[...] = g

            @pl.when(i > 0)
            def _(o=o, g=g):
                o[...] += g

    out = pl.pallas_call(
        body, name=name, grid=(n // tile,),
        in_specs=[s for _, s, _ in rs] + [_full_spec(f) for f in fulls] + [s for _, s, _ in cs],
        out_specs=out_specs, out_shape=out_shapes,
        compiler_params=_cp("arbitrary"),
    )(*[a for a, _, _ in rs], *fulls, *[a for a, _, _ in cs])
    return out


def whole(name, fn, args, outs):
    def body(*refs):
        res = fn(*[r[...] for r in refs[:len(args)]])
        for r, v in zip(refs[len(args):], res):
            r[...] = v.astype(r.dtype)

    return pl.pallas_call(
        body, name=name, out_shape=[jax.ShapeDtypeStruct(s, d) for s, d in outs],
        compiler_params=pltpu.CompilerParams(vmem_limit_bytes=VMEM_LIMIT),
    )(*args)


def f_norm(x, g):
    return (_rms(x, g),)


def f_norm_res(x, g):
    return _rms(x, g), x


def f_swiglu(g, u):
    return (g * jax.nn.sigmoid(g) * u,)


def f_glu(y, w, b):
    h = jax.nn.gelu(y)
    return (h * jax.nn.sigmoid(mm(h, w) + b),)


def f_xattn(q, kv, qg, kg):
    outs = []
    for h in range(XA_H):
        sl = slice(h * XA_D, (h + 1) * XA_D)
        qn = _rms(q[:, sl], qg)
        kn = _rms(kv[:, sl], kg)
        vh = kv[:, D + h * XA_D:D + (h + 1) * XA_D]
        p = _softmax(mm_nt(qn, kn) * (XA_D ** -0.5))
        outs.append(mm(p, vh))
    return (jnp.concatenate(outs, axis=-1),)


def f_adam(w, g, m, v):
    m2 = B1 * m + (1.0 - B1) * g
    v2 = B2 * v + (1.0 - B2) * (g * g)
    m_hat = m2 / (1.0 - B1 ** STEP)
    v_hat = v2 / (1.0 - B2 ** STEP)
    delta = -LR * (m_hat / (jnp.sqrt(v_hat) + AEPS) + WD * w)
    return delta, m2, v2


S5_NTAB, S5_NROW = 4, 6


def _s5_tables(are, aim, ldt):
    T = CHUNK
    dt = jnp.exp(ldt)
    ar, ai = are * dt, aim * dt
    t = lax.broadcasted_iota(jnp.int32, (T, 1), 0).astype(F32)
    mag, inv = jnp.exp(t * ar), jnp.exp(-t * ar)
    cs, sn = jnp.cos(t * ai), jnp.sin(t * ai)
    e_re, e_im = mag * cs, mag * sn
    n_re, n_im = inv * cs, -inv * sn
    l_re, l_im = jnp.exp(ar) * jnp.cos(ai), jnp.exp(ar) * jnp.sin(ai)
    den = are * are + aim * aim
    k_re = ((l_re - 1.0) * are + l_im * aim) / den
    k_im = (l_im * are - (l_re - 1.0) * aim) / den
    tl = float(T - 1)
    m_re, m_im = jnp.exp(tl * ar) * jnp.cos(tl * ai), jnp.exp(tl * ar) * jnp.sin(tl * ai)
    return (e_re, e_im, n_re, n_im), (l_re, l_im, k_re, k_im, m_re, m_im)


def _s5_chunk(u, sre, sim, tabs, rows, b_re, b_im, c_re, c_im, dv):
    e_re, e_im, n_re, n_im = tabs
    l_re, l_im, k_re, k_im, m_re, m_im = rows
    x_re, x_im = mm(u, b_re), mm(u, b_im)
    bu_re = k_re * x_re - k_im * x_im
    bu_im = k_re * x_im + k_im * x_re
    v_re = bu_re * n_re - bu_im * n_im
    v_im = bu_re * n_im + bu_im * n_re
    p_re = l_re * sre - l_im * sim
    p_im = l_re * sim + l_im * sre
    w_re = cumsum_rows(v_re) + p_re
    w_im = cumsum_rows(v_im) + p_im
    s_re = e_re * w_re - e_im * w_im
    s_im = e_re * w_im + e_im * w_re
    y = mm(s_re, c_re) - mm(s_im, c_im) + dv * u
    z_re = jnp.sum(v_re, axis=0, keepdims=True) + p_re
    z_im = jnp.sum(v_im, axis=0, keepdims=True) + p_im
    return y, m_re * z_re - m_im * z_im, m_re * z_im + m_im * z_re


def _s5_fill_tables(are_ref, aim_ref, ldt_ref, tab, row):
    for g in range(S5_GB):
        ls = slice(512 * g, 512 * (g + 1))
        tabs, rows = _s5_tables(are_ref[:, ls], aim_ref[:, ls], ldt_ref[:, ls])
        for k, t in enumerate(tabs):
            tab[k, :, ls] = t
        for k, r in enumerate(rows):
            row[k:k + 1, ls] = r


def _s5_read_tables(tab, row, ls):
    return (tuple(tab[k, :, ls] for k in range(S5_NTAB)), tuple(row[k:k + 1, ls] for k in range(S5_NROW)))


def _s5_specs(nc, rev):
    T = CHUNK

    def ci(c):
        return nc - 1 - c if rev else c

    u_spec = pl.BlockSpec((T, S5_W), lambda c: (ci(c), 0))
    p_spec = pl.BlockSpec((1, S5_G * S5_P), lambda c: (0, 0))
    b_spec = pl.BlockSpec((S5_GB, 128, 512), lambda c: (0, 0, 0))
    c_spec = pl.BlockSpec((S5_GB, 512, 128), lambda c: (0, 0, 0))
    d_spec = pl.BlockSpec((1, S5_W), lambda c: (0, 0))
    st_spec = pl.BlockSpec((None, 2, S5_G * S5_P), lambda c: (ci(c), 0, 0))
    return u_spec, p_spec, b_spec, c_spec, d_spec, st_spec


def s5_fwd(pm, are, aim, ldt, b_re, b_im, c_re, c_im, dv):
    S = pm.shape[0]
    nc = S // CHUNK
    u_spec, p_spec, b_spec, c_spec, d_spec, st_spec = _s5_specs(nc, False)

    def body(u_ref, are_ref, aim_ref, ldt_ref, bre_ref, bim_ref, cre_ref, cim_ref, dv_ref,
             y_ref, st_ref, state, tab, row):
        c = pl.program_id(0)

        @pl.when(c == 0)
        def _():
            state[...] = jnp.zeros_like(state)
            _s5_fill_tables(are_ref, aim_ref, ldt_ref, tab, row)

        st_ref[...] = state[...]
        for g in range(S5_GB):
            lu, ls = slice(128 * g, 128 * (g + 1)), slice(512 * g, 512 * (g + 1))
            tabs, rows = _s5_read_tables(tab, row, ls)
            y, e_re, e_im = _s5_chunk(u_ref[:, lu], state[0:1, ls], state[1:2, ls], tabs, rows,
                                      bre_ref[g], bim_ref[g], cre_ref[g], cim_ref[g], dv_ref[:, lu])
            y_ref[:, lu] = y
            state[0:1, ls] = e_re
            state[1:2, ls] = e_im

    n_state = S5_G * S5_P
    return pl.pallas_call(
        body, name="s5_fwd", grid=(nc,),
        in_specs=[u_spec, p_spec, p_spec, p_spec, b_spec, b_spec, c_spec, c_spec, d_spec],
        out_specs=[u_spec, st_spec],
        out_shape=[jax.ShapeDtypeStruct((S, S5_W), F32), jax.ShapeDtypeStruct((nc, 2, n_state), F32)],
        scratch_shapes=[pltpu.VMEM((2, n_state), F32), pltpu.VMEM((S5_NTAB, CHUNK, n_state), F32),
                        pltpu.VMEM((8, n_state), F32)],
        compiler_params=_cp("arbitrary"),
    )(pm, are, aim, ldt, b_re, b_im, c_re, c_im, dv)


def s5_bwd(pm, st, dy, are, aim, ldt, b_re, b_im, c_re, c_im, dv):
    S = pm.shape[0]
    nc = S // CHUNK
    u_spec, p_spec, b_spec, c_spec, d_spec, st_spec = _s5_specs(nc, True)

    def body(u_ref, st_ref, dy_ref, are_ref, aim_ref, ldt_ref, bre_ref, bim_ref, cre_ref, cim_ref, dv_ref,
             du_ref, dare_ref, daim_ref, dldt_ref, dbre_ref, dbim_ref, dcre_ref, dcim_ref, ddv_ref,
             dstate, tab, row, dtab, drow):
        c = pl.program_id(0)

        @pl.when(c == 0)
        def _():
            dstate[...] = jnp.zeros_like(dstate)
            dtab[...] = jnp.zeros_like(dtab)
            drow[...] = jnp.zeros_like(drow)
            _s5_fill_tables(are_ref, aim_ref, ldt_ref, tab, row)

        for g in range(S5_GB):
            lu, ls = slice(128 * g, 128 * (g + 1)), slice(512 * g, 512 * (g + 1))
            every = slice(None)
            tabs, rows = _s5_read_tables(tab, row, ls)
            args = (u_ref[:, lu], st_ref[0:1, ls], st_ref[1:2, ls], tabs, rows,
                    bre_ref[g], bim_ref[g], cre_ref[g], cim_ref[g], dv_ref[:, lu])
            _, vjp = jax.vjp(_s5_chunk, *args)
            gr = vjp((dy_ref[:, lu], dstate[0:1, ls], dstate[1:2, ls]))
            du_ref[:, lu] = gr[0]
            dstate[0:1, ls] = gr[1]
            dstate[1:2, ls] = gr[2]
            for k, t in enumerate(gr[3]):
                dtab[k, :, ls] += t
            for k, r in enumerate(gr[4]):
                drow[k:k + 1, ls] += r
            accs = ((dbre_ref, (g,)), (dbim_ref, (g,)), (dcre_ref, (g,)), (dcim_ref, (g,)), (ddv_ref, (every, lu)))
            for (o, idx), gv in zip(accs, gr[5:]):
                @pl.when(c == 0)
                def _(o=o, idx=idx, gv=gv):
                    o[idx] = gv

                @pl.when(c > 0)
                def _(o=o, idx=idx, gv=gv):
                    o[idx] += gv

        @pl.when(c == nc - 1)
        def _():
            for g in range(S5_GB):
                ls = slice(512 * g, 512 * (g + 1))
                _, vjp = jax.vjp(_s5_tables, are_ref[:, ls], aim_ref[:, ls], ldt_ref[:, ls])
                dtabs, drows = _s5_read_tables(dtab, drow, ls)
                ga, gi, gl = vjp((dtabs, drows))
                dare_ref[:, ls] = ga
                daim_ref[:, ls] = gi
                dldt_ref[:, ls] = gl

    n_state = S5_G * S5_P
    return pl.pallas_call(
        body, name="s5_bwd", grid=(nc,),
        in_specs=[u_spec, st_spec, u_spec, p_spec, p_spec, p_spec, b_spec, b_spec, c_spec, c_spec, d_spec],
        out_specs=[u_spec, p_spec, p_spec, p_spec, b_spec, b_spec, c_spec, c_spec, d_spec],
        out_shape=[jax.ShapeDtypeStruct((S, S5_W), F32)] + [jax.ShapeDtypeStruct((1, n_state), F32)] * 3
        + [jax.ShapeDtypeStruct((S5_GB, 128, 512), F32)] * 2 + [jax.ShapeDtypeStruct((S5_GB, 512, 128), F32)] * 2
        + [jax.ShapeDtypeStruct((1, S5_W), F32)],
        scratch_shapes=[pltpu.VMEM((2, n_state), F32), pltpu.VMEM((S5_NTAB, CHUNK, n_state), F32),
                        pltpu.VMEM((8, n_state), F32), pltpu.VMEM((S5_NTAB, CHUNK, n_state), F32),
                        pltpu.VMEM((8, n_state), F32)],
        compiler_params=_cp("arbitrary"),
    )(pm, st, dy, are, aim, ldt, b_re, b_im, c_re, c_im, dv)


def conv_fwd(pm, w):
    S = pm.shape[0]

    def body(x_ref, w_ref, o_ref, pad):
        x = x_ref[...]
        pad[0:8, :] = jnp.zeros((8, 128), F32)
        pad[8:, :] = x
        y = (w_ref[3:4, :] * x + w_ref[2:3, :] * pad[7:7 + S, :] + w_ref[1:2, :] * pad[6:6 + S, :]
             + w_ref[0:1, :] * pad[5:5 + S, :])
        o_ref[...] = y * jax.nn.sigmoid(y)

    return pl.pallas_call(
        body, name="conv_fwd", grid=(12,),
        in_specs=[pl.BlockSpec((S, 128), lambda j: (0, 4 + j)), pl.BlockSpec((4, 128), lambda j: (0, j))],
        out_specs=pl.BlockSpec((S, 128), lambda j: (0, j)),
        out_shape=jax.ShapeDtypeStruct((S, 1536), F32),
        scratch_shapes=[pltpu.VMEM((S + 8, 128), F32)],
        compiler_params=_cp("parallel"),
    )(pm, w)


def conv_bwd(pm, w, dout):
    S = pm.shape[0]

    def body(x_ref, w_ref, do_ref, dx_ref, dw_ref, pad, dpad):
        x = x_ref[...]
        pad[0:8, :] = jnp.zeros((8, 128), F32)
        pad[8:, :] = x
        xs = [pad[5:5 + S, :], pad[6:6 + S, :], pad[7:7 + S, :], x]
        y = w_ref[0:1, :] * xs[0] + w_ref[1:2, :] * xs[1] + w_ref[2:3, :] * xs[2] + w_ref[3:4, :] * xs[3]
        sg = jax.nn.sigmoid(y)
        dy = do_ref[...] * (sg + y * sg * (1.0 - sg))
        dpad[0:S, :] = dy
        dpad[S:, :] = jnp.zeros((8, 128), F32)
        dx_ref[...] = (w_ref[3:4, :] * dy + w_ref[2:3, :] * dpad[1:1 + S, :] + w_ref[1:2, :] * dpad[2:2 + S, :]
                       + w_ref[0:1, :] * dpad[3:3 + S, :])
        for i in range(4):
            dw_ref[i:i + 1, :] = jnp.sum(dy * xs[i], axis=0, keepdims=True)

    return pl.pallas_call(
        body, name="conv_bwd", grid=(12,),
        in_specs=[pl.BlockSpec((S, 128), lambda j: (0, 4 + j)), pl.BlockSpec((4, 128), lambda j: (0, j)),
                  pl.BlockSpec((S, 128), lambda j: (0, j))],
        out_specs=[pl.BlockSpec((S, 128), lambda j: (0, j)), pl.BlockSpec((4, 128), lambda j: (0, j))],
        out_shape=[jax.ShapeDtypeStruct((S, 1536), F32), jax.ShapeDtypeStruct((4, 1536), F32)],
        scratch_shapes=[pltpu.VMEM((S + 8, 128), F32), pltpu.VMEM((S + 8, 128), F32)],
        compiler_params=_cp("parallel"),
    )(pm, w, dout)


GDN_SUP = 4
GDN_ROWS = GDN_SUP * CHUNK


@jax.custom_vjp
def _saved_inverse(a, x):
    return x


_saved_inverse.defvjp(lambda a, x: (x, x),
                      lambda x, g: (-mm_tn(x, mm_nt(g, x)), jnp.zeros_like(x)))


def _gdn_chunk(q, k, v, gate, al, bl, alog, dtb, og, state, inv=None, want_inv=False):
    R = q.shape[0]
    r = lax.broadcasted_iota(jnp.int32, (R, R), 0)
    c = lax.broadcasted_iota(jnp.int32, (R, R), 1)
    same = (r // CHUNK) == (c // CHUNK)
    eye = (r == c).astype(F32)
    strict, causal, upper = same & (r > c), same & (r >= c), same & (r <= c)
    qn = q * lax.rsqrt(jnp.sum(q * q, axis=-1, keepdims=True) + EPS) * (GDN_D ** -0.5)
    kn = k * lax.rsqrt(jnp.sum(k * k, axis=-1, keepdims=True) + EPS)
    beta = jnp.sum(eye * jax.nn.sigmoid(bl), axis=1, keepdims=True)
    g_row = -jnp.exp(alog) * _softplus(al + dtb)
    g = jnp.sum(eye * g_row, axis=1, keepdims=True)
    gc_col = jnp.sum(jnp.where(causal, g_row, 0.0), axis=1, keepdims=True)
    gc_row = jnp.sum(jnp.where(upper, g, 0.0), axis=0, keepdims=True)
    gtot = jnp.sum(jnp.where(same, g_row, 0.0), axis=1, keepdims=True)
    gamma = jnp.exp(gc_col)
    diff = gc_col - gc_row
    d_strict = jnp.where(strict, jnp.exp(jnp.where(strict, diff, 0.0)), 0.0)
    d_causal = jnp.where(causal, jnp.exp(jnp.where(causal, diff, 0.0)), 0.0)
    a = beta * mm_nt(kn, kn) * d_strict
    if inv is None:
        p = -a
        x = eye + p
        for _ in range(5):
            p = mm(p, p)
            x = x + mm(x, p)
    else:
        x = _saved_inverse(a, inv)
    u_new = mm(x, beta * v)
    w_k = mm(x, (beta * gamma) * kn)
    qk = mm_nt(qn, kn) * d_causal
    q_g = qn * gamma
    k_tail = kn * jnp.exp(gtot - gc_col)
    ws, os_ = [], []
    for i in range(R // CHUNK):
        rows = slice(CHUNK * i, CHUNK * (i + 1))
        w_i = u_new[rows] - mm(w_k[rows], state)
        os_.append(mm(q_g[rows], state))
        decay = jnp.exp(jnp.sum(g[rows], axis=0, keepdims=True))
        state = decay * state + mm_tn(k_tail[rows], w_i)
        ws.append(w_i)
    o = jnp.concatenate(os_, axis=0) + mm(qk, jnp.concatenate(ws, axis=0))
    out = _rms(o, og) * (gate * jax.nn.sigmoid(gate))
    return (out, state, x) if want_inv else (out, state)


def _gdn_specs(nc, rev):
    def ci(c):
        return nc - 1 - c if rev else c

    def blk(cb):
        return pl.BlockSpec((GDN_ROWS, 512), lambda c: (ci(c), cb))

    col = lambda n: pl.BlockSpec((n, GDN_ROWS), lambda c: (0, ci(c)))
    sc = pl.BlockSpec((GDN_H, 1, 1), lambda c: (0, 0, 0))
    og = pl.BlockSpec((1, 128), lambda c: (0, 0))
    st = pl.BlockSpec((GDN_H, None, 128, 128), lambda c: (0, ci(c), 0, 0))
    return blk, col, sc, og, st


def gdn_fwd(qkvc, pm, abt, alog, dtb, og):
    S = qkvc.shape[0]
    nc = S // GDN_ROWS
    blk, col, sc, ogs, st = _gdn_specs(nc, False)

    def body(q_ref, k_ref, v_ref, gate_ref, ab_ref, alog_ref, dtb_ref, og_ref, o_ref, st_ref, inv_ref, state):
        c = pl.program_id(0)

        @pl.when(c == 0)
        def _():
            state[...] = jnp.zeros_like(state)

        st_ref[...] = state[...]
        for h in range(GDN_H):
            sl = slice(GDN_D * h, GDN_D * (h + 1))
            out, new_state, inv = _gdn_chunk(
                q_ref[:, sl], k_ref[:, sl], v_ref[:, sl], gate_ref[:, sl], ab_ref[h:h + 1, :], ab_ref[GDN_H + h:GDN_H + h + 1, :],
                alog_ref[h], dtb_ref[h], og_ref[...], state[h], want_inv=True)
            o_ref[:, sl] = out
            state[h] = new_state
            inv_ref[h] = inv

    inv_spec = pl.BlockSpec((GDN_H, None, GDN_ROWS, GDN_ROWS), lambda c: (0, c, 0, 0))
    return pl.pallas_call(
        body, name="gdn_fwd", grid=(nc,),
        in_specs=[blk(0), blk(1), blk(2), blk(4), col(2 * GDN_H), sc, sc, ogs],
        out_specs=[blk(0), st, inv_spec],
        out_shape=[jax.ShapeDtypeStruct((S, 512), F32), jax.ShapeDtypeStruct((GDN_H, nc, 128, 128), F32),
                   jax.ShapeDtypeStruct((GDN_H, nc, GDN_ROWS, GDN_ROWS), F32)],
        scratch_shapes=[pltpu.VMEM((GDN_H, 128, 128), F32)],
        compiler_params=_cp("arbitrary"),
    )(qkvc, qkvc, qkvc, pm, abt, alog, dtb, og)


def gdn_bwd(qkvc, pm, abt, alog, dtb, og, st, inv, dout):
    S = qkvc.shape[0]
    nc = S // GDN_ROWS
    blk, col, sc, ogs, sts = _gdn_specs(nc, True)

    def body(q_ref, k_ref, v_ref, gate_ref, ab_ref, alog_ref, dtb_ref, og_ref, st_ref, inv_ref, do_ref,
             dq_ref, dk_ref, dv_ref, dgate_ref, dal_ref, dbl_ref, dalog_ref, ddtb_ref, dog_ref, dstate):
        c = pl.program_id(0)

        @pl.when(c == 0)
        def _():
            dstate[...] = jnp.zeros_like(dstate)

        for h in range(GDN_H):
            sl = slice(GDN_D * h, GDN_D * (h + 1))
            args = (q_ref[:, sl], k_ref[:, sl], v_ref[:, sl], gate_ref[:, sl], ab_ref[h:h + 1, :], ab_ref[GDN_H + h:GDN_H + h + 1, :],
                    alog_ref[h], dtb_ref[h], og_ref[...], st_ref[h])
            inv_h = inv_ref[h]
            _, vjp = jax.vjp(lambda *a, inv_h=inv_h: _gdn_chunk(*a, inv=inv_h), *args)
            g = vjp((do_ref[:, sl], dstate[h]))
            for o, gv in zip((dq_ref, dk_ref, dv_ref, dgate_ref), g[:4]):
                o[:, sl] = gv
            dal_ref[h:h + 1, :] = g[4]
            dbl_ref[h:h + 1, :] = g[5]
            dstate[h] = g[9]
            for o, gv in zip((dalog_ref, ddtb_ref, dog_ref), g[6:9]):
                @pl.when(c == 0)
                def _(o=o, gv=gv, h=h):
                    o[h] = gv

                @pl.when(c > 0)
                def _(o=o, gv=gv, h=h):
                    o[h] += gv

    ogo = pl.BlockSpec((GDN_H, 1, 128), lambda c: (0, 0, 0))
    inv_spec = pl.BlockSpec((GDN_H, None, GDN_ROWS, GDN_ROWS), lambda c: (0, nc - 1 - c, 0, 0))
    sd = jax.ShapeDtypeStruct
    return pl.pallas_call(
        body, name="gdn_bwd", grid=(nc,),
        in_specs=[blk(0), blk(1), blk(2), blk(4), col(2 * GDN_H), sc, sc, ogs, sts, inv_spec, blk(0)],
        out_specs=[blk(0), blk(0), blk(0), blk(0), col(GDN_H), col(GDN_H), sc, sc, ogo],
        out_shape=[sd((S, 512), F32)] * 4 + [sd((GDN_H, S), F32)] * 2 + [sd((GDN_H, 1, 1), F32)] * 2
        + [sd((GDN_H, 1, 128), F32)],
        scratch_shapes=[pltpu.VMEM((GDN_H, 128, 128), F32)],
        compiler_params=_cp("arbitrary"),
    )(qkvc, qkvc, qkvc, pm, abt, alog, dtb, og, st, inv, dout)


HG = 8
HG_LANES = HG * CA_D


def _group_mean_raw(y):
    r = lax.broadcasted_iota(jnp.int32, (128, 128), 0)
    c = lax.broadcasted_iota(jnp.int32, (128, 128), 1)
    g = jnp.where((r // CA_D) == (c // CA_D), 1.0 / CA_D, 0.0).astype(BF16)
    d = lambda u: lax.dot_general(u, g, (((1,), (0,)), ((), ())), preferred_element_type=F32)
    outs = []
    for j in range(y.shape[1] // 128):
        hi, lo = _split(y[:, 128 * j:128 * (j + 1)])
        outs.append(d(hi) + d(lo))
    return jnp.concatenate(outs, axis=1)


@jax.custom_vjp
def group_mean(y):
    return _group_mean_raw(y)


group_mean.defvjp(lambda y: (_group_mean_raw(y), None), lambda _, g: (_group_mean_raw(g),))


def f_headnorm(t, g):
    return (t * lax.rsqrt(group_mean(t * t) + EPS) * g,)


def _cattn_chunk(q, kb, vb, bias, valid):
    lane = lax.broadcasted_iota(jnp.int32, (1, 128), 1)
    m0 = (lane < CA_D).astype(F32)
    m1 = 1.0 - m0
    pairs = range(HG // 2)
    sl = [slice(128 * p, 128 * (p + 1)) for p in pairs]
    q2 = [jnp.concatenate([q[:, s] * m0, q[:, s] * m1], axis=0) for s in sl]
    sc = [mm_nt(q2[p], kb[:, sl[p]]) * (CA_D ** -0.5) + bias[sl[p]] for p in pairs]
    pr = [_softmax(jnp.where(valid, s, -1e30)) for s in sc]
    o2 = [mm(pr[p], vb[:, sl[p]]) for p in pairs]
    return jnp.concatenate([o[:CHUNK] * m0 + o[CHUNK:] * m1 for o in o2], axis=1)


def _cattn_valid(c):
    pos = lax.broadcasted_iota(jnp.int32, (1, CA_BAND), 1) + c * CHUNK
    return pos >= CA_PAD


def _cattn_specs(S):
    q_spec = pl.BlockSpec((CHUNK, HG_LANES), lambda h, c: (c, h))
    kv_spec = pl.BlockSpec((S + CA_PAD, HG_LANES), lambda h, c: (0, h))
    b_spec = pl.BlockSpec((HG * CHUNK, CA_BAND), lambda h, c: (h, 0))
    return q_spec, kv_spec, b_spec


def cattn_fwd(qn, kp, vp, bias):
    S = qn.shape[0]
    nc = S // CHUNK
    q_spec, kv_spec, b_spec = _cattn_specs(S)

    def body(q_ref, k_ref, v_ref, b_ref, o_ref):
        c = pl.program_id(1)
        start = pl.multiple_of(c * CHUNK, CHUNK)
        kb = k_ref[pl.ds(start, CA_BAND), :]
        vb = v_ref[pl.ds(start, CA_BAND), :]
        o_ref[...] = _cattn_chunk(q_ref[...], kb, vb, b_ref[...], _cattn_valid(c)).astype(o_ref.dtype)

    return pl.pallas_call(
        body, name="cattn_fwd", grid=(CA_H // HG, nc), in_specs=[q_spec, kv_spec, kv_spec, b_spec],
        out_specs=q_spec, out_shape=jax.ShapeDtypeStruct((S, D), BF16),
        compiler_params=_cp("parallel", "arbitrary"),
    )(qn, kp, vp, bias)


def kv_prep(qkv, kg):
    S = qkv.shape[0]
    tile = ROW_TILE
    lead = CA_PAD // tile

    def body(k_ref, v_ref, g_ref, kp_ref, vp_ref):
        i = pl.program_id(0)

        @pl.when(i < lead)
        def _():
            kp_ref[...] = jnp.zeros_like(kp_ref)
            vp_ref[...] = jnp.zeros_like(vp_ref)

        @pl.when(i >= lead)
        def _():
            kp_ref[...] = f_headnorm(k_ref[...], g_ref[...])[0].astype(BF16)
            vp_ref[...] = v_ref[...].astype(BF16)

    src = lambda cb: pl.BlockSpec((tile, D), lambda i, cb=cb: (jnp.maximum(i - lead, 0), cb))
    out = pl.BlockSpec((tile, D), lambda i: (i, 0))
    return pl.pallas_call(
        body, name="kv_prep", grid=((S + CA_PAD) // tile,),
        in_specs=[src(1), src(2), pl.BlockSpec((1, D), lambda i: (0, 0))], out_specs=[out, out],
        out_shape=[jax.ShapeDtypeStruct((S + CA_PAD, D), BF16)] * 2,
        compiler_params=_cp("parallel"),
    )(qkv, qkv, kg)


def cattn_bwd(qn, kp, vp, bias, do):
    S = qn.shape[0]
    nc = S // CHUNK
    q_spec, kv_spec, b_spec = _cattn_specs(S)

    def body(q_ref, k_ref, v_ref, b_ref, do_ref, dq_ref, dk_ref, dv_ref, db_ref):
        c = pl.program_id(1)

        @pl.when(c == 0)
        def _():
            dk_ref[...] = jnp.zeros_like(dk_ref)
            dv_ref[...] = jnp.zeros_like(dv_ref)
            db_ref[...] = jnp.zeros_like(db_ref)

        start = pl.multiple_of(c * CHUNK, CHUNK)
        kb = k_ref[pl.ds(start, CA_BAND), :].astype(F32)
        vb = v_ref[pl.ds(start, CA_BAND), :].astype(F32)
        valid = _cattn_valid(c)
        _, vjp = jax.vjp(lambda q, k, v, b: _cattn_chunk(q, k, v, b, valid), q_ref[...], kb, vb, b_ref[...])
        dq, dk, dv, db = vjp(do_ref[...])
        dq_ref[...] = dq
        dk_ref[pl.ds(start, CA_BAND), :] += dk
        dv_ref[pl.ds(start, CA_BAND), :] += dv
        db_ref[...] += db

    sd = jax.ShapeDtypeStruct
    return pl.pallas_call(
        body, name="cattn_bwd", grid=(CA_H // HG, nc), in_specs=[q_spec, kv_spec, kv_spec, b_spec, q_spec],
        out_specs=[q_spec, kv_spec, kv_spec, b_spec],
        out_shape=[sd((S, D), F32), sd((S + CA_PAD, D), F32), sd((S + CA_PAD, D), F32),
                   sd((CA_H * CHUNK, CA_BAND), F32)],
        compiler_params=_cp("parallel", "arbitrary"),
    )(qn, kp, vp, bias, do)


_REL_IDX = np.clip(np.arange(CHUNK)[:, None] - np.arange(CA_BAND)[None, :] + CA_PAD, -MAX_REL, MAX_REL) + MAX_REL
SKEW_W = CA_BAND + CHUNK


def rel_bias_grad(dbias):
    padded = jnp.pad(dbias, ((0, 0), (0, 0), (CHUNK, 0)))
    flat = jnp.pad(padded.reshape(CA_H, CHUNK * SKEW_W), ((0, 0), (0, CHUNK)))
    skew = flat.reshape(CA_H, CHUNK, SKEW_W + 1)

    first_near = SKEW_W - CHUNK - MAX_REL

    def fn(t):
        colsum = jnp.sum(t, axis=1, keepdims=True)
        j = lax.broadcasted_iota(jnp.int32, colsum.shape, 2)
        far = jnp.sum(jnp.where(j < first_near, colsum, 0.0), axis=2, keepdims=True)
        return (colsum + jnp.where(j == first_near, far, 0.0),)

    (colsum,) = whole("relbias_sum", fn, [skew], [((CA_H, 1, SKEW_W + 1), F32)])
    near = colsum[:, 0, first_near:SKEW_W][:, ::-1]
    return jnp.concatenate([jnp.zeros((CA_H, CHUNK + 1), F32), near], axis=1)


def rel_bias_expand(rb):
    near = rb[:, CHUNK + 1:][:, ::-1]
    far = jnp.broadcast_to(rb[:, 2 * MAX_REL:], (CA_H, SKEW_W - CHUNK - MAX_REL))
    t = jnp.concatenate([far, near, jnp.zeros((CA_H, 1), rb.dtype)], axis=1)
    rows = jnp.tile(t, (1, CHUNK))[:, :CHUNK * SKEW_W].reshape(CA_H, CHUNK, SKEW_W)
    return rows[:, :, CHUNK:]


def loss_head(y, target):
    S = y.shape[0]
    tile = min(ROW_TILE, S)

    def body(y_ref, t_ref, dy_ref, acc_ref):
        i = pl.program_id(0)
        e = y_ref[...] - t_ref[...]
        dy_ref[...] = e * (1.0 / D)
        part = jnp.sum(e * e, axis=0, keepdims=True) * (0.5 / D)

        @pl.when(i == 0)
        def _():
            acc_ref[...] = part

        @pl.when(i > 0)
        def _():
            acc_ref[...] += part

    row = pl.BlockSpec((tile, D), lambda i: (i, 0))
    return pl.pallas_call(
        body, name="loss_head", grid=(S // tile,), in_specs=[row, row],
        out_specs=[row, pl.BlockSpec((1, D), lambda i: (0, 0))],
        out_shape=[jax.ShapeDtypeStruct((S, D), F32), jax.ShapeDtypeStruct((1, D), F32)],
        compiler_params=_cp("arbitrary"),
    )(y, target)


ANY = pl.BlockSpec(memory_space=pl.ANY)


HBM = pl.BlockSpec(memory_space=pltpu.HBM)
SEM = pl.BlockSpec(memory_space=pltpu.SEMAPHORE)
EFFECT = pltpu.SideEffectType.DATAFLOW_SIDE_EFFECTING


def _chip_copies(src_ref, land_ref, send_sems, recv_sems, per_dest):
    x, y, c = lax.axis_index("x"), lax.axis_index("y"), lax.axis_index("c")
    me = 2 * x + y
    if per_dest == "sibling":
        cp = pltpu.make_async_remote_copy(src_ref=src_ref, dst_ref=land_ref, send_sem=send_sems.at[0],
                                          recv_sem=recv_sems.at[0], device_id=(x, y, 1 - c), device_id_type=MESH)
        return [(cp, cp)]
    out = []
    if per_dest in ("gather_half", "forward_half"):
        half = land_ref.shape[1] // 2
        mine = pl.ds(pl.multiple_of(c * half, 16), half)
        theirs = pl.ds(pl.multiple_of((1 - c) * half, 16), half)
        for j, (px, py) in enumerate([(1 - x, y), (x, 1 - y), (1 - x, 1 - y)]):
            peer = 2 * px + py
            if per_dest == "gather_half":
                pairs = ((src_ref.at[mine], land_ref.at[me, mine]), (src_ref.at[mine], land_ref.at[peer, mine]))
                dev = (px, py, c)
            else:
                pairs = ((land_ref.at[peer, mine],) * 2, (land_ref.at[peer, theirs],) * 2)
                dev = (x, y, 1 - c)
            out.append(tuple(pltpu.make_async_remote_copy(
                src_ref=s, dst_ref=d, send_sem=send_sems.at[j], recv_sem=recv_sems.at[j], device_id=dev,
                device_id_type=MESH) for s, d in pairs))
        return out
    for j, (px, py) in enumerate([(1 - x, y), (x, 1 - y), (1 - x, 1 - y)]):
        peer = 2 * px + py
        if per_dest:
            send = (src_ref.at[peer], land_ref.at[j])
            recv = (src_ref.at[me], land_ref.at[j])
        else:
            send = (src_ref, land_ref.at[me])
            recv = (src_ref, land_ref.at[peer])
        mk = lambda s, d, j=j, px=px, py=py: pltpu.make_async_remote_copy(
            src_ref=s, dst_ref=d, send_sem=send_sems.at[j], recv_sem=recv_sems.at[j],
            device_id=(px, py, c), device_id_type=MESH)
        out.append((mk(*send), mk(*recv)))
    return out


def exchange_start(src, land, carry, name, per_dest):
    def body(src_ref, land_ref, carry_ref, send_sems, recv_sems, src_out, land_out, carry_out):
        for send, _ in _chip_copies(src_ref, land_ref, send_sems, recv_sems, per_dest):
            send.start()

    hbm = lambda a: pltpu.HBM(a.shape, a.dtype)
    n = 1 if per_dest == "sibling" else 3
    return pl.pallas_call(
        body, name=name,
        out_shape=(pltpu.SemaphoreType.DMA((n,)), pltpu.SemaphoreType.DMA((n,)), hbm(src), hbm(land), hbm(carry)),
        in_specs=(HBM, HBM, HBM), out_specs=(SEM, SEM, HBM, HBM, HBM),
        input_output_aliases={0: 2, 1: 3, 2: 4},
        compiler_params=pltpu.CompilerParams(has_side_effects=EFFECT),
    )(pltpu.with_memory_space_constraint(src, pltpu.HBM), pltpu.with_memory_space_constraint(land, pltpu.HBM),
      pltpu.with_memory_space_constraint(carry, pltpu.HBM))


def exchange_wait(src, land, send_sems, recv_sems, after, name, per_dest):
    def body(src_ref, land_ref, send_sems_ref, recv_sems_ref, after_ref, src_out, land_out):
        for send, recv in _chip_copies(src_ref, land_ref, send_sems_ref, recv_sems_ref, per_dest):
            send.wait_send()
            recv.wait_recv()

    hbm = lambda a: pltpu.HBM(a.shape, a.dtype)
    return pl.pallas_call(
        body, name=name, out_shape=(hbm(src), hbm(land)),
        in_specs=(HBM, HBM, SEM, SEM, ANY), out_specs=(HBM, HBM), input_output_aliases={0: 0, 1: 1},
        compiler_params=pltpu.CompilerParams(has_side_effects=EFFECT),
    )(src, land, send_sems, recv_sems, after)


def sibling_exchange(srcs, name):
    n = len(srcs)

    def body(*refs):
        src_refs, out_refs, send_sems, recv_sems = refs[:n], refs[n:2 * n], refs[2 * n], refs[2 * n + 1]
        x, y, c = lax.axis_index("x"), lax.axis_index("y"), lax.axis_index("c")
        copies = [pltpu.make_async_remote_copy(src_ref=s, dst_ref=o, send_sem=send_sems.at[k], recv_sem=recv_sems.at[k],
                                               device_id=(x, y, 1 - c), device_id_type=MESH)
                  for k, (s, o) in enumerate(zip(src_refs, out_refs))]
        for cp in copies:
            cp.start()
        for cp in copies:
            cp.wait()

    return pl.pallas_call(
        body, name=name, in_specs=[ANY] * n, out_specs=[ANY] * n,
        out_shape=[jax.ShapeDtypeStruct(s.shape, s.dtype) for s in srcs],
        scratch_shapes=[pltpu.SemaphoreType.DMA((n,)), pltpu.SemaphoreType.DMA((n,))],
    )(*srcs)


def chip_gather(src, name):
    def body(src_ref, out_ref, send_sems, recv_sems, local_sem):
        x, y, c = lax.axis_index("x"), lax.axis_index("y"), lax.axis_index("c")
        me = 2 * x + y
        local = pltpu.make_async_copy(src_ref, out_ref.at[me], local_sem)
        local.start()
        peers = [(1 - x, y), (x, 1 - y), (1 - x, 1 - y)]
        mk = lambda j, px, py, slab: pltpu.make_async_remote_copy(
            src_ref=src_ref, dst_ref=out_ref.at[slab], send_sem=send_sems.at[j], recv_sem=recv_sems.at[j],
            device_id=(px, py, c), device_id_type=MESH)
        sends = [mk(j, px, py, me) for j, (px, py) in enumerate(peers)]
        for cp in sends:
            cp.start()
        for j, (px, py) in enumerate(peers):
            mk(j, px, py, 2 * px + py).wait_recv()
        for cp in sends:
            cp.wait_send()
        local.wait()

    return pl.pallas_call(
        body, name=name, in_specs=[ANY], out_specs=ANY,
        out_shape=jax.ShapeDtypeStruct((N_CHIPS,) + tuple(src.shape), src.dtype),
        scratch_shapes=[pltpu.SemaphoreType.DMA((3,)), pltpu.SemaphoreType.DMA((3,)), pltpu.SemaphoreType.DMA],
    )(src)


def all_exchange(src, name):
    def body(src_ref, out_ref, send_sems, recv_sems, local_sem):
        x, y, c = lax.axis_index("x"), lax.axis_index("y"), lax.axis_index("c")
        me = 4 * x + 2 * y + c
        local = pltpu.make_async_copy(src_ref, out_ref.at[me], local_sem)
        local.start()
        sends = []
        peers = []
        for k in range(1, N_DEV):
            bx, by, bc = (k >> 2) & 1, (k >> 1) & 1, k & 1
            px = 1 - x if bx else x
            py = 1 - y if by else y
            pc = 1 - c if bc else c
            peers.append((px, py, pc))
        for k, peer in enumerate(peers):
            cp = pltpu.make_async_remote_copy(src_ref=src_ref, dst_ref=out_ref.at[me], send_sem=send_sems.at[k],
                                              recv_sem=recv_sems.at[k], device_id=peer, device_id_type=MESH)
            cp.start()
            sends.append(cp)
        for k, (px, py, pc) in enumerate(peers):
            pltpu.make_async_remote_copy(src_ref=src_ref, dst_ref=out_ref.at[4 * px + 2 * py + pc],
                                         send_sem=send_sems.at[k], recv_sem=recv_sems.at[k],
                                         device_id=(px, py, pc), device_id_type=MESH).wait_recv()
        for cp in sends:
            cp.wait_send()
        local.wait()

    return pl.pallas_call(
        body, name=name, in_specs=[ANY], out_specs=ANY,
        out_shape=jax.ShapeDtypeStruct((N_DEV,) + tuple(src.shape), src.dtype),
        scratch_shapes=[pltpu.SemaphoreType.DMA((N_DEV - 1,)), pltpu.SemaphoreType.DMA((N_DEV - 1,)),
                        pltpu.SemaphoreType.DMA],
    )(src)


def sum_own_slabs(src, chip, land, name, tile=512):
    _, R, C = src.shape
    n = land.shape[0]
    tile = _tile(R, tile, 16)

    def body(chip_ref, o_ref, t_ref, out_ref):
        acc = o_ref[...].astype(F32)
        for s in range(n):
            acc = acc + t_ref[s].astype(F32)
        out_ref[...] = acc

    return pl.pallas_call(
        body, name=name,
        grid_spec=pltpu.PrefetchScalarGridSpec(
            num_scalar_prefetch=1, grid=(R // tile,),
            in_specs=[pl.BlockSpec((None, tile, C), lambda i, chip_ref: (chip_ref[0], i, 0)),
                      pl.BlockSpec((n, tile, C), lambda i, chip_ref: (0, i, 0))],
            out_specs=pl.BlockSpec((tile, C), lambda i, chip_ref: (i, 0))),
        out_shape=jax.ShapeDtypeStruct((R, C), F32),
        compiler_params=_cp("parallel"),
    )(jnp.reshape(chip, (1,)).astype(jnp.int32), src, land)


def sum_slabs(t, name, tile=512):
    n, R, C = t.shape
    tile = _tile(R, tile, 16)

    def body(t_ref, o_ref):
        acc = t_ref[0].astype(F32)
        for s in range(1, n):
            acc = acc + t_ref[s].astype(F32)
        o_ref[...] = acc

    return pl.pallas_call(
        body, name=name, grid=(R // tile,), in_specs=[pl.BlockSpec((n, tile, C), lambda i: (0, i, 0))],
        out_specs=pl.BlockSpec((tile, C), lambda i: (i, 0)), out_shape=jax.ShapeDtypeStruct((R, C), F32),
        compiler_params=_cp("parallel"),
    )(t)


PACK_ROW_MULT = 512


def _pad_rows(a, mult=16):
    r = (-a.shape[0]) % mult
    return jnp.pad(a, ((0, r), (0, 0))) if r else a


BIG = [
    ("ab_w_in", True, 2, 642), ("c_w_qkv", True, 2, 768), ("xa_w_kv", True, 4, 512),
    ("f_w_gate", True, 4, 704), ("f_w_up", True, 4, 704),
    ("ab_w_out", False, 2, 256), ("c_w_out", False, 2, 256), ("xa_w_q", False, 4, 256),
    ("xa_w_out", False, 4, 256), ("f_w_down", False, 4, 704), ("s5_w_glu", False, 2, 64),
]


GROUPS = ("B", "A")
GROUP_ROW_MULT = 64


def group_spec(layer, grp):
    i = layer // 2
    if grp == "A":
        return [("xa_w_kv", layer, True, 512), ("xa_w_q", layer, False, 256), ("xa_w_out", layer, False, 256),
                ("f_w_gate", layer, True, 704), ("f_w_up", layer, True, 704), ("f_w_down", layer, False, 704)]
    if layer % 2 == 0:
        return [("ab_w_in", i, True, 642), ("ab_w_out", i, False, 256), ("s5_w_glu", i, False, 64)]
    return [("c_w_qkv", i, True, 768), ("c_w_out", i, False, 256)]


def _seg_rows(rows):
    return rows + ((-rows) % 16)


def _f32_rows(a):
    bits = lax.bitcast_convert_type(a.reshape(-1), BF16).reshape(-1)
    return jnp.pad(bits, (0, 16 * D - bits.shape[0])).reshape(16, D)


def pack_group_shards(p, layer, grp):
    segs = []
    for name, idx, transposed, rows in group_spec(layer, grp):
        w = p[name][idx]
        if transposed:
            w = w.T
        segs.append(_pad_rows(w.astype(BF16).reshape(-1, D)))
    if grp == "B":
        small = p["gdn_conv_w"] if layer % 2 == 0 else p["c_norm_g"]
        segs.append(_f32_rows(small[layer // 2]))
    return _pad_rows(jnp.concatenate(segs, axis=0), GROUP_ROW_MULT)


def unpack_group_gathered(g, layer, grp):
    out, off = {}, 0
    for name, idx, transposed, rows in group_spec(layer, grp):
        seg = g[:, off:off + rows]
        if name == "s5_w_glu":
            out[name] = seg.reshape(N_CHIPS * 128, 512)
        else:
            out[name] = seg.reshape(N_CHIPS * rows, D)
        off += _seg_rows(rows)
    if grp == "B":
        n = 4 * 384 if layer % 2 == 0 else 256
        bits = g[:, off:off + 16].reshape(N_CHIPS, -1)[:, :2 * n].reshape(N_CHIPS, n, 2)
        small = lax.bitcast_convert_type(bits, F32)
        if layer % 2 == 0:
            out["gdn_conv_w"] = jnp.swapaxes(small.reshape(N_CHIPS, 4, 384), 0, 1).reshape(4, 1536)
        else:
            out["c_norm_g"] = small.reshape(D)
    return out


def pack_group_grads(gr, layer, grp):
    segs = []
    for name, idx, transposed, rows in group_spec(layer, grp):
        w = gr[name].astype(BF16)
        seg = w.reshape(N_CHIPS, rows, D)
        r = (-rows) % 16
        if r:
            seg = jnp.pad(seg, ((0, 0), (0, r), (0, 0)))
        segs.append(seg)
    out = jnp.concatenate(segs, axis=1)
    return jnp.pad(out, ((0, 0), (0, (-out.shape[1]) % GROUP_ROW_MULT), (0, 0)))


def unpack_group_reduced(g, layer, grp):
    out, off = {}, 0
    for name, idx, transposed, rows in group_spec(layer, grp):
        seg = g[off:off + rows]
        if name == "s5_w_glu":
            out[name] = seg.reshape(128, 512)
        else:
            out[name] = seg.T if (transposed and name not in ADAM_TRANSPOSED) else seg
        off += _seg_rows(rows)
    return out


ADAM_TRANSPOSED = ("f_w_gate", "f_w_up")


SMALL = ["ab_norm_g", "s5_a_re", "s5_a_im", "s5_log_dt", "s5_b_re", "s5_b_im", "s5_c_re", "s5_c_im", "s5_d",
         "s5_b_glu", "gdn_conv_w", "gdn_a_log", "gdn_dt_bias", "gdn_out_norm_g", "c_norm_g", "c_q_norm_g",
         "c_k_norm_g", "c_rel_bias", "mem_norm_g", "xa_norm_g", "xa_q_norm_g", "xa_k_norm_g", "f_norm_g"]


def _lane_rows(a):
    flat = a.reshape(-1).astype(F32)
    return jnp.pad(flat, (0, (-flat.shape[0]) % 1024)).reshape(-1, 128)


def pack_small(d, extra=None):
    parts = [_lane_rows(d[n]) for n in SMALL]
    if extra is not None:
        parts.append(_lane_rows(extra))
    rows = jnp.concatenate(parts, axis=0)
    return jnp.pad(rows, ((0, (-rows.shape[0]) % 128), (0, 0)))


def unpack_small(rows, shapes):
    out, off = {}, 0
    for n in SMALL:
        sz = int(np.prod(shapes[n]))
        k = 8 * -(-sz // 1024)
        out[n] = rows[off:off + k].reshape(-1)[:sz].reshape(shapes[n])
        off += k
    return out, rows[off:]


def _s5_blockdiag_b(b):
    bt = jnp.swapaxes(b, 1, 2).reshape(S5_GB, 8, S5_C, S5_P)
    eye = jnp.eye(8, dtype=b.dtype)
    return jnp.einsum("bgcp,gh->bgchp", bt, eye).reshape(S5_GB, 8 * S5_C, 8 * S5_P)


def _s5_blockdiag_c(c):
    ct = jnp.swapaxes(c, 1, 2).reshape(S5_GB, 8, S5_P, S5_C)
    eye = jnp.eye(8, dtype=c.dtype)
    return jnp.einsum("bgpc,gh->bgphc", ct, eye).reshape(S5_GB, 8 * S5_P, 8 * S5_C)


def _s5_diag_b(db):
    t = db.reshape(S5_GB, 8, S5_C, 8, S5_P)
    t = jnp.transpose(t, (0, 2, 4, 1, 3)).reshape(S5_GB, S5_C, S5_P, 64)
    d = t[..., ::9]
    return jnp.transpose(d, (0, 3, 2, 1)).reshape(S5_G, S5_P, S5_C)


def _s5_diag_c(dc):
    t = dc.reshape(S5_GB, 8, S5_P, 8, S5_C)
    t = jnp.transpose(t, (0, 2, 4, 1, 3)).reshape(S5_GB, S5_P, S5_C, 64)
    d = t[..., ::9]
    return jnp.transpose(d, (0, 3, 2, 1)).reshape(S5_G, S5_C, S5_P)


def _heads(t):
    return jnp.swapaxes(t.reshape(t.shape[0], CA_H, CA_D), 0, 1)


def _unheads(t):
    return jnp.swapaxes(t, 0, 1).reshape(t.shape[1], D)


def local_step(x, mem, target, p, wsrc, gsink, wpre=lambda layer, grp, carry: carry):
    S = x.shape[0]
    row2 = lambda a: a.reshape(1, -1)
    saved = []
    (mem_n,) = row_fwd("mem_norm", f_norm, [mem], [row2(p["mem_norm_g"])], [(D, BF16)])
    gs = {}

    for layer in range(DEPTH):
        i = layer // 2
        w = wsrc(layer, "B", x)
        sv = {"x0": x, "wB": w}
        if layer % 2 == 0:
            (h,) = row_fwd("norm", f_norm, [x], [row2(p["ab_norm_g"][i])], [(D, BF16)])
            w_in = w["ab_w_in"]
            pm = matmul("nt", h, w_in[:2560], "proj_main")
            abt = matmul("nt", w_in[2560:], h, "proj_ab")
            s5p = dict(
                are=p["s5_a_re"][i].reshape(1, -1), aim=p["s5_a_im"][i].reshape(1, -1),
                ldt=jnp.broadcast_to(p["s5_log_dt"][i][:, None], (S5_G, S5_P)).reshape(1, -1),
                b_re=_s5_blockdiag_b(p["s5_b_re"][i]), b_im=_s5_blockdiag_b(p["s5_b_im"][i]),
                c_re=_s5_blockdiag_c(p["s5_c_re"][i]), c_im=_s5_blockdiag_c(p["s5_c_im"][i]),
                dv=p["s5_d"][i].reshape(1, -1))
            y5, st5 = s5_fwd(pm, **s5p)
            (a_out,) = row_fwd("glu", f_glu, [y5], [w["s5_w_glu"], row2(p["s5_b_glu"][i])], [(S5_W, F32)])
            conv_w = w["gdn_conv_w"]
            qkvc = conv_fwd(pm, conv_w)
            alog = p["gdn_a_log"][i].reshape(GDN_H, 1, 1)
            dtb = p["gdn_dt_bias"][i].reshape(GDN_H, 1, 1)
            og = row2(p["gdn_out_norm_g"][i])
            b_out, stg, inv = gdn_fwd(qkvc, pm, abt, alog, dtb, og)
            cat = wpre(layer, "A", jnp.concatenate([a_out, b_out], axis=1))
            x, hq = matmul("nn", cat, w["ab_w_out"], "mix_out", add=x, norm_out=row2(p["xa_norm_g"][layer]))
            sv.update(h=h, pm=pm, s5p=s5p, y5=y5, st5=st5, qkvc=qkvc, abt=abt, alog=alog, dtb=dtb, og=og,
                      stg=stg, inv=inv, cat=cat, conv_w=conv_w)
        else:
            (h,) = row_fwd("norm", f_norm, [x], [row2(w["c_norm_g"])], [(D, BF16)])
            qkv = matmul("nt", h, w["c_w_qkv"], "proj_qkv")
            qg = jnp.tile(row2(p["c_q_norm_g"][i]), (1, CA_H))
            kg = jnp.tile(row2(p["c_k_norm_g"][i]), (1, CA_H))
            (qn,) = row_fwd("headnorm_q", f_headnorm, [(qkv, D, 0)], [qg], [(D, F32)])
            kp, vp = kv_prep(qkv, kg)
            bias = rel_bias_expand(p["c_rel_bias"][i]).reshape(CA_H * CHUNK, CA_BAND)
            o = wpre(layer, "A", cattn_fwd(qn, kp, vp, bias))
            x, hq = matmul("nn", o, w["c_w_out"], "mix_out", add=x, norm_out=row2(p["xa_norm_g"][layer]))
            sv.update(h=h, qkv=qkv, qg=qg, kg=kg, qn=qn, kp=kp, vp=vp, bias=bias, o=o)
        sv["x1"] = x
        w = wsrc(layer, "A", x)
        sv["wA"] = w
        qx = matmul("nn", hq, w["xa_w_q"], "xa_q")
        kv = matmul("nt", mem_n, w["xa_w_kv"], "xa_kv")
        xqg, xkg = row2(p["xa_q_norm_g"][layer]), row2(p["xa_k_norm_g"][layer])
        (ox,) = row_fwd("xattn", f_xattn, [qx], [kv, xqg, xkg], [(D, BF16)])
        x, hf = matmul("nn", ox, w["xa_w_out"], "xa_out", add=x, norm_out=row2(p["f_norm_g"][layer]))
        sv.update(hq=hq, qx=qx, kv=kv, ox=ox)
        sv["x2"] = x
        gate, up, act = ffn_in(hf, w["f_w_gate"], w["f_w_up"])
        if layer + 1 < DEPTH:
            act = wpre(layer + 1, "B", act)
        x = matmul("nn", act, w["f_w_down"], "ffn_down", add=x)
        sv.update(hf=hf, gate=gate, up=up, act=act)
        saved.append(sv)

    dx, loss_vec = loss_head(x, target)

    dmem_n = None
    for layer in reversed(range(DEPTH)):
        i = layer // 2
        sv = saved[layer]
        w, gw = sv["wA"], {}
        dgate, dup = ffn_dact(dx, w["f_w_down"], sv["gate"], sv["up"])
        gw["f_w_down"] = matmul("tn", sv["act"], dx, "ffn_dwd", out_dtype=BF16)
        gw["f_w_gate"] = matmul("tn", dgate, sv["hf"], "ffn_dwg", out_dtype=BF16)
        gw["f_w_up"] = matmul("tn", dup, sv["hf"], "ffn_dwu", out_dtype=BF16)
        dh = matmul("nn", dgate, w["f_w_gate"], "ffn_dhg")
        dx, dg = matmul("nn", dup, w["f_w_up"], "ffn_dhu", add=dh,
                        norm_bwd=(sv["x2"], row2(p["f_norm_g"][layer]), dx))
        gs.setdefault("f_norm_g", [None] * DEPTH)[layer] = dg[0]
        do = matmul("nt", dx, w["xa_w_out"], "xa_do")
        gw["xa_w_out"] = matmul("tn", sv["ox"], dx, "xa_dwo", out_dtype=BF16)
        xqg, xkg = row2(p["xa_q_norm_g"][layer]), row2(p["xa_k_norm_g"][layer])
        dqx, dkv, dqg, dkg = row_bwd("xattn_bwd", f_xattn, [sv["qx"]], [sv["kv"], xqg, xkg], [do],
                                     [0], [0, 1, 2])
        gs.setdefault("xa_q_norm_g", [None] * DEPTH)[layer] = dqg[0]
        gs.setdefault("xa_k_norm_g", [None] * DEPTH)[layer] = dkg[0]
        gw["xa_w_q"] = matmul("tn", sv["hq"], dqx, "xa_dwq", out_dtype=BF16)
        gw["xa_w_kv"] = matmul("tn", dkv, mem_n, "xa_dwkv", out_dtype=BF16)
        dx = gsink(layer, "A", gw, dx)
        dmem_n = matmul("nn", dkv, w["xa_w_kv"], "xa_dmem", add=dmem_n)
        dx, dg = matmul("nt", dqx, w["xa_w_q"], "xa_dhq", norm_bwd=(sv["x1"], row2(p["xa_norm_g"][layer]), dx))
        gs.setdefault("xa_norm_g", [None] * DEPTH)[layer] = dg[0]
        w, gw = sv["wB"], {}
        if layer % 2 == 0:
            dcat = matmul("nt", dx, w["ab_w_out"], "mix_dcat")
            gw["ab_w_out"] = matmul("tn", sv["cat"], dx, "mix_dwo", out_dtype=BF16)
            dy5, dwglu, dbglu = row_bwd("glu_bwd", f_glu, [sv["y5"]], [w["s5_w_glu"], row2(p["s5_b_glu"][i])],
                                        [(dcat, S5_W, 0)], [0], [0, 1])
            gw["s5_w_glu"] = dwglu
            gs.setdefault("s5_b_glu", [None] * 2)[i] = dbglu[0]
            s5p = sv["s5p"]
            du, dare, daim, dldt, dbre, dbim, dcre, dcim, ddv = s5_bwd(sv["pm"], sv["st5"], dy5, **s5p)
            (dldt_g,) = whole("s5_dt_sum", lambda t: (jnp.sum(t, axis=1, keepdims=True),),
                              [dldt.reshape(S5_G, S5_P)], [((S5_G, 1), F32)])
            for nme, val in (("s5_a_re", dare.reshape(S5_G, S5_P)), ("s5_a_im", daim.reshape(S5_G, S5_P)),
                             ("s5_log_dt", dldt_g[:, 0]), ("s5_b_re", _s5_diag_b(dbre)),
                             ("s5_b_im", _s5_diag_b(dbim)), ("s5_c_re", _s5_diag_c(dcre)),
                             ("s5_c_im", _s5_diag_c(dcim)), ("s5_d", ddv.reshape(S5_G, S5_C))):
                gs.setdefault(nme, [None] * 2)[i] = val
            dq, dk, dv, dgate, dal, dbl, dalog, ddtb, dog = gdn_bwd(
                sv["qkvc"], sv["pm"], sv["abt"], sv["alog"], sv["dtb"], sv["og"], sv["stg"], sv["inv"],
                dcat[:, S5_W:])
            (dog_s,) = whole("gdn_og_sum", lambda t: (jnp.sum(t, axis=0, keepdims=True),),
                             [dog.reshape(GDN_H, GDN_D)], [((1, GDN_D), F32)])
            gs.setdefault("gdn_out_norm_g", [None] * 2)[i] = dog_s[0]
            gs.setdefault("gdn_a_log", [None] * 2)[i] = dalog.reshape(GDN_H)
            gs.setdefault("gdn_dt_bias", [None] * 2)[i] = ddtb.reshape(GDN_H)
            dqkvc = jnp.concatenate([dq, dk, dv], axis=1)
            dqkv, dconv = conv_bwd(sv["pm"], sv["conv_w"], dqkvc)
            gs.setdefault("gdn_conv_w", [None] * 2)[i] = dconv
            dpm = jnp.concatenate([du, dqkv, dgate], axis=1).astype(BF16)
            dabt = jnp.concatenate([dal, dbl], axis=0)
            dw_main = matmul("tn", dpm, sv["h"], "proj_dw", out_dtype=BF16)
            dw_ab = matmul("nn", dabt, sv["h"], "proj_ab_dw", out_dtype=BF16)
            gw["ab_w_in"] = jnp.concatenate([dw_main, dw_ab], axis=0)
            dx = gsink(layer, "B", gw, dx)
            w_in = w["ab_w_in"]
            dh = matmul("nn", dpm, w_in[:2560], "proj_dh")
            dx, dg = matmul("tn", dabt, w_in[2560:], "proj_ab_dh", add=dh,
                            norm_bwd=(sv["x0"], row2(p["ab_norm_g"][i]), dx))
            gs.setdefault("ab_norm_g", [None] * 2)[i] = dg[0]
        else:
            do = matmul("nt", dx, w["c_w_out"], "mix_dcat")
            gw["c_w_out"] = matmul("tn", sv["o"], dx, "mix_dwo", out_dtype=BF16)
            dqn, dkp, dvp, dbias = cattn_bwd(sv["qn"], sv["kp"], sv["vp"], sv["bias"], do)
            gs.setdefault("c_rel_bias", [None] * 2)[i] = rel_bias_grad(dbias.reshape(CA_H, CHUNK, CA_BAND))
            dq, dqg = row_bwd("headnorm_bwd", f_headnorm, [(sv["qkv"], D, 0)], [sv["qg"]], [dqn], [0], [0])
            dk, dkg = row_bwd("headnorm_bwd", f_headnorm, [(sv["qkv"], D, 1)], [sv["kg"]],
                              [(dkp, D, 0, CA_PAD)], [0], [0])
            head_sum = lambda t: (jnp.sum(t, axis=0, keepdims=True),)
            (dqg,) = whole("headgain_sum", head_sum, [dqg.reshape(CA_H, CA_D)], [((1, CA_D), F32)])
            (dkg,) = whole("headgain_sum", head_sum, [dkg.reshape(CA_H, CA_D)], [((1, CA_D), F32)])
            gs.setdefault("c_q_norm_g", [None] * 2)[i] = dqg[0]
            gs.setdefault("c_k_norm_g", [None] * 2)[i] = dkg[0]
            dqkv = jnp.concatenate([dq, dk, dvp[CA_PAD:]], axis=1).astype(BF16)
            gw["c_w_qkv"] = matmul("tn", dqkv, sv["h"], "proj_qkv_dw", out_dtype=BF16)
            dx = gsink(layer, "B", gw, dx)
            dx, dg = matmul("nn", dqkv, w["c_w_qkv"], "proj_qkv_dh",
                            norm_bwd=(sv["x0"], row2(w["c_norm_g"]), dx))
            gs.setdefault("c_norm_g", [None] * 2)[i] = dg[0]
    (dmg,) = row_bwd("mem_norm_bwd", f_norm, [mem], [row2(p["mem_norm_g"])], [dmem_n], [], [0])
    small = {n: jnp.stack(v) for n, v in gs.items()}
    small["mem_norm_g"] = dmg[0]
    return loss_vec, dx, small


ADAM_BLOCK_BYTES = 3 << 19


def adam(w, g, m, v, name):
    shape = w.shape
    if w.ndim == 3:
        d0, n, d2 = shape
        fits = [t for t in range(8, n + 1, 8) if n % t == 0 and d0 * t * d2 * 4 <= ADAM_BLOCK_BYTES]
        return tuple(row_fwd(name, f_adam, [w, g, m, v], [], [((d0, d2), F32)] * 3, tile=max(fits)))
    cols = shape[-1]
    w2, g2, m2, v2 = (t.reshape(-1, cols) for t in (w, g, m, v))
    rows = w2.shape[0]
    tile = rows if rows <= 512 else _tile(rows, 512, 8)
    outs = row_fwd(name, f_adam, [w2, g2, m2, v2], [], [(cols, F32)] * 3, tile=tile)
    return tuple(o.reshape(shape) for o in outs)


WEIGHTS = ['ab_norm_g', 'ab_w_in', 'ab_w_out', 's5_a_re', 's5_a_im', 's5_log_dt', 's5_b_re', 's5_b_im', 's5_c_re',
           's5_c_im', 's5_d', 's5_w_glu', 's5_b_glu', 'gdn_conv_w', 'gdn_a_log', 'gdn_dt_bias', 'gdn_out_norm_g',
           'c_norm_g', 'c_w_qkv', 'c_w_out', 'c_q_norm_g', 'c_k_norm_g', 'c_rel_bias', 'mem_norm_g', 'xa_norm_g',
           'xa_w_q', 'xa_w_kv', 'xa_w_out', 'xa_q_norm_g', 'xa_k_norm_g', 'f_norm_g', 'f_w_gate', 'f_w_up',
           'f_w_down']
SHARDED_SMALL = {"gdn_conv_w": (2, 384), "c_norm_g": (1, 256)}


def kernel(x, mem, ab_norm_g, ab_w_in, ab_w_out, s5_a_re, s5_a_im, s5_log_dt, s5_b_re, s5_b_im, s5_c_re, s5_c_im, s5_d, s5_w_glu, s5_b_glu, gdn_conv_w, gdn_a_log, gdn_dt_bias, gdn_out_norm_g, c_norm_g, c_w_qkv, c_w_out, c_q_norm_g, c_k_norm_g, c_rel_bias, mem_norm_g, xa_norm_g, xa_w_q, xa_w_kv, xa_w_out, xa_q_norm_g, xa_k_norm_g, f_norm_g, f_w_gate, f_w_up, f_w_down, loss_target, m_ab_norm_g, m_ab_w_in, m_ab_w_out, m_s5_a_re, m_s5_a_im, m_s5_log_dt, m_s5_b_re, m_s5_b_im, m_s5_c_re, m_s5_c_im, m_s5_d, m_s5_w_glu, m_s5_b_glu, m_gdn_conv_w, m_gdn_a_log, m_gdn_dt_bias, m_gdn_out_norm_g, m_c_norm_g, m_c_w_qkv, m_c_w_out, m_c_q_norm_g, m_c_k_norm_g, m_c_rel_bias, m_mem_norm_g, m_xa_norm_g, m_xa_w_q, m_xa_w_kv, m_xa_w_out, m_xa_q_norm_g, m_xa_k_norm_g, m_f_norm_g, m_f_w_gate, m_f_w_up, m_f_w_down, v_ab_norm_g, v_ab_w_in, v_ab_w_out, v_s5_a_re, v_s5_a_im, v_s5_log_dt, v_s5_b_re, v_s5_b_im, v_s5_c_re, v_s5_c_im, v_s5_d, v_s5_w_glu, v_s5_b_glu, v_gdn_conv_w, v_gdn_a_log, v_gdn_dt_bias, v_gdn_out_norm_g, v_c_norm_g, v_c_w_qkv, v_c_w_out, v_c_q_norm_g, v_c_k_norm_g, v_c_rel_bias, v_mem_norm_g, v_xa_norm_g, v_xa_w_q, v_xa_w_kv, v_xa_w_out, v_xa_q_norm_g, v_xa_k_norm_g, v_f_norm_g, v_f_w_gate, v_f_w_up, v_f_w_down):
    args = locals()
    p = {n: args[n] for n in WEIGHTS}
    m = {n: args["m_" + n] for n in WEIGHTS}
    v = {n: args["v_" + n] for n in WEIGHTS}
    chip = 2 * lax.axis_index("x") + lax.axis_index("y")

    carry = x[0]
    gathers = {}
    for layer in range(DEPTH):
        for grp in GROUPS:
            src = pack_group_shards(p, layer, grp)
            land = lax.dynamic_update_slice(lax.empty((N_CHIPS,) + src.shape, BF16), src[None], (chip, 0, 0))
            send_sems, recv_sems, src, land, carry = exchange_start(
                src, land, carry, f"gather_start_{layer}{grp}", per_dest="gather_half")
            gathers[layer, grp] = (src, land, send_sems, recv_sems)
    forwards = {}

    def wpre(layer, grp, carry):
        if (layer, grp) not in forwards:
            src, land, send_sems, recv_sems = gathers[layer, grp]
            src, land = exchange_wait(src, land, send_sems, recv_sems, carry, f"gather_wait_{layer}{grp}",
                                      per_dest="gather_half")
            send_sems, recv_sems, src, land, carry = exchange_start(
                src, land, carry, f"forward_start_{layer}{grp}", per_dest="forward_half")
            forwards[layer, grp] = (src, land, send_sems, recv_sems)
        return carry

    def wsrc(layer, grp, after):
        wpre(layer, grp, after)
        src, land, send_sems, recv_sems = forwards[layer, grp]
        _, land = exchange_wait(src, land, send_sems, recv_sems, after, f"forward_wait_{layer}{grp}",
                                per_dest="forward_half")
        return unpack_group_gathered(land, layer, grp)

    scatters, siblings = [], []
    LAG = 2

    def finish(carry):
        layer, grp, src, land, send_sems, recv_sems = scatters[len(siblings)]
        src, land = exchange_wait(src, land, send_sems, recv_sems, carry, f"scatter_wait_{layer}{grp}", per_dest=True)
        part = sum_own_slabs(src, chip, land, "sum_chips")
        send_sems, recv_sems, part, other, carry = exchange_start(
            part, lax.empty(part.shape, F32), carry, f"sibling_start_{layer}{grp}", per_dest="sibling")
        siblings.append((layer, grp, part, other, send_sems, recv_sems))
        return carry

    def gsink(layer, grp, gw, carry):
        src = pack_group_grads(gw, layer, grp)
        land = lax.empty((3,) + src.shape[1:], BF16)
        send_sems, recv_sems, src, land, carry = exchange_start(
            src, land, carry, f"scatter_start_{layer}{grp}", per_dest=True)
        scatters.append((layer, grp, src, land, send_sems, recv_sems))
        if len(scatters) > LAG:
            carry = finish(carry)
        return carry

    carry = wpre(0, "B", carry)
    loss_vec, dx, g_small = local_step(carry, mem[0], loss_target[0], p, wsrc, gsink, wpre)

    full_shapes = {n: ((2, 4, 1536) if n == "gdn_conv_w" else (2, D) if n == "c_norm_g" else p[n].shape)
                   for n in SMALL}
    small_mine = pack_small(g_small, extra=loss_vec)
    (small_other,) = sibling_exchange([small_mine], "sibling_small")
    (small_chip,) = row_fwd("sum_small_cores", lambda a, b: (a + b,), [small_mine, small_other], [], [(128, F32)],
                            tile=128)
    small_sum = sum_slabs(chip_gather(small_chip, "gather_small"), "sum_small")
    g_s, rest = unpack_small(small_sum, full_shapes)
    (loss11,) = whole("loss_sum", lambda t: (jnp.sum(jnp.sum(t, axis=1, keepdims=True), axis=0, keepdims=True),),
                      [rest[:D // 128]], [((1, 1), F32)])
    for n, (axis, width) in SHARDED_SMALL.items():
        g_s[n] = lax.dynamic_slice_in_dim(g_s[n], chip * width, width, axis=axis)
    delta, new_m, new_v = {}, {}, {}
    for n in SMALL:
        shape = p[n].shape
        two = (1, shape[0]) if len(shape) == 1 else (int(np.prod(shape[:-1])), shape[-1])
        outs = whole("adam_" + n, f_adam, [t.reshape(two) for t in (p[n], g_s[n], m[n], v[n])], [(two, F32)] * 3)
        delta[n], new_m[n], new_v[n] = (o.reshape(shape) for o in outs)

    while len(siblings) < len(scatters):
        dx = finish(dx)

    per_layer = {}
    for layer, grp, part, other, send_sems, recv_sems in siblings:
        part, other = exchange_wait(part, other, send_sems, recv_sems, dx, f"sibling_wait_{layer}{grp}",
                                    per_dest="sibling")
        (total,) = row_fwd("sum_cores", lambda a, b: (a + b,), [part, other], [], [(D, F32)],
                           tile=_tile(part.shape[0], 512, 16))
        for name, g in unpack_group_reduced(total, layer, grp).items():
            per_layer.setdefault(name, {})[layer] = g
    grads = {name: jnp.stack([d[k] for k in sorted(d)]) for name, d in per_layer.items()}
    grads.update(g_s)
    for name, _, _, _ in BIG:
        if name in ADAM_TRANSPOSED:
            t = lambda a: jnp.swapaxes(a, 1, 2)
            outs = adam(t(p[name]), grads[name], t(m[name]), t(v[name]), "adam_" + name)
            delta[name], new_m[name], new_v[name] = (t(o) for o in outs)
            grads[name] = t(grads[name])
        else:
            delta[name], new_m[name], new_v[name] = adam(p[name], grads[name], m[name], v[name], "adam_" + name)

    return (loss11[0, 0], dx[None], *[grads[n] for n in WEIGHTS], *[delta[n] for n in WEIGHTS],
            *[new_m[n] for n in WEIGHTS], *[new_v[n] for n in WEIGHTS])
```

```python
import numpy as np
import jax
import jax.numpy as jnp
from jax import lax
from jax.experimental import pallas as pl
from jax.experimental.pallas import tpu as pltpu

F32 = jnp.float32
BF16 = jnp.bfloat16
MESH = pl.DeviceIdType.MESH

D = 1024
CHUNK = 64
EPS = 1e-6
S5_W = 512
S5_G = 32
S5_C = 16
S5_P = 64
S5_GB = 4
GDN_H = 4
GDN_D = 128
CA_H = 16
CA_D = 64
CA_LEFT = 8
CA_BAND = (CA_LEFT + 1) * CHUNK
CA_PAD = CA_LEFT * CHUNK
MAX_REL = 128
XA_H = 4
XA_D = 256
FFN = 2816
DEPTH = 4
N_CHIPS = 4
LR, B1, B2, AEPS, WD, STEP = 0.001, 0.9, 0.999, 1e-08, 0.01, 10

VMEM_LIMIT = 56 * 1024 * 1024
ROW_TILE = 256


def _cp(*sem):
    return pltpu.CompilerParams(dimension_semantics=sem, vmem_limit_bytes=VMEM_LIMIT)


def _dg(a, b, ca, cb):
    return lax.dot_general(a.astype(BF16), b.astype(BF16), (((ca,), (cb,)), ((), ())),
                           preferred_element_type=F32)


@jax.custom_vjp
def mm(a, b):
    return _dg(a, b, 1, 0)


@jax.custom_vjp
def mm_nt(a, b):
    return _dg(a, b, 1, 1)


@jax.custom_vjp
def mm_tn(a, b):
    return _dg(a, b, 0, 0)


mm.defvjp(lambda a, b: (mm(a, b), (a, b)), lambda r, g: (mm_nt(g, r[1]), mm_tn(r[0], g)))
mm_nt.defvjp(lambda a, b: (mm_nt(a, b), (a, b)), lambda r, g: (mm(g, r[1]), mm_tn(g, r[0])))
mm_tn.defvjp(lambda a, b: (mm_tn(a, b), (a, b)), lambda r, g: (mm_nt(r[1], g), mm(r[0], g)))


def _split(a):
    hi = a.astype(BF16)
    return hi, (a - hi.astype(F32)).astype(BF16)


def _tri_mm(v, upper):
    T = v.shape[0]
    r = lax.broadcasted_iota(jnp.int32, (T, T), 0)
    c = lax.broadcasted_iota(jnp.int32, (T, T), 1)
    m = ((c >= r) if upper else (r >= c)).astype(BF16)
    hi, lo = _split(v)
    d = lambda u: lax.dot_general(m, u, (((1,), (0,)), ((), ())), preferred_element_type=F32)
    return d(hi) + d(lo)


@jax.custom_vjp
def cumsum_rows(v):
    return _tri_mm(v, False)


cumsum_rows.defvjp(lambda v: (_tri_mm(v, False), None), lambda _, g: (_tri_mm(g, True),))


def _rms(x, g):
    return x * lax.rsqrt(jnp.mean(x * x, axis=-1, keepdims=True) + EPS) * g


def _softmax(s):
    e = jnp.exp(s - lax.stop_gradient(jnp.max(s, axis=-1, keepdims=True)))
    return e / jnp.sum(e, axis=-1, keepdims=True)


def _softplus(x):
    return jnp.maximum(x, 0.0) + jnp.log(1.0 + jnp.exp(-jnp.abs(x)))


def _tile(n, cap, align):
    if n <= cap:
        return n
    best = None
    for d in range(align, cap + 1, align):
        if n % d == 0:
            best = d
    assert best is not None, (n, cap, align)
    return best


def matmul(mode, a, b, name, out_dtype=F32, add=None, norm_out=None, norm_bwd=None):
    if mode == "nn":
        (M, K), (K2, N) = a.shape, b.shape
    elif mode == "nt":
        (M, K), (N, K2) = a.shape, b.shape
    else:
        (K, M), (K2, N) = a.shape, b.shape
    assert K == K2, (mode, a.shape, b.shape)
    if mode != "tn" and norm_out is None and norm_bwd is None:
        tm, tn, tk = _tile(M, 1024, 128), _tile(N, 512, 128), _tile(K, 2048, 128)
    else:
        tm, tn, tk = _tile(M, 512, 128), _tile(N, 1536, 128), _tile(K, 2048, 128)
    nk = K // tk
    if mode == "nn":
        a_spec = pl.BlockSpec((tm, tk), lambda i, j, k: (i, k))
        b_spec = pl.BlockSpec((tk, tn), lambda i, j, k: (k, j))
        dn = (((1,), (0,)), ((), ()))
    elif mode == "nt":
        a_spec = pl.BlockSpec((tm, tk), lambda i, j, k: (i, k))
        b_spec = pl.BlockSpec((tn, tk), lambda i, j, k: (j, k))
        dn = (((1,), (1,)), ((), ()))
    else:
        a_spec = pl.BlockSpec((tk, tm), lambda i, j, k: (k, i))
        b_spec = pl.BlockSpec((tk, tn), lambda i, j, k: (k, j))
        dn = (((0,), (0,)), ((), ()))
    o_spec = pl.BlockSpec((tm, tn), lambda i, j, k: (i, j))
    has_add = add is not None
    g_spec = pl.BlockSpec((1, tn), lambda i, j, k: (0, j))
    extra, extra_specs, out_shapes, out_specs = [], [], [jax.ShapeDtypeStruct((M, N), out_dtype)], [o_spec]
    sem = ("parallel", "parallel", "arbitrary")
    if norm_out is not None:
        assert tn == N
        extra, extra_specs = [norm_out], [g_spec]
        out_shapes.append(jax.ShapeDtypeStruct((M, N), BF16))
        out_specs.append(o_spec)
    if norm_bwd is not None:
        assert tn == N
        x_in, g_in, res_in = norm_bwd
        extra, extra_specs = [x_in, g_in, res_in], [o_spec, g_spec, o_spec]
        out_shapes.append(jax.ShapeDtypeStruct((1, N), F32))
        out_specs.append(g_spec)
        sem = ("arbitrary", "arbitrary", "arbitrary")
    n_in = 2 + int(has_add) + len(extra)
    n_out = len(out_shapes)

    def body(*refs):
        a_ref, b_ref = refs[0], refs[1]
        add_ref = refs[2] if has_add else None
        extra_refs = refs[2 + int(has_add):n_in]
        o_ref = refs[n_in]
        acc_ref = refs[-1]
        i = pl.program_id(0)
        p = lax.dot_general(a_ref[...].astype(BF16), b_ref[...].astype(BF16), dn,
                            preferred_element_type=F32)

        def finish(total):
            if has_add:
                total = total + add_ref[...]
            if norm_out is not None:
                o_ref[...] = total.astype(o_ref.dtype)
                refs[n_in + 1][...] = _rms(total, extra_refs[0][...]).astype(BF16)
            elif norm_bwd is not None:
                x_ref, g_ref, res_ref = extra_refs
                _, vjp = jax.vjp(f_norm_res, x_ref[...], g_ref[...])
                dx, dg = vjp((total, res_ref[...]))
                o_ref[...] = dx
                dg_ref = refs[n_in + 1]

                @pl.when(i == 0)
                def _():
                    dg_ref[...] = dg

                @pl.when(i > 0)
                def _():
                    dg_ref[...] += dg
            else:
                o_ref[...] = total.astype(o_ref.dtype)

        if nk == 1:
            finish(p)
        else:
            k = pl.program_id(2)

            @pl.when(k == 0)
            def _():
                acc_ref[...] = p

            @pl.when(k > 0)
            def _():
                acc_ref[...] += p

            @pl.when(k == nk - 1)
            def _():
                finish(acc_ref[...])

    ins = [a, b] + ([add] if has_add else []) + extra
    specs = [a_spec, b_spec] + ([o_spec] if has_add else []) + extra_specs
    out = pl.pallas_call(
        body, name=name, grid=(M // tm, N // tn, nk), in_specs=specs, out_specs=out_specs,
        out_shape=out_shapes, scratch_shapes=[pltpu.VMEM((tm, tn), F32)],
        compiler_params=_cp(*sem),
    )(*ins)
    return out[0] if n_out == 1 else out


def ffn_in(h, wg, wu):
    (S, K), F = h.shape, wg.shape[0]
    tm, tn = _tile(S, 1024, 128), _tile(F, 512, 128)
    dn = (((1,), (1,)), ((), ()))

    def body(h_ref, wg_ref, wu_ref, g_ref, u_ref, act_ref):
        a = h_ref[...].astype(BF16)
        g = lax.dot_general(a, wg_ref[...].astype(BF16), dn, preferred_element_type=F32)
        u = lax.dot_general(a, wu_ref[...].astype(BF16), dn, preferred_element_type=F32)
        g_ref[...] = g.astype(BF16)
        u_ref[...] = u.astype(BF16)
        act_ref[...] = f_swiglu(g, u)[0].astype(BF16)

    w_spec = pl.BlockSpec((tn, K), lambda i, j: (j, 0))
    o_spec = pl.BlockSpec((tm, tn), lambda i, j: (i, j))
    sd = jax.ShapeDtypeStruct
    return pl.pallas_call(
        body, name="ffn_in", grid=(S // tm, F // tn),
        in_specs=[pl.BlockSpec((tm, K), lambda i, j: (i, 0)), w_spec, w_spec], out_specs=[o_spec] * 3,
        out_shape=[sd((S, F), BF16), sd((S, F), BF16), sd((S, F), BF16)],
        compiler_params=_cp("parallel", "parallel"),
    )(h, wg, wu)


def ffn_dact(dy, wd, gate, up):
    (S, K), F = dy.shape, wd.shape[0]
    tm, tn = _tile(S, 1024, 128), _tile(F, 512, 128)
    dn = (((1,), (1,)), ((), ()))

    def body(dy_ref, wd_ref, g_ref, u_ref, dg_ref, du_ref):
        dact = lax.dot_general(dy_ref[...].astype(BF16), wd_ref[...].astype(BF16), dn, preferred_element_type=F32)
        _, vjp = jax.vjp(f_swiglu, g_ref[...].astype(F32), u_ref[...].astype(F32))
        dg, du = vjp((dact,))
        dg_ref[...] = dg.astype(BF16)
        du_ref[...] = du.astype(BF16)

    o_spec = pl.BlockSpec((tm, tn), lambda i, j: (i, j))
    sd = jax.ShapeDtypeStruct
    return pl.pallas_call(
        body, name="ffn_dact", grid=(S // tm, F // tn),
        in_specs=[pl.BlockSpec((tm, K), lambda i, j: (i, 0)), pl.BlockSpec((tn, K), lambda i, j: (j, 0)),
                  o_spec, o_spec],
        out_specs=[o_spec] * 2, out_shape=[sd((S, F), BF16)] * 2,
        compiler_params=_cp("parallel", "parallel"),
    )(dy, wd, gate, up)


def _row_spec(arr, tile):
    if isinstance(arr, tuple):
        a, w, cb = arr[:3]
        ro = (arr[3] // tile) if len(arr) > 3 else 0
        assert len(arr) < 4 or arr[3] % tile == 0
        return a, pl.BlockSpec((tile, w), lambda i, cb=cb, ro=ro: (i + ro, cb)), (tile, w)
    if arr.ndim == 3:
        d0, _, d2 = arr.shape
        return arr, pl.BlockSpec((d0, tile, d2), lambda i: (0, i, 0)), (d0, tile, d2)
    return arr, pl.BlockSpec((tile, arr.shape[1]), lambda i: (i, 0)), (tile, arr.shape[1])


def _full_spec(arr):
    nd = arr.ndim
    return pl.BlockSpec(arr.shape, lambda i, nd=nd: (0,) * nd)


def _n_rows(arr):
    a = arr[0] if isinstance(arr, tuple) else arr
    return a.shape[1] if a.ndim == 3 else a.shape[0]


def _row_out_shape(shape_tail, n, dtype):
    if isinstance(shape_tail, tuple):
        d0, d2 = shape_tail
        return (jax.ShapeDtypeStruct((d0, n, d2), dtype),
                lambda tile: pl.BlockSpec((d0, tile, d2), lambda i: (0, i, 0)))
    return (jax.ShapeDtypeStruct((n, shape_tail), dtype),
            lambda tile: pl.BlockSpec((tile, shape_tail), lambda i: (i, 0)))


def _f32(v):
    return v.astype(F32) if v.dtype == BF16 else v


def row_fwd(name, fn, rows, fulls, outs, tile=ROW_TILE):
    n = _n_rows(rows[0])
    tile = min(tile, n)
    assert n % tile == 0, (name, n, tile)
    rs = [_row_spec(r, tile) for r in rows]
    os_ = [_row_out_shape(w, n, dt) for w, dt in outs]
    nr, nf = len(rows), len(fulls)

    def body(*refs):
        vals = [_f32(r[...]) for r in refs[:nr + nf]]
        res = fn(*vals)
        for r, v in zip(refs[nr + nf:], res):
            r[...] = v.astype(r.dtype)

    out = pl.pallas_call(
        body, name=name, grid=(n // tile,),
        in_specs=[s for _, s, _ in rs] + [_full_spec(f) for f in fulls],
        out_specs=[mk(tile) for _, mk in os_], out_shape=[sh for sh, _ in os_],
        compiler_params=_cp("parallel"),
    )(*[a for a, _, _ in rs], *fulls)
    return out


def row_bwd(name, fn, rows, fulls, cts, want_rows, want_fulls, row_dtypes=None, tile=ROW_TILE):
    n = _n_rows(rows[0])
    tile = min(tile, n)
    assert n % tile == 0, (name, n, tile)
    rs = [_row_spec(r, tile) for r in rows]
    cs = [_row_spec(c, tile) for c in cts]
    nr, nf, nc = len(rows), len(fulls), len(cts)
    row_dtypes = row_dtypes or [F32] * len(want_rows)
    out_shapes, out_specs = [], []
    for k, idx in enumerate(want_rows):
        a, _, blk = rs[idx]
        if len(blk) == 3:
            sh, mk = _row_out_shape((blk[0], blk[2]), n, row_dtypes[k])
        else:
            sh, mk = _row_out_shape(blk[1], n, row_dtypes[k])
        out_shapes.append(sh)
        out_specs.append(mk(tile))
    for idx in want_fulls:
        out_shapes.append(jax.ShapeDtypeStruct(fulls[idx].shape, F32))
        out_specs.append(_full_spec(fulls[idx]))
    n_wr = len(want_rows)

    def body(*refs):
        i = pl.program_id(0)
        vals = [_f32(r[...]) for r in refs[:nr + nf]]
        ct_vals = [_f32(r[...]) for r in refs[nr + nf:nr + nf + nc]]
        outs = refs[nr + nf + nc:]
        _, vjp = jax.vjp(fn, *vals)
        grads = vjp(tuple(ct_vals))
        for k, idx in enumerate(want_rows):
            outs[k][...] = grads[idx].astype(outs[k].dtype)
        for k, idx in enumerate(want_fulls):
            o = outs[n_wr + k]
            g = grads[nr + idx]

            @pl.when(i == 0)
            def _(o=o, g=g):
                o[...] = g

            @pl.when(i > 0)
            def _(o=o, g=g):
                o[...] += g

    out = pl.pallas_call(
        body, name=name, grid=(n // tile,),
        in_specs=[s for _, s, _ in rs] + [_full_spec(f) for f in fulls] + [s for _, s, _ in cs],
        out_specs=out_specs, out_shape=out_shapes,
        compiler_params=_cp("arbitrary"),
    )(*[a for a, _, _ in rs], *fulls, *[a for a, _, _ in cs])
    return out


def whole(name, fn, args, outs):
    def body(*refs):
        res = fn(*[r[...] for r in refs[:len(args)]])
        for r, v in zip(refs[len(args):], res):
            r[...] = v.astype(r.dtype)

    return pl.pallas_call(
        body, name=name, out_shape=[jax.ShapeDtypeStruct(s, d) for s, d in outs],
        compiler_params=pltpu.CompilerParams(vmem_limit_bytes=VMEM_LIMIT),
    )(*args)


def f_norm(x, g):
    return (_rms(x, g),)


def f_norm_res(x, g):
    return _rms(x, g), x


def f_swiglu(g, u):
    return (g * jax.nn.sigmoid(g) * u,)


def f_glu(y, w, b):
    h = jax.nn.gelu(y)
    return (h * jax.nn.sigmoid(mm(h, w) + b),)


def f_xattn(q, kv, qg, kg):
    outs = []
    for h in range(XA_H):
        sl = slice(h * XA_D, (h + 1) * XA_D)
        qn = _rms(q[:, sl], qg)
        kn = _rms(kv[:, sl], kg)
        vh = kv[:, D + h * XA_D:D + (h + 1) * XA_D]
        p = _softmax(mm_nt(qn, kn) * (XA_D ** -0.5))
        outs.append(mm(p, vh))
    return (jnp.concatenate(outs, axis=-1),)


def f_adam(w, g, m, v):
    m2 = B1 * m + (1.0 - B1) * g
    v2 = B2 * v + (1.0 - B2) * (g * g)
    m_hat = m2 / (1.0 - B1 ** STEP)
    v_hat = v2 / (1.0 - B2 ** STEP)
    delta = -LR * (m_hat / (jnp.sqrt(v_hat) + AEPS) + WD * w)
    return delta, m2, v2


S5_NTAB, S5_NROW = 4, 6


def _s5_tables(are, aim, ldt):
    T = CHUNK
    dt = jnp.exp(ldt)
    ar, ai = are * dt, aim * dt
    t = lax.broadcasted_iota(jnp.int32, (T, 1), 0).astype(F32)
    mag, inv = jnp.exp(t * ar), jnp.exp(-t * ar)
    cs, sn = jnp.cos(t * ai), jnp.sin(t * ai)
    e_re, e_im = mag * cs, mag * sn
    n_re, n_im = inv * cs, -inv * sn
    l_re, l_im = jnp.exp(ar) * jnp.cos(ai), jnp.exp(ar) * jnp.sin(ai)
    den = are * are + aim * aim
    k_re = ((l_re - 1.0) * are + l_im * aim) / den
    k_im = (l_im * are - (l_re - 1.0) * aim) / den
    tl = float(T - 1)
    m_re, m_im = jnp.exp(tl * ar) * jnp.cos(tl * ai), jnp.exp(tl * ar) * jnp.sin(tl * ai)
    return (e_re, e_im, n_re, n_im), (l_re, l_im, k_re, k_im, m_re, m_im)


def _s5_chunk(u, sre, sim, tabs, rows, b_re, b_im, c_re, c_im, dv):
    e_re, e_im, n_re, n_im = tabs
    l_re, l_im, k_re, k_im, m_re, m_im = rows
    x_re, x_im = mm(u, b_re), mm(u, b_im)
    bu_re = k_re * x_re - k_im * x_im
    bu_im = k_re * x_im + k_im * x_re
    v_re = bu_re * n_re - bu_im * n_im
    v_im = bu_re * n_im + bu_im * n_re
    p_re = l_re * sre - l_im * sim
    p_im = l_re * sim + l_im * sre
    w_re = cumsum_rows(v_re) + p_re
    w_im = cumsum_rows(v_im) + p_im
    s_re = e_re * w_re - e_im * w_im
    s_im = e_re * w_im + e_im * w_re
    y = mm(s_re, c_re) - mm(s_im, c_im) + dv * u
    z_re = jnp.sum(v_re, axis=0, keepdims=True) + p_re
    z_im = jnp.sum(v_im, axis=0, keepdims=True) + p_im
    return y, m_re * z_re - m_im * z_im, m_re * z_im + m_im * z_re


def _s5_fill_tables(are_ref, aim_ref, ldt_ref, tab, row):
    for g in range(S5_GB):
        ls = slice(512 * g, 512 * (g + 1))
        tabs, rows = _s5_tables(are_ref[:, ls], aim_ref[:, ls], ldt_ref[:, ls])
        for k, t in enumerate(tabs):
            tab[k, :, ls] = t
        for k, r in enumerate(rows):
            row[k:k + 1, ls] = r


def _s5_read_tables(tab, row, ls):
    return (tuple(tab[k, :, ls] for k in range(S5_NTAB)), tuple(row[k:k + 1, ls] for k in range(S5_NROW)))


def _s5_specs(nc, rev):
    T = CHUNK

    def ci(c):
        return nc - 1 - c if rev else c

    u_spec = pl.BlockSpec((T, S5_W), lambda c: (ci(c), 0))
    p_spec = pl.BlockSpec((1, S5_G * S5_P), lambda c: (0, 0))
    b_spec = pl.BlockSpec((S5_GB, 128, 512), lambda c: (0, 0, 0))
    c_spec = pl.BlockSpec((S5_GB, 512, 128), lambda c: (0, 0, 0))
    d_spec = pl.BlockSpec((1, S5_W), lambda c: (0, 0))
    st_spec = pl.BlockSpec((None, 2, S5_G * S5_P), lambda c: (ci(c), 0, 0))
    return u_spec, p_spec, b_spec, c_spec, d_spec, st_spec


def s5_fwd(pm, are, aim, ldt, b_re, b_im, c_re, c_im, dv):
    S = pm.shape[0]
    nc = S // CHUNK
    u_spec, p_spec, b_spec, c_spec, d_spec, st_spec = _s5_specs(nc, False)

    def body(u_ref, are_ref, aim_ref, ldt_ref, bre_ref, bim_ref, cre_ref, cim_ref, dv_ref,
             y_ref, st_ref, state, tab, row):
        c = pl.program_id(0)

        @pl.when(c == 0)
        def _():
            state[...] = jnp.zeros_like(state)
            _s5_fill_tables(are_ref, aim_ref, ldt_ref, tab, row)

        st_ref[...] = state[...]
        for g in range(S5_GB):
            lu, ls = slice(128 * g, 128 * (g + 1)), slice(512 * g, 512 * (g + 1))
            tabs, rows = _s5_read_tables(tab, row, ls)
            y, e_re, e_im = _s5_chunk(u_ref[:, lu], state[0:1, ls], state[1:2, ls], tabs, rows,
                                      bre_ref[g], bim_ref[g], cre_ref[g], cim_ref[g], dv_ref[:, lu])
            y_ref[:, lu] = y
            state[0:1, ls] = e_re
            state[1:2, ls] = e_im

    n_state = S5_G * S5_P
    return pl.pallas_call(
        body, name="s5_fwd", grid=(nc,),
        in_specs=[u_spec, p_spec, p_spec, p_spec, b_spec, b_spec, c_spec, c_spec, d_spec],
        out_specs=[u_spec, st_spec],
        out_shape=[jax.ShapeDtypeStruct((S, S5_W), F32), jax.ShapeDtypeStruct((nc, 2, n_state), F32)],
        scratch_shapes=[pltpu.VMEM((2, n_state), F32), pltpu.VMEM((S5_NTAB, CHUNK, n_state), F32),
                        pltpu.VMEM((8, n_state), F32)],
        compiler_params=_cp("arbitrary"),
    )(pm, are, aim, ldt, b_re, b_im, c_re, c_im, dv)


def s5_bwd(pm, st, dy, are, aim, ldt, b_re, b_im, c_re, c_im, dv):
    S = pm.shape[0]
    nc = S // CHUNK
    u_spec, p_spec, b_spec, c_spec, d_spec, st_spec = _s5_specs(nc, True)

    def body(u_ref, st_ref, dy_ref, are_ref, aim_ref, ldt_ref, bre_ref, bim_ref, cre_ref, cim_ref, dv_ref,
             du_ref, dare_ref, daim_ref, dldt_ref, dbre_ref, dbim_ref, dcre_ref, dcim_ref, ddv_ref,
             dstate, tab, row, dtab, drow):
        c = pl.program_id(0)

        @pl.when(c == 0)
        def _():
            dstate[...] = jnp.zeros_like(dstate)
            dtab[...] = jnp.zeros_like(dtab)
            drow[...] = jnp.zeros_like(drow)
            _s5_fill_tables(are_ref, aim_ref, ldt_ref, tab, row)

        for g in range(S5_GB):
            lu, ls = slice(128 * g, 128 * (g + 1)), slice(512 * g, 512 * (g + 1))
            every = slice(None)
            tabs, rows = _s5_read_tables(tab, row, ls)
            args = (u_ref[:, lu], st_ref[0:1, ls], st_ref[1:2, ls], tabs, rows,
                    bre_ref[g], bim_ref[g], cre_ref[g], cim_ref[g], dv_ref[:, lu])
            _, vjp = jax.vjp(_s5_chunk, *args)
            gr = vjp((dy_ref[:, lu], dstate[0:1, ls], dstate[1:2, ls]))
            du_ref[:, lu] = gr[0]
            dstate[0:1, ls] = gr[1]
            dstate[1:2, ls] = gr[2]
            for k, t in enumerate(gr[3]):
                dtab[k, :, ls] += t
            for k, r in enumerate(gr[4]):
                drow[k:k + 1, ls] += r
            accs = ((dbre_ref, (g,)), (dbim_ref, (g,)), (dcre_ref, (g,)), (dcim_ref, (g,)), (ddv_ref, (every, lu)))
            for (o, idx), gv in zip(accs, gr[5:]):
                @pl.when(c == 0)
                def _(o=o, idx=idx, gv=gv):
                    o[idx] = gv

                @pl.when(c > 0)
                def _(o=o, idx=idx, gv=gv):
                    o[idx] += gv

        @pl.when(c == nc - 1)
        def _():
            for g in range(S5_GB):
                ls = slice(512 * g, 512 * (g + 1))
                _, vjp = jax.vjp(_s5_tables, are_ref[:, ls], aim_ref[:, ls], ldt_ref[:, ls])
                dtabs, drows = _s5_read_tables(dtab, drow, ls)
                ga, gi, gl = vjp((dtabs, drows))
                dare_ref[:, ls] = ga
                daim_ref[:, ls] = gi
                dldt_ref[:, ls] = gl

    n_state = S5_G * S5_P
    return pl.pallas_call(
        body, name="s5_bwd", grid=(nc,),
        in_specs=[u_spec, st_spec, u_spec, p_spec, p_spec, p_spec, b_spec, b_spec, c_spec, c_spec, d_spec],
        out_specs=[u_spec, p_spec, p_spec, p_spec, b_spec, b_spec, c_spec, c_spec, d_spec],
        out_shape=[jax.ShapeDtypeStruct((S, S5_W), F32)] + [jax.ShapeDtypeStruct((1, n_state), F32)] * 3
        + [jax.ShapeDtypeStruct((S5_GB, 128, 512), F32)] * 2 + [jax.ShapeDtypeStruct((S5_GB, 512, 128), F32)] * 2
        + [jax.ShapeDtypeStruct((1, S5_W), F32)],
        scratch_shapes=[pltpu.VMEM((2, n_state), F32), pltpu.VMEM((S5_NTAB, CHUNK, n_state), F32),
                        pltpu.VMEM((8, n_state), F32), pltpu.VMEM((S5_NTAB, CHUNK, n_state), F32),
                        pltpu.VMEM((8, n_state), F32)],
        compiler_params=_cp("arbitrary"),
    )(pm, st, dy, are, aim, ldt, b_re, b_im, c_re, c_im, dv)


def conv_fwd(pm, w):
    S = pm.shape[0]

    def body(x_ref, w_ref, o_ref, pad):
        x = x_ref[...]
        pad[0:8, :] = jnp.zeros((8, 128), F32)
        pad[8:, :] = x
        y = (w_ref[3:4, :] * x + w_ref[2:3, :] * pad[7:7 + S, :] + w_ref[1:2, :] * pad[6:6 + S, :]
             + w_ref[0:1, :] * pad[5:5 + S, :])
        o_ref[...] = y * jax.nn.sigmoid(y)

    return pl.pallas_call(
        body, name="conv_fwd", grid=(12,),
        in_specs=[pl.BlockSpec((S, 128), lambda j: (0, 4 + j)), pl.BlockSpec((4, 128), lambda j: (0, j))],
        out_specs=pl.BlockSpec((S, 128), lambda j: (0, j)),
        out_shape=jax.ShapeDtypeStruct((S, 1536), F32),
        scratch_shapes=[pltpu.VMEM((S + 8, 128), F32)],
        compiler_params=_cp("parallel"),
    )(pm, w)


def conv_bwd(pm, w, dout):
    S = pm.shape[0]

    def body(x_ref, w_ref, do_ref, dx_ref, dw_ref, pad, dpad):
        x = x_ref[...]
        pad[0:8, :] = jnp.zeros((8, 128), F32)
        pad[8:, :] = x
        xs = [pad[5:5 + S, :], pad[6:6 + S, :], pad[7:7 + S, :], x]
        y = w_ref[0:1, :] * xs[0] + w_ref[1:2, :] * xs[1] + w_ref[2:3, :] * xs[2] + w_ref[3:4, :] * xs[3]
        sg = jax.nn.sigmoid(y)
        dy = do_ref[...] * (sg + y * sg * (1.0 - sg))
        dpad[0:S, :] = dy
        dpad[S:, :] = jnp.zeros((8, 128), F32)
        dx_ref[...] = (w_ref[3:4, :] * dy + w_ref[2:3, :] * dpad[1:1 + S, :] + w_ref[1:2, :] * dpad[2:2 + S, :]
                       + w_ref[0:1, :] * dpad[3:3 + S, :])
        for i in range(4):
            dw_ref[i:i + 1, :] = jnp.sum(dy * xs[i], axis=0, keepdims=True)

    return pl.pallas_call(
        body, name="conv_bwd", grid=(12,),
        in_specs=[pl.BlockSpec((S, 128), lambda j: (0, 4 + j)), pl.BlockSpec((4, 128), lambda j: (0, j)),
                  pl.BlockSpec((S, 128), lambda j: (0, j))],
        out_specs=[pl.BlockSpec((S, 128), lambda j: (0, j)), pl.BlockSpec((4, 128), lambda j: (0, j))],
        out_shape=[jax.ShapeDtypeStruct((S, 1536), F32), jax.ShapeDtypeStruct((4, 1536), F32)],
        scratch_shapes=[pltpu.VMEM((S + 8, 128), F32), pltpu.VMEM((S + 8, 128), F32)],
        compiler_params=_cp("parallel"),
    )(pm, w, dout)


GDN_SUP = 4
GDN_ROWS = GDN_SUP * CHUNK


@jax.custom_vjp
def _saved_inverse(a, x):
    return x


_saved_inverse.defvjp(lambda a, x: (x, x),
                      lambda x, g: (-mm_tn(x, mm_nt(g, x)), jnp.zeros_like(x)))


def _gdn_chunk(q, k, v, gate, al, bl, alog, dtb, og, state, inv=None, want_inv=False):
    R = q.shape[0]
    r = lax.broadcasted_iota(jnp.int32, (R, R), 0)
    c = lax.broadcasted_iota(jnp.int32, (R, R), 1)
    same = (r // CHUNK) == (c // CHUNK)
    eye = (r == c).astype(F32)
    strict, causal, upper = same & (r > c), same & (r >= c), same & (r <= c)
    qn = q * lax.rsqrt(jnp.sum(q * q, axis=-1, keepdims=True) + EPS) * (GDN_D ** -0.5)
    kn = k * lax.rsqrt(jnp.sum(k * k, axis=-1, keepdims=True) + EPS)
    beta = jnp.sum(eye * jax.nn.sigmoid(bl), axis=1, keepdims=True)
    g_row = -jnp.exp(alog) * _softplus(al + dtb)
    g = jnp.sum(eye * g_row, axis=1, keepdims=True)
    gc_col = jnp.sum(jnp.where(causal, g_row, 0.0), axis=1, keepdims=True)
    gc_row = jnp.sum(jnp.where(upper, g, 0.0), axis=0, keepdims=True)
    gtot = jnp.sum(jnp.where(same, g_row, 0.0), axis=1, keepdims=True)
    gamma = jnp.exp(gc_col)
    diff = gc_col - gc_row
    d_strict = jnp.where(strict, jnp.exp(jnp.where(strict, diff, 0.0)), 0.0)
    d_causal = jnp.where(causal, jnp.exp(jnp.where(causal, diff, 0.0)), 0.0)
    a = beta * mm_nt(kn, kn) * d_strict
    if inv is None:
        p = -a
        x = eye + p
        for _ in range(5):
            p = mm(p, p)
            x = x + mm(x, p)
    else:
        x = _saved_inverse(a, inv)
    u_new = mm(x, beta * v)
    w_k = mm(x, (beta * gamma) * kn)
    qk = mm_nt(qn, kn) * d_causal
    q_g = qn * gamma
    k_tail = kn * jnp.exp(gtot - gc_col)
    ws, os_ = [], []
    for i in range(R // CHUNK):
        rows = slice(CHUNK * i, CHUNK * (i + 1))
        w_i = u_new[rows] - mm(w_k[rows], state)
        os_.append(mm(q_g[rows], state))
        decay = jnp.exp(jnp.sum(g[rows], axis=0, keepdims=True))
        state = decay * state + mm_tn(k_tail[rows], w_i)
        ws.append(w_i)
    o = jnp.concatenate(os_, axis=0) + mm(qk, jnp.concatenate(ws, axis=0))
    out = _rms(o, og) * (gate * jax.nn.sigmoid(gate))
    return (out, state, x) if want_inv else (out, state)


def _gdn_specs(nc, rev):
    def ci(c):
        return nc - 1 - c if rev else c

    def blk(cb):
        return pl.BlockSpec((GDN_ROWS, 512), lambda c: (ci(c), cb))

    col = lambda n: pl.BlockSpec((n, GDN_ROWS), lambda c: (0, ci(c)))
    sc = pl.BlockSpec((GDN_H, 1, 1), lambda c: (0, 0, 0))
    og = pl.BlockSpec((1, 128), lambda c: (0, 0))
    st = pl.BlockSpec((GDN_H, None, 128, 128), lambda c: (0, ci(c), 0, 0))
    return blk, col, sc, og, st


def gdn_fwd(qkvc, pm, abt, alog, dtb, og):
    S = qkvc.shape[0]
    nc = S // GDN_ROWS
    blk, col, sc, ogs, st = _gdn_specs(nc, False)

    def body(q_ref, k_ref, v_ref, gate_ref, ab_ref, alog_ref, dtb_ref, og_ref, o_ref, st_ref, inv_ref, state):
        c = pl.program_id(0)

        @pl.when(c == 0)
        def _():
            state[...] = jnp.zeros_like(state)

        st_ref[...] = state[...]
        for h in range(GDN_H):
            sl = slice(GDN_D * h, GDN_D * (h + 1))
            out, new_state, inv = _gdn_chunk(
                q_ref[:, sl], k_ref[:, sl], v_ref[:, sl], gate_ref[:, sl], ab_ref[h:h + 1, :], ab_ref[GDN_H + h:GDN_H + h + 1, :],
                alog_ref[h], dtb_ref[h], og_ref[...], state[h], want_inv=True)
            o_ref[:, sl] = out
            state[h] = new_state
            inv_ref[h] = inv

    inv_spec = pl.BlockSpec((GDN_H, None, GDN_ROWS, GDN_ROWS), lambda c: (0, c, 0, 0))
    return pl.pallas_call(
        body, name="gdn_fwd", grid=(nc,),
        in_specs=[blk(0), blk(1), blk(2), blk(4), col(2 * GDN_H), sc, sc, ogs],
        out_specs=[blk(0), st, inv_spec],
        out_shape=[jax.ShapeDtypeStruct((S, 512), F32), jax.ShapeDtypeStruct((GDN_H, nc, 128, 128), F32),
                   jax.ShapeDtypeStruct((GDN_H, nc, GDN_ROWS, GDN_ROWS), F32)],
        scratch_shapes=[pltpu.VMEM((GDN_H, 128, 128), F32)],
        compiler_params=_cp("arbitrary"),
    )(qkvc, qkvc, qkvc, pm, abt, alog, dtb, og)


def gdn_bwd(qkvc, pm, abt, alog, dtb, og, st, inv, dout):
    S = qkvc.shape[0]
    nc = S // GDN_ROWS
    blk, col, sc, ogs, sts = _gdn_specs(nc, True)

    def body(q_ref, k_ref, v_ref, gate_ref, ab_ref, alog_ref, dtb_ref, og_ref, st_ref, inv_ref, do_ref,
             dq_ref, dk_ref, dv_ref, dgate_ref, dal_ref, dbl_ref, dalog_ref, ddtb_ref, dog_ref, dstate):
        c = pl.program_id(0)

        @pl.when(c == 0)
        def _():
            dstate[...] = jnp.zeros_like(dstate)

        for h in range(GDN_H):
            sl = slice(GDN_D * h, GDN_D * (h + 1))
            args = (q_ref[:, sl], k_ref[:, sl], v_ref[:, sl], gate_ref[:, sl], ab_ref[h:h + 1, :], ab_ref[GDN_H + h:GDN_H + h + 1, :],
                    alog_ref[h], dtb_ref[h], og_ref[...], st_ref[h])
            inv_h = inv_ref[h]
            _, vjp = jax.vjp(lambda *a, inv_h=inv_h: _gdn_chunk(*a, inv=inv_h), *args)
            g = vjp((do_ref[:, sl], dstate[h]))
            for o, gv in zip((dq_ref, dk_ref, dv_ref, dgate_ref), g[:4]):
                o[:, sl] = gv
            dal_ref[h:h + 1, :] = g[4]
            dbl_ref[h:h + 1, :] = g[5]
            dstate[h] = g[9]
            for o, gv in zip((dalog_ref, ddtb_ref, dog_ref), g[6:9]):
                @pl.when(c == 0)
                def _(o=o, gv=gv, h=h):
                    o[h] = gv

                @pl.when(c > 0)
                def _(o=o, gv=gv, h=h):
                    o[h] += gv

    ogo = pl.BlockSpec((GDN_H, 1, 128), lambda c: (0, 0, 0))
    inv_spec = pl.BlockSpec((GDN_H, None, GDN_ROWS, GDN_ROWS), lambda c: (0, nc - 1 - c, 0, 0))
    sd = jax.ShapeDtypeStruct
    return pl.pallas_call(
        body, name="gdn_bwd", grid=(nc,),
        in_specs=[blk(0), blk(1), blk(2), blk(4), col(2 * GDN_H), sc, sc, ogs, sts, inv_spec, blk(0)],
        out_specs=[blk(0), blk(0), blk(0), blk(0), col(GDN_H), col(GDN_H), sc, sc, ogo],
        out_shape=[sd((S, 512), F32)] * 4 + [sd((GDN_H, S), F32)] * 2 + [sd((GDN_H, 1, 1), F32)] * 2
        + [sd((GDN_H, 1, 128), F32)],
        scratch_shapes=[pltpu.VMEM((GDN_H, 128, 128), F32)],
        compiler_params=_cp("arbitrary"),
    )(qkvc, qkvc, qkvc, pm, abt, alog, dtb, og, st, inv, dout)


HG = 8
HG_LANES = HG * CA_D


def _group_mean_raw(y):
    r = lax.broadcasted_iota(jnp.int32, (128, 128), 0)
    c = lax.broadcasted_iota(jnp.int32, (128, 128), 1)
    g = jnp.where((r // CA_D) == (c // CA_D), 1.0 / CA_D, 0.0).astype(BF16)
    d = lambda u: lax.dot_general(u, g, (((1,), (0,)), ((), ())), preferred_element_type=F32)
    outs = []
    for j in range(y.shape[1] // 128):
        hi, lo = _split(y[:, 128 * j:128 * (j + 1)])
        outs.append(d(hi) + d(lo))
    return jnp.concatenate(outs, axis=1)


@jax.custom_vjp
def group_mean(y):
    return _group_mean_raw(y)


group_mean.defvjp(lambda y: (_group_mean_raw(y), None), lambda _, g: (_group_mean_raw(g),))


def f_headnorm(t, g):
    return (t * lax.rsqrt(group_mean(t * t) + EPS) * g,)


def _cattn_chunk(q, kb, vb, bias, valid):
    lane = lax.broadcasted_iota(jnp.int32, (1, 128), 1)
    m0 = (lane < CA_D).astype(F32)
    m1 = 1.0 - m0
    pairs = range(HG // 2)
    sl = [slice(128 * p, 128 * (p + 1)) for p in pairs]
    q2 = [jnp.concatenate([q[:, s] * m0, q[:, s] * m1], axis=0) for s in sl]
    sc = [mm_nt(q2[p], kb[:, sl[p]]) * (CA_D ** -0.5) + bias[sl[p]] for p in pairs]
    pr = [_softmax(jnp.where(valid, s, -1e30)) for s in sc]
    o2 = [mm(pr[p], vb[:, sl[p]]) for p in pairs]
    return jnp.concatenate([o[:CHUNK] * m0 + o[CHUNK:] * m1 for o in o2], axis=1)


def _cattn_valid(c):
    pos = lax.broadcasted_iota(jnp.int32, (1, CA_BAND), 1) + c * CHUNK
    return pos >= CA_PAD


def _cattn_specs(S):
    q_spec = pl.BlockSpec((CHUNK, HG_LANES), lambda h, c: (c, h))
    kv_spec = pl.BlockSpec((S + CA_PAD, HG_LANES), lambda h, c: (0, h))
    b_spec = pl.BlockSpec((HG * CHUNK, CA_BAND), lambda h, c: (h, 0))
    return q_spec, kv_spec, b_spec


def cattn_fwd(qn, kp, vp, bias):
    S = qn.shape[0]
    nc = S // CHUNK
    q_spec, kv_spec, b_spec = _cattn_specs(S)

    def body(q_ref, k_ref, v_ref, b_ref, o_ref):
        c = pl.program_id(1)
        start = pl.multiple_of(c * CHUNK, CHUNK)
        kb = k_ref[pl.ds(start, CA_BAND), :]
        vb = v_ref[pl.ds(start, CA_BAND), :]
        o_ref[...] = _cattn_chunk(q_ref[...], kb, vb, b_ref[...], _cattn_valid(c)).astype(o_ref.dtype)

    return pl.pallas_call(
        body, name="cattn_fwd", grid=(CA_H // HG, nc), in_specs=[q_spec, kv_spec, kv_spec, b_spec],
        out_specs=q_spec, out_shape=jax.ShapeDtypeStruct((S, D), BF16),
        compiler_params=_cp("parallel", "arbitrary"),
    )(qn, kp, vp, bias)


def kv_prep(qkv, kg):
    S = qkv.shape[0]
    tile = ROW_TILE
    lead = CA_PAD // tile

    def body(k_ref, v_ref, g_ref, kp_ref, vp_ref):
        i = pl.program_id(0)

        @pl.when(i < lead)
        def _():
            kp_ref[...] = jnp.zeros_like(kp_ref)
            vp_ref[...] = jnp.zeros_like(vp_ref)

        @pl.when(i >= lead)
        def _():
            kp_ref[...] = f_headnorm(k_ref[...], g_ref[...])[0].astype(BF16)
            vp_ref[...] = v_ref[...].astype(BF16)

    src = lambda cb: pl.BlockSpec((tile, D), lambda i, cb=cb: (jnp.maximum(i - lead, 0), cb))
    out = pl.BlockSpec((tile, D), lambda i: (i, 0))
    return pl.pallas_call(
        body, name="kv_prep", grid=((S + CA_PAD) // tile,),
        in_specs=[src(1), src(2), pl.BlockSpec((1, D), lambda i: (0, 0))], out_specs=[out, out],
        out_shape=[jax.ShapeDtypeStruct((S + CA_PAD, D), BF16)] * 2,
        compiler_params=_cp("parallel"),
    )(qkv, qkv, kg)


def cattn_bwd(qn, kp, vp, bias, do):
    S = qn.shape[0]
    nc = S // CHUNK
    q_spec, kv_spec, b_spec = _cattn_specs(S)

    def body(q_ref, k_ref, v_ref, b_ref, do_ref, dq_ref, dk_ref, dv_ref, db_ref):
        c = pl.program_id(1)

        @pl.when(c == 0)
        def _():
            dk_ref[...] = jnp.zeros_like(dk_ref)
            dv_ref[...] = jnp.zeros_like(dv_ref)
            db_ref[...] = jnp.zeros_like(db_ref)

        start = pl.multiple_of(c * CHUNK, CHUNK)
        kb = k_ref[pl.ds(start, CA_BAND), :].astype(F32)
        vb = v_ref[pl.ds(start, CA_BAND), :].astype(F32)
        valid = _cattn_valid(c)
        _, vjp = jax.vjp(lambda q, k, v, b: _cattn_chunk(q, k, v, b, valid), q_ref[...], kb, vb, b_ref[...])
        dq, dk, dv, db = vjp(do_ref[...])
        dq_ref[...] = dq
        dk_ref[pl.ds(start, CA_BAND), :] += dk
        dv_ref[pl.ds(start, CA_BAND), :] += dv
        db_ref[...] += db

    sd = jax.ShapeDtypeStruct
    return pl.pallas_call(
        body, name="cattn_bwd", grid=(CA_H // HG, nc), in_specs=[q_spec, kv_spec, kv_spec, b_spec, q_spec],
        out_specs=[q_spec, kv_spec, kv_spec, b_spec],
        out_shape=[sd((S, D), F32), sd((S + CA_PAD, D), F32), sd((S + CA_PAD, D), F32),
                   sd((CA_H * CHUNK, CA_BAND), F32)],
        compiler_params=_cp("parallel", "arbitrary"),
    )(qn, kp, vp, bias, do)


SKEW_W = CA_BAND + CHUNK


def rel_bias_grad(dbias):
    padded = jnp.pad(dbias, ((0, 0), (0, 0), (CHUNK, 0)))
    flat = jnp.pad(padded.reshape(CA_H, CHUNK * SKEW_W), ((0, 0), (0, CHUNK)))
    skew = flat.reshape(CA_H, CHUNK, SKEW_W + 1)

    first_near = SKEW_W - CHUNK - MAX_REL

    def fn(t):
        colsum = jnp.sum(t, axis=1, keepdims=True)
        j = lax.broadcasted_iota(jnp.int32, colsum.shape, 2)
        far = jnp.sum(jnp.where(j < first_near, colsum, 0.0), axis=2, keepdims=True)
        return (colsum + jnp.where(j == first_near, far, 0.0),)

    (colsum,) = whole("relbias_sum", fn, [skew], [((CA_H, 1, SKEW_W + 1), F32)])
    near = colsum[:, 0, first_near:SKEW_W][:, ::-1]
    return jnp.concatenate([jnp.zeros((CA_H, CHUNK + 1), F32), near], axis=1)


def rel_bias_expand(rb):
    near = rb[:, CHUNK + 1:][:, ::-1]
    far = jnp.broadcast_to(rb[:, 2 * MAX_REL:], (CA_H, SKEW_W - CHUNK - MAX_REL))
    t = jnp.concatenate([far, near, jnp.zeros((CA_H, 1), rb.dtype)], axis=1)
    rows = jnp.tile(t, (1, CHUNK))[:, :CHUNK * SKEW_W].reshape(CA_H, CHUNK, SKEW_W)
    return rows[:, :, CHUNK:]


def loss_head(y, target):
    S = y.shape[0]
    tile = min(ROW_TILE, S)

    def body(y_ref, t_ref, dy_ref, acc_ref):
        i = pl.program_id(0)
        e = y_ref[...] - t_ref[...]
        dy_ref[...] = e * (1.0 / D)
        part = jnp.sum(e * e, axis=0, keepdims=True) * (0.5 / D)

        @pl.when(i == 0)
        def _():
            acc_ref[...] = part

        @pl.when(i > 0)
        def _():
            acc_ref[...] += part

    row = pl.BlockSpec((tile, D), lambda i: (i, 0))
    return pl.pallas_call(
        body, name="loss_head", grid=(S // tile,), in_specs=[row, row],
        out_specs=[row, pl.BlockSpec((1, D), lambda i: (0, 0))],
        out_shape=[jax.ShapeDtypeStruct((S, D), F32), jax.ShapeDtypeStruct((1, D), F32)],
        compiler_params=_cp("arbitrary"),
    )(y, target)


ANY = pl.BlockSpec(memory_space=pl.ANY)


HBM = pl.BlockSpec(memory_space=pltpu.HBM)
SEM = pl.BlockSpec(memory_space=pltpu.SEMAPHORE)
EFFECT = pltpu.SideEffectType.DATAFLOW_SIDE_EFFECTING


def _chip_copies(src_ref, land_ref, send_sems, recv_sems, mode):
    x, y, c = lax.axis_index("x"), lax.axis_index("y"), lax.axis_index("c")
    me = 2 * x + y

    def copy(j, s, d, dev):
        return pltpu.make_async_remote_copy(src_ref=s, dst_ref=d, send_sem=send_sems.at[j], recv_sem=recv_sems.at[j],
                                            device_id=dev, device_id_type=MESH)

    if mode == "sibling":
        cp = copy(0, src_ref, land_ref, (x, y, 1 - c))
        return [(cp, cp)]
    out = []
    for j, (px, py) in enumerate([(1 - x, y), (x, 1 - y), (1 - x, 1 - y)]):
        peer = 2 * px + py
        if mode == "scatter":
            pairs, dev = ((src_ref.at[peer], land_ref.at[j]), (src_ref.at[me], land_ref.at[j])), (px, py, c)
        else:
            half = land_ref.shape[1] // 2
            mine = pl.ds(pl.multiple_of(c * half, 16), half)
            theirs = pl.ds(pl.multiple_of((1 - c) * half, 16), half)
            if mode == "gather_half":
                pairs = ((src_ref.at[mine], land_ref.at[me, mine]), (src_ref.at[mine], land_ref.at[peer, mine]))
                dev = (px, py, c)
            else:
                pairs = ((land_ref.at[peer, mine],) * 2, (land_ref.at[peer, theirs],) * 2)
                dev = (x, y, 1 - c)
        out.append(tuple(copy(j, s, d, dev) for s, d in pairs))
    return out


def exchange_start(src, land, carry, name, mode):
    def body(src_ref, land_ref, carry_ref, send_sems, recv_sems, src_out, land_out, carry_out):
        for send, _ in _chip_copies(src_ref, land_ref, send_sems, recv_sems, mode):
            send.start()

    hbm = lambda a: pltpu.HBM(a.shape, a.dtype)
    n = 1 if mode == "sibling" else 3
    return pl.pallas_call(
        body, name=name,
        out_shape=(pltpu.SemaphoreType.DMA((n,)), pltpu.SemaphoreType.DMA((n,)), hbm(src), hbm(land), hbm(carry)),
        in_specs=(HBM, HBM, HBM), out_specs=(SEM, SEM, HBM, HBM, HBM),
        input_output_aliases={0: 2, 1: 3, 2: 4},
        compiler_params=pltpu.CompilerParams(has_side_effects=EFFECT),
    )(pltpu.with_memory_space_constraint(src, pltpu.HBM), pltpu.with_memory_space_constraint(land, pltpu.HBM),
      pltpu.with_memory_space_constraint(carry, pltpu.HBM))


def exchange_wait(src, land, send_sems, recv_sems, after, name, mode):
    def body(src_ref, land_ref, send_sems_ref, recv_sems_ref, after_ref, src_out, land_out):
        for send, recv in _chip_copies(src_ref, land_ref, send_sems_ref, recv_sems_ref, mode):
            send.wait_send()
            recv.wait_recv()

    hbm = lambda a: pltpu.HBM(a.shape, a.dtype)
    return pl.pallas_call(
        body, name=name, out_shape=(hbm(src), hbm(land)),
        in_specs=(HBM, HBM, SEM, SEM, ANY), out_specs=(HBM, HBM), input_output_aliases={0: 0, 1: 1},
        compiler_params=pltpu.CompilerParams(has_side_effects=EFFECT),
    )(src, land, send_sems, recv_sems, after)


def sibling_exchange(srcs, name):
    n = len(srcs)

    def body(*refs):
        src_refs, out_refs, send_sems, recv_sems = refs[:n], refs[n:2 * n], refs[2 * n], refs[2 * n + 1]
        x, y, c = lax.axis_index("x"), lax.axis_index("y"), lax.axis_index("c")
        copies = [pltpu.make_async_remote_copy(src_ref=s, dst_ref=o, send_sem=send_sems.at[k], recv_sem=recv_sems.at[k],
                                               device_id=(x, y, 1 - c), device_id_type=MESH)
                  for k, (s, o) in enumerate(zip(src_refs, out_refs))]
        for cp in copies:
            cp.start()
        for cp in copies:
            cp.wait()

    return pl.pallas_call(
        body, name=name, in_specs=[ANY] * n, out_specs=[ANY] * n,
        out_shape=[jax.ShapeDtypeStruct(s.shape, s.dtype) for s in srcs],
        scratch_shapes=[pltpu.SemaphoreType.DMA((n,)), pltpu.SemaphoreType.DMA((n,))],
    )(*srcs)


def chip_gather(src, name):
    def body(src_ref, out_ref, send_sems, recv_sems, local_sem):
        x, y, c = lax.axis_index("x"), lax.axis_index("y"), lax.axis_index("c")
        me = 2 * x + y
        local = pltpu.make_async_copy(src_ref, out_ref.at[me], local_sem)
        local.start()
        peers = [(1 - x, y), (x, 1 - y), (1 - x, 1 - y)]
        mk = lambda j, px, py, slab: pltpu.make_async_remote_copy(
            src_ref=src_ref, dst_ref=out_ref.at[slab], send_sem=send_sems.at[j], recv_sem=recv_sems.at[j],
            device_id=(px, py, c), device_id_type=MESH)
        sends = [mk(j, px, py, me) for j, (px, py) in enumerate(peers)]
        for cp in sends:
            cp.start()
        for j, (px, py) in enumerate(peers):
            mk(j, px, py, 2 * px + py).wait_recv()
        for cp in sends:
            cp.wait_send()
        local.wait()

    return pl.pallas_call(
        body, name=name, in_specs=[ANY], out_specs=ANY,
        out_shape=jax.ShapeDtypeStruct((N_CHIPS,) + tuple(src.shape), src.dtype),
        scratch_shapes=[pltpu.SemaphoreType.DMA((3,)), pltpu.SemaphoreType.DMA((3,)), pltpu.SemaphoreType.DMA],
    )(src)


def sum_own_slabs(src, chip, land, name, tile=512):
    _, R, C = src.shape
    n = land.shape[0]
    tile = _tile(R, tile, 16)

    def body(chip_ref, o_ref, t_ref, out_ref):
        acc = o_ref[...].astype(F32)
        for s in range(n):
            acc = acc + t_ref[s].astype(F32)
        out_ref[...] = acc

    return pl.pallas_call(
        body, name=name,
        grid_spec=pltpu.PrefetchScalarGridSpec(
            num_scalar_prefetch=1, grid=(R // tile,),
            in_specs=[pl.BlockSpec((None, tile, C), lambda i, chip_ref: (chip_ref[0], i, 0)),
                      pl.BlockSpec((n, tile, C), lambda i, chip_ref: (0, i, 0))],
            out_specs=pl.BlockSpec((tile, C), lambda i, chip_ref: (i, 0))),
        out_shape=jax.ShapeDtypeStruct((R, C), F32),
        compiler_params=_cp("parallel"),
    )(jnp.reshape(chip, (1,)).astype(jnp.int32), src, land)


def sum_slabs(t, name, tile=512):
    n, R, C = t.shape
    tile = _tile(R, tile, 16)

    def body(t_ref, o_ref):
        acc = t_ref[0].astype(F32)
        for s in range(1, n):
            acc = acc + t_ref[s].astype(F32)
        o_ref[...] = acc

    return pl.pallas_call(
        body, name=name, grid=(R // tile,), in_specs=[pl.BlockSpec((n, tile, C), lambda i: (0, i, 0))],
        out_specs=pl.BlockSpec((tile, C), lambda i: (i, 0)), out_shape=jax.ShapeDtypeStruct((R, C), F32),
        compiler_params=_cp("parallel"),
    )(t)


def _pad_rows(a, mult=16):
    r = (-a.shape[0]) % mult
    return jnp.pad(a, ((0, r), (0, 0))) if r else a


BIG = ["ab_w_in", "c_w_qkv", "xa_w_kv", "f_w_gate", "f_w_up", "ab_w_out", "c_w_out", "xa_w_q", "xa_w_out",
       "f_w_down", "s5_w_glu"]


GROUPS = ("B", "A")
GROUP_ROW_MULT = 64


def group_spec(layer, grp):
    i = layer // 2
    if grp == "A":
        return [("xa_w_kv", layer, True, 512), ("xa_w_q", layer, False, 256), ("xa_w_out", layer, False, 256),
                ("f_w_gate", layer, True, 704), ("f_w_up", layer, True, 704), ("f_w_down", layer, False, 704)]
    if layer % 2 == 0:
        return [("ab_w_in", i, True, 642), ("ab_w_out", i, False, 256), ("s5_w_glu", i, False, 64)]
    return [("c_w_qkv", i, True, 768), ("c_w_out", i, False, 256)]


def _seg_rows(rows):
    return rows + ((-rows) % 16)


def _f32_rows(a):
    bits = lax.bitcast_convert_type(a.reshape(-1), BF16).reshape(-1)
    return jnp.pad(bits, (0, 16 * D - bits.shape[0])).reshape(16, D)


def pack_group_shards(p, layer, grp):
    segs = []
    for name, idx, transposed, rows in group_spec(layer, grp):
        w = p[name][idx]
        if transposed:
            w = w.T
        segs.append(_pad_rows(w.astype(BF16).reshape(-1, D)))
    if grp == "B":
        small = p["gdn_conv_w"] if layer % 2 == 0 else p["c_norm_g"]
        segs.append(_f32_rows(small[layer // 2]))
    return _pad_rows(jnp.concatenate(segs, axis=0), GROUP_ROW_MULT)


def unpack_group_gathered(g, layer, grp):
    out, off = {}, 0
    for name, idx, transposed, rows in group_spec(layer, grp):
        seg = g[:, off:off + rows]
        if name == "s5_w_glu":
            out[name] = seg.reshape(N_CHIPS * 128, 512)
        else:
            out[name] = seg.reshape(N_CHIPS * rows, D)
        off += _seg_rows(rows)
    if grp == "B":
        n = 4 * 384 if layer % 2 == 0 else 256
        bits = g[:, off:off + 16].reshape(N_CHIPS, -1)[:, :2 * n].reshape(N_CHIPS, n, 2)
        small = lax.bitcast_convert_type(bits, F32)
        if layer % 2 == 0:
            out["gdn_conv_w"] = jnp.swapaxes(small.reshape(N_CHIPS, 4, 384), 0, 1).reshape(4, 1536)
        else:
            out["c_norm_g"] = small.reshape(D)
    return out


def pack_group_grads(gr, layer, grp):
    segs = []
    for name, idx, transposed, rows in group_spec(layer, grp):
        w = gr[name].astype(BF16)
        seg = w.reshape(N_CHIPS, rows, D)
        r = (-rows) % 16
        if r:
            seg = jnp.pad(seg, ((0, 0), (0, r), (0, 0)))
        segs.append(seg)
    out = jnp.concatenate(segs, axis=1)
    return jnp.pad(out, ((0, 0), (0, (-out.shape[1]) % GROUP_ROW_MULT), (0, 0)))


def unpack_group_reduced(g, layer, grp):
    out, off = {}, 0
    for name, idx, transposed, rows in group_spec(layer, grp):
        seg = g[off:off + rows]
        if name == "s5_w_glu":
            out[name] = seg.reshape(128, 512)
        else:
            out[name] = seg.T if (transposed and name not in ADAM_TRANSPOSED) else seg
        off += _seg_rows(rows)
    return out


ADAM_TRANSPOSED = ("f_w_gate", "f_w_up")


SMALL = ["ab_norm_g", "s5_a_re", "s5_a_im", "s5_log_dt", "s5_b_re", "s5_b_im", "s5_c_re", "s5_c_im", "s5_d",
         "s5_b_glu", "gdn_conv_w", "gdn_a_log", "gdn_dt_bias", "gdn_out_norm_g", "c_norm_g", "c_q_norm_g",
         "c_k_norm_g", "c_rel_bias", "mem_norm_g", "xa_norm_g", "xa_q_norm_g", "xa_k_norm_g", "f_norm_g"]


def _lane_rows(a):
    flat = a.reshape(-1).astype(F32)
    return jnp.pad(flat, (0, (-flat.shape[0]) % 1024)).reshape(-1, 128)


def pack_small(d, extra=None):
    parts = [_lane_rows(d[n]) for n in SMALL]
    if extra is not None:
        parts.append(_lane_rows(extra))
    rows = jnp.concatenate(parts, axis=0)
    return jnp.pad(rows, ((0, (-rows.shape[0]) % 128), (0, 0)))


def unpack_small(rows, shapes):
    out, off = {}, 0
    for n in SMALL:
        sz = int(np.prod(shapes[n]))
        k = 8 * -(-sz // 1024)
        out[n] = rows[off:off + k].reshape(-1)[:sz].reshape(shapes[n])
        off += k
    return out, rows[off:]


def _s5_blockdiag_b(b):
    bt = jnp.swapaxes(b, 1, 2).reshape(S5_GB, 8, S5_C, S5_P)
    eye = jnp.eye(8, dtype=b.dtype)
    return jnp.einsum("bgcp,gh->bgchp", bt, eye).reshape(S5_GB, 8 * S5_C, 8 * S5_P)


def _s5_blockdiag_c(c):
    ct = jnp.swapaxes(c, 1, 2).reshape(S5_GB, 8, S5_P, S5_C)
    eye = jnp.eye(8, dtype=c.dtype)
    return jnp.einsum("bgpc,gh->bgphc", ct, eye).reshape(S5_GB, 8 * S5_P, 8 * S5_C)


def _s5_diag_b(db):
    t = db.reshape(S5_GB, 8, S5_C, 8, S5_P)
    t = jnp.transpose(t, (0, 2, 4, 1, 3)).reshape(S5_GB, S5_C, S5_P, 64)
    d = t[..., ::9]
    return jnp.transpose(d, (0, 3, 2, 1)).reshape(S5_G, S5_P, S5_C)


def _s5_diag_c(dc):
    t = dc.reshape(S5_GB, 8, S5_P, 8, S5_C)
    t = jnp.transpose(t, (0, 2, 4, 1, 3)).reshape(S5_GB, S5_P, S5_C, 64)
    d = t[..., ::9]
    return jnp.transpose(d, (0, 3, 2, 1)).reshape(S5_G, S5_C, S5_P)


def local_step(x, mem, target, p, wsrc, gsink, wpre=lambda layer, grp, carry: carry):
    S = x.shape[0]
    row2 = lambda a: a.reshape(1, -1)
    saved = []
    (mem_n,) = row_fwd("mem_norm", f_norm, [mem], [row2(p["mem_norm_g"])], [(D, BF16)])
    gs = {}

    for layer in range(DEPTH):
        i = layer // 2
        w = wsrc(layer, "B", x)
        sv = {"x0": x, "wB": w}
        if layer % 2 == 0:
            (h,) = row_fwd("norm", f_norm, [x], [row2(p["ab_norm_g"][i])], [(D, BF16)])
            w_in = w["ab_w_in"]
            pm = matmul("nt", h, w_in[:2560], "proj_main")
            abt = matmul("nt", w_in[2560:], h, "proj_ab")
            s5p = dict(
                are=p["s5_a_re"][i].reshape(1, -1), aim=p["s5_a_im"][i].reshape(1, -1),
                ldt=jnp.broadcast_to(p["s5_log_dt"][i][:, None], (S5_G, S5_P)).reshape(1, -1),
                b_re=_s5_blockdiag_b(p["s5_b_re"][i]), b_im=_s5_blockdiag_b(p["s5_b_im"][i]),
                c_re=_s5_blockdiag_c(p["s5_c_re"][i]), c_im=_s5_blockdiag_c(p["s5_c_im"][i]),
                dv=p["s5_d"][i].reshape(1, -1))
            y5, st5 = s5_fwd(pm, **s5p)
            (a_out,) = row_fwd("glu", f_glu, [y5], [w["s5_w_glu"], row2(p["s5_b_glu"][i])], [(S5_W, F32)])
            conv_w = w["gdn_conv_w"]
            qkvc = conv_fwd(pm, conv_w)
            alog = p["gdn_a_log"][i].reshape(GDN_H, 1, 1)
            dtb = p["gdn_dt_bias"][i].reshape(GDN_H, 1, 1)
            og = row2(p["gdn_out_norm_g"][i])
            b_out, stg, inv = gdn_fwd(qkvc, pm, abt, alog, dtb, og)
            cat = wpre(layer, "A", jnp.concatenate([a_out, b_out], axis=1))
            x, hq = matmul("nn", cat, w["ab_w_out"], "mix_out", add=x, norm_out=row2(p["xa_norm_g"][layer]))
            sv.update(h=h, pm=pm, s5p=s5p, y5=y5, st5=st5, qkvc=qkvc, abt=abt, alog=alog, dtb=dtb, og=og,
                      stg=stg, inv=inv, cat=cat, conv_w=conv_w)
        else:
            (h,) = row_fwd("norm", f_norm, [x], [row2(w["c_norm_g"])], [(D, BF16)])
            qkv = matmul("nt", h, w["c_w_qkv"], "proj_qkv")
            qg = jnp.tile(row2(p["c_q_norm_g"][i]), (1, CA_H))
            kg = jnp.tile(row2(p["c_k_norm_g"][i]), (1, CA_H))
            (qn,) = row_fwd("headnorm_q", f_headnorm, [(qkv, D, 0)], [qg], [(D, F32)])
            kp, vp = kv_prep(qkv, kg)
            bias = rel_bias_expand(p["c_rel_bias"][i]).reshape(CA_H * CHUNK, CA_BAND)
            o = wpre(layer, "A", cattn_fwd(qn, kp, vp, bias))
            x, hq = matmul("nn", o, w["c_w_out"], "mix_out", add=x, norm_out=row2(p["xa_norm_g"][layer]))
            sv.update(h=h, qkv=qkv, qg=qg, kg=kg, qn=qn, kp=kp, vp=vp, bias=bias, o=o)
        sv["x1"] = x
        w = wsrc(layer, "A", x)
        sv["wA"] = w
        qx = matmul("nn", hq, w["xa_w_q"], "xa_q")
        kv = matmul("nt", mem_n, w["xa_w_kv"], "xa_kv")
        xqg, xkg = row2(p["xa_q_norm_g"][layer]), row2(p["xa_k_norm_g"][layer])
        (ox,) = row_fwd("xattn", f_xattn, [qx], [kv, xqg, xkg], [(D, BF16)], tile=2 * ROW_TILE)
        x, hf = matmul("nn", ox, w["xa_w_out"], "xa_out", add=x, norm_out=row2(p["f_norm_g"][layer]))
        sv.update(hq=hq, qx=qx, kv=kv, ox=ox)
        sv["x2"] = x
        gate, up, act = ffn_in(hf, w["f_w_gate"], w["f_w_up"])
        if layer + 1 < DEPTH:
            act = wpre(layer + 1, "B", act)
        x = matmul("nn", act, w["f_w_down"], "ffn_down", add=x)
        sv.update(hf=hf, gate=gate, up=up, act=act)
        saved.append(sv)

    dx, loss_vec = loss_head(x, target)

    dmem_n = None
    for layer in reversed(range(DEPTH)):
        i = layer // 2
        sv = saved[layer]
        w, gw = sv["wA"], {}
        dgate, dup = ffn_dact(dx, w["f_w_down"], sv["gate"], sv["up"])
        gw["f_w_down"] = matmul("tn", sv["act"], dx, "ffn_dwd", out_dtype=BF16)
        gw["f_w_gate"] = matmul("tn", dgate, sv["hf"], "ffn_dwg", out_dtype=BF16)
        gw["f_w_up"] = matmul("tn", dup, sv["hf"], "ffn_dwu", out_dtype=BF16)
        dh = matmul("nn", dgate, w["f_w_gate"], "ffn_dhg")
        dx, dg = matmul("nn", dup, w["f_w_up"], "ffn_dhu", add=dh,
                        norm_bwd=(sv["x2"], row2(p["f_norm_g"][layer]), dx))
        gs.setdefault("f_norm_g", [None] * DEPTH)[layer] = dg[0]
        do = matmul("nt", dx, w["xa_w_out"], "xa_do")
        gw["xa_w_out"] = matmul("tn", sv["ox"], dx, "xa_dwo", out_dtype=BF16)
        xqg, xkg = row2(p["xa_q_norm_g"][layer]), row2(p["xa_k_norm_g"][layer])
        dqx, dkv, dqg, dkg = row_bwd("xattn_bwd", f_xattn, [sv["qx"]], [sv["kv"], xqg, xkg], [do],
                                     [0], [0, 1, 2], tile=2 * ROW_TILE)
        gs.setdefault("xa_q_norm_g", [None] * DEPTH)[layer] = dqg[0]
        gs.setdefault("xa_k_norm_g", [None] * DEPTH)[layer] = dkg[0]
        gw["xa_w_q"] = matmul("tn", sv["hq"], dqx, "xa_dwq", out_dtype=BF16)
        gw["xa_w_kv"] = matmul("tn", dkv, mem_n, "xa_dwkv", out_dtype=BF16)
        dx = gsink(layer, "A", gw, dx)
        dmem_n = matmul("nn", dkv, w["xa_w_kv"], "xa_dmem", add=dmem_n)
        dx, dg = matmul("nt", dqx, w["xa_w_q"], "xa_dhq", norm_bwd=(sv["x1"], row2(p["xa_norm_g"][layer]), dx))
        gs.setdefault("xa_norm_g", [None] * DEPTH)[layer] = dg[0]
        w, gw = sv["wB"], {}
        if layer % 2 == 0:
            dcat = matmul("nt", dx, w["ab_w_out"], "mix_dcat")
            gw["ab_w_out"] = matmul("tn", sv["cat"], dx, "mix_dwo", out_dtype=BF16)
            dy5, dwglu, dbglu = row_bwd("glu_bwd", f_glu, [sv["y5"]], [w["s5_w_glu"], row2(p["s5_b_glu"][i])],
                                        [(dcat, S5_W, 0)], [0], [0, 1])
            gw["s5_w_glu"] = dwglu
            gs.setdefault("s5_b_glu", [None] * 2)[i] = dbglu[0]
            s5p = sv["s5p"]
            du, dare, daim, dldt, dbre, dbim, dcre, dcim, ddv = s5_bwd(sv["pm"], sv["st5"], dy5, **s5p)
            (dldt_g,) = whole("s5_dt_sum", lambda t: (jnp.sum(t, axis=1, keepdims=True),),
                              [dldt.reshape(S5_G, S5_P)], [((S5_G, 1), F32)])
            for nme, val in (("s5_a_re", dare.reshape(S5_G, S5_P)), ("s5_a_im", daim.reshape(S5_G, S5_P)),
                             ("s5_log_dt", dldt_g[:, 0]), ("s5_b_re", _s5_diag_b(dbre)),
                             ("s5_b_im", _s5_diag_b(dbim)), ("s5_c_re", _s5_diag_c(dcre)),
                             ("s5_c_im", _s5_diag_c(dcim)), ("s5_d", ddv.reshape(S5_G, S5_C))):
                gs.setdefault(nme, [None] * 2)[i] = val
            dq, dk, dv, dgate, dal, dbl, dalog, ddtb, dog = gdn_bwd(
                sv["qkvc"], sv["pm"], sv["abt"], sv["alog"], sv["dtb"], sv["og"], sv["stg"], sv["inv"],
                dcat[:, S5_W:])
            (dog_s,) = whole("gdn_og_sum", lambda t: (jnp.sum(t, axis=0, keepdims=True),),
                             [dog.reshape(GDN_H, GDN_D)], [((1, GDN_D), F32)])
            gs.setdefault("gdn_out_norm_g", [None] * 2)[i] = dog_s[0]
            gs.setdefault("gdn_a_log", [None] * 2)[i] = dalog.reshape(GDN_H)
            gs.setdefault("gdn_dt_bias", [None] * 2)[i] = ddtb.reshape(GDN_H)
            dqkvc = jnp.concatenate([dq, dk, dv], axis=1)
            dqkv, dconv = conv_bwd(sv["pm"], sv["conv_w"], dqkvc)
            gs.setdefault("gdn_conv_w", [None] * 2)[i] = dconv
            dpm = jnp.concatenate([du, dqkv, dgate], axis=1).astype(BF16)
            dabt = jnp.concatenate([dal, dbl], axis=0)
            dw_main = matmul("tn", dpm, sv["h"], "proj_dw", out_dtype=BF16)
            dw_ab = matmul("nn", dabt, sv["h"], "proj_ab_dw", out_dtype=BF16)
            gw["ab_w_in"] = jnp.concatenate([dw_main, dw_ab], axis=0)
            dx = gsink(layer, "B", gw, dx)
            w_in = w["ab_w_in"]
            dh = matmul("nn", dpm, w_in[:2560], "proj_dh")
            dx, dg = matmul("tn", dabt, w_in[2560:], "proj_ab_dh", add=dh,
                            norm_bwd=(sv["x0"], row2(p["ab_norm_g"][i]), dx))
            gs.setdefault("ab_norm_g", [None] * 2)[i] = dg[0]
        else:
            do = matmul("nt", dx, w["c_w_out"], "mix_dcat")
            gw["c_w_out"] = matmul("tn", sv["o"], dx, "mix_dwo", out_dtype=BF16)
            dqn, dkp, dvp, dbias = cattn_bwd(sv["qn"], sv["kp"], sv["vp"], sv["bias"], do)
            gs.setdefault("c_rel_bias", [None] * 2)[i] = rel_bias_grad(dbias.reshape(CA_H, CHUNK, CA_BAND))
            dq, dqg = row_bwd("headnorm_bwd", f_headnorm, [(sv["qkv"], D, 0)], [sv["qg"]], [dqn], [0], [0])
            dk, dkg = row_bwd("headnorm_bwd", f_headnorm, [(sv["qkv"], D, 1)], [sv["kg"]],
                              [(dkp, D, 0, CA_PAD)], [0], [0])
            head_sum = lambda t: (jnp.sum(t, axis=0, keepdims=True),)
            (dqg,) = whole("headgain_sum", head_sum, [dqg.reshape(CA_H, CA_D)], [((1, CA_D), F32)])
            (dkg,) = whole("headgain_sum", head_sum, [dkg.reshape(CA_H, CA_D)], [((1, CA_D), F32)])
            gs.setdefault("c_q_norm_g", [None] * 2)[i] = dqg[0]
            gs.setdefault("c_k_norm_g", [None] * 2)[i] = dkg[0]
            dqkv = jnp.concatenate([dq, dk, dvp[CA_PAD:]], axis=1).astype(BF16)
            gw["c_w_qkv"] = matmul("tn", dqkv, sv["h"], "proj_qkv_dw", out_dtype=BF16)
            dx = gsink(layer, "B", gw, dx)
            dx, dg = matmul("nn", dqkv, w["c_w_qkv"], "proj_qkv_dh",
                            norm_bwd=(sv["x0"], row2(w["c_norm_g"]), dx))
            gs.setdefault("c_norm_g", [None] * 2)[i] = dg[0]
    (dmg,) = row_bwd("mem_norm_bwd", f_norm, [mem], [row2(p["mem_norm_g"])], [dmem_n], [], [0])
    small = {n: jnp.stack(v) for n, v in gs.items()}
    small["mem_norm_g"] = dmg[0]
    return loss_vec, dx, small


ADAM_BLOCK_BYTES = 3 << 19


def adam(w, g, m, v, name):
    shape = w.shape
    if w.ndim == 3:
        d0, n, d2 = shape
        fits = [t for t in range(8, n + 1, 8) if n % t == 0 and d0 * t * d2 * 4 <= ADAM_BLOCK_BYTES]
        return tuple(row_fwd(name, f_adam, [w, g, m, v], [], [((d0, d2), F32)] * 3, tile=max(fits)))
    cols = shape[-1]
    w2, g2, m2, v2 = (t.reshape(-1, cols) for t in (w, g, m, v))
    rows = w2.shape[0]
    tile = rows if rows <= 512 else _tile(rows, 512, 8)
    outs = row_fwd(name, f_adam, [w2, g2, m2, v2], [], [(cols, F32)] * 3, tile=tile)
    return tuple(o.reshape(shape) for o in outs)


WEIGHTS = ['ab_norm_g', 'ab_w_in', 'ab_w_out', 's5_a_re', 's5_a_im', 's5_log_dt', 's5_b_re', 's5_b_im', 's5_c_re',
           's5_c_im', 's5_d', 's5_w_glu', 's5_b_glu', 'gdn_conv_w', 'gdn_a_log', 'gdn_dt_bias', 'gdn_out_norm_g',
           'c_norm_g', 'c_w_qkv', 'c_w_out', 'c_q_norm_g', 'c_k_norm_g', 'c_rel_bias', 'mem_norm_g', 'xa_norm_g',
           'xa_w_q', 'xa_w_kv', 'xa_w_out', 'xa_q_norm_g', 'xa_k_norm_g', 'f_norm_g', 'f_w_gate', 'f_w_up',
           'f_w_down']
SHARDED_SMALL = {"gdn_conv_w": (2, 384), "c_norm_g": (1, 256)}


def kernel(x, mem, ab_norm_g, ab_w_in, ab_w_out, s5_a_re, s5_a_im, s5_log_dt, s5_b_re, s5_b_im, s5_c_re, s5_c_im, s5_d, s5_w_glu, s5_b_glu, gdn_conv_w, gdn_a_log, gdn_dt_bias, gdn_out_norm_g, c_norm_g, c_w_qkv, c_w_out, c_q_norm_g, c_k_norm_g, c_rel_bias, mem_norm_g, xa_norm_g, xa_w_q, xa_w_kv, xa_w_out, xa_q_norm_g, xa_k_norm_g, f_norm_g, f_w_gate, f_w_up, f_w_down, loss_target, m_ab_norm_g, m_ab_w_in, m_ab_w_out, m_s5_a_re, m_s5_a_im, m_s5_log_dt, m_s5_b_re, m_s5_b_im, m_s5_c_re, m_s5_c_im, m_s5_d, m_s5_w_glu, m_s5_b_glu, m_gdn_conv_w, m_gdn_a_log, m_gdn_dt_bias, m_gdn_out_norm_g, m_c_norm_g, m_c_w_qkv, m_c_w_out, m_c_q_norm_g, m_c_k_norm_g, m_c_rel_bias, m_mem_norm_g, m_xa_norm_g, m_xa_w_q, m_xa_w_kv, m_xa_w_out, m_xa_q_norm_g, m_xa_k_norm_g, m_f_norm_g, m_f_w_gate, m_f_w_up, m_f_w_down, v_ab_norm_g, v_ab_w_in, v_ab_w_out, v_s5_a_re, v_s5_a_im, v_s5_log_dt, v_s5_b_re, v_s5_b_im, v_s5_c_re, v_s5_c_im, v_s5_d, v_s5_w_glu, v_s5_b_glu, v_gdn_conv_w, v_gdn_a_log, v_gdn_dt_bias, v_gdn_out_norm_g, v_c_norm_g, v_c_w_qkv, v_c_w_out, v_c_q_norm_g, v_c_k_norm_g, v_c_rel_bias, v_mem_norm_g, v_xa_norm_g, v_xa_w_q, v_xa_w_kv, v_xa_w_out, v_xa_q_norm_g, v_xa_k_norm_g, v_f_norm_g, v_f_w_gate, v_f_w_up, v_f_w_down):
    args = locals()
    p = {n: args[n] for n in WEIGHTS}
    m = {n: args["m_" + n] for n in WEIGHTS}
    v = {n: args["v_" + n] for n in WEIGHTS}
    chip = 2 * lax.axis_index("x") + lax.axis_index("y")

    carry = x[0]
    gathers = {}
    for layer in range(DEPTH):
        for grp in GROUPS:
            src = pack_group_shards(p, layer, grp)
            land = lax.dynamic_update_slice(lax.empty((N_CHIPS,) + src.shape, BF16), src[None], (chip, 0, 0))
            send_sems, recv_sems, src, land, carry = exchange_start(
                src, land, carry, f"gather_start_{layer}{grp}", mode="gather_half")
            gathers[layer, grp] = (src, land, send_sems, recv_sems)
    forwards = {}

    def wpre(layer, grp, carry):
        if (layer, grp) not in forwards:
            src, land, send_sems, recv_sems = gathers[layer, grp]
            src, land = exchange_wait(src, land, send_sems, recv_sems, carry, f"gather_wait_{layer}{grp}",
                                      mode="gather_half")
            send_sems, recv_sems, src, land, carry = exchange_start(
                src, land, carry, f"forward_start_{layer}{grp}", mode="forward_half")
            forwards[layer, grp] = (src, land, send_sems, recv_sems)
        return carry

    def wsrc(layer, grp, after):
        wpre(layer, grp, after)
        src, land, send_sems, recv_sems = forwards[layer, grp]
        _, land = exchange_wait(src, land, send_sems, recv_sems, after, f"forward_wait_{layer}{grp}",
                                mode="forward_half")
        return unpack_group_gathered(land, layer, grp)

    scatters, siblings = [], []
    LAG = 2

    def finish(carry):
        layer, grp, src, land, send_sems, recv_sems = scatters[len(siblings)]
        src, land = exchange_wait(src, land, send_sems, recv_sems, carry, f"scatter_wait_{layer}{grp}", mode="scatter")
        part = sum_own_slabs(src, chip, land, "sum_chips")
        send_sems, recv_sems, part, other, carry = exchange_start(
            part, lax.empty(part.shape, F32), carry, f"sibling_start_{layer}{grp}", mode="sibling")
        siblings.append((layer, grp, part, other, send_sems, recv_sems))
        return carry

    def gsink(layer, grp, gw, carry):
        src = pack_group_grads(gw, layer, grp)
        land = lax.empty((3,) + src.shape[1:], BF16)
        send_sems, recv_sems, src, land, carry = exchange_start(
            src, land, carry, f"scatter_start_{layer}{grp}", mode="scatter")
        scatters.append((layer, grp, src, land, send_sems, recv_sems))
        if len(scatters) > LAG:
            carry = finish(carry)
        return carry

    carry = wpre(0, "B", carry)
    loss_vec, dx, g_small = local_step(carry, mem[0], loss_target[0], p, wsrc, gsink, wpre)

    full_shapes = {n: ((2, 4, 1536) if n == "gdn_conv_w" else (2, D) if n == "c_norm_g" else p[n].shape)
                   for n in SMALL}
    small_mine = pack_small(g_small, extra=loss_vec)
    (small_other,) = sibling_exchange([small_mine], "sibling_small")
    (small_chip,) = row_fwd("sum_small_cores", lambda a, b: (a + b,), [small_mine, small_other], [], [(128, F32)],
                            tile=128)
    small_sum = sum_slabs(chip_gather(small_chip, "gather_small"), "sum_small")
    g_s, rest = unpack_small(small_sum, full_shapes)
    (loss11,) = whole("loss_sum", lambda t: (jnp.sum(jnp.sum(t, axis=1, keepdims=True), axis=0, keepdims=True),),
                      [rest[:D // 128]], [((1, 1), F32)])
    for n, (axis, width) in SHARDED_SMALL.items():
        g_s[n] = lax.dynamic_slice_in_dim(g_s[n], chip * width, width, axis=axis)
    delta, new_m, new_v = {}, {}, {}
    for n in SMALL:
        shape = p[n].shape
        two = (1, shape[0]) if len(shape) == 1 else (int(np.prod(shape[:-1])), shape[-1])
        outs = whole("adam_" + n, f_adam, [t.reshape(two) for t in (p[n], g_s[n], m[n], v[n])], [(two, F32)] * 3)
        delta[n], new_m[n], new_v[n] = (o.reshape(shape) for o in outs)

    while len(siblings) < len(scatters):
        dx = finish(dx)

    per_layer = {}
    for layer, grp, part, other, send_sems, recv_sems in siblings:
        part, other = exchange_wait(part, other, send_sems, recv_sems, dx, f"sibling_wait_{layer}{grp}",
                                    mode="sibling")
        (total,) = row_fwd("sum_cores", lambda a, b: (a + b,), [part, other], [], [(D, F32)],
                           tile=_tile(part.shape[0], 512, 16))
        for name, g in unpack_group_reduced(total, layer, grp).items():
            per_layer.setdefault(name, {})[layer] = g
    grads = {name: jnp.stack([d[k] for k in sorted(d)]) for name, d in per_layer.items()}
    grads.update(g_s)
    for name in BIG:
        if name in ADAM_TRANSPOSED:
            t = lambda a: jnp.swapaxes(a, 1, 2)
            outs = adam(t(p[name]), grads[name], t(m[name]), t(v[name]), "adam_" + name)
            delta[name], new_m[name], new_v[name] = (t(o) for o in outs)
            grads[name] = t(grads[name])
        else:
            delta[name], new_m[name], new_v[name] = adam(p[name], grads[name], m[name], v[name], "adam_" + name)

    return (loss11[0, 0], dx[None], *[grads[n] for n in WEIGHTS], *[delta[n] for n in WEIGHTS],
            *[new_m[n] for n in WEIGHTS], *[new_v[n] for n in WEIGHTS])
```

```python
import numpy as np
import jax
import jax.numpy as jnp
from jax import lax
from jax.experimental import pallas as pl
from jax.experimental.pallas import tpu as pltpu

F32 = jnp.float32
BF16 = jnp.bfloat16
MESH = pl.DeviceIdType.MESH

D = 1024
CHUNK = 64
EPS = 1e-6
S5_W = 512
S5_G = 32
S5_C = 16
S5_P = 64
S5_GB = 4
GDN_H = 4
GDN_D = 128
CA_H = 16
CA_D = 64
CA_LEFT = 8
CA_BAND = (CA_LEFT + 1) * CHUNK
CA_PAD = CA_LEFT * CHUNK
MAX_REL = 128
XA_H = 4
XA_D = 256
FFN = 2816
DEPTH = 4
N_CHIPS = 4
LR, B1, B2, AEPS, WD, STEP = 0.001, 0.9, 0.999, 1e-08, 0.01, 10

VMEM_LIMIT = 56 * 1024 * 1024
ROW_TILE = 256


def _cp(*sem):
    return pltpu.CompilerParams(dimension_semantics=sem, vmem_limit_bytes=VMEM_LIMIT)


def _dg(a, b, ca, cb):
    return lax.dot_general(a.astype(BF16), b.astype(BF16), (((ca,), (cb,)), ((), ())),
                           preferred_element_type=F32)


@jax.custom_vjp
def mm(a, b):
    return _dg(a, b, 1, 0)


@jax.custom_vjp
def mm_nt(a, b):
    return _dg(a, b, 1, 1)


@jax.custom_vjp
def mm_tn(a, b):
    return _dg(a, b, 0, 0)


mm.defvjp(lambda a, b: (mm(a, b), (a, b)), lambda r, g: (mm_nt(g, r[1]), mm_tn(r[0], g)))
mm_nt.defvjp(lambda a, b: (mm_nt(a, b), (a, b)), lambda r, g: (mm(g, r[1]), mm_tn(g, r[0])))
mm_tn.defvjp(lambda a, b: (mm_tn(a, b), (a, b)), lambda r, g: (mm_nt(r[1], g), mm(r[0], g)))


def _split(a):
    hi = a.astype(BF16)
    return hi, (a - hi.astype(F32)).astype(BF16)


def _tri_mm(v, upper):
    T = v.shape[0]
    r = lax.broadcasted_iota(jnp.int32, (T, T), 0)
    c = lax.broadcasted_iota(jnp.int32, (T, T), 1)
    m = ((c >= r) if upper else (r >= c)).astype(BF16)
    hi, lo = _split(v)
    d = lambda u: lax.dot_general(m, u, (((1,), (0,)), ((), ())), preferred_element_type=F32)
    return d(hi) + d(lo)


@jax.custom_vjp
def cumsum_rows(v):
    return _tri_mm(v, False)


cumsum_rows.defvjp(lambda v: (_tri_mm(v, False), None), lambda _, g: (_tri_mm(g, True),))


def _rms(x, g):
    return x * lax.rsqrt(jnp.mean(x * x, axis=-1, keepdims=True) + EPS) * g


def _softmax(s):
    e = jnp.exp(s - lax.stop_gradient(jnp.max(s, axis=-1, keepdims=True)))
    return e / jnp.sum(e, axis=-1, keepdims=True)


def _softplus(x):
    return jnp.maximum(x, 0.0) + jnp.log(1.0 + jnp.exp(-jnp.abs(x)))


def _tile(n, cap, align):
    if n <= cap:
        return n
    best = None
    for d in range(align, cap + 1, align):
        if n % d == 0:
            best = d
    assert best is not None, (n, cap, align)
    return best


def matmul(mode, a, b, name, out_dtype=F32, add=None, norm_out=None, norm_bwd=None):
    if mode == "nn":
        (M, K), (K2, N) = a.shape, b.shape
    elif mode == "nt":
        (M, K), (N, K2) = a.shape, b.shape
    else:
        (K, M), (K2, N) = a.shape, b.shape
    assert K == K2, (mode, a.shape, b.shape)
    if mode != "tn" and norm_out is None and norm_bwd is None:
        tm, tn, tk = _tile(M, 1024, 128), _tile(N, 512, 128), _tile(K, 2048, 128)
    else:
        tm, tn, tk = _tile(M, 512, 128), _tile(N, 1536, 128), _tile(K, 2048, 128)
    nk = K // tk
    if mode == "nn":
        a_spec = pl.BlockSpec((tm, tk), lambda i, j, k: (i, k))
        b_spec = pl.BlockSpec((tk, tn), lambda i, j, k: (k, j))
        dn = (((1,), (0,)), ((), ()))
    elif mode == "nt":
        a_spec = pl.BlockSpec((tm, tk), lambda i, j, k: (i, k))
        b_spec = pl.BlockSpec((tn, tk), lambda i, j, k: (j, k))
        dn = (((1,), (1,)), ((), ()))
    else:
        a_spec = pl.BlockSpec((tk, tm), lambda i, j, k: (k, i))
        b_spec = pl.BlockSpec((tk, tn), lambda i, j, k: (k, j))
        dn = (((0,), (0,)), ((), ()))
    o_spec = pl.BlockSpec((tm, tn), lambda i, j, k: (i, j))
    has_add = add is not None
    g_spec = pl.BlockSpec((1, tn), lambda i, j, k: (0, j))
    extra, extra_specs, out_shapes, out_specs = [], [], [jax.ShapeDtypeStruct((M, N), out_dtype)], [o_spec]
    sem = ("parallel", "parallel", "arbitrary")
    if norm_out is not None:
        assert tn == N
        extra, extra_specs = [norm_out], [g_spec]
        out_shapes.append(jax.ShapeDtypeStruct((M, N), BF16))
        out_specs.append(o_spec)
    if norm_bwd is not None:
        assert tn == N
        x_in, g_in, res_in = norm_bwd
        extra, extra_specs = [x_in, g_in, res_in], [o_spec, g_spec, o_spec]
        out_shapes.append(jax.ShapeDtypeStruct((1, N), F32))
        out_specs.append(g_spec)
        sem = ("arbitrary", "arbitrary", "arbitrary")
    n_in = 2 + int(has_add) + len(extra)
    n_out = len(out_shapes)

    def body(*refs):
        a_ref, b_ref = refs[0], refs[1]
        add_ref = refs[2] if has_add else None
        extra_refs = refs[2 + int(has_add):n_in]
        o_ref = refs[n_in]
        acc_ref = refs[-1]
        i = pl.program_id(0)
        p = lax.dot_general(a_ref[...].astype(BF16), b_ref[...].astype(BF16), dn,
                            preferred_element_type=F32)

        def finish(total):
            if has_add:
                total = total + add_ref[...]
            if norm_out is not None:
                o_ref[...] = total.astype(o_ref.dtype)
                refs[n_in + 1][...] = _rms(total, extra_refs[0][...]).astype(BF16)
            elif norm_bwd is not None:
                x_ref, g_ref, res_ref = extra_refs
                _, vjp = jax.vjp(f_norm_res, x_ref[...], g_ref[...])
                dx, dg = vjp((total, res_ref[...]))
                o_ref[...] = dx
                dg_ref = refs[n_in + 1]

                @pl.when(i == 0)
                def _():
                    dg_ref[...] = dg

                @pl.when(i > 0)
                def _():
                    dg_ref[...] += dg
            else:
                o_ref[...] = total.astype(o_ref.dtype)

        if nk == 1:
            finish(p)
        else:
            k = pl.program_id(2)

            @pl.when(k == 0)
            def _():
                acc_ref[...] = p

            @pl.when(k > 0)
            def _():
                acc_ref[...] += p

            @pl.when(k == nk - 1)
            def _():
                finish(acc_ref[...])

    ins = [a, b] + ([add] if has_add else []) + extra
    specs = [a_spec, b_spec] + ([o_spec] if has_add else []) + extra_specs
    out = pl.pallas_call(
        body, name=name, grid=(M // tm, N // tn, nk), in_specs=specs, out_specs=out_specs,
        out_shape=out_shapes, scratch_shapes=[pltpu.VMEM((tm, tn), F32)],
        compiler_params=_cp(*sem),
    )(*ins)
    return out[0] if n_out == 1 else out


def ffn_in(h, wg, wu):
    (S, K), F = h.shape, wg.shape[0]
    tm, tn = _tile(S, 1024, 128), _tile(F, 512, 128)
    dn = (((1,), (1,)), ((), ()))

    def body(h_ref, wg_ref, wu_ref, g_ref, u_ref, act_ref):
        a = h_ref[...].astype(BF16)
        g = lax.dot_general(a, wg_ref[...].astype(BF16), dn, preferred_element_type=F32)
        u = lax.dot_general(a, wu_ref[...].astype(BF16), dn, preferred_element_type=F32)
        g_ref[...] = g.astype(BF16)
        u_ref[...] = u.astype(BF16)
        act_ref[...] = f_swiglu(g, u)[0].astype(BF16)

    w_spec = pl.BlockSpec((tn, K), lambda i, j: (j, 0))
    o_spec = pl.BlockSpec((tm, tn), lambda i, j: (i, j))
    sd = jax.ShapeDtypeStruct
    return pl.pallas_call(
        body, name="ffn_in", grid=(S // tm, F // tn),
        in_specs=[pl.BlockSpec((tm, K), lambda i, j: (i, 0)), w_spec, w_spec], out_specs=[o_spec] * 3,
        out_shape=[sd((S, F), BF16), sd((S, F), BF16), sd((S, F), BF16)],
        compiler_params=_cp("parallel", "parallel"),
    )(h, wg, wu)


def ffn_dact(dy, wd, gate, up):
    (S, K), F = dy.shape, wd.shape[0]
    tm, tn = _tile(S, 1024, 128), _tile(F, 512, 128)
    dn = (((1,), (1,)), ((), ()))

    def body(dy_ref, wd_ref, g_ref, u_ref, dg_ref, du_ref):
        dact = lax.dot_general(dy_ref[...].astype(BF16), wd_ref[...].astype(BF16), dn, preferred_element_type=F32)
        _, vjp = jax.vjp(f_swiglu, g_ref[...].astype(F32), u_ref[...].astype(F32))
        dg, du = vjp((dact,))
        dg_ref[...] = dg.astype(BF16)
        du_ref[...] = du.astype(BF16)

    o_spec = pl.BlockSpec((tm, tn), lambda i, j: (i, j))
    sd = jax.ShapeDtypeStruct
    return pl.pallas_call(
        body, name="ffn_dact", grid=(S // tm, F // tn),
        in_specs=[pl.BlockSpec((tm, K), lambda i, j: (i, 0)), pl.BlockSpec((tn, K), lambda i, j: (j, 0)),
                  o_spec, o_spec],
        out_specs=[o_spec] * 2, out_shape=[sd((S, F), BF16)] * 2,
        compiler_params=_cp("parallel", "parallel"),
    )(dy, wd, gate, up)


def _row_spec(arr, tile):
    if isinstance(arr, tuple):
        a, w, cb = arr[:3]
        ro = (arr[3] // tile) if len(arr) > 3 else 0
        assert len(arr) < 4 or arr[3] % tile == 0
        return a, pl.BlockSpec((tile, w), lambda i, cb=cb, ro=ro: (i + ro, cb)), (tile, w)
    if arr.ndim == 3:
        d0, _, d2 = arr.shape
        return arr, pl.BlockSpec((d0, tile, d2), lambda i: (0, i, 0)), (d0, tile, d2)
    return arr, pl.BlockSpec((tile, arr.shape[1]), lambda i: (i, 0)), (tile, arr.shape[1])


def _full_spec(arr):
    nd = arr.ndim
    return pl.BlockSpec(arr.shape, lambda i, nd=nd: (0,) * nd)


def _n_rows(arr):
    a = arr[0] if isinstance(arr, tuple) else arr
    return a.shape[1] if a.ndim == 3 else a.shape[0]


def _row_out_shape(shape_tail, n, dtype):
    if isinstance(shape_tail, tuple):
        d0, d2 = shape_tail
        return (jax.ShapeDtypeStruct((d0, n, d2), dtype),
                lambda tile: pl.BlockSpec((d0, tile, d2), lambda i: (0, i, 0)))
    return (jax.ShapeDtypeStruct((n, shape_tail), dtype),
            lambda tile: pl.BlockSpec((tile, shape_tail), lambda i: (i, 0)))


def _f32(v):
    return v.astype(F32) if v.dtype == BF16 else v


def row_fwd(name, fn, rows, fulls, outs, tile=ROW_TILE):
    n = _n_rows(rows[0])
    tile = min(tile, n)
    assert n % tile == 0, (name, n, tile)
    rs = [_row_spec(r, tile) for r in rows]
    os_ = [_row_out_shape(w, n, dt) for w, dt in outs]
    nr, nf = len(rows), len(fulls)

    def body(*refs):
        vals = [_f32(r[...]) for r in refs[:nr + nf]]
        res = fn(*vals)
        for r, v in zip(refs[nr + nf:], res):
            r[...] = v.astype(r.dtype)

    out = pl.pallas_call(
        body, name=name, grid=(n // tile,),
        in_specs=[s for _, s, _ in rs] + [_full_spec(f) for f in fulls],
        out_specs=[mk(tile) for _, mk in os_], out_shape=[sh for sh, _ in os_],
        compiler_params=_cp("parallel"),
    )(*[a for a, _, _ in rs], *fulls)
    return out


def row_bwd(name, fn, rows, fulls, cts, want_rows, want_fulls, row_dtypes=None, tile=ROW_TILE):
    n = _n_rows(rows[0])
    tile = min(tile, n)
    assert n % tile == 0, (name, n, tile)
    rs = [_row_spec(r, tile) for r in rows]
    cs = [_row_spec(c, tile) for c in cts]
    nr, nf, nc = len(rows), len(fulls), len(cts)
    row_dtypes = row_dtypes or [F32] * len(want_rows)
    out_shapes, out_specs = [], []
    for k, idx in enumerate(want_rows):
        a, _, blk = rs[idx]
        if len(blk) == 3:
            sh, mk = _row_out_shape((blk[0], blk[2]), n, row_dtypes[k])
        else:
            sh, mk = _row_out_shape(blk[1], n, row_dtypes[k])
        out_shapes.append(sh)
        out_specs.append(mk(tile))
    for idx in want_fulls:
        out_shapes.append(jax.ShapeDtypeStruct(fulls[idx].shape, F32))
        out_specs.append(_full_spec(fulls[idx]))
    n_wr = len(want_rows)

    def body(*refs):
        i = pl.program_id(0)
        vals = [_f32(r[...]) for r in refs[:nr + nf]]
        ct_vals = [_f32(r[...]) for r in refs[nr + nf:nr + nf + nc]]
        outs = refs[nr + nf + nc:]
        _, vjp = jax.vjp(fn, *vals)
        grads = vjp(tuple(ct_vals))
        for k, idx in enumerate(want_rows):
            outs[k][...] = grads[idx].astype(outs[k].dtype)
        for k, idx in enumerate(want_fulls):
            o = outs[n_wr + k]
            g = grads[nr + idx]

            @pl.when(i == 0)
            def _(o=o, g=g):
                o[...] = g

            @pl.when(i > 0)
            def _(o=o, g=g):
                o[...] += g

    out = pl.pallas_call(
        body, name=name, grid=(n // tile,),
        in_specs=[s for _, s, _ in rs] + [_full_spec(f) for f in fulls] + [s for _, s, _ in cs],
        out_specs=out_specs, out_shape=out_shapes,
        compiler_params=_cp("arbitrary"),
    )(*[a for a, _, _ in rs], *fulls, *[a for a, _, _ in cs])
    return out


def whole(name, fn, args, outs):
    def body(*refs):
        res = fn(*[r[...] for r in refs[:len(args)]])
        for r, v in zip(refs[len(args):], res):
            r[...] = v.astype(r.dtype)

    return pl.pallas_call(
        body, name=name, out_shape=[jax.ShapeDtypeStruct(s, d) for s, d in outs],
        compiler_params=pltpu.CompilerParams(vmem_limit_bytes=VMEM_LIMIT),
    )(*args)


def f_norm(x, g):
    return (_rms(x, g),)


def f_norm_res(x, g):
    return _rms(x, g), x


def f_swiglu(g, u):
    return (g * jax.nn.sigmoid(g) * u,)


def f_glu(y, w, b):
    h = jax.nn.gelu(y)
    return (h * jax.nn.sigmoid(mm(h, w) + b),)


def f_xattn(q, kv, qg, kg):
    outs = []
    for h in range(XA_H):
        sl = slice(h * XA_D, (h + 1) * XA_D)
        qn = _rms(q[:, sl], qg)
        kn = _rms(kv[:, sl], kg)
        vh = kv[:, D + h * XA_D:D + (h + 1) * XA_D]
        p = _softmax(mm_nt(qn, kn) * (XA_D ** -0.5))
        outs.append(mm(p, vh))
    return (jnp.concatenate(outs, axis=-1),)


def f_adam(w, g, m, v):
    m2 = B1 * m + (1.0 - B1) * g
    v2 = B2 * v + (1.0 - B2) * (g * g)
    m_hat = m2 / (1.0 - B1 ** STEP)
    v_hat = v2 / (1.0 - B2 ** STEP)
    delta = -LR * (m_hat / (jnp.sqrt(v_hat) + AEPS) + WD * w)
    return delta, m2, v2


S5_NTAB, S5_NROW = 4, 6


def _s5_tables(are, aim, ldt):
    T = CHUNK
    dt = jnp.exp(ldt)
    ar, ai = are * dt, aim * dt
    t = lax.broadcasted_iota(jnp.int32, (T, 1), 0).astype(F32)
    mag, inv = jnp.exp(t * ar), jnp.exp(-t * ar)
    cs, sn = jnp.cos(t * ai), jnp.sin(t * ai)
    e_re, e_im = mag * cs, mag * sn
    n_re, n_im = inv * cs, -inv * sn
    l_re, l_im = jnp.exp(ar) * jnp.cos(ai), jnp.exp(ar) * jnp.sin(ai)
    den = are * are + aim * aim
    k_re = ((l_re - 1.0) * are + l_im * aim) / den
    k_im = (l_im * are - (l_re - 1.0) * aim) / den
    tl = float(T - 1)
    m_re, m_im = jnp.exp(tl * ar) * jnp.cos(tl * ai), jnp.exp(tl * ar) * jnp.sin(tl * ai)
    return (e_re, e_im, n_re, n_im), (l_re, l_im, k_re, k_im, m_re, m_im)


def _s5_chunk(u, sre, sim, tabs, rows, b_re, b_im, c_re, c_im, dv):
    e_re, e_im, n_re, n_im = tabs
    l_re, l_im, k_re, k_im, m_re, m_im = rows
    x_re, x_im = mm(u, b_re), mm(u, b_im)
    bu_re = k_re * x_re - k_im * x_im
    bu_im = k_re * x_im + k_im * x_re
    v_re = bu_re * n_re - bu_im * n_im
    v_im = bu_re * n_im + bu_im * n_re
    p_re = l_re * sre - l_im * sim
    p_im = l_re * sim + l_im * sre
    w_re = cumsum_rows(v_re) + p_re
    w_im = cumsum_rows(v_im) + p_im
    s_re = e_re * w_re - e_im * w_im
    s_im = e_re * w_im + e_im * w_re
    y = mm(s_re, c_re) - mm(s_im, c_im) + dv * u
    z_re = jnp.sum(v_re, axis=0, keepdims=True) + p_re
    z_im = jnp.sum(v_im, axis=0, keepdims=True) + p_im
    return y, m_re * z_re - m_im * z_im, m_re * z_im + m_im * z_re


def _s5_fill_tables(are_ref, aim_ref, ldt_ref, tab, row):
    for g in range(S5_GB):
        ls = slice(512 * g, 512 * (g + 1))
        tabs, rows = _s5_tables(are_ref[:, ls], aim_ref[:, ls], ldt_ref[:, ls])
        for k, t in enumerate(tabs):
            tab[k, :, ls] = t
        for k, r in enumerate(rows):
            row[k:k + 1, ls] = r


def _s5_read_tables(tab, row, ls):
    return (tuple(tab[k, :, ls] for k in range(S5_NTAB)), tuple(row[k:k + 1, ls] for k in range(S5_NROW)))


def _s5_specs(nc, rev):
    T = CHUNK

    def ci(c):
        return nc - 1 - c if rev else c

    u_spec = pl.BlockSpec((T, S5_W), lambda c: (ci(c), 0))
    p_spec = pl.BlockSpec((1, S5_G * S5_P), lambda c: (0, 0))
    b_spec = pl.BlockSpec((S5_GB, 128, 512), lambda c: (0, 0, 0))
    c_spec = pl.BlockSpec((S5_GB, 512, 128), lambda c: (0, 0, 0))
    d_spec = pl.BlockSpec((1, S5_W), lambda c: (0, 0))
    st_spec = pl.BlockSpec((None, 2, S5_G * S5_P), lambda c: (ci(c), 0, 0))
    return u_spec, p_spec, b_spec, c_spec, d_spec, st_spec


def s5_fwd(pm, are, aim, ldt, b_re, b_im, c_re, c_im, dv):
    S = pm.shape[0]
    nc = S // CHUNK
    u_spec, p_spec, b_spec, c_spec, d_spec, st_spec = _s5_specs(nc, False)

    def body(u_ref, are_ref, aim_ref, ldt_ref, bre_ref, bim_ref, cre_ref, cim_ref, dv_ref,
             y_ref, st_ref, state, tab, row):
        c = pl.program_id(0)

        @pl.when(c == 0)
        def _():
            state[...] = jnp.zeros_like(state)
            _s5_fill_tables(are_ref, aim_ref, ldt_ref, tab, row)

        st_ref[...] = state[...]
        for g in range(S5_GB):
            lu, ls = slice(128 * g, 128 * (g + 1)), slice(512 * g, 512 * (g + 1))
            tabs, rows = _s5_read_tables(tab, row, ls)
            y, e_re, e_im = _s5_chunk(u_ref[:, lu], state[0:1, ls], state[1:2, ls], tabs, rows,
                                      bre_ref[g], bim_ref[g], cre_ref[g], cim_ref[g], dv_ref[:, lu])
            y_ref[:, lu] = y
            state[0:1, ls] = e_re
            state[1:2, ls] = e_im

    n_state = S5_G * S5_P
    return pl.pallas_call(
        body, name="s5_fwd", grid=(nc,),
        in_specs=[u_spec, p_spec, p_spec, p_spec, b_spec, b_spec, c_spec, c_spec, d_spec],
        out_specs=[u_spec, st_spec],
        out_shape=[jax.ShapeDtypeStruct((S, S5_W), F32), jax.ShapeDtypeStruct((nc, 2, n_state), F32)],
        scratch_shapes=[pltpu.VMEM((2, n_state), F32), pltpu.VMEM((S5_NTAB, CHUNK, n_state), F32),
                        pltpu.VMEM((8, n_state), F32)],
        compiler_params=_cp("arbitrary"),
    )(pm, are, aim, ldt, b_re, b_im, c_re, c_im, dv)


def s5_bwd(pm, st, dy, are, aim, ldt, b_re, b_im, c_re, c_im, dv):
    S = pm.shape[0]
    nc = S // CHUNK
    u_spec, p_spec, b_spec, c_spec, d_spec, st_spec = _s5_specs(nc, True)

    def body(u_ref, st_ref, dy_ref, are_ref, aim_ref, ldt_ref, bre_ref, bim_ref, cre_ref, cim_ref, dv_ref,
             du_ref, dare_ref, daim_ref, dldt_ref, dbre_ref, dbim_ref, dcre_ref, dcim_ref, ddv_ref,
             dstate, tab, row, dtab, drow):
        c = pl.program_id(0)

        @pl.when(c == 0)
        def _():
            dstate[...] = jnp.zeros_like(dstate)
            dtab[...] = jnp.zeros_like(dtab)
            drow[...] = jnp.zeros_like(drow)
            _s5_fill_tables(are_ref, aim_ref, ldt_ref, tab, row)

        for g in range(S5_GB):
            lu, ls = slice(128 * g, 128 * (g + 1)), slice(512 * g, 512 * (g + 1))
            every = slice(None)
            tabs, rows = _s5_read_tables(tab, row, ls)
            args = (u_ref[:, lu], st_ref[0:1, ls], st_ref[1:2, ls], tabs, rows,
                    bre_ref[g], bim_ref[g], cre_ref[g], cim_ref[g], dv_ref[:, lu])
            _, vjp = jax.vjp(_s5_chunk, *args)
            gr = vjp((dy_ref[:, lu], dstate[0:1, ls], dstate[1:2, ls]))
            du_ref[:, lu] = gr[0]
            dstate[0:1, ls] = gr[1]
            dstate[1:2, ls] = gr[2]
            for k, t in enumerate(gr[3]):
                dtab[k, :, ls] += t
            for k, r in enumerate(gr[4]):
                drow[k:k + 1, ls] += r
            accs = ((dbre_ref, (g,)), (dbim_ref, (g,)), (dcre_ref, (g,)), (dcim_ref, (g,)), (ddv_ref, (every, lu)))
            for (o, idx), gv in zip(accs, gr[5:]):
                @pl.when(c == 0)
                def _(o=o, idx=idx, gv=gv):
                    o[idx] = gv

                @pl.when(c > 0)
                def _(o=o, idx=idx, gv=gv):
                    o[idx] += gv

        @pl.when(c == nc - 1)
        def _():
            for g in range(S5_GB):
                ls = slice(512 * g, 512 * (g + 1))
                _, vjp = jax.vjp(_s5_tables, are_ref[:, ls], aim_ref[:, ls], ldt_ref[:, ls])
                dtabs, drows = _s5_read_tables(dtab, drow, ls)
                ga, gi, gl = vjp((dtabs, drows))
                dare_ref[:, ls] = ga
                daim_ref[:, ls] = gi
                dldt_ref[:, ls] = gl

    n_state = S5_G * S5_P
    return pl.pallas_call(
        body, name="s5_bwd", grid=(nc,),
        in_specs=[u_spec, st_spec, u_spec, p_spec, p_spec, p_spec, b_spec, b_spec, c_spec, c_spec, d_spec],
        out_specs=[u_spec, p_spec, p_spec, p_spec, b_spec, b_spec, c_spec, c_spec, d_spec],
        out_shape=[jax.ShapeDtypeStruct((S, S5_W), F32)] + [jax.ShapeDtypeStruct((1, n_state), F32)] * 3
        + [jax.ShapeDtypeStruct((S5_GB, 128, 512), F32)] * 2 + [jax.ShapeDtypeStruct((S5_GB, 512, 128), F32)] * 2
        + [jax.ShapeDtypeStruct((1, S5_W), F32)],
        scratch_shapes=[pltpu.VMEM((2, n_state), F32), pltpu.VMEM((S5_NTAB, CHUNK, n_state), F32),
                        pltpu.VMEM((8, n_state), F32), pltpu.VMEM((S5_NTAB, CHUNK, n_state), F32),
                        pltpu.VMEM((8, n_state), F32)],
        compiler_params=_cp("arbitrary"),
    )(pm, st, dy, are, aim, ldt, b_re, b_im, c_re, c_im, dv)


def conv_fwd(pm, w):
    S = pm.shape[0]

    def body(x_ref, w_ref, o_ref, pad):
        x = x_ref[...]
        pad[0:8, :] = jnp.zeros((8, 128), F32)
        pad[8:, :] = x
        y = (w_ref[3:4, :] * x + w_ref[2:3, :] * pad[7:7 + S, :] + w_ref[1:2, :] * pad[6:6 + S, :]
             + w_ref[0:1, :] * pad[5:5 + S, :])
        o_ref[...] = y * jax.nn.sigmoid(y)

    return pl.pallas_call(
        body, name="conv_fwd", grid=(12,),
        in_specs=[pl.BlockSpec((S, 128), lambda j: (0, 4 + j)), pl.BlockSpec((4, 128), lambda j: (0, j))],
        out_specs=pl.BlockSpec((S, 128), lambda j: (0, j)),
        out_shape=jax.ShapeDtypeStruct((S, 1536), F32),
        scratch_shapes=[pltpu.VMEM((S + 8, 128), F32)],
        compiler_params=_cp("parallel"),
    )(pm, w)


def conv_bwd(pm, w, dout):
    S = pm.shape[0]

    def body(x_ref, w_ref, do_ref, dx_ref, dw_ref, pad, dpad):
        x = x_ref[...]
        pad[0:8, :] = jnp.zeros((8, 128), F32)
        pad[8:, :] = x
        xs = [pad[5:5 + S, :], pad[6:6 + S, :], pad[7:7 + S, :], x]
        y = w_ref[0:1, :] * xs[0] + w_ref[1:2, :] * xs[1] + w_ref[2:3, :] * xs[2] + w_ref[3:4, :] * xs[3]
        sg = jax.nn.sigmoid(y)
        dy = do_ref[...] * (sg + y * sg * (1.0 - sg))
        dpad[0:S, :] = dy
        dpad[S:, :] = jnp.zeros((8, 128), F32)
        dx_ref[...] = (w_ref[3:4, :] * dy + w_ref[2:3, :] * dpad[1:1 + S, :] + w_ref[1:2, :] * dpad[2:2 + S, :]
                       + w_ref[0:1, :] * dpad[3:3 + S, :])
        for i in range(4):
            dw_ref[i:i + 1, :] = jnp.sum(dy * xs[i], axis=0, keepdims=True)

    return pl.pallas_call(
        body, name="conv_bwd", grid=(12,),
        in_specs=[pl.BlockSpec((S, 128), lambda j: (0, 4 + j)), pl.BlockSpec((4, 128), lambda j: (0, j)),
                  pl.BlockSpec((S, 128), lambda j: (0, j))],
        out_specs=[pl.BlockSpec((S, 128), lambda j: (0, j)), pl.BlockSpec((4, 128), lambda j: (0, j))],
        out_shape=[jax.ShapeDtypeStruct((S, 1536), F32), jax.ShapeDtypeStruct((4, 1536), F32)],
        scratch_shapes=[pltpu.VMEM((S + 8, 128), F32), pltpu.VMEM((S + 8, 128), F32)],
        compiler_params=_cp("parallel"),
    )(pm, w, dout)


GDN_SUP = 4
GDN_ROWS = GDN_SUP * CHUNK


@jax.custom_vjp
def _saved_inverse(a, x):
    return x


_saved_inverse.defvjp(lambda a, x: (x, x),
                      lambda x, g: (-mm_tn(x, mm_nt(g, x)), jnp.zeros_like(x)))


def _gdn_chunk(q, k, v, gate, al, bl, alog, dtb, og, state, inv=None, want_inv=False):
    R = q.shape[0]
    r = lax.broadcasted_iota(jnp.int32, (R, R), 0)
    c = lax.broadcasted_iota(jnp.int32, (R, R), 1)
    same = (r // CHUNK) == (c // CHUNK)
    eye = (r == c).astype(F32)
    strict, causal, upper = same & (r > c), same & (r >= c), same & (r <= c)
    qn = q * lax.rsqrt(jnp.sum(q * q, axis=-1, keepdims=True) + EPS) * (GDN_D ** -0.5)
    kn = k * lax.rsqrt(jnp.sum(k * k, axis=-1, keepdims=True) + EPS)
    beta = jnp.sum(eye * jax.nn.sigmoid(bl), axis=1, keepdims=True)
    g_row = -jnp.exp(alog) * _softplus(al + dtb)
    g = jnp.sum(eye * g_row, axis=1, keepdims=True)
    gc_col = jnp.sum(jnp.where(causal, g_row, 0.0), axis=1, keepdims=True)
    gc_row = jnp.sum(jnp.where(upper, g, 0.0), axis=0, keepdims=True)
    gtot = jnp.sum(jnp.where(same, g_row, 0.0), axis=1, keepdims=True)
    gamma = jnp.exp(gc_col)
    diff = gc_col - gc_row
    d_strict = jnp.where(strict, jnp.exp(jnp.where(strict, diff, 0.0)), 0.0)
    d_causal = jnp.where(causal, jnp.exp(jnp.where(causal, diff, 0.0)), 0.0)
    a = beta * mm_nt(kn, kn) * d_strict
    if inv is None:
        p = -a
        x = eye + p
        for _ in range(5):
            p = mm(p, p)
            x = x + mm(x, p)
    else:
        x = _saved_inverse(a, inv)
    u_new = mm(x, beta * v)
    w_k = mm(x, (beta * gamma) * kn)
    qk = mm_nt(qn, kn) * d_causal
    q_g = qn * gamma
    k_tail = kn * jnp.exp(gtot - gc_col)
    ws, os_ = [], []
    for i in range(R // CHUNK):
        rows = slice(CHUNK * i, CHUNK * (i + 1))
        w_i = u_new[rows] - mm(w_k[rows], state)
        os_.append(mm(q_g[rows], state))
        decay = jnp.exp(jnp.sum(g[rows], axis=0, keepdims=True))
        state = decay * state + mm_tn(k_tail[rows], w_i)
        ws.append(w_i)
    o = jnp.concatenate(os_, axis=0) + mm(qk, jnp.concatenate(ws, axis=0))
    out = _rms(o, og) * (gate * jax.nn.sigmoid(gate))
    return (out, state, x) if want_inv else (out, state)


def _gdn_specs(nc, rev):
    def ci(c):
        return nc - 1 - c if rev else c

    def blk(cb):
        return pl.BlockSpec((GDN_ROWS, 512), lambda c: (ci(c), cb))

    col = lambda n: pl.BlockSpec((n, GDN_ROWS), lambda c: (0, ci(c)))
    sc = pl.BlockSpec((GDN_H, 1, 1), lambda c: (0, 0, 0))
    og = pl.BlockSpec((1, 128), lambda c: (0, 0))
    st = pl.BlockSpec((GDN_H, None, 128, 128), lambda c: (0, ci(c), 0, 0))
    return blk, col, sc, og, st


def gdn_fwd(qkvc, pm, abt, alog, dtb, og):
    S = qkvc.shape[0]
    nc = S // GDN_ROWS
    blk, col, sc, ogs, st = _gdn_specs(nc, False)

    def body(q_ref, k_ref, v_ref, gate_ref, ab_ref, alog_ref, dtb_ref, og_ref, o_ref, st_ref, inv_ref, state):
        c = pl.program_id(0)

        @pl.when(c == 0)
        def _():
            state[...] = jnp.zeros_like(state)

        st_ref[...] = state[...]
        for h in range(GDN_H):
            sl = slice(GDN_D * h, GDN_D * (h + 1))
            out, new_state, inv = _gdn_chunk(
                q_ref[:, sl], k_ref[:, sl], v_ref[:, sl], gate_ref[:, sl], ab_ref[h:h + 1, :], ab_ref[GDN_H + h:GDN_H + h + 1, :],
                alog_ref[h], dtb_ref[h], og_ref[...], state[h], want_inv=True)
            o_ref[:, sl] = out
            state[h] = new_state
            inv_ref[h] = inv

    inv_spec = pl.BlockSpec((GDN_H, None, GDN_ROWS, GDN_ROWS), lambda c: (0, c, 0, 0))
    return pl.pallas_call(
        body, name="gdn_fwd", grid=(nc,),
        in_specs=[blk(0), blk(1), blk(2), blk(4), col(2 * GDN_H), sc, sc, ogs],
        out_specs=[blk(0), st, inv_spec],
        out_shape=[jax.ShapeDtypeStruct((S, 512), F32), jax.ShapeDtypeStruct((GDN_H, nc, 128, 128), F32),
                   jax.ShapeDtypeStruct((GDN_H, nc, GDN_ROWS, GDN_ROWS), F32)],
        scratch_shapes=[pltpu.VMEM((GDN_H, 128, 128), F32)],
        compiler_params=_cp("arbitrary"),
    )(qkvc, qkvc, qkvc, pm, abt, alog, dtb, og)


def gdn_bwd(qkvc, pm, abt, alog, dtb, og, st, inv, dout):
    S = qkvc.shape[0]
    nc = S // GDN_ROWS
    blk, col, sc, ogs, sts = _gdn_specs(nc, True)

    def body(q_ref, k_ref, v_ref, gate_ref, ab_ref, alog_ref, dtb_ref, og_ref, st_ref, inv_ref, do_ref,
             dq_ref, dk_ref, dv_ref, dgate_ref, dal_ref, dbl_ref, dalog_ref, ddtb_ref, dog_ref, dstate):
        c = pl.program_id(0)

        @pl.when(c == 0)
        def _():
            dstate[...] = jnp.zeros_like(dstate)

        for h in range(GDN_H):
            sl = slice(GDN_D * h, GDN_D * (h + 1))
            args = (q_ref[:, sl], k_ref[:, sl], v_ref[:, sl], gate_ref[:, sl], ab_ref[h:h + 1, :], ab_ref[GDN_H + h:GDN_H + h + 1, :],
                    alog_ref[h], dtb_ref[h], og_ref[...], st_ref[h])
            inv_h = inv_ref[h]
            _, vjp = jax.vjp(lambda *a, inv_h=inv_h: _gdn_chunk(*a, inv=inv_h), *args)
            g = vjp((do_ref[:, sl], dstate[h]))
            for o, gv in zip((dq_ref, dk_ref, dv_ref, dgate_ref), g[:4]):
                o[:, sl] = gv
            dal_ref[h:h + 1, :] = g[4]
            dbl_ref[h:h + 1, :] = g[5]
            dstate[h] = g[9]
            for o, gv in zip((dalog_ref, ddtb_ref, dog_ref), g[6:9]):
                @pl.when(c == 0)
                def _(o=o, gv=gv, h=h):
                    o[h] = gv

                @pl.when(c > 0)
                def _(o=o, gv=gv, h=h):
                    o[h] += gv

    ogo = pl.BlockSpec((GDN_H, 1, 128), lambda c: (0, 0, 0))
    inv_spec = pl.BlockSpec((GDN_H, None, GDN_ROWS, GDN_ROWS), lambda c: (0, nc - 1 - c, 0, 0))
    sd = jax.ShapeDtypeStruct
    return pl.pallas_call(
        body, name="gdn_bwd", grid=(nc,),
        in_specs=[blk(0), blk(1), blk(2), blk(4), col(2 * GDN_H), sc, sc, ogs, sts, inv_spec, blk(0)],
        out_specs=[blk(0), blk(0), blk(0), blk(0), col(GDN_H), col(GDN_H), sc, sc, ogo],
        out_shape=[sd((S, 512), F32)] * 4 + [sd((GDN_H, S), F32)] * 2 + [sd((GDN_H, 1, 1), F32)] * 2
        + [sd((GDN_H, 1, 128), F32)],
        scratch_shapes=[pltpu.VMEM((GDN_H, 128, 128), F32)],
        compiler_params=_cp("arbitrary"),
    )(qkvc, qkvc, qkvc, pm, abt, alog, dtb, og, st, inv, dout)


HG = 8
HG_LANES = HG * CA_D


def _group_mean_raw(y):
    r = lax.broadcasted_iota(jnp.int32, (128, 128), 0)
    c = lax.broadcasted_iota(jnp.int32, (128, 128), 1)
    g = jnp.where((r // CA_D) == (c // CA_D), 1.0 / CA_D, 0.0).astype(BF16)
    d = lambda u: lax.dot_general(u, g, (((1,), (0,)), ((), ())), preferred_element_type=F32)
    outs = []
    for j in range(y.shape[1] // 128):
        hi, lo = _split(y[:, 128 * j:128 * (j + 1)])
        outs.append(d(hi) + d(lo))
    return jnp.concatenate(outs, axis=1)


@jax.custom_vjp
def group_mean(y):
    return _group_mean_raw(y)


group_mean.defvjp(lambda y: (_group_mean_raw(y), None), lambda _, g: (_group_mean_raw(g),))


def f_headnorm(t, g):
    return (t * lax.rsqrt(group_mean(t * t) + EPS) * g,)


def _cattn_chunk(q, kb, vb, bias, valid):
    lane = lax.broadcasted_iota(jnp.int32, (1, 128), 1)
    m0 = (lane < CA_D).astype(F32)
    m1 = 1.0 - m0
    pairs = range(HG // 2)
    sl = [slice(128 * p, 128 * (p + 1)) for p in pairs]
    q2 = [jnp.concatenate([q[:, s] * m0, q[:, s] * m1], axis=0) for s in sl]
    sc = [mm_nt(q2[p], kb[:, sl[p]]) * (CA_D ** -0.5) + bias[sl[p]] for p in pairs]
    pr = [_softmax(jnp.where(valid, s, -1e30)) for s in sc]
    o2 = [mm(pr[p], vb[:, sl[p]]) for p in pairs]
    return jnp.concatenate([o[:CHUNK] * m0 + o[CHUNK:] * m1 for o in o2], axis=1)


def _cattn_valid(c):
    pos = lax.broadcasted_iota(jnp.int32, (1, CA_BAND), 1) + c * CHUNK
    return pos >= CA_PAD


def _cattn_specs(S):
    q_spec = pl.BlockSpec((CHUNK, HG_LANES), lambda h, c: (c, h))
    kv_spec = pl.BlockSpec((S + CA_PAD, HG_LANES), lambda h, c: (0, h))
    b_spec = pl.BlockSpec((HG * CHUNK, CA_BAND), lambda h, c: (h, 0))
    return q_spec, kv_spec, b_spec


def cattn_fwd(qn, kp, vp, bias):
    S = qn.shape[0]
    nc = S // CHUNK
    q_spec, kv_spec, b_spec = _cattn_specs(S)

    def body(q_ref, k_ref, v_ref, b_ref, o_ref):
        c = pl.program_id(1)
        start = pl.multiple_of(c * CHUNK, CHUNK)
        kb = k_ref[pl.ds(start, CA_BAND), :]
        vb = v_ref[pl.ds(start, CA_BAND), :]
        o_ref[...] = _cattn_chunk(q_ref[...], kb, vb, b_ref[...], _cattn_valid(c)).astype(o_ref.dtype)

    return pl.pallas_call(
        body, name="cattn_fwd", grid=(CA_H // HG, nc), in_specs=[q_spec, kv_spec, kv_spec, b_spec],
        out_specs=q_spec, out_shape=jax.ShapeDtypeStruct((S, D), BF16),
        compiler_params=_cp("parallel", "arbitrary"),
    )(qn, kp, vp, bias)


def kv_prep(qkv, kg):
    S = qkv.shape[0]
    tile = ROW_TILE
    lead = CA_PAD // tile

    def body(k_ref, v_ref, g_ref, kp_ref, vp_ref):
        i = pl.program_id(0)

        @pl.when(i < lead)
        def _():
            kp_ref[...] = jnp.zeros_like(kp_ref)
            vp_ref[...] = jnp.zeros_like(vp_ref)

        @pl.when(i >= lead)
        def _():
            kp_ref[...] = f_headnorm(k_ref[...], g_ref[...])[0].astype(BF16)
            vp_ref[...] = v_ref[...].astype(BF16)

    src = lambda cb: pl.BlockSpec((tile, D), lambda i, cb=cb: (jnp.maximum(i - lead, 0), cb))
    out = pl.BlockSpec((tile, D), lambda i: (i, 0))
    return pl.pallas_call(
        body, name="kv_prep", grid=((S + CA_PAD) // tile,),
        in_specs=[src(1), src(2), pl.BlockSpec((1, D), lambda i: (0, 0))], out_specs=[out, out],
        out_shape=[jax.ShapeDtypeStruct((S + CA_PAD, D), BF16)] * 2,
        compiler_params=_cp("parallel"),
    )(qkv, qkv, kg)


def cattn_bwd(qn, kp, vp, bias, do):
    S = qn.shape[0]
    nc = S // CHUNK
    q_spec, kv_spec, b_spec = _cattn_specs(S)

    def body(q_ref, k_ref, v_ref, b_ref, do_ref, dq_ref, dk_ref, dv_ref, db_ref):
        c = pl.program_id(1)

        @pl.when(c == 0)
        def _():
            dk_ref[...] = jnp.zeros_like(dk_ref)
            dv_ref[...] = jnp.zeros_like(dv_ref)
            db_ref[...] = jnp.zeros_like(db_ref)

        start = pl.multiple_of(c * CHUNK, CHUNK)
        kb = k_ref[pl.ds(start, CA_BAND), :].astype(F32)
        vb = v_ref[pl.ds(start, CA_BAND), :].astype(F32)
        valid = _cattn_valid(c)
        _, vjp = jax.vjp(lambda q, k, v, b: _cattn_chunk(q, k, v, b, valid), q_ref[...], kb, vb, b_ref[...])
        dq, dk, dv, db = vjp(do_ref[...])
        dq_ref[...] = dq
        dk_ref[pl.ds(start, CA_BAND), :] += dk
        dv_ref[pl.ds(start, CA_BAND), :] += dv
        db_ref[...] += db

    sd = jax.ShapeDtypeStruct
    return pl.pallas_call(
        body, name="cattn_bwd", grid=(CA_H // HG, nc), in_specs=[q_spec, kv_spec, kv_spec, b_spec, q_spec],
        out_specs=[q_spec, kv_spec, kv_spec, b_spec],
        out_shape=[sd((S, D), F32), sd((S + CA_PAD, D), F32), sd((S + CA_PAD, D), F32),
                   sd((CA_H * CHUNK, CA_BAND), F32)],
        compiler_params=_cp("parallel", "arbitrary"),
    )(qn, kp, vp, bias, do)


SKEW_W = CA_BAND + CHUNK


def rel_bias_grad(dbias):
    padded = jnp.pad(dbias, ((0, 0), (0, 0), (CHUNK, 0)))
    flat = jnp.pad(padded.reshape(CA_H, CHUNK * SKEW_W), ((0, 0), (0, CHUNK)))
    skew = flat.reshape(CA_H, CHUNK, SKEW_W + 1)

    first_near = SKEW_W - CHUNK - MAX_REL

    def fn(t):
        colsum = jnp.sum(t, axis=1, keepdims=True)
        j = lax.broadcasted_iota(jnp.int32, colsum.shape, 2)
        far = jnp.sum(jnp.where(j < first_near, colsum, 0.0), axis=2, keepdims=True)
        return (colsum + jnp.where(j == first_near, far, 0.0),)

    (colsum,) = whole("relbias_sum", fn, [skew], [((CA_H, 1, SKEW_W + 1), F32)])
    near = colsum[:, 0, first_near:SKEW_W][:, ::-1]
    return jnp.concatenate([jnp.zeros((CA_H, CHUNK + 1), F32), near], axis=1)


def rel_bias_expand(rb):
    near = rb[:, CHUNK + 1:][:, ::-1]
    far = jnp.broadcast_to(rb[:, 2 * MAX_REL:], (CA_H, SKEW_W - CHUNK - MAX_REL))
    t = jnp.concatenate([far, near, jnp.zeros((CA_H, 1), rb.dtype)], axis=1)
    rows = jnp.tile(t, (1, CHUNK))[:, :CHUNK * SKEW_W].reshape(CA_H, CHUNK, SKEW_W)
    return rows[:, :, CHUNK:]


def loss_head(y, target):
    S = y.shape[0]
    tile = min(ROW_TILE, S)

    def body(y_ref, t_ref, dy_ref, acc_ref):
        i = pl.program_id(0)
        e = y_ref[...] - t_ref[...]
        dy_ref[...] = e * (1.0 / D)
        part = jnp.sum(e * e, axis=0, keepdims=True) * (0.5 / D)

        @pl.when(i == 0)
        def _():
            acc_ref[...] = part

        @pl.when(i > 0)
        def _():
            acc_ref[...] += part

    row = pl.BlockSpec((tile, D), lambda i: (i, 0))
    return pl.pallas_call(
        body, name="loss_head", grid=(S // tile,), in_specs=[row, row],
        out_specs=[row, pl.BlockSpec((1, D), lambda i: (0, 0))],
        out_shape=[jax.ShapeDtypeStruct((S, D), F32), jax.ShapeDtypeStruct((1, D), F32)],
        compiler_params=_cp("arbitrary"),
    )(y, target)


ANY = pl.BlockSpec(memory_space=pl.ANY)


HBM = pl.BlockSpec(memory_space=pltpu.HBM)
SEM = pl.BlockSpec(memory_space=pltpu.SEMAPHORE)
EFFECT = pltpu.SideEffectType.DATAFLOW_SIDE_EFFECTING


def _chip_copies(src_ref, land_ref, send_sems, recv_sems, mode):
    x, y, c = lax.axis_index("x"), lax.axis_index("y"), lax.axis_index("c")
    me = 2 * x + y

    def copy(j, s, d, dev):
        return pltpu.make_async_remote_copy(src_ref=s, dst_ref=d, send_sem=send_sems.at[j], recv_sem=recv_sems.at[j],
                                            device_id=dev, device_id_type=MESH)

    if mode == "sibling":
        cp = copy(0, src_ref, land_ref, (x, y, 1 - c))
        return [(cp, cp)]
    out = []
    for j, (px, py) in enumerate([(1 - x, y), (x, 1 - y), (1 - x, 1 - y)]):
        peer = 2 * px + py
        if mode == "scatter":
            pairs, dev = ((src_ref.at[peer], land_ref.at[j]), (src_ref.at[me], land_ref.at[j])), (px, py, c)
        else:
            half = land_ref.shape[1] // 2
            mine = pl.ds(pl.multiple_of(c * half, 16), half)
            theirs = pl.ds(pl.multiple_of((1 - c) * half, 16), half)
            if mode == "gather_half":
                pairs = ((src_ref.at[mine], land_ref.at[me, mine]), (src_ref.at[mine], land_ref.at[peer, mine]))
                dev = (px, py, c)
            else:
                pairs = ((land_ref.at[peer, mine],) * 2, (land_ref.at[peer, theirs],) * 2)
                dev = (x, y, 1 - c)
        out.append(tuple(copy(j, s, d, dev) for s, d in pairs))
    return out


def exchange_start(src, land, carry, name, mode):
    def body(src_ref, land_ref, carry_ref, send_sems, recv_sems, src_out, land_out, carry_out):
        for send, _ in _chip_copies(src_ref, land_ref, send_sems, recv_sems, mode):
            send.start()

    hbm = lambda a: pltpu.HBM(a.shape, a.dtype)
    n = 1 if mode == "sibling" else 3
    return pl.pallas_call(
        body, name=name,
        out_shape=(pltpu.SemaphoreType.DMA((n,)), pltpu.SemaphoreType.DMA((n,)), hbm(src), hbm(land), hbm(carry)),
        in_specs=(HBM, HBM, HBM), out_specs=(SEM, SEM, HBM, HBM, HBM),
        input_output_aliases={0: 2, 1: 3, 2: 4},
        compiler_params=pltpu.CompilerParams(has_side_effects=EFFECT),
    )(pltpu.with_memory_space_constraint(src, pltpu.HBM), pltpu.with_memory_space_constraint(land, pltpu.HBM),
      pltpu.with_memory_space_constraint(carry, pltpu.HBM))


def exchange_wait(src, land, send_sems, recv_sems, after, name, mode):
    def body(src_ref, land_ref, send_sems_ref, recv_sems_ref, after_ref, src_out, land_out):
        for send, recv in _chip_copies(src_ref, land_ref, send_sems_ref, recv_sems_ref, mode):
            send.wait_send()
            recv.wait_recv()

    hbm = lambda a: pltpu.HBM(a.shape, a.dtype)
    return pl.pallas_call(
        body, name=name, out_shape=(hbm(src), hbm(land)),
        in_specs=(HBM, HBM, SEM, SEM, ANY), out_specs=(HBM, HBM), input_output_aliases={0: 0, 1: 1},
        compiler_params=pltpu.CompilerParams(has_side_effects=EFFECT),
    )(src, land, send_sems, recv_sems, after)


def sibling_exchange(srcs, name):
    n = len(srcs)

    def body(*refs):
        src_refs, out_refs, send_sems, recv_sems = refs[:n], refs[n:2 * n], refs[2 * n], refs[2 * n + 1]
        x, y, c = lax.axis_index("x"), lax.axis_index("y"), lax.axis_index("c")
        copies = [pltpu.make_async_remote_copy(src_ref=s, dst_ref=o, send_sem=send_sems.at[k], recv_sem=recv_sems.at[k],
                                               device_id=(x, y, 1 - c), device_id_type=MESH)
                  for k, (s, o) in enumerate(zip(src_refs, out_refs))]
        for cp in copies:
            cp.start()
        for cp in copies:
            cp.wait()

    return pl.pallas_call(
        body, name=name, in_specs=[ANY] * n, out_specs=[ANY] * n,
        out_shape=[jax.ShapeDtypeStruct(s.shape, s.dtype) for s in srcs],
        scratch_shapes=[pltpu.SemaphoreType.DMA((n,)), pltpu.SemaphoreType.DMA((n,))],
    )(*srcs)


def chip_gather(src, name):
    def body(src_ref, out_ref, send_sems, recv_sems, local_sem):
        x, y, c = lax.axis_index("x"), lax.axis_index("y"), lax.axis_index("c")
        me = 2 * x + y
        local = pltpu.make_async_copy(src_ref, out_ref.at[me], local_sem)
        local.start()
        peers = [(1 - x, y), (x, 1 - y), (1 - x, 1 - y)]
        mk = lambda j, px, py, slab: pltpu.make_async_remote_copy(
            src_ref=src_ref, dst_ref=out_ref.at[slab], send_sem=send_sems.at[j], recv_sem=recv_sems.at[j],
            device_id=(px, py, c), device_id_type=MESH)
        sends = [mk(j, px, py, me) for j, (px, py) in enumerate(peers)]
        for cp in sends:
            cp.start()
        for j, (px, py) in enumerate(peers):
            mk(j, px, py, 2 * px + py).wait_recv()
        for cp in sends:
            cp.wait_send()
        local.wait()

    return pl.pallas_call(
        body, name=name, in_specs=[ANY], out_specs=ANY,
        out_shape=jax.ShapeDtypeStruct((N_CHIPS,) + tuple(src.shape), src.dtype),
        scratch_shapes=[pltpu.SemaphoreType.DMA((3,)), pltpu.SemaphoreType.DMA((3,)), pltpu.SemaphoreType.DMA],
    )(src)


def sum_own_slabs(src, chip, land, name, tile=512):
    _, R, C = src.shape
    n = land.shape[0]
    tile = _tile(R, tile, 16)

    def body(chip_ref, o_ref, t_ref, out_ref):
        acc = o_ref[...].astype(F32)
        for s in range(n):
            acc = acc + t_ref[s].astype(F32)
        out_ref[...] = acc

    return pl.pallas_call(
        body, name=name,
        grid_spec=pltpu.PrefetchScalarGridSpec(
            num_scalar_prefetch=1, grid=(R // tile,),
            in_specs=[pl.BlockSpec((None, tile, C), lambda i, chip_ref: (chip_ref[0], i, 0)),
                      pl.BlockSpec((n, tile, C), lambda i, chip_ref: (0, i, 0))],
            out_specs=pl.BlockSpec((tile, C), lambda i, chip_ref: (i, 0))),
        out_shape=jax.ShapeDtypeStruct((R, C), F32),
        compiler_params=_cp("parallel"),
    )(jnp.reshape(chip, (1,)).astype(jnp.int32), src, land)


def sum_slabs(t, name, tile=512):
    n, R, C = t.shape
    tile = _tile(R, tile, 16)

    def body(t_ref, o_ref):
        acc = t_ref[0].astype(F32)
        for s in range(1, n):
            acc = acc + t_ref[s].astype(F32)
        o_ref[...] = acc

    return pl.pallas_call(
        body, name=name, grid=(R // tile,), in_specs=[pl.BlockSpec((n, tile, C), lambda i: (0, i, 0))],
        out_specs=pl.BlockSpec((tile, C), lambda i: (i, 0)), out_shape=jax.ShapeDtypeStruct((R, C), F32),
        compiler_params=_cp("parallel"),
    )(t)


def _pad_rows(a, mult=16):
    r = (-a.shape[0]) % mult
    return jnp.pad(a, ((0, r), (0, 0))) if r else a


BIG = ["ab_w_in", "c_w_qkv", "xa_w_kv", "f_w_gate", "f_w_up", "ab_w_out", "c_w_out", "xa_w_q", "xa_w_out",
       "f_w_down", "s5_w_glu"]


GROUPS = ("B", "A")
GROUP_ROW_MULT = 64


def group_spec(layer, grp):
    i = layer // 2
    if grp == "A":
        return [("xa_w_kv", layer, True, 512), ("xa_w_q", layer, False, 256), ("xa_w_out", layer, False, 256),
                ("f_w_gate", layer, True, 704), ("f_w_up", layer, True, 704), ("f_w_down", layer, False, 704)]
    if layer % 2 == 0:
        return [("ab_w_in", i, True, 642), ("ab_w_out", i, False, 256), ("s5_w_glu", i, False, 64)]
    return [("c_w_qkv", i, True, 768), ("c_w_out", i, False, 256)]


def _seg_rows(rows):
    return rows + ((-rows) % 16)


def _f32_rows(a):
    bits = lax.bitcast_convert_type(a.reshape(-1), BF16).reshape(-1)
    return jnp.pad(bits, (0, 16 * D - bits.shape[0])).reshape(16, D)


def pack_group_shards(p, layer, grp):
    segs = []
    for name, idx, transposed, rows in group_spec(layer, grp):
        w = p[name][idx]
        if transposed:
            w = w.T
        segs.append(_pad_rows(w.astype(BF16).reshape(-1, D)))
    if grp == "B":
        small = p["gdn_conv_w"] if layer % 2 == 0 else p["c_norm_g"]
        segs.append(_f32_rows(small[layer // 2]))
    return _pad_rows(jnp.concatenate(segs, axis=0), GROUP_ROW_MULT)


def unpack_group_gathered(g, layer, grp):
    out, off = {}, 0
    for name, idx, transposed, rows in group_spec(layer, grp):
        seg = g[:, off:off + rows]
        if name == "s5_w_glu":
            out[name] = seg.reshape(N_CHIPS * 128, 512)
        else:
            out[name] = seg.reshape(N_CHIPS * rows, D)
        off += _seg_rows(rows)
    if grp == "B":
        n = 4 * 384 if layer % 2 == 0 else 256
        bits = g[:, off:off + 16].reshape(N_CHIPS, -1)[:, :2 * n].reshape(N_CHIPS, n, 2)
        small = lax.bitcast_convert_type(bits, F32)
        if layer % 2 == 0:
            out["gdn_conv_w"] = jnp.swapaxes(small.reshape(N_CHIPS, 4, 384), 0, 1).reshape(4, 1536)
        else:
            out["c_norm_g"] = small.reshape(D)
    return out


def pack_group_grads(gr, layer, grp):
    segs = []
    for name, idx, transposed, rows in group_spec(layer, grp):
        w = gr[name].astype(BF16)
        seg = w.reshape(N_CHIPS, rows, D)
        r = (-rows) % 16
        if r:
            seg = jnp.pad(seg, ((0, 0), (0, r), (0, 0)))
        segs.append(seg)
    out = jnp.concatenate(segs, axis=1)
    return jnp.pad(out, ((0, 0), (0, (-out.shape[1]) % GROUP_ROW_MULT), (0, 0)))


def unpack_group_reduced(g, layer, grp):
    out, off = {}, 0
    for name, idx, transposed, rows in group_spec(layer, grp):
        seg = g[:, off:off + rows]
        if name == "s5_w_glu":
            out[name] = seg.reshape(-1, 128, 512)
        else:
            out[name] = jnp.swapaxes(seg, 1, 2) if (transposed and name not in ADAM_TRANSPOSED) else seg
        off += _seg_rows(rows)
    return out


def sum_cores_into(buf, idx, part, other, name):
    R, C = part.shape
    tile = _tile(R, 512, 16)

    def body(a_ref, b_ref, buf_ref, out_ref):
        out_ref[...] = a_ref[...] + b_ref[...]

    row = pl.BlockSpec((tile, C), lambda i: (i, 0))
    return pl.pallas_call(
        body, name=name, grid=(R // tile,), in_specs=[row, row, ANY],
        out_specs=pl.BlockSpec((None, tile, C), lambda i: (idx, i, 0)),
        out_shape=jax.ShapeDtypeStruct(buf.shape, F32), input_output_aliases={2: 0},
        compiler_params=_cp("arbitrary"),
    )(part, other, buf)


ADAM_TRANSPOSED = ("f_w_gate", "f_w_up")


SMALL = ["ab_norm_g", "s5_a_re", "s5_a_im", "s5_log_dt", "s5_b_re", "s5_b_im", "s5_c_re", "s5_c_im", "s5_d",
         "s5_b_glu", "gdn_conv_w", "gdn_a_log", "gdn_dt_bias", "gdn_out_norm_g", "c_norm_g", "c_q_norm_g",
         "c_k_norm_g", "c_rel_bias", "mem_norm_g", "xa_norm_g", "xa_q_norm_g", "xa_k_norm_g", "f_norm_g"]


def _lane_rows(a):
    flat = a.reshape(-1).astype(F32)
    return jnp.pad(flat, (0, (-flat.shape[0]) % 1024)).reshape(-1, 128)


def pack_small(d, extra=None):
    parts = [_lane_rows(d[n]) for n in SMALL]
    if extra is not None:
        parts.append(_lane_rows(extra))
    rows = jnp.concatenate(parts, axis=0)
    return jnp.pad(rows, ((0, (-rows.shape[0]) % 128), (0, 0)))


def unpack_small(rows, shapes):
    out, off = {}, 0
    for n in SMALL:
        sz = int(np.prod(shapes[n]))
        k = 8 * -(-sz // 1024)
        out[n] = rows[off:off + k].reshape(-1)[:sz].reshape(shapes[n])
        off += k
    return out, rows[off:]


def _s5_blockdiag_b(b):
    bt = jnp.swapaxes(b, 1, 2).reshape(S5_GB, 8, S5_C, S5_P)
    eye = jnp.eye(8, dtype=b.dtype)
    return jnp.einsum("bgcp,gh->bgchp", bt, eye).reshape(S5_GB, 8 * S5_C, 8 * S5_P)


def _s5_blockdiag_c(c):
    ct = jnp.swapaxes(c, 1, 2).reshape(S5_GB, 8, S5_P, S5_C)
    eye = jnp.eye(8, dtype=c.dtype)
    return jnp.einsum("bgpc,gh->bgphc", ct, eye).reshape(S5_GB, 8 * S5_P, 8 * S5_C)


def _s5_diag_b(db):
    t = db.reshape(S5_GB, 8, S5_C, 8, S5_P)
    t = jnp.transpose(t, (0, 2, 4, 1, 3)).reshape(S5_GB, S5_C, S5_P, 64)
    d = t[..., ::9]
    return jnp.transpose(d, (0, 3, 2, 1)).reshape(S5_G, S5_P, S5_C)


def _s5_diag_c(dc):
    t = dc.reshape(S5_GB, 8, S5_P, 8, S5_C)
    t = jnp.transpose(t, (0, 2, 4, 1, 3)).reshape(S5_GB, S5_P, S5_C, 64)
    d = t[..., ::9]
    return jnp.transpose(d, (0, 3, 2, 1)).reshape(S5_G, S5_C, S5_P)


def local_step(x, mem, target, p, wsrc, gsink, wpre=lambda layer, grp, carry: carry):
    S = x.shape[0]
    row2 = lambda a: a.reshape(1, -1)
    saved = []
    (mem_n,) = row_fwd("mem_norm", f_norm, [mem], [row2(p["mem_norm_g"])], [(D, BF16)])
    gs = {}

    for layer in range(DEPTH):
        i = layer // 2
        w = wsrc(layer, "B", x)
        sv = {"x0": x, "wB": w}
        if layer % 2 == 0:
            (h,) = row_fwd("norm", f_norm, [x], [row2(p["ab_norm_g"][i])], [(D, BF16)])
            w_in = w["ab_w_in"]
            pm = matmul("nt", h, w_in[:2560], "proj_main")
            abt = matmul("nt", w_in[2560:], h, "proj_ab")
            s5p = dict(
                are=p["s5_a_re"][i].reshape(1, -1), aim=p["s5_a_im"][i].reshape(1, -1),
                ldt=jnp.broadcast_to(p["s5_log_dt"][i][:, None], (S5_G, S5_P)).reshape(1, -1),
                b_re=_s5_blockdiag_b(p["s5_b_re"][i]), b_im=_s5_blockdiag_b(p["s5_b_im"][i]),
                c_re=_s5_blockdiag_c(p["s5_c_re"][i]), c_im=_s5_blockdiag_c(p["s5_c_im"][i]),
                dv=p["s5_d"][i].reshape(1, -1))
            y5, st5 = s5_fwd(pm, **s5p)
            (a_out,) = row_fwd("glu", f_glu, [y5], [w["s5_w_glu"], row2(p["s5_b_glu"][i])], [(S5_W, F32)])
            conv_w = w["gdn_conv_w"]
            qkvc = conv_fwd(pm, conv_w)
            alog = p["gdn_a_log"][i].reshape(GDN_H, 1, 1)
            dtb = p["gdn_dt_bias"][i].reshape(GDN_H, 1, 1)
            og = row2(p["gdn_out_norm_g"][i])
            b_out, stg, inv = gdn_fwd(qkvc, pm, abt, alog, dtb, og)
            cat = wpre(layer, "A", jnp.concatenate([a_out, b_out], axis=1))
            x, hq = matmul("nn", cat, w["ab_w_out"], "mix_out", add=x, norm_out=row2(p["xa_norm_g"][layer]))
            sv.update(h=h, pm=pm, s5p=s5p, y5=y5, st5=st5, qkvc=qkvc, abt=abt, alog=alog, dtb=dtb, og=og,
                      stg=stg, inv=inv, cat=cat, conv_w=conv_w)
        else:
            (h,) = row_fwd("norm", f_norm, [x], [row2(w["c_norm_g"])], [(D, BF16)])
            qkv = matmul("nt", h, w["c_w_qkv"], "proj_qkv")
            qg = jnp.tile(row2(p["c_q_norm_g"][i]), (1, CA_H))
            kg = jnp.tile(row2(p["c_k_norm_g"][i]), (1, CA_H))
            (qn,) = row_fwd("headnorm_q", f_headnorm, [(qkv, D, 0)], [qg], [(D, F32)])
            kp, vp = kv_prep(qkv, kg)
            bias = rel_bias_expand(p["c_rel_bias"][i]).reshape(CA_H * CHUNK, CA_BAND)
            o = wpre(layer, "A", cattn_fwd(qn, kp, vp, bias))
            x, hq = matmul("nn", o, w["c_w_out"], "mix_out", add=x, norm_out=row2(p["xa_norm_g"][layer]))
            sv.update(h=h, qkv=qkv, qg=qg, kg=kg, qn=qn, kp=kp, vp=vp, bias=bias, o=o)
        sv["x1"] = x
        w = wsrc(layer, "A", x)
        sv["wA"] = w
        qx = matmul("nn", hq, w["xa_w_q"], "xa_q")
        kv = matmul("nt", mem_n, w["xa_w_kv"], "xa_kv")
        xqg, xkg = row2(p["xa_q_norm_g"][layer]), row2(p["xa_k_norm_g"][layer])
        (ox,) = row_fwd("xattn", f_xattn, [qx], [kv, xqg, xkg], [(D, BF16)], tile=2 * ROW_TILE)
        x, hf = matmul("nn", ox, w["xa_w_out"], "xa_out", add=x, norm_out=row2(p["f_norm_g"][layer]))
        sv.update(hq=hq, qx=qx, kv=kv, ox=ox)
        sv["x2"] = x
        gate, up, act = ffn_in(hf, w["f_w_gate"], w["f_w_up"])
        if layer + 1 < DEPTH:
            act = wpre(layer + 1, "B", act)
        x = matmul("nn", act, w["f_w_down"], "ffn_down", add=x)
        sv.update(hf=hf, gate=gate, up=up, act=act)
        saved.append(sv)

    dx, loss_vec = loss_head(x, target)

    dmem_n = None
    for layer in reversed(range(DEPTH)):
        i = layer // 2
        sv = saved[layer]
        w, gw = sv["wA"], {}
        dgate, dup = ffn_dact(dx, w["f_w_down"], sv["gate"], sv["up"])
        gw["f_w_down"] = matmul("tn", sv["act"], dx, "ffn_dwd", out_dtype=BF16)
        gw["f_w_gate"] = matmul("tn", dgate, sv["hf"], "ffn_dwg", out_dtype=BF16)
        gw["f_w_up"] = matmul("tn", dup, sv["hf"], "ffn_dwu", out_dtype=BF16)
        dh = matmul("nn", dgate, w["f_w_gate"], "ffn_dhg")
        dx, dg = matmul("nn", dup, w["f_w_up"], "ffn_dhu", add=dh,
                        norm_bwd=(sv["x2"], row2(p["f_norm_g"][layer]), dx))
        gs.setdefault("f_norm_g", [None] * DEPTH)[layer] = dg[0]
        do = matmul("nt", dx, w["xa_w_out"], "xa_do")
        gw["xa_w_out"] = matmul("tn", sv["ox"], dx, "xa_dwo", out_dtype=BF16)
        xqg, xkg = row2(p["xa_q_norm_g"][layer]), row2(p["xa_k_norm_g"][layer])
        dqx, dkv, dqg, dkg = row_bwd("xattn_bwd", f_xattn, [sv["qx"]], [sv["kv"], xqg, xkg], [do],
                                     [0], [0, 1, 2], tile=2 * ROW_TILE)
        gs.setdefault("xa_q_norm_g", [None] * DEPTH)[layer] = dqg[0]
        gs.setdefault("xa_k_norm_g", [None] * DEPTH)[layer] = dkg[0]
        gw["xa_w_q"] = matmul("tn", sv["hq"], dqx, "xa_dwq", out_dtype=BF16)
        gw["xa_w_kv"] = matmul("tn", dkv, mem_n, "xa_dwkv", out_dtype=BF16)
        dx = gsink(layer, "A", gw, dx)
        dmem_n = matmul("nn", dkv, w["xa_w_kv"], "xa_dmem", add=dmem_n)
        dx, dg = matmul("nt", dqx, w["xa_w_q"], "xa_dhq", norm_bwd=(sv["x1"], row2(p["xa_norm_g"][layer]), dx))
        gs.setdefault("xa_norm_g", [None] * DEPTH)[layer] = dg[0]
        w, gw = sv["wB"], {}
        if layer % 2 == 0:
            dcat = matmul("nt", dx, w["ab_w_out"], "mix_dcat")
            gw["ab_w_out"] = matmul("tn", sv["cat"], dx, "mix_dwo", out_dtype=BF16)
            dy5, dwglu, dbglu = row_bwd("glu_bwd", f_glu, [sv["y5"]], [w["s5_w_glu"], row2(p["s5_b_glu"][i])],
                                        [(dcat, S5_W, 0)], [0], [0, 1])
            gw["s5_w_glu"] = dwglu
            gs.setdefault("s5_b_glu", [None] * 2)[i] = dbglu[0]
            s5p = sv["s5p"]
            du, dare, daim, dldt, dbre, dbim, dcre, dcim, ddv = s5_bwd(sv["pm"], sv["st5"], dy5, **s5p)
            (dldt_g,) = whole("s5_dt_sum", lambda t: (jnp.sum(t, axis=1, keepdims=True),),
                              [dldt.reshape(S5_G, S5_P)], [((S5_G, 1), F32)])
            for nme, val in (("s5_a_re", dare.reshape(S5_G, S5_P)), ("s5_a_im", daim.reshape(S5_G, S5_P)),
                             ("s5_log_dt", dldt_g[:, 0]), ("s5_b_re", _s5_diag_b(dbre)),
                             ("s5_b_im", _s5_diag_b(dbim)), ("s5_c_re", _s5_diag_c(dcre)),
                             ("s5_c_im", _s5_diag_c(dcim)), ("s5_d", ddv.reshape(S5_G, S5_C))):
                gs.setdefault(nme, [None] * 2)[i] = val
            dq, dk, dv, dgate, dal, dbl, dalog, ddtb, dog = gdn_bwd(
                sv["qkvc"], sv["pm"], sv["abt"], sv["alog"], sv["dtb"], sv["og"], sv["stg"], sv["inv"],
                dcat[:, S5_W:])
            (dog_s,) = whole("gdn_og_sum", lambda t: (jnp.sum(t, axis=0, keepdims=True),),
                             [dog.reshape(GDN_H, GDN_D)], [((1, GDN_D), F32)])
            gs.setdefault("gdn_out_norm_g", [None] * 2)[i] = dog_s[0]
            gs.setdefault("gdn_a_log", [None] * 2)[i] = dalog.reshape(GDN_H)
            gs.setdefault("gdn_dt_bias", [None] * 2)[i] = ddtb.reshape(GDN_H)
            dqkvc = jnp.concatenate([dq, dk, dv], axis=1)
            dqkv, dconv = conv_bwd(sv["pm"], sv["conv_w"], dqkvc)
            gs.setdefault("gdn_conv_w", [None] * 2)[i] = dconv
            dpm = jnp.concatenate([du, dqkv, dgate], axis=1).astype(BF16)
            dabt = jnp.concatenate([dal, dbl], axis=0)
            dw_main = matmul("tn", dpm, sv["h"], "proj_dw", out_dtype=BF16)
            dw_ab = matmul("nn", dabt, sv["h"], "proj_ab_dw", out_dtype=BF16)
            gw["ab_w_in"] = jnp.concatenate([dw_main, dw_ab], axis=0)
            dx = gsink(layer, "B", gw, dx)
            w_in = w["ab_w_in"]
            dh = matmul("nn", dpm, w_in[:2560], "proj_dh")
            dx, dg = matmul("tn", dabt, w_in[2560:], "proj_ab_dh", add=dh,
                            norm_bwd=(sv["x0"], row2(p["ab_norm_g"][i]), dx))
            gs.setdefault("ab_norm_g", [None] * 2)[i] = dg[0]
        else:
            do = matmul("nt", dx, w["c_w_out"], "mix_dcat")
            gw["c_w_out"] = matmul("tn", sv["o"], dx, "mix_dwo", out_dtype=BF16)
            dqn, dkp, dvp, dbias = cattn_bwd(sv["qn"], sv["kp"], sv["vp"], sv["bias"], do)
            gs.setdefault("c_rel_bias", [None] * 2)[i] = rel_bias_grad(dbias.reshape(CA_H, CHUNK, CA_BAND))
            dq, dqg = row_bwd("headnorm_bwd", f_headnorm, [(sv["qkv"], D, 0)], [sv["qg"]], [dqn], [0], [0])
            dk, dkg = row_bwd("headnorm_bwd", f_headnorm, [(sv["qkv"], D, 1)], [sv["kg"]],
                              [(dkp, D, 0, CA_PAD)], [0], [0])
            head_sum = lambda t: (jnp.sum(t, axis=0, keepdims=True),)
            (dqg,) = whole("headgain_sum", head_sum, [dqg.reshape(CA_H, CA_D)], [((1, CA_D), F32)])
            (dkg,) = whole("headgain_sum", head_sum, [dkg.reshape(CA_H, CA_D)], [((1, CA_D), F32)])
            gs.setdefault("c_q_norm_g", [None] * 2)[i] = dqg[0]
            gs.setdefault("c_k_norm_g", [None] * 2)[i] = dkg[0]
            dqkv = jnp.concatenate([dq, dk, dvp[CA_PAD:]], axis=1).astype(BF16)
            gw["c_w_qkv"] = matmul("tn", dqkv, sv["h"], "proj_qkv_dw", out_dtype=BF16)
            dx = gsink(layer, "B", gw, dx)
            dx, dg = matmul("nn", dqkv, w["c_w_qkv"], "proj_qkv_dh",
                            norm_bwd=(sv["x0"], row2(w["c_norm_g"]), dx))
            gs.setdefault("c_norm_g", [None] * 2)[i] = dg[0]
    (dmg,) = row_bwd("mem_norm_bwd", f_norm, [mem], [row2(p["mem_norm_g"])], [dmem_n], [], [0])
    small = {n: jnp.stack(v) for n, v in gs.items()}
    small["mem_norm_g"] = dmg[0]
    return loss_vec, dx, small


ADAM_BLOCK_BYTES = 3 << 19


def adam(w, g, m, v, name):
    shape = w.shape
    if w.ndim == 3:
        d0, n, d2 = shape
        fits = [t for t in range(8, n + 1, 8) if n % t == 0 and d0 * t * d2 * 4 <= ADAM_BLOCK_BYTES]
        return tuple(row_fwd(name, f_adam, [w, g, m, v], [], [((d0, d2), F32)] * 3, tile=max(fits)))
    cols = shape[-1]
    w2, g2, m2, v2 = (t.reshape(-1, cols) for t in (w, g, m, v))
    rows = w2.shape[0]
    tile = rows if rows <= 512 else _tile(rows, 512, 8)
    outs = row_fwd(name, f_adam, [w2, g2, m2, v2], [], [(cols, F32)] * 3, tile=tile)
    return tuple(o.reshape(shape) for o in outs)


WEIGHTS = ['ab_norm_g', 'ab_w_in', 'ab_w_out', 's5_a_re', 's5_a_im', 's5_log_dt', 's5_b_re', 's5_b_im', 's5_c_re',
           's5_c_im', 's5_d', 's5_w_glu', 's5_b_glu', 'gdn_conv_w', 'gdn_a_log', 'gdn_dt_bias', 'gdn_out_norm_g',
           'c_norm_g', 'c_w_qkv', 'c_w_out', 'c_q_norm_g', 'c_k_norm_g', 'c_rel_bias', 'mem_norm_g', 'xa_norm_g',
           'xa_w_q', 'xa_w_kv', 'xa_w_out', 'xa_q_norm_g', 'xa_k_norm_g', 'f_norm_g', 'f_w_gate', 'f_w_up',
           'f_w_down']
SHARDED_SMALL = {"gdn_conv_w": (2, 384), "c_norm_g": (1, 256)}


def kernel(x, mem, ab_norm_g, ab_w_in, ab_w_out, s5_a_re, s5_a_im, s5_log_dt, s5_b_re, s5_b_im, s5_c_re, s5_c_im, s5_d, s5_w_glu, s5_b_glu, gdn_conv_w, gdn_a_log, gdn_dt_bias, gdn_out_norm_g, c_norm_g, c_w_qkv, c_w_out, c_q_norm_g, c_k_norm_g, c_rel_bias, mem_norm_g, xa_norm_g, xa_w_q, xa_w_kv, xa_w_out, xa_q_norm_g, xa_k_norm_g, f_norm_g, f_w_gate, f_w_up, f_w_down, loss_target, m_ab_norm_g, m_ab_w_in, m_ab_w_out, m_s5_a_re, m_s5_a_im, m_s5_log_dt, m_s5_b_re, m_s5_b_im, m_s5_c_re, m_s5_c_im, m_s5_d, m_s5_w_glu, m_s5_b_glu, m_gdn_conv_w, m_gdn_a_log, m_gdn_dt_bias, m_gdn_out_norm_g, m_c_norm_g, m_c_w_qkv, m_c_w_out, m_c_q_norm_g, m_c_k_norm_g, m_c_rel_bias, m_mem_norm_g, m_xa_norm_g, m_xa_w_q, m_xa_w_kv, m_xa_w_out, m_xa_q_norm_g, m_xa_k_norm_g, m_f_norm_g, m_f_w_gate, m_f_w_up, m_f_w_down, v_ab_norm_g, v_ab_w_in, v_ab_w_out, v_s5_a_re, v_s5_a_im, v_s5_log_dt, v_s5_b_re, v_s5_b_im, v_s5_c_re, v_s5_c_im, v_s5_d, v_s5_w_glu, v_s5_b_glu, v_gdn_conv_w, v_gdn_a_log, v_gdn_dt_bias, v_gdn_out_norm_g, v_c_norm_g, v_c_w_qkv, v_c_w_out, v_c_q_norm_g, v_c_k_norm_g, v_c_rel_bias, v_mem_norm_g, v_xa_norm_g, v_xa_w_q, v_xa_w_kv, v_xa_w_out, v_xa_q_norm_g, v_xa_k_norm_g, v_f_norm_g, v_f_w_gate, v_f_w_up, v_f_w_down):
    args = locals()
    p = {n: args[n] for n in WEIGHTS}
    m = {n: args["m_" + n] for n in WEIGHTS}
    v = {n: args["v_" + n] for n in WEIGHTS}
    chip = 2 * lax.axis_index("x") + lax.axis_index("y")

    carry = x[0]
    gathers = {}
    for layer in range(DEPTH):
        for grp in GROUPS:
            src = pack_group_shards(p, layer, grp)
            land = lax.dynamic_update_slice(lax.empty((N_CHIPS,) + src.shape, BF16), src[None], (chip, 0, 0))
            send_sems, recv_sems, src, land, carry = exchange_start(
                src, land, carry, f"gather_start_{layer}{grp}", mode="gather_half")
            gathers[layer, grp] = (src, land, send_sems, recv_sems)
    forwards = {}

    def wpre(layer, grp, carry):
        if (layer, grp) not in forwards:
            src, land, send_sems, recv_sems = gathers[layer, grp]
            src, land = exchange_wait(src, land, send_sems, recv_sems, carry, f"gather_wait_{layer}{grp}",
                                      mode="gather_half")
            send_sems, recv_sems, src, land, carry = exchange_start(
                src, land, carry, f"forward_start_{layer}{grp}", mode="forward_half")
            forwards[layer, grp] = (src, land, send_sems, recv_sems)
        return carry

    def wsrc(layer, grp, after):
        wpre(layer, grp, after)
        src, land, send_sems, recv_sems = forwards[layer, grp]
        _, land = exchange_wait(src, land, send_sems, recv_sems, after, f"forward_wait_{layer}{grp}",
                                mode="forward_half")
        return unpack_group_gathered(land, layer, grp)

    scatters, siblings = [], []
    LAG = 2

    def finish(carry):
        layer, grp, src, land, send_sems, recv_sems = scatters[len(siblings)]
        src, land = exchange_wait(src, land, send_sems, recv_sems, carry, f"scatter_wait_{layer}{grp}", mode="scatter")
        part = sum_own_slabs(src, chip, land, "sum_chips")
        send_sems, recv_sems, part, other, carry = exchange_start(
            part, lax.empty(part.shape, F32), carry, f"sibling_start_{layer}{grp}", mode="sibling")
        siblings.append((layer, grp, part, other, send_sems, recv_sems))
        return carry

    def gsink(layer, grp, gw, carry):
        src = pack_group_grads(gw, layer, grp)
        land = lax.empty((3,) + src.shape[1:], BF16)
        send_sems, recv_sems, src, land, carry = exchange_start(
            src, land, carry, f"scatter_start_{layer}{grp}", mode="scatter")
        scatters.append((layer, grp, src, land, send_sems, recv_sems))
        if len(scatters) > LAG:
            carry = finish(carry)
        return carry

    carry = wpre(0, "B", carry)
    loss_vec, dx, g_small = local_step(carry, mem[0], loss_target[0], p, wsrc, gsink, wpre)

    full_shapes = {n: ((2, 4, 1536) if n == "gdn_conv_w" else (2, D) if n == "c_norm_g" else p[n].shape)
                   for n in SMALL}
    small_mine = pack_small(g_small, extra=loss_vec)
    (small_other,) = sibling_exchange([small_mine], "sibling_small")
    (small_chip,) = row_fwd("sum_small_cores", lambda a, b: (a + b,), [small_mine, small_other], [], [(128, F32)],
                            tile=128)
    small_sum = sum_slabs(chip_gather(small_chip, "gather_small"), "sum_small")
    g_s, rest = unpack_small(small_sum, full_shapes)
    (loss11,) = whole("loss_sum", lambda t: (jnp.sum(jnp.sum(t, axis=1, keepdims=True), axis=0, keepdims=True),),
                      [rest[:D // 128]], [((1, 1), F32)])
    for n, (axis, width) in SHARDED_SMALL.items():
        g_s[n] = lax.dynamic_slice_in_dim(g_s[n], chip * width, width, axis=axis)
    delta, new_m, new_v = {}, {}, {}
    for n in SMALL:
        shape = p[n].shape
        two = (1, shape[0]) if len(shape) == 1 else (int(np.prod(shape[:-1])), shape[-1])
        outs = whole("adam_" + n, f_adam, [t.reshape(two) for t in (p[n], g_s[n], m[n], v[n])], [(two, F32)] * 3)
        delta[n], new_m[n], new_v[n] = (o.reshape(shape) for o in outs)

    while len(siblings) < len(scatters):
        dx = finish(dx)

    totals = {}
    for layer, grp, part, other, send_sems, recv_sems in siblings:
        part, other = exchange_wait(part, other, send_sems, recv_sems, dx, f"sibling_wait_{layer}{grp}",
                                    mode="sibling")
        kind = (grp, layer % 2) if grp == "B" else (grp, 0)
        n_layers, idx = (DEPTH // 2, layer // 2) if grp == "B" else (DEPTH, layer)
        if kind not in totals:
            totals[kind] = lax.empty((n_layers,) + part.shape, F32)
        totals[kind] = sum_cores_into(totals[kind], idx, part, other, "sum_cores")
    grads = dict(g_s)
    for (grp, first_layer), total in totals.items():
        grads.update(unpack_group_reduced(total, first_layer, grp))
    for name in BIG:
        if name in ADAM_TRANSPOSED:
            t = lambda a: jnp.swapaxes(a, 1, 2)
            outs = adam(t(p[name]), grads[name], t(m[name]), t(v[name]), "adam_" + name)
            delta[name], new_m[name], new_v[name] = (t(o) for o in outs)
            grads[name] = t(grads[name])
        else:
            delta[name], new_m[name], new_v[name] = adam(p[name], grads[name], m[name], v[name], "adam_" + name)

    return (loss11[0, 0], dx[None], *[grads[n] for n in WEIGHTS], *[delta[n] for n in WEIGHTS],
            *[new_m[n] for n in WEIGHTS], *[new_v[n] for n in WEIGHTS])
```

```python
import numpy as np
import jax
import jax.numpy as jnp
from jax import lax
from jax.experimental import pallas as pl
from jax.experimental.pallas import tpu as pltpu

F32 = jnp.float32
BF16 = jnp.bfloat16
MESH = pl.DeviceIdType.MESH

D = 1024
CHUNK = 64
EPS = 1e-6
S5_W = 512
S5_G = 32
S5_C = 16
S5_P = 64
S5_GB = 4
GDN_H = 4
GDN_D = 128
CA_H = 16
CA_D = 64
CA_LEFT = 8
CA_BAND = (CA_LEFT + 1) * CHUNK
CA_PAD = CA_LEFT * CHUNK
MAX_REL = 128
XA_H = 4
XA_D = 256
FFN = 2816
DEPTH = 4
N_CHIPS = 4
LR, B1, B2, AEPS, WD, STEP = 0.001, 0.9, 0.999, 1e-08, 0.01, 10

VMEM_LIMIT = 56 * 1024 * 1024
ROW_TILE = 256


def _cp(*sem):
    return pltpu.CompilerParams(dimension_semantics=sem, vmem_limit_bytes=VMEM_LIMIT)


def _dg(a, b, ca, cb):
    return lax.dot_general(a.astype(BF16), b.astype(BF16), (((ca,), (cb,)), ((), ())),
                           preferred_element_type=F32)


@jax.custom_vjp
def mm(a, b):
    return _dg(a, b, 1, 0)


@jax.custom_vjp
def mm_nt(a, b):
    return _dg(a, b, 1, 1)


@jax.custom_vjp
def mm_tn(a, b):
    return _dg(a, b, 0, 0)


mm.defvjp(lambda a, b: (mm(a, b), (a, b)), lambda r, g: (mm_nt(g, r[1]), mm_tn(r[0], g)))
mm_nt.defvjp(lambda a, b: (mm_nt(a, b), (a, b)), lambda r, g: (mm(g, r[1]), mm_tn(g, r[0])))
mm_tn.defvjp(lambda a, b: (mm_tn(a, b), (a, b)), lambda r, g: (mm_nt(r[1], g), mm(r[0], g)))


def _split(a):
    hi = a.astype(BF16)
    return hi, (a - hi.astype(F32)).astype(BF16)


def _tri_mm(v, upper):
    T = v.shape[0]
    r = lax.broadcasted_iota(jnp.int32, (T, T), 0)
    c = lax.broadcasted_iota(jnp.int32, (T, T), 1)
    m = ((c >= r) if upper else (r >= c)).astype(BF16)
    hi, lo = _split(v)
    d = lambda u: lax.dot_general(m, u, (((1,), (0,)), ((), ())), preferred_element_type=F32)
    return d(hi) + d(lo)


@jax.custom_vjp
def cumsum_rows(v):
    return _tri_mm(v, False)


cumsum_rows.defvjp(lambda v: (_tri_mm(v, False), None), lambda _, g: (_tri_mm(g, True),))


def _rms(x, g):
    return x * lax.rsqrt(jnp.mean(x * x, axis=-1, keepdims=True) + EPS) * g


def _softmax(s):
    e = jnp.exp(s - lax.stop_gradient(jnp.max(s, axis=-1, keepdims=True)))
    return e / jnp.sum(e, axis=-1, keepdims=True)


def _softplus(x):
    return jnp.maximum(x, 0.0) + jnp.log(1.0 + jnp.exp(-jnp.abs(x)))


def _tile(n, cap, align):
    if n <= cap:
        return n
    best = None
    for d in range(align, cap + 1, align):
        if n % d == 0:
            best = d
    assert best is not None, (n, cap, align)
    return best


def matmul(mode, a, b, name, out_dtype=F32, add=None, norm_out=None, norm_bwd=None):
    if mode == "nn":
        (M, K), (K2, N) = a.shape, b.shape
    elif mode == "nt":
        (M, K), (N, K2) = a.shape, b.shape
    else:
        (K, M), (K2, N) = a.shape, b.shape
    assert K == K2, (mode, a.shape, b.shape)
    if mode != "tn" and norm_out is None and norm_bwd is None:
        tm, tn, tk = _tile(M, 1024, 128), _tile(N, 1024, 128), _tile(K, 2048, 128)
    else:
        tm, tn, tk = _tile(M, 512, 128), _tile(N, 1536, 128), _tile(K, 2048, 128)
    nk = K // tk
    if mode == "nn":
        a_spec = pl.BlockSpec((tm, tk), lambda i, j, k: (i, k))
        b_spec = pl.BlockSpec((tk, tn), lambda i, j, k: (k, j))
        dn = (((1,), (0,)), ((), ()))
    elif mode == "nt":
        a_spec = pl.BlockSpec((tm, tk), lambda i, j, k: (i, k))
        b_spec = pl.BlockSpec((tn, tk), lambda i, j, k: (j, k))
        dn = (((1,), (1,)), ((), ()))
    else:
        a_spec = pl.BlockSpec((tk, tm), lambda i, j, k: (k, i))
        b_spec = pl.BlockSpec((tk, tn), lambda i, j, k: (k, j))
        dn = (((0,), (0,)), ((), ()))
    o_spec = pl.BlockSpec((tm, tn), lambda i, j, k: (i, j))
    has_add = add is not None
    g_spec = pl.BlockSpec((1, tn), lambda i, j, k: (0, j))
    extra, extra_specs, out_shapes, out_specs = [], [], [jax.ShapeDtypeStruct((M, N), out_dtype)], [o_spec]
    sem = ("parallel", "parallel", "arbitrary")
    if norm_out is not None:
        assert tn == N
        extra, extra_specs = [norm_out], [g_spec]
        out_shapes.append(jax.ShapeDtypeStruct((M, N), BF16))
        out_specs.append(o_spec)
    if norm_bwd is not None:
        assert tn == N
        x_in, g_in, res_in = norm_bwd
        extra, extra_specs = [x_in, g_in, res_in], [o_spec, g_spec, o_spec]
        out_shapes.append(jax.ShapeDtypeStruct((1, N), F32))
        out_specs.append(g_spec)
        sem = ("arbitrary", "arbitrary", "arbitrary")
    n_in = 2 + int(has_add) + len(extra)
    n_out = len(out_shapes)

    def body(*refs):
        a_ref, b_ref = refs[0], refs[1]
        add_ref = refs[2] if has_add else None
        extra_refs = refs[2 + int(has_add):n_in]
        o_ref = refs[n_in]
        acc_ref = refs[-1]
        i = pl.program_id(0)
        p = lax.dot_general(a_ref[...].astype(BF16), b_ref[...].astype(BF16), dn,
                            preferred_element_type=F32)

        def finish(total):
            if has_add:
                total = total + add_ref[...]
            if norm_out is not None:
                o_ref[...] = total.astype(o_ref.dtype)
                refs[n_in + 1][...] = _rms(total, extra_refs[0][...]).astype(BF16)
            elif norm_bwd is not None:
                x_ref, g_ref, res_ref = extra_refs
                _, vjp = jax.vjp(f_norm_res, x_ref[...], g_ref[...])
                dx, dg = vjp((total, res_ref[...]))
                o_ref[...] = dx
                dg_ref = refs[n_in + 1]

                @pl.when(i == 0)
                def _():
                    dg_ref[...] = dg

                @pl.when(i > 0)
                def _():
                    dg_ref[...] += dg
            else:
                o_ref[...] = total.astype(o_ref.dtype)

        if nk == 1:
            finish(p)
        else:
            k = pl.program_id(2)

            @pl.when(k == 0)
            def _():
                acc_ref[...] = p

            @pl.when(k > 0)
            def _():
                acc_ref[...] += p

            @pl.when(k == nk - 1)
            def _():
                finish(acc_ref[...])

    ins = [a, b] + ([add] if has_add else []) + extra
    specs = [a_spec, b_spec] + ([o_spec] if has_add else []) + extra_specs
    out = pl.pallas_call(
        body, name=name, grid=(M // tm, N // tn, nk), in_specs=specs, out_specs=out_specs,
        out_shape=out_shapes, scratch_shapes=[pltpu.VMEM((tm, tn), F32)],
        compiler_params=_cp(*sem),
    )(*ins)
    return out[0] if n_out == 1 else out


def ffn_in(h, wg, wu):
    (S, K), F = h.shape, wg.shape[0]
    tm, tn = _tile(S, 512, 128), _tile(F, 1536, 128)
    dn = (((1,), (1,)), ((), ()))

    def body(h_ref, wg_ref, wu_ref, g_ref, u_ref, act_ref):
        a = h_ref[...].astype(BF16)
        g = lax.dot_general(a, wg_ref[...].astype(BF16), dn, preferred_element_type=F32)
        u = lax.dot_general(a, wu_ref[...].astype(BF16), dn, preferred_element_type=F32)
        g_ref[...] = g.astype(BF16)
        u_ref[...] = u.astype(BF16)
        act_ref[...] = f_swiglu(g, u)[0].astype(BF16)

    w_spec = pl.BlockSpec((tn, K), lambda i, j: (j, 0))
    o_spec = pl.BlockSpec((tm, tn), lambda i, j: (i, j))
    sd = jax.ShapeDtypeStruct
    return pl.pallas_call(
        body, name="ffn_in", grid=(S // tm, F // tn),
        in_specs=[pl.BlockSpec((tm, K), lambda i, j: (i, 0)), w_spec, w_spec], out_specs=[o_spec] * 3,
        out_shape=[sd((S, F), BF16), sd((S, F), BF16), sd((S, F), BF16)],
        compiler_params=_cp("parallel", "parallel"),
    )(h, wg, wu)


def ffn_dact(dy, wd, gate, up):
    (S, K), F = dy.shape, wd.shape[0]
    tm, tn = _tile(S, 512, 128), _tile(F, 1536, 128)
    dn = (((1,), (1,)), ((), ()))

    def body(dy_ref, wd_ref, g_ref, u_ref, dg_ref, du_ref):
        dact = lax.dot_general(dy_ref[...].astype(BF16), wd_ref[...].astype(BF16), dn, preferred_element_type=F32)
        _, vjp = jax.vjp(f_swiglu, g_ref[...].astype(F32), u_ref[...].astype(F32))
        dg, du = vjp((dact,))
        dg_ref[...] = dg.astype(BF16)
        du_ref[...] = du.astype(BF16)

    o_spec = pl.BlockSpec((tm, tn), lambda i, j: (i, j))
    sd = jax.ShapeDtypeStruct
    return pl.pallas_call(
        body, name="ffn_dact", grid=(S // tm, F // tn),
        in_specs=[pl.BlockSpec((tm, K), lambda i, j: (i, 0)), pl.BlockSpec((tn, K), lambda i, j: (j, 0)),
                  o_spec, o_spec],
        out_specs=[o_spec] * 2, out_shape=[sd((S, F), BF16)] * 2,
        compiler_params=_cp("parallel", "parallel"),
    )(dy, wd, gate, up)


def _row_spec(arr, tile):
    if isinstance(arr, tuple):
        a, w, cb = arr[:3]
        ro = (arr[3] // tile) if len(arr) > 3 else 0
        assert len(arr) < 4 or arr[3] % tile == 0
        return a, pl.BlockSpec((tile, w), lambda i, cb=cb, ro=ro: (i + ro, cb)), (tile, w)
    if arr.ndim == 3:
        d0, _, d2 = arr.shape
        return arr, pl.BlockSpec((d0, tile, d2), lambda i: (0, i, 0)), (d0, tile, d2)
    return arr, pl.BlockSpec((tile, arr.shape[1]), lambda i: (i, 0)), (tile, arr.shape[1])


def _full_spec(arr):
    nd = arr.ndim
    return pl.BlockSpec(arr.shape, lambda i, nd=nd: (0,) * nd)


def _n_rows(arr):
    a = arr[0] if isinstance(arr, tuple) else arr
    return a.shape[1] if a.ndim == 3 else a.shape[0]


def _row_out_shape(shape_tail, n, dtype):
    if isinstance(shape_tail, tuple):
        d0, d2 = shape_tail
        return (jax.ShapeDtypeStruct((d0, n, d2), dtype),
                lambda tile: pl.BlockSpec((d0, tile, d2), lambda i: (0, i, 0)))
    return (jax.ShapeDtypeStruct((n, shape_tail), dtype),
            lambda tile: pl.BlockSpec((tile, shape_tail), lambda i: (i, 0)))


def _f32(v):
    return v.astype(F32) if v.dtype == BF16 else v


def row_fwd(name, fn, rows, fulls, outs, tile=ROW_TILE):
    n = _n_rows(rows[0])
    tile = min(tile, n)
    assert n % tile == 0, (name, n, tile)
    rs = [_row_spec(r, tile) for r in rows]
    os_ = [_row_out_shape(w, n, dt) for w, dt in outs]
    nr, nf = len(rows), len(fulls)

    def body(*refs):
        vals = [_f32(r[...]) for r in refs[:nr + nf]]
        res = fn(*vals)
        for r, v in zip(refs[nr + nf:], res):
            r[...] = v.astype(r.dtype)

    out = pl.pallas_call(
        body, name=name, grid=(n // tile,),
        in_specs=[s for _, s, _ in rs] + [_full_spec(f) for f in fulls],
        out_specs=[mk(tile) for _, mk in os_], out_shape=[sh for sh, _ in os_],
        compiler_params=_cp("parallel"),
    )(*[a for a, _, _ in rs], *fulls)
    return out


def row_bwd(name, fn, rows, fulls, cts, want_rows, want_fulls, row_dtypes=None, tile=ROW_TILE):
    n = _n_rows(rows[0])
    tile = min(tile, n)
    assert n % tile == 0, (name, n, tile)
    rs = [_row_spec(r, tile) for r in rows]
    cs = [_row_spec(c, tile) for c in cts]
    nr, nf, nc = len(rows), len(fulls), len(cts)
    row_dtypes = row_dtypes or [F32] * len(want_rows)
    out_shapes, out_specs = [], []
    for k, idx in enumerate(want_rows):
        a, _, blk = rs[idx]
        if len(blk) == 3:
            sh, mk = _row_out_shape((blk[0], blk[2]), n, row_dtypes[k])
        else:
            sh, mk = _row_out_shape(blk[1], n, row_dtypes[k])
        out_shapes.append(sh)
        out_specs.append(mk(tile))
    for idx in want_fulls:
        out_shapes.append(jax.ShapeDtypeStruct(fulls[idx].shape, F32))
        out_specs.append(_full_spec(fulls[idx]))
    n_wr = len(want_rows)

    def body(*refs):
        i = pl.program_id(0)
        vals = [_f32(r[...]) for r in refs[:nr + nf]]
        ct_vals = [_f32(r[...]) for r in refs[nr + nf:nr + nf + nc]]
        outs = refs[nr + nf + nc:]
        _, vjp = jax.vjp(fn, *vals)
        grads = vjp(tuple(ct_vals))
        for k, idx in enumerate(want_rows):
            outs[k][...] = grads[idx].astype(outs[k].dtype)
        for k, idx in enumerate(want_fulls):
            o = outs[n_wr + k]
            g = grads[nr + idx]

            @pl.when(i == 0)
            def _(o=o, g=g):
                o[...] = g

            @pl.when(i > 0)
            def _(o=o, g=g):
                o[...] += g

    out = pl.pallas_call(
        body, name=name, grid=(n // tile,),
        in_specs=[s for _, s, _ in rs] + [_full_spec(f) for f in fulls] + [s for _, s, _ in cs],
        out_specs=out_specs, out_shape=out_shapes,
        compiler_params=_cp("arbitrary"),
    )(*[a for a, _, _ in rs], *fulls, *[a for a, _, _ in cs])
    return out


def whole(name, fn, args, outs):
    def body(*refs):
        res = fn(*[r[...] for r in refs[:len(args)]])
        for r, v in zip(refs[len(args):], res):
            r[...] = v.astype(r.dtype)

    return pl.pallas_call(
        body, name=name, out_shape=[jax.ShapeDtypeStruct(s, d) for s, d in outs],
        compiler_params=pltpu.CompilerParams(vmem_limit_bytes=VMEM_LIMIT),
    )(*args)


def f_norm(x, g):
    return (_rms(x, g),)


def f_norm_res(x, g):
    return _rms(x, g), x


def f_swiglu(g, u):
    return (g * jax.nn.sigmoid(g) * u,)


def f_glu(y, w, b):
    h = jax.nn.gelu(y)
    return (h * jax.nn.sigmoid(mm(h, w) + b),)


def f_xattn(q, kv, qg, kg):
    outs = []
    for h in range(XA_H):
        sl = slice(h * XA_D, (h + 1) * XA_D)
        qn = _rms(q[:, sl], qg)
        kn = _rms(kv[:, sl], kg)
        vh = kv[:, D + h * XA_D:D + (h + 1) * XA_D]
        p = _softmax(mm_nt(qn, kn) * (XA_D ** -0.5))
        outs.append(mm(p, vh))
    return (jnp.concatenate(outs, axis=-1),)


def f_adam(w, g, m, v):
    m2 = B1 * m + (1.0 - B1) * g
    v2 = B2 * v + (1.0 - B2) * (g * g)
    m_hat = m2 / (1.0 - B1 ** STEP)
    v_hat = v2 / (1.0 - B2 ** STEP)
    delta = -LR * (m_hat / (jnp.sqrt(v_hat) + AEPS) + WD * w)
    return delta, m2, v2


S5_NTAB, S5_NROW = 4, 6


def _s5_tables(are, aim, ldt):
    T = CHUNK
    dt = jnp.exp(ldt)
    ar, ai = are * dt, aim * dt
    t = lax.broadcasted_iota(jnp.int32, (T, 1), 0).astype(F32)
    mag, inv = jnp.exp(t * ar), jnp.exp(-t * ar)
    cs, sn = jnp.cos(t * ai), jnp.sin(t * ai)
    e_re, e_im = mag * cs, mag * sn
    n_re, n_im = inv * cs, -inv * sn
    l_re, l_im = jnp.exp(ar) * jnp.cos(ai), jnp.exp(ar) * jnp.sin(ai)
    den = are * are + aim * aim
    k_re = ((l_re - 1.0) * are + l_im * aim) / den
    k_im = (l_im * are - (l_re - 1.0) * aim) / den
    tl = float(T - 1)
    m_re, m_im = jnp.exp(tl * ar) * jnp.cos(tl * ai), jnp.exp(tl * ar) * jnp.sin(tl * ai)
    return (e_re, e_im, n_re, n_im), (l_re, l_im, k_re, k_im, m_re, m_im)


def _s5_chunk(u, sre, sim, tabs, rows, b_re, b_im, c_re, c_im, dv):
    e_re, e_im, n_re, n_im = tabs
    l_re, l_im, k_re, k_im, m_re, m_im = rows
    x_re, x_im = mm(u, b_re), mm(u, b_im)
    bu_re = k_re * x_re - k_im * x_im
    bu_im = k_re * x_im + k_im * x_re
    v_re = bu_re * n_re - bu_im * n_im
    v_im = bu_re * n_im + bu_im * n_re
    p_re = l_re * sre - l_im * sim
    p_im = l_re * sim + l_im * sre
    w_re = cumsum_rows(v_re) + p_re
    w_im = cumsum_rows(v_im) + p_im
    s_re = e_re * w_re - e_im * w_im
    s_im = e_re * w_im + e_im * w_re
    y = mm(s_re, c_re) - mm(s_im, c_im) + dv * u
    z_re = jnp.sum(v_re, axis=0, keepdims=True) + p_re
    z_im = jnp.sum(v_im, axis=0, keepdims=True) + p_im
    return y, m_re * z_re - m_im * z_im, m_re * z_im + m_im * z_re


def _s5_fill_tables(are_ref, aim_ref, ldt_ref, tab, row):
    for g in range(S5_GB):
        ls = slice(512 * g, 512 * (g + 1))
        tabs, rows = _s5_tables(are_ref[:, ls], aim_ref[:, ls], ldt_ref[:, ls])
        for k, t in enumerate(tabs):
            tab[k, :, ls] = t
        for k, r in enumerate(rows):
            row[k:k + 1, ls] = r


def _s5_read_tables(tab, row, ls):
    return (tuple(tab[k, :, ls] for k in range(S5_NTAB)), tuple(row[k:k + 1, ls] for k in range(S5_NROW)))


def _s5_specs(nc, rev):
    T = CHUNK

    def ci(c):
        return nc - 1 - c if rev else c

    u_spec = pl.BlockSpec((T, S5_W), lambda c: (ci(c), 0))
    p_spec = pl.BlockSpec((1, S5_G * S5_P), lambda c: (0, 0))
    b_spec = pl.BlockSpec((S5_GB, 128, 512), lambda c: (0, 0, 0))
    c_spec = pl.BlockSpec((S5_GB, 512, 128), lambda c: (0, 0, 0))
    d_spec = pl.BlockSpec((1, S5_W), lambda c: (0, 0))
    st_spec = pl.BlockSpec((None, 2, S5_G * S5_P), lambda c: (ci(c), 0, 0))
    return u_spec, p_spec, b_spec, c_spec, d_spec, st_spec


def s5_fwd(pm, are, aim, ldt, b_re, b_im, c_re, c_im, dv):
    S = pm.shape[0]
    nc = S // CHUNK
    u_spec, p_spec, b_spec, c_spec, d_spec, st_spec = _s5_specs(nc, False)

    def body(u_ref, are_ref, aim_ref, ldt_ref, bre_ref, bim_ref, cre_ref, cim_ref, dv_ref,
             y_ref, st_ref, state, tab, row):
        c = pl.program_id(0)

        @pl.when(c == 0)
        def _():
            state[...] = jnp.zeros_like(state)
            _s5_fill_tables(are_ref, aim_ref, ldt_ref, tab, row)

        st_ref[...] = state[...]
        for g in range(S5_GB):
            lu, ls = slice(128 * g, 128 * (g + 1)), slice(512 * g, 512 * (g + 1))
            tabs, rows = _s5_read_tables(tab, row, ls)
            y, e_re, e_im = _s5_chunk(u_ref[:, lu], state[0:1, ls], state[1:2, ls], tabs, rows,
                                      bre_ref[g], bim_ref[g], cre_ref[g], cim_ref[g], dv_ref[:, lu])
            y_ref[:, lu] = y
            state[0:1, ls] = e_re
            state[1:2, ls] = e_im

    n_state = S5_G * S5_P
    return pl.pallas_call(
        body, name="s5_fwd", grid=(nc,),
        in_specs=[u_spec, p_spec, p_spec, p_spec, b_spec, b_spec, c_spec, c_spec, d_spec],
        out_specs=[u_spec, st_spec],
        out_shape=[jax.ShapeDtypeStruct((S, S5_W), F32), jax.ShapeDtypeStruct((nc, 2, n_state), F32)],
        scratch_shapes=[pltpu.VMEM((2, n_state), F32), pltpu.VMEM((S5_NTAB, CHUNK, n_state), F32),
                        pltpu.VMEM((8, n_state), F32)],
        compiler_params=_cp("arbitrary"),
    )(pm, are, aim, ldt, b_re, b_im, c_re, c_im, dv)


def s5_bwd(pm, st, dy, are, aim, ldt, b_re, b_im, c_re, c_im, dv):
    S = pm.shape[0]
    nc = S // CHUNK
    u_spec, p_spec, b_spec, c_spec, d_spec, st_spec = _s5_specs(nc, True)

    def body(u_ref, st_ref, dy_ref, are_ref, aim_ref, ldt_ref, bre_ref, bim_ref, cre_ref, cim_ref, dv_ref,
             du_ref, dare_ref, daim_ref, dldt_ref, dbre_ref, dbim_ref, dcre_ref, dcim_ref, ddv_ref,
             dstate, tab, row, dtab, drow):
        c = pl.program_id(0)

        @pl.when(c == 0)
        def _():
            dstate[...] = jnp.zeros_like(dstate)
            dtab[...] = jnp.zeros_like(dtab)
            drow[...] = jnp.zeros_like(drow)
            _s5_fill_tables(are_ref, aim_ref, ldt_ref, tab, row)

        for g in range(S5_GB):
            lu, ls = slice(128 * g, 128 * (g + 1)), slice(512 * g, 512 * (g + 1))
            every = slice(None)
            tabs, rows = _s5_read_tables(tab, row, ls)
            args = (u_ref[:, lu], st_ref[0:1, ls], st_ref[1:2, ls], tabs, rows,
                    bre_ref[g], bim_ref[g], cre_ref[g], cim_ref[g], dv_ref[:, lu])
            _, vjp = jax.vjp(_s5_chunk, *args)
            gr = vjp((dy_ref[:, lu], dstate[0:1, ls], dstate[1:2, ls]))
            du_ref[:, lu] = gr[0]
            dstate[0:1, ls] = gr[1]
            dstate[1:2, ls] = gr[2]
            for k, t in enumerate(gr[3]):
                dtab[k, :, ls] += t
            for k, r in enumerate(gr[4]):
                drow[k:k + 1, ls] += r
            accs = ((dbre_ref, (g,)), (dbim_ref, (g,)), (dcre_ref, (g,)), (dcim_ref, (g,)), (ddv_ref, (every, lu)))
            for (o, idx), gv in zip(accs, gr[5:]):
                @pl.when(c == 0)
                def _(o=o, idx=idx, gv=gv):
                    o[idx] = gv

                @pl.when(c > 0)
                def _(o=o, idx=idx, gv=gv):
                    o[idx] += gv

        @pl.when(c == nc - 1)
        def _():
            for g in range(S5_GB):
                ls = slice(512 * g, 512 * (g + 1))
                _, vjp = jax.vjp(_s5_tables, are_ref[:, ls], aim_ref[:, ls], ldt_ref[:, ls])
                dtabs, drows = _s5_read_tables(dtab, drow, ls)
                ga, gi, gl = vjp((dtabs, drows))
                dare_ref[:, ls] = ga
                daim_ref[:, ls] = gi
                dldt_ref[:, ls] = gl

    n_state = S5_G * S5_P
    return pl.pallas_call(
        body, name="s5_bwd", grid=(nc,),
        in_specs=[u_spec, st_spec, u_spec, p_spec, p_spec, p_spec, b_spec, b_spec, c_spec, c_spec, d_spec],
        out_specs=[u_spec, p_spec, p_spec, p_spec, b_spec, b_spec, c_spec, c_spec, d_spec],
        out_shape=[jax.ShapeDtypeStruct((S, S5_W), F32)] + [jax.ShapeDtypeStruct((1, n_state), F32)] * 3
        + [jax.ShapeDtypeStruct((S5_GB, 128, 512), F32)] * 2 + [jax.ShapeDtypeStruct((S5_GB, 512, 128), F32)] * 2
        + [jax.ShapeDtypeStruct((1, S5_W), F32)],
        scratch_shapes=[pltpu.VMEM((2, n_state), F32), pltpu.VMEM((S5_NTAB, CHUNK, n_state), F32),
                        pltpu.VMEM((8, n_state), F32), pltpu.VMEM((S5_NTAB, CHUNK, n_state), F32),
                        pltpu.VMEM((8, n_state), F32)],
        compiler_params=_cp("arbitrary"),
    )(pm, st, dy, are, aim, ldt, b_re, b_im, c_re, c_im, dv)


def conv_fwd(pm, w):
    S = pm.shape[0]

    def body(x_ref, w_ref, o_ref, pad):
        x = x_ref[...]
        pad[0:8, :] = jnp.zeros((8, 128), F32)
        pad[8:, :] = x
        y = (w_ref[3:4, :] * x + w_ref[2:3, :] * pad[7:7 + S, :] + w_ref[1:2, :] * pad[6:6 + S, :]
             + w_ref[0:1, :] * pad[5:5 + S, :])
        o_ref[...] = y * jax.nn.sigmoid(y)

    return pl.pallas_call(
        body, name="conv_fwd", grid=(12,),
        in_specs=[pl.BlockSpec((S, 128), lambda j: (0, 4 + j)), pl.BlockSpec((4, 128), lambda j: (0, j))],
        out_specs=pl.BlockSpec((S, 128), lambda j: (0, j)),
        out_shape=jax.ShapeDtypeStruct((S, 1536), F32),
        scratch_shapes=[pltpu.VMEM((S + 8, 128), F32)],
        compiler_params=_cp("parallel"),
    )(pm, w)


def conv_bwd(pm, w, dout):
    S = pm.shape[0]

    def body(x_ref, w_ref, do_ref, dx_ref, dw_ref, pad, dpad):
        x = x_ref[...]
        pad[0:8, :] = jnp.zeros((8, 128), F32)
        pad[8:, :] = x
        xs = [pad[5:5 + S, :], pad[6:6 + S, :], pad[7:7 + S, :], x]
        y = w_ref[0:1, :] * xs[0] + w_ref[1:2, :] * xs[1] + w_ref[2:3, :] * xs[2] + w_ref[3:4, :] * xs[3]
        sg = jax.nn.sigmoid(y)
        dy = do_ref[...] * (sg + y * sg * (1.0 - sg))
        dpad[0:S, :] = dy
        dpad[S:, :] = jnp.zeros((8, 128), F32)
        dx_ref[...] = (w_ref[3:4, :] * dy + w_ref[2:3, :] * dpad[1:1 + S, :] + w_ref[1:2, :] * dpad[2:2 + S, :]
                       + w_ref[0:1, :] * dpad[3:3 + S, :])
        for i in range(4):
            dw_ref[i:i + 1, :] = jnp.sum(dy * xs[i], axis=0, keepdims=True)

    return pl.pallas_call(
        body, name="conv_bwd", grid=(12,),
        in_specs=[pl.BlockSpec((S, 128), lambda j: (0, 4 + j)), pl.BlockSpec((4, 128), lambda j: (0, j)),
                  pl.BlockSpec((S, 128), lambda j: (0, j))],
        out_specs=[pl.BlockSpec((S, 128), lambda j: (0, j)), pl.BlockSpec((4, 128), lambda j: (0, j))],
        out_shape=[jax.ShapeDtypeStruct((S, 1536), F32), jax.ShapeDtypeStruct((4, 1536), F32)],
        scratch_shapes=[pltpu.VMEM((S + 8, 128), F32), pltpu.VMEM((S + 8, 128), F32)],
        compiler_params=_cp("parallel"),
    )(pm, w, dout)


GDN_SUP = 4
GDN_ROWS = GDN_SUP * CHUNK


@jax.custom_vjp
def _saved_inverse(a, x):
    return x


_saved_inverse.defvjp(lambda a, x: (x, x),
                      lambda x, g: (-mm_tn(x, mm_nt(g, x)), jnp.zeros_like(x)))


def _gdn_chunk(q, k, v, gate, al, bl, alog, dtb, og, state, inv=None, want_inv=False):
    R = q.shape[0]
    r = lax.broadcasted_iota(jnp.int32, (R, R), 0)
    c = lax.broadcasted_iota(jnp.int32, (R, R), 1)
    same = (r // CHUNK) == (c // CHUNK)
    eye = (r == c).astype(F32)
    strict, causal, upper = same & (r > c), same & (r >= c), same & (r <= c)
    qn = q * lax.rsqrt(jnp.sum(q * q, axis=-1, keepdims=True) + EPS) * (GDN_D ** -0.5)
    kn = k * lax.rsqrt(jnp.sum(k * k, axis=-1, keepdims=True) + EPS)
    beta = jnp.sum(eye * jax.nn.sigmoid(bl), axis=1, keepdims=True)
    g_row = -jnp.exp(alog) * _softplus(al + dtb)
    g = jnp.sum(eye * g_row, axis=1, keepdims=True)
    gc_col = jnp.sum(jnp.where(causal, g_row, 0.0), axis=1, keepdims=True)
    gc_row = jnp.sum(jnp.where(upper, g, 0.0), axis=0, keepdims=True)
    gtot = jnp.sum(jnp.where(same, g_row, 0.0), axis=1, keepdims=True)
    gamma = jnp.exp(gc_col)
    diff = gc_col - gc_row
    d_strict = jnp.where(strict, jnp.exp(jnp.where(strict, diff, 0.0)), 0.0)
    d_causal = jnp.where(causal, jnp.exp(jnp.where(causal, diff, 0.0)), 0.0)
    a = beta * mm_nt(kn, kn) * d_strict
    if inv is None:
        p = -a
        x = eye + p
        for _ in range(5):
            p = mm(p, p)
            x = x + mm(x, p)
    else:
        x = _saved_inverse(a, inv)
    u_new = mm(x, beta * v)
    w_k = mm(x, (beta * gamma) * kn)
    qk = mm_nt(qn, kn) * d_causal
    q_g = qn * gamma
    k_tail = kn * jnp.exp(gtot - gc_col)
    ws, os_ = [], []
    for i in range(R // CHUNK):
        rows = slice(CHUNK * i, CHUNK * (i + 1))
        w_i = u_new[rows] - mm(w_k[rows], state)
        os_.append(mm(q_g[rows], state))
        decay = jnp.exp(jnp.sum(g[rows], axis=0, keepdims=True))
        state = decay * state + mm_tn(k_tail[rows], w_i)
        ws.append(w_i)
    o = jnp.concatenate(os_, axis=0) + mm(qk, jnp.concatenate(ws, axis=0))
    out = _rms(o, og) * (gate * jax.nn.sigmoid(gate))
    return (out, state, x) if want_inv else (out, state)


def _gdn_specs(nc, rev):
    def ci(c):
        return nc - 1 - c if rev else c

    def blk(cb):
        return pl.BlockSpec((GDN_ROWS, 512), lambda c: (ci(c), cb))

    col = lambda n: pl.BlockSpec((n, GDN_ROWS), lambda c: (0, ci(c)))
    sc = pl.BlockSpec((GDN_H, 1, 1), lambda c: (0, 0, 0))
    og = pl.BlockSpec((1, 128), lambda c: (0, 0))
    st = pl.BlockSpec((GDN_H, None, 128, 128), lambda c: (0, ci(c), 0, 0))
    return blk, col, sc, og, st


def gdn_fwd(qkvc, pm, abt, alog, dtb, og):
    S = qkvc.shape[0]
    nc = S // GDN_ROWS
    blk, col, sc, ogs, st = _gdn_specs(nc, False)

    def body(q_ref, k_ref, v_ref, gate_ref, ab_ref, alog_ref, dtb_ref, og_ref, o_ref, st_ref, inv_ref, state):
        c = pl.program_id(0)

        @pl.when(c == 0)
        def _():
            state[...] = jnp.zeros_like(state)

        st_ref[...] = state[...]
        for h in range(GDN_H):
            sl = slice(GDN_D * h, GDN_D * (h + 1))
            out, new_state, inv = _gdn_chunk(
                q_ref[:, sl], k_ref[:, sl], v_ref[:, sl], gate_ref[:, sl], ab_ref[h:h + 1, :], ab_ref[GDN_H + h:GDN_H + h + 1, :],
                alog_ref[h], dtb_ref[h], og_ref[...], state[h], want_inv=True)
            o_ref[:, sl] = out
            state[h] = new_state
            inv_ref[h] = inv

    inv_spec = pl.BlockSpec((GDN_H, None, GDN_ROWS, GDN_ROWS), lambda c: (0, c, 0, 0))
    return pl.pallas_call(
        body, name="gdn_fwd", grid=(nc,),
        in_specs=[blk(0), blk(1), blk(2), blk(4), col(2 * GDN_H), sc, sc, ogs],
        out_specs=[blk(0), st, inv_spec],
        out_shape=[jax.ShapeDtypeStruct((S, 512), F32), jax.ShapeDtypeStruct((GDN_H, nc, 128, 128), F32),
                   jax.ShapeDtypeStruct((GDN_H, nc, GDN_ROWS, GDN_ROWS), F32)],
        scratch_shapes=[pltpu.VMEM((GDN_H, 128, 128), F32)],
        compiler_params=_cp("arbitrary"),
    )(qkvc, qkvc, qkvc, pm, abt, alog, dtb, og)


def gdn_bwd(qkvc, pm, abt, alog, dtb, og, st, inv, dout):
    S = qkvc.shape[0]
    nc = S // GDN_ROWS
    blk, col, sc, ogs, sts = _gdn_specs(nc, True)

    def body(q_ref, k_ref, v_ref, gate_ref, ab_ref, alog_ref, dtb_ref, og_ref, st_ref, inv_ref, do_ref,
             dqkv_ref, dgate_ref, dal_ref, dbl_ref, dalog_ref, ddtb_ref, dog_ref, dstate):
        c = pl.program_id(0)

        @pl.when(c == 0)
        def _():
            dstate[...] = jnp.zeros_like(dstate)

        for h in range(GDN_H):
            sl = slice(GDN_D * h, GDN_D * (h + 1))
            args = (q_ref[:, sl], k_ref[:, sl], v_ref[:, sl], gate_ref[:, sl], ab_ref[h:h + 1, :], ab_ref[GDN_H + h:GDN_H + h + 1, :],
                    alog_ref[h], dtb_ref[h], og_ref[...], st_ref[h])
            inv_h = inv_ref[h]
            _, vjp = jax.vjp(lambda *a, inv_h=inv_h: _gdn_chunk(*a, inv=inv_h), *args)
            g = vjp((do_ref[:, sl], dstate[h]))
            for part in range(3):
                dqkv_ref[:, slice(512 * part + sl.start, 512 * part + sl.stop)] = g[part]
            dgate_ref[:, sl] = g[3]
            dal_ref[h:h + 1, :] = g[4]
            dbl_ref[h:h + 1, :] = g[5]
            dstate[h] = g[9]
            for o, gv in zip((dalog_ref, ddtb_ref, dog_ref), g[6:9]):
                @pl.when(c == 0)
                def _(o=o, gv=gv, h=h):
                    o[h] = gv

                @pl.when(c > 0)
                def _(o=o, gv=gv, h=h):
                    o[h] += gv

    ogo = pl.BlockSpec((GDN_H, 1, 128), lambda c: (0, 0, 0))
    inv_spec = pl.BlockSpec((GDN_H, None, GDN_ROWS, GDN_ROWS), lambda c: (0, nc - 1 - c, 0, 0))
    sd = jax.ShapeDtypeStruct
    return pl.pallas_call(
        body, name="gdn_bwd", grid=(nc,),
        in_specs=[blk(0), blk(1), blk(2), blk(4), col(2 * GDN_H), sc, sc, ogs, sts, inv_spec, blk(0)],
        out_specs=[pl.BlockSpec((GDN_ROWS, 1536), lambda c: (nc - 1 - c, 0)), blk(0), col(GDN_H), col(GDN_H),
                   sc, sc, ogo],
        out_shape=[sd((S, 1536), F32), sd((S, 512), F32)] + [sd((GDN_H, S), F32)] * 2 + [sd((GDN_H, 1, 1), F32)] * 2
        + [sd((GDN_H, 1, 128), F32)],
        scratch_shapes=[pltpu.VMEM((GDN_H, 128, 128), F32)],
        compiler_params=_cp("arbitrary"),
    )(qkvc, qkvc, qkvc, pm, abt, alog, dtb, og, st, inv, dout)


HG = 8
HG_LANES = HG * CA_D


def _group_mean_raw(y):
    r = lax.broadcasted_iota(jnp.int32, (128, 128), 0)
    c = lax.broadcasted_iota(jnp.int32, (128, 128), 1)
    g = jnp.where((r // CA_D) == (c // CA_D), 1.0 / CA_D, 0.0).astype(BF16)
    d = lambda u: lax.dot_general(u, g, (((1,), (0,)), ((), ())), preferred_element_type=F32)
    outs = []
    for j in range(y.shape[1] // 128):
        hi, lo = _split(y[:, 128 * j:128 * (j + 1)])
        outs.append(d(hi) + d(lo))
    return jnp.concatenate(outs, axis=1)


@jax.custom_vjp
def group_mean(y):
    return _group_mean_raw(y)


group_mean.defvjp(lambda y: (_group_mean_raw(y), None), lambda _, g: (_group_mean_raw(g),))


def f_headnorm(t, g):
    return (t * lax.rsqrt(group_mean(t * t) + EPS) * g,)


def _cattn_chunk(q, kb, vb, bias, valid):
    lane = lax.broadcasted_iota(jnp.int32, (1, 128), 1)
    m0 = (lane < CA_D).astype(F32)
    m1 = 1.0 - m0
    pairs = range(HG // 2)
    sl = [slice(128 * p, 128 * (p + 1)) for p in pairs]
    q2 = [jnp.concatenate([q[:, s] * m0, q[:, s] * m1], axis=0) for s in sl]
    sc = [mm_nt(q2[p], kb[:, sl[p]]) * (CA_D ** -0.5) + bias[sl[p]] for p in pairs]
    pr = [_softmax(jnp.where(valid, s, -1e30)) for s in sc]
    o2 = [mm(pr[p], vb[:, sl[p]]) for p in pairs]
    return jnp.concatenate([o[:CHUNK] * m0 + o[CHUNK:] * m1 for o in o2], axis=1)


def _cattn_valid(c):
    pos = lax.broadcasted_iota(jnp.int32, (1, CA_BAND), 1) + c * CHUNK
    return pos >= CA_PAD


def _cattn_specs(S):
    q_spec = pl.BlockSpec((CHUNK, HG_LANES), lambda h, c: (c, h))
    kv_spec = pl.BlockSpec((S + CA_PAD, HG_LANES), lambda h, c: (0, h))
    b_spec = pl.BlockSpec((HG * CHUNK, CA_BAND), lambda h, c: (h, 0))
    return q_spec, kv_spec, b_spec


def cattn_fwd(qn, kp, vp, bias):
    S = qn.shape[0]
    nc = S // CHUNK
    q_spec, kv_spec, b_spec = _cattn_specs(S)

    def body(q_ref, k_ref, v_ref, b_ref, o_ref):
        c = pl.program_id(1)
        start = pl.multiple_of(c * CHUNK, CHUNK)
        kb = k_ref[pl.ds(start, CA_BAND), :]
        vb = v_ref[pl.ds(start, CA_BAND), :]
        o_ref[...] = _cattn_chunk(q_ref[...], kb, vb, b_ref[...], _cattn_valid(c)).astype(o_ref.dtype)

    return pl.pallas_call(
        body, name="cattn_fwd", grid=(CA_H // HG, nc), in_specs=[q_spec, kv_spec, kv_spec, b_spec],
        out_specs=q_spec, out_shape=jax.ShapeDtypeStruct((S, D), BF16),
        compiler_params=_cp("parallel", "arbitrary"),
    )(qn, kp, vp, bias)


def kv_prep(qkv, kg):
    S = qkv.shape[0]
    tile = ROW_TILE
    lead = CA_PAD // tile

    def body(k_ref, v_ref, g_ref, kp_ref, vp_ref):
        i = pl.program_id(0)

        @pl.when(i < lead)
        def _():
            kp_ref[...] = jnp.zeros_like(kp_ref)
            vp_ref[...] = jnp.zeros_like(vp_ref)

        @pl.when(i >= lead)
        def _():
            kp_ref[...] = f_headnorm(k_ref[...], g_ref[...])[0].astype(BF16)
            vp_ref[...] = v_ref[...].astype(BF16)

    src = lambda cb: pl.BlockSpec((tile, D), lambda i, cb=cb: (jnp.maximum(i - lead, 0), cb))
    out = pl.BlockSpec((tile, D), lambda i: (i, 0))
    return pl.pallas_call(
        body, name="kv_prep", grid=((S + CA_PAD) // tile,),
        in_specs=[src(1), src(2), pl.BlockSpec((1, D), lambda i: (0, 0))], out_specs=[out, out],
        out_shape=[jax.ShapeDtypeStruct((S + CA_PAD, D), BF16)] * 2,
        compiler_params=_cp("parallel"),
    )(qkv, qkv, kg)


def cattn_bwd(qn, kp, vp, bias, do):
    S = qn.shape[0]
    nc = S // CHUNK
    q_spec, kv_spec, b_spec = _cattn_specs(S)

    def body(q_ref, k_ref, v_ref, b_ref, do_ref, dq_ref, dk_ref, dv_ref, db_ref):
        c = pl.program_id(1)

        @pl.when(c == 0)
        def _():
            dk_ref[...] = jnp.zeros_like(dk_ref)
            dv_ref[...] = jnp.zeros_like(dv_ref)
            db_ref[...] = jnp.zeros_like(db_ref)

        start = pl.multiple_of(c * CHUNK, CHUNK)
        kb = k_ref[pl.ds(start, CA_BAND), :].astype(F32)
        vb = v_ref[pl.ds(start, CA_BAND), :].astype(F32)
        valid = _cattn_valid(c)
        _, vjp = jax.vjp(lambda q, k, v, b: _cattn_chunk(q, k, v, b, valid), q_ref[...], kb, vb, b_ref[...])
        dq, dk, dv, db = vjp(do_ref[...])
        dq_ref[...] = dq
        dk_ref[pl.ds(start, CA_BAND), :] += dk
        dv_ref[pl.ds(start, CA_BAND), :] += dv
        db_ref[...] += db

    sd = jax.ShapeDtypeStruct
    return pl.pallas_call(
        body, name="cattn_bwd", grid=(CA_H // HG, nc), in_specs=[q_spec, kv_spec, kv_spec, b_spec, q_spec],
        out_specs=[q_spec, kv_spec, kv_spec, b_spec],
        out_shape=[sd((S, D), F32), sd((S + CA_PAD, D), F32), sd((S + CA_PAD, D), F32),
                   sd((CA_H * CHUNK, CA_BAND), F32)],
        compiler_params=_cp("parallel", "arbitrary"),
    )(qn, kp, vp, bias, do)


SKEW_W = CA_BAND + CHUNK


def rel_bias_grad(dbias):
    padded = jnp.pad(dbias, ((0, 0), (0, 0), (CHUNK, 0)))
    flat = jnp.pad(padded.reshape(CA_H, CHUNK * SKEW_W), ((0, 0), (0, CHUNK)))
    skew = flat.reshape(CA_H, CHUNK, SKEW_W + 1)

    first_near = SKEW_W - CHUNK - MAX_REL

    def fn(t):
        colsum = jnp.sum(t, axis=1, keepdims=True)
        j = lax.broadcasted_iota(jnp.int32, colsum.shape, 2)
        far = jnp.sum(jnp.where(j < first_near, colsum, 0.0), axis=2, keepdims=True)
        return (colsum + jnp.where(j == first_near, far, 0.0),)

    (colsum,) = whole("relbias_sum", fn, [skew], [((CA_H, 1, SKEW_W + 1), F32)])
    near = colsum[:, 0, first_near:SKEW_W][:, ::-1]
    return jnp.concatenate([jnp.zeros((CA_H, CHUNK + 1), F32), near], axis=1)


def rel_bias_expand(rb):
    near = rb[:, CHUNK + 1:][:, ::-1]
    far = jnp.broadcast_to(rb[:, 2 * MAX_REL:], (CA_H, SKEW_W - CHUNK - MAX_REL))
    t = jnp.concatenate([far, near, jnp.zeros((CA_H, 1), rb.dtype)], axis=1)
    rows = jnp.tile(t, (1, CHUNK))[:, :CHUNK * SKEW_W].reshape(CA_H, CHUNK, SKEW_W)
    return rows[:, :, CHUNK:]


def loss_head(y, target):
    S = y.shape[0]
    tile = min(ROW_TILE, S)

    def body(y_ref, t_ref, dy_ref, acc_ref):
        i = pl.program_id(0)
        e = y_ref[...] - t_ref[...]
        dy_ref[...] = e * (1.0 / D)
        part = jnp.sum(e * e, axis=0, keepdims=True) * (0.5 / D)

        @pl.when(i == 0)
        def _():
            acc_ref[...] = part

        @pl.when(i > 0)
        def _():
            acc_ref[...] += part

    row = pl.BlockSpec((tile, D), lambda i: (i, 0))
    return pl.pallas_call(
        body, name="loss_head", grid=(S // tile,), in_specs=[row, row],
        out_specs=[row, pl.BlockSpec((1, D), lambda i: (0, 0))],
        out_shape=[jax.ShapeDtypeStruct((S, D), F32), jax.ShapeDtypeStruct((1, D), F32)],
        compiler_params=_cp("arbitrary"),
    )(y, target)


ANY = pl.BlockSpec(memory_space=pl.ANY)


HBM = pl.BlockSpec(memory_space=pltpu.HBM)
SEM = pl.BlockSpec(memory_space=pltpu.SEMAPHORE)
EFFECT = pltpu.SideEffectType.DATAFLOW_SIDE_EFFECTING


def _chip_copies(src_ref, land_ref, send_sems, recv_sems, mode):
    x, y, c = lax.axis_index("x"), lax.axis_index("y"), lax.axis_index("c")
    me = 2 * x + y

    def copy(j, s, d, dev):
        return pltpu.make_async_remote_copy(src_ref=s, dst_ref=d, send_sem=send_sems.at[j], recv_sem=recv_sems.at[j],
                                            device_id=dev, device_id_type=MESH)

    if mode == "sibling":
        cp = copy(0, src_ref, land_ref, (x, y, 1 - c))
        return [(cp, cp)]
    out = []
    for j, (px, py) in enumerate([(1 - x, y), (x, 1 - y), (1 - x, 1 - y)]):
        peer = 2 * px + py
        if mode == "scatter":
            pairs, dev = ((src_ref.at[peer], land_ref.at[j]), (src_ref.at[me], land_ref.at[j])), (px, py, c)
        else:
            half = land_ref.shape[1] // 2
            mine = pl.ds(pl.multiple_of(c * half, 16), half)
            theirs = pl.ds(pl.multiple_of((1 - c) * half, 16), half)
            if mode == "gather_half":
                pairs = ((src_ref.at[mine], land_ref.at[me, mine]), (src_ref.at[mine], land_ref.at[peer, mine]))
                dev = (px, py, c)
            else:
                pairs = ((land_ref.at[peer, mine],) * 2, (land_ref.at[peer, theirs],) * 2)
                dev = (x, y, 1 - c)
        out.append(tuple(copy(j, s, d, dev) for s, d in pairs))
    return out


def exchange_start(src, land, carry, name, mode):
    def body(src_ref, land_ref, carry_ref, send_sems, recv_sems, src_out, land_out, carry_out):
        for send, _ in _chip_copies(src_ref, land_ref, send_sems, recv_sems, mode):
            send.start()

    hbm = lambda a: pltpu.HBM(a.shape, a.dtype)
    n = 1 if mode == "sibling" else 3
    return pl.pallas_call(
        body, name=name,
        out_shape=(pltpu.SemaphoreType.DMA((n,)), pltpu.SemaphoreType.DMA((n,)), hbm(src), hbm(land), hbm(carry)),
        in_specs=(HBM, HBM, HBM), out_specs=(SEM, SEM, HBM, HBM, HBM),
        input_output_aliases={0: 2, 1: 3, 2: 4},
        compiler_params=pltpu.CompilerParams(has_side_effects=EFFECT),
    )(pltpu.with_memory_space_constraint(src, pltpu.HBM), pltpu.with_memory_space_constraint(land, pltpu.HBM),
      pltpu.with_memory_space_constraint(carry, pltpu.HBM))


def exchange_wait(src, land, send_sems, recv_sems, after, name, mode):
    def body(src_ref, land_ref, send_sems_ref, recv_sems_ref, after_ref, src_out, land_out):
        for send, recv in _chip_copies(src_ref, land_ref, send_sems_ref, recv_sems_ref, mode):
            send.wait_send()
            recv.wait_recv()

    hbm = lambda a: pltpu.HBM(a.shape, a.dtype)
    return pl.pallas_call(
        body, name=name, out_shape=(hbm(src), hbm(land)),
        in_specs=(HBM, HBM, SEM, SEM, ANY), out_specs=(HBM, HBM), input_output_aliases={0: 0, 1: 1},
        compiler_params=pltpu.CompilerParams(has_side_effects=EFFECT),
    )(src, land, send_sems, recv_sems, after)


def sibling_exchange(srcs, name):
    n = len(srcs)

    def body(*refs):
        src_refs, out_refs, send_sems, recv_sems = refs[:n], refs[n:2 * n], refs[2 * n], refs[2 * n + 1]
        x, y, c = lax.axis_index("x"), lax.axis_index("y"), lax.axis_index("c")
        copies = [pltpu.make_async_remote_copy(src_ref=s, dst_ref=o, send_sem=send_sems.at[k], recv_sem=recv_sems.at[k],
                                               device_id=(x, y, 1 - c), device_id_type=MESH)
                  for k, (s, o) in enumerate(zip(src_refs, out_refs))]
        for cp in copies:
            cp.start()
        for cp in copies:
            cp.wait()

    return pl.pallas_call(
        body, name=name, in_specs=[ANY] * n, out_specs=[ANY] * n,
        out_shape=[jax.ShapeDtypeStruct(s.shape, s.dtype) for s in srcs],
        scratch_shapes=[pltpu.SemaphoreType.DMA((n,)), pltpu.SemaphoreType.DMA((n,))],
    )(*srcs)


def chip_gather(src, name):
    def body(src_ref, out_ref, send_sems, recv_sems, local_sem):
        x, y, c = lax.axis_index("x"), lax.axis_index("y"), lax.axis_index("c")
        me = 2 * x + y
        local = pltpu.make_async_copy(src_ref, out_ref.at[me], local_sem)
        local.start()
        peers = [(1 - x, y), (x, 1 - y), (1 - x, 1 - y)]
        mk = lambda j, px, py, slab: pltpu.make_async_remote_copy(
            src_ref=src_ref, dst_ref=out_ref.at[slab], send_sem=send_sems.at[j], recv_sem=recv_sems.at[j],
            device_id=(px, py, c), device_id_type=MESH)
        sends = [mk(j, px, py, me) for j, (px, py) in enumerate(peers)]
        for cp in sends:
            cp.start()
        for j, (px, py) in enumerate(peers):
            mk(j, px, py, 2 * px + py).wait_recv()
        for cp in sends:
            cp.wait_send()
        local.wait()

    return pl.pallas_call(
        body, name=name, in_specs=[ANY], out_specs=ANY,
        out_shape=jax.ShapeDtypeStruct((N_CHIPS,) + tuple(src.shape), src.dtype),
        scratch_shapes=[pltpu.SemaphoreType.DMA((3,)), pltpu.SemaphoreType.DMA((3,)), pltpu.SemaphoreType.DMA],
    )(src)


def sum_own_slabs(src, chip, land, name, tile=512):
    _, R, C = src.shape
    n = land.shape[0]
    tile = _tile(R, tile, 16)

    def body(chip_ref, o_ref, t_ref, out_ref):
        acc = o_ref[...].astype(F32)
        for s in range(n):
            acc = acc + t_ref[s].astype(F32)
        out_ref[...] = acc

    return pl.pallas_call(
        body, name=name,
        grid_spec=pltpu.PrefetchScalarGridSpec(
            num_scalar_prefetch=1, grid=(R // tile,),
            in_specs=[pl.BlockSpec((None, tile, C), lambda i, chip_ref: (chip_ref[0], i, 0)),
                      pl.BlockSpec((n, tile, C), lambda i, chip_ref: (0, i, 0))],
            out_specs=pl.BlockSpec((tile, C), lambda i, chip_ref: (i, 0))),
        out_shape=jax.ShapeDtypeStruct((R, C), F32),
        compiler_params=_cp("parallel"),
    )(jnp.reshape(chip, (1,)).astype(jnp.int32), src, land)


def sum_slabs(t, name, tile=512):
    n, R, C = t.shape
    tile = _tile(R, tile, 16)

    def body(t_ref, o_ref):
        acc = t_ref[0].astype(F32)
        for s in range(1, n):
            acc = acc + t_ref[s].astype(F32)
        o_ref[...] = acc

    return pl.pallas_call(
        body, name=name, grid=(R // tile,), in_specs=[pl.BlockSpec((n, tile, C), lambda i: (0, i, 0))],
        out_specs=pl.BlockSpec((tile, C), lambda i: (i, 0)), out_shape=jax.ShapeDtypeStruct((R, C), F32),
        compiler_params=_cp("parallel"),
    )(t)


def _pad_rows(a, mult=16):
    r = (-a.shape[0]) % mult
    return jnp.pad(a, ((0, r), (0, 0))) if r else a


BIG = ["ab_w_in", "c_w_qkv", "xa_w_kv", "f_w_gate", "f_w_up", "ab_w_out", "c_w_out", "xa_w_q", "xa_w_out",
       "f_w_down", "s5_w_glu"]


GROUPS = ("B", "A")
GROUP_ROW_MULT = 64


def group_spec(layer, grp):
    i = layer // 2
    if grp == "A":
        return [("xa_w_kv", layer, True, 512), ("xa_w_q", layer, False, 256), ("xa_w_out", layer, False, 256),
                ("f_w_gate", layer, True, 704), ("f_w_up", layer, True, 704), ("f_w_down", layer, False, 704)]
    if layer % 2 == 0:
        return [("ab_w_in", i, True, 642), ("ab_w_out", i, False, 256), ("s5_w_glu", i, False, 64)]
    return [("c_w_qkv", i, True, 768), ("c_w_out", i, False, 256)]


def _seg_rows(rows):
    return rows + ((-rows) % 16)


def _f32_rows(a):
    bits = lax.bitcast_convert_type(a.reshape(-1), BF16).reshape(-1)
    return jnp.pad(bits, (0, 16 * D - bits.shape[0])).reshape(16, D)


def pack_group_shards(p, layer, grp):
    segs = []
    for name, idx, transposed, rows in group_spec(layer, grp):
        w = p[name][idx]
        if transposed:
            w = w.T
        segs.append(_pad_rows(w.astype(BF16).reshape(-1, D)))
    if grp == "B":
        small = p["gdn_conv_w"] if layer % 2 == 0 else p["c_norm_g"]
        segs.append(_f32_rows(small[layer // 2]))
    return _pad_rows(jnp.concatenate(segs, axis=0), GROUP_ROW_MULT)


def unpack_group_gathered(g, layer, grp):
    out, off = {}, 0
    for name, idx, transposed, rows in group_spec(layer, grp):
        seg = g[:, off:off + rows]
        if name == "s5_w_glu":
            out[name] = seg.reshape(N_CHIPS * 128, 512)
        else:
            out[name] = seg.reshape(N_CHIPS * rows, D)
        off += _seg_rows(rows)
    if grp == "B":
        n = 4 * 384 if layer % 2 == 0 else 256
        bits = g[:, off:off + 16].reshape(N_CHIPS, -1)[:, :2 * n].reshape(N_CHIPS, n, 2)
        small = lax.bitcast_convert_type(bits, F32)
        if layer % 2 == 0:
            out["gdn_conv_w"] = jnp.swapaxes(small.reshape(N_CHIPS, 4, 384), 0, 1).reshape(4, 1536)
        else:
            out["c_norm_g"] = small.reshape(D)
    return out


def pack_group_grads(gr, layer, grp):
    segs = []
    for name, idx, transposed, rows in group_spec(layer, grp):
        w = gr[name].astype(BF16)
        seg = w.reshape(N_CHIPS, rows, D)
        r = (-rows) % 16
        if r:
            seg = jnp.pad(seg, ((0, 0), (0, r), (0, 0)))
        segs.append(seg)
    out = jnp.concatenate(segs, axis=1)
    return jnp.pad(out, ((0, 0), (0, (-out.shape[1]) % GROUP_ROW_MULT), (0, 0)))


def unpack_group_reduced(g, layer, grp):
    out, off = {}, 0
    for name, idx, transposed, rows in group_spec(layer, grp):
        seg = g[off:off + rows]
        if name == "s5_w_glu":
            out[name] = seg.reshape(128, 512)
        else:
            out[name] = seg.T if (transposed and name not in ADAM_TRANSPOSED) else seg
        off += _seg_rows(rows)
    return out


ADAM_TRANSPOSED = ("f_w_gate", "f_w_up")


SMALL = ["ab_norm_g", "s5_a_re", "s5_a_im", "s5_log_dt", "s5_b_re", "s5_b_im", "s5_c_re", "s5_c_im", "s5_d",
         "s5_b_glu", "gdn_conv_w", "gdn_a_log", "gdn_dt_bias", "gdn_out_norm_g", "c_norm_g", "c_q_norm_g",
         "c_k_norm_g", "c_rel_bias", "mem_norm_g", "xa_norm_g", "xa_q_norm_g", "xa_k_norm_g", "f_norm_g"]


def _lane_rows(a):
    flat = a.reshape(-1).astype(F32)
    return jnp.pad(flat, (0, (-flat.shape[0]) % 1024)).reshape(-1, 128)


def pack_small(d, extra=None):
    parts = [_lane_rows(d[n]) for n in SMALL]
    if extra is not None:
        parts.append(_lane_rows(extra))
    rows = jnp.concatenate(parts, axis=0)
    return jnp.pad(rows, ((0, (-rows.shape[0]) % 128), (0, 0)))


def unpack_small(rows, shapes):
    out, off = {}, 0
    for n in SMALL:
        sz = int(np.prod(shapes[n]))
        k = 8 * -(-sz // 1024)
        out[n] = rows[off:off + k].reshape(-1)[:sz].reshape(shapes[n])
        off += k
    return out, rows[off:]


def _s5_blockdiag_b(b):
    bt = jnp.swapaxes(b, 1, 2).reshape(S5_GB, 8, S5_C, S5_P)
    eye = jnp.eye(8, dtype=b.dtype)
    return jnp.einsum("bgcp,gh->bgchp", bt, eye).reshape(S5_GB, 8 * S5_C, 8 * S5_P)


def _s5_blockdiag_c(c):
    ct = jnp.swapaxes(c, 1, 2).reshape(S5_GB, 8, S5_P, S5_C)
    eye = jnp.eye(8, dtype=c.dtype)
    return jnp.einsum("bgpc,gh->bgphc", ct, eye).reshape(S5_GB, 8 * S5_P, 8 * S5_C)


def _s5_diag_b(db):
    t = db.reshape(S5_GB, 8, S5_C, 8, S5_P)
    t = jnp.transpose(t, (0, 2, 4, 1, 3)).reshape(S5_GB, S5_C, S5_P, 64)
    d = t[..., ::9]
    return jnp.transpose(d, (0, 3, 2, 1)).reshape(S5_G, S5_P, S5_C)


def _s5_diag_c(dc):
    t = dc.reshape(S5_GB, 8, S5_P, 8, S5_C)
    t = jnp.transpose(t, (0, 2, 4, 1, 3)).reshape(S5_GB, S5_P, S5_C, 64)
    d = t[..., ::9]
    return jnp.transpose(d, (0, 3, 2, 1)).reshape(S5_G, S5_C, S5_P)


def local_step(x, mem, target, p, wsrc, gsink, wpre=lambda layer, grp, carry: carry):
    S = x.shape[0]
    row2 = lambda a: a.reshape(1, -1)
    saved = []
    (mem_n,) = row_fwd("mem_norm", f_norm, [mem], [row2(p["mem_norm_g"])], [(D, BF16)])
    gs = {}

    for layer in range(DEPTH):
        i = layer // 2
        w = wsrc(layer, "B", x)
        sv = {"x0": x, "wB": w}
        if layer % 2 == 0:
            (h,) = row_fwd("norm", f_norm, [x], [row2(p["ab_norm_g"][i])], [(D, BF16)])
            w_in = w["ab_w_in"]
            pm = matmul("nt", h, w_in[:2560], "proj_main")
            abt = matmul("nt", w_in[2560:], h, "proj_ab")
            s5p = dict(
                are=p["s5_a_re"][i].reshape(1, -1), aim=p["s5_a_im"][i].reshape(1, -1),
                ldt=jnp.broadcast_to(p["s5_log_dt"][i][:, None], (S5_G, S5_P)).reshape(1, -1),
                b_re=_s5_blockdiag_b(p["s5_b_re"][i]), b_im=_s5_blockdiag_b(p["s5_b_im"][i]),
                c_re=_s5_blockdiag_c(p["s5_c_re"][i]), c_im=_s5_blockdiag_c(p["s5_c_im"][i]),
                dv=p["s5_d"][i].reshape(1, -1))
            y5, st5 = s5_fwd(pm, **s5p)
            (a_out,) = row_fwd("glu", f_glu, [y5], [w["s5_w_glu"], row2(p["s5_b_glu"][i])], [(S5_W, F32)])
            conv_w = w["gdn_conv_w"]
            qkvc = conv_fwd(pm, conv_w)
            alog = p["gdn_a_log"][i].reshape(GDN_H, 1, 1)
            dtb = p["gdn_dt_bias"][i].reshape(GDN_H, 1, 1)
            og = row2(p["gdn_out_norm_g"][i])
            b_out, stg, inv = gdn_fwd(qkvc, pm, abt, alog, dtb, og)
            cat = wpre(layer, "A", jnp.concatenate([a_out, b_out], axis=1))
            x, hq = matmul("nn", cat, w["ab_w_out"], "mix_out", add=x, norm_out=row2(p["xa_norm_g"][layer]))
            sv.update(h=h, pm=pm, s5p=s5p, y5=y5, st5=st5, qkvc=qkvc, abt=abt, alog=alog, dtb=dtb, og=og,
                      stg=stg, inv=inv, cat=cat, conv_w=conv_w)
        else:
            (h,) = row_fwd("norm", f_norm, [x], [row2(w["c_norm_g"])], [(D, BF16)])
            qkv = matmul("nt", h, w["c_w_qkv"], "proj_qkv")
            qg = jnp.tile(row2(p["c_q_norm_g"][i]), (1, CA_H))
            kg = jnp.tile(row2(p["c_k_norm_g"][i]), (1, CA_H))
            (qn,) = row_fwd("headnorm_q", f_headnorm, [(qkv, D, 0)], [qg], [(D, F32)])
            kp, vp = kv_prep(qkv, kg)
            bias = rel_bias_expand(p["c_rel_bias"][i]).reshape(CA_H * CHUNK, CA_BAND)
            o = wpre(layer, "A", cattn_fwd(qn, kp, vp, bias))
            x, hq = matmul("nn", o, w["c_w_out"], "mix_out", add=x, norm_out=row2(p["xa_norm_g"][layer]))
            sv.update(h=h, qkv=qkv, qg=qg, kg=kg, qn=qn, kp=kp, vp=vp, bias=bias, o=o)
        sv["x1"] = x
        w = wsrc(layer, "A", x)
        sv["wA"] = w
        qx = matmul("nn", hq, w["xa_w_q"], "xa_q")
        kv = matmul("nt", mem_n, w["xa_w_kv"], "xa_kv")
        xqg, xkg = row2(p["xa_q_norm_g"][layer]), row2(p["xa_k_norm_g"][layer])
        (ox,) = row_fwd("xattn", f_xattn, [qx], [kv, xqg, xkg], [(D, BF16)], tile=2 * ROW_TILE)
        x, hf = matmul("nn", ox, w["xa_w_out"], "xa_out", add=x, norm_out=row2(p["f_norm_g"][layer]))
        sv.update(hq=hq, qx=qx, kv=kv, ox=ox)
        sv["x2"] = x
        gate, up, act = ffn_in(hf, w["f_w_gate"], w["f_w_up"])
        if layer + 1 < DEPTH:
            act = wpre(layer + 1, "B", act)
        x = matmul("nn", act, w["f_w_down"], "ffn_down", add=x)
        sv.update(hf=hf, gate=gate, up=up, act=act)
        saved.append(sv)

    dx, loss_vec = loss_head(x, target)

    dmem_n = None
    for layer in reversed(range(DEPTH)):
        i = layer // 2
        sv = saved[layer]
        w, gw = sv["wA"], {}
        dgate, dup = ffn_dact(dx, w["f_w_down"], sv["gate"], sv["up"])
        gw["f_w_down"] = matmul("tn", sv["act"], dx, "ffn_dwd", out_dtype=BF16)
        gw["f_w_gate"] = matmul("tn", dgate, sv["hf"], "ffn_dwg", out_dtype=BF16)
        gw["f_w_up"] = matmul("tn", dup, sv["hf"], "ffn_dwu", out_dtype=BF16)
        dh = matmul("nn", dgate, w["f_w_gate"], "ffn_dhg")
        dx, dg = matmul("nn", dup, w["f_w_up"], "ffn_dhu", add=dh,
                        norm_bwd=(sv["x2"], row2(p["f_norm_g"][layer]), dx))
        gs.setdefault("f_norm_g", [None] * DEPTH)[layer] = dg[0]
        do = matmul("nt", dx, w["xa_w_out"], "xa_do")
        gw["xa_w_out"] = matmul("tn", sv["ox"], dx, "xa_dwo", out_dtype=BF16)
        xqg, xkg = row2(p["xa_q_norm_g"][layer]), row2(p["xa_k_norm_g"][layer])
        dqx, dkv, dqg, dkg = row_bwd("xattn_bwd", f_xattn, [sv["qx"]], [sv["kv"], xqg, xkg], [do],
                                     [0], [0, 1, 2], tile=2 * ROW_TILE)
        gs.setdefault("xa_q_norm_g", [None] * DEPTH)[layer] = dqg[0]
        gs.setdefault("xa_k_norm_g", [None] * DEPTH)[layer] = dkg[0]
        gw["xa_w_q"] = matmul("tn", sv["hq"], dqx, "xa_dwq", out_dtype=BF16)
        gw["xa_w_kv"] = matmul("tn", dkv, mem_n, "xa_dwkv", out_dtype=BF16)
        dx = gsink(layer, "A", gw, dx)
        dmem_n = matmul("nn", dkv, w["xa_w_kv"], "xa_dmem", add=dmem_n)
        dx, dg = matmul("nt", dqx, w["xa_w_q"], "xa_dhq", norm_bwd=(sv["x1"], row2(p["xa_norm_g"][layer]), dx))
        gs.setdefault("xa_norm_g", [None] * DEPTH)[layer] = dg[0]
        w, gw = sv["wB"], {}
        if layer % 2 == 0:
            dcat = matmul("nt", dx, w["ab_w_out"], "mix_dcat")
            gw["ab_w_out"] = matmul("tn", sv["cat"], dx, "mix_dwo", out_dtype=BF16)
            dy5, dwglu, dbglu = row_bwd("glu_bwd", f_glu, [sv["y5"]], [w["s5_w_glu"], row2(p["s5_b_glu"][i])],
                                        [(dcat, S5_W, 0)], [0], [0, 1])
            gw["s5_w_glu"] = dwglu
            gs.setdefault("s5_b_glu", [None] * 2)[i] = dbglu[0]
            s5p = sv["s5p"]
            du, dare, daim, dldt, dbre, dbim, dcre, dcim, ddv = s5_bwd(sv["pm"], sv["st5"], dy5, **s5p)
            (dldt_g,) = whole("s5_dt_sum", lambda t: (jnp.sum(t, axis=1, keepdims=True),),
                              [dldt.reshape(S5_G, S5_P)], [((S5_G, 1), F32)])
            for nme, val in (("s5_a_re", dare.reshape(S5_G, S5_P)), ("s5_a_im", daim.reshape(S5_G, S5_P)),
                             ("s5_log_dt", dldt_g[:, 0]), ("s5_b_re", _s5_diag_b(dbre)),
                             ("s5_b_im", _s5_diag_b(dbim)), ("s5_c_re", _s5_diag_c(dcre)),
                             ("s5_c_im", _s5_diag_c(dcim)), ("s5_d", ddv.reshape(S5_G, S5_C))):
                gs.setdefault(nme, [None] * 2)[i] = val
            dqkvc, dgate, dal, dbl, dalog, ddtb, dog = gdn_bwd(
                sv["qkvc"], sv["pm"], sv["abt"], sv["alog"], sv["dtb"], sv["og"], sv["stg"], sv["inv"],
                dcat[:, S5_W:])
            (dog_s,) = whole("gdn_og_sum", lambda t: (jnp.sum(t, axis=0, keepdims=True),),
                             [dog.reshape(GDN_H, GDN_D)], [((1, GDN_D), F32)])
            gs.setdefault("gdn_out_norm_g", [None] * 2)[i] = dog_s[0]
            gs.setdefault("gdn_a_log", [None] * 2)[i] = dalog.reshape(GDN_H)
            gs.setdefault("gdn_dt_bias", [None] * 2)[i] = ddtb.reshape(GDN_H)
            dqkv, dconv = conv_bwd(sv["pm"], sv["conv_w"], dqkvc)
            gs.setdefault("gdn_conv_w", [None] * 2)[i] = dconv
            dpm = jnp.concatenate([du, dqkv, dgate], axis=1).astype(BF16)
            dabt = jnp.concatenate([dal, dbl], axis=0)
            dw_main = matmul("tn", dpm, sv["h"], "proj_dw", out_dtype=BF16)
            dw_ab = matmul("nn", dabt, sv["h"], "proj_ab_dw", out_dtype=BF16)
            gw["ab_w_in"] = jnp.concatenate([dw_main, dw_ab], axis=0)
            dx = gsink(layer, "B", gw, dx)
            w_in = w["ab_w_in"]
            dh = matmul("nn", dpm, w_in[:2560], "proj_dh")
            dx, dg = matmul("tn", dabt, w_in[2560:], "proj_ab_dh", add=dh,
                            norm_bwd=(sv["x0"], row2(p["ab_norm_g"][i]), dx))
            gs.setdefault("ab_norm_g", [None] * 2)[i] = dg[0]
        else:
            do = matmul("nt", dx, w["c_w_out"], "mix_dcat")
            gw["c_w_out"] = matmul("tn", sv["o"], dx, "mix_dwo", out_dtype=BF16)
            dqn, dkp, dvp, dbias = cattn_bwd(sv["qn"], sv["kp"], sv["vp"], sv["bias"], do)
            gs.setdefault("c_rel_bias", [None] * 2)[i] = rel_bias_grad(dbias.reshape(CA_H, CHUNK, CA_BAND))
            dq, dqg = row_bwd("headnorm_bwd", f_headnorm, [(sv["qkv"], D, 0)], [sv["qg"]], [dqn], [0], [0])
            dk, dkg = row_bwd("headnorm_bwd", f_headnorm, [(sv["qkv"], D, 1)], [sv["kg"]],
                              [(dkp, D, 0, CA_PAD)], [0], [0])
            head_sum = lambda t: (jnp.sum(t, axis=0, keepdims=True),)
            (dqg,) = whole("headgain_sum", head_sum, [dqg.reshape(CA_H, CA_D)], [((1, CA_D), F32)])
            (dkg,) = whole("headgain_sum", head_sum, [dkg.reshape(CA_H, CA_D)], [((1, CA_D), F32)])
            gs.setdefault("c_q_norm_g", [None] * 2)[i] = dqg[0]
            gs.setdefault("c_k_norm_g", [None] * 2)[i] = dkg[0]
            dqkv = jnp.concatenate([dq, dk, dvp[CA_PAD:]], axis=1).astype(BF16)
            gw["c_w_qkv"] = matmul("tn", dqkv, sv["h"], "proj_qkv_dw", out_dtype=BF16)
            dx = gsink(layer, "B", gw, dx)
            dx, dg = matmul("nn", dqkv, w["c_w_qkv"], "proj_qkv_dh",
                            norm_bwd=(sv["x0"], row2(w["c_norm_g"]), dx))
            gs.setdefault("c_norm_g", [None] * 2)[i] = dg[0]
    (dmg,) = row_bwd("mem_norm_bwd", f_norm, [mem], [row2(p["mem_norm_g"])], [dmem_n], [], [0])
    small = {n: jnp.stack(v) for n, v in gs.items()}
    small["mem_norm_g"] = dmg[0]
    return loss_vec, dx, small


ADAM_BLOCK_BYTES = 3 << 19


def adam(w, g, m, v, name):
    shape = w.shape
    if w.ndim == 3:
        d0, n, d2 = shape
        fits = [t for t in range(8, n + 1, 8) if n % t == 0 and d0 * t * d2 * 4 <= ADAM_BLOCK_BYTES]
        return tuple(row_fwd(name, f_adam, [w, g, m, v], [], [((d0, d2), F32)] * 3, tile=max(fits)))
    cols = shape[-1]
    w2, g2, m2, v2 = (t.reshape(-1, cols) for t in (w, g, m, v))
    rows = w2.shape[0]
    tile = rows if rows <= 512 else _tile(rows, 512, 8)
    outs = row_fwd(name, f_adam, [w2, g2, m2, v2], [], [(cols, F32)] * 3, tile=tile)
    return tuple(o.reshape(shape) for o in outs)


WEIGHTS = ['ab_norm_g', 'ab_w_in', 'ab_w_out', 's5_a_re', 's5_a_im', 's5_log_dt', 's5_b_re', 's5_b_im', 's5_c_re',
           's5_c_im', 's5_d', 's5_w_glu', 's5_b_glu', 'gdn_conv_w', 'gdn_a_log', 'gdn_dt_bias', 'gdn_out_norm_g',
           'c_norm_g', 'c_w_qkv', 'c_w_out', 'c_q_norm_g', 'c_k_norm_g', 'c_rel_bias', 'mem_norm_g', 'xa_norm_g',
           'xa_w_q', 'xa_w_kv', 'xa_w_out', 'xa_q_norm_g', 'xa_k_norm_g', 'f_norm_g', 'f_w_gate', 'f_w_up',
           'f_w_down']
SHARDED_SMALL = {"gdn_conv_w": (2, 384), "c_norm_g": (1, 256)}


def kernel(x, mem, ab_norm_g, ab_w_in, ab_w_out, s5_a_re, s5_a_im, s5_log_dt, s5_b_re, s5_b_im, s5_c_re, s5_c_im, s5_d, s5_w_glu, s5_b_glu, gdn_conv_w, gdn_a_log, gdn_dt_bias, gdn_out_norm_g, c_norm_g, c_w_qkv, c_w_out, c_q_norm_g, c_k_norm_g, c_rel_bias, mem_norm_g, xa_norm_g, xa_w_q, xa_w_kv, xa_w_out, xa_q_norm_g, xa_k_norm_g, f_norm_g, f_w_gate, f_w_up, f_w_down, loss_target, m_ab_norm_g, m_ab_w_in, m_ab_w_out, m_s5_a_re, m_s5_a_im, m_s5_log_dt, m_s5_b_re, m_s5_b_im, m_s5_c_re, m_s5_c_im, m_s5_d, m_s5_w_glu, m_s5_b_glu, m_gdn_conv_w, m_gdn_a_log, m_gdn_dt_bias, m_gdn_out_norm_g, m_c_norm_g, m_c_w_qkv, m_c_w_out, m_c_q_norm_g, m_c_k_norm_g, m_c_rel_bias, m_mem_norm_g, m_xa_norm_g, m_xa_w_q, m_xa_w_kv, m_xa_w_out, m_xa_q_norm_g, m_xa_k_norm_g, m_f_norm_g, m_f_w_gate, m_f_w_up, m_f_w_down, v_ab_norm_g, v_ab_w_in, v_ab_w_out, v_s5_a_re, v_s5_a_im, v_s5_log_dt, v_s5_b_re, v_s5_b_im, v_s5_c_re, v_s5_c_im, v_s5_d, v_s5_w_glu, v_s5_b_glu, v_gdn_conv_w, v_gdn_a_log, v_gdn_dt_bias, v_gdn_out_norm_g, v_c_norm_g, v_c_w_qkv, v_c_w_out, v_c_q_norm_g, v_c_k_norm_g, v_c_rel_bias, v_mem_norm_g, v_xa_norm_g, v_xa_w_q, v_xa_w_kv, v_xa_w_out, v_xa_q_norm_g, v_xa_k_norm_g, v_f_norm_g, v_f_w_gate, v_f_w_up, v_f_w_down):
    args = locals()
    p = {n: args[n] for n in WEIGHTS}
    m = {n: args["m_" + n] for n in WEIGHTS}
    v = {n: args["v_" + n] for n in WEIGHTS}
    chip = 2 * lax.axis_index("x") + lax.axis_index("y")

    carry = x[0]
    gathers = {}
    for layer in range(DEPTH):
        for grp in GROUPS:
            src = pack_group_shards(p, layer, grp)
            land = lax.dynamic_update_slice(lax.empty((N_CHIPS,) + src.shape, BF16), src[None], (chip, 0, 0))
            send_sems, recv_sems, src, land, carry = exchange_start(
                src, land, carry, f"gather_start_{layer}{grp}", mode="gather_half")
            gathers[layer, grp] = (src, land, send_sems, recv_sems)
    forwards = {}

    def wpre(layer, grp, carry):
        if (layer, grp) not in forwards:
            src, land, send_sems, recv_sems = gathers[layer, grp]
            src, land = exchange_wait(src, land, send_sems, recv_sems, carry, f"gather_wait_{layer}{grp}",
                                      mode="gather_half")
            send_sems, recv_sems, src, land, carry = exchange_start(
                src, land, carry, f"forward_start_{layer}{grp}", mode="forward_half")
            forwards[layer, grp] = (src, land, send_sems, recv_sems)
        return carry

    def wsrc(layer, grp, after):
        wpre(layer, grp, after)
        src, land, send_sems, recv_sems = forwards[layer, grp]
        _, land = exchange_wait(src, land, send_sems, recv_sems, after, f"forward_wait_{layer}{grp}",
                                mode="forward_half")
        return unpack_group_gathered(land, layer, grp)

    scatters, siblings = [], []
    LAG = 2

    def finish(carry):
        layer, grp, src, land, send_sems, recv_sems = scatters[len(siblings)]
        src, land = exchange_wait(src, land, send_sems, recv_sems, carry, f"scatter_wait_{layer}{grp}", mode="scatter")
        part = sum_own_slabs(src, chip, land, "sum_chips")
        send_sems, recv_sems, part, other, carry = exchange_start(
            part, lax.empty(part.shape, F32), carry, f"sibling_start_{layer}{grp}", mode="sibling")
        siblings.append((layer, grp, part, other, send_sems, recv_sems))
        return carry

    def gsink(layer, grp, gw, carry):
        src = pack_group_grads(gw, layer, grp)
        land = lax.empty((3,) + src.shape[1:], BF16)
        send_sems, recv_sems, src, land, carry = exchange_start(
            src, land, carry, f"scatter_start_{layer}{grp}", mode="scatter")
        scatters.append((layer, grp, src, land, send_sems, recv_sems))
        if len(scatters) > LAG:
            carry = finish(carry)
        return carry

    carry = wpre(0, "B", carry)
    loss_vec, dx, g_small = local_step(carry, mem[0], loss_target[0], p, wsrc, gsink, wpre)

    full_shapes = {n: ((2, 4, 1536) if n == "gdn_conv_w" else (2, D) if n == "c_norm_g" else p[n].shape)
                   for n in SMALL}
    small_mine = pack_small(g_small, extra=loss_vec)
    (small_other,) = sibling_exchange([small_mine], "sibling_small")
    (small_chip,) = row_fwd("sum_small_cores", lambda a, b: (a + b,), [small_mine, small_other], [], [(128, F32)],
                            tile=128)
    small_sum = sum_slabs(chip_gather(small_chip, "gather_small"), "sum_small")
    g_s, rest = unpack_small(small_sum, full_shapes)
    (loss11,) = whole("loss_sum", lambda t: (jnp.sum(jnp.sum(t, axis=1, keepdims=True), axis=0, keepdims=True),),
                      [rest[:D // 128]], [((1, 1), F32)])
    for n, (axis, width) in SHARDED_SMALL.items():
        g_s[n] = lax.dynamic_slice_in_dim(g_s[n], chip * width, width, axis=axis)
    delta, new_m, new_v = {}, {}, {}
    for n in SMALL:
        shape = p[n].shape
        two = (1, shape[0]) if len(shape) == 1 else (int(np.prod(shape[:-1])), shape[-1])
        outs = whole("adam_" + n, f_adam, [t.reshape(two) for t in (p[n], g_s[n], m[n], v[n])], [(two, F32)] * 3)
        delta[n], new_m[n], new_v[n] = (o.reshape(shape) for o in outs)

    while len(siblings) < len(scatters):
        dx = finish(dx)

    per_layer = {}
    for layer, grp, part, other, send_sems, recv_sems in siblings:
        part, other = exchange_wait(part, other, send_sems, recv_sems, dx, f"sibling_wait_{layer}{grp}",
                                    mode="sibling")
        (total,) = row_fwd("sum_cores", lambda a, b: (a + b,), [part, other], [], [(D, F32)],
                           tile=_tile(part.shape[0], 512, 16))
        for name, g in unpack_group_reduced(total, layer, grp).items():
            per_layer.setdefault(name, {})[layer] = g
    grads = {name: jnp.stack([d[k] for k in sorted(d)]) for name, d in per_layer.items()}
    grads.update(g_s)
    for name in BIG:
        if name in ADAM_TRANSPOSED:
            t = lambda a: jnp.swapaxes(a, 1, 2)
            outs = adam(t(p[name]), grads[name], t(m[name]), t(v[name]), "adam_" + name)
            delta[name], new_m[name], new_v[name] = (t(o) for o in outs)
            grads[name] = t(grads[name])
        else:
            delta[name], new_m[name], new_v[name] = adam(p[name], grads[name], m[name], v[name], "adam_" + name)

    return (loss11[0, 0], dx[None], *[grads[n] for n in WEIGHTS], *[delta[n] for n in WEIGHTS],
            *[new_m[n] for n in WEIGHTS], *[new_v[n] for n in WEIGHTS])
```

```python
import numpy as np
import jax
import jax.numpy as jnp
from jax import lax
from jax.experimental import pallas as pl
from jax.experimental.pallas import tpu as pltpu

F32 = jnp.float32
BF16 = jnp.bfloat16
MESH = pl.DeviceIdType.MESH

D = 1024
CHUNK = 64
EPS = 1e-6
S5_W = 512
S5_G = 32
S5_C = 16
S5_P = 64
S5_GB = 4
GDN_H = 4
GDN_D = 128
CA_H = 16
CA_D = 64
CA_LEFT = 8
CA_BAND = (CA_LEFT + 1) * CHUNK
CA_PAD = CA_LEFT * CHUNK
MAX_REL = 128
XA_H = 4
XA_D = 256
FFN = 2816
DEPTH = 4
N_CHIPS = 4
LR, B1, B2, AEPS, WD, STEP = 0.001, 0.9, 0.999, 1e-08, 0.01, 10

VMEM_LIMIT = 56 * 1024 * 1024
ROW_TILE = 256


def _cp(*sem):
    return pltpu.CompilerParams(dimension_semantics=sem, vmem_limit_bytes=VMEM_LIMIT)


def _dg(a, b, ca, cb):
    return lax.dot_general(a.astype(BF16), b.astype(BF16), (((ca,), (cb,)), ((), ())),
                           preferred_element_type=F32)


@jax.custom_vjp
def mm(a, b):
    return _dg(a, b, 1, 0)


@jax.custom_vjp
def mm_nt(a, b):
    return _dg(a, b, 1, 1)


@jax.custom_vjp
def mm_tn(a, b):
    return _dg(a, b, 0, 0)


mm.defvjp(lambda a, b: (mm(a, b), (a, b)), lambda r, g: (mm_nt(g, r[1]), mm_tn(r[0], g)))
mm_nt.defvjp(lambda a, b: (mm_nt(a, b), (a, b)), lambda r, g: (mm(g, r[1]), mm_tn(g, r[0])))
mm_tn.defvjp(lambda a, b: (mm_tn(a, b), (a, b)), lambda r, g: (mm_nt(r[1], g), mm(r[0], g)))


def _split(a):
    hi = a.astype(BF16)
    return hi, (a - hi.astype(F32)).astype(BF16)


def _tri_mm(v, upper):
    T = v.shape[0]
    r = lax.broadcasted_iota(jnp.int32, (T, T), 0)
    c = lax.broadcasted_iota(jnp.int32, (T, T), 1)
    m = ((c >= r) if upper else (r >= c)).astype(BF16)
    hi, lo = _split(v)
    d = lambda u: lax.dot_general(m, u, (((1,), (0,)), ((), ())), preferred_element_type=F32)
    return d(hi) + d(lo)


@jax.custom_vjp
def cumsum_rows(v):
    return _tri_mm(v, False)


cumsum_rows.defvjp(lambda v: (_tri_mm(v, False), None), lambda _, g: (_tri_mm(g, True),))


def _rms(x, g):
    return x * lax.rsqrt(jnp.mean(x * x, axis=-1, keepdims=True) + EPS) * g


def _softmax(s):
    e = jnp.exp(s - lax.stop_gradient(jnp.max(s, axis=-1, keepdims=True)))
    return e / jnp.sum(e, axis=-1, keepdims=True)


def _softplus(x):
    return jnp.maximum(x, 0.0) + jnp.log(1.0 + jnp.exp(-jnp.abs(x)))


def _tile(n, cap, align):
    if n <= cap:
        return n
    best = None
    for d in range(align, cap + 1, align):
        if n % d == 0:
            best = d
    assert best is not None, (n, cap, align)
    return best


def matmul(mode, a, b, name, out_dtype=F32, add=None, norm_out=None, norm_bwd=None):
    if mode == "nn":
        (M, K), (K2, N) = a.shape, b.shape
    elif mode == "nt":
        (M, K), (N, K2) = a.shape, b.shape
    else:
        (K, M), (K2, N) = a.shape, b.shape
    assert K == K2, (mode, a.shape, b.shape)
    if mode != "tn" and norm_out is None and norm_bwd is None:
        tm, tn, tk = _tile(M, 1024, 128), _tile(N, 1024, 128), _tile(K, 2048, 128)
    else:
        tm, tn, tk = _tile(M, 512, 128), _tile(N, 1536, 128), _tile(K, 2048, 128)
    nk = K // tk
    if mode == "nn":
        a_spec = pl.BlockSpec((tm, tk), lambda i, j, k: (i, k))
        b_spec = pl.BlockSpec((tk, tn), lambda i, j, k: (k, j))
        dn = (((1,), (0,)), ((), ()))
    elif mode == "nt":
        a_spec = pl.BlockSpec((tm, tk), lambda i, j, k: (i, k))
        b_spec = pl.BlockSpec((tn, tk), lambda i, j, k: (j, k))
        dn = (((1,), (1,)), ((), ()))
    else:
        a_spec = pl.BlockSpec((tk, tm), lambda i, j, k: (k, i))
        b_spec = pl.BlockSpec((tk, tn), lambda i, j, k: (k, j))
        dn = (((0,), (0,)), ((), ()))
    o_spec = pl.BlockSpec((tm, tn), lambda i, j, k: (i, j))
    has_add = add is not None
    g_spec = pl.BlockSpec((1, tn), lambda i, j, k: (0, j))
    extra, extra_specs, out_shapes, out_specs = [], [], [jax.ShapeDtypeStruct((M, N), out_dtype)], [o_spec]
    sem = ("parallel", "parallel", "arbitrary")
    if norm_out is not None:
        assert tn == N
        extra, extra_specs = [norm_out], [g_spec]
        out_shapes.append(jax.ShapeDtypeStruct((M, N), BF16))
        out_specs.append(o_spec)
    if norm_bwd is not None:
        assert tn == N
        x_in, g_in, res_in = norm_bwd
        extra, extra_specs = [x_in, g_in, res_in], [o_spec, g_spec, o_spec]
        out_shapes.append(jax.ShapeDtypeStruct((1, N), F32))
        out_specs.append(g_spec)
        sem = ("arbitrary", "arbitrary", "arbitrary")
    n_in = 2 + int(has_add) + len(extra)
    n_out = len(out_shapes)

    def body(*refs):
        a_ref, b_ref = refs[0], refs[1]
        add_ref = refs[2] if has_add else None
        extra_refs = refs[2 + int(has_add):n_in]
        o_ref = refs[n_in]
        acc_ref = refs[-1]
        i = pl.program_id(0)
        p = lax.dot_general(a_ref[...].astype(BF16), b_ref[...].astype(BF16), dn,
                            preferred_element_type=F32)

        def finish(total):
            if has_add:
                total = total + add_ref[...]
            if norm_out is not None:
                o_ref[...] = total.astype(o_ref.dtype)
                refs[n_in + 1][...] = _rms(total, extra_refs[0][...]).astype(BF16)
            elif norm_bwd is not None:
                x_ref, g_ref, res_ref = extra_refs
                _, vjp = jax.vjp(f_norm_res, x_ref[...], g_ref[...])
                dx, dg = vjp((total, res_ref[...]))
                o_ref[...] = dx
                dg_ref = refs[n_in + 1]

                @pl.when(i == 0)
                def _():
                    dg_ref[...] = dg

                @pl.when(i > 0)
                def _():
                    dg_ref[...] += dg
            else:
                o_ref[...] = total.astype(o_ref.dtype)

        if nk == 1:
            finish(p)
        else:
            k = pl.program_id(2)

            @pl.when(k == 0)
            def _():
                acc_ref[...] = p

            @pl.when(k > 0)
            def _():
                acc_ref[...] += p

            @pl.when(k == nk - 1)
            def _():
                finish(acc_ref[...])

    ins = [a, b] + ([add] if has_add else []) + extra
    specs = [a_spec, b_spec] + ([o_spec] if has_add else []) + extra_specs
    out = pl.pallas_call(
        body, name=name, grid=(M // tm, N // tn, nk), in_specs=specs, out_specs=out_specs,
        out_shape=out_shapes, scratch_shapes=[pltpu.VMEM((tm, tn), F32)],
        compiler_params=_cp(*sem),
    )(*ins)
    return out[0] if n_out == 1 else out


def ffn_in(h, wg, wu):
    (S, K), F = h.shape, wg.shape[0]
    tm, tn = _tile(S, 1024, 128), _tile(F, 1536, 128)
    dn = (((1,), (1,)), ((), ()))

    def body(h_ref, wg_ref, wu_ref, g_ref, u_ref, act_ref):
        a = h_ref[...].astype(BF16)
        g = lax.dot_general(a, wg_ref[...].astype(BF16), dn, preferred_element_type=F32)
        u = lax.dot_general(a, wu_ref[...].astype(BF16), dn, preferred_element_type=F32)
        g_ref[...] = g.astype(BF16)
        u_ref[...] = u.astype(BF16)
        act_ref[...] = f_swiglu(g, u)[0].astype(BF16)

    w_spec = pl.BlockSpec((tn, K), lambda i, j: (j, 0))
    o_spec = pl.BlockSpec((tm, tn), lambda i, j: (i, j))
    sd = jax.ShapeDtypeStruct
    return pl.pallas_call(
        body, name="ffn_in", grid=(S // tm, F // tn),
        in_specs=[pl.BlockSpec((tm, K), lambda i, j: (i, 0)), w_spec, w_spec], out_specs=[o_spec] * 3,
        out_shape=[sd((S, F), BF16), sd((S, F), BF16), sd((S, F), BF16)],
        compiler_params=_cp("parallel", "parallel"),
    )(h, wg, wu)


def ffn_dact(dy, wd, gate, up):
    (S, K), F = dy.shape, wd.shape[0]
    tm, tn = _tile(S, 1024, 128), _tile(F, 1536, 128)
    dn = (((1,), (1,)), ((), ()))

    def body(dy_ref, wd_ref, g_ref, u_ref, dg_ref, du_ref):
        dact = lax.dot_general(dy_ref[...].astype(BF16), wd_ref[...].astype(BF16), dn, preferred_element_type=F32)
        _, vjp = jax.vjp(f_swiglu, g_ref[...].astype(F32), u_ref[...].astype(F32))
        dg, du = vjp((dact,))
        dg_ref[...] = dg.astype(BF16)
        du_ref[...] = du.astype(BF16)

    o_spec = pl.BlockSpec((tm, tn), lambda i, j: (i, j))
    sd = jax.ShapeDtypeStruct
    return pl.pallas_call(
        body, name="ffn_dact", grid=(S // tm, F // tn),
        in_specs=[pl.BlockSpec((tm, K), lambda i, j: (i, 0)), pl.BlockSpec((tn, K), lambda i, j: (j, 0)),
                  o_spec, o_spec],
        out_specs=[o_spec] * 2, out_shape=[sd((S, F), BF16)] * 2,
        compiler_params=_cp("parallel", "parallel"),
    )(dy, wd, gate, up)


def _row_spec(arr, tile):
    if isinstance(arr, tuple):
        a, w, cb = arr[:3]
        ro = (arr[3] // tile) if len(arr) > 3 else 0
        assert len(arr) < 4 or arr[3] % tile == 0
        return a, pl.BlockSpec((tile, w), lambda i, cb=cb, ro=ro: (i + ro, cb)), (tile, w)
    if arr.ndim == 3:
        d0, _, d2 = arr.shape
        return arr, pl.BlockSpec((d0, tile, d2), lambda i: (0, i, 0)), (d0, tile, d2)
    return arr, pl.BlockSpec((tile, arr.shape[1]), lambda i: (i, 0)), (tile, arr.shape[1])


def _full_spec(arr):
    nd = arr.ndim
    return pl.BlockSpec(arr.shape, lambda i, nd=nd: (0,) * nd)


def _n_rows(arr):
    a = arr[0] if isinstance(arr, tuple) else arr
    return a.shape[1] if a.ndim == 3 else a.shape[0]


def _row_out_shape(shape_tail, n, dtype):
    if isinstance(shape_tail, tuple):
        d0, d2 = shape_tail
        return (jax.ShapeDtypeStruct((d0, n, d2), dtype),
                lambda tile: pl.BlockSpec((d0, tile, d2), lambda i: (0, i, 0)))
    return (jax.ShapeDtypeStruct((n, shape_tail), dtype),
            lambda tile: pl.BlockSpec((tile, shape_tail), lambda i: (i, 0)))


def _f32(v):
    return v.astype(F32) if v.dtype == BF16 else v


def row_fwd(name, fn, rows, fulls, outs, tile=ROW_TILE):
    n = _n_rows(rows[0])
    tile = min(tile, n)
    assert n % tile == 0, (name, n, tile)
    rs = [_row_spec(r, tile) for r in rows]
    os_ = [_row_out_shape(w, n, dt) for w, dt in outs]
    nr, nf = len(rows), len(fulls)

    def body(*refs):
        vals = [_f32(r[...]) for r in refs[:nr + nf]]
        res = fn(*vals)
        for r, v in zip(refs[nr + nf:], res):
            r[...] = v.astype(r.dtype)

    out = pl.pallas_call(
        body, name=name, grid=(n // tile,),
        in_specs=[s for _, s, _ in rs] + [_full_spec(f) for f in fulls],
        out_specs=[mk(tile) for _, mk in os_], out_shape=[sh for sh, _ in os_],
        compiler_params=_cp("parallel"),
    )(*[a for a, _, _ in rs], *fulls)
    return out


def row_bwd(name, fn, rows, fulls, cts, want_rows, want_fulls, row_dtypes=None, tile=ROW_TILE):
    n = _n_rows(rows[0])
    tile = min(tile, n)
    assert n % tile == 0, (name, n, tile)
    rs = [_row_spec(r, tile) for r in rows]
    cs = [_row_spec(c, tile) for c in cts]
    nr, nf, nc = len(rows), len(fulls), len(cts)
    row_dtypes = row_dtypes or [F32] * len(want_rows)
    out_shapes, out_specs = [], []
    for k, idx in enumerate(want_rows):
        a, _, blk = rs[idx]
        if len(blk) == 3:
            sh, mk = _row_out_shape((blk[0], blk[2]), n, row_dtypes[k])
        else:
            sh, mk = _row_out_shape(blk[1], n, row_dtypes[k])
        out_shapes.append(sh)
        out_specs.append(mk(tile))
    for idx in want_fulls:
        out_shapes.append(jax.ShapeDtypeStruct(fulls[idx].shape, F32))
        out_specs.append(_full_spec(fulls[idx]))
    n_wr = len(want_rows)

    def body(*refs):
        i = pl.program_id(0)
        vals = [_f32(r[...]) for r in refs[:nr + nf]]
        ct_vals = [_f32(r[...]) for r in refs[nr + nf:nr + nf + nc]]
        outs = refs[nr + nf + nc:]
        _, vjp = jax.vjp(fn, *vals)
        grads = vjp(tuple(ct_vals))
        for k, idx in enumerate(want_rows):
            outs[k][...] = grads[idx].astype(outs[k].dtype)
        for k, idx in enumerate(want_fulls):
            o = outs[n_wr + k]
            g = grads[nr + idx]

            @pl.when(i == 0)
            def _(o=o, g=g):
                o[...] = g

            @pl.when(i > 0)
            def _(o=o, g=g):
                o[...] += g

    out = pl.pallas_call(
        body, name=name, grid=(n // tile,),
        in_specs=[s for _, s, _ in rs] + [_full_spec(f) for f in fulls] + [s for _, s, _ in cs],
        out_specs=out_specs, out_shape=out_shapes,
        compiler_params=_cp("arbitrary"),
    )(*[a for a, _, _ in rs], *fulls, *[a for a, _, _ in cs])
    return out


def whole(name, fn, args, outs):
    def body(*refs):
        res = fn(*[r[...] for r in refs[:len(args)]])
        for r, v in zip(refs[len(args):], res):
            r[...] = v.astype(r.dtype)

    return pl.pallas_call(
        body, name=name, out_shape=[jax.ShapeDtypeStruct(s, d) for s, d in outs],
        compiler_params=pltpu.CompilerParams(vmem_limit_bytes=VMEM_LIMIT),
    )(*args)


def f_norm(x, g):
    return (_rms(x, g),)


def f_norm_res(x, g):
    return _rms(x, g), x


def f_swiglu(g, u):
    return (g * jax.nn.sigmoid(g) * u,)


def f_glu(y, w, b):
    h = jax.nn.gelu(y)
    return (h * jax.nn.sigmoid(mm(h, w) + b),)


def f_xattn(q, kv, qg, kg):
    outs = []
    for h in range(XA_H):
        sl = slice(h * XA_D, (h + 1) * XA_D)
        qn = _rms(q[:, sl], qg)
        kn = _rms(kv[:, sl], kg)
        vh = kv[:, D + h * XA_D:D + (h + 1) * XA_D]
        p = _softmax(mm_nt(qn, kn) * (XA_D ** -0.5))
        outs.append(mm(p, vh))
    return (jnp.concatenate(outs, axis=-1),)


def f_adam(w, g, m, v):
    m2 = B1 * m + (1.0 - B1) * g
    v2 = B2 * v + (1.0 - B2) * (g * g)
    m_hat = m2 / (1.0 - B1 ** STEP)
    v_hat = v2 / (1.0 - B2 ** STEP)
    delta = -LR * (m_hat / (jnp.sqrt(v_hat) + AEPS) + WD * w)
    return delta, m2, v2


S5_NTAB, S5_NROW = 4, 6


def _s5_tables(are, aim, ldt):
    T = CHUNK
    dt = jnp.exp(ldt)
    ar, ai = are * dt, aim * dt
    t = lax.broadcasted_iota(jnp.int32, (T, 1), 0).astype(F32)
    mag, inv = jnp.exp(t * ar), jnp.exp(-t * ar)
    cs, sn = jnp.cos(t * ai), jnp.sin(t * ai)
    e_re, e_im = mag * cs, mag * sn
    n_re, n_im = inv * cs, -inv * sn
    l_re, l_im = jnp.exp(ar) * jnp.cos(ai), jnp.exp(ar) * jnp.sin(ai)
    den = are * are + aim * aim
    k_re = ((l_re - 1.0) * are + l_im * aim) / den
    k_im = (l_im * are - (l_re - 1.0) * aim) / den
    tl = float(T - 1)
    m_re, m_im = jnp.exp(tl * ar) * jnp.cos(tl * ai), jnp.exp(tl * ar) * jnp.sin(tl * ai)
    return (e_re, e_im, n_re, n_im), (l_re, l_im, k_re, k_im, m_re, m_im)


def _s5_chunk(u, sre, sim, tabs, rows, b_re, b_im, c_re, c_im, dv):
    e_re, e_im, n_re, n_im = tabs
    l_re, l_im, k_re, k_im, m_re, m_im = rows
    x_re, x_im = mm(u, b_re), mm(u, b_im)
    bu_re = k_re * x_re - k_im * x_im
    bu_im = k_re * x_im + k_im * x_re
    v_re = bu_re * n_re - bu_im * n_im
    v_im = bu_re * n_im + bu_im * n_re
    p_re = l_re * sre - l_im * sim
    p_im = l_re * sim + l_im * sre
    w_re = cumsum_rows(v_re) + p_re
    w_im = cumsum_rows(v_im) + p_im
    s_re = e_re * w_re - e_im * w_im
    s_im = e_re * w_im + e_im * w_re
    y = mm(s_re, c_re) - mm(s_im, c_im) + dv * u
    z_re = jnp.sum(v_re, axis=0, keepdims=True) + p_re
    z_im = jnp.sum(v_im, axis=0, keepdims=True) + p_im
    return y, m_re * z_re - m_im * z_im, m_re * z_im + m_im * z_re


def _s5_fill_tables(are_ref, aim_ref, ldt_ref, tab, row):
    for g in range(S5_GB):
        ls = slice(512 * g, 512 * (g + 1))
        tabs, rows = _s5_tables(are_ref[:, ls], aim_ref[:, ls], ldt_ref[:, ls])
        for k, t in enumerate(tabs):
            tab[k, :, ls] = t
        for k, r in enumerate(rows):
            row[k:k + 1, ls] = r


def _s5_read_tables(tab, row, ls):
    return (tuple(tab[k, :, ls] for k in range(S5_NTAB)), tuple(row[k:k + 1, ls] for k in range(S5_NROW)))


def _s5_specs(nc, rev):
    T = CHUNK

    def ci(c):
        return nc - 1 - c if rev else c

    u_spec = pl.BlockSpec((T, S5_W), lambda c: (ci(c), 0))
    p_spec = pl.BlockSpec((1, S5_G * S5_P), lambda c: (0, 0))
    b_spec = pl.BlockSpec((S5_GB, 128, 512), lambda c: (0, 0, 0))
    c_spec = pl.BlockSpec((S5_GB, 512, 128), lambda c: (0, 0, 0))
    d_spec = pl.BlockSpec((1, S5_W), lambda c: (0, 0))
    st_spec = pl.BlockSpec((None, 2, S5_G * S5_P), lambda c: (ci(c), 0, 0))
    return u_spec, p_spec, b_spec, c_spec, d_spec, st_spec


def s5_fwd(pm, are, aim, ldt, b_re, b_im, c_re, c_im, dv):
    S = pm.shape[0]
    nc = S // CHUNK
    u_spec, p_spec, b_spec, c_spec, d_spec, st_spec = _s5_specs(nc, False)

    def body(u_ref, are_ref, aim_ref, ldt_ref, bre_ref, bim_ref, cre_ref, cim_ref, dv_ref,
             y_ref, st_ref, state, tab, row):
        c = pl.program_id(0)

        @pl.when(c == 0)
        def _():
            state[...] = jnp.zeros_like(state)
            _s5_fill_tables(are_ref, aim_ref, ldt_ref, tab, row)

        st_ref[...] = state[...]
        for g in range(S5_GB):
            lu, ls = slice(128 * g, 128 * (g + 1)), slice(512 * g, 512 * (g + 1))
            tabs, rows = _s5_read_tables(tab, row, ls)
            y, e_re, e_im = _s5_chunk(u_ref[:, lu], state[0:1, ls], state[1:2, ls], tabs, rows,
                                      bre_ref[g], bim_ref[g], cre_ref[g], cim_ref[g], dv_ref[:, lu])
            y_ref[:, lu] = y
            state[0:1, ls] = e_re
            state[1:2, ls] = e_im

    n_state = S5_G * S5_P
    return pl.pallas_call(
        body, name="s5_fwd", grid=(nc,),
        in_specs=[u_spec, p_spec, p_spec, p_spec, b_spec, b_spec, c_spec, c_spec, d_spec],
        out_specs=[u_spec, st_spec],
        out_shape=[jax.ShapeDtypeStruct((S, S5_W), F32), jax.ShapeDtypeStruct((nc, 2, n_state), F32)],
        scratch_shapes=[pltpu.VMEM((2, n_state), F32), pltpu.VMEM((S5_NTAB, CHUNK, n_state), F32),
                        pltpu.VMEM((8, n_state), F32)],
        compiler_params=_cp("arbitrary"),
    )(pm, are, aim, ldt, b_re, b_im, c_re, c_im, dv)


def s5_bwd(pm, st, dy, are, aim, ldt, b_re, b_im, c_re, c_im, dv):
    S = pm.shape[0]
    nc = S // CHUNK
    u_spec, p_spec, b_spec, c_spec, d_spec, st_spec = _s5_specs(nc, True)

    def body(u_ref, st_ref, dy_ref, are_ref, aim_ref, ldt_ref, bre_ref, bim_ref, cre_ref, cim_ref, dv_ref,
             du_ref, dare_ref, daim_ref, dldt_ref, dbre_ref, dbim_ref, dcre_ref, dcim_ref, ddv_ref,
             dstate, tab, row, dtab, drow):
        c = pl.program_id(0)

        @pl.when(c == 0)
        def _():
            dstate[...] = jnp.zeros_like(dstate)
            dtab[...] = jnp.zeros_like(dtab)
            drow[...] = jnp.zeros_like(drow)
            _s5_fill_tables(are_ref, aim_ref, ldt_ref, tab, row)

        for g in range(S5_GB):
            lu, ls = slice(128 * g, 128 * (g + 1)), slice(512 * g, 512 * (g + 1))
            every = slice(None)
            tabs, rows = _s5_read_tables(tab, row, ls)
            args = (u_ref[:, lu], st_ref[0:1, ls], st_ref[1:2, ls], tabs, rows,
                    bre_ref[g], bim_ref[g], cre_ref[g], cim_ref[g], dv_ref[:, lu])
            _, vjp = jax.vjp(_s5_chunk, *args)
            gr = vjp((dy_ref[:, lu], dstate[0:1, ls], dstate[1:2, ls]))
            du_ref[:, lu] = gr[0]
            dstate[0:1, ls] = gr[1]
            dstate[1:2, ls] = gr[2]
            for k, t in enumerate(gr[3]):
                dtab[k, :, ls] += t
            for k, r in enumerate(gr[4]):
                drow[k:k + 1, ls] += r
            accs = ((dbre_ref, (g,)), (dbim_ref, (g,)), (dcre_ref, (g,)), (dcim_ref, (g,)), (ddv_ref, (every, lu)))
            for (o, idx), gv in zip(accs, gr[5:]):
                @pl.when(c == 0)
                def _(o=o, idx=idx, gv=gv):
                    o[idx] = gv

                @pl.when(c > 0)
                def _(o=o, idx=idx, gv=gv):
                    o[idx] += gv

        @pl.when(c == nc - 1)
        def _():
            for g in range(S5_GB):
                ls = slice(512 * g, 512 * (g + 1))
                _, vjp = jax.vjp(_s5_tables, are_ref[:, ls], aim_ref[:, ls], ldt_ref[:, ls])
                dtabs, drows = _s5_read_tables(dtab, drow, ls)
                ga, gi, gl = vjp((dtabs, drows))
                dare_ref[:, ls] = ga
                daim_ref[:, ls] = gi
                dldt_ref[:, ls] = gl

    n_state = S5_G * S5_P
    return pl.pallas_call(
        body, name="s5_bwd", grid=(nc,),
        in_specs=[u_spec, st_spec, u_spec, p_spec, p_spec, p_spec, b_spec, b_spec, c_spec, c_spec, d_spec],
        out_specs=[u_spec, p_spec, p_spec, p_spec, b_spec, b_spec, c_spec, c_spec, d_spec],
        out_shape=[jax.ShapeDtypeStruct((S, S5_W), F32)] + [jax.ShapeDtypeStruct((1, n_state), F32)] * 3
        + [jax.ShapeDtypeStruct((S5_GB, 128, 512), F32)] * 2 + [jax.ShapeDtypeStruct((S5_GB, 512, 128), F32)] * 2
        + [jax.ShapeDtypeStruct((1, S5_W), F32)],
        scratch_shapes=[pltpu.VMEM((2, n_state), F32), pltpu.VMEM((S5_NTAB, CHUNK, n_state), F32),
                        pltpu.VMEM((8, n_state), F32), pltpu.VMEM((S5_NTAB, CHUNK, n_state), F32),
                        pltpu.VMEM((8, n_state), F32)],
        compiler_params=_cp("arbitrary"),
    )(pm, st, dy, are, aim, ldt, b_re, b_im, c_re, c_im, dv)


def conv_fwd(pm, w):
    S = pm.shape[0]

    def body(x_ref, w_ref, o_ref, pad):
        x = x_ref[...]
        pad[0:8, :] = jnp.zeros((8, 128), F32)
        pad[8:, :] = x
        y = (w_ref[3:4, :] * x + w_ref[2:3, :] * pad[7:7 + S, :] + w_ref[1:2, :] * pad[6:6 + S, :]
             + w_ref[0:1, :] * pad[5:5 + S, :])
        o_ref[...] = y * jax.nn.sigmoid(y)

    return pl.pallas_call(
        body, name="conv_fwd", grid=(12,),
        in_specs=[pl.BlockSpec((S, 128), lambda j: (0, 4 + j)), pl.BlockSpec((4, 128), lambda j: (0, j))],
        out_specs=pl.BlockSpec((S, 128), lambda j: (0, j)),
        out_shape=jax.ShapeDtypeStruct((S, 1536), F32),
        scratch_shapes=[pltpu.VMEM((S + 8, 128), F32)],
        compiler_params=_cp("parallel"),
    )(pm, w)


def conv_bwd(pm, w, dout):
    S = pm.shape[0]

    def body(x_ref, w_ref, do_ref, dx_ref, dw_ref, pad, dpad):
        x = x_ref[...]
        pad[0:8, :] = jnp.zeros((8, 128), F32)
        pad[8:, :] = x
        xs = [pad[5:5 + S, :], pad[6:6 + S, :], pad[7:7 + S, :], x]
        y = w_ref[0:1, :] * xs[0] + w_ref[1:2, :] * xs[1] + w_ref[2:3, :] * xs[2] + w_ref[3:4, :] * xs[3]
        sg = jax.nn.sigmoid(y)
        dy = do_ref[...] * (sg + y * sg * (1.0 - sg))
        dpad[0:S, :] = dy
        dpad[S:, :] = jnp.zeros((8, 128), F32)
        dx_ref[...] = (w_ref[3:4, :] * dy + w_ref[2:3, :] * dpad[1:1 + S, :] + w_ref[1:2, :] * dpad[2:2 + S, :]
                       + w_ref[0:1, :] * dpad[3:3 + S, :])
        for i in range(4):
            dw_ref[i:i + 1, :] = jnp.sum(dy * xs[i], axis=0, keepdims=True)

    return pl.pallas_call(
        body, name="conv_bwd", grid=(12,),
        in_specs=[pl.BlockSpec((S, 128), lambda j: (0, 4 + j)), pl.BlockSpec((4, 128), lambda j: (0, j)),
                  pl.BlockSpec((S, 128), lambda j: (0, j))],
        out_specs=[pl.BlockSpec((S, 128), lambda j: (0, j)), pl.BlockSpec((4, 128), lambda j: (0, j))],
        out_shape=[jax.ShapeDtypeStruct((S, 1536), F32), jax.ShapeDtypeStruct((4, 1536), F32)],
        scratch_shapes=[pltpu.VMEM((S + 8, 128), F32), pltpu.VMEM((S + 8, 128), F32)],
        compiler_params=_cp("parallel"),
    )(pm, w, dout)


GDN_SUP = 4
GDN_ROWS = GDN_SUP * CHUNK


@jax.custom_vjp
def _saved_inverse(a, x):
    return x


_saved_inverse.defvjp(lambda a, x: (x, x),
                      lambda x, g: (-mm_tn(x, mm_nt(g, x)), jnp.zeros_like(x)))


def _gdn_chunk(q, k, v, gate, al, bl, alog, dtb, og, state, inv=None, want_inv=False):
    R = q.shape[0]
    r = lax.broadcasted_iota(jnp.int32, (R, R), 0)
    c = lax.broadcasted_iota(jnp.int32, (R, R), 1)
    same = (r // CHUNK) == (c // CHUNK)
    eye = (r == c).astype(F32)
    strict, causal, upper = same & (r > c), same & (r >= c), same & (r <= c)
    qn = q * lax.rsqrt(jnp.sum(q * q, axis=-1, keepdims=True) + EPS) * (GDN_D ** -0.5)
    kn = k * lax.rsqrt(jnp.sum(k * k, axis=-1, keepdims=True) + EPS)
    beta = jnp.sum(eye * jax.nn.sigmoid(bl), axis=1, keepdims=True)
    g_row = -jnp.exp(alog) * _softplus(al + dtb)
    g = jnp.sum(eye * g_row, axis=1, keepdims=True)
    gc_col = jnp.sum(jnp.where(causal, g_row, 0.0), axis=1, keepdims=True)
    gc_row = jnp.sum(jnp.where(upper, g, 0.0), axis=0, keepdims=True)
    gtot = jnp.sum(jnp.where(same, g_row, 0.0), axis=1, keepdims=True)
    gamma = jnp.exp(gc_col)
    diff = gc_col - gc_row
    d_strict = jnp.where(strict, jnp.exp(jnp.where(strict, diff, 0.0)), 0.0)
    d_causal = jnp.where(causal, jnp.exp(jnp.where(causal, diff, 0.0)), 0.0)
    a = beta * mm_nt(kn, kn) * d_strict
    if inv is None:
        p = -a
        x = eye + p
        for _ in range(5):
            p = mm(p, p)
            x = x + mm(x, p)
    else:
        x = _saved_inverse(a, inv)
    u_new = mm(x, beta * v)
    w_k = mm(x, (beta * gamma) * kn)
    qk = mm_nt(qn, kn) * d_causal
    q_g = qn * gamma
    k_tail = kn * jnp.exp(gtot - gc_col)
    ws, os_ = [], []
    for i in range(R // CHUNK):
        rows = slice(CHUNK * i, CHUNK * (i + 1))
        w_i = u_new[rows] - mm(w_k[rows], state)
        os_.append(mm(q_g[rows], state))
        decay = jnp.exp(jnp.sum(g[rows], axis=0, keepdims=True))
        state = decay * state + mm_tn(k_tail[rows], w_i)
        ws.append(w_i)
    o = jnp.concatenate(os_, axis=0) + mm(qk, jnp.concatenate(ws, axis=0))
    out = _rms(o, og) * (gate * jax.nn.sigmoid(gate))
    return (out, state, x) if want_inv else (out, state)


def _gdn_specs(nc, rev):
    def ci(c):
        return nc - 1 - c if rev else c

    def blk(cb):
        return pl.BlockSpec((GDN_ROWS, 512), lambda c: (ci(c), cb))

    col = lambda n: pl.BlockSpec((n, GDN_ROWS), lambda c: (0, ci(c)))
    sc = pl.BlockSpec((GDN_H, 1, 1), lambda c: (0, 0, 0))
    og = pl.BlockSpec((1, 128), lambda c: (0, 0))
    st = pl.BlockSpec((GDN_H, None, 128, 128), lambda c: (0, ci(c), 0, 0))
    return blk, col, sc, og, st


def gdn_fwd(qkvc, pm, abt, alog, dtb, og):
    S = qkvc.shape[0]
    nc = S // GDN_ROWS
    blk, col, sc, ogs, st = _gdn_specs(nc, False)

    def body(q_ref, k_ref, v_ref, gate_ref, ab_ref, alog_ref, dtb_ref, og_ref, o_ref, st_ref, inv_ref, state):
        c = pl.program_id(0)

        @pl.when(c == 0)
        def _():
            state[...] = jnp.zeros_like(state)

        st_ref[...] = state[...]
        for h in range(GDN_H):
            sl = slice(GDN_D * h, GDN_D * (h + 1))
            out, new_state, inv = _gdn_chunk(
                q_ref[:, sl], k_ref[:, sl], v_ref[:, sl], gate_ref[:, sl], ab_ref[h:h + 1, :], ab_ref[GDN_H + h:GDN_H + h + 1, :],
                alog_ref[h], dtb_ref[h], og_ref[...], state[h], want_inv=True)
            o_ref[:, sl] = out
            state[h] = new_state
            inv_ref[h] = inv

    inv_spec = pl.BlockSpec((GDN_H, None, GDN_ROWS, GDN_ROWS), lambda c: (0, c, 0, 0))
    return pl.pallas_call(
        body, name="gdn_fwd", grid=(nc,),
        in_specs=[blk(0), blk(1), blk(2), blk(4), col(2 * GDN_H), sc, sc, ogs],
        out_specs=[blk(0), st, inv_spec],
        out_shape=[jax.ShapeDtypeStruct((S, 512), F32), jax.ShapeDtypeStruct((GDN_H, nc, 128, 128), F32),
                   jax.ShapeDtypeStruct((GDN_H, nc, GDN_ROWS, GDN_ROWS), F32)],
        scratch_shapes=[pltpu.VMEM((GDN_H, 128, 128), F32)],
        compiler_params=_cp("arbitrary"),
    )(qkvc, qkvc, qkvc, pm, abt, alog, dtb, og)


def gdn_bwd(qkvc, pm, abt, alog, dtb, og, st, inv, dout):
    S = qkvc.shape[0]
    nc = S // GDN_ROWS
    blk, col, sc, ogs, sts = _gdn_specs(nc, True)

    def body(q_ref, k_ref, v_ref, gate_ref, ab_ref, alog_ref, dtb_ref, og_ref, st_ref, inv_ref, do_ref,
             dqkv_ref, dgate_ref, dal_ref, dbl_ref, dalog_ref, ddtb_ref, dog_ref, dstate):
        c = pl.program_id(0)

        @pl.when(c == 0)
        def _():
            dstate[...] = jnp.zeros_like(dstate)

        for h in range(GDN_H):
            sl = slice(GDN_D * h, GDN_D * (h + 1))
            args = (q_ref[:, sl], k_ref[:, sl], v_ref[:, sl], gate_ref[:, sl], ab_ref[h:h + 1, :], ab_ref[GDN_H + h:GDN_H + h + 1, :],
                    alog_ref[h], dtb_ref[h], og_ref[...], st_ref[h])
            inv_h = inv_ref[h]
            _, vjp = jax.vjp(lambda *a, inv_h=inv_h: _gdn_chunk(*a, inv=inv_h), *args)
            g = vjp((do_ref[:, sl], dstate[h]))
            for part in range(3):
                dqkv_ref[:, slice(512 * part + sl.start, 512 * part + sl.stop)] = g[part]
            dgate_ref[:, sl] = g[3]
            dal_ref[h:h + 1, :] = g[4]
            dbl_ref[h:h + 1, :] = g[5]
            dstate[h] = g[9]
            for o, gv in zip((dalog_ref, ddtb_ref, dog_ref), g[6:9]):
                @pl.when(c == 0)
                def _(o=o, gv=gv, h=h):
                    o[h] = gv

                @pl.when(c > 0)
                def _(o=o, gv=gv, h=h):
                    o[h] += gv

    ogo = pl.BlockSpec((GDN_H, 1, 128), lambda c: (0, 0, 0))
    inv_spec = pl.BlockSpec((GDN_H, None, GDN_ROWS, GDN_ROWS), lambda c: (0, nc - 1 - c, 0, 0))
    sd = jax.ShapeDtypeStruct
    return pl.pallas_call(
        body, name="gdn_bwd", grid=(nc,),
        in_specs=[blk(0), blk(1), blk(2), blk(4), col(2 * GDN_H), sc, sc, ogs, sts, inv_spec, blk(0)],
        out_specs=[pl.BlockSpec((GDN_ROWS, 1536), lambda c: (nc - 1 - c, 0)), blk(0), col(GDN_H), col(GDN_H),
                   sc, sc, ogo],
        out_shape=[sd((S, 1536), F32), sd((S, 512), F32)] + [sd((GDN_H, S), F32)] * 2 + [sd((GDN_H, 1, 1), F32)] * 2
        + [sd((GDN_H, 1, 128), F32)],
        scratch_shapes=[pltpu.VMEM((GDN_H, 128, 128), F32)],
        compiler_params=_cp("arbitrary"),
    )(qkvc, qkvc, qkvc, pm, abt, alog, dtb, og, st, inv, dout)


HG = 8
HG_LANES = HG * CA_D


def _group_mean_raw(y):
    r = lax.broadcasted_iota(jnp.int32, (128, 128), 0)
    c = lax.broadcasted_iota(jnp.int32, (128, 128), 1)
    g = jnp.where((r // CA_D) == (c // CA_D), 1.0 / CA_D, 0.0).astype(BF16)
    d = lambda u: lax.dot_general(u, g, (((1,), (0,)), ((), ())), preferred_element_type=F32)
    outs = []
    for j in range(y.shape[1] // 128):
        hi, lo = _split(y[:, 128 * j:128 * (j + 1)])
        outs.append(d(hi) + d(lo))
    return jnp.concatenate(outs, axis=1)


@jax.custom_vjp
def group_mean(y):
    return _group_mean_raw(y)


group_mean.defvjp(lambda y: (_group_mean_raw(y), None), lambda _, g: (_group_mean_raw(g),))


def f_headnorm(t, g):
    return (t * lax.rsqrt(group_mean(t * t) + EPS) * g,)


def _cattn_chunk(q, kb, vb, bias, valid):
    lane = lax.broadcasted_iota(jnp.int32, (1, 128), 1)
    m0 = (lane < CA_D).astype(F32)
    m1 = 1.0 - m0
    pairs = range(HG // 2)
    sl = [slice(128 * p, 128 * (p + 1)) for p in pairs]
    q2 = [jnp.concatenate([q[:, s] * m0, q[:, s] * m1], axis=0) for s in sl]
    sc = [mm_nt(q2[p], kb[:, sl[p]]) * (CA_D ** -0.5) + bias[sl[p]] for p in pairs]
    pr = [_softmax(jnp.where(valid, s, -1e30)) for s in sc]
    o2 = [mm(pr[p], vb[:, sl[p]]) for p in pairs]
    return jnp.concatenate([o[:CHUNK] * m0 + o[CHUNK:] * m1 for o in o2], axis=1)


def _cattn_valid(c):
    pos = lax.broadcasted_iota(jnp.int32, (1, CA_BAND), 1) + c * CHUNK
    return pos >= CA_PAD


def _cattn_specs(S):
    q_spec = pl.BlockSpec((CHUNK, HG_LANES), lambda h, c: (c, h))
    kv_spec = pl.BlockSpec((S + CA_PAD, HG_LANES), lambda h, c: (0, h))
    b_spec = pl.BlockSpec((HG * CHUNK, CA_BAND), lambda h, c: (h, 0))
    return q_spec, kv_spec, b_spec


def cattn_fwd(qn, kp, vp, bias):
    S = qn.shape[0]
    nc = S // CHUNK
    q_spec, kv_spec, b_spec = _cattn_specs(S)

    def body(q_ref, k_ref, v_ref, b_ref, o_ref):
        c = pl.program_id(1)
        start = pl.multiple_of(c * CHUNK, CHUNK)
        kb = k_ref[pl.ds(start, CA_BAND), :]
        vb = v_ref[pl.ds(start, CA_BAND), :]
        o_ref[...] = _cattn_chunk(q_ref[...], kb, vb, b_ref[...], _cattn_valid(c)).astype(o_ref.dtype)

    return pl.pallas_call(
        body, name="cattn_fwd", grid=(CA_H // HG, nc), in_specs=[q_spec, kv_spec, kv_spec, b_spec],
        out_specs=q_spec, out_shape=jax.ShapeDtypeStruct((S, D), BF16),
        compiler_params=_cp("parallel", "arbitrary"),
    )(qn, kp, vp, bias)


def kv_prep(qkv, kg):
    S = qkv.shape[0]
    tile = ROW_TILE
    lead = CA_PAD // tile

    def body(k_ref, v_ref, g_ref, kp_ref, vp_ref):
        i = pl.program_id(0)

        @pl.when(i < lead)
        def _():
            kp_ref[...] = jnp.zeros_like(kp_ref)
            vp_ref[...] = jnp.zeros_like(vp_ref)

        @pl.when(i >= lead)
        def _():
            kp_ref[...] = f_headnorm(k_ref[...], g_ref[...])[0].astype(BF16)
            vp_ref[...] = v_ref[...].astype(BF16)

    src = lambda cb: pl.BlockSpec((tile, D), lambda i, cb=cb: (jnp.maximum(i - lead, 0), cb))
    out = pl.BlockSpec((tile, D), lambda i: (i, 0))
    return pl.pallas_call(
        body, name="kv_prep", grid=((S + CA_PAD) // tile,),
        in_specs=[src(1), src(2), pl.BlockSpec((1, D), lambda i: (0, 0))], out_specs=[out, out],
        out_shape=[jax.ShapeDtypeStruct((S + CA_PAD, D), BF16)] * 2,
        compiler_params=_cp("parallel"),
    )(qkv, qkv, kg)


def cattn_bwd(qn, kp, vp, bias, do):
    S = qn.shape[0]
    nc = S // CHUNK
    q_spec, kv_spec, b_spec = _cattn_specs(S)

    def body(q_ref, k_ref, v_ref, b_ref, do_ref, dq_ref, dk_ref, dv_ref, db_ref):
        c = pl.program_id(1)

        @pl.when(c == 0)
        def _():
            dk_ref[...] = jnp.zeros_like(dk_ref)
            dv_ref[...] = jnp.zeros_like(dv_ref)
            db_ref[...] = jnp.zeros_like(db_ref)

        start = pl.multiple_of(c * CHUNK, CHUNK)
        kb = k_ref[pl.ds(start, CA_BAND), :].astype(F32)
        vb = v_ref[pl.ds(start, CA_BAND), :].astype(F32)
        valid = _cattn_valid(c)
        _, vjp = jax.vjp(lambda q, k, v, b: _cattn_chunk(q, k, v, b, valid), q_ref[...], kb, vb, b_ref[...])
        dq, dk, dv, db = vjp(do_ref[...])
        dq_ref[...] = dq
        dk_ref[pl.ds(start, CA_BAND), :] += dk
        dv_ref[pl.ds(start, CA_BAND), :] += dv
        db_ref[...] += db

    sd = jax.ShapeDtypeStruct
    return pl.pallas_call(
        body, name="cattn_bwd", grid=(CA_H // HG, nc), in_specs=[q_spec, kv_spec, kv_spec, b_spec, q_spec],
        out_specs=[q_spec, kv_spec, kv_spec, b_spec],
        out_shape=[sd((S, D), F32), sd((S + CA_PAD, D), F32), sd((S + CA_PAD, D), F32),
                   sd((CA_H * CHUNK, CA_BAND), F32)],
        compiler_params=_cp("parallel", "arbitrary"),
    )(qn, kp, vp, bias, do)


SKEW_W = CA_BAND + CHUNK


def rel_bias_grad(dbias):
    padded = jnp.pad(dbias, ((0, 0), (0, 0), (CHUNK, 0)))
    flat = jnp.pad(padded.reshape(CA_H, CHUNK * SKEW_W), ((0, 0), (0, CHUNK)))
    skew = flat.reshape(CA_H, CHUNK, SKEW_W + 1)

    first_near = SKEW_W - CHUNK - MAX_REL

    def fn(t):
        colsum = jnp.sum(t, axis=1, keepdims=True)
        j = lax.broadcasted_iota(jnp.int32, colsum.shape, 2)
        far = jnp.sum(jnp.where(j < first_near, colsum, 0.0), axis=2, keepdims=True)
        return (colsum + jnp.where(j == first_near, far, 0.0),)

    (colsum,) = whole("relbias_sum", fn, [skew], [((CA_H, 1, SKEW_W + 1), F32)])
    near = colsum[:, 0, first_near:SKEW_W][:, ::-1]
    return jnp.concatenate([jnp.zeros((CA_H, CHUNK + 1), F32), near], axis=1)


def rel_bias_expand(rb):
    near = rb[:, CHUNK + 1:][:, ::-1]
    far = jnp.broadcast_to(rb[:, 2 * MAX_REL:], (CA_H, SKEW_W - CHUNK - MAX_REL))
    t = jnp.concatenate([far, near, jnp.zeros((CA_H, 1), rb.dtype)], axis=1)
    rows = jnp.tile(t, (1, CHUNK))[:, :CHUNK * SKEW_W].reshape(CA_H, CHUNK, SKEW_W)
    return rows[:, :, CHUNK:]


def loss_head(y, target):
    S = y.shape[0]
    tile = min(ROW_TILE, S)

    def body(y_ref, t_ref, dy_ref, acc_ref):
        i = pl.program_id(0)
        e = y_ref[...] - t_ref[...]
        dy_ref[...] = e * (1.0 / D)
        part = jnp.sum(e * e, axis=0, keepdims=True) * (0.5 / D)

        @pl.when(i == 0)
        def _():
            acc_ref[...] = part

        @pl.when(i > 0)
        def _():
            acc_ref[...] += part

    row = pl.BlockSpec((tile, D), lambda i: (i, 0))
    return pl.pallas_call(
        body, name="loss_head", grid=(S // tile,), in_specs=[row, row],
        out_specs=[row, pl.BlockSpec((1, D), lambda i: (0, 0))],
        out_shape=[jax.ShapeDtypeStruct((S, D), F32), jax.ShapeDtypeStruct((1, D), F32)],
        compiler_params=_cp("arbitrary"),
    )(y, target)


ANY = pl.BlockSpec(memory_space=pl.ANY)


HBM = pl.BlockSpec(memory_space=pltpu.HBM)
SEM = pl.BlockSpec(memory_space=pltpu.SEMAPHORE)
EFFECT = pltpu.SideEffectType.DATAFLOW_SIDE_EFFECTING


def _chip_copies(src_ref, land_ref, send_sems, recv_sems, mode):
    x, y, c = lax.axis_index("x"), lax.axis_index("y"), lax.axis_index("c")
    me = 2 * x + y

    def copy(j, s, d, dev):
        return pltpu.make_async_remote_copy(src_ref=s, dst_ref=d, send_sem=send_sems.at[j], recv_sem=recv_sems.at[j],
                                            device_id=dev, device_id_type=MESH)

    if mode == "sibling":
        cp = copy(0, src_ref, land_ref, (x, y, 1 - c))
        return [(cp, cp)]
    out = []
    for j, (px, py) in enumerate([(1 - x, y), (x, 1 - y), (1 - x, 1 - y)]):
        peer = 2 * px + py
        if mode == "scatter":
            pairs, dev = ((src_ref.at[peer], land_ref.at[j]), (src_ref.at[me], land_ref.at[j])), (px, py, c)
        else:
            half = land_ref.shape[1] // 2
            mine = pl.ds(pl.multiple_of(c * half, 16), half)
            theirs = pl.ds(pl.multiple_of((1 - c) * half, 16), half)
            if mode == "gather_half":
                pairs = ((src_ref.at[mine], land_ref.at[me, mine]), (src_ref.at[mine], land_ref.at[peer, mine]))
                dev = (px, py, c)
            else:
                pairs = ((land_ref.at[peer, mine],) * 2, (land_ref.at[peer, theirs],) * 2)
                dev = (x, y, 1 - c)
        out.append(tuple(copy(j, s, d, dev) for s, d in pairs))
    return out


def exchange_start(src, land, carry, name, mode):
    def body(src_ref, land_ref, carry_ref, send_sems, recv_sems, src_out, land_out, carry_out):
        for send, _ in _chip_copies(src_ref, land_ref, send_sems, recv_sems, mode):
            send.start()

    hbm = lambda a: pltpu.HBM(a.shape, a.dtype)
    n = 1 if mode == "sibling" else 3
    return pl.pallas_call(
        body, name=name,
        out_shape=(pltpu.SemaphoreType.DMA((n,)), pltpu.SemaphoreType.DMA((n,)), hbm(src), hbm(land), hbm(carry)),
        in_specs=(HBM, HBM, HBM), out_specs=(SEM, SEM, HBM, HBM, HBM),
        input_output_aliases={0: 2, 1: 3, 2: 4},
        compiler_params=pltpu.CompilerParams(has_side_effects=EFFECT),
    )(pltpu.with_memory_space_constraint(src, pltpu.HBM), pltpu.with_memory_space_constraint(land, pltpu.HBM),
      pltpu.with_memory_space_constraint(carry, pltpu.HBM))


def exchange_wait(src, land, send_sems, recv_sems, after, name, mode):
    def body(src_ref, land_ref, send_sems_ref, recv_sems_ref, after_ref, src_out, land_out):
        for send, recv in _chip_copies(src_ref, land_ref, send_sems_ref, recv_sems_ref, mode):
            send.wait_send()
            recv.wait_recv()

    hbm = lambda a: pltpu.HBM(a.shape, a.dtype)
    return pl.pallas_call(
        body, name=name, out_shape=(hbm(src), hbm(land)),
        in_specs=(HBM, HBM, SEM, SEM, ANY), out_specs=(HBM, HBM), input_output_aliases={0: 0, 1: 1},
        compiler_params=pltpu.CompilerParams(has_side_effects=EFFECT),
    )(src, land, send_sems, recv_sems, after)


def sibling_exchange(srcs, name):
    n = len(srcs)

    def body(*refs):
        src_refs, out_refs, send_sems, recv_sems = refs[:n], refs[n:2 * n], refs[2 * n], refs[2 * n + 1]
        x, y, c = lax.axis_index("x"), lax.axis_index("y"), lax.axis_index("c")
        copies = [pltpu.make_async_remote_copy(src_ref=s, dst_ref=o, send_sem=send_sems.at[k], recv_sem=recv_sems.at[k],
                                               device_id=(x, y, 1 - c), device_id_type=MESH)
                  for k, (s, o) in enumerate(zip(src_refs, out_refs))]
        for cp in copies:
            cp.start()
        for cp in copies:
            cp.wait()

    return pl.pallas_call(
        body, name=name, in_specs=[ANY] * n, out_specs=[ANY] * n,
        out_shape=[jax.ShapeDtypeStruct(s.shape, s.dtype) for s in srcs],
        scratch_shapes=[pltpu.SemaphoreType.DMA((n,)), pltpu.SemaphoreType.DMA((n,))],
    )(*srcs)


def chip_gather(src, name):
    def body(src_ref, out_ref, send_sems, recv_sems, local_sem):
        x, y, c = lax.axis_index("x"), lax.axis_index("y"), lax.axis_index("c")
        me = 2 * x + y
        local = pltpu.make_async_copy(src_ref, out_ref.at[me], local_sem)
        local.start()
        peers = [(1 - x, y), (x, 1 - y), (1 - x, 1 - y)]
        mk = lambda j, px, py, slab: pltpu.make_async_remote_copy(
            src_ref=src_ref, dst_ref=out_ref.at[slab], send_sem=send_sems.at[j], recv_sem=recv_sems.at[j],
            device_id=(px, py, c), device_id_type=MESH)
        sends = [mk(j, px, py, me) for j, (px, py) in enumerate(peers)]
        for cp in sends:
            cp.start()
        for j, (px, py) in enumerate(peers):
            mk(j, px, py, 2 * px + py).wait_recv()
        for cp in sends:
            cp.wait_send()
        local.wait()

    return pl.pallas_call(
        body, name=name, in_specs=[ANY], out_specs=ANY,
        out_shape=jax.ShapeDtypeStruct((N_CHIPS,) + tuple(src.shape), src.dtype),
        scratch_shapes=[pltpu.SemaphoreType.DMA((3,)), pltpu.SemaphoreType.DMA((3,)), pltpu.SemaphoreType.DMA],
    )(src)


def sum_own_slabs(src, chip, land, name, tile=512):
    _, R, C = src.shape
    n = land.shape[0]
    tile = _tile(R, tile, 16)

    def body(chip_ref, o_ref, t_ref, out_ref):
        acc = o_ref[...].astype(F32)
        for s in range(n):
            acc = acc + t_ref[s].astype(F32)
        out_ref[...] = acc

    return pl.pallas_call(
        body, name=name,
        grid_spec=pltpu.PrefetchScalarGridSpec(
            num_scalar_prefetch=1, grid=(R // tile,),
            in_specs=[pl.BlockSpec((None, tile, C), lambda i, chip_ref: (chip_ref[0], i, 0)),
                      pl.BlockSpec((n, tile, C), lambda i, chip_ref: (0, i, 0))],
            out_specs=pl.BlockSpec((tile, C), lambda i, chip_ref: (i, 0))),
        out_shape=jax.ShapeDtypeStruct((R, C), F32),
        compiler_params=_cp("parallel"),
    )(jnp.reshape(chip, (1,)).astype(jnp.int32), src, land)


def sum_slabs(t, name, tile=512):
    n, R, C = t.shape
    tile = _tile(R, tile, 16)

    def body(t_ref, o_ref):
        acc = t_ref[0].astype(F32)
        for s in range(1, n):
            acc = acc + t_ref[s].astype(F32)
        o_ref[...] = acc

    return pl.pallas_call(
        body, name=name, grid=(R // tile,), in_specs=[pl.BlockSpec((n, tile, C), lambda i: (0, i, 0))],
        out_specs=pl.BlockSpec((tile, C), lambda i: (i, 0)), out_shape=jax.ShapeDtypeStruct((R, C), F32),
        compiler_params=_cp("parallel"),
    )(t)


def _pad_rows(a, mult=16):
    r = (-a.shape[0]) % mult
    return jnp.pad(a, ((0, r), (0, 0))) if r else a


BIG = ["ab_w_in", "c_w_qkv", "xa_w_kv", "f_w_gate", "f_w_up", "ab_w_out", "c_w_out", "xa_w_q", "xa_w_out",
       "f_w_down", "s5_w_glu"]


GROUPS = ("B", "A")
GROUP_ROW_MULT = 64


def group_spec(layer, grp):
    i = layer // 2
    if grp == "A":
        return [("xa_w_kv", layer, True, 512), ("xa_w_q", layer, False, 256), ("xa_w_out", layer, False, 256),
                ("f_w_gate", layer, True, 704), ("f_w_up", layer, True, 704), ("f_w_down", layer, False, 704)]
    if layer % 2 == 0:
        return [("ab_w_in", i, True, 642), ("ab_w_out", i, False, 256), ("s5_w_glu", i, False, 64)]
    return [("c_w_qkv", i, True, 768), ("c_w_out", i, False, 256)]


def _seg_rows(rows):
    return rows + ((-rows) % 16)


def _f32_rows(a):
    bits = lax.bitcast_convert_type(a.reshape(-1), BF16).reshape(-1)
    return jnp.pad(bits, (0, 16 * D - bits.shape[0])).reshape(16, D)


def pack_group_shards(p, layer, grp):
    segs = []
    for name, idx, transposed, rows in group_spec(layer, grp):
        w = p[name][idx]
        if transposed:
            w = w.T
        segs.append(_pad_rows(w.astype(BF16).reshape(-1, D)))
    if grp == "B":
        small = p["gdn_conv_w"] if layer % 2 == 0 else p["c_norm_g"]
        segs.append(_f32_rows(small[layer // 2]))
    return _pad_rows(jnp.concatenate(segs, axis=0), GROUP_ROW_MULT)


def unpack_group_gathered(g, layer, grp):
    out, off = {}, 0
    for name, idx, transposed, rows in group_spec(layer, grp):
        seg = g[:, off:off + rows]
        if name == "s5_w_glu":
            out[name] = seg.reshape(N_CHIPS * 128, 512)
        else:
            out[name] = seg.reshape(N_CHIPS * rows, D)
        off += _seg_rows(rows)
    if grp == "B":
        n = 4 * 384 if layer % 2 == 0 else 256
        bits = g[:, off:off + 16].reshape(N_CHIPS, -1)[:, :2 * n].reshape(N_CHIPS, n, 2)
        small = lax.bitcast_convert_type(bits, F32)
        if layer % 2 == 0:
            out["gdn_conv_w"] = jnp.swapaxes(small.reshape(N_CHIPS, 4, 384), 0, 1).reshape(4, 1536)
        else:
            out["c_norm_g"] = small.reshape(D)
    return out


def pack_group_grads(gr, layer, grp):
    segs = []
    for name, idx, transposed, rows in group_spec(layer, grp):
        w = gr[name].astype(BF16)
        seg = w.reshape(N_CHIPS, rows, D)
        r = (-rows) % 16
        if r:
            seg = jnp.pad(seg, ((0, 0), (0, r), (0, 0)))
        segs.append(seg)
    out = jnp.concatenate(segs, axis=1)
    return jnp.pad(out, ((0, 0), (0, (-out.shape[1]) % GROUP_ROW_MULT), (0, 0)))


def unpack_group_reduced(g, layer, grp):
    out, off = {}, 0
    for name, idx, transposed, rows in group_spec(layer, grp):
        seg = g[off:off + rows]
        if name == "s5_w_glu":
            out[name] = seg.reshape(128, 512)
        else:
            out[name] = seg.T if (transposed and name not in ADAM_TRANSPOSED) else seg
        off += _seg_rows(rows)
    return out


ADAM_TRANSPOSED = ("f_w_gate", "f_w_up")


SMALL = ["ab_norm_g", "s5_a_re", "s5_a_im", "s5_log_dt", "s5_b_re", "s5_b_im", "s5_c_re", "s5_c_im", "s5_d",
         "s5_b_glu", "gdn_conv_w", "gdn_a_log", "gdn_dt_bias", "gdn_out_norm_g", "c_norm_g", "c_q_norm_g",
         "c_k_norm_g", "c_rel_bias", "mem_norm_g", "xa_norm_g", "xa_q_norm_g", "xa_k_norm_g", "f_norm_g"]


def _lane_rows(a):
    flat = a.reshape(-1).astype(F32)
    return jnp.pad(flat, (0, (-flat.shape[0]) % 1024)).reshape(-1, 128)


def pack_small(d, extra=None):
    parts = [_lane_rows(d[n]) for n in SMALL]
    if extra is not None:
        parts.append(_lane_rows(extra))
    rows = jnp.concatenate(parts, axis=0)
    return jnp.pad(rows, ((0, (-rows.shape[0]) % 128), (0, 0)))


def unpack_small(rows, shapes):
    out, off = {}, 0
    for n in SMALL:
        sz = int(np.prod(shapes[n]))
        k = 8 * -(-sz // 1024)
        out[n] = rows[off:off + k].reshape(-1)[:sz].reshape(shapes[n])
        off += k
    return out, rows[off:]


def _s5_blockdiag_b(b):
    bt = jnp.swapaxes(b, 1, 2).reshape(S5_GB, 8, S5_C, S5_P)
    eye = jnp.eye(8, dtype=b.dtype)
    return jnp.einsum("bgcp,gh->bgchp", bt, eye).reshape(S5_GB, 8 * S5_C, 8 * S5_P)


def _s5_blockdiag_c(c):
    ct = jnp.swapaxes(c, 1, 2).reshape(S5_GB, 8, S5_P, S5_C)
    eye = jnp.eye(8, dtype=c.dtype)
    return jnp.einsum("bgpc,gh->bgphc", ct, eye).reshape(S5_GB, 8 * S5_P, 8 * S5_C)


def _s5_diag_b(db):
    t = db.reshape(S5_GB, 8, S5_C, 8, S5_P)
    t = jnp.transpose(t, (0, 2, 4, 1, 3)).reshape(S5_GB, S5_C, S5_P, 64)
    d = t[..., ::9]
    return jnp.transpose(d, (0, 3, 2, 1)).reshape(S5_G, S5_P, S5_C)


def _s5_diag_c(dc):
    t = dc.reshape(S5_GB, 8, S5_P, 8, S5_C)
    t = jnp.transpose(t, (0, 2, 4, 1, 3)).reshape(S5_GB, S5_P, S5_C, 64)
    d = t[..., ::9]
    return jnp.transpose(d, (0, 3, 2, 1)).reshape(S5_G, S5_C, S5_P)


def local_step(x, mem, target, p, wsrc, gsink, wpre=lambda layer, grp, carry: carry):
    S = x.shape[0]
    row2 = lambda a: a.reshape(1, -1)
    saved = []
    (mem_n,) = row_fwd("mem_norm", f_norm, [mem], [row2(p["mem_norm_g"])], [(D, BF16)])
    gs = {}

    for layer in range(DEPTH):
        i = layer // 2
        w = wsrc(layer, "B", x)
        sv = {"x0": x, "wB": w}
        if layer % 2 == 0:
            (h,) = row_fwd("norm", f_norm, [x], [row2(p["ab_norm_g"][i])], [(D, BF16)])
            w_in = w["ab_w_in"]
            pm = matmul("nt", h, w_in[:2560], "proj_main")
            abt = matmul("nt", w_in[2560:], h, "proj_ab")
            s5p = dict(
                are=p["s5_a_re"][i].reshape(1, -1), aim=p["s5_a_im"][i].reshape(1, -1),
                ldt=jnp.broadcast_to(p["s5_log_dt"][i][:, None], (S5_G, S5_P)).reshape(1, -1),
                b_re=_s5_blockdiag_b(p["s5_b_re"][i]), b_im=_s5_blockdiag_b(p["s5_b_im"][i]),
                c_re=_s5_blockdiag_c(p["s5_c_re"][i]), c_im=_s5_blockdiag_c(p["s5_c_im"][i]),
                dv=p["s5_d"][i].reshape(1, -1))
            y5, st5 = s5_fwd(pm, **s5p)
            (a_out,) = row_fwd("glu", f_glu, [y5], [w["s5_w_glu"], row2(p["s5_b_glu"][i])], [(S5_W, F32)])
            conv_w = w["gdn_conv_w"]
            qkvc = conv_fwd(pm, conv_w)
            alog = p["gdn_a_log"][i].reshape(GDN_H, 1, 1)
            dtb = p["gdn_dt_bias"][i].reshape(GDN_H, 1, 1)
            og = row2(p["gdn_out_norm_g"][i])
            b_out, stg, inv = gdn_fwd(qkvc, pm, abt, alog, dtb, og)
            cat = wpre(layer, "A", jnp.concatenate([a_out, b_out], axis=1))
            x, hq = matmul("nn", cat, w["ab_w_out"], "mix_out", add=x, norm_out=row2(p["xa_norm_g"][layer]))
            sv.update(h=h, pm=pm, s5p=s5p, y5=y5, st5=st5, qkvc=qkvc, abt=abt, alog=alog, dtb=dtb, og=og,
                      stg=stg, inv=inv, cat=cat, conv_w=conv_w)
        else:
            (h,) = row_fwd("norm", f_norm, [x], [row2(w["c_norm_g"])], [(D, BF16)])
            qkv = matmul("nt", h, w["c_w_qkv"], "proj_qkv")
            qg = jnp.tile(row2(p["c_q_norm_g"][i]), (1, CA_H))
            kg = jnp.tile(row2(p["c_k_norm_g"][i]), (1, CA_H))
            (qn,) = row_fwd("headnorm_q", f_headnorm, [(qkv, D, 0)], [qg], [(D, F32)])
            kp, vp = kv_prep(qkv, kg)
            bias = rel_bias_expand(p["c_rel_bias"][i]).reshape(CA_H * CHUNK, CA_BAND)
            o = wpre(layer, "A", cattn_fwd(qn, kp, vp, bias))
            x, hq = matmul("nn", o, w["c_w_out"], "mix_out", add=x, norm_out=row2(p["xa_norm_g"][layer]))
            sv.update(h=h, qkv=qkv, qg=qg, kg=kg, qn=qn, kp=kp, vp=vp, bias=bias, o=o)
        sv["x1"] = x
        w = wsrc(layer, "A", x)
        sv["wA"] = w
        qx = matmul("nn", hq, w["xa_w_q"], "xa_q")
        kv = matmul("nt", mem_n, w["xa_w_kv"], "xa_kv")
        xqg, xkg = row2(p["xa_q_norm_g"][layer]), row2(p["xa_k_norm_g"][layer])
        (ox,) = row_fwd("xattn", f_xattn, [qx], [kv, xqg, xkg], [(D, BF16)], tile=2 * ROW_TILE)
        x, hf = matmul("nn", ox, w["xa_w_out"], "xa_out", add=x, norm_out=row2(p["f_norm_g"][layer]))
        sv.update(hq=hq, qx=qx, kv=kv, ox=ox)
        sv["x2"] = x
        gate, up, act = ffn_in(hf, w["f_w_gate"], w["f_w_up"])
        if layer + 1 < DEPTH:
            act = wpre(layer + 1, "B", act)
        x = matmul("nn", act, w["f_w_down"], "ffn_down", add=x)
        sv.update(hf=hf, gate=gate, up=up, act=act)
        saved.append(sv)

    dx, loss_vec = loss_head(x, target)

    dmem_n = None
    for layer in reversed(range(DEPTH)):
        i = layer // 2
        sv = saved[layer]
        w, gw = sv["wA"], {}
        dgate, dup = ffn_dact(dx, w["f_w_down"], sv["gate"], sv["up"])
        gw["f_w_down"] = matmul("tn", sv["act"], dx, "ffn_dwd", out_dtype=BF16)
        gw["f_w_gate"] = matmul("tn", dgate, sv["hf"], "ffn_dwg", out_dtype=BF16)
        gw["f_w_up"] = matmul("tn", dup, sv["hf"], "ffn_dwu", out_dtype=BF16)
        dh = matmul("nn", dgate, w["f_w_gate"], "ffn_dhg")
        dx, dg = matmul("nn", dup, w["f_w_up"], "ffn_dhu", add=dh,
                        norm_bwd=(sv["x2"], row2(p["f_norm_g"][layer]), dx))
        gs.setdefault("f_norm_g", [None] * DEPTH)[layer] = dg[0]
        do = matmul("nt", dx, w["xa_w_out"], "xa_do")
        gw["xa_w_out"] = matmul("tn", sv["ox"], dx, "xa_dwo", out_dtype=BF16)
        xqg, xkg = row2(p["xa_q_norm_g"][layer]), row2(p["xa_k_norm_g"][layer])
        dqx, dkv, dqg, dkg = row_bwd("xattn_bwd", f_xattn, [sv["qx"]], [sv["kv"], xqg, xkg], [do],
                                     [0], [0, 1, 2], tile=2 * ROW_TILE)
        gs.setdefault("xa_q_norm_g", [None] * DEPTH)[layer] = dqg[0]
        gs.setdefault("xa_k_norm_g", [None] * DEPTH)[layer] = dkg[0]
        gw["xa_w_q"] = matmul("tn", sv["hq"], dqx, "xa_dwq", out_dtype=BF16)
        gw["xa_w_kv"] = matmul("tn", dkv, mem_n, "xa_dwkv", out_dtype=BF16)
        dx = gsink(layer, "A", gw, dx)
        dmem_n = matmul("nn", dkv, w["xa_w_kv"], "xa_dmem", add=dmem_n)
        dx, dg = matmul("nt", dqx, w["xa_w_q"], "xa_dhq", norm_bwd=(sv["x1"], row2(p["xa_norm_g"][layer]), dx))
        gs.setdefault("xa_norm_g", [None] * DEPTH)[layer] = dg[0]
        w, gw = sv["wB"], {}
        if layer % 2 == 0:
            dcat = matmul("nt", dx, w["ab_w_out"], "mix_dcat")
            gw["ab_w_out"] = matmul("tn", sv["cat"], dx, "mix_dwo", out_dtype=BF16)
            dy5, dwglu, dbglu = row_bwd("glu_bwd", f_glu, [sv["y5"]], [w["s5_w_glu"], row2(p["s5_b_glu"][i])],
                                        [(dcat, S5_W, 0)], [0], [0, 1])
            gw["s5_w_glu"] = dwglu
            gs.setdefault("s5_b_glu", [None] * 2)[i] = dbglu[0]
            s5p = sv["s5p"]
            du, dare, daim, dldt, dbre, dbim, dcre, dcim, ddv = s5_bwd(sv["pm"], sv["st5"], dy5, **s5p)
            (dldt_g,) = whole("s5_dt_sum", lambda t: (jnp.sum(t, axis=1, keepdims=True),),
                              [dldt.reshape(S5_G, S5_P)], [((S5_G, 1), F32)])
            for nme, val in (("s5_a_re", dare.reshape(S5_G, S5_P)), ("s5_a_im", daim.reshape(S5_G, S5_P)),
                             ("s5_log_dt", dldt_g[:, 0]), ("s5_b_re", _s5_diag_b(dbre)),
                             ("s5_b_im", _s5_diag_b(dbim)), ("s5_c_re", _s5_diag_c(dcre)),
                             ("s5_c_im", _s5_diag_c(dcim)), ("s5_d", ddv.reshape(S5_G, S5_C))):
                gs.setdefault(nme, [None] * 2)[i] = val
            dqkvc, dgate, dal, dbl, dalog, ddtb, dog = gdn_bwd(
                sv["qkvc"], sv["pm"], sv["abt"], sv["alog"], sv["dtb"], sv["og"], sv["stg"], sv["inv"],
                dcat[:, S5_W:])
            (dog_s,) = whole("gdn_og_sum", lambda t: (jnp.sum(t, axis=0, keepdims=True),),
                             [dog.reshape(GDN_H, GDN_D)], [((1, GDN_D), F32)])
            gs.setdefault("gdn_out_norm_g", [None] * 2)[i] = dog_s[0]
            gs.setdefault("gdn_a_log", [None] * 2)[i] = dalog.reshape(GDN_H)
            gs.setdefault("gdn_dt_bias", [None] * 2)[i] = ddtb.reshape(GDN_H)
            dqkv, dconv = conv_bwd(sv["pm"], sv["conv_w"], dqkvc)
            gs.setdefault("gdn_conv_w", [None] * 2)[i] = dconv
            dpm = jnp.concatenate([du, dqkv, dgate], axis=1).astype(BF16)
            dabt = jnp.concatenate([dal, dbl], axis=0)
            dw_main = matmul("tn", dpm, sv["h"], "proj_dw", out_dtype=BF16)
            dw_ab = matmul("nn", dabt, sv["h"], "proj_ab_dw", out_dtype=BF16)
            gw["ab_w_in"] = jnp.concatenate([dw_main, dw_ab], axis=0)
            dx = gsink(layer, "B", gw, dx)
            w_in = w["ab_w_in"]
            dh = matmul("nn", dpm, w_in[:2560], "proj_dh")
            dx, dg = matmul("tn", dabt, w_in[2560:], "proj_ab_dh", add=dh,
                            norm_bwd=(sv["x0"], row2(p["ab_norm_g"][i]), dx))
            gs.setdefault("ab_norm_g", [None] * 2)[i] = dg[0]
        else:
            do = matmul("nt", dx, w["c_w_out"], "mix_dcat")
            gw["c_w_out"] = matmul("tn", sv["o"], dx, "mix_dwo", out_dtype=BF16)
            dqn, dkp, dvp, dbias = cattn_bwd(sv["qn"], sv["kp"], sv["vp"], sv["bias"], do)
            gs.setdefault("c_rel_bias", [None] * 2)[i] = rel_bias_grad(dbias.reshape(CA_H, CHUNK, CA_BAND))
            dq, dqg = row_bwd("headnorm_bwd", f_headnorm, [(sv["qkv"], D, 0)], [sv["qg"]], [dqn], [0], [0])
            dk, dkg = row_bwd("headnorm_bwd", f_headnorm, [(sv["qkv"], D, 1)], [sv["kg"]],
                              [(dkp, D, 0, CA_PAD)], [0], [0])
            head_sum = lambda t: (jnp.sum(t, axis=0, keepdims=True),)
            (dqg,) = whole("headgain_sum", head_sum, [dqg.reshape(CA_H, CA_D)], [((1, CA_D), F32)])
            (dkg,) = whole("headgain_sum", head_sum, [dkg.reshape(CA_H, CA_D)], [((1, CA_D), F32)])
            gs.setdefault("c_q_norm_g", [None] * 2)[i] = dqg[0]
            gs.setdefault("c_k_norm_g", [None] * 2)[i] = dkg[0]
            dqkv = jnp.concatenate([dq, dk, dvp[CA_PAD:]], axis=1).astype(BF16)
            gw["c_w_qkv"] = matmul("tn", dqkv, sv["h"], "proj_qkv_dw", out_dtype=BF16)
            dx = gsink(layer, "B", gw, dx)
            dx, dg = matmul("nn", dqkv, w["c_w_qkv"], "proj_qkv_dh",
                            norm_bwd=(sv["x0"], row2(w["c_norm_g"]), dx))
            gs.setdefault("c_norm_g", [None] * 2)[i] = dg[0]
    (dmg,) = row_bwd("mem_norm_bwd", f_norm, [mem], [row2(p["mem_norm_g"])], [dmem_n], [], [0])
    small = {n: jnp.stack(v) for n, v in gs.items()}
    small["mem_norm_g"] = dmg[0]
    return loss_vec, dx, small


ADAM_BLOCK_BYTES = 3 << 19


def adam(w, g, m, v, name):
    shape = w.shape
    if w.ndim == 3:
        d0, n, d2 = shape
        fits = [t for t in range(8, n + 1, 8) if n % t == 0 and d0 * t * d2 * 4 <= ADAM_BLOCK_BYTES]
        return tuple(row_fwd(name, f_adam, [w, g, m, v], [], [((d0, d2), F32)] * 3, tile=max(fits)))
    cols = shape[-1]
    w2, g2, m2, v2 = (t.reshape(-1, cols) for t in (w, g, m, v))
    rows = w2.shape[0]
    tile = rows if rows <= 512 else _tile(rows, 512, 8)
    outs = row_fwd(name, f_adam, [w2, g2, m2, v2], [], [(cols, F32)] * 3, tile=tile)
    return tuple(o.reshape(shape) for o in outs)


WEIGHTS = ['ab_norm_g', 'ab_w_in', 'ab_w_out', 's5_a_re', 's5_a_im', 's5_log_dt', 's5_b_re', 's5_b_im', 's5_c_re',
           's5_c_im', 's5_d', 's5_w_glu', 's5_b_glu', 'gdn_conv_w', 'gdn_a_log', 'gdn_dt_bias', 'gdn_out_norm_g',
           'c_norm_g', 'c_w_qkv', 'c_w_out', 'c_q_norm_g', 'c_k_norm_g', 'c_rel_bias', 'mem_norm_g', 'xa_norm_g',
           'xa_w_q', 'xa_w_kv', 'xa_w_out', 'xa_q_norm_g', 'xa_k_norm_g', 'f_norm_g', 'f_w_gate', 'f_w_up',
           'f_w_down']
SHARDED_SMALL = {"gdn_conv_w": (2, 384), "c_norm_g": (1, 256)}


def kernel(x, mem, ab_norm_g, ab_w_in, ab_w_out, s5_a_re, s5_a_im, s5_log_dt, s5_b_re, s5_b_im, s5_c_re, s5_c_im, s5_d, s5_w_glu, s5_b_glu, gdn_conv_w, gdn_a_log, gdn_dt_bias, gdn_out_norm_g, c_norm_g, c_w_qkv, c_w_out, c_q_norm_g, c_k_norm_g, c_rel_bias, mem_norm_g, xa_norm_g, xa_w_q, xa_w_kv, xa_w_out, xa_q_norm_g, xa_k_norm_g, f_norm_g, f_w_gate, f_w_up, f_w_down, loss_target, m_ab_norm_g, m_ab_w_in, m_ab_w_out, m_s5_a_re, m_s5_a_im, m_s5_log_dt, m_s5_b_re, m_s5_b_im, m_s5_c_re, m_s5_c_im, m_s5_d, m_s5_w_glu, m_s5_b_glu, m_gdn_conv_w, m_gdn_a_log, m_gdn_dt_bias, m_gdn_out_norm_g, m_c_norm_g, m_c_w_qkv, m_c_w_out, m_c_q_norm_g, m_c_k_norm_g, m_c_rel_bias, m_mem_norm_g, m_xa_norm_g, m_xa_w_q, m_xa_w_kv, m_xa_w_out, m_xa_q_norm_g, m_xa_k_norm_g, m_f_norm_g, m_f_w_gate, m_f_w_up, m_f_w_down, v_ab_norm_g, v_ab_w_in, v_ab_w_out, v_s5_a_re, v_s5_a_im, v_s5_log_dt, v_s5_b_re, v_s5_b_im, v_s5_c_re, v_s5_c_im, v_s5_d, v_s5_w_glu, v_s5_b_glu, v_gdn_conv_w, v_gdn_a_log, v_gdn_dt_bias, v_gdn_out_norm_g, v_c_norm_g, v_c_w_qkv, v_c_w_out, v_c_q_norm_g, v_c_k_norm_g, v_c_rel_bias, v_mem_norm_g, v_xa_norm_g, v_xa_w_q, v_xa_w_kv, v_xa_w_out, v_xa_q_norm_g, v_xa_k_norm_g, v_f_norm_g, v_f_w_gate, v_f_w_up, v_f_w_down):
    args = locals()
    p = {n: args[n] for n in WEIGHTS}
    m = {n: args["m_" + n] for n in WEIGHTS}
    v = {n: args["v_" + n] for n in WEIGHTS}
    chip = 2 * lax.axis_index("x") + lax.axis_index("y")

    carry = x[0]
    gathers = {}
    for layer in range(DEPTH):
        for grp in GROUPS:
            src = pack_group_shards(p, layer, grp)
            land = lax.dynamic_update_slice(lax.empty((N_CHIPS,) + src.shape, BF16), src[None], (chip, 0, 0))
            send_sems, recv_sems, src, land, carry = exchange_start(
                src, land, carry, f"gather_start_{layer}{grp}", mode="gather_half")
            gathers[layer, grp] = (src, land, send_sems, recv_sems)
    forwards = {}

    def wpre(layer, grp, carry):
        if (layer, grp) not in forwards:
            src, land, send_sems, recv_sems = gathers[layer, grp]
            src, land = exchange_wait(src, land, send_sems, recv_sems, carry, f"gather_wait_{layer}{grp}",
                                      mode="gather_half")
            send_sems, recv_sems, src, land, carry = exchange_start(
                src, land, carry, f"forward_start_{layer}{grp}", mode="forward_half")
            forwards[layer, grp] = (src, land, send_sems, recv_sems)
        return carry

    def wsrc(layer, grp, after):
        wpre(layer, grp, after)
        src, land, send_sems, recv_sems = forwards[layer, grp]
        _, land = exchange_wait(src, land, send_sems, recv_sems, after, f"forward_wait_{layer}{grp}",
                                mode="forward_half")
        return unpack_group_gathered(land, layer, grp)

    scatters, siblings = [], []
    LAG = 2

    def finish(carry):
        layer, grp, src, land, send_sems, recv_sems = scatters[len(siblings)]
        src, land = exchange_wait(src, land, send_sems, recv_sems, carry, f"scatter_wait_{layer}{grp}", mode="scatter")
        part = sum_own_slabs(src, chip, land, "sum_chips")
        send_sems, recv_sems, part, other, carry = exchange_start(
            part, lax.empty(part.shape, F32), carry, f"sibling_start_{layer}{grp}", mode="sibling")
        siblings.append((layer, grp, part, other, send_sems, recv_sems))
        return carry

    def gsink(layer, grp, gw, carry):
        src = pack_group_grads(gw, layer, grp)
        land = lax.empty((3,) + src.shape[1:], BF16)
        send_sems, recv_sems, src, land, carry = exchange_start(
            src, land, carry, f"scatter_start_{layer}{grp}", mode="scatter")
        scatters.append((layer, grp, src, land, send_sems, recv_sems))
        if len(scatters) > LAG:
            carry = finish(carry)
        return carry

    carry = wpre(0, "B", carry)
    loss_vec, dx, g_small = local_step(carry, mem[0], loss_target[0], p, wsrc, gsink, wpre)

    full_shapes = {n: ((2, 4, 1536) if n == "gdn_conv_w" else (2, D) if n == "c_norm_g" else p[n].shape)
                   for n in SMALL}
    small_mine = pack_small(g_small, extra=loss_vec)
    (small_other,) = sibling_exchange([small_mine], "sibling_small")
    (small_chip,) = row_fwd("sum_small_cores", lambda a, b: (a + b,), [small_mine, small_other], [], [(128, F32)],
                            tile=128)
    small_sum = sum_slabs(chip_gather(small_chip, "gather_small"), "sum_small")
    g_s, rest = unpack_small(small_sum, full_shapes)
    (loss11,) = whole("loss_sum", lambda t: (jnp.sum(jnp.sum(t, axis=1, keepdims=True), axis=0, keepdims=True),),
                      [rest[:D // 128]], [((1, 1), F32)])
    for n, (axis, width) in SHARDED_SMALL.items():
        g_s[n] = lax.dynamic_slice_in_dim(g_s[n], chip * width, width, axis=axis)
    delta, new_m, new_v = {}, {}, {}
    for n in SMALL:
        shape = p[n].shape
        two = (1, shape[0]) if len(shape) == 1 else (int(np.prod(shape[:-1])), shape[-1])
        outs = whole("adam_" + n, f_adam, [t.reshape(two) for t in (p[n], g_s[n], m[n], v[n])], [(two, F32)] * 3)
        delta[n], new_m[n], new_v[n] = (o.reshape(shape) for o in outs)

    while len(siblings) < len(scatters):
        dx = finish(dx)

    per_layer = {}
    for layer, grp, part, other, send_sems, recv_sems in siblings:
        part, other = exchange_wait(part, other, send_sems, recv_sems, dx, f"sibling_wait_{layer}{grp}",
                                    mode="sibling")
        (total,) = row_fwd("sum_cores", lambda a, b: (a + b,), [part, other], [], [(D, F32)],
                           tile=_tile(part.shape[0], 512, 16))
        for name, g in unpack_group_reduced(total, layer, grp).items():
            per_layer.setdefault(name, {})[layer] = g
    grads = {name: jnp.stack([d[k] for k in sorted(d)]) for name, d in per_layer.items()}
    grads.update(g_s)
    for name in BIG:
        if name in ADAM_TRANSPOSED:
            t = lambda a: jnp.swapaxes(a, 1, 2)
            outs = adam(t(p[name]), grads[name], t(m[name]), t(v[name]), "adam_" + name)
            delta[name], new_m[name], new_v[name] = (t(o) for o in outs)
            grads[name] = t(grads[name])
        else:
            delta[name], new_m[name], new_v[name] = adam(p[name], grads[name], m[name], v[name], "adam_" + name)

    return (loss11[0, 0], dx[None], *[grads[n] for n in WEIGHTS], *[delta[n] for n in WEIGHTS],
            *[new_m[n] for n in WEIGHTS], *[new_v[n] for n in WEIGHTS])
```

```python
import numpy as np
import jax
import jax.numpy as jnp
from jax import lax
from jax.experimental import pallas as pl
from jax.experimental.pallas import tpu as pltpu

F32 = jnp.float32
BF16 = jnp.bfloat16
MESH = pl.DeviceIdType.MESH

D = 1024
CHUNK = 64
EPS = 1e-6
S5_W = 512
S5_G = 32
S5_C = 16
S5_P = 64
S5_GB = 4
GDN_H = 4
GDN_D = 128
CA_H = 16
CA_D = 64
CA_LEFT = 8
CA_BAND = (CA_LEFT + 1) * CHUNK
CA_PAD = CA_LEFT * CHUNK
MAX_REL = 128
XA_H = 4
XA_D = 256
DEPTH = 4
N_CHIPS = 4
LR, B1, B2, AEPS, WD, STEP = 0.001, 0.9, 0.999, 1e-08, 0.01, 10

VMEM_LIMIT = 56 * 1024 * 1024
ROW_TILE = 256


def _cp(*sem):
    return pltpu.CompilerParams(dimension_semantics=sem, vmem_limit_bytes=VMEM_LIMIT)


def _dg(a, b, ca, cb):
    return lax.dot_general(a.astype(BF16), b.astype(BF16), (((ca,), (cb,)), ((), ())),
                           preferred_element_type=F32)


@jax.custom_vjp
def mm(a, b):
    return _dg(a, b, 1, 0)


@jax.custom_vjp
def mm_nt(a, b):
    return _dg(a, b, 1, 1)


@jax.custom_vjp
def mm_tn(a, b):
    return _dg(a, b, 0, 0)


mm.defvjp(lambda a, b: (mm(a, b), (a, b)), lambda r, g: (mm_nt(g, r[1]), mm_tn(r[0], g)))
mm_nt.defvjp(lambda a, b: (mm_nt(a, b), (a, b)), lambda r, g: (mm(g, r[1]), mm_tn(g, r[0])))
mm_tn.defvjp(lambda a, b: (mm_tn(a, b), (a, b)), lambda r, g: (mm_nt(r[1], g), mm(r[0], g)))


def _split(a):
    hi = a.astype(BF16)
    return hi, (a - hi.astype(F32)).astype(BF16)


def _tri_mm(v, upper):
    T = v.shape[0]
    r = lax.broadcasted_iota(jnp.int32, (T, T), 0)
    c = lax.broadcasted_iota(jnp.int32, (T, T), 1)
    m = ((c >= r) if upper else (r >= c)).astype(BF16)
    hi, lo = _split(v)
    d = lambda u: lax.dot_general(m, u, (((1,), (0,)), ((), ())), preferred_element_type=F32)
    return d(hi) + d(lo)


@jax.custom_vjp
def cumsum_rows(v):
    return _tri_mm(v, False)


cumsum_rows.defvjp(lambda v: (_tri_mm(v, False), None), lambda _, g: (_tri_mm(g, True),))


def _rms(x, g):
    return x * lax.rsqrt(jnp.mean(x * x, axis=-1, keepdims=True) + EPS) * g


def _softmax(s):
    e = jnp.exp(s - lax.stop_gradient(jnp.max(s, axis=-1, keepdims=True)))
    return e / jnp.sum(e, axis=-1, keepdims=True)


def _softplus(x):
    return jnp.maximum(x, 0.0) + jnp.log(1.0 + jnp.exp(-jnp.abs(x)))


def _tile(n, cap, align):
    if n <= cap:
        return n
    best = None
    for d in range(align, cap + 1, align):
        if n % d == 0:
            best = d
    assert best is not None, (n, cap, align)
    return best


def matmul(mode, a, b, name, out_dtype=F32, add=None, norm_out=None, norm_bwd=None):
    if mode == "nn":
        (M, K), (K2, N) = a.shape, b.shape
    elif mode == "nt":
        (M, K), (N, K2) = a.shape, b.shape
    else:
        (K, M), (K2, N) = a.shape, b.shape
    assert K == K2, (mode, a.shape, b.shape)
    if mode != "tn" and norm_out is None and norm_bwd is None:
        tm, tn, tk = _tile(M, 1024, 128), _tile(N, 1024, 128), _tile(K, 2048, 128)
    else:
        tm, tn, tk = _tile(M, 512, 128), _tile(N, 1536, 128), _tile(K, 2048, 128)
    nk = K // tk
    if mode == "nn":
        a_spec = pl.BlockSpec((tm, tk), lambda i, j, k: (i, k))
        b_spec = pl.BlockSpec((tk, tn), lambda i, j, k: (k, j))
        dn = (((1,), (0,)), ((), ()))
    elif mode == "nt":
        a_spec = pl.BlockSpec((tm, tk), lambda i, j, k: (i, k))
        b_spec = pl.BlockSpec((tn, tk), lambda i, j, k: (j, k))
        dn = (((1,), (1,)), ((), ()))
    else:
        a_spec = pl.BlockSpec((tk, tm), lambda i, j, k: (k, i))
        b_spec = pl.BlockSpec((tk, tn), lambda i, j, k: (k, j))
        dn = (((0,), (0,)), ((), ()))
    o_spec = pl.BlockSpec((tm, tn), lambda i, j, k: (i, j))
    has_add = add is not None
    g_spec = pl.BlockSpec((1, tn), lambda i, j, k: (0, j))
    extra, extra_specs, out_shapes, out_specs = [], [], [jax.ShapeDtypeStruct((M, N), out_dtype)], [o_spec]
    sem = ("parallel", "parallel", "arbitrary")
    if norm_out is not None:
        assert tn == N
        extra, extra_specs = [norm_out], [g_spec]
        out_shapes.append(jax.ShapeDtypeStruct((M, N), BF16))
        out_specs.append(o_spec)
    if norm_bwd is not None:
        assert tn == N
        x_in, g_in, res_in = norm_bwd
        extra, extra_specs = [x_in, g_in, res_in], [o_spec, g_spec, o_spec]
        out_shapes.append(jax.ShapeDtypeStruct((1, N), F32))
        out_specs.append(g_spec)
        sem = ("arbitrary", "arbitrary", "arbitrary")
    n_in = 2 + int(has_add) + len(extra)
    n_out = len(out_shapes)

    def body(*refs):
        a_ref, b_ref = refs[0], refs[1]
        add_ref = refs[2] if has_add else None
        extra_refs = refs[2 + int(has_add):n_in]
        o_ref = refs[n_in]
        acc_ref = refs[-1]
        i = pl.program_id(0)
        p = lax.dot_general(a_ref[...].astype(BF16), b_ref[...].astype(BF16), dn,
                            preferred_element_type=F32)

        def finish(total):
            if has_add:
                total = total + add_ref[...]
            if norm_out is not None:
                o_ref[...] = total.astype(o_ref.dtype)
                refs[n_in + 1][...] = _rms(total, extra_refs[0][...]).astype(BF16)
            elif norm_bwd is not None:
                x_ref, g_ref, res_ref = extra_refs
                _, vjp = jax.vjp(f_norm_res, x_ref[...], g_ref[...])
                dx, dg = vjp((total, res_ref[...]))
                o_ref[...] = dx
                dg_ref = refs[n_in + 1]

                @pl.when(i == 0)
                def _():
                    dg_ref[...] = dg

                @pl.when(i > 0)
                def _():
                    dg_ref[...] += dg
            else:
                o_ref[...] = total.astype(o_ref.dtype)

        if nk == 1:
            finish(p)
        else:
            k = pl.program_id(2)

            @pl.when(k == 0)
            def _():
                acc_ref[...] = p

            @pl.when(k > 0)
            def _():
                acc_ref[...] += p

            @pl.when(k == nk - 1)
            def _():
                finish(acc_ref[...])

    ins = [a, b] + ([add] if has_add else []) + extra
    specs = [a_spec, b_spec] + ([o_spec] if has_add else []) + extra_specs
    out = pl.pallas_call(
        body, name=name, grid=(M // tm, N // tn, nk), in_specs=specs, out_specs=out_specs,
        out_shape=out_shapes, scratch_shapes=[pltpu.VMEM((tm, tn), F32)],
        compiler_params=_cp(*sem),
    )(*ins)
    return out[0] if n_out == 1 else out


def ffn_in(h, wg, wu):
    (S, K), F = h.shape, wg.shape[0]
    tm, tn = _tile(S, 1024, 128), _tile(F, 1536, 128)
    dn = (((1,), (1,)), ((), ()))

    def body(h_ref, wg_ref, wu_ref, g_ref, u_ref, act_ref):
        a = h_ref[...].astype(BF16)
        g = lax.dot_general(a, wg_ref[...].astype(BF16), dn, preferred_element_type=F32)
        u = lax.dot_general(a, wu_ref[...].astype(BF16), dn, preferred_element_type=F32)
        g_ref[...] = g.astype(BF16)
        u_ref[...] = u.astype(BF16)
        act_ref[...] = f_swiglu(g, u)[0].astype(BF16)

    w_spec = pl.BlockSpec((tn, K), lambda i, j: (j, 0))
    o_spec = pl.BlockSpec((tm, tn), lambda i, j: (i, j))
    sd = jax.ShapeDtypeStruct
    return pl.pallas_call(
        body, name="ffn_in", grid=(S // tm, F // tn),
        in_specs=[pl.BlockSpec((tm, K), lambda i, j: (i, 0)), w_spec, w_spec], out_specs=[o_spec] * 3,
        out_shape=[sd((S, F), BF16), sd((S, F), BF16), sd((S, F), BF16)],
        compiler_params=_cp("parallel", "parallel"),
    )(h, wg, wu)


def ffn_dact(dy, wd, gate, up):
    (S, K), F = dy.shape, wd.shape[0]
    tm, tn = _tile(S, 1024, 128), _tile(F, 1536, 128)
    dn = (((1,), (1,)), ((), ()))

    def body(dy_ref, wd_ref, g_ref, u_ref, dg_ref, du_ref):
        dact = lax.dot_general(dy_ref[...].astype(BF16), wd_ref[...].astype(BF16), dn, preferred_element_type=F32)
        _, vjp = jax.vjp(f_swiglu, g_ref[...].astype(F32), u_ref[...].astype(F32))
        dg, du = vjp((dact,))
        dg_ref[...] = dg.astype(BF16)
        du_ref[...] = du.astype(BF16)

    o_spec = pl.BlockSpec((tm, tn), lambda i, j: (i, j))
    sd = jax.ShapeDtypeStruct
    return pl.pallas_call(
        body, name="ffn_dact", grid=(S // tm, F // tn),
        in_specs=[pl.BlockSpec((tm, K), lambda i, j: (i, 0)), pl.BlockSpec((tn, K), lambda i, j: (j, 0)),
                  o_spec, o_spec],
        out_specs=[o_spec] * 2, out_shape=[sd((S, F), BF16)] * 2,
        compiler_params=_cp("parallel", "parallel"),
    )(dy, wd, gate, up)


def _row_spec(arr, tile):
    if isinstance(arr, tuple):
        a, w, cb = arr[:3]
        ro = (arr[3] // tile) if len(arr) > 3 else 0
        assert len(arr) < 4 or arr[3] % tile == 0
        return a, pl.BlockSpec((tile, w), lambda i, cb=cb, ro=ro: (i + ro, cb)), (tile, w)
    if arr.ndim == 3:
        d0, _, d2 = arr.shape
        return arr, pl.BlockSpec((d0, tile, d2), lambda i: (0, i, 0)), (d0, tile, d2)
    return arr, pl.BlockSpec((tile, arr.shape[1]), lambda i: (i, 0)), (tile, arr.shape[1])


def _full_spec(arr):
    nd = arr.ndim
    return pl.BlockSpec(arr.shape, lambda i, nd=nd: (0,) * nd)


def _n_rows(arr):
    a = arr[0] if isinstance(arr, tuple) else arr
    return a.shape[1] if a.ndim == 3 else a.shape[0]


def _row_out_shape(shape_tail, n, dtype):
    if isinstance(shape_tail, tuple):
        d0, d2 = shape_tail
        return (jax.ShapeDtypeStruct((d0, n, d2), dtype),
                lambda tile: pl.BlockSpec((d0, tile, d2), lambda i: (0, i, 0)))
    return (jax.ShapeDtypeStruct((n, shape_tail), dtype),
            lambda tile: pl.BlockSpec((tile, shape_tail), lambda i: (i, 0)))


def _f32(v):
    return v.astype(F32) if v.dtype == BF16 else v


def row_fwd(name, fn, rows, fulls, outs, tile=ROW_TILE):
    n = _n_rows(rows[0])
    tile = min(tile, n)
    assert n % tile == 0, (name, n, tile)
    rs = [_row_spec(r, tile) for r in rows]
    os_ = [_row_out_shape(w, n, dt) for w, dt in outs]
    nr, nf = len(rows), len(fulls)

    def body(*refs):
        vals = [_f32(r[...]) for r in refs[:nr + nf]]
        res = fn(*vals)
        for r, v in zip(refs[nr + nf:], res):
            r[...] = v.astype(r.dtype)

    out = pl.pallas_call(
        body, name=name, grid=(n // tile,),
        in_specs=[s for _, s, _ in rs] + [_full_spec(f) for f in fulls],
        out_specs=[mk(tile) for _, mk in os_], out_shape=[sh for sh, _ in os_],
        compiler_params=_cp("parallel"),
    )(*[a for a, _, _ in rs], *fulls)
    return out


def row_bwd(name, fn, rows, fulls, cts, want_rows, want_fulls, row_dtypes=None, tile=ROW_TILE):
    n = _n_rows(rows[0])
    tile = min(tile, n)
    assert n % tile == 0, (name, n, tile)
    rs = [_row_spec(r, tile) for r in rows]
    cs = [_row_spec(c, tile) for c in cts]
    nr, nf, nc = len(rows), len(fulls), len(cts)
    row_dtypes = row_dtypes or [F32] * len(want_rows)
    out_shapes, out_specs = [], []
    for k, idx in enumerate(want_rows):
        a, _, blk = rs[idx]
        if len(blk) == 3:
            sh, mk = _row_out_shape((blk[0], blk[2]), n, row_dtypes[k])
        else:
            sh, mk = _row_out_shape(blk[1], n, row_dtypes[k])
        out_shapes.append(sh)
        out_specs.append(mk(tile))
    for idx in want_fulls:
        out_shapes.append(jax.ShapeDtypeStruct(fulls[idx].shape, F32))
        out_specs.append(_full_spec(fulls[idx]))
    n_wr = len(want_rows)

    def body(*refs):
        i = pl.program_id(0)
        vals = [_f32(r[...]) for r in refs[:nr + nf]]
        ct_vals = [_f32(r[...]) for r in refs[nr + nf:nr + nf + nc]]
        outs = refs[nr + nf + nc:]
        _, vjp = jax.vjp(fn, *vals)
        grads = vjp(tuple(ct_vals))
        for k, idx in enumerate(want_rows):
            outs[k][...] = grads[idx].astype(outs[k].dtype)
        for k, idx in enumerate(want_fulls):
            o = outs[n_wr + k]
            g = grads[nr + idx]

            @pl.when(i == 0)
            def _(o=o, g=g):
                o[...] = g

            @pl.when(i > 0)
            def _(o=o, g=g):
                o[...] += g

    out = pl.pallas_call(
        body, name=name, grid=(n // tile,),
        in_specs=[s for _, s, _ in rs] + [_full_spec(f) for f in fulls] + [s for _, s, _ in cs],
        out_specs=out_specs, out_shape=out_shapes,
        compiler_params=_cp("arbitrary"),
    )(*[a for a, _, _ in rs], *fulls, *[a for a, _, _ in cs])
    return out


def whole(name, fn, args, outs):
    def body(*refs):
        res = fn(*[r[...] for r in refs[:len(args)]])
        for r, v in zip(refs[len(args):], res):
            r[...] = v.astype(r.dtype)

    return pl.pallas_call(
        body, name=name, out_shape=[jax.ShapeDtypeStruct(s, d) for s, d in outs],
        compiler_params=pltpu.CompilerParams(vmem_limit_bytes=VMEM_LIMIT),
    )(*args)


def f_norm(x, g):
    return (_rms(x, g),)


def f_norm_res(x, g):
    return _rms(x, g), x


def f_swiglu(g, u):
    return (g * jax.nn.sigmoid(g) * u,)


def f_glu(y, w, b):
    h = jax.nn.gelu(y)
    return (h * jax.nn.sigmoid(mm(h, w) + b),)


def f_xattn(q, kv, qg, kg):
    outs = []
    for h in range(XA_H):
        sl = slice(h * XA_D, (h + 1) * XA_D)
        qn = _rms(q[:, sl], qg)
        kn = _rms(kv[:, sl], kg)
        vh = kv[:, D + h * XA_D:D + (h + 1) * XA_D]
        p = _softmax(mm_nt(qn, kn) * (XA_D ** -0.5))
        outs.append(mm(p, vh))
    return (jnp.concatenate(outs, axis=-1),)


def f_adam(w, g, m, v):
    m2 = B1 * m + (1.0 - B1) * g
    v2 = B2 * v + (1.0 - B2) * (g * g)
    m_hat = m2 / (1.0 - B1 ** STEP)
    v_hat = v2 / (1.0 - B2 ** STEP)
    delta = -LR * (m_hat / (jnp.sqrt(v_hat) + AEPS) + WD * w)
    return delta, m2, v2


S5_NTAB, S5_NROW = 4, 6


def _s5_tables(are, aim, ldt):
    T = CHUNK
    dt = jnp.exp(ldt)
    ar, ai = are * dt, aim * dt
    t = lax.broadcasted_iota(jnp.int32, (T, 1), 0).astype(F32)
    mag, inv = jnp.exp(t * ar), jnp.exp(-t * ar)
    cs, sn = jnp.cos(t * ai), jnp.sin(t * ai)
    e_re, e_im = mag * cs, mag * sn
    n_re, n_im = inv * cs, -inv * sn
    l_re, l_im = jnp.exp(ar) * jnp.cos(ai), jnp.exp(ar) * jnp.sin(ai)
    den = are * are + aim * aim
    k_re = ((l_re - 1.0) * are + l_im * aim) / den
    k_im = (l_im * are - (l_re - 1.0) * aim) / den
    tl = float(T - 1)
    m_re, m_im = jnp.exp(tl * ar) * jnp.cos(tl * ai), jnp.exp(tl * ar) * jnp.sin(tl * ai)
    return (e_re, e_im, n_re, n_im), (l_re, l_im, k_re, k_im, m_re, m_im)


def _s5_chunk(u, sre, sim, tabs, rows, b_re, b_im, c_re, c_im, dv):
    e_re, e_im, n_re, n_im = tabs
    l_re, l_im, k_re, k_im, m_re, m_im = rows
    x_re, x_im = mm(u, b_re), mm(u, b_im)
    bu_re = k_re * x_re - k_im * x_im
    bu_im = k_re * x_im + k_im * x_re
    v_re = bu_re * n_re - bu_im * n_im
    v_im = bu_re * n_im + bu_im * n_re
    p_re = l_re * sre - l_im * sim
    p_im = l_re * sim + l_im * sre
    w_re = cumsum_rows(v_re) + p_re
    w_im = cumsum_rows(v_im) + p_im
    s_re = e_re * w_re - e_im * w_im
    s_im = e_re * w_im + e_im * w_re
    y = mm(s_re, c_re) - mm(s_im, c_im) + dv * u
    z_re = jnp.sum(v_re, axis=0, keepdims=True) + p_re
    z_im = jnp.sum(v_im, axis=0, keepdims=True) + p_im
    return y, m_re * z_re - m_im * z_im, m_re * z_im + m_im * z_re


def _s5_fill_tables(are_ref, aim_ref, ldt_ref, tab, row):
    for g in range(S5_GB):
        ls = slice(512 * g, 512 * (g + 1))
        tabs, rows = _s5_tables(are_ref[:, ls], aim_ref[:, ls], ldt_ref[:, ls])
        for k, t in enumerate(tabs):
            tab[k, :, ls] = t
        for k, r in enumerate(rows):
            row[k:k + 1, ls] = r


def _s5_read_tables(tab, row, ls):
    return (tuple(tab[k, :, ls] for k in range(S5_NTAB)), tuple(row[k:k + 1, ls] for k in range(S5_NROW)))


def _s5_specs(nc, rev):
    T = CHUNK

    def ci(c):
        return nc - 1 - c if rev else c

    u_spec = pl.BlockSpec((T, S5_W), lambda c: (ci(c), 0))
    p_spec = pl.BlockSpec((1, S5_G * S5_P), lambda c: (0, 0))
    b_spec = pl.BlockSpec((S5_GB, 128, 512), lambda c: (0, 0, 0))
    c_spec = pl.BlockSpec((S5_GB, 512, 128), lambda c: (0, 0, 0))
    d_spec = pl.BlockSpec((1, S5_W), lambda c: (0, 0))
    st_spec = pl.BlockSpec((None, 2, S5_G * S5_P), lambda c: (ci(c), 0, 0))
    return u_spec, p_spec, b_spec, c_spec, d_spec, st_spec


def s5_fwd(pm, are, aim, ldt, b_re, b_im, c_re, c_im, dv):
    S = pm.shape[0]
    nc = S // CHUNK
    u_spec, p_spec, b_spec, c_spec, d_spec, st_spec = _s5_specs(nc, False)

    def body(u_ref, are_ref, aim_ref, ldt_ref, bre_ref, bim_ref, cre_ref, cim_ref, dv_ref,
             y_ref, st_ref, state, tab, row):
        c = pl.program_id(0)

        @pl.when(c == 0)
        def _():
            state[...] = jnp.zeros_like(state)
            _s5_fill_tables(are_ref, aim_ref, ldt_ref, tab, row)

        st_ref[...] = state[...]
        for g in range(S5_GB):
            lu, ls = slice(128 * g, 128 * (g + 1)), slice(512 * g, 512 * (g + 1))
            tabs, rows = _s5_read_tables(tab, row, ls)
            y, e_re, e_im = _s5_chunk(u_ref[:, lu], state[0:1, ls], state[1:2, ls], tabs, rows,
                                      bre_ref[g], bim_ref[g], cre_ref[g], cim_ref[g], dv_ref[:, lu])
            y_ref[:, lu] = y
            state[0:1, ls] = e_re
            state[1:2, ls] = e_im

    n_state = S5_G * S5_P
    return pl.pallas_call(
        body, name="s5_fwd", grid=(nc,),
        in_specs=[u_spec, p_spec, p_spec, p_spec, b_spec, b_spec, c_spec, c_spec, d_spec],
        out_specs=[u_spec, st_spec],
        out_shape=[jax.ShapeDtypeStruct((S, S5_W), F32), jax.ShapeDtypeStruct((nc, 2, n_state), F32)],
        scratch_shapes=[pltpu.VMEM((2, n_state), F32), pltpu.VMEM((S5_NTAB, CHUNK, n_state), F32),
                        pltpu.VMEM((8, n_state), F32)],
        compiler_params=_cp("arbitrary"),
    )(pm, are, aim, ldt, b_re, b_im, c_re, c_im, dv)


def s5_bwd(pm, st, dy, are, aim, ldt, b_re, b_im, c_re, c_im, dv):
    S = pm.shape[0]
    nc = S // CHUNK
    u_spec, p_spec, b_spec, c_spec, d_spec, st_spec = _s5_specs(nc, True)

    def body(u_ref, st_ref, dy_ref, are_ref, aim_ref, ldt_ref, bre_ref, bim_ref, cre_ref, cim_ref, dv_ref,
             du_ref, dare_ref, daim_ref, dldt_ref, dbre_ref, dbim_ref, dcre_ref, dcim_ref, ddv_ref,
             dstate, tab, row, dtab, drow):
        c = pl.program_id(0)

        @pl.when(c == 0)
        def _():
            dstate[...] = jnp.zeros_like(dstate)
            dtab[...] = jnp.zeros_like(dtab)
            drow[...] = jnp.zeros_like(drow)
            _s5_fill_tables(are_ref, aim_ref, ldt_ref, tab, row)

        for g in range(S5_GB):
            lu, ls = slice(128 * g, 128 * (g + 1)), slice(512 * g, 512 * (g + 1))
            every = slice(None)
            tabs, rows = _s5_read_tables(tab, row, ls)
            args = (u_ref[:, lu], st_ref[0:1, ls], st_ref[1:2, ls], tabs, rows,
                    bre_ref[g], bim_ref[g], cre_ref[g], cim_ref[g], dv_ref[:, lu])
            _, vjp = jax.vjp(_s5_chunk, *args)
            gr = vjp((dy_ref[:, lu], dstate[0:1, ls], dstate[1:2, ls]))
            du_ref[:, lu] = gr[0]
            dstate[0:1, ls] = gr[1]
            dstate[1:2, ls] = gr[2]
            for k, t in enumerate(gr[3]):
                dtab[k, :, ls] += t
            for k, r in enumerate(gr[4]):
                drow[k:k + 1, ls] += r
            accs = ((dbre_ref, (g,)), (dbim_ref, (g,)), (dcre_ref, (g,)), (dcim_ref, (g,)), (ddv_ref, (every, lu)))
            for (o, idx), gv in zip(accs, gr[5:]):
                @pl.when(c == 0)
                def _(o=o, idx=idx, gv=gv):
                    o[idx] = gv

                @pl.when(c > 0)
                def _(o=o, idx=idx, gv=gv):
                    o[idx] += gv

        @pl.when(c == nc - 1)
        def _():
            for g in range(S5_GB):
                ls = slice(512 * g, 512 * (g + 1))
                _, vjp = jax.vjp(_s5_tables, are_ref[:, ls], aim_ref[:, ls], ldt_ref[:, ls])
                dtabs, drows = _s5_read_tables(dtab, drow, ls)
                ga, gi, gl = vjp((dtabs, drows))
                dare_ref[:, ls] = ga
                daim_ref[:, ls] = gi
                dldt_ref[:, ls] = gl

    n_state = S5_G * S5_P
    return pl.pallas_call(
        body, name="s5_bwd", grid=(nc,),
        in_specs=[u_spec, st_spec, u_spec, p_spec, p_spec, p_spec, b_spec, b_spec, c_spec, c_spec, d_spec],
        out_specs=[u_spec, p_spec, p_spec, p_spec, b_spec, b_spec, c_spec, c_spec, d_spec],
        out_shape=[jax.ShapeDtypeStruct((S, S5_W), F32)] + [jax.ShapeDtypeStruct((1, n_state), F32)] * 3
        + [jax.ShapeDtypeStruct((S5_GB, 128, 512), F32)] * 2 + [jax.ShapeDtypeStruct((S5_GB, 512, 128), F32)] * 2
        + [jax.ShapeDtypeStruct((1, S5_W), F32)],
        scratch_shapes=[pltpu.VMEM((2, n_state), F32), pltpu.VMEM((S5_NTAB, CHUNK, n_state), F32),
                        pltpu.VMEM((8, n_state), F32), pltpu.VMEM((S5_NTAB, CHUNK, n_state), F32),
                        pltpu.VMEM((8, n_state), F32)],
        compiler_params=_cp("arbitrary"),
    )(pm, st, dy, are, aim, ldt, b_re, b_im, c_re, c_im, dv)


def conv_fwd(pm, w):
    S = pm.shape[0]

    def body(x_ref, w_ref, o_ref, pad):
        x = x_ref[...]
        pad[0:8, :] = jnp.zeros((8, 128), F32)
        pad[8:, :] = x
        y = (w_ref[3:4, :] * x + w_ref[2:3, :] * pad[7:7 + S, :] + w_ref[1:2, :] * pad[6:6 + S, :]
             + w_ref[0:1, :] * pad[5:5 + S, :])
        o_ref[...] = y * jax.nn.sigmoid(y)

    return pl.pallas_call(
        body, name="conv_fwd", grid=(12,),
        in_specs=[pl.BlockSpec((S, 128), lambda j: (0, 4 + j)), pl.BlockSpec((4, 128), lambda j: (0, j))],
        out_specs=pl.BlockSpec((S, 128), lambda j: (0, j)),
        out_shape=jax.ShapeDtypeStruct((S, 1536), F32),
        scratch_shapes=[pltpu.VMEM((S + 8, 128), F32)],
        compiler_params=_cp("parallel"),
    )(pm, w)


def conv_bwd(pm, w, dout):
    S = pm.shape[0]

    def body(x_ref, w_ref, do_ref, dx_ref, dw_ref, pad, dpad):
        x = x_ref[...]
        pad[0:8, :] = jnp.zeros((8, 128), F32)
        pad[8:, :] = x
        xs = [pad[5:5 + S, :], pad[6:6 + S, :], pad[7:7 + S, :], x]
        y = w_ref[0:1, :] * xs[0] + w_ref[1:2, :] * xs[1] + w_ref[2:3, :] * xs[2] + w_ref[3:4, :] * xs[3]
        sg = jax.nn.sigmoid(y)
        dy = do_ref[...] * (sg + y * sg * (1.0 - sg))
        dpad[0:S, :] = dy
        dpad[S:, :] = jnp.zeros((8, 128), F32)
        dx_ref[...] = (w_ref[3:4, :] * dy + w_ref[2:3, :] * dpad[1:1 + S, :] + w_ref[1:2, :] * dpad[2:2 + S, :]
                       + w_ref[0:1, :] * dpad[3:3 + S, :])
        for i in range(4):
            dw_ref[i:i + 1, :] = jnp.sum(dy * xs[i], axis=0, keepdims=True)

    return pl.pallas_call(
        body, name="conv_bwd", grid=(12,),
        in_specs=[pl.BlockSpec((S, 128), lambda j: (0, 4 + j)), pl.BlockSpec((4, 128), lambda j: (0, j)),
                  pl.BlockSpec((S, 128), lambda j: (0, j))],
        out_specs=[pl.BlockSpec((S, 128), lambda j: (0, j)), pl.BlockSpec((4, 128), lambda j: (0, j))],
        out_shape=[jax.ShapeDtypeStruct((S, 1536), F32), jax.ShapeDtypeStruct((4, 1536), F32)],
        scratch_shapes=[pltpu.VMEM((S + 8, 128), F32), pltpu.VMEM((S + 8, 128), F32)],
        compiler_params=_cp("parallel"),
    )(pm, w, dout)


GDN_SUP = 4
GDN_ROWS = GDN_SUP * CHUNK


@jax.custom_vjp
def _saved_inverse(a, x):
    return x


_saved_inverse.defvjp(lambda a, x: (x, x),
                      lambda x, g: (-mm_tn(x, mm_nt(g, x)), jnp.zeros_like(x)))


def _gdn_chunk(q, k, v, gate, al, bl, alog, dtb, og, state, inv=None, want_inv=False):
    R = q.shape[0]
    r = lax.broadcasted_iota(jnp.int32, (R, R), 0)
    c = lax.broadcasted_iota(jnp.int32, (R, R), 1)
    same = (r // CHUNK) == (c // CHUNK)
    eye = (r == c).astype(F32)
    strict, causal, upper = same & (r > c), same & (r >= c), same & (r <= c)
    qn = q * lax.rsqrt(jnp.sum(q * q, axis=-1, keepdims=True) + EPS) * (GDN_D ** -0.5)
    kn = k * lax.rsqrt(jnp.sum(k * k, axis=-1, keepdims=True) + EPS)
    beta = jnp.sum(eye * jax.nn.sigmoid(bl), axis=1, keepdims=True)
    g_row = -jnp.exp(alog) * _softplus(al + dtb)
    g = jnp.sum(eye * g_row, axis=1, keepdims=True)
    gc_col = jnp.sum(jnp.where(causal, g_row, 0.0), axis=1, keepdims=True)
    gc_row = jnp.sum(jnp.where(upper, g, 0.0), axis=0, keepdims=True)
    gtot = jnp.sum(jnp.where(same, g_row, 0.0), axis=1, keepdims=True)
    gamma = jnp.exp(gc_col)
    diff = gc_col - gc_row
    d_strict = jnp.where(strict, jnp.exp(jnp.where(strict, diff, 0.0)), 0.0)
    d_causal = jnp.where(causal, jnp.exp(jnp.where(causal, diff, 0.0)), 0.0)
    a = beta * mm_nt(kn, kn) * d_strict
    if inv is None:
        p = -a
        x = eye + p
        for _ in range(5):
            p = mm(p, p)
            x = x + mm(x, p)
    else:
        x = _saved_inverse(a, inv)
    u_new = mm(x, beta * v)
    w_k = mm(x, (beta * gamma) * kn)
    qk = mm_nt(qn, kn) * d_causal
    q_g = qn * gamma
    k_tail = kn * jnp.exp(gtot - gc_col)
    ws, os_ = [], []
    for i in range(R // CHUNK):
        rows = slice(CHUNK * i, CHUNK * (i + 1))
        w_i = u_new[rows] - mm(w_k[rows], state)
        os_.append(mm(q_g[rows], state))
        decay = jnp.exp(jnp.sum(g[rows], axis=0, keepdims=True))
        state = decay * state + mm_tn(k_tail[rows], w_i)
        ws.append(w_i)
    o = jnp.concatenate(os_, axis=0) + mm(qk, jnp.concatenate(ws, axis=0))
    out = _rms(o, og) * (gate * jax.nn.sigmoid(gate))
    return (out, state, x) if want_inv else (out, state)


def _gdn_specs(nc, rev):
    def ci(c):
        return nc - 1 - c if rev else c

    def blk(cb):
        return pl.BlockSpec((GDN_ROWS, 512), lambda c: (ci(c), cb))

    col = lambda n: pl.BlockSpec((n, GDN_ROWS), lambda c: (0, ci(c)))
    sc = pl.BlockSpec((GDN_H, 1, 1), lambda c: (0, 0, 0))
    og = pl.BlockSpec((1, 128), lambda c: (0, 0))
    st = pl.BlockSpec((GDN_H, None, 128, 128), lambda c: (0, ci(c), 0, 0))
    return blk, col, sc, og, st


def gdn_fwd(qkvc, pm, abt, alog, dtb, og):
    S = qkvc.shape[0]
    nc = S // GDN_ROWS
    blk, col, sc, ogs, st = _gdn_specs(nc, False)

    def body(q_ref, k_ref, v_ref, gate_ref, ab_ref, alog_ref, dtb_ref, og_ref, o_ref, st_ref, inv_ref, state):
        c = pl.program_id(0)

        @pl.when(c == 0)
        def _():
            state[...] = jnp.zeros_like(state)

        st_ref[...] = state[...]
        for h in range(GDN_H):
            sl = slice(GDN_D * h, GDN_D * (h + 1))
            out, new_state, inv = _gdn_chunk(
                q_ref[:, sl], k_ref[:, sl], v_ref[:, sl], gate_ref[:, sl], ab_ref[h:h + 1, :], ab_ref[GDN_H + h:GDN_H + h + 1, :],
                alog_ref[h], dtb_ref[h], og_ref[...], state[h], want_inv=True)
            o_ref[:, sl] = out
            state[h] = new_state
            inv_ref[h] = inv

    inv_spec = pl.BlockSpec((GDN_H, None, GDN_ROWS, GDN_ROWS), lambda c: (0, c, 0, 0))
    return pl.pallas_call(
        body, name="gdn_fwd", grid=(nc,),
        in_specs=[blk(0), blk(1), blk(2), blk(4), col(2 * GDN_H), sc, sc, ogs],
        out_specs=[blk(0), st, inv_spec],
        out_shape=[jax.ShapeDtypeStruct((S, 512), F32), jax.ShapeDtypeStruct((GDN_H, nc, 128, 128), F32),
                   jax.ShapeDtypeStruct((GDN_H, nc, GDN_ROWS, GDN_ROWS), F32)],
        scratch_shapes=[pltpu.VMEM((GDN_H, 128, 128), F32)],
        compiler_params=_cp("arbitrary"),
    )(qkvc, qkvc, qkvc, pm, abt, alog, dtb, og)


def gdn_bwd(qkvc, pm, abt, alog, dtb, og, st, inv, dout):
    S = qkvc.shape[0]
    nc = S // GDN_ROWS
    blk, col, sc, ogs, sts = _gdn_specs(nc, True)

    def body(q_ref, k_ref, v_ref, gate_ref, ab_ref, alog_ref, dtb_ref, og_ref, st_ref, inv_ref, do_ref,
             dqkv_ref, dgate_ref, dal_ref, dbl_ref, dalog_ref, ddtb_ref, dog_ref, dstate):
        c = pl.program_id(0)

        @pl.when(c == 0)
        def _():
            dstate[...] = jnp.zeros_like(dstate)

        for h in range(GDN_H):
            sl = slice(GDN_D * h, GDN_D * (h + 1))
            args = (q_ref[:, sl], k_ref[:, sl], v_ref[:, sl], gate_ref[:, sl], ab_ref[h:h + 1, :], ab_ref[GDN_H + h:GDN_H + h + 1, :],
                    alog_ref[h], dtb_ref[h], og_ref[...], st_ref[h])
            inv_h = inv_ref[h]
            _, vjp = jax.vjp(lambda *a, inv_h=inv_h: _gdn_chunk(*a, inv=inv_h), *args)
            g = vjp((do_ref[:, sl], dstate[h]))
            for part in range(3):
                dqkv_ref[:, slice(512 * part + sl.start, 512 * part + sl.stop)] = g[part]
            dgate_ref[:, sl] = g[3]
            dal_ref[h:h + 1, :] = g[4]
            dbl_ref[h:h + 1, :] = g[5]
            dstate[h] = g[9]
            for o, gv in zip((dalog_ref, ddtb_ref, dog_ref), g[6:9]):
                @pl.when(c == 0)
                def _(o=o, gv=gv, h=h):
                    o[h] = gv

                @pl.when(c > 0)
                def _(o=o, gv=gv, h=h):
                    o[h] += gv

    ogo = pl.BlockSpec((GDN_H, 1, 128), lambda c: (0, 0, 0))
    inv_spec = pl.BlockSpec((GDN_H, None, GDN_ROWS, GDN_ROWS), lambda c: (0, nc - 1 - c, 0, 0))
    sd = jax.ShapeDtypeStruct
    return pl.pallas_call(
        body, name="gdn_bwd", grid=(nc,),
        in_specs=[blk(0), blk(1), blk(2), blk(4), col(2 * GDN_H), sc, sc, ogs, sts, inv_spec, blk(0)],
        out_specs=[pl.BlockSpec((GDN_ROWS, 1536), lambda c: (nc - 1 - c, 0)), blk(0), col(GDN_H), col(GDN_H),
                   sc, sc, ogo],
        out_shape=[sd((S, 1536), F32), sd((S, 512), F32)] + [sd((GDN_H, S), F32)] * 2 + [sd((GDN_H, 1, 1), F32)] * 2
        + [sd((GDN_H, 1, 128), F32)],
        scratch_shapes=[pltpu.VMEM((GDN_H, 128, 128), F32)],
        compiler_params=_cp("arbitrary"),
    )(qkvc, qkvc, qkvc, pm, abt, alog, dtb, og, st, inv, dout)


HG = 8
HG_LANES = HG * CA_D


def _group_mean_raw(y):
    r = lax.broadcasted_iota(jnp.int32, (128, 128), 0)
    c = lax.broadcasted_iota(jnp.int32, (128, 128), 1)
    g = jnp.where((r // CA_D) == (c // CA_D), 1.0 / CA_D, 0.0).astype(BF16)
    d = lambda u: lax.dot_general(u, g, (((1,), (0,)), ((), ())), preferred_element_type=F32)
    outs = []
    for j in range(y.shape[1] // 128):
        hi, lo = _split(y[:, 128 * j:128 * (j + 1)])
        outs.append(d(hi) + d(lo))
    return jnp.concatenate(outs, axis=1)


@jax.custom_vjp
def group_mean(y):
    return _group_mean_raw(y)


group_mean.defvjp(lambda y: (_group_mean_raw(y), None), lambda _, g: (_group_mean_raw(g),))


def f_headnorm(t, g):
    return (t * lax.rsqrt(group_mean(t * t) + EPS) * g,)


def _cattn_chunk(q, kb, vb, bias, valid):
    lane = lax.broadcasted_iota(jnp.int32, (1, 128), 1)
    m0 = (lane < CA_D).astype(F32)
    m1 = 1.0 - m0
    pairs = range(HG // 2)
    sl = [slice(128 * p, 128 * (p + 1)) for p in pairs]
    q2 = [jnp.concatenate([q[:, s] * m0, q[:, s] * m1], axis=0) for s in sl]
    sc = [mm_nt(q2[p], kb[:, sl[p]]) * (CA_D ** -0.5) + bias[sl[p]] for p in pairs]
    pr = [_softmax(jnp.where(valid, s, -1e30)) for s in sc]
    o2 = [mm(pr[p], vb[:, sl[p]]) for p in pairs]
    return jnp.concatenate([o[:CHUNK] * m0 + o[CHUNK:] * m1 for o in o2], axis=1)


def _cattn_valid(c):
    pos = lax.broadcasted_iota(jnp.int32, (1, CA_BAND), 1) + c * CHUNK
    return pos >= CA_PAD


def _cattn_specs(S):
    q_spec = pl.BlockSpec((CHUNK, HG_LANES), lambda h, c: (c, h))
    kv_spec = pl.BlockSpec((S + CA_PAD, HG_LANES), lambda h, c: (0, h))
    b_spec = pl.BlockSpec((HG * CHUNK, CA_BAND), lambda h, c: (h, 0))
    return q_spec, kv_spec, b_spec


def cattn_fwd(qn, kp, vp, bias):
    S = qn.shape[0]
    nc = S // CHUNK
    q_spec, kv_spec, b_spec = _cattn_specs(S)

    def body(q_ref, k_ref, v_ref, b_ref, o_ref):
        c = pl.program_id(1)
        start = pl.multiple_of(c * CHUNK, CHUNK)
        kb = k_ref[pl.ds(start, CA_BAND), :]
        vb = v_ref[pl.ds(start, CA_BAND), :]
        o_ref[...] = _cattn_chunk(q_ref[...], kb, vb, b_ref[...], _cattn_valid(c)).astype(o_ref.dtype)

    return pl.pallas_call(
        body, name="cattn_fwd", grid=(CA_H // HG, nc), in_specs=[q_spec, kv_spec, kv_spec, b_spec],
        out_specs=q_spec, out_shape=jax.ShapeDtypeStruct((S, D), BF16),
        compiler_params=_cp("parallel", "arbitrary"),
    )(qn, kp, vp, bias)


def kv_prep(qkv, kg):
    S = qkv.shape[0]
    tile = ROW_TILE
    lead = CA_PAD // tile

    def body(k_ref, v_ref, g_ref, kp_ref, vp_ref):
        i = pl.program_id(0)

        @pl.when(i < lead)
        def _():
            kp_ref[...] = jnp.zeros_like(kp_ref)
            vp_ref[...] = jnp.zeros_like(vp_ref)

        @pl.when(i >= lead)
        def _():
            kp_ref[...] = f_headnorm(k_ref[...], g_ref[...])[0].astype(BF16)
            vp_ref[...] = v_ref[...].astype(BF16)

    src = lambda cb: pl.BlockSpec((tile, D), lambda i, cb=cb: (jnp.maximum(i - lead, 0), cb))
    out = pl.BlockSpec((tile, D), lambda i: (i, 0))
    return pl.pallas_call(
        body, name="kv_prep", grid=((S + CA_PAD) // tile,),
        in_specs=[src(1), src(2), pl.BlockSpec((1, D), lambda i: (0, 0))], out_specs=[out, out],
        out_shape=[jax.ShapeDtypeStruct((S + CA_PAD, D), BF16)] * 2,
        compiler_params=_cp("parallel"),
    )(qkv, qkv, kg)


def cattn_bwd(qn, kp, vp, bias, do):
    S = qn.shape[0]
    nc = S // CHUNK
    q_spec, kv_spec, b_spec = _cattn_specs(S)

    def body(q_ref, k_ref, v_ref, b_ref, do_ref, dq_ref, dk_ref, dv_ref, db_ref):
        c = pl.program_id(1)

        @pl.when(c == 0)
        def _():
            dk_ref[...] = jnp.zeros_like(dk_ref)
            dv_ref[...] = jnp.zeros_like(dv_ref)
            db_ref[...] = jnp.zeros_like(db_ref)

        start = pl.multiple_of(c * CHUNK, CHUNK)
        kb = k_ref[pl.ds(start, CA_BAND), :].astype(F32)
        vb = v_ref[pl.ds(start, CA_BAND), :].astype(F32)
        valid = _cattn_valid(c)
        _, vjp = jax.vjp(lambda q, k, v, b: _cattn_chunk(q, k, v, b, valid), q_ref[...], kb, vb, b_ref[...])
        dq, dk, dv, db = vjp(do_ref[...])
        dq_ref[...] = dq
        dk_ref[pl.ds(start, CA_BAND), :] += dk
        dv_ref[pl.ds(start, CA_BAND), :] += dv
        db_ref[...] += db

    sd = jax.ShapeDtypeStruct
    return pl.pallas_call(
        body, name="cattn_bwd", grid=(CA_H // HG, nc), in_specs=[q_spec, kv_spec, kv_spec, b_spec, q_spec],
        out_specs=[q_spec, kv_spec, kv_spec, b_spec],
        out_shape=[sd((S, D), F32), sd((S + CA_PAD, D), F32), sd((S + CA_PAD, D), F32),
                   sd((CA_H * CHUNK, CA_BAND), F32)],
        compiler_params=_cp("parallel", "arbitrary"),
    )(qn, kp, vp, bias, do)


SKEW_W = CA_BAND + CHUNK


def rel_bias_grad(dbias):
    padded = jnp.pad(dbias, ((0, 0), (0, 0), (CHUNK, 0)))
    flat = jnp.pad(padded.reshape(CA_H, CHUNK * SKEW_W), ((0, 0), (0, CHUNK)))
    skew = flat.reshape(CA_H, CHUNK, SKEW_W + 1)

    first_near = SKEW_W - CHUNK - MAX_REL

    def fn(t):
        colsum = jnp.sum(t, axis=1, keepdims=True)
        j = lax.broadcasted_iota(jnp.int32, colsum.shape, 2)
        far = jnp.sum(jnp.where(j < first_near, colsum, 0.0), axis=2, keepdims=True)
        return (colsum + jnp.where(j == first_near, far, 0.0),)

    (colsum,) = whole("relbias_sum", fn, [skew], [((CA_H, 1, SKEW_W + 1), F32)])
    near = colsum[:, 0, first_near:SKEW_W][:, ::-1]
    return jnp.concatenate([jnp.zeros((CA_H, CHUNK + 1), F32), near], axis=1)


def rel_bias_expand(rb):
    near = rb[:, CHUNK + 1:][:, ::-1]
    far = jnp.broadcast_to(rb[:, 2 * MAX_REL:], (CA_H, SKEW_W - CHUNK - MAX_REL))
    t = jnp.concatenate([far, near, jnp.zeros((CA_H, 1), rb.dtype)], axis=1)
    rows = jnp.tile(t, (1, CHUNK))[:, :CHUNK * SKEW_W].reshape(CA_H, CHUNK, SKEW_W)
    return rows[:, :, CHUNK:]


def loss_head(y, target):
    S = y.shape[0]
    tile = min(ROW_TILE, S)

    def body(y_ref, t_ref, dy_ref, acc_ref):
        i = pl.program_id(0)
        e = y_ref[...] - t_ref[...]
        dy_ref[...] = e * (1.0 / D)
        part = jnp.sum(e * e, axis=0, keepdims=True) * (0.5 / D)

        @pl.when(i == 0)
        def _():
            acc_ref[...] = part

        @pl.when(i > 0)
        def _():
            acc_ref[...] += part

    row = pl.BlockSpec((tile, D), lambda i: (i, 0))
    return pl.pallas_call(
        body, name="loss_head", grid=(S // tile,), in_specs=[row, row],
        out_specs=[row, pl.BlockSpec((1, D), lambda i: (0, 0))],
        out_shape=[jax.ShapeDtypeStruct((S, D), F32), jax.ShapeDtypeStruct((1, D), F32)],
        compiler_params=_cp("arbitrary"),
    )(y, target)


ANY = pl.BlockSpec(memory_space=pl.ANY)


HBM = pl.BlockSpec(memory_space=pltpu.HBM)
SEM = pl.BlockSpec(memory_space=pltpu.SEMAPHORE)
EFFECT = pltpu.SideEffectType.DATAFLOW_SIDE_EFFECTING


def _chip_copies(src_ref, land_ref, send_sems, recv_sems, mode):
    x, y, c = lax.axis_index("x"), lax.axis_index("y"), lax.axis_index("c")
    me = 2 * x + y

    def copy(j, s, d, dev):
        return pltpu.make_async_remote_copy(src_ref=s, dst_ref=d, send_sem=send_sems.at[j], recv_sem=recv_sems.at[j],
                                            device_id=dev, device_id_type=MESH)

    if mode == "sibling":
        cp = copy(0, src_ref, land_ref, (x, y, 1 - c))
        return [(cp, cp)]
    out = []
    for j, (px, py) in enumerate([(1 - x, y), (x, 1 - y), (1 - x, 1 - y)]):
        peer = 2 * px + py
        if mode == "scatter":
            pairs, dev = ((src_ref.at[peer], land_ref.at[j]), (src_ref.at[me], land_ref.at[j])), (px, py, c)
        elif mode == "gather":
            pairs, dev = ((src_ref, land_ref.at[me]), (src_ref, land_ref.at[peer])), (px, py, c)
        else:
            half = land_ref.shape[1] // 2
            mine = pl.ds(pl.multiple_of(c * half, 16), half)
            theirs = pl.ds(pl.multiple_of((1 - c) * half, 16), half)
            if mode == "gather_half":
                pairs = ((src_ref.at[mine], land_ref.at[me, mine]), (src_ref.at[mine], land_ref.at[peer, mine]))
                dev = (px, py, c)
            else:
                pairs = ((land_ref.at[peer, mine],) * 2, (land_ref.at[peer, theirs],) * 2)
                dev = (x, y, 1 - c)
        out.append(tuple(copy(j, s, d, dev) for s, d in pairs))
    return out


def exchange_start(src, land, carry, name, mode):
    def body(src_ref, land_ref, carry_ref, send_sems, recv_sems, src_out, land_out, carry_out):
        for send, _ in _chip_copies(src_ref, land_ref, send_sems, recv_sems, mode):
            send.start()

    hbm = lambda a: pltpu.HBM(a.shape, a.dtype)
    n = 1 if mode == "sibling" else 3
    return pl.pallas_call(
        body, name=name,
        out_shape=(pltpu.SemaphoreType.DMA((n,)), pltpu.SemaphoreType.DMA((n,)), hbm(src), hbm(land), hbm(carry)),
        in_specs=(HBM, HBM, HBM), out_specs=(SEM, SEM, HBM, HBM, HBM),
        input_output_aliases={0: 2, 1: 3, 2: 4},
        compiler_params=pltpu.CompilerParams(has_side_effects=EFFECT),
    )(pltpu.with_memory_space_constraint(src, pltpu.HBM), pltpu.with_memory_space_constraint(land, pltpu.HBM),
      pltpu.with_memory_space_constraint(carry, pltpu.HBM))


def exchange_wait(src, land, send_sems, recv_sems, after, name, mode):
    def body(src_ref, land_ref, send_sems_ref, recv_sems_ref, after_ref, src_out, land_out):
        for send, recv in _chip_copies(src_ref, land_ref, send_sems_ref, recv_sems_ref, mode):
            send.wait_send()
            recv.wait_recv()

    hbm = lambda a: pltpu.HBM(a.shape, a.dtype)
    return pl.pallas_call(
        body, name=name, out_shape=(hbm(src), hbm(land)),
        in_specs=(HBM, HBM, SEM, SEM, ANY), out_specs=(HBM, HBM), input_output_aliases={0: 0, 1: 1},
        compiler_params=pltpu.CompilerParams(has_side_effects=EFFECT),
    )(src, land, send_sems, recv_sems, after)


def sibling_exchange(srcs, name):
    n = len(srcs)

    def body(*refs):
        src_refs, out_refs, send_sems, recv_sems = refs[:n], refs[n:2 * n], refs[2 * n], refs[2 * n + 1]
        x, y, c = lax.axis_index("x"), lax.axis_index("y"), lax.axis_index("c")
        copies = [pltpu.make_async_remote_copy(src_ref=s, dst_ref=o, send_sem=send_sems.at[k], recv_sem=recv_sems.at[k],
                                               device_id=(x, y, 1 - c), device_id_type=MESH)
                  for k, (s, o) in enumerate(zip(src_refs, out_refs))]
        for cp in copies:
            cp.start()
        for cp in copies:
            cp.wait()

    return pl.pallas_call(
        body, name=name, in_specs=[ANY] * n, out_specs=[ANY] * n,
        out_shape=[jax.ShapeDtypeStruct(s.shape, s.dtype) for s in srcs],
        scratch_shapes=[pltpu.SemaphoreType.DMA((n,)), pltpu.SemaphoreType.DMA((n,))],
    )(*srcs)


def sum_own_slabs(src, chip, land, name, tile=512):
    _, R, C = src.shape
    n = land.shape[0]
    tile = _tile(R, tile, 16)

    def body(chip_ref, o_ref, t_ref, out_ref):
        acc = o_ref[...].astype(F32)
        for s in range(n):
            acc = acc + t_ref[s].astype(F32)
        out_ref[...] = acc

    return pl.pallas_call(
        body, name=name,
        grid_spec=pltpu.PrefetchScalarGridSpec(
            num_scalar_prefetch=1, grid=(R // tile,),
            in_specs=[pl.BlockSpec((None, tile, C), lambda i, chip_ref: (chip_ref[0], i, 0)),
                      pl.BlockSpec((n, tile, C), lambda i, chip_ref: (0, i, 0))],
            out_specs=pl.BlockSpec((tile, C), lambda i, chip_ref: (i, 0))),
        out_shape=jax.ShapeDtypeStruct((R, C), F32),
        compiler_params=_cp("parallel"),
    )(jnp.reshape(chip, (1,)).astype(jnp.int32), src, land)


def sum_slabs(t, name, tile=512):
    n, R, C = t.shape
    tile = _tile(R, tile, 16)

    def body(t_ref, o_ref):
        acc = t_ref[0].astype(F32)
        for s in range(1, n):
            acc = acc + t_ref[s].astype(F32)
        o_ref[...] = acc

    return pl.pallas_call(
        body, name=name, grid=(R // tile,), in_specs=[pl.BlockSpec((n, tile, C), lambda i: (0, i, 0))],
        out_specs=pl.BlockSpec((tile, C), lambda i: (i, 0)), out_shape=jax.ShapeDtypeStruct((R, C), F32),
        compiler_params=_cp("parallel"),
    )(t)


def _pad_rows(a, mult=16):
    r = (-a.shape[0]) % mult
    return jnp.pad(a, ((0, r), (0, 0))) if r else a


BIG = ["ab_w_in", "c_w_qkv", "xa_w_kv", "f_w_gate", "f_w_up", "ab_w_out", "c_w_out", "xa_w_q", "xa_w_out",
       "f_w_down", "s5_w_glu"]


GROUPS = ("B", "A")
GROUP_ROW_MULT = 64


def group_spec(layer, grp):
    i = layer // 2
    if grp == "A":
        return [("xa_w_kv", layer, True, 512), ("xa_w_q", layer, False, 256), ("xa_w_out", layer, False, 256),
                ("f_w_gate", layer, True, 704), ("f_w_up", layer, True, 704), ("f_w_down", layer, False, 704)]
    if layer % 2 == 0:
        return [("ab_w_in", i, True, 642), ("ab_w_out", i, False, 256), ("s5_w_glu", i, False, 64)]
    return [("c_w_qkv", i, True, 768), ("c_w_out", i, False, 256)]


def _seg_rows(rows):
    return rows + ((-rows) % 16)


def _f32_rows(a):
    bits = lax.bitcast_convert_type(a.reshape(-1), BF16).reshape(-1)
    return jnp.pad(bits, (0, 16 * D - bits.shape[0])).reshape(16, D)


def pack_group_shards(p, layer, grp):
    segs = []
    for name, idx, transposed, rows in group_spec(layer, grp):
        w = p[name][idx]
        if transposed:
            w = w.T
        segs.append(_pad_rows(w.astype(BF16).reshape(-1, D)))
    if grp == "B":
        small = p["gdn_conv_w"] if layer % 2 == 0 else p["c_norm_g"]
        segs.append(_f32_rows(small[layer // 2]))
    return _pad_rows(jnp.concatenate(segs, axis=0), GROUP_ROW_MULT)


def unpack_group_gathered(g, layer, grp):
    out, off = {}, 0
    for name, idx, transposed, rows in group_spec(layer, grp):
        seg = g[:, off:off + rows]
        if name == "s5_w_glu":
            out[name] = seg.reshape(N_CHIPS * 128, 512)
        else:
            out[name] = seg.reshape(N_CHIPS * rows, D)
        off += _seg_rows(rows)
    if grp == "B":
        n = 4 * 384 if layer % 2 == 0 else 256
        bits = g[:, off:off + 16].reshape(N_CHIPS, -1)[:, :2 * n].reshape(N_CHIPS, n, 2)
        small = lax.bitcast_convert_type(bits, F32)
        if layer % 2 == 0:
            out["gdn_conv_w"] = jnp.swapaxes(small.reshape(N_CHIPS, 4, 384), 0, 1).reshape(4, 1536)
        else:
            out["c_norm_g"] = small.reshape(D)
    return out


def pack_group_grads(gr, layer, grp):
    segs = []
    for name, idx, transposed, rows in group_spec(layer, grp):
        w = gr[name].astype(BF16)
        seg = w.reshape(N_CHIPS, rows, D)
        r = (-rows) % 16
        if r:
            seg = jnp.pad(seg, ((0, 0), (0, r), (0, 0)))
        segs.append(seg)
    out = jnp.concatenate(segs, axis=1)
    return jnp.pad(out, ((0, 0), (0, (-out.shape[1]) % GROUP_ROW_MULT), (0, 0)))


def unpack_group_reduced(g, layer, grp):
    out, off = {}, 0
    for name, idx, transposed, rows in group_spec(layer, grp):
        seg = g[off:off + rows]
        if name == "s5_w_glu":
            out[name] = seg.reshape(128, 512)
        else:
            out[name] = seg.T if (transposed and name not in ADAM_TRANSPOSED) else seg
        off += _seg_rows(rows)
    return out


ADAM_TRANSPOSED = ("f_w_gate", "f_w_up")


SMALL = ["ab_norm_g", "s5_a_re", "s5_a_im", "s5_log_dt", "s5_b_re", "s5_b_im", "s5_c_re", "s5_c_im", "s5_d",
         "s5_b_glu", "gdn_conv_w", "gdn_a_log", "gdn_dt_bias", "gdn_out_norm_g", "c_norm_g", "c_q_norm_g",
         "c_k_norm_g", "c_rel_bias", "mem_norm_g", "xa_norm_g", "xa_q_norm_g", "xa_k_norm_g", "f_norm_g"]


def _lane_rows(a):
    flat = a.reshape(-1).astype(F32)
    return jnp.pad(flat, (0, (-flat.shape[0]) % 1024)).reshape(-1, 128)


def pack_small(d, extra=None):
    parts = [_lane_rows(d[n]) for n in SMALL]
    if extra is not None:
        parts.append(_lane_rows(extra))
    rows = jnp.concatenate(parts, axis=0)
    return jnp.pad(rows, ((0, (-rows.shape[0]) % 128), (0, 0)))


def unpack_small(rows, shapes):
    out, off = {}, 0
    for n in SMALL:
        sz = int(np.prod(shapes[n]))
        k = 8 * -(-sz // 1024)
        out[n] = rows[off:off + k].reshape(-1)[:sz].reshape(shapes[n])
        off += k
    return out, rows[off:]


def _s5_blockdiag_b(b):
    bt = jnp.swapaxes(b, 1, 2).reshape(S5_GB, 8, S5_C, S5_P)
    eye = jnp.eye(8, dtype=b.dtype)
    return jnp.einsum("bgcp,gh->bgchp", bt, eye).reshape(S5_GB, 8 * S5_C, 8 * S5_P)


def _s5_blockdiag_c(c):
    ct = jnp.swapaxes(c, 1, 2).reshape(S5_GB, 8, S5_P, S5_C)
    eye = jnp.eye(8, dtype=c.dtype)
    return jnp.einsum("bgpc,gh->bgphc", ct, eye).reshape(S5_GB, 8 * S5_P, 8 * S5_C)


def _s5_diag_b(db):
    t = db.reshape(S5_GB, 8, S5_C, 8, S5_P)
    t = jnp.transpose(t, (0, 2, 4, 1, 3)).reshape(S5_GB, S5_C, S5_P, 64)
    d = t[..., ::9]
    return jnp.transpose(d, (0, 3, 2, 1)).reshape(S5_G, S5_P, S5_C)


def _s5_diag_c(dc):
    t = dc.reshape(S5_GB, 8, S5_P, 8, S5_C)
    t = jnp.transpose(t, (0, 2, 4, 1, 3)).reshape(S5_GB, S5_P, S5_C, 64)
    d = t[..., ::9]
    return jnp.transpose(d, (0, 3, 2, 1)).reshape(S5_G, S5_C, S5_P)


def local_step(x, mem, target, p, wsrc, gsink, wpre=lambda layer, grp, carry: carry):
    S = x.shape[0]
    row2 = lambda a: a.reshape(1, -1)
    saved = []
    (mem_n,) = row_fwd("mem_norm", f_norm, [mem], [row2(p["mem_norm_g"])], [(D, BF16)])
    gs = {}

    for layer in range(DEPTH):
        i = layer // 2
        w = wsrc(layer, "B", x)
        sv = {"x0": x, "wB": w}
        if layer % 2 == 0:
            (h,) = row_fwd("norm", f_norm, [x], [row2(p["ab_norm_g"][i])], [(D, BF16)])
            w_in = w["ab_w_in"]
            pm = matmul("nt", h, w_in[:2560], "proj_main")
            abt = matmul("nt", w_in[2560:], h, "proj_ab")
            s5p = dict(
                are=p["s5_a_re"][i].reshape(1, -1), aim=p["s5_a_im"][i].reshape(1, -1),
                ldt=jnp.broadcast_to(p["s5_log_dt"][i][:, None], (S5_G, S5_P)).reshape(1, -1),
                b_re=_s5_blockdiag_b(p["s5_b_re"][i]), b_im=_s5_blockdiag_b(p["s5_b_im"][i]),
                c_re=_s5_blockdiag_c(p["s5_c_re"][i]), c_im=_s5_blockdiag_c(p["s5_c_im"][i]),
                dv=p["s5_d"][i].reshape(1, -1))
            y5, st5 = s5_fwd(pm, **s5p)
            (a_out,) = row_fwd("glu", f_glu, [y5], [w["s5_w_glu"], row2(p["s5_b_glu"][i])], [(S5_W, F32)])
            conv_w = w["gdn_conv_w"]
            qkvc = conv_fwd(pm, conv_w)
            alog = p["gdn_a_log"][i].reshape(GDN_H, 1, 1)
            dtb = p["gdn_dt_bias"][i].reshape(GDN_H, 1, 1)
            og = row2(p["gdn_out_norm_g"][i])
            b_out, stg, inv = gdn_fwd(qkvc, pm, abt, alog, dtb, og)
            cat = wpre(layer, "A", jnp.concatenate([a_out, b_out], axis=1))
            x, hq = matmul("nn", cat, w["ab_w_out"], "mix_out", add=x, norm_out=row2(p["xa_norm_g"][layer]))
            sv.update(h=h, pm=pm, s5p=s5p, y5=y5, st5=st5, qkvc=qkvc, abt=abt, alog=alog, dtb=dtb, og=og,
                      stg=stg, inv=inv, cat=cat, conv_w=conv_w)
        else:
            (h,) = row_fwd("norm", f_norm, [x], [row2(w["c_norm_g"])], [(D, BF16)])
            qkv = matmul("nt", h, w["c_w_qkv"], "proj_qkv")
            qg = jnp.tile(row2(p["c_q_norm_g"][i]), (1, CA_H))
            kg = jnp.tile(row2(p["c_k_norm_g"][i]), (1, CA_H))
            (qn,) = row_fwd("headnorm_q", f_headnorm, [(qkv, D, 0)], [qg], [(D, F32)])
            kp, vp = kv_prep(qkv, kg)
            bias = rel_bias_expand(p["c_rel_bias"][i]).reshape(CA_H * CHUNK, CA_BAND)
            o = wpre(layer, "A", cattn_fwd(qn, kp, vp, bias))
            x, hq = matmul("nn", o, w["c_w_out"], "mix_out", add=x, norm_out=row2(p["xa_norm_g"][layer]))
            sv.update(h=h, qkv=qkv, qg=qg, kg=kg, qn=qn, kp=kp, vp=vp, bias=bias, o=o)
        sv["x1"] = x
        w = wsrc(layer, "A", x)
        sv["wA"] = w
        qx = matmul("nn", hq, w["xa_w_q"], "xa_q")
        kv = matmul("nt", mem_n, w["xa_w_kv"], "xa_kv")
        xqg, xkg = row2(p["xa_q_norm_g"][layer]), row2(p["xa_k_norm_g"][layer])
        (ox,) = row_fwd("xattn", f_xattn, [qx], [kv, xqg, xkg], [(D, BF16)], tile=2 * ROW_TILE)
        x, hf = matmul("nn", ox, w["xa_w_out"], "xa_out", add=x, norm_out=row2(p["f_norm_g"][layer]))
        sv.update(hq=hq, qx=qx, kv=kv, ox=ox)
        sv["x2"] = x
        gate, up, act = ffn_in(hf, w["f_w_gate"], w["f_w_up"])
        if layer + 1 < DEPTH:
            act = wpre(layer + 1, "B", act)
        x = matmul("nn", act, w["f_w_down"], "ffn_down", add=x)
        sv.update(hf=hf, gate=gate, up=up, act=act)
        saved.append(sv)

    dx, loss_vec = loss_head(x, target)

    dmem_n = None
    for layer in reversed(range(DEPTH)):
        i = layer // 2
        sv = saved[layer]
        w, gw = sv["wA"], {}
        dgate, dup = ffn_dact(dx, w["f_w_down"], sv["gate"], sv["up"])
        gw["f_w_down"] = matmul("tn", sv["act"], dx, "ffn_dwd", out_dtype=BF16)
        gw["f_w_gate"] = matmul("tn", dgate, sv["hf"], "ffn_dwg", out_dtype=BF16)
        gw["f_w_up"] = matmul("tn", dup, sv["hf"], "ffn_dwu", out_dtype=BF16)
        dh = matmul("nn", dgate, w["f_w_gate"], "ffn_dhg")
        dx, dg = matmul("nn", dup, w["f_w_up"], "ffn_dhu", add=dh,
                        norm_bwd=(sv["x2"], row2(p["f_norm_g"][layer]), dx))
        gs.setdefault("f_norm_g", [None] * DEPTH)[layer] = dg[0]
        do = matmul("nt", dx, w["xa_w_out"], "xa_do")
        gw["xa_w_out"] = matmul("tn", sv["ox"], dx, "xa_dwo", out_dtype=BF16)
        xqg, xkg = row2(p["xa_q_norm_g"][layer]), row2(p["xa_k_norm_g"][layer])
        dqx, dkv, dqg, dkg = row_bwd("xattn_bwd", f_xattn, [sv["qx"]], [sv["kv"], xqg, xkg], [do],
                                     [0], [0, 1, 2], tile=2 * ROW_TILE)
        gs.setdefault("xa_q_norm_g", [None] * DEPTH)[layer] = dqg[0]
        gs.setdefault("xa_k_norm_g", [None] * DEPTH)[layer] = dkg[0]
        gw["xa_w_q"] = matmul("tn", sv["hq"], dqx, "xa_dwq", out_dtype=BF16)
        gw["xa_w_kv"] = matmul("tn", dkv, mem_n, "xa_dwkv", out_dtype=BF16)
        dx = gsink(layer, "A", gw, dx)
        dmem_n = matmul("nn", dkv, w["xa_w_kv"], "xa_dmem", add=dmem_n)
        dx, dg = matmul("nt", dqx, w["xa_w_q"], "xa_dhq", norm_bwd=(sv["x1"], row2(p["xa_norm_g"][layer]), dx))
        gs.setdefault("xa_norm_g", [None] * DEPTH)[layer] = dg[0]
        w, gw = sv["wB"], {}
        if layer % 2 == 0:
            dcat = matmul("nt", dx, w["ab_w_out"], "mix_dcat")
            gw["ab_w_out"] = matmul("tn", sv["cat"], dx, "mix_dwo", out_dtype=BF16)
            dy5, dwglu, dbglu = row_bwd("glu_bwd", f_glu, [sv["y5"]], [w["s5_w_glu"], row2(p["s5_b_glu"][i])],
                                        [(dcat, S5_W, 0)], [0], [0, 1])
            gw["s5_w_glu"] = dwglu
            gs.setdefault("s5_b_glu", [None] * 2)[i] = dbglu[0]
            s5p = sv["s5p"]
            du, dare, daim, dldt, dbre, dbim, dcre, dcim, ddv = s5_bwd(sv["pm"], sv["st5"], dy5, **s5p)
            (dldt_g,) = whole("s5_dt_sum", lambda t: (jnp.sum(t, axis=1, keepdims=True),),
                              [dldt.reshape(S5_G, S5_P)], [((S5_G, 1), F32)])
            for nme, val in (("s5_a_re", dare.reshape(S5_G, S5_P)), ("s5_a_im", daim.reshape(S5_G, S5_P)),
                             ("s5_log_dt", dldt_g[:, 0]), ("s5_b_re", _s5_diag_b(dbre)),
                             ("s5_b_im", _s5_diag_b(dbim)), ("s5_c_re", _s5_diag_c(dcre)),
                             ("s5_c_im", _s5_diag_c(dcim)), ("s5_d", ddv.reshape(S5_G, S5_C))):
                gs.setdefault(nme, [None] * 2)[i] = val
            dqkvc, dgate, dal, dbl, dalog, ddtb, dog = gdn_bwd(
                sv["qkvc"], sv["pm"], sv["abt"], sv["alog"], sv["dtb"], sv["og"], sv["stg"], sv["inv"],
                dcat[:, S5_W:])
            (dog_s,) = whole("gdn_og_sum", lambda t: (jnp.sum(t, axis=0, keepdims=True),),
                             [dog.reshape(GDN_H, GDN_D)], [((1, GDN_D), F32)])
            gs.setdefault("gdn_out_norm_g", [None] * 2)[i] = dog_s[0]
            gs.setdefault("gdn_a_log", [None] * 2)[i] = dalog.reshape(GDN_H)
            gs.setdefault("gdn_dt_bias", [None] * 2)[i] = ddtb.reshape(GDN_H)
            dqkv, dconv = conv_bwd(sv["pm"], sv["conv_w"], dqkvc)
            gs.setdefault("gdn_conv_w", [None] * 2)[i] = dconv
            dpm = jnp.concatenate([du, dqkv, dgate], axis=1).astype(BF16)
            dabt = jnp.concatenate([dal, dbl], axis=0)
            dw_main = matmul("tn", dpm, sv["h"], "proj_dw", out_dtype=BF16)
            dw_ab = matmul("nn", dabt, sv["h"], "proj_ab_dw", out_dtype=BF16)
            gw["ab_w_in"] = jnp.concatenate([dw_main, dw_ab], axis=0)
            dx = gsink(layer, "B", gw, dx)
            w_in = w["ab_w_in"]
            dh = matmul("nn", dpm, w_in[:2560], "proj_dh")
            dx, dg = matmul("tn", dabt, w_in[2560:], "proj_ab_dh", add=dh,
                            norm_bwd=(sv["x0"], row2(p["ab_norm_g"][i]), dx))
            gs.setdefault("ab_norm_g", [None] * 2)[i] = dg[0]
        else:
            do = matmul("nt", dx, w["c_w_out"], "mix_dcat")
            gw["c_w_out"] = matmul("tn", sv["o"], dx, "mix_dwo", out_dtype=BF16)
            dqn, dkp, dvp, dbias = cattn_bwd(sv["qn"], sv["kp"], sv["vp"], sv["bias"], do)
            gs.setdefault("c_rel_bias", [None] * 2)[i] = rel_bias_grad(dbias.reshape(CA_H, CHUNK, CA_BAND))
            dq, dqg = row_bwd("headnorm_bwd", f_headnorm, [(sv["qkv"], D, 0)], [sv["qg"]], [dqn], [0], [0])
            dk, dkg = row_bwd("headnorm_bwd", f_headnorm, [(sv["qkv"], D, 1)], [sv["kg"]],
                              [(dkp, D, 0, CA_PAD)], [0], [0])
            head_sum = lambda t: (jnp.sum(t, axis=0, keepdims=True),)
            (dqg,) = whole("headgain_sum", head_sum, [dqg.reshape(CA_H, CA_D)], [((1, CA_D), F32)])
            (dkg,) = whole("headgain_sum", head_sum, [dkg.reshape(CA_H, CA_D)], [((1, CA_D), F32)])
            gs.setdefault("c_q_norm_g", [None] * 2)[i] = dqg[0]
            gs.setdefault("c_k_norm_g", [None] * 2)[i] = dkg[0]
            dqkv = jnp.concatenate([dq, dk, dvp[CA_PAD:]], axis=1).astype(BF16)
            gw["c_w_qkv"] = matmul("tn", dqkv, sv["h"], "proj_qkv_dw", out_dtype=BF16)
            dx = gsink(layer, "B", gw, dx)
            dx, dg = matmul("nn", dqkv, w["c_w_qkv"], "proj_qkv_dh",
                            norm_bwd=(sv["x0"], row2(w["c_norm_g"]), dx))
            gs.setdefault("c_norm_g", [None] * 2)[i] = dg[0]
    (dmg,) = row_bwd("mem_norm_bwd", f_norm, [mem], [row2(p["mem_norm_g"])], [dmem_n], [], [0])
    small = {n: jnp.stack(v) for n, v in gs.items()}
    small["mem_norm_g"] = dmg[0]
    return loss_vec, dx, small


ADAM_BLOCK_BYTES = 3 << 19


def adam(w, g, m, v, name):
    shape = w.shape
    if w.ndim == 3:
        d0, n, d2 = shape
        fits = [t for t in range(8, n + 1, 8) if n % t == 0 and d0 * t * d2 * 4 <= ADAM_BLOCK_BYTES]
        return tuple(row_fwd(name, f_adam, [w, g, m, v], [], [((d0, d2), F32)] * 3, tile=max(fits)))
    cols = shape[-1]
    w2, g2, m2, v2 = (t.reshape(-1, cols) for t in (w, g, m, v))
    rows = w2.shape[0]
    tile = rows if rows <= 512 else _tile(rows, 512, 8)
    outs = row_fwd(name, f_adam, [w2, g2, m2, v2], [], [(cols, F32)] * 3, tile=tile)
    return tuple(o.reshape(shape) for o in outs)


WEIGHTS = ['ab_norm_g', 'ab_w_in', 'ab_w_out', 's5_a_re', 's5_a_im', 's5_log_dt', 's5_b_re', 's5_b_im', 's5_c_re',
           's5_c_im', 's5_d', 's5_w_glu', 's5_b_glu', 'gdn_conv_w', 'gdn_a_log', 'gdn_dt_bias', 'gdn_out_norm_g',
           'c_norm_g', 'c_w_qkv', 'c_w_out', 'c_q_norm_g', 'c_k_norm_g', 'c_rel_bias', 'mem_norm_g', 'xa_norm_g',
           'xa_w_q', 'xa_w_kv', 'xa_w_out', 'xa_q_norm_g', 'xa_k_norm_g', 'f_norm_g', 'f_w_gate', 'f_w_up',
           'f_w_down']
SHARDED_SMALL = {"gdn_conv_w": (2, 384), "c_norm_g": (1, 256)}


def kernel(x, mem, ab_norm_g, ab_w_in, ab_w_out, s5_a_re, s5_a_im, s5_log_dt, s5_b_re, s5_b_im, s5_c_re, s5_c_im, s5_d, s5_w_glu, s5_b_glu, gdn_conv_w, gdn_a_log, gdn_dt_bias, gdn_out_norm_g, c_norm_g, c_w_qkv, c_w_out, c_q_norm_g, c_k_norm_g, c_rel_bias, mem_norm_g, xa_norm_g, xa_w_q, xa_w_kv, xa_w_out, xa_q_norm_g, xa_k_norm_g, f_norm_g, f_w_gate, f_w_up, f_w_down, loss_target, m_ab_norm_g, m_ab_w_in, m_ab_w_out, m_s5_a_re, m_s5_a_im, m_s5_log_dt, m_s5_b_re, m_s5_b_im, m_s5_c_re, m_s5_c_im, m_s5_d, m_s5_w_glu, m_s5_b_glu, m_gdn_conv_w, m_gdn_a_log, m_gdn_dt_bias, m_gdn_out_norm_g, m_c_norm_g, m_c_w_qkv, m_c_w_out, m_c_q_norm_g, m_c_k_norm_g, m_c_rel_bias, m_mem_norm_g, m_xa_norm_g, m_xa_w_q, m_xa_w_kv, m_xa_w_out, m_xa_q_norm_g, m_xa_k_norm_g, m_f_norm_g, m_f_w_gate, m_f_w_up, m_f_w_down, v_ab_norm_g, v_ab_w_in, v_ab_w_out, v_s5_a_re, v_s5_a_im, v_s5_log_dt, v_s5_b_re, v_s5_b_im, v_s5_c_re, v_s5_c_im, v_s5_d, v_s5_w_glu, v_s5_b_glu, v_gdn_conv_w, v_gdn_a_log, v_gdn_dt_bias, v_gdn_out_norm_g, v_c_norm_g, v_c_w_qkv, v_c_w_out, v_c_q_norm_g, v_c_k_norm_g, v_c_rel_bias, v_mem_norm_g, v_xa_norm_g, v_xa_w_q, v_xa_w_kv, v_xa_w_out, v_xa_q_norm_g, v_xa_k_norm_g, v_f_norm_g, v_f_w_gate, v_f_w_up, v_f_w_down):
    args = locals()
    p = {n: args[n] for n in WEIGHTS}
    m = {n: args["m_" + n] for n in WEIGHTS}
    v = {n: args["v_" + n] for n in WEIGHTS}
    chip = 2 * lax.axis_index("x") + lax.axis_index("y")

    carry = x[0]
    gathers = {}
    for layer in range(DEPTH):
        for grp in GROUPS:
            src = pack_group_shards(p, layer, grp)
            land = lax.dynamic_update_slice(lax.empty((N_CHIPS,) + src.shape, BF16), src[None], (chip, 0, 0))
            send_sems, recv_sems, src, land, carry = exchange_start(
                src, land, carry, f"gather_start_{layer}{grp}", mode="gather_half")
            gathers[layer, grp] = (src, land, send_sems, recv_sems)
    forwards = {}

    def wpre(layer, grp, carry):
        if (layer, grp) not in forwards:
            src, land, send_sems, recv_sems = gathers[layer, grp]
            src, land = exchange_wait(src, land, send_sems, recv_sems, carry, f"gather_wait_{layer}{grp}",
                                      mode="gather_half")
            send_sems, recv_sems, src, land, carry = exchange_start(
                src, land, carry, f"forward_start_{layer}{grp}", mode="forward_half")
            forwards[layer, grp] = (src, land, send_sems, recv_sems)
        return carry

    def wsrc(layer, grp, after):
        wpre(layer, grp, after)
        src, land, send_sems, recv_sems = forwards[layer, grp]
        _, land = exchange_wait(src, land, send_sems, recv_sems, after, f"forward_wait_{layer}{grp}",
                                mode="forward_half")
        return unpack_group_gathered(land, layer, grp)

    scatters, siblings = [], []
    LAG = 2

    def finish(carry):
        layer, grp, src, land, send_sems, recv_sems = scatters[len(siblings)]
        src, land = exchange_wait(src, land, send_sems, recv_sems, carry, f"scatter_wait_{layer}{grp}", mode="scatter")
        part = sum_own_slabs(src, chip, land, "sum_chips")
        send_sems, recv_sems, part, other, carry = exchange_start(
            part, lax.empty(part.shape, F32), carry, f"sibling_start_{layer}{grp}", mode="sibling")
        siblings.append((layer, grp, part, other, send_sems, recv_sems))
        return carry

    def gsink(layer, grp, gw, carry):
        src = pack_group_grads(gw, layer, grp)
        land = lax.empty((3,) + src.shape[1:], BF16)
        send_sems, recv_sems, src, land, carry = exchange_start(
            src, land, carry, f"scatter_start_{layer}{grp}", mode="scatter")
        scatters.append((layer, grp, src, land, send_sems, recv_sems))
        if len(scatters) > LAG:
            carry = finish(carry)
        return carry

    carry = wpre(0, "B", carry)
    loss_vec, dx, g_small = local_step(carry, mem[0], loss_target[0], p, wsrc, gsink, wpre)

    full_shapes = {n: ((2, 4, 1536) if n == "gdn_conv_w" else (2, D) if n == "c_norm_g" else p[n].shape)
                   for n in SMALL}
    small_mine = pack_small(g_small, extra=loss_vec)
    (small_other,) = sibling_exchange([small_mine], "sibling_small")
    (small_chip,) = row_fwd("sum_small_cores", lambda a, b: (a + b,), [small_mine, small_other], [], [(128, F32)],
                            tile=128)
    small_land = lax.dynamic_update_slice(lax.empty((N_CHIPS,) + small_chip.shape, F32), small_chip[None], (chip, 0, 0))
    small_sems = exchange_start(small_chip, small_land, dx, "gather_small_start", mode="gather")
    dx = small_sems[4]

    while len(siblings) < len(scatters):
        dx = finish(dx)

    per_layer = {}
    for layer, grp, part, other, send_sems, recv_sems in siblings:
        part, other = exchange_wait(part, other, send_sems, recv_sems, dx, f"sibling_wait_{layer}{grp}",
                                    mode="sibling")
        (total,) = row_fwd("sum_cores", lambda a, b: (a + b,), [part, other], [], [(D, F32)],
                           tile=_tile(part.shape[0], 512, 16))
        for name, g in unpack_group_reduced(total, layer, grp).items():
            per_layer.setdefault(name, {})[layer] = g
    grads = {name: jnp.stack([d[k] for k in sorted(d)]) for name, d in per_layer.items()}

    send_sems, recv_sems, small_chip, small_land, _ = small_sems
    _, small_land = exchange_wait(small_chip, small_land, send_sems, recv_sems, dx, "gather_small_wait", mode="gather")
    g_s, rest = unpack_small(sum_slabs(small_land, "sum_small"), full_shapes)
    (loss11,) = whole("loss_sum", lambda t: (jnp.sum(jnp.sum(t, axis=1, keepdims=True), axis=0, keepdims=True),),
                      [rest[:D // 128]], [((1, 1), F32)])
    for n, (axis, width) in SHARDED_SMALL.items():
        g_s[n] = lax.dynamic_slice_in_dim(g_s[n], chip * width, width, axis=axis)
    grads.update(g_s)

    delta, new_m, new_v = {}, {}, {}
    for n in SMALL:
        shape = p[n].shape
        two = (1, shape[0]) if len(shape) == 1 else (int(np.prod(shape[:-1])), shape[-1])
        outs = whole("adam_" + n, f_adam, [t.reshape(two) for t in (p[n], g_s[n], m[n], v[n])], [(two, F32)] * 3)
        delta[n], new_m[n], new_v[n] = (o.reshape(shape) for o in outs)
    for name in BIG:
        if name in ADAM_TRANSPOSED:
            t = lambda a: jnp.swapaxes(a, 1, 2)
            outs = adam(t(p[name]), grads[name], t(m[name]), t(v[name]), "adam_" + name)
            delta[name], new_m[name], new_v[name] = (t(o) for o in outs)
            grads[name] = t(grads[name])
        else:
            delta[name], new_m[name], new_v[name] = adam(p[name], grads[name], m[name], v[name], "adam_" + name)

    return (loss11[0, 0], dx[None], *[grads[n] for n in WEIGHTS], *[delta[n] for n in WEIGHTS],
            *[new_m[n] for n in WEIGHTS], *[new_v[n] for n in WEIGHTS])
```

```python
import numpy as np
import jax
import jax.numpy as jnp
from jax import lax
from jax.experimental import pallas as pl
from jax.experimental.pallas import tpu as pltpu

F32 = jnp.float32
BF16 = jnp.bfloat16
MESH = pl.DeviceIdType.MESH

D = 1024
CHUNK = 64
EPS = 1e-6
S5_W = 512
S5_G = 32
S5_C = 16
S5_P = 64
S5_GB = 4
GDN_H = 4
GDN_D = 128
CA_H = 16
CA_D = 64
CA_LEFT = 8
CA_BAND = (CA_LEFT + 1) * CHUNK
CA_PAD = CA_LEFT * CHUNK
MAX_REL = 128
XA_H = 4
XA_D = 256
DEPTH = 4
N_CHIPS = 4
LR, B1, B2, AEPS, WD, STEP = 0.001, 0.9, 0.999, 1e-08, 0.01, 10

VMEM_LIMIT = 56 * 1024 * 1024
ROW_TILE = 512


def _cp(*sem):
    return pltpu.CompilerParams(dimension_semantics=sem, vmem_limit_bytes=VMEM_LIMIT)


def _dg(a, b, ca, cb):
    return lax.dot_general(a.astype(BF16), b.astype(BF16), (((ca,), (cb,)), ((), ())),
                           preferred_element_type=F32)


@jax.custom_vjp
def mm(a, b):
    return _dg(a, b, 1, 0)


@jax.custom_vjp
def mm_nt(a, b):
    return _dg(a, b, 1, 1)


@jax.custom_vjp
def mm_tn(a, b):
    return _dg(a, b, 0, 0)


mm.defvjp(lambda a, b: (mm(a, b), (a, b)), lambda r, g: (mm_nt(g, r[1]), mm_tn(r[0], g)))
mm_nt.defvjp(lambda a, b: (mm_nt(a, b), (a, b)), lambda r, g: (mm(g, r[1]), mm_tn(g, r[0])))
mm_tn.defvjp(lambda a, b: (mm_tn(a, b), (a, b)), lambda r, g: (mm_nt(r[1], g), mm(r[0], g)))


def _split(a):
    hi = a.astype(BF16)
    return hi, (a - hi.astype(F32)).astype(BF16)


def _tri_mm(v, upper):
    T = v.shape[0]
    r = lax.broadcasted_iota(jnp.int32, (T, T), 0)
    c = lax.broadcasted_iota(jnp.int32, (T, T), 1)
    m = ((c >= r) if upper else (r >= c)).astype(BF16)
    hi, lo = _split(v)
    d = lambda u: lax.dot_general(m, u, (((1,), (0,)), ((), ())), preferred_element_type=F32)
    return d(hi) + d(lo)


@jax.custom_vjp
def cumsum_rows(v):
    return _tri_mm(v, False)


cumsum_rows.defvjp(lambda v: (_tri_mm(v, False), None), lambda _, g: (_tri_mm(g, True),))


def _rms(x, g):
    return x * lax.rsqrt(jnp.mean(x * x, axis=-1, keepdims=True) + EPS) * g


def _softmax(s):
    e = jnp.exp(s - lax.stop_gradient(jnp.max(s, axis=-1, keepdims=True)))
    return e / jnp.sum(e, axis=-1, keepdims=True)


def _softplus(x):
    return jnp.maximum(x, 0.0) + jnp.log(1.0 + jnp.exp(-jnp.abs(x)))


def _tile(n, cap, align):
    if n <= cap:
        return n
    best = None
    for d in range(align, cap + 1, align):
        if n % d == 0:
            best = d
    assert best is not None, (n, cap, align)
    return best


def matmul(mode, a, b, name, out_dtype=F32, add=None, norm_out=None, norm_bwd=None):
    if mode == "nn":
        (M, K), (K2, N) = a.shape, b.shape
    elif mode == "nt":
        (M, K), (N, K2) = a.shape, b.shape
    else:
        (K, M), (K2, N) = a.shape, b.shape
    assert K == K2, (mode, a.shape, b.shape)
    if mode != "tn" and norm_out is None and norm_bwd is None:
        tm, tn, tk = _tile(M, 1024, 128), _tile(N, 1024, 128), _tile(K, 2048, 128)
    else:
        tm, tn, tk = _tile(M, 512, 128), _tile(N, 1536, 128), _tile(K, 2048, 128)
    nk = K // tk
    if mode == "nn":
        a_spec = pl.BlockSpec((tm, tk), lambda i, j, k: (i, k))
        b_spec = pl.BlockSpec((tk, tn), lambda i, j, k: (k, j))
        dn = (((1,), (0,)), ((), ()))
    elif mode == "nt":
        a_spec = pl.BlockSpec((tm, tk), lambda i, j, k: (i, k))
        b_spec = pl.BlockSpec((tn, tk), lambda i, j, k: (j, k))
        dn = (((1,), (1,)), ((), ()))
    else:
        a_spec = pl.BlockSpec((tk, tm), lambda i, j, k: (k, i))
        b_spec = pl.BlockSpec((tk, tn), lambda i, j, k: (k, j))
        dn = (((0,), (0,)), ((), ()))
    o_spec = pl.BlockSpec((tm, tn), lambda i, j, k: (i, j))
    has_add = add is not None
    g_spec = pl.BlockSpec((1, tn), lambda i, j, k: (0, j))
    extra, extra_specs, out_shapes, out_specs = [], [], [jax.ShapeDtypeStruct((M, N), out_dtype)], [o_spec]
    sem = ("parallel", "parallel", "arbitrary")
    if norm_out is not None:
        assert tn == N
        extra, extra_specs = [norm_out], [g_spec]
        out_shapes.append(jax.ShapeDtypeStruct((M, N), BF16))
        out_specs.append(o_spec)
    if norm_bwd is not None:
        assert tn == N
        x_in, g_in, res_in = norm_bwd
        extra, extra_specs = [x_in, g_in, res_in], [o_spec, g_spec, o_spec]
        out_shapes.append(jax.ShapeDtypeStruct((1, N), F32))
        out_specs.append(g_spec)
        sem = ("arbitrary", "arbitrary", "arbitrary")
    n_in = 2 + int(has_add) + len(extra)
    n_out = len(out_shapes)

    def body(*refs):
        a_ref, b_ref = refs[0], refs[1]
        add_ref = refs[2] if has_add else None
        extra_refs = refs[2 + int(has_add):n_in]
        o_ref = refs[n_in]
        acc_ref = refs[-1]
        i = pl.program_id(0)
        p = lax.dot_general(a_ref[...].astype(BF16), b_ref[...].astype(BF16), dn,
                            preferred_element_type=F32)

        def finish(total):
            if has_add:
                total = total + add_ref[...]
            if norm_out is not None:
                o_ref[...] = total.astype(o_ref.dtype)
                refs[n_in + 1][...] = _rms(total, extra_refs[0][...]).astype(BF16)
            elif norm_bwd is not None:
                x_ref, g_ref, res_ref = extra_refs
                _, vjp = jax.vjp(f_norm_res, x_ref[...], g_ref[...])
                dx, dg = vjp((total, res_ref[...]))
                o_ref[...] = dx
                dg_ref = refs[n_in + 1]

                @pl.when(i == 0)
                def _():
                    dg_ref[...] = dg

                @pl.when(i > 0)
                def _():
                    dg_ref[...] += dg
            else:
                o_ref[...] = total.astype(o_ref.dtype)

        if nk == 1:
            finish(p)
        else:
            k = pl.program_id(2)

            @pl.when(k == 0)
            def _():
                acc_ref[...] = p

            @pl.when(k > 0)
            def _():
                acc_ref[...] += p

            @pl.when(k == nk - 1)
            def _():
                finish(acc_ref[...])

    ins = [a, b] + ([add] if has_add else []) + extra
    specs = [a_spec, b_spec] + ([o_spec] if has_add else []) + extra_specs
    out = pl.pallas_call(
        body, name=name, grid=(M // tm, N // tn, nk), in_specs=specs, out_specs=out_specs,
        out_shape=out_shapes, scratch_shapes=[pltpu.VMEM((tm, tn), F32)],
        compiler_params=_cp(*sem),
    )(*ins)
    return out[0] if n_out == 1 else out


def ffn_in(h, wg, wu):
    (S, K), F = h.shape, wg.shape[0]
    tm, tn = _tile(S, 1024, 128), _tile(F, 1536, 128)
    dn = (((1,), (1,)), ((), ()))

    def body(h_ref, wg_ref, wu_ref, g_ref, u_ref, act_ref):
        a = h_ref[...].astype(BF16)
        g = lax.dot_general(a, wg_ref[...].astype(BF16), dn, preferred_element_type=F32)
        u = lax.dot_general(a, wu_ref[...].astype(BF16), dn, preferred_element_type=F32)
        g_ref[...] = g.astype(BF16)
        u_ref[...] = u.astype(BF16)
        act_ref[...] = f_swiglu(g, u)[0].astype(BF16)

    w_spec = pl.BlockSpec((tn, K), lambda i, j: (j, 0))
    o_spec = pl.BlockSpec((tm, tn), lambda i, j: (i, j))
    sd = jax.ShapeDtypeStruct
    return pl.pallas_call(
        body, name="ffn_in", grid=(S // tm, F // tn),
        in_specs=[pl.BlockSpec((tm, K), lambda i, j: (i, 0)), w_spec, w_spec], out_specs=[o_spec] * 3,
        out_shape=[sd((S, F), BF16), sd((S, F), BF16), sd((S, F), BF16)],
        compiler_params=_cp("parallel", "parallel"),
    )(h, wg, wu)


def ffn_dact(dy, wd, gate, up):
    (S, K), F = dy.shape, wd.shape[0]
    tm, tn = _tile(S, 1024, 128), _tile(F, 1536, 128)
    dn = (((1,), (1,)), ((), ()))

    def body(dy_ref, wd_ref, g_ref, u_ref, dg_ref, du_ref):
        dact = lax.dot_general(dy_ref[...].astype(BF16), wd_ref[...].astype(BF16), dn, preferred_element_type=F32)
        _, vjp = jax.vjp(f_swiglu, g_ref[...].astype(F32), u_ref[...].astype(F32))
        dg, du = vjp((dact,))
        dg_ref[...] = dg.astype(BF16)
        du_ref[...] = du.astype(BF16)

    o_spec = pl.BlockSpec((tm, tn), lambda i, j: (i, j))
    sd = jax.ShapeDtypeStruct
    return pl.pallas_call(
        body, name="ffn_dact", grid=(S // tm, F // tn),
        in_specs=[pl.BlockSpec((tm, K), lambda i, j: (i, 0)), pl.BlockSpec((tn, K), lambda i, j: (j, 0)),
                  o_spec, o_spec],
        out_specs=[o_spec] * 2, out_shape=[sd((S, F), BF16)] * 2,
        compiler_params=_cp("parallel", "parallel"),
    )(dy, wd, gate, up)


def _row_spec(arr, tile):
    if isinstance(arr, tuple):
        a, w, cb = arr[:3]
        ro = (arr[3] // tile) if len(arr) > 3 else 0
        assert len(arr) < 4 or arr[3] % tile == 0
        return a, pl.BlockSpec((tile, w), lambda i, cb=cb, ro=ro: (i + ro, cb)), (tile, w)
    if arr.ndim == 3:
        d0, _, d2 = arr.shape
        return arr, pl.BlockSpec((d0, tile, d2), lambda i: (0, i, 0)), (d0, tile, d2)
    return arr, pl.BlockSpec((tile, arr.shape[1]), lambda i: (i, 0)), (tile, arr.shape[1])


def _full_spec(arr):
    nd = arr.ndim
    return pl.BlockSpec(arr.shape, lambda i, nd=nd: (0,) * nd)


def _n_rows(arr):
    a = arr[0] if isinstance(arr, tuple) else arr
    return a.shape[1] if a.ndim == 3 else a.shape[0]


def _row_out_shape(shape_tail, n, dtype):
    if isinstance(shape_tail, tuple):
        d0, d2 = shape_tail
        return (jax.ShapeDtypeStruct((d0, n, d2), dtype),
                lambda tile: pl.BlockSpec((d0, tile, d2), lambda i: (0, i, 0)))
    return (jax.ShapeDtypeStruct((n, shape_tail), dtype),
            lambda tile: pl.BlockSpec((tile, shape_tail), lambda i: (i, 0)))


def _f32(v):
    return v.astype(F32) if v.dtype == BF16 else v


def row_fwd(name, fn, rows, fulls, outs, tile=ROW_TILE):
    n = _n_rows(rows[0])
    tile = min(tile, n)
    assert n % tile == 0, (name, n, tile)
    rs = [_row_spec(r, tile) for r in rows]
    os_ = [_row_out_shape(w, n, dt) for w, dt in outs]
    nr, nf = len(rows), len(fulls)

    def body(*refs):
        vals = [_f32(r[...]) for r in refs[:nr + nf]]
        res = fn(*vals)
        for r, v in zip(refs[nr + nf:], res):
            r[...] = v.astype(r.dtype)

    out = pl.pallas_call(
        body, name=name, grid=(n // tile,),
        in_specs=[s for _, s, _ in rs] + [_full_spec(f) for f in fulls],
        out_specs=[mk(tile) for _, mk in os_], out_shape=[sh for sh, _ in os_],
        compiler_params=_cp("parallel"),
    )(*[a for a, _, _ in rs], *fulls)
    return out


def row_bwd(name, fn, rows, fulls, cts, want_rows, want_fulls, row_dtypes=None, tile=ROW_TILE):
    n = _n_rows(rows[0])
    tile = min(tile, n)
    assert n % tile == 0, (name, n, tile)
    rs = [_row_spec(r, tile) for r in rows]
    cs = [_row_spec(c, tile) for c in cts]
    nr, nf, nc = len(rows), len(fulls), len(cts)
    row_dtypes = row_dtypes or [F32] * len(want_rows)
    out_shapes, out_specs = [], []
    for k, idx in enumerate(want_rows):
        a, _, blk = rs[idx]
        if len(blk) == 3:
            sh, mk = _row_out_shape((blk[0], blk[2]), n, row_dtypes[k])
        else:
            sh, mk = _row_out_shape(blk[1], n, row_dtypes[k])
        out_shapes.append(sh)
        out_specs.append(mk(tile))
    for idx in want_fulls:
        out_shapes.append(jax.ShapeDtypeStruct(fulls[idx].shape, F32))
        out_specs.append(_full_spec(fulls[idx]))
    n_wr = len(want_rows)

    def body(*refs):
        i = pl.program_id(0)
        vals = [_f32(r[...]) for r in refs[:nr + nf]]
        ct_vals = [_f32(r[...]) for r in refs[nr + nf:nr + nf + nc]]
        outs = refs[nr + nf + nc:]
        _, vjp = jax.vjp(fn, *vals)
        grads = vjp(tuple(ct_vals))
        for k, idx in enumerate(want_rows):
            outs[k][...] = grads[idx].astype(outs[k].dtype)
        for k, idx in enumerate(want_fulls):
            o = outs[n_wr + k]
            g = grads[nr + idx]

            @pl.when(i == 0)
            def _(o=o, g=g):
                o[...] = g

            @pl.when(i > 0)
            def _(o=o, g=g):
                o[...] += g

    out = pl.pallas_call(
        body, name=name, grid=(n // tile,),
        in_specs=[s for _, s, _ in rs] + [_full_spec(f) for f in fulls] + [s for _, s, _ in cs],
        out_specs=out_specs, out_shape=out_shapes,
        compiler_params=_cp("arbitrary"),
    )(*[a for a, _, _ in rs], *fulls, *[a for a, _, _ in cs])
    return out


def whole(name, fn, args, outs):
    def body(*refs):
        res = fn(*[r[...] for r in refs[:len(args)]])
        for r, v in zip(refs[len(args):], res):
            r[...] = v.astype(r.dtype)

    return pl.pallas_call(
        body, name=name, out_shape=[jax.ShapeDtypeStruct(s, d) for s, d in outs],
        compiler_params=pltpu.CompilerParams(vmem_limit_bytes=VMEM_LIMIT),
    )(*args)


def f_norm(x, g):
    return (_rms(x, g),)


def f_norm_res(x, g):
    return _rms(x, g), x


def f_swiglu(g, u):
    return (g * jax.nn.sigmoid(g) * u,)


def f_glu(y, w, b):
    h = jax.nn.gelu(y)
    return (h * jax.nn.sigmoid(mm(h, w) + b),)


def f_xattn(q, kv, qg, kg):
    outs = []
    for h in range(XA_H):
        sl = slice(h * XA_D, (h + 1) * XA_D)
        qn = _rms(q[:, sl], qg)
        kn = _rms(kv[:, sl], kg)
        vh = kv[:, D + h * XA_D:D + (h + 1) * XA_D]
        p = _softmax(mm_nt(qn, kn) * (XA_D ** -0.5))
        outs.append(mm(p, vh))
    return (jnp.concatenate(outs, axis=-1),)


def f_adam(w, g, m, v):
    m2 = B1 * m + (1.0 - B1) * g
    v2 = B2 * v + (1.0 - B2) * (g * g)
    m_hat = m2 / (1.0 - B1 ** STEP)
    v_hat = v2 / (1.0 - B2 ** STEP)
    delta = -LR * (m_hat / (jnp.sqrt(v_hat) + AEPS) + WD * w)
    return delta, m2, v2


S5_NTAB, S5_NROW = 4, 6


def _s5_tables(are, aim, ldt):
    T = CHUNK
    dt = jnp.exp(ldt)
    ar, ai = are * dt, aim * dt
    t = lax.broadcasted_iota(jnp.int32, (T, 1), 0).astype(F32)
    mag, inv = jnp.exp(t * ar), jnp.exp(-t * ar)
    cs, sn = jnp.cos(t * ai), jnp.sin(t * ai)
    e_re, e_im = mag * cs, mag * sn
    n_re, n_im = inv * cs, -inv * sn
    l_re, l_im = jnp.exp(ar) * jnp.cos(ai), jnp.exp(ar) * jnp.sin(ai)
    den = are * are + aim * aim
    k_re = ((l_re - 1.0) * are + l_im * aim) / den
    k_im = (l_im * are - (l_re - 1.0) * aim) / den
    tl = float(T - 1)
    m_re, m_im = jnp.exp(tl * ar) * jnp.cos(tl * ai), jnp.exp(tl * ar) * jnp.sin(tl * ai)
    return (e_re, e_im, n_re, n_im), (l_re, l_im, k_re, k_im, m_re, m_im)


def _s5_chunk(u, sre, sim, tabs, rows, b_re, b_im, c_re, c_im, dv):
    e_re, e_im, n_re, n_im = tabs
    l_re, l_im, k_re, k_im, m_re, m_im = rows
    x_re, x_im = mm(u, b_re), mm(u, b_im)
    bu_re = k_re * x_re - k_im * x_im
    bu_im = k_re * x_im + k_im * x_re
    v_re = bu_re * n_re - bu_im * n_im
    v_im = bu_re * n_im + bu_im * n_re
    p_re = l_re * sre - l_im * sim
    p_im = l_re * sim + l_im * sre
    w_re = cumsum_rows(v_re) + p_re
    w_im = cumsum_rows(v_im) + p_im
    s_re = e_re * w_re - e_im * w_im
    s_im = e_re * w_im + e_im * w_re
    y = mm(s_re, c_re) - mm(s_im, c_im) + dv * u
    z_re = jnp.sum(v_re, axis=0, keepdims=True) + p_re
    z_im = jnp.sum(v_im, axis=0, keepdims=True) + p_im
    return y, m_re * z_re - m_im * z_im, m_re * z_im + m_im * z_re


def _s5_fill_tables(are_ref, aim_ref, ldt_ref, tab, row):
    for g in range(S5_GB):
        ls = slice(512 * g, 512 * (g + 1))
        tabs, rows = _s5_tables(are_ref[:, ls], aim_ref[:, ls], ldt_ref[:, ls])
        for k, t in enumerate(tabs):
            tab[k, :, ls] = t
        for k, r in enumerate(rows):
            row[k:k + 1, ls] = r


def _s5_read_tables(tab, row, ls):
    return (tuple(tab[k, :, ls] for k in range(S5_NTAB)), tuple(row[k:k + 1, ls] for k in range(S5_NROW)))


def _s5_specs(nc, rev):
    T = CHUNK

    def ci(c):
        return nc - 1 - c if rev else c

    u_spec = pl.BlockSpec((T, S5_W), lambda c: (ci(c), 0))
    p_spec = pl.BlockSpec((1, S5_G * S5_P), lambda c: (0, 0))
    b_spec = pl.BlockSpec((S5_GB, 128, 512), lambda c: (0, 0, 0))
    c_spec = pl.BlockSpec((S5_GB, 512, 128), lambda c: (0, 0, 0))
    d_spec = pl.BlockSpec((1, S5_W), lambda c: (0, 0))
    st_spec = pl.BlockSpec((None, 2, S5_G * S5_P), lambda c: (ci(c), 0, 0))
    return u_spec, p_spec, b_spec, c_spec, d_spec, st_spec


def s5_fwd(pm, are, aim, ldt, b_re, b_im, c_re, c_im, dv):
    S = pm.shape[0]
    nc = S // CHUNK
    u_spec, p_spec, b_spec, c_spec, d_spec, st_spec = _s5_specs(nc, False)

    def body(u_ref, are_ref, aim_ref, ldt_ref, bre_ref, bim_ref, cre_ref, cim_ref, dv_ref,
             y_ref, st_ref, state, tab, row):
        c = pl.program_id(0)

        @pl.when(c == 0)
        def _():
            state[...] = jnp.zeros_like(state)
            _s5_fill_tables(are_ref, aim_ref, ldt_ref, tab, row)

        st_ref[...] = state[...]
        for g in range(S5_GB):
            lu, ls = slice(128 * g, 128 * (g + 1)), slice(512 * g, 512 * (g + 1))
            tabs, rows = _s5_read_tables(tab, row, ls)
            y, e_re, e_im = _s5_chunk(u_ref[:, lu], state[0:1, ls], state[1:2, ls], tabs, rows,
                                      bre_ref[g], bim_ref[g], cre_ref[g], cim_ref[g], dv_ref[:, lu])
            y_ref[:, lu] = y
            state[0:1, ls] = e_re
            state[1:2, ls] = e_im

    n_state = S5_G * S5_P
    return pl.pallas_call(
        body, name="s5_fwd", grid=(nc,),
        in_specs=[u_spec, p_spec, p_spec, p_spec, b_spec, b_spec, c_spec, c_spec, d_spec],
        out_specs=[u_spec, st_spec],
        out_shape=[jax.ShapeDtypeStruct((S, S5_W), F32), jax.ShapeDtypeStruct((nc, 2, n_state), F32)],
        scratch_shapes=[pltpu.VMEM((2, n_state), F32), pltpu.VMEM((S5_NTAB, CHUNK, n_state), F32),
                        pltpu.VMEM((8, n_state), F32)],
        compiler_params=_cp("arbitrary"),
    )(pm, are, aim, ldt, b_re, b_im, c_re, c_im, dv)


def s5_bwd(pm, st, dy, are, aim, ldt, b_re, b_im, c_re, c_im, dv):
    S = pm.shape[0]
    nc = S // CHUNK
    u_spec, p_spec, b_spec, c_spec, d_spec, st_spec = _s5_specs(nc, True)

    def body(u_ref, st_ref, dy_ref, are_ref, aim_ref, ldt_ref, bre_ref, bim_ref, cre_ref, cim_ref, dv_ref,
             du_ref, dare_ref, daim_ref, dldt_ref, dbre_ref, dbim_ref, dcre_ref, dcim_ref, ddv_ref,
             dstate, tab, row, dtab, drow):
        c = pl.program_id(0)

        @pl.when(c == 0)
        def _():
            dstate[...] = jnp.zeros_like(dstate)
            dtab[...] = jnp.zeros_like(dtab)
            drow[...] = jnp.zeros_like(drow)
            _s5_fill_tables(are_ref, aim_ref, ldt_ref, tab, row)

        for g in range(S5_GB):
            lu, ls = slice(128 * g, 128 * (g + 1)), slice(512 * g, 512 * (g + 1))
            every = slice(None)
            tabs, rows = _s5_read_tables(tab, row, ls)
            args = (u_ref[:, lu], st_ref[0:1, ls], st_ref[1:2, ls], tabs, rows,
                    bre_ref[g], bim_ref[g], cre_ref[g], cim_ref[g], dv_ref[:, lu])
            _, vjp = jax.vjp(_s5_chunk, *args)
            gr = vjp((dy_ref[:, lu], dstate[0:1, ls], dstate[1:2, ls]))
            du_ref[:, lu] = gr[0]
            dstate[0:1, ls] = gr[1]
            dstate[1:2, ls] = gr[2]
            for k, t in enumerate(gr[3]):
                dtab[k, :, ls] += t
            for k, r in enumerate(gr[4]):
                drow[k:k + 1, ls] += r
            accs = ((dbre_ref, (g,)), (dbim_ref, (g,)), (dcre_ref, (g,)), (dcim_ref, (g,)), (ddv_ref, (every, lu)))
            for (o, idx), gv in zip(accs, gr[5:]):
                @pl.when(c == 0)
                def _(o=o, idx=idx, gv=gv):
                    o[idx] = gv

                @pl.when(c > 0)
                def _(o=o, idx=idx, gv=gv):
                    o[idx] += gv

        @pl.when(c == nc - 1)
        def _():
            for g in range(S5_GB):
                ls = slice(512 * g, 512 * (g + 1))
                _, vjp = jax.vjp(_s5_tables, are_ref[:, ls], aim_ref[:, ls], ldt_ref[:, ls])
                dtabs, drows = _s5_read_tables(dtab, drow, ls)
                ga, gi, gl = vjp((dtabs, drows))
                dare_ref[:, ls] = ga
                daim_ref[:, ls] = gi
                dldt_ref[:, ls] = gl

    n_state = S5_G * S5_P
    return pl.pallas_call(
        body, name="s5_bwd", grid=(nc,),
        in_specs=[u_spec, st_spec, u_spec, p_spec, p_spec, p_spec, b_spec, b_spec, c_spec, c_spec, d_spec],
        out_specs=[u_spec, p_spec, p_spec, p_spec, b_spec, b_spec, c_spec, c_spec, d_spec],
        out_shape=[jax.ShapeDtypeStruct((S, S5_W), F32)] + [jax.ShapeDtypeStruct((1, n_state), F32)] * 3
        + [jax.ShapeDtypeStruct((S5_GB, 128, 512), F32)] * 2 + [jax.ShapeDtypeStruct((S5_GB, 512, 128), F32)] * 2
        + [jax.ShapeDtypeStruct((1, S5_W), F32)],
        scratch_shapes=[pltpu.VMEM((2, n_state), F32), pltpu.VMEM((S5_NTAB, CHUNK, n_state), F32),
                        pltpu.VMEM((8, n_state), F32), pltpu.VMEM((S5_NTAB, CHUNK, n_state), F32),
                        pltpu.VMEM((8, n_state), F32)],
        compiler_params=_cp("arbitrary"),
    )(pm, st, dy, are, aim, ldt, b_re, b_im, c_re, c_im, dv)


def conv_fwd(pm, w):
    S = pm.shape[0]

    def body(x_ref, w_ref, o_ref, pad):
        x = x_ref[...]
        pad[0:8, :] = jnp.zeros((8, 128), F32)
        pad[8:, :] = x
        y = (w_ref[3:4, :] * x + w_ref[2:3, :] * pad[7:7 + S, :] + w_ref[1:2, :] * pad[6:6 + S, :]
             + w_ref[0:1, :] * pad[5:5 + S, :])
        o_ref[...] = y * jax.nn.sigmoid(y)

    return pl.pallas_call(
        body, name="conv_fwd", grid=(12,),
        in_specs=[pl.BlockSpec((S, 128), lambda j: (0, 4 + j)), pl.BlockSpec((4, 128), lambda j: (0, j))],
        out_specs=pl.BlockSpec((S, 128), lambda j: (0, j)),
        out_shape=jax.ShapeDtypeStruct((S, 1536), F32),
        scratch_shapes=[pltpu.VMEM((S + 8, 128), F32)],
        compiler_params=_cp("parallel"),
    )(pm, w)


def conv_bwd(pm, w, dout):
    S = pm.shape[0]

    def body(x_ref, w_ref, do_ref, dx_ref, dw_ref, pad, dpad):
        x = x_ref[...]
        pad[0:8, :] = jnp.zeros((8, 128), F32)
        pad[8:, :] = x
        xs = [pad[5:5 + S, :], pad[6:6 + S, :], pad[7:7 + S, :], x]
        y = w_ref[0:1, :] * xs[0] + w_ref[1:2, :] * xs[1] + w_ref[2:3, :] * xs[2] + w_ref[3:4, :] * xs[3]
        sg = jax.nn.sigmoid(y)
        dy = do_ref[...] * (sg + y * sg * (1.0 - sg))
        dpad[0:S, :] = dy
        dpad[S:, :] = jnp.zeros((8, 128), F32)
        dx_ref[...] = (w_ref[3:4, :] * dy + w_ref[2:3, :] * dpad[1:1 + S, :] + w_ref[1:2, :] * dpad[2:2 + S, :]
                       + w_ref[0:1, :] * dpad[3:3 + S, :])
        for i in range(4):
            dw_ref[i:i + 1, :] = jnp.sum(dy * xs[i], axis=0, keepdims=True)

    return pl.pallas_call(
        body, name="conv_bwd", grid=(12,),
        in_specs=[pl.BlockSpec((S, 128), lambda j: (0, 4 + j)), pl.BlockSpec((4, 128), lambda j: (0, j)),
                  pl.BlockSpec((S, 128), lambda j: (0, j))],
        out_specs=[pl.BlockSpec((S, 128), lambda j: (0, j)), pl.BlockSpec((4, 128), lambda j: (0, j))],
        out_shape=[jax.ShapeDtypeStruct((S, 1536), F32), jax.ShapeDtypeStruct((4, 1536), F32)],
        scratch_shapes=[pltpu.VMEM((S + 8, 128), F32), pltpu.VMEM((S + 8, 128), F32)],
        compiler_params=_cp("parallel"),
    )(pm, w, dout)


GDN_SUP = 4
GDN_ROWS = GDN_SUP * CHUNK


@jax.custom_vjp
def _saved_inverse(a, x):
    return x


_saved_inverse.defvjp(lambda a, x: (x, x),
                      lambda x, g: (-mm_tn(x, mm_nt(g, x)), jnp.zeros_like(x)))


def _gdn_chunk(q, k, v, gate, al, bl, alog, dtb, og, state, inv=None, want_inv=False):
    R = q.shape[0]
    r = lax.broadcasted_iota(jnp.int32, (R, R), 0)
    c = lax.broadcasted_iota(jnp.int32, (R, R), 1)
    same = (r // CHUNK) == (c // CHUNK)
    eye = (r == c).astype(F32)
    strict, causal, upper = same & (r > c), same & (r >= c), same & (r <= c)
    qn = q * lax.rsqrt(jnp.sum(q * q, axis=-1, keepdims=True) + EPS) * (GDN_D ** -0.5)
    kn = k * lax.rsqrt(jnp.sum(k * k, axis=-1, keepdims=True) + EPS)
    beta = jnp.sum(eye * jax.nn.sigmoid(bl), axis=1, keepdims=True)
    g_row = -jnp.exp(alog) * _softplus(al + dtb)
    g = jnp.sum(eye * g_row, axis=1, keepdims=True)
    gc_col = jnp.sum(jnp.where(causal, g_row, 0.0), axis=1, keepdims=True)
    gc_row = jnp.sum(jnp.where(upper, g, 0.0), axis=0, keepdims=True)
    gtot = jnp.sum(jnp.where(same, g_row, 0.0), axis=1, keepdims=True)
    gamma = jnp.exp(gc_col)
    diff = gc_col - gc_row
    d_strict = jnp.where(strict, jnp.exp(jnp.where(strict, diff, 0.0)), 0.0)
    d_causal = jnp.where(causal, jnp.exp(jnp.where(causal, diff, 0.0)), 0.0)
    a = beta * mm_nt(kn, kn) * d_strict
    if inv is None:
        p = -a
        x = eye + p
        for _ in range(5):
            p = mm(p, p)
            x = x + mm(x, p)
    else:
        x = _saved_inverse(a, inv)
    u_new = mm(x, beta * v)
    w_k = mm(x, (beta * gamma) * kn)
    qk = mm_nt(qn, kn) * d_causal
    q_g = qn * gamma
    k_tail = kn * jnp.exp(gtot - gc_col)
    ws, os_ = [], []
    for i in range(R // CHUNK):
        rows = slice(CHUNK * i, CHUNK * (i + 1))
        w_i = u_new[rows] - mm(w_k[rows], state)
        os_.append(mm(q_g[rows], state))
        decay = jnp.exp(jnp.sum(g[rows], axis=0, keepdims=True))
        state = decay * state + mm_tn(k_tail[rows], w_i)
        ws.append(w_i)
    o = jnp.concatenate(os_, axis=0) + mm(qk, jnp.concatenate(ws, axis=0))
    out = _rms(o, og) * (gate * jax.nn.sigmoid(gate))
    return (out, state, x) if want_inv else (out, state)


def _gdn_specs(nc, rev):
    def ci(c):
        return nc - 1 - c if rev else c

    def blk(cb):
        return pl.BlockSpec((GDN_ROWS, 512), lambda c: (ci(c), cb))

    col = lambda n: pl.BlockSpec((n, GDN_ROWS), lambda c: (0, ci(c)))
    sc = pl.BlockSpec((GDN_H, 1, 1), lambda c: (0, 0, 0))
    og = pl.BlockSpec((1, 128), lambda c: (0, 0))
    st = pl.BlockSpec((GDN_H, None, 128, 128), lambda c: (0, ci(c), 0, 0))
    return blk, col, sc, og, st


def gdn_fwd(qkvc, pm, abt, alog, dtb, og):
    S = qkvc.shape[0]
    nc = S // GDN_ROWS
    blk, col, sc, ogs, st = _gdn_specs(nc, False)

    def body(q_ref, k_ref, v_ref, gate_ref, ab_ref, alog_ref, dtb_ref, og_ref, o_ref, st_ref, inv_ref, state):
        c = pl.program_id(0)

        @pl.when(c == 0)
        def _():
            state[...] = jnp.zeros_like(state)

        st_ref[...] = state[...]
        for h in range(GDN_H):
            sl = slice(GDN_D * h, GDN_D * (h + 1))
            out, new_state, inv = _gdn_chunk(
                q_ref[:, sl], k_ref[:, sl], v_ref[:, sl], gate_ref[:, sl], ab_ref[h:h + 1, :], ab_ref[GDN_H + h:GDN_H + h + 1, :],
                alog_ref[h], dtb_ref[h], og_ref[...], state[h], want_inv=True)
            o_ref[:, sl] = out
            state[h] = new_state
            inv_ref[h] = inv

    inv_spec = pl.BlockSpec((GDN_H, None, GDN_ROWS, GDN_ROWS), lambda c: (0, c, 0, 0))
    return pl.pallas_call(
        body, name="gdn_fwd", grid=(nc,),
        in_specs=[blk(0), blk(1), blk(2), blk(4), col(2 * GDN_H), sc, sc, ogs],
        out_specs=[blk(0), st, inv_spec],
        out_shape=[jax.ShapeDtypeStruct((S, 512), F32), jax.ShapeDtypeStruct((GDN_H, nc, 128, 128), F32),
                   jax.ShapeDtypeStruct((GDN_H, nc, GDN_ROWS, GDN_ROWS), F32)],
        scratch_shapes=[pltpu.VMEM((GDN_H, 128, 128), F32)],
        compiler_params=_cp("arbitrary"),
    )(qkvc, qkvc, qkvc, pm, abt, alog, dtb, og)


def gdn_bwd(qkvc, pm, abt, alog, dtb, og, st, inv, dout):
    S = qkvc.shape[0]
    nc = S // GDN_ROWS
    blk, col, sc, ogs, sts = _gdn_specs(nc, True)

    def body(q_ref, k_ref, v_ref, gate_ref, ab_ref, alog_ref, dtb_ref, og_ref, st_ref, inv_ref, do_ref,
             dqkv_ref, dgate_ref, dal_ref, dbl_ref, dalog_ref, ddtb_ref, dog_ref, dstate):
        c = pl.program_id(0)

        @pl.when(c == 0)
        def _():
            dstate[...] = jnp.zeros_like(dstate)

        for h in range(GDN_H):
            sl = slice(GDN_D * h, GDN_D * (h + 1))
            args = (q_ref[:, sl], k_ref[:, sl], v_ref[:, sl], gate_ref[:, sl], ab_ref[h:h + 1, :], ab_ref[GDN_H + h:GDN_H + h + 1, :],
                    alog_ref[h], dtb_ref[h], og_ref[...], st_ref[h])
            inv_h = inv_ref[h]
            _, vjp = jax.vjp(lambda *a, inv_h=inv_h: _gdn_chunk(*a, inv=inv_h), *args)
            g = vjp((do_ref[:, sl], dstate[h]))
            for part in range(3):
                dqkv_ref[:, slice(512 * part + sl.start, 512 * part + sl.stop)] = g[part]
            dgate_ref[:, sl] = g[3]
            dal_ref[h:h + 1, :] = g[4]
            dbl_ref[h:h + 1, :] = g[5]
            dstate[h] = g[9]
            for o, gv in zip((dalog_ref, ddtb_ref, dog_ref), g[6:9]):
                @pl.when(c == 0)
                def _(o=o, gv=gv, h=h):
                    o[h] = gv

                @pl.when(c > 0)
                def _(o=o, gv=gv, h=h):
                    o[h] += gv

    ogo = pl.BlockSpec((GDN_H, 1, 128), lambda c: (0, 0, 0))
    inv_spec = pl.BlockSpec((GDN_H, None, GDN_ROWS, GDN_ROWS), lambda c: (0, nc - 1 - c, 0, 0))
    sd = jax.ShapeDtypeStruct
    return pl.pallas_call(
        body, name="gdn_bwd", grid=(nc,),
        in_specs=[blk(0), blk(1), blk(2), blk(4), col(2 * GDN_H), sc, sc, ogs, sts, inv_spec, blk(0)],
        out_specs=[pl.BlockSpec((GDN_ROWS, 1536), lambda c: (nc - 1 - c, 0)), blk(0), col(GDN_H), col(GDN_H),
                   sc, sc, ogo],
        out_shape=[sd((S, 1536), F32), sd((S, 512), F32)] + [sd((GDN_H, S), F32)] * 2 + [sd((GDN_H, 1, 1), F32)] * 2
        + [sd((GDN_H, 1, 128), F32)],
        scratch_shapes=[pltpu.VMEM((GDN_H, 128, 128), F32)],
        compiler_params=_cp("arbitrary"),
    )(qkvc, qkvc, qkvc, pm, abt, alog, dtb, og, st, inv, dout)


HG = 8
HG_LANES = HG * CA_D


def _group_mean_raw(y):
    r = lax.broadcasted_iota(jnp.int32, (128, 128), 0)
    c = lax.broadcasted_iota(jnp.int32, (128, 128), 1)
    g = jnp.where((r // CA_D) == (c // CA_D), 1.0 / CA_D, 0.0).astype(BF16)
    d = lambda u: lax.dot_general(u, g, (((1,), (0,)), ((), ())), preferred_element_type=F32)
    outs = []
    for j in range(y.shape[1] // 128):
        hi, lo = _split(y[:, 128 * j:128 * (j + 1)])
        outs.append(d(hi) + d(lo))
    return jnp.concatenate(outs, axis=1)


@jax.custom_vjp
def group_mean(y):
    return _group_mean_raw(y)


group_mean.defvjp(lambda y: (_group_mean_raw(y), None), lambda _, g: (_group_mean_raw(g),))


def f_headnorm(t, g):
    return (t * lax.rsqrt(group_mean(t * t) + EPS) * g,)


def _cattn_chunk(q, kb, vb, bias, valid):
    lane = lax.broadcasted_iota(jnp.int32, (1, 128), 1)
    m0 = (lane < CA_D).astype(F32)
    m1 = 1.0 - m0
    pairs = range(HG // 2)
    sl = [slice(128 * p, 128 * (p + 1)) for p in pairs]
    q2 = [jnp.concatenate([q[:, s] * m0, q[:, s] * m1], axis=0) for s in sl]
    sc = [mm_nt(q2[p], kb[:, sl[p]]) * (CA_D ** -0.5) + bias[sl[p]] for p in pairs]
    pr = [_softmax(jnp.where(valid, s, -1e30)) for s in sc]
    o2 = [mm(pr[p], vb[:, sl[p]]) for p in pairs]
    return jnp.concatenate([o[:CHUNK] * m0 + o[CHUNK:] * m1 for o in o2], axis=1)


def _cattn_valid(c):
    pos = lax.broadcasted_iota(jnp.int32, (1, CA_BAND), 1) + c * CHUNK
    return pos >= CA_PAD


def _cattn_specs(S):
    q_spec = pl.BlockSpec((CHUNK, HG_LANES), lambda h, c: (c, h))
    kv_spec = pl.BlockSpec((S + CA_PAD, HG_LANES), lambda h, c: (0, h))
    b_spec = pl.BlockSpec((HG * CHUNK, CA_BAND), lambda h, c: (h, 0))
    return q_spec, kv_spec, b_spec


def cattn_fwd(qn, kp, vp, bias):
    S = qn.shape[0]
    nc = S // CHUNK
    q_spec, kv_spec, b_spec = _cattn_specs(S)

    def body(q_ref, k_ref, v_ref, b_ref, o_ref):
        c = pl.program_id(1)
        start = pl.multiple_of(c * CHUNK, CHUNK)
        kb = k_ref[pl.ds(start, CA_BAND), :]
        vb = v_ref[pl.ds(start, CA_BAND), :]
        o_ref[...] = _cattn_chunk(q_ref[...], kb, vb, b_ref[...], _cattn_valid(c)).astype(o_ref.dtype)

    return pl.pallas_call(
        body, name="cattn_fwd", grid=(CA_H // HG, nc), in_specs=[q_spec, kv_spec, kv_spec, b_spec],
        out_specs=q_spec, out_shape=jax.ShapeDtypeStruct((S, D), BF16),
        compiler_params=_cp("parallel", "arbitrary"),
    )(qn, kp, vp, bias)


def kv_prep(qkv, kg):
    S = qkv.shape[0]
    tile = ROW_TILE
    lead = CA_PAD // tile

    def body(k_ref, v_ref, g_ref, kp_ref, vp_ref):
        i = pl.program_id(0)

        @pl.when(i < lead)
        def _():
            kp_ref[...] = jnp.zeros_like(kp_ref)
            vp_ref[...] = jnp.zeros_like(vp_ref)

        @pl.when(i >= lead)
        def _():
            kp_ref[...] = f_headnorm(k_ref[...], g_ref[...])[0].astype(BF16)
            vp_ref[...] = v_ref[...].astype(BF16)

    src = lambda cb: pl.BlockSpec((tile, D), lambda i, cb=cb: (jnp.maximum(i - lead, 0), cb))
    out = pl.BlockSpec((tile, D), lambda i: (i, 0))
    return pl.pallas_call(
        body, name="kv_prep", grid=((S + CA_PAD) // tile,),
        in_specs=[src(1), src(2), pl.BlockSpec((1, D), lambda i: (0, 0))], out_specs=[out, out],
        out_shape=[jax.ShapeDtypeStruct((S + CA_PAD, D), BF16)] * 2,
        compiler_params=_cp("parallel"),
    )(qkv, qkv, kg)


def cattn_bwd(qn, kp, vp, bias, do):
    S = qn.shape[0]
    nc = S // CHUNK
    q_spec, kv_spec, b_spec = _cattn_specs(S)

    def body(q_ref, k_ref, v_ref, b_ref, do_ref, dq_ref, dk_ref, dv_ref, db_ref):
        c = pl.program_id(1)

        @pl.when(c == 0)
        def _():
            dk_ref[...] = jnp.zeros_like(dk_ref)
            dv_ref[...] = jnp.zeros_like(dv_ref)
            db_ref[...] = jnp.zeros_like(db_ref)

        start = pl.multiple_of(c * CHUNK, CHUNK)
        kb = k_ref[pl.ds(start, CA_BAND), :].astype(F32)
        vb = v_ref[pl.ds(start, CA_BAND), :].astype(F32)
        valid = _cattn_valid(c)
        _, vjp = jax.vjp(lambda q, k, v, b: _cattn_chunk(q, k, v, b, valid), q_ref[...], kb, vb, b_ref[...])
        dq, dk, dv, db = vjp(do_ref[...])
        dq_ref[...] = dq
        dk_ref[pl.ds(start, CA_BAND), :] += dk
        dv_ref[pl.ds(start, CA_BAND), :] += dv
        db_ref[...] += db

    sd = jax.ShapeDtypeStruct
    return pl.pallas_call(
        body, name="cattn_bwd", grid=(CA_H // HG, nc), in_specs=[q_spec, kv_spec, kv_spec, b_spec, q_spec],
        out_specs=[q_spec, kv_spec, kv_spec, b_spec],
        out_shape=[sd((S, D), F32), sd((S + CA_PAD, D), F32), sd((S + CA_PAD, D), F32),
                   sd((CA_H * CHUNK, CA_BAND), F32)],
        compiler_params=_cp("parallel", "arbitrary"),
    )(qn, kp, vp, bias, do)


SKEW_W = CA_BAND + CHUNK


def rel_bias_grad(dbias):
    padded = jnp.pad(dbias, ((0, 0), (0, 0), (CHUNK, 0)))
    flat = jnp.pad(padded.reshape(CA_H, CHUNK * SKEW_W), ((0, 0), (0, CHUNK)))
    skew = flat.reshape(CA_H, CHUNK, SKEW_W + 1)

    first_near = SKEW_W - CHUNK - MAX_REL

    def fn(t):
        colsum = jnp.sum(t, axis=1, keepdims=True)
        j = lax.broadcasted_iota(jnp.int32, colsum.shape, 2)
        far = jnp.sum(jnp.where(j < first_near, colsum, 0.0), axis=2, keepdims=True)
        return (colsum + jnp.where(j == first_near, far, 0.0),)

    (colsum,) = whole("relbias_sum", fn, [skew], [((CA_H, 1, SKEW_W + 1), F32)])
    near = colsum[:, 0, first_near:SKEW_W][:, ::-1]
    return jnp.concatenate([jnp.zeros((CA_H, CHUNK + 1), F32), near], axis=1)


def rel_bias_expand(rb):
    near = rb[:, CHUNK + 1:][:, ::-1]
    far = jnp.broadcast_to(rb[:, 2 * MAX_REL:], (CA_H, SKEW_W - CHUNK - MAX_REL))
    t = jnp.concatenate([far, near, jnp.zeros((CA_H, 1), rb.dtype)], axis=1)
    rows = jnp.tile(t, (1, CHUNK))[:, :CHUNK * SKEW_W].reshape(CA_H, CHUNK, SKEW_W)
    return rows[:, :, CHUNK:]


def loss_head(y, target):
    S = y.shape[0]
    tile = min(ROW_TILE, S)

    def body(y_ref, t_ref, dy_ref, acc_ref):
        i = pl.program_id(0)
        e = y_ref[...] - t_ref[...]
        dy_ref[...] = e * (1.0 / D)
        part = jnp.sum(e * e, axis=0, keepdims=True) * (0.5 / D)

        @pl.when(i == 0)
        def _():
            acc_ref[...] = part

        @pl.when(i > 0)
        def _():
            acc_ref[...] += part

    row = pl.BlockSpec((tile, D), lambda i: (i, 0))
    return pl.pallas_call(
        body, name="loss_head", grid=(S // tile,), in_specs=[row, row],
        out_specs=[row, pl.BlockSpec((1, D), lambda i: (0, 0))],
        out_shape=[jax.ShapeDtypeStruct((S, D), F32), jax.ShapeDtypeStruct((1, D), F32)],
        compiler_params=_cp("arbitrary"),
    )(y, target)


ANY = pl.BlockSpec(memory_space=pl.ANY)


HBM = pl.BlockSpec(memory_space=pltpu.HBM)
SEM = pl.BlockSpec(memory_space=pltpu.SEMAPHORE)
EFFECT = pltpu.SideEffectType.DATAFLOW_SIDE_EFFECTING


def _chip_copies(src_ref, land_ref, send_sems, recv_sems, mode):
    x, y, c = lax.axis_index("x"), lax.axis_index("y"), lax.axis_index("c")
    me = 2 * x + y

    def copy(j, s, d, dev):
        return pltpu.make_async_remote_copy(src_ref=s, dst_ref=d, send_sem=send_sems.at[j], recv_sem=recv_sems.at[j],
                                            device_id=dev, device_id_type=MESH)

    if mode == "sibling":
        cp = copy(0, src_ref, land_ref, (x, y, 1 - c))
        return [(cp, cp)]
    out = []
    for j, (px, py) in enumerate([(1 - x, y), (x, 1 - y), (1 - x, 1 - y)]):
        peer = 2 * px + py
        if mode == "scatter":
            pairs, dev = ((src_ref.at[peer], land_ref.at[j]), (src_ref.at[me], land_ref.at[j])), (px, py, c)
        elif mode == "gather":
            pairs, dev = ((src_ref, land_ref.at[me]), (src_ref, land_ref.at[peer])), (px, py, c)
        else:
            half = land_ref.shape[1] // 2
            mine = pl.ds(pl.multiple_of(c * half, 16), half)
            theirs = pl.ds(pl.multiple_of((1 - c) * half, 16), half)
            if mode == "gather_half":
                pairs = ((src_ref.at[mine], land_ref.at[me, mine]), (src_ref.at[mine], land_ref.at[peer, mine]))
                dev = (px, py, c)
            else:
                pairs = ((land_ref.at[peer, mine],) * 2, (land_ref.at[peer, theirs],) * 2)
                dev = (x, y, 1 - c)
        out.append(tuple(copy(j, s, d, dev) for s, d in pairs))
    return out


def exchange_start(src, land, carry, name, mode):
    def body(src_ref, land_ref, carry_ref, send_sems, recv_sems, src_out, land_out, carry_out):
        for send, _ in _chip_copies(src_ref, land_ref, send_sems, recv_sems, mode):
            send.start()

    hbm = lambda a: pltpu.HBM(a.shape, a.dtype)
    n = 1 if mode == "sibling" else 3
    return pl.pallas_call(
        body, name=name,
        out_shape=(pltpu.SemaphoreType.DMA((n,)), pltpu.SemaphoreType.DMA((n,)), hbm(src), hbm(land), hbm(carry)),
        in_specs=(HBM, HBM, HBM), out_specs=(SEM, SEM, HBM, HBM, HBM),
        input_output_aliases={0: 2, 1: 3, 2: 4},
        compiler_params=pltpu.CompilerParams(has_side_effects=EFFECT),
    )(pltpu.with_memory_space_constraint(src, pltpu.HBM), pltpu.with_memory_space_constraint(land, pltpu.HBM),
      pltpu.with_memory_space_constraint(carry, pltpu.HBM))


def exchange_wait(src, land, send_sems, recv_sems, after, name, mode):
    def body(src_ref, land_ref, send_sems_ref, recv_sems_ref, after_ref, src_out, land_out):
        for send, recv in _chip_copies(src_ref, land_ref, send_sems_ref, recv_sems_ref, mode):
            send.wait_send()
            recv.wait_recv()

    hbm = lambda a: pltpu.HBM(a.shape, a.dtype)
    return pl.pallas_call(
        body, name=name, out_shape=(hbm(src), hbm(land)),
        in_specs=(HBM, HBM, SEM, SEM, ANY), out_specs=(HBM, HBM), input_output_aliases={0: 0, 1: 1},
        compiler_params=pltpu.CompilerParams(has_side_effects=EFFECT),
    )(src, land, send_sems, recv_sems, after)


def sibling_exchange(srcs, name):
    n = len(srcs)

    def body(*refs):
        src_refs, out_refs, send_sems, recv_sems = refs[:n], refs[n:2 * n], refs[2 * n], refs[2 * n + 1]
        x, y, c = lax.axis_index("x"), lax.axis_index("y"), lax.axis_index("c")
        copies = [pltpu.make_async_remote_copy(src_ref=s, dst_ref=o, send_sem=send_sems.at[k], recv_sem=recv_sems.at[k],
                                               device_id=(x, y, 1 - c), device_id_type=MESH)
                  for k, (s, o) in enumerate(zip(src_refs, out_refs))]
        for cp in copies:
            cp.start()
        for cp in copies:
            cp.wait()

    return pl.pallas_call(
        body, name=name, in_specs=[ANY] * n, out_specs=[ANY] * n,
        out_shape=[jax.ShapeDtypeStruct(s.shape, s.dtype) for s in srcs],
        scratch_shapes=[pltpu.SemaphoreType.DMA((n,)), pltpu.SemaphoreType.DMA((n,))],
    )(*srcs)


def sum_own_slabs(src, chip, land, name, tile=512):
    _, R, C = src.shape
    n = land.shape[0]
    tile = _tile(R, tile, 16)

    def body(chip_ref, o_ref, t_ref, out_ref):
        acc = o_ref[...].astype(F32)
        for s in range(n):
            acc = acc + t_ref[s].astype(F32)
        out_ref[...] = acc

    return pl.pallas_call(
        body, name=name,
        grid_spec=pltpu.PrefetchScalarGridSpec(
            num_scalar_prefetch=1, grid=(R // tile,),
            in_specs=[pl.BlockSpec((None, tile, C), lambda i, chip_ref: (chip_ref[0], i, 0)),
                      pl.BlockSpec((n, tile, C), lambda i, chip_ref: (0, i, 0))],
            out_specs=pl.BlockSpec((tile, C), lambda i, chip_ref: (i, 0))),
        out_shape=jax.ShapeDtypeStruct((R, C), F32),
        compiler_params=_cp("parallel"),
    )(jnp.reshape(chip, (1,)).astype(jnp.int32), src, land)


def sum_slabs(t, name, tile=512):
    n, R, C = t.shape
    tile = _tile(R, tile, 16)

    def body(t_ref, o_ref):
        acc = t_ref[0].astype(F32)
        for s in range(1, n):
            acc = acc + t_ref[s].astype(F32)
        o_ref[...] = acc

    return pl.pallas_call(
        body, name=name, grid=(R // tile,), in_specs=[pl.BlockSpec((n, tile, C), lambda i: (0, i, 0))],
        out_specs=pl.BlockSpec((tile, C), lambda i: (i, 0)), out_shape=jax.ShapeDtypeStruct((R, C), F32),
        compiler_params=_cp("parallel"),
    )(t)


def _pad_rows(a, mult=16):
    r = (-a.shape[0]) % mult
    return jnp.pad(a, ((0, r), (0, 0))) if r else a


BIG = ["ab_w_in", "c_w_qkv", "xa_w_kv", "f_w_gate", "f_w_up", "ab_w_out", "c_w_out", "xa_w_q", "xa_w_out",
       "f_w_down", "s5_w_glu"]


GROUPS = ("B", "A")
GROUP_ROW_MULT = 64


def group_spec(layer, grp):
    i = layer // 2
    if grp == "A":
        return [("xa_w_kv", layer, True, 512), ("xa_w_q", layer, False, 256), ("xa_w_out", layer, False, 256),
                ("f_w_gate", layer, True, 704), ("f_w_up", layer, True, 704), ("f_w_down", layer, False, 704)]
    if layer % 2 == 0:
        return [("ab_w_in", i, True, 642), ("ab_w_out", i, False, 256), ("s5_w_glu", i, False, 64)]
    return [("c_w_qkv", i, True, 768), ("c_w_out", i, False, 256)]


def _seg_rows(rows):
    return rows + ((-rows) % 16)


def _f32_rows(a):
    bits = lax.bitcast_convert_type(a.reshape(-1), BF16).reshape(-1)
    return jnp.pad(bits, (0, 16 * D - bits.shape[0])).reshape(16, D)


def pack_group_shards(p, layer, grp):
    segs = []
    for name, idx, transposed, rows in group_spec(layer, grp):
        w = p[name][idx]
        if transposed:
            w = w.T
        segs.append(_pad_rows(w.astype(BF16).reshape(-1, D)))
    if grp == "B":
        small = p["gdn_conv_w"] if layer % 2 == 0 else p["c_norm_g"]
        segs.append(_f32_rows(small[layer // 2]))
    return _pad_rows(jnp.concatenate(segs, axis=0), GROUP_ROW_MULT)


def unpack_group_gathered(g, layer, grp):
    out, off = {}, 0
    for name, idx, transposed, rows in group_spec(layer, grp):
        seg = g[:, off:off + rows]
        if name == "s5_w_glu":
            out[name] = seg.reshape(N_CHIPS * 128, 512)
        else:
            out[name] = seg.reshape(N_CHIPS * rows, D)
        off += _seg_rows(rows)
    if grp == "B":
        n = 4 * 384 if layer % 2 == 0 else 256
        bits = g[:, off:off + 16].reshape(N_CHIPS, -1)[:, :2 * n].reshape(N_CHIPS, n, 2)
        small = lax.bitcast_convert_type(bits, F32)
        if layer % 2 == 0:
            out["gdn_conv_w"] = jnp.swapaxes(small.reshape(N_CHIPS, 4, 384), 0, 1).reshape(4, 1536)
        else:
            out["c_norm_g"] = small.reshape(D)
    return out


def pack_group_grads(gr, layer, grp):
    segs = []
    for name, idx, transposed, rows in group_spec(layer, grp):
        w = gr[name].astype(BF16)
        seg = w.reshape(N_CHIPS, rows, D)
        r = (-rows) % 16
        if r:
            seg = jnp.pad(seg, ((0, 0), (0, r), (0, 0)))
        segs.append(seg)
    out = jnp.concatenate(segs, axis=1)
    return jnp.pad(out, ((0, 0), (0, (-out.shape[1]) % GROUP_ROW_MULT), (0, 0)))


def unpack_group_reduced(g, layer, grp):
    out, off = {}, 0
    for name, idx, transposed, rows in group_spec(layer, grp):
        seg = g[off:off + rows]
        if name == "s5_w_glu":
            out[name] = seg.reshape(128, 512)
        else:
            out[name] = seg.T if (transposed and name not in ADAM_TRANSPOSED) else seg
        off += _seg_rows(rows)
    return out


ADAM_TRANSPOSED = ("f_w_gate", "f_w_up")


SMALL = ["ab_norm_g", "s5_a_re", "s5_a_im", "s5_log_dt", "s5_b_re", "s5_b_im", "s5_c_re", "s5_c_im", "s5_d",
         "s5_b_glu", "gdn_conv_w", "gdn_a_log", "gdn_dt_bias", "gdn_out_norm_g", "c_norm_g", "c_q_norm_g",
         "c_k_norm_g", "c_rel_bias", "mem_norm_g", "xa_norm_g", "xa_q_norm_g", "xa_k_norm_g", "f_norm_g"]


def _lane_rows(a):
    flat = a.reshape(-1).astype(F32)
    return jnp.pad(flat, (0, (-flat.shape[0]) % 1024)).reshape(-1, 128)


def pack_small(d, extra=None):
    parts = [_lane_rows(d[n]) for n in SMALL]
    if extra is not None:
        parts.append(_lane_rows(extra))
    rows = jnp.concatenate(parts, axis=0)
    return jnp.pad(rows, ((0, (-rows.shape[0]) % 128), (0, 0)))


def unpack_small(rows, shapes):
    out, off = {}, 0
    for n in SMALL:
        sz = int(np.prod(shapes[n]))
        k = 8 * -(-sz // 1024)
        out[n] = rows[off:off + k].reshape(-1)[:sz].reshape(shapes[n])
        off += k
    return out, rows[off:]


def _s5_blockdiag_b(b):
    bt = jnp.swapaxes(b, 1, 2).reshape(S5_GB, 8, S5_C, S5_P)
    eye = jnp.eye(8, dtype=b.dtype)
    return jnp.einsum("bgcp,gh->bgchp", bt, eye).reshape(S5_GB, 8 * S5_C, 8 * S5_P)


def _s5_blockdiag_c(c):
    ct = jnp.swapaxes(c, 1, 2).reshape(S5_GB, 8, S5_P, S5_C)
    eye = jnp.eye(8, dtype=c.dtype)
    return jnp.einsum("bgpc,gh->bgphc", ct, eye).reshape(S5_GB, 8 * S5_P, 8 * S5_C)


def _s5_diag_b(db):
    t = db.reshape(S5_GB, 8, S5_C, 8, S5_P)
    t = jnp.transpose(t, (0, 2, 4, 1, 3)).reshape(S5_GB, S5_C, S5_P, 64)
    d = t[..., ::9]
    return jnp.transpose(d, (0, 3, 2, 1)).reshape(S5_G, S5_P, S5_C)


def _s5_diag_c(dc):
    t = dc.reshape(S5_GB, 8, S5_P, 8, S5_C)
    t = jnp.transpose(t, (0, 2, 4, 1, 3)).reshape(S5_GB, S5_P, S5_C, 64)
    d = t[..., ::9]
    return jnp.transpose(d, (0, 3, 2, 1)).reshape(S5_G, S5_C, S5_P)


def local_step(x, mem, target, p, wsrc, gsink, wpre=lambda layer, grp, carry: carry):
    S = x.shape[0]
    row2 = lambda a: a.reshape(1, -1)
    saved = []
    (mem_n,) = row_fwd("mem_norm", f_norm, [mem], [row2(p["mem_norm_g"])], [(D, BF16)])
    gs = {}

    for layer in range(DEPTH):
        i = layer // 2
        w = wsrc(layer, "B", x)
        sv = {"x0": x, "wB": w}
        if layer % 2 == 0:
            (h,) = row_fwd("norm", f_norm, [x], [row2(p["ab_norm_g"][i])], [(D, BF16)])
            w_in = w["ab_w_in"]
            pm = matmul("nt", h, w_in[:2560], "proj_main")
            abt = matmul("nt", w_in[2560:], h, "proj_ab")
            s5p = dict(
                are=p["s5_a_re"][i].reshape(1, -1), aim=p["s5_a_im"][i].reshape(1, -1),
                ldt=jnp.broadcast_to(p["s5_log_dt"][i][:, None], (S5_G, S5_P)).reshape(1, -1),
                b_re=_s5_blockdiag_b(p["s5_b_re"][i]), b_im=_s5_blockdiag_b(p["s5_b_im"][i]),
                c_re=_s5_blockdiag_c(p["s5_c_re"][i]), c_im=_s5_blockdiag_c(p["s5_c_im"][i]),
                dv=p["s5_d"][i].reshape(1, -1))
            y5, st5 = s5_fwd(pm, **s5p)
            (a_out,) = row_fwd("glu", f_glu, [y5], [w["s5_w_glu"], row2(p["s5_b_glu"][i])], [(S5_W, F32)])
            conv_w = w["gdn_conv_w"]
            qkvc = conv_fwd(pm, conv_w)
            alog = p["gdn_a_log"][i].reshape(GDN_H, 1, 1)
            dtb = p["gdn_dt_bias"][i].reshape(GDN_H, 1, 1)
            og = row2(p["gdn_out_norm_g"][i])
            b_out, stg, inv = gdn_fwd(qkvc, pm, abt, alog, dtb, og)
            cat = wpre(layer, "A", jnp.concatenate([a_out, b_out], axis=1))
            x, hq = matmul("nn", cat, w["ab_w_out"], "mix_out", add=x, norm_out=row2(p["xa_norm_g"][layer]))
            sv.update(h=h, pm=pm, s5p=s5p, y5=y5, st5=st5, qkvc=qkvc, abt=abt, alog=alog, dtb=dtb, og=og,
                      stg=stg, inv=inv, cat=cat, conv_w=conv_w)
        else:
            (h,) = row_fwd("norm", f_norm, [x], [row2(w["c_norm_g"])], [(D, BF16)])
            qkv = matmul("nt", h, w["c_w_qkv"], "proj_qkv")
            qg = jnp.tile(row2(p["c_q_norm_g"][i]), (1, CA_H))
            kg = jnp.tile(row2(p["c_k_norm_g"][i]), (1, CA_H))
            (qn,) = row_fwd("headnorm_q", f_headnorm, [(qkv, D, 0)], [qg], [(D, F32)])
            kp, vp = kv_prep(qkv, kg)
            bias = rel_bias_expand(p["c_rel_bias"][i]).reshape(CA_H * CHUNK, CA_BAND)
            o = wpre(layer, "A", cattn_fwd(qn, kp, vp, bias))
            x, hq = matmul("nn", o, w["c_w_out"], "mix_out", add=x, norm_out=row2(p["xa_norm_g"][layer]))
            sv.update(h=h, qkv=qkv, qg=qg, kg=kg, qn=qn, kp=kp, vp=vp, bias=bias, o=o)
        sv["x1"] = x
        w = wsrc(layer, "A", x)
        sv["wA"] = w
        qx = matmul("nn", hq, w["xa_w_q"], "xa_q")
        kv = matmul("nt", mem_n, w["xa_w_kv"], "xa_kv")
        xqg, xkg = row2(p["xa_q_norm_g"][layer]), row2(p["xa_k_norm_g"][layer])
        (ox,) = row_fwd("xattn", f_xattn, [qx], [kv, xqg, xkg], [(D, BF16)])
        x, hf = matmul("nn", ox, w["xa_w_out"], "xa_out", add=x, norm_out=row2(p["f_norm_g"][layer]))
        sv.update(hq=hq, qx=qx, kv=kv, ox=ox)
        sv["x2"] = x
        gate, up, act = ffn_in(hf, w["f_w_gate"], w["f_w_up"])
        if layer + 1 < DEPTH:
            act = wpre(layer + 1, "B", act)
        x = matmul("nn", act, w["f_w_down"], "ffn_down", add=x)
        sv.update(hf=hf, gate=gate, up=up, act=act)
        saved.append(sv)

    dx, loss_vec = loss_head(x, target)

    dmem_n = None
    for layer in reversed(range(DEPTH)):
        i = layer // 2
        sv = saved[layer]
        w, gw = sv["wA"], {}
        dgate, dup = ffn_dact(dx, w["f_w_down"], sv["gate"], sv["up"])
        gw["f_w_down"] = matmul("tn", sv["act"], dx, "ffn_dwd", out_dtype=BF16)
        gw["f_w_gate"] = matmul("tn", dgate, sv["hf"], "ffn_dwg", out_dtype=BF16)
        gw["f_w_up"] = matmul("tn", dup, sv["hf"], "ffn_dwu", out_dtype=BF16)
        dh = matmul("nn", dgate, w["f_w_gate"], "ffn_dhg")
        dx, dg = matmul("nn", dup, w["f_w_up"], "ffn_dhu", add=dh,
                        norm_bwd=(sv["x2"], row2(p["f_norm_g"][layer]), dx))
        gs.setdefault("f_norm_g", [None] * DEPTH)[layer] = dg[0]
        do = matmul("nt", dx, w["xa_w_out"], "xa_do")
        gw["xa_w_out"] = matmul("tn", sv["ox"], dx, "xa_dwo", out_dtype=BF16)
        xqg, xkg = row2(p["xa_q_norm_g"][layer]), row2(p["xa_k_norm_g"][layer])
        dqx, dkv, dqg, dkg = row_bwd("xattn_bwd", f_xattn, [sv["qx"]], [sv["kv"], xqg, xkg], [do],
                                     [0], [0, 1, 2])
        gs.setdefault("xa_q_norm_g", [None] * DEPTH)[layer] = dqg[0]
        gs.setdefault("xa_k_norm_g", [None] * DEPTH)[layer] = dkg[0]
        gw["xa_w_q"] = matmul("tn", sv["hq"], dqx, "xa_dwq", out_dtype=BF16)
        gw["xa_w_kv"] = matmul("tn", dkv, mem_n, "xa_dwkv", out_dtype=BF16)
        dx = gsink(layer, "A", gw, dx)
        dmem_n = matmul("nn", dkv, w["xa_w_kv"], "xa_dmem", add=dmem_n)
        dx, dg = matmul("nt", dqx, w["xa_w_q"], "xa_dhq", norm_bwd=(sv["x1"], row2(p["xa_norm_g"][layer]), dx))
        gs.setdefault("xa_norm_g", [None] * DEPTH)[layer] = dg[0]
        w, gw = sv["wB"], {}
        if layer % 2 == 0:
            dcat = matmul("nt", dx, w["ab_w_out"], "mix_dcat")
            gw["ab_w_out"] = matmul("tn", sv["cat"], dx, "mix_dwo", out_dtype=BF16)
            dy5, dwglu, dbglu = row_bwd("glu_bwd", f_glu, [sv["y5"]], [w["s5_w_glu"], row2(p["s5_b_glu"][i])],
                                        [(dcat, S5_W, 0)], [0], [0, 1])
            gw["s5_w_glu"] = dwglu
            gs.setdefault("s5_b_glu", [None] * 2)[i] = dbglu[0]
            s5p = sv["s5p"]
            du, dare, daim, dldt, dbre, dbim, dcre, dcim, ddv = s5_bwd(sv["pm"], sv["st5"], dy5, **s5p)
            (dldt_g,) = whole("s5_dt_sum", lambda t: (jnp.sum(t, axis=1, keepdims=True),),
                              [dldt.reshape(S5_G, S5_P)], [((S5_G, 1), F32)])
            for nme, val in (("s5_a_re", dare.reshape(S5_G, S5_P)), ("s5_a_im", daim.reshape(S5_G, S5_P)),
                             ("s5_log_dt", dldt_g[:, 0]), ("s5_b_re", _s5_diag_b(dbre)),
                             ("s5_b_im", _s5_diag_b(dbim)), ("s5_c_re", _s5_diag_c(dcre)),
                             ("s5_c_im", _s5_diag_c(dcim)), ("s5_d", ddv.reshape(S5_G, S5_C))):
                gs.setdefault(nme, [None] * 2)[i] = val
            dqkvc, dgate, dal, dbl, dalog, ddtb, dog = gdn_bwd(
                sv["qkvc"], sv["pm"], sv["abt"], sv["alog"], sv["dtb"], sv["og"], sv["stg"], sv["inv"],
                dcat[:, S5_W:])
            (dog_s,) = whole("gdn_og_sum", lambda t: (jnp.sum(t, axis=0, keepdims=True),),
                             [dog.reshape(GDN_H, GDN_D)], [((1, GDN_D), F32)])
            gs.setdefault("gdn_out_norm_g", [None] * 2)[i] = dog_s[0]
            gs.setdefault("gdn_a_log", [None] * 2)[i] = dalog.reshape(GDN_H)
            gs.setdefault("gdn_dt_bias", [None] * 2)[i] = ddtb.reshape(GDN_H)
            dqkv, dconv = conv_bwd(sv["pm"], sv["conv_w"], dqkvc)
            gs.setdefault("gdn_conv_w", [None] * 2)[i] = dconv
            dpm = jnp.concatenate([du, dqkv, dgate], axis=1).astype(BF16)
            dabt = jnp.concatenate([dal, dbl], axis=0)
            dw_main = matmul("tn", dpm, sv["h"], "proj_dw", out_dtype=BF16)
            dw_ab = matmul("nn", dabt, sv["h"], "proj_ab_dw", out_dtype=BF16)
            gw["ab_w_in"] = jnp.concatenate([dw_main, dw_ab], axis=0)
            dx = gsink(layer, "B", gw, dx)
            w_in = w["ab_w_in"]
            dh = matmul("nn", dpm, w_in[:2560], "proj_dh")
            dx, dg = matmul("tn", dabt, w_in[2560:], "proj_ab_dh", add=dh,
                            norm_bwd=(sv["x0"], row2(p["ab_norm_g"][i]), dx))
            gs.setdefault("ab_norm_g", [None] * 2)[i] = dg[0]
        else:
            do = matmul("nt", dx, w["c_w_out"], "mix_dcat")
            gw["c_w_out"] = matmul("tn", sv["o"], dx, "mix_dwo", out_dtype=BF16)
            dqn, dkp, dvp, dbias = cattn_bwd(sv["qn"], sv["kp"], sv["vp"], sv["bias"], do)
            gs.setdefault("c_rel_bias", [None] * 2)[i] = rel_bias_grad(dbias.reshape(CA_H, CHUNK, CA_BAND))
            dq, dqg = row_bwd("headnorm_bwd", f_headnorm, [(sv["qkv"], D, 0)], [sv["qg"]], [dqn], [0], [0])
            dk, dkg = row_bwd("headnorm_bwd", f_headnorm, [(sv["qkv"], D, 1)], [sv["kg"]],
                              [(dkp, D, 0, CA_PAD)], [0], [0])
            head_sum = lambda t: (jnp.sum(t, axis=0, keepdims=True),)
            (dqg,) = whole("headgain_sum", head_sum, [dqg.reshape(CA_H, CA_D)], [((1, CA_D), F32)])
            (dkg,) = whole("headgain_sum", head_sum, [dkg.reshape(CA_H, CA_D)], [((1, CA_D), F32)])
            gs.setdefault("c_q_norm_g", [None] * 2)[i] = dqg[0]
            gs.setdefault("c_k_norm_g", [None] * 2)[i] = dkg[0]
            dqkv = jnp.concatenate([dq, dk, dvp[CA_PAD:]], axis=1).astype(BF16)
            gw["c_w_qkv"] = matmul("tn", dqkv, sv["h"], "proj_qkv_dw", out_dtype=BF16)
            dx = gsink(layer, "B", gw, dx)
            dx, dg = matmul("nn", dqkv, w["c_w_qkv"], "proj_qkv_dh",
                            norm_bwd=(sv["x0"], row2(w["c_norm_g"]), dx))
            gs.setdefault("c_norm_g", [None] * 2)[i] = dg[0]
    (dmg,) = row_bwd("mem_norm_bwd", f_norm, [mem], [row2(p["mem_norm_g"])], [dmem_n], [], [0])
    small = {n: jnp.stack(v) for n, v in gs.items()}
    small["mem_norm_g"] = dmg[0]
    return loss_vec, dx, small


ADAM_BLOCK_BYTES = 3 << 19


def adam(w, g, m, v, name):
    shape = w.shape
    if w.ndim == 3:
        d0, n, d2 = shape
        fits = [t for t in range(8, n + 1, 8) if n % t == 0 and d0 * t * d2 * 4 <= ADAM_BLOCK_BYTES]
        return tuple(row_fwd(name, f_adam, [w, g, m, v], [], [((d0, d2), F32)] * 3, tile=max(fits)))
    cols = shape[-1]
    w2, g2, m2, v2 = (t.reshape(-1, cols) for t in (w, g, m, v))
    rows = w2.shape[0]
    tile = rows if rows <= 512 else _tile(rows, 512, 8)
    outs = row_fwd(name, f_adam, [w2, g2, m2, v2], [], [(cols, F32)] * 3, tile=tile)
    return tuple(o.reshape(shape) for o in outs)


WEIGHTS = ['ab_norm_g', 'ab_w_in', 'ab_w_out', 's5_a_re', 's5_a_im', 's5_log_dt', 's5_b_re', 's5_b_im', 's5_c_re',
           's5_c_im', 's5_d', 's5_w_glu', 's5_b_glu', 'gdn_conv_w', 'gdn_a_log', 'gdn_dt_bias', 'gdn_out_norm_g',
           'c_norm_g', 'c_w_qkv', 'c_w_out', 'c_q_norm_g', 'c_k_norm_g', 'c_rel_bias', 'mem_norm_g', 'xa_norm_g',
           'xa_w_q', 'xa_w_kv', 'xa_w_out', 'xa_q_norm_g', 'xa_k_norm_g', 'f_norm_g', 'f_w_gate', 'f_w_up',
           'f_w_down']
SHARDED_SMALL = {"gdn_conv_w": (2, 384), "c_norm_g": (1, 256)}


def kernel(x, mem, ab_norm_g, ab_w_in, ab_w_out, s5_a_re, s5_a_im, s5_log_dt, s5_b_re, s5_b_im, s5_c_re, s5_c_im, s5_d, s5_w_glu, s5_b_glu, gdn_conv_w, gdn_a_log, gdn_dt_bias, gdn_out_norm_g, c_norm_g, c_w_qkv, c_w_out, c_q_norm_g, c_k_norm_g, c_rel_bias, mem_norm_g, xa_norm_g, xa_w_q, xa_w_kv, xa_w_out, xa_q_norm_g, xa_k_norm_g, f_norm_g, f_w_gate, f_w_up, f_w_down, loss_target, m_ab_norm_g, m_ab_w_in, m_ab_w_out, m_s5_a_re, m_s5_a_im, m_s5_log_dt, m_s5_b_re, m_s5_b_im, m_s5_c_re, m_s5_c_im, m_s5_d, m_s5_w_glu, m_s5_b_glu, m_gdn_conv_w, m_gdn_a_log, m_gdn_dt_bias, m_gdn_out_norm_g, m_c_norm_g, m_c_w_qkv, m_c_w_out, m_c_q_norm_g, m_c_k_norm_g, m_c_rel_bias, m_mem_norm_g, m_xa_norm_g, m_xa_w_q, m_xa_w_kv, m_xa_w_out, m_xa_q_norm_g, m_xa_k_norm_g, m_f_norm_g, m_f_w_gate, m_f_w_up, m_f_w_down, v_ab_norm_g, v_ab_w_in, v_ab_w_out, v_s5_a_re, v_s5_a_im, v_s5_log_dt, v_s5_b_re, v_s5_b_im, v_s5_c_re, v_s5_c_im, v_s5_d, v_s5_w_glu, v_s5_b_glu, v_gdn_conv_w, v_gdn_a_log, v_gdn_dt_bias, v_gdn_out_norm_g, v_c_norm_g, v_c_w_qkv, v_c_w_out, v_c_q_norm_g, v_c_k_norm_g, v_c_rel_bias, v_mem_norm_g, v_xa_norm_g, v_xa_w_q, v_xa_w_kv, v_xa_w_out, v_xa_q_norm_g, v_xa_k_norm_g, v_f_norm_g, v_f_w_gate, v_f_w_up, v_f_w_down):
    args = locals()
    p = {n: args[n] for n in WEIGHTS}
    m = {n: args["m_" + n] for n in WEIGHTS}
    v = {n: args["v_" + n] for n in WEIGHTS}
    chip = 2 * lax.axis_index("x") + lax.axis_index("y")

    carry = x[0]
    gathers = {}
    for layer in range(DEPTH):
        for grp in GROUPS:
            src = pack_group_shards(p, layer, grp)
            land = lax.dynamic_update_slice(lax.empty((N_CHIPS,) + src.shape, BF16), src[None], (chip, 0, 0))
            send_sems, recv_sems, src, land, carry = exchange_start(
                src, land, carry, f"gather_start_{layer}{grp}", mode="gather_half")
            gathers[layer, grp] = (src, land, send_sems, recv_sems)
    forwards = {}

    def wpre(layer, grp, carry):
        if (layer, grp) not in forwards:
            src, land, send_sems, recv_sems = gathers[layer, grp]
            src, land = exchange_wait(src, land, send_sems, recv_sems, carry, f"gather_wait_{layer}{grp}",
                                      mode="gather_half")
            send_sems, recv_sems, src, land, carry = exchange_start(
                src, land, carry, f"forward_start_{layer}{grp}", mode="forward_half")
            forwards[layer, grp] = (src, land, send_sems, recv_sems)
        return carry

    def wsrc(layer, grp, after):
        wpre(layer, grp, after)
        src, land, send_sems, recv_sems = forwards[layer, grp]
        _, land = exchange_wait(src, land, send_sems, recv_sems, after, f"forward_wait_{layer}{grp}",
                                mode="forward_half")
        return unpack_group_gathered(land, layer, grp)

    scatters, siblings = [], []
    LAG = 2

    def finish(carry):
        layer, grp, src, land, send_sems, recv_sems = scatters[len(siblings)]
        src, land = exchange_wait(src, land, send_sems, recv_sems, carry, f"scatter_wait_{layer}{grp}", mode="scatter")
        part = sum_own_slabs(src, chip, land, "sum_chips")
        send_sems, recv_sems, part, other, carry = exchange_start(
            part, lax.empty(part.shape, F32), carry, f"sibling_start_{layer}{grp}", mode="sibling")
        siblings.append((layer, grp, part, other, send_sems, recv_sems))
        return carry

    def gsink(layer, grp, gw, carry):
        src = pack_group_grads(gw, layer, grp)
        land = lax.empty((3,) + src.shape[1:], BF16)
        send_sems, recv_sems, src, land, carry = exchange_start(
            src, land, carry, f"scatter_start_{layer}{grp}", mode="scatter")
        scatters.append((layer, grp, src, land, send_sems, recv_sems))
        if len(scatters) > LAG:
            carry = finish(carry)
        return carry

    carry = wpre(0, "B", carry)
    loss_vec, dx, g_small = local_step(carry, mem[0], loss_target[0], p, wsrc, gsink, wpre)

    full_shapes = {n: ((2, 4, 1536) if n == "gdn_conv_w" else (2, D) if n == "c_norm_g" else p[n].shape)
                   for n in SMALL}
    small_mine = pack_small(g_small, extra=loss_vec)
    (small_other,) = sibling_exchange([small_mine], "sibling_small")
    (small_chip,) = row_fwd("sum_small_cores", lambda a, b: (a + b,), [small_mine, small_other], [], [(128, F32)],
                            tile=128)
    small_land = lax.dynamic_update_slice(lax.empty((N_CHIPS,) + small_chip.shape, F32), small_chip[None], (chip, 0, 0))
    small_sems = exchange_start(small_chip, small_land, dx, "gather_small_start", mode="gather")
    dx = small_sems[4]

    while len(siblings) < len(scatters):
        dx = finish(dx)

    per_layer = {}
    for layer, grp, part, other, send_sems, recv_sems in siblings:
        part, other = exchange_wait(part, other, send_sems, recv_sems, dx, f"sibling_wait_{layer}{grp}",
                                    mode="sibling")
        (total,) = row_fwd("sum_cores", lambda a, b: (a + b,), [part, other], [], [(D, F32)],
                           tile=_tile(part.shape[0], 512, 16))
        for name, g in unpack_group_reduced(total, layer, grp).items():
            per_layer.setdefault(name, {})[layer] = g
    grads = {name: jnp.stack([d[k] for k in sorted(d)]) for name, d in per_layer.items()}

    send_sems, recv_sems, small_chip, small_land, _ = small_sems
    _, small_land = exchange_wait(small_chip, small_land, send_sems, recv_sems, dx, "gather_small_wait", mode="gather")
    g_s, rest = unpack_small(sum_slabs(small_land, "sum_small"), full_shapes)
    (loss11,) = whole("loss_sum", lambda t: (jnp.sum(jnp.sum(t, axis=1, keepdims=True), axis=0, keepdims=True),),
                      [rest[:D // 128]], [((1, 1), F32)])
    for n, (axis, width) in SHARDED_SMALL.items():
        g_s[n] = lax.dynamic_slice_in_dim(g_s[n], chip * width, width, axis=axis)
    grads.update(g_s)

    delta, new_m, new_v = {}, {}, {}
    for n in SMALL:
        shape = p[n].shape
        two = (1, shape[0]) if len(shape) == 1 else (int(np.prod(shape[:-1])), shape[-1])
        outs = whole("adam_" + n, f_adam, [t.reshape(two) for t in (p[n], g_s[n], m[n], v[n])], [(two, F32)] * 3)
        delta[n], new_m[n], new_v[n] = (o.reshape(shape) for o in outs)
    for name in BIG:
        if name in ADAM_TRANSPOSED:
            t = lambda a: jnp.swapaxes(a, 1, 2)
            outs = adam(t(p[name]), grads[name], t(m[name]), t(v[name]), "adam_" + name)
            delta[name], new_m[name], new_v[name] = (t(o) for o in outs)
            grads[name] = t(grads[name])
        else:
            delta[name], new_m[name], new_v[name] = adam(p[name], grads[name], m[name], v[name], "adam_" + name)

    return (loss11[0, 0], dx[None], *[grads[n] for n in WEIGHTS], *[delta[n] for n in WEIGHTS],
            *[new_m[n] for n in WEIGHTS], *[new_v[n] for n in WEIGHTS])
```

```python
import numpy as np
import jax
import jax.numpy as jnp
from jax import lax
from jax.experimental import pallas as pl
from jax.experimental.pallas import tpu as pltpu

F32 = jnp.float32
BF16 = jnp.bfloat16
MESH = pl.DeviceIdType.MESH

D = 1024
CHUNK = 64
EPS = 1e-6
S5_W = 512
S5_G = 32
S5_C = 16
S5_P = 64
S5_GB = 4
GDN_H = 4
GDN_D = 128
CA_H = 16
CA_D = 64
CA_LEFT = 8
CA_BAND = (CA_LEFT + 1) * CHUNK
CA_PAD = CA_LEFT * CHUNK
MAX_REL = 128
XA_H = 4
XA_D = 256
DEPTH = 4
N_CHIPS = 4
LR, B1, B2, AEPS, WD, STEP = 0.001, 0.9, 0.999, 1e-08, 0.01, 10

VMEM_LIMIT = 56 * 1024 * 1024
ROW_TILE = 512


def _cp(*sem):
    return pltpu.CompilerParams(dimension_semantics=sem, vmem_limit_bytes=VMEM_LIMIT)


def _dg(a, b, ca, cb):
    return lax.dot_general(a.astype(BF16), b.astype(BF16), (((ca,), (cb,)), ((), ())),
                           preferred_element_type=F32)


@jax.custom_vjp
def mm(a, b):
    return _dg(a, b, 1, 0)


@jax.custom_vjp
def mm_nt(a, b):
    return _dg(a, b, 1, 1)


@jax.custom_vjp
def mm_tn(a, b):
    return _dg(a, b, 0, 0)


mm.defvjp(lambda a, b: (mm(a, b), (a, b)), lambda r, g: (mm_nt(g, r[1]), mm_tn(r[0], g)))
mm_nt.defvjp(lambda a, b: (mm_nt(a, b), (a, b)), lambda r, g: (mm(g, r[1]), mm_tn(g, r[0])))
mm_tn.defvjp(lambda a, b: (mm_tn(a, b), (a, b)), lambda r, g: (mm_nt(r[1], g), mm(r[0], g)))


def _split(a):
    hi = a.astype(BF16)
    return hi, (a - hi.astype(F32)).astype(BF16)


def _tri_mm(v, upper):
    T = v.shape[0]
    r = lax.broadcasted_iota(jnp.int32, (T, T), 0)
    c = lax.broadcasted_iota(jnp.int32, (T, T), 1)
    m = ((c >= r) if upper else (r >= c)).astype(BF16)
    hi, lo = _split(v)
    d = lambda u: lax.dot_general(m, u, (((1,), (0,)), ((), ())), preferred_element_type=F32)
    return d(hi) + d(lo)


@jax.custom_vjp
def cumsum_rows(v):
    return _tri_mm(v, False)


cumsum_rows.defvjp(lambda v: (_tri_mm(v, False), None), lambda _, g: (_tri_mm(g, True),))


def _rms(x, g):
    return x * lax.rsqrt(jnp.mean(x * x, axis=-1, keepdims=True) + EPS) * g


def _softmax(s):
    e = jnp.exp(s - lax.stop_gradient(jnp.max(s, axis=-1, keepdims=True)))
    return e / jnp.sum(e, axis=-1, keepdims=True)


def _softplus(x):
    return jnp.maximum(x, 0.0) + jnp.log(1.0 + jnp.exp(-jnp.abs(x)))


def _tile(n, cap, align):
    if n <= cap:
        return n
    best = None
    for d in range(align, cap + 1, align):
        if n % d == 0:
            best = d
    assert best is not None, (n, cap, align)
    return best


def matmul(mode, a, b, name, out_dtype=F32, add=None, norm_out=None, norm_bwd=None):
    if mode == "nn":
        (M, K), (K2, N) = a.shape, b.shape
    elif mode == "nt":
        (M, K), (N, K2) = a.shape, b.shape
    else:
        (K, M), (K2, N) = a.shape, b.shape
    assert K == K2, (mode, a.shape, b.shape)
    if mode != "tn" and norm_out is None and norm_bwd is None:
        tm, tn, tk = _tile(M, 1024, 128), _tile(N, 1024, 128), _tile(K, 2048, 128)
    else:
        tm, tn, tk = _tile(M, 512, 128), _tile(N, 1536, 128), _tile(K, 2048, 128)
    nk = K // tk
    if mode == "nn":
        a_spec = pl.BlockSpec((tm, tk), lambda i, j, k: (i, k))
        b_spec = pl.BlockSpec((tk, tn), lambda i, j, k: (k, j))
        dn = (((1,), (0,)), ((), ()))
    elif mode == "nt":
        a_spec = pl.BlockSpec((tm, tk), lambda i, j, k: (i, k))
        b_spec = pl.BlockSpec((tn, tk), lambda i, j, k: (j, k))
        dn = (((1,), (1,)), ((), ()))
    else:
        a_spec = pl.BlockSpec((tk, tm), lambda i, j, k: (k, i))
        b_spec = pl.BlockSpec((tk, tn), lambda i, j, k: (k, j))
        dn = (((0,), (0,)), ((), ()))
    o_spec = pl.BlockSpec((tm, tn), lambda i, j, k: (i, j))
    has_add = add is not None
    g_spec = pl.BlockSpec((1, tn), lambda i, j, k: (0, j))
    extra, extra_specs, out_shapes, out_specs = [], [], [jax.ShapeDtypeStruct((M, N), out_dtype)], [o_spec]
    sem = ("parallel", "parallel", "arbitrary")
    if norm_out is not None:
        assert tn == N
        extra, extra_specs = [norm_out], [g_spec]
        out_shapes.append(jax.ShapeDtypeStruct((M, N), BF16))
        out_specs.append(o_spec)
    if norm_bwd is not None:
        assert tn == N
        x_in, g_in, res_in = norm_bwd
        extra, extra_specs = [x_in, g_in, res_in], [o_spec, g_spec, o_spec]
        out_shapes.append(jax.ShapeDtypeStruct((1, N), F32))
        out_specs.append(g_spec)
        out_shapes.append(jax.ShapeDtypeStruct((M, N), BF16))
        out_specs.append(o_spec)
        sem = ("arbitrary", "arbitrary", "arbitrary")
    n_in = 2 + int(has_add) + len(extra)
    n_out = len(out_shapes)

    def body(*refs):
        a_ref, b_ref = refs[0], refs[1]
        add_ref = refs[2] if has_add else None
        extra_refs = refs[2 + int(has_add):n_in]
        o_ref = refs[n_in]
        acc_ref = refs[-1]
        i = pl.program_id(0)
        p = lax.dot_general(a_ref[...].astype(BF16), b_ref[...].astype(BF16), dn,
                            preferred_element_type=F32)

        def finish(total):
            if has_add:
                total = total + add_ref[...]
            if norm_out is not None:
                o_ref[...] = total.astype(o_ref.dtype)
                refs[n_in + 1][...] = _rms(total, extra_refs[0][...]).astype(BF16)
            elif norm_bwd is not None:
                x_ref, g_ref, res_ref = extra_refs
                _, vjp = jax.vjp(f_norm_res, x_ref[...], g_ref[...])
                dx, dg = vjp((total, res_ref[...]))
                o_ref[...] = dx
                refs[n_in + 2][...] = dx.astype(BF16)
                dg_ref = refs[n_in + 1]

                @pl.when(i == 0)
                def _():
                    dg_ref[...] = dg

                @pl.when(i > 0)
                def _():
                    dg_ref[...] += dg
            else:
                o_ref[...] = total.astype(o_ref.dtype)

        if nk == 1:
            finish(p)
        else:
            k = pl.program_id(2)

            @pl.when(k == 0)
            def _():
                acc_ref[...] = p

            @pl.when(k > 0)
            def _():
                acc_ref[...] += p

            @pl.when(k == nk - 1)
            def _():
                finish(acc_ref[...])

    ins = [a, b] + ([add] if has_add else []) + extra
    specs = [a_spec, b_spec] + ([o_spec] if has_add else []) + extra_specs
    out = pl.pallas_call(
        body, name=name, grid=(M // tm, N // tn, nk), in_specs=specs, out_specs=out_specs,
        out_shape=out_shapes, scratch_shapes=[pltpu.VMEM((tm, tn), F32)],
        compiler_params=_cp(*sem),
    )(*ins)
    return out[0] if n_out == 1 else out


def ffn_in(h, wg, wu):
    (S, K), F = h.shape, wg.shape[0]
    tm, tn = _tile(S, 1024, 128), _tile(F, 1536, 128)
    dn = (((1,), (1,)), ((), ()))

    def body(h_ref, wg_ref, wu_ref, g_ref, u_ref, act_ref):
        a = h_ref[...].astype(BF16)
        g = lax.dot_general(a, wg_ref[...].astype(BF16), dn, preferred_element_type=F32)
        u = lax.dot_general(a, wu_ref[...].astype(BF16), dn, preferred_element_type=F32)
        g_ref[...] = g.astype(BF16)
        u_ref[...] = u.astype(BF16)
        act_ref[...] = f_swiglu(g, u)[0].astype(BF16)

    w_spec = pl.BlockSpec((tn, K), lambda i, j: (j, 0))
    o_spec = pl.BlockSpec((tm, tn), lambda i, j: (i, j))
    sd = jax.ShapeDtypeStruct
    return pl.pallas_call(
        body, name="ffn_in", grid=(S // tm, F // tn),
        in_specs=[pl.BlockSpec((tm, K), lambda i, j: (i, 0)), w_spec, w_spec], out_specs=[o_spec] * 3,
        out_shape=[sd((S, F), BF16), sd((S, F), BF16), sd((S, F), BF16)],
        compiler_params=_cp("parallel", "parallel"),
    )(h, wg, wu)


def ffn_dact(dy, wd, gate, up):
    (S, K), F = dy.shape, wd.shape[0]
    tm, tn = _tile(S, 1024, 128), _tile(F, 1536, 128)
    dn = (((1,), (1,)), ((), ()))

    def body(dy_ref, wd_ref, g_ref, u_ref, dg_ref, du_ref):
        dact = lax.dot_general(dy_ref[...].astype(BF16), wd_ref[...].astype(BF16), dn, preferred_element_type=F32)
        _, vjp = jax.vjp(f_swiglu, g_ref[...].astype(F32), u_ref[...].astype(F32))
        dg, du = vjp((dact,))
        dg_ref[...] = dg.astype(BF16)
        du_ref[...] = du.astype(BF16)

    o_spec = pl.BlockSpec((tm, tn), lambda i, j: (i, j))
    sd = jax.ShapeDtypeStruct
    return pl.pallas_call(
        body, name="ffn_dact", grid=(S // tm, F // tn),
        in_specs=[pl.BlockSpec((tm, K), lambda i, j: (i, 0)), pl.BlockSpec((tn, K), lambda i, j: (j, 0)),
                  o_spec, o_spec],
        out_specs=[o_spec] * 2, out_shape=[sd((S, F), BF16)] * 2,
        compiler_params=_cp("parallel", "parallel"),
    )(dy, wd, gate, up)


def _row_spec(arr, tile):
    if isinstance(arr, tuple):
        a, w, cb = arr[:3]
        ro = (arr[3] // tile) if len(arr) > 3 else 0
        assert len(arr) < 4 or arr[3] % tile == 0
        return a, pl.BlockSpec((tile, w), lambda i, cb=cb, ro=ro: (i + ro, cb)), (tile, w)
    if arr.ndim == 3:
        d0, _, d2 = arr.shape
        return arr, pl.BlockSpec((d0, tile, d2), lambda i: (0, i, 0)), (d0, tile, d2)
    return arr, pl.BlockSpec((tile, arr.shape[1]), lambda i: (i, 0)), (tile, arr.shape[1])


def _full_spec(arr):
    nd = arr.ndim
    return pl.BlockSpec(arr.shape, lambda i, nd=nd: (0,) * nd)


def _n_rows(arr):
    a = arr[0] if isinstance(arr, tuple) else arr
    return a.shape[1] if a.ndim == 3 else a.shape[0]


def _row_out_shape(shape_tail, n, dtype):
    if isinstance(shape_tail, tuple):
        d0, d2 = shape_tail
        return (jax.ShapeDtypeStruct((d0, n, d2), dtype),
                lambda tile: pl.BlockSpec((d0, tile, d2), lambda i: (0, i, 0)))
    return (jax.ShapeDtypeStruct((n, shape_tail), dtype),
            lambda tile: pl.BlockSpec((tile, shape_tail), lambda i: (i, 0)))


def _f32(v):
    return v.astype(F32) if v.dtype == BF16 else v


def row_fwd(name, fn, rows, fulls, outs, tile=ROW_TILE):
    n = _n_rows(rows[0])
    tile = min(tile, n)
    assert n % tile == 0, (name, n, tile)
    rs = [_row_spec(r, tile) for r in rows]
    os_ = [_row_out_shape(w, n, dt) for w, dt in outs]
    nr, nf = len(rows), len(fulls)

    def body(*refs):
        vals = [_f32(r[...]) for r in refs[:nr + nf]]
        res = fn(*vals)
        for r, v in zip(refs[nr + nf:], res):
            r[...] = v.astype(r.dtype)

    out = pl.pallas_call(
        body, name=name, grid=(n // tile,),
        in_specs=[s for _, s, _ in rs] + [_full_spec(f) for f in fulls],
        out_specs=[mk(tile) for _, mk in os_], out_shape=[sh for sh, _ in os_],
        compiler_params=_cp("parallel"),
    )(*[a for a, _, _ in rs], *fulls)
    return out


def row_bwd(name, fn, rows, fulls, cts, want_rows, want_fulls, row_dtypes=None, tile=ROW_TILE):
    n = _n_rows(rows[0])
    tile = min(tile, n)
    assert n % tile == 0, (name, n, tile)
    rs = [_row_spec(r, tile) for r in rows]
    cs = [_row_spec(c, tile) for c in cts]
    nr, nf, nc = len(rows), len(fulls), len(cts)
    row_dtypes = row_dtypes or [F32] * len(want_rows)
    out_shapes, out_specs = [], []
    for k, idx in enumerate(want_rows):
        a, _, blk = rs[idx]
        if len(blk) == 3:
            sh, mk = _row_out_shape((blk[0], blk[2]), n, row_dtypes[k])
        else:
            sh, mk = _row_out_shape(blk[1], n, row_dtypes[k])
        out_shapes.append(sh)
        out_specs.append(mk(tile))
    for idx in want_fulls:
        out_shapes.append(jax.ShapeDtypeStruct(fulls[idx].shape, F32))
        out_specs.append(_full_spec(fulls[idx]))
    n_wr = len(want_rows)

    def body(*refs):
        i = pl.program_id(0)
        vals = [_f32(r[...]) for r in refs[:nr + nf]]
        ct_vals = [_f32(r[...]) for r in refs[nr + nf:nr + nf + nc]]
        outs = refs[nr + nf + nc:]
        _, vjp = jax.vjp(fn, *vals)
        grads = vjp(tuple(ct_vals))
        for k, idx in enumerate(want_rows):
            outs[k][...] = grads[idx].astype(outs[k].dtype)
        for k, idx in enumerate(want_fulls):
            o = outs[n_wr + k]
            g = grads[nr + idx]

            @pl.when(i == 0)
            def _(o=o, g=g):
                o[...] = g

            @pl.when(i > 0)
            def _(o=o, g=g):
                o[...] += g

    out = pl.pallas_call(
        body, name=name, grid=(n // tile,),
        in_specs=[s for _, s, _ in rs] + [_full_spec(f) for f in fulls] + [s for _, s, _ in cs],
        out_specs=out_specs, out_shape=out_shapes,
        compiler_params=_cp("arbitrary"),
    )(*[a for a, _, _ in rs], *fulls, *[a for a, _, _ in cs])
    return out


def whole(name, fn, args, outs):
    def body(*refs):
        res = fn(*[r[...] for r in refs[:len(args)]])
        for r, v in zip(refs[len(args):], res):
            r[...] = v.astype(r.dtype)

    return pl.pallas_call(
        body, name=name, out_shape=[jax.ShapeDtypeStruct(s, d) for s, d in outs],
        compiler_params=pltpu.CompilerParams(vmem_limit_bytes=VMEM_LIMIT),
    )(*args)


def f_norm(x, g):
    return (_rms(x, g),)


def f_norm_res(x, g):
    return _rms(x, g), x


def f_swiglu(g, u):
    return (g * jax.nn.sigmoid(g) * u,)


def f_glu(y, w, b):
    h = jax.nn.gelu(y)
    return (h * jax.nn.sigmoid(mm(h, w) + b),)


def f_xattn(q, kv, qg, kg):
    outs = []
    for h in range(XA_H):
        sl = slice(h * XA_D, (h + 1) * XA_D)
        qn = _rms(q[:, sl], qg)
        kn = _rms(kv[:, sl], kg)
        vh = kv[:, D + h * XA_D:D + (h + 1) * XA_D]
        p = _softmax(mm_nt(qn, kn) * (XA_D ** -0.5))
        outs.append(mm(p, vh))
    return (jnp.concatenate(outs, axis=-1),)


def f_adam(w, g, m, v):
    m2 = B1 * m + (1.0 - B1) * g
    v2 = B2 * v + (1.0 - B2) * (g * g)
    m_hat = m2 / (1.0 - B1 ** STEP)
    v_hat = v2 / (1.0 - B2 ** STEP)
    delta = -LR * (m_hat / (jnp.sqrt(v_hat) + AEPS) + WD * w)
    return delta, m2, v2


S5_NTAB, S5_NROW = 4, 6


def _s5_tables(are, aim, ldt):
    T = CHUNK
    dt = jnp.exp(ldt)
    ar, ai = are * dt, aim * dt
    t = lax.broadcasted_iota(jnp.int32, (T, 1), 0).astype(F32)
    mag, inv = jnp.exp(t * ar), jnp.exp(-t * ar)
    cs, sn = jnp.cos(t * ai), jnp.sin(t * ai)
    e_re, e_im = mag * cs, mag * sn
    n_re, n_im = inv * cs, -inv * sn
    l_re, l_im = jnp.exp(ar) * jnp.cos(ai), jnp.exp(ar) * jnp.sin(ai)
    den = are * are + aim * aim
    k_re = ((l_re - 1.0) * are + l_im * aim) / den
    k_im = (l_im * are - (l_re - 1.0) * aim) / den
    tl = float(T - 1)
    m_re, m_im = jnp.exp(tl * ar) * jnp.cos(tl * ai), jnp.exp(tl * ar) * jnp.sin(tl * ai)
    return (e_re, e_im, n_re, n_im), (l_re, l_im, k_re, k_im, m_re, m_im)


def _s5_chunk(u, sre, sim, tabs, rows, b_re, b_im, c_re, c_im, dv):
    e_re, e_im, n_re, n_im = tabs
    l_re, l_im, k_re, k_im, m_re, m_im = rows
    x_re, x_im = mm(u, b_re), mm(u, b_im)
    bu_re = k_re * x_re - k_im * x_im
    bu_im = k_re * x_im + k_im * x_re
    v_re = bu_re * n_re - bu_im * n_im
    v_im = bu_re * n_im + bu_im * n_re
    p_re = l_re * sre - l_im * sim
    p_im = l_re * sim + l_im * sre
    w_re = cumsum_rows(v_re) + p_re
    w_im = cumsum_rows(v_im) + p_im
    s_re = e_re * w_re - e_im * w_im
    s_im = e_re * w_im + e_im * w_re
    y = mm(s_re, c_re) - mm(s_im, c_im) + dv * u
    z_re = jnp.sum(v_re, axis=0, keepdims=True) + p_re
    z_im = jnp.sum(v_im, axis=0, keepdims=True) + p_im
    return y, m_re * z_re - m_im * z_im, m_re * z_im + m_im * z_re


def _s5_fill_tables(are_ref, aim_ref, ldt_ref, tab, row):
    for g in range(S5_GB):
        ls = slice(512 * g, 512 * (g + 1))
        tabs, rows = _s5_tables(are_ref[:, ls], aim_ref[:, ls], ldt_ref[:, ls])
        for k, t in enumerate(tabs):
            tab[k, :, ls] = t
        for k, r in enumerate(rows):
            row[k:k + 1, ls] = r


def _s5_read_tables(tab, row, ls):
    return (tuple(tab[k, :, ls] for k in range(S5_NTAB)), tuple(row[k:k + 1, ls] for k in range(S5_NROW)))


def _s5_specs(nc, rev):
    T = CHUNK

    def ci(c):
        return nc - 1 - c if rev else c

    u_spec = pl.BlockSpec((T, S5_W), lambda c: (ci(c), 0))
    p_spec = pl.BlockSpec((1, S5_G * S5_P), lambda c: (0, 0))
    b_spec = pl.BlockSpec((S5_GB, 128, 512), lambda c: (0, 0, 0))
    c_spec = pl.BlockSpec((S5_GB, 512, 128), lambda c: (0, 0, 0))
    d_spec = pl.BlockSpec((1, S5_W), lambda c: (0, 0))
    st_spec = pl.BlockSpec((None, 2, S5_G * S5_P), lambda c: (ci(c), 0, 0))
    return u_spec, p_spec, b_spec, c_spec, d_spec, st_spec


def s5_fwd(pm, are, aim, ldt, b_re, b_im, c_re, c_im, dv):
    S = pm.shape[0]
    nc = S // CHUNK
    u_spec, p_spec, b_spec, c_spec, d_spec, st_spec = _s5_specs(nc, False)

    def body(u_ref, are_ref, aim_ref, ldt_ref, bre_ref, bim_ref, cre_ref, cim_ref, dv_ref,
             y_ref, st_ref, state, tab, row):
        c = pl.program_id(0)

        @pl.when(c == 0)
        def _():
            state[...] = jnp.zeros_like(state)
            _s5_fill_tables(are_ref, aim_ref, ldt_ref, tab, row)

        st_ref[...] = state[...]
        for g in range(S5_GB):
            lu, ls = slice(128 * g, 128 * (g + 1)), slice(512 * g, 512 * (g + 1))
            tabs, rows = _s5_read_tables(tab, row, ls)
            y, e_re, e_im = _s5_chunk(u_ref[:, lu], state[0:1, ls], state[1:2, ls], tabs, rows,
                                      bre_ref[g], bim_ref[g], cre_ref[g], cim_ref[g], dv_ref[:, lu])
            y_ref[:, lu] = y
            state[0:1, ls] = e_re
            state[1:2, ls] = e_im

    n_state = S5_G * S5_P
    return pl.pallas_call(
        body, name="s5_fwd", grid=(nc,),
        in_specs=[u_spec, p_spec, p_spec, p_spec, b_spec, b_spec, c_spec, c_spec, d_spec],
        out_specs=[u_spec, st_spec],
        out_shape=[jax.ShapeDtypeStruct((S, S5_W), F32), jax.ShapeDtypeStruct((nc, 2, n_state), F32)],
        scratch_shapes=[pltpu.VMEM((2, n_state), F32), pltpu.VMEM((S5_NTAB, CHUNK, n_state), F32),
                        pltpu.VMEM((8, n_state), F32)],
        compiler_params=_cp("arbitrary"),
    )(pm, are, aim, ldt, b_re, b_im, c_re, c_im, dv)


def s5_bwd(pm, st, dy, are, aim, ldt, b_re, b_im, c_re, c_im, dv):
    S = pm.shape[0]
    nc = S // CHUNK
    u_spec, p_spec, b_spec, c_spec, d_spec, st_spec = _s5_specs(nc, True)

    def body(u_ref, st_ref, dy_ref, are_ref, aim_ref, ldt_ref, bre_ref, bim_ref, cre_ref, cim_ref, dv_ref,
             du_ref, dare_ref, daim_ref, dldt_ref, dbre_ref, dbim_ref, dcre_ref, dcim_ref, ddv_ref,
             dstate, tab, row, dtab, drow):
        c = pl.program_id(0)

        @pl.when(c == 0)
        def _():
            dstate[...] = jnp.zeros_like(dstate)
            dtab[...] = jnp.zeros_like(dtab)
            drow[...] = jnp.zeros_like(drow)
            _s5_fill_tables(are_ref, aim_ref, ldt_ref, tab, row)

        for g in range(S5_GB):
            lu, ls = slice(128 * g, 128 * (g + 1)), slice(512 * g, 512 * (g + 1))
            every = slice(None)
            tabs, rows = _s5_read_tables(tab, row, ls)
            args = (u_ref[:, lu], st_ref[0:1, ls], st_ref[1:2, ls], tabs, rows,
                    bre_ref[g], bim_ref[g], cre_ref[g], cim_ref[g], dv_ref[:, lu])
            _, vjp = jax.vjp(_s5_chunk, *args)
            gr = vjp((dy_ref[:, lu], dstate[0:1, ls], dstate[1:2, ls]))
            du_ref[:, lu] = gr[0]
            dstate[0:1, ls] = gr[1]
            dstate[1:2, ls] = gr[2]
            for k, t in enumerate(gr[3]):
                dtab[k, :, ls] += t
            for k, r in enumerate(gr[4]):
                drow[k:k + 1, ls] += r
            accs = ((dbre_ref, (g,)), (dbim_ref, (g,)), (dcre_ref, (g,)), (dcim_ref, (g,)), (ddv_ref, (every, lu)))
            for (o, idx), gv in zip(accs, gr[5:]):
                @pl.when(c == 0)
                def _(o=o, idx=idx, gv=gv):
                    o[idx] = gv

                @pl.when(c > 0)
                def _(o=o, idx=idx, gv=gv):
                    o[idx] += gv

        @pl.when(c == nc - 1)
        def _():
            for g in range(S5_GB):
                ls = slice(512 * g, 512 * (g + 1))
                _, vjp = jax.vjp(_s5_tables, are_ref[:, ls], aim_ref[:, ls], ldt_ref[:, ls])
                dtabs, drows = _s5_read_tables(dtab, drow, ls)
                ga, gi, gl = vjp((dtabs, drows))
                dare_ref[:, ls] = ga
                daim_ref[:, ls] = gi
                dldt_ref[:, ls] = gl

    n_state = S5_G * S5_P
    return pl.pallas_call(
        body, name="s5_bwd", grid=(nc,),
        in_specs=[u_spec, st_spec, u_spec, p_spec, p_spec, p_spec, b_spec, b_spec, c_spec, c_spec, d_spec],
        out_specs=[u_spec, p_spec, p_spec, p_spec, b_spec, b_spec, c_spec, c_spec, d_spec],
        out_shape=[jax.ShapeDtypeStruct((S, S5_W), F32)] + [jax.ShapeDtypeStruct((1, n_state), F32)] * 3
        + [jax.ShapeDtypeStruct((S5_GB, 128, 512), F32)] * 2 + [jax.ShapeDtypeStruct((S5_GB, 512, 128), F32)] * 2
        + [jax.ShapeDtypeStruct((1, S5_W), F32)],
        scratch_shapes=[pltpu.VMEM((2, n_state), F32), pltpu.VMEM((S5_NTAB, CHUNK, n_state), F32),
                        pltpu.VMEM((8, n_state), F32), pltpu.VMEM((S5_NTAB, CHUNK, n_state), F32),
                        pltpu.VMEM((8, n_state), F32)],
        compiler_params=_cp("arbitrary"),
    )(pm, st, dy, are, aim, ldt, b_re, b_im, c_re, c_im, dv)


def conv_fwd(pm, w):
    S = pm.shape[0]

    def body(x_ref, w_ref, o_ref, pad):
        x = x_ref[...]
        pad[0:8, :] = jnp.zeros((8, 128), F32)
        pad[8:, :] = x
        y = (w_ref[3:4, :] * x + w_ref[2:3, :] * pad[7:7 + S, :] + w_ref[1:2, :] * pad[6:6 + S, :]
             + w_ref[0:1, :] * pad[5:5 + S, :])
        o_ref[...] = y * jax.nn.sigmoid(y)

    return pl.pallas_call(
        body, name="conv_fwd", grid=(12,),
        in_specs=[pl.BlockSpec((S, 128), lambda j: (0, 4 + j)), pl.BlockSpec((4, 128), lambda j: (0, j))],
        out_specs=pl.BlockSpec((S, 128), lambda j: (0, j)),
        out_shape=jax.ShapeDtypeStruct((S, 1536), F32),
        scratch_shapes=[pltpu.VMEM((S + 8, 128), F32)],
        compiler_params=_cp("parallel"),
    )(pm, w)


def conv_bwd(pm, w, dout):
    S = pm.shape[0]

    def body(x_ref, w_ref, do_ref, dx_ref, dw_ref, pad, dpad):
        x = x_ref[...]
        pad[0:8, :] = jnp.zeros((8, 128), F32)
        pad[8:, :] = x
        xs = [pad[5:5 + S, :], pad[6:6 + S, :], pad[7:7 + S, :], x]
        y = w_ref[0:1, :] * xs[0] + w_ref[1:2, :] * xs[1] + w_ref[2:3, :] * xs[2] + w_ref[3:4, :] * xs[3]
        sg = jax.nn.sigmoid(y)
        dy = do_ref[...] * (sg + y * sg * (1.0 - sg))
        dpad[0:S, :] = dy
        dpad[S:, :] = jnp.zeros((8, 128), F32)
        dx_ref[...] = (w_ref[3:4, :] * dy + w_ref[2:3, :] * dpad[1:1 + S, :] + w_ref[1:2, :] * dpad[2:2 + S, :]
                       + w_ref[0:1, :] * dpad[3:3 + S, :])
        for i in range(4):
            dw_ref[i:i + 1, :] = jnp.sum(dy * xs[i], axis=0, keepdims=True)

    return pl.pallas_call(
        body, name="conv_bwd", grid=(12,),
        in_specs=[pl.BlockSpec((S, 128), lambda j: (0, 4 + j)), pl.BlockSpec((4, 128), lambda j: (0, j)),
                  pl.BlockSpec((S, 128), lambda j: (0, j))],
        out_specs=[pl.BlockSpec((S, 128), lambda j: (0, j)), pl.BlockSpec((4, 128), lambda j: (0, j))],
        out_shape=[jax.ShapeDtypeStruct((S, 1536), F32), jax.ShapeDtypeStruct((4, 1536), F32)],
        scratch_shapes=[pltpu.VMEM((S + 8, 128), F32), pltpu.VMEM((S + 8, 128), F32)],
        compiler_params=_cp("parallel"),
    )(pm, w, dout)


GDN_SUP = 4
GDN_ROWS = GDN_SUP * CHUNK


@jax.custom_vjp
def _saved_inverse(a, x):
    return x


_saved_inverse.defvjp(lambda a, x: (x, x),
                      lambda x, g: (-mm_tn(x, mm_nt(g, x)), jnp.zeros_like(x)))


def _gdn_chunk(q, k, v, gate, al, bl, alog, dtb, og, state, inv=None, want_inv=False):
    R = q.shape[0]
    r = lax.broadcasted_iota(jnp.int32, (R, R), 0)
    c = lax.broadcasted_iota(jnp.int32, (R, R), 1)
    same = (r // CHUNK) == (c // CHUNK)
    eye = (r == c).astype(F32)
    strict, causal, upper = same & (r > c), same & (r >= c), same & (r <= c)
    qn = q * lax.rsqrt(jnp.sum(q * q, axis=-1, keepdims=True) + EPS) * (GDN_D ** -0.5)
    kn = k * lax.rsqrt(jnp.sum(k * k, axis=-1, keepdims=True) + EPS)
    beta = jnp.sum(eye * jax.nn.sigmoid(bl), axis=1, keepdims=True)
    g_row = -jnp.exp(alog) * _softplus(al + dtb)
    g = jnp.sum(eye * g_row, axis=1, keepdims=True)
    gc_col = jnp.sum(jnp.where(causal, g_row, 0.0), axis=1, keepdims=True)
    gc_row = jnp.sum(jnp.where(upper, g, 0.0), axis=0, keepdims=True)
    gtot = jnp.sum(jnp.where(same, g_row, 0.0), axis=1, keepdims=True)
    gamma = jnp.exp(gc_col)
    diff = gc_col - gc_row
    d_strict = jnp.where(strict, jnp.exp(jnp.where(strict, diff, 0.0)), 0.0)
    d_causal = jnp.where(causal, jnp.exp(jnp.where(causal, diff, 0.0)), 0.0)
    a = beta * mm_nt(kn, kn) * d_strict
    if inv is None:
        p = -a
        x = eye + p
        for _ in range(5):
            p = mm(p, p)
            x = x + mm(x, p)
    else:
        x = _saved_inverse(a, inv)
    u_new = mm(x, beta * v)
    w_k = mm(x, (beta * gamma) * kn)
    qk = mm_nt(qn, kn) * d_causal
    q_g = qn * gamma
    k_tail = kn * jnp.exp(gtot - gc_col)
    ws, os_ = [], []
    for i in range(R // CHUNK):
        rows = slice(CHUNK * i, CHUNK * (i + 1))
        w_i = u_new[rows] - mm(w_k[rows], state)
        os_.append(mm(q_g[rows], state))
        decay = jnp.exp(jnp.sum(g[rows], axis=0, keepdims=True))
        state = decay * state + mm_tn(k_tail[rows], w_i)
        ws.append(w_i)
    o = jnp.concatenate(os_, axis=0) + mm(qk, jnp.concatenate(ws, axis=0))
    out = _rms(o, og) * (gate * jax.nn.sigmoid(gate))
    return (out, state, x) if want_inv else (out, state)


def _gdn_specs(nc, rev):
    def ci(c):
        return nc - 1 - c if rev else c

    def blk(cb):
        return pl.BlockSpec((GDN_ROWS, 512), lambda c: (ci(c), cb))

    col = lambda n: pl.BlockSpec((n, GDN_ROWS), lambda c: (0, ci(c)))
    sc = pl.BlockSpec((GDN_H, 1, 1), lambda c: (0, 0, 0))
    og = pl.BlockSpec((1, 128), lambda c: (0, 0))
    st = pl.BlockSpec((GDN_H, None, 128, 128), lambda c: (0, ci(c), 0, 0))
    return blk, col, sc, og, st


def gdn_fwd(qkvc, pm, abt, alog, dtb, og):
    S = qkvc.shape[0]
    nc = S // GDN_ROWS
    blk, col, sc, ogs, st = _gdn_specs(nc, False)

    def body(q_ref, k_ref, v_ref, gate_ref, ab_ref, alog_ref, dtb_ref, og_ref, o_ref, st_ref, inv_ref, state):
        c = pl.program_id(0)

        @pl.when(c == 0)
        def _():
            state[...] = jnp.zeros_like(state)

        st_ref[...] = state[...]
        for h in range(GDN_H):
            sl = slice(GDN_D * h, GDN_D * (h + 1))
            out, new_state, inv = _gdn_chunk(
                q_ref[:, sl], k_ref[:, sl], v_ref[:, sl], gate_ref[:, sl], ab_ref[h:h + 1, :], ab_ref[GDN_H + h:GDN_H + h + 1, :],
                alog_ref[h], dtb_ref[h], og_ref[...], state[h], want_inv=True)
            o_ref[:, sl] = out
            state[h] = new_state
            inv_ref[h] = inv

    inv_spec = pl.BlockSpec((GDN_H, None, GDN_ROWS, GDN_ROWS), lambda c: (0, c, 0, 0))
    return pl.pallas_call(
        body, name="gdn_fwd", grid=(nc,),
        in_specs=[blk(0), blk(1), blk(2), blk(4), col(2 * GDN_H), sc, sc, ogs],
        out_specs=[blk(0), st, inv_spec],
        out_shape=[jax.ShapeDtypeStruct((S, 512), F32), jax.ShapeDtypeStruct((GDN_H, nc, 128, 128), F32),
                   jax.ShapeDtypeStruct((GDN_H, nc, GDN_ROWS, GDN_ROWS), F32)],
        scratch_shapes=[pltpu.VMEM((GDN_H, 128, 128), F32)],
        compiler_params=_cp("arbitrary"),
    )(qkvc, qkvc, qkvc, pm, abt, alog, dtb, og)


def gdn_bwd(qkvc, pm, abt, alog, dtb, og, st, inv, dout):
    S = qkvc.shape[0]
    nc = S // GDN_ROWS
    blk, col, sc, ogs, sts = _gdn_specs(nc, True)

    def body(q_ref, k_ref, v_ref, gate_ref, ab_ref, alog_ref, dtb_ref, og_ref, st_ref, inv_ref, do_ref,
             dqkv_ref, dgate_ref, dal_ref, dbl_ref, dalog_ref, ddtb_ref, dog_ref, dstate):
        c = pl.program_id(0)

        @pl.when(c == 0)
        def _():
            dstate[...] = jnp.zeros_like(dstate)

        for h in range(GDN_H):
            sl = slice(GDN_D * h, GDN_D * (h + 1))
            args = (q_ref[:, sl], k_ref[:, sl], v_ref[:, sl], gate_ref[:, sl], ab_ref[h:h + 1, :], ab_ref[GDN_H + h:GDN_H + h + 1, :],
                    alog_ref[h], dtb_ref[h], og_ref[...], st_ref[h])
            inv_h = inv_ref[h]
            _, vjp = jax.vjp(lambda *a, inv_h=inv_h: _gdn_chunk(*a, inv=inv_h), *args)
            g = vjp((do_ref[:, sl], dstate[h]))
            for part in range(3):
                dqkv_ref[:, slice(512 * part + sl.start, 512 * part + sl.stop)] = g[part]
            dgate_ref[:, sl] = g[3]
            dal_ref[h:h + 1, :] = g[4]
            dbl_ref[h:h + 1, :] = g[5]
            dstate[h] = g[9]
            for o, gv in zip((dalog_ref, ddtb_ref, dog_ref), g[6:9]):
                @pl.when(c == 0)
                def _(o=o, gv=gv, h=h):
                    o[h] = gv

                @pl.when(c > 0)
                def _(o=o, gv=gv, h=h):
                    o[h] += gv

    ogo = pl.BlockSpec((GDN_H, 1, 128), lambda c: (0, 0, 0))
    inv_spec = pl.BlockSpec((GDN_H, None, GDN_ROWS, GDN_ROWS), lambda c: (0, nc - 1 - c, 0, 0))
    sd = jax.ShapeDtypeStruct
    return pl.pallas_call(
        body, name="gdn_bwd", grid=(nc,),
        in_specs=[blk(0), blk(1), blk(2), blk(4), col(2 * GDN_H), sc, sc, ogs, sts, inv_spec, blk(0)],
        out_specs=[pl.BlockSpec((GDN_ROWS, 1536), lambda c: (nc - 1 - c, 0)), blk(0), col(GDN_H), col(GDN_H),
                   sc, sc, ogo],
        out_shape=[sd((S, 1536), F32), sd((S, 512), F32)] + [sd((GDN_H, S), F32)] * 2 + [sd((GDN_H, 1, 1), F32)] * 2
        + [sd((GDN_H, 1, 128), F32)],
        scratch_shapes=[pltpu.VMEM((GDN_H, 128, 128), F32)],
        compiler_params=_cp("arbitrary"),
    )(qkvc, qkvc, qkvc, pm, abt, alog, dtb, og, st, inv, dout)


HG = 8
HG_LANES = HG * CA_D


def _group_mean_raw(y):
    r = lax.broadcasted_iota(jnp.int32, (128, 128), 0)
    c = lax.broadcasted_iota(jnp.int32, (128, 128), 1)
    g = jnp.where((r // CA_D) == (c // CA_D), 1.0 / CA_D, 0.0).astype(BF16)
    d = lambda u: lax.dot_general(u, g, (((1,), (0,)), ((), ())), preferred_element_type=F32)
    outs = []
    for j in range(y.shape[1] // 128):
        hi, lo = _split(y[:, 128 * j:128 * (j + 1)])
        outs.append(d(hi) + d(lo))
    return jnp.concatenate(outs, axis=1)


@jax.custom_vjp
def group_mean(y):
    return _group_mean_raw(y)


group_mean.defvjp(lambda y: (_group_mean_raw(y), None), lambda _, g: (_group_mean_raw(g),))


def f_headnorm(t, g):
    return (t * lax.rsqrt(group_mean(t * t) + EPS) * g,)


def _cattn_chunk(q, kb, vb, bias, valid):
    lane = lax.broadcasted_iota(jnp.int32, (1, 128), 1)
    m0 = (lane < CA_D).astype(F32)
    m1 = 1.0 - m0
    pairs = range(HG // 2)
    sl = [slice(128 * p, 128 * (p + 1)) for p in pairs]
    q2 = [jnp.concatenate([q[:, s] * m0, q[:, s] * m1], axis=0) for s in sl]
    sc = [mm_nt(q2[p], kb[:, sl[p]]) * (CA_D ** -0.5) + bias[sl[p]] for p in pairs]
    pr = [_softmax(jnp.where(valid, s, -1e30)) for s in sc]
    o2 = [mm(pr[p], vb[:, sl[p]]) for p in pairs]
    return jnp.concatenate([o[:CHUNK] * m0 + o[CHUNK:] * m1 for o in o2], axis=1)


def _cattn_valid(c):
    pos = lax.broadcasted_iota(jnp.int32, (1, CA_BAND), 1) + c * CHUNK
    return pos >= CA_PAD


def _cattn_specs(S):
    q_spec = pl.BlockSpec((CHUNK, HG_LANES), lambda h, c: (c, h))
    kv_spec = pl.BlockSpec((S + CA_PAD, HG_LANES), lambda h, c: (0, h))
    b_spec = pl.BlockSpec((HG * CHUNK, CA_BAND), lambda h, c: (h, 0))
    return q_spec, kv_spec, b_spec


def cattn_fwd(qn, kp, vp, bias):
    S = qn.shape[0]
    nc = S // CHUNK
    q_spec, kv_spec, b_spec = _cattn_specs(S)

    def body(q_ref, k_ref, v_ref, b_ref, o_ref):
        c = pl.program_id(1)
        start = pl.multiple_of(c * CHUNK, CHUNK)
        kb = k_ref[pl.ds(start, CA_BAND), :]
        vb = v_ref[pl.ds(start, CA_BAND), :]
        o_ref[...] = _cattn_chunk(q_ref[...], kb, vb, b_ref[...], _cattn_valid(c)).astype(o_ref.dtype)

    return pl.pallas_call(
        body, name="cattn_fwd", grid=(CA_H // HG, nc), in_specs=[q_spec, kv_spec, kv_spec, b_spec],
        out_specs=q_spec, out_shape=jax.ShapeDtypeStruct((S, D), BF16),
        compiler_params=_cp("parallel", "arbitrary"),
    )(qn, kp, vp, bias)


def kv_prep(qkv, kg):
    S = qkv.shape[0]
    tile = ROW_TILE
    lead = CA_PAD // tile

    def body(k_ref, v_ref, g_ref, kp_ref, vp_ref):
        i = pl.program_id(0)

        @pl.when(i < lead)
        def _():
            kp_ref[...] = jnp.zeros_like(kp_ref)
            vp_ref[...] = jnp.zeros_like(vp_ref)

        @pl.when(i >= lead)
        def _():
            kp_ref[...] = f_headnorm(k_ref[...], g_ref[...])[0].astype(BF16)
            vp_ref[...] = v_ref[...].astype(BF16)

    src = lambda cb: pl.BlockSpec((tile, D), lambda i, cb=cb: (jnp.maximum(i - lead, 0), cb))
    out = pl.BlockSpec((tile, D), lambda i: (i, 0))
    return pl.pallas_call(
        body, name="kv_prep", grid=((S + CA_PAD) // tile,),
        in_specs=[src(1), src(2), pl.BlockSpec((1, D), lambda i: (0, 0))], out_specs=[out, out],
        out_shape=[jax.ShapeDtypeStruct((S + CA_PAD, D), BF16)] * 2,
        compiler_params=_cp("parallel"),
    )(qkv, qkv, kg)


def cattn_bwd(qn, kp, vp, bias, do):
    S = qn.shape[0]
    nc = S // CHUNK
    q_spec, kv_spec, b_spec = _cattn_specs(S)

    def body(q_ref, k_ref, v_ref, b_ref, do_ref, dq_ref, dk_ref, dv_ref, db_ref):
        c = pl.program_id(1)

        @pl.when(c == 0)
        def _():
            dk_ref[...] = jnp.zeros_like(dk_ref)
            dv_ref[...] = jnp.zeros_like(dv_ref)
            db_ref[...] = jnp.zeros_like(db_ref)

        start = pl.multiple_of(c * CHUNK, CHUNK)
        kb = k_ref[pl.ds(start, CA_BAND), :].astype(F32)
        vb = v_ref[pl.ds(start, CA_BAND), :].astype(F32)
        valid = _cattn_valid(c)
        _, vjp = jax.vjp(lambda q, k, v, b: _cattn_chunk(q, k, v, b, valid), q_ref[...], kb, vb, b_ref[...])
        dq, dk, dv, db = vjp(do_ref[...])
        dq_ref[...] = dq
        dk_ref[pl.ds(start, CA_BAND), :] += dk
        dv_ref[pl.ds(start, CA_BAND), :] += dv
        db_ref[...] += db

    sd = jax.ShapeDtypeStruct
    return pl.pallas_call(
        body, name="cattn_bwd", grid=(CA_H // HG, nc), in_specs=[q_spec, kv_spec, kv_spec, b_spec, q_spec],
        out_specs=[q_spec, kv_spec, kv_spec, b_spec],
        out_shape=[sd((S, D), F32), sd((S + CA_PAD, D), F32), sd((S + CA_PAD, D), F32),
                   sd((CA_H * CHUNK, CA_BAND), F32)],
        compiler_params=_cp("parallel", "arbitrary"),
    )(qn, kp, vp, bias, do)


SKEW_W = CA_BAND + CHUNK


def rel_bias_grad(dbias):
    padded = jnp.pad(dbias, ((0, 0), (0, 0), (CHUNK, 0)))
    flat = jnp.pad(padded.reshape(CA_H, CHUNK * SKEW_W), ((0, 0), (0, CHUNK)))
    skew = flat.reshape(CA_H, CHUNK, SKEW_W + 1)

    first_near = SKEW_W - CHUNK - MAX_REL

    def fn(t):
        colsum = jnp.sum(t, axis=1, keepdims=True)
        j = lax.broadcasted_iota(jnp.int32, colsum.shape, 2)
        far = jnp.sum(jnp.where(j < first_near, colsum, 0.0), axis=2, keepdims=True)
        return (colsum + jnp.where(j == first_near, far, 0.0),)

    (colsum,) = whole("relbias_sum", fn, [skew], [((CA_H, 1, SKEW_W + 1), F32)])
    near = colsum[:, 0, first_near:SKEW_W][:, ::-1]
    return jnp.concatenate([jnp.zeros((CA_H, CHUNK + 1), F32), near], axis=1)


def rel_bias_expand(rb):
    near = rb[:, CHUNK + 1:][:, ::-1]
    far = jnp.broadcast_to(rb[:, 2 * MAX_REL:], (CA_H, SKEW_W - CHUNK - MAX_REL))
    t = jnp.concatenate([far, near, jnp.zeros((CA_H, 1), rb.dtype)], axis=1)
    rows = jnp.tile(t, (1, CHUNK))[:, :CHUNK * SKEW_W].reshape(CA_H, CHUNK, SKEW_W)
    return rows[:, :, CHUNK:]


def loss_head(y, target):
    S = y.shape[0]
    tile = min(ROW_TILE, S)

    def body(y_ref, t_ref, dy_ref, acc_ref, dyb_ref):
        i = pl.program_id(0)
        e = y_ref[...] - t_ref[...]
        dy_ref[...] = e * (1.0 / D)
        dyb_ref[...] = (e * (1.0 / D)).astype(BF16)
        part = jnp.sum(e * e, axis=0, keepdims=True) * (0.5 / D)

        @pl.when(i == 0)
        def _():
            acc_ref[...] = part

        @pl.when(i > 0)
        def _():
            acc_ref[...] += part

    row = pl.BlockSpec((tile, D), lambda i: (i, 0))
    return pl.pallas_call(
        body, name="loss_head", grid=(S // tile,), in_specs=[row, row],
        out_specs=[row, pl.BlockSpec((1, D), lambda i: (0, 0)), row],
        out_shape=[jax.ShapeDtypeStruct((S, D), F32), jax.ShapeDtypeStruct((1, D), F32),
                   jax.ShapeDtypeStruct((S, D), BF16)],
        compiler_params=_cp("arbitrary"),
    )(y, target)


ANY = pl.BlockSpec(memory_space=pl.ANY)


HBM = pl.BlockSpec(memory_space=pltpu.HBM)
SEM = pl.BlockSpec(memory_space=pltpu.SEMAPHORE)
EFFECT = pltpu.SideEffectType.DATAFLOW_SIDE_EFFECTING


def _chip_copies(src_ref, land_ref, send_sems, recv_sems, mode):
    x, y, c = lax.axis_index("x"), lax.axis_index("y"), lax.axis_index("c")
    me = 2 * x + y

    def copy(j, s, d, dev):
        return pltpu.make_async_remote_copy(src_ref=s, dst_ref=d, send_sem=send_sems.at[j], recv_sem=recv_sems.at[j],
                                            device_id=dev, device_id_type=MESH)

    if mode == "sibling":
        cp = copy(0, src_ref, land_ref, (x, y, 1 - c))
        return [(cp, cp)]
    out = []
    for j, (px, py) in enumerate([(1 - x, y), (x, 1 - y), (1 - x, 1 - y)]):
        peer = 2 * px + py
        if mode == "scatter":
            pairs, dev = ((src_ref.at[peer], land_ref.at[j]), (src_ref.at[me], land_ref.at[j])), (px, py, c)
        elif mode == "gather":
            pairs, dev = ((src_ref, land_ref.at[me]), (src_ref, land_ref.at[peer])), (px, py, c)
        else:
            half = land_ref.shape[1] // 2
            mine = pl.ds(pl.multiple_of(c * half, 16), half)
            theirs = pl.ds(pl.multiple_of((1 - c) * half, 16), half)
            if mode == "gather_half":
                pairs = ((src_ref.at[mine], land_ref.at[me, mine]), (src_ref.at[mine], land_ref.at[peer, mine]))
                dev = (px, py, c)
            else:
                pairs = ((land_ref.at[peer, mine],) * 2, (land_ref.at[peer, theirs],) * 2)
                dev = (x, y, 1 - c)
        out.append(tuple(copy(j, s, d, dev) for s, d in pairs))
    return out


def exchange_start(src, land, carry, name, mode):
    def body(src_ref, land_ref, carry_ref, send_sems, recv_sems, src_out, land_out, carry_out):
        for send, _ in _chip_copies(src_ref, land_ref, send_sems, recv_sems, mode):
            send.start()

    hbm = lambda a: pltpu.HBM(a.shape, a.dtype)
    n = 1 if mode == "sibling" else 3
    return pl.pallas_call(
        body, name=name,
        out_shape=(pltpu.SemaphoreType.DMA((n,)), pltpu.SemaphoreType.DMA((n,)), hbm(src), hbm(land), hbm(carry)),
        in_specs=(HBM, HBM, HBM), out_specs=(SEM, SEM, HBM, HBM, HBM),
        input_output_aliases={0: 2, 1: 3, 2: 4},
        compiler_params=pltpu.CompilerParams(has_side_effects=EFFECT),
    )(pltpu.with_memory_space_constraint(src, pltpu.HBM), pltpu.with_memory_space_constraint(land, pltpu.HBM),
      pltpu.with_memory_space_constraint(carry, pltpu.HBM))


def exchange_wait(src, land, send_sems, recv_sems, after, name, mode):
    def body(src_ref, land_ref, send_sems_ref, recv_sems_ref, after_ref, src_out, land_out):
        for send, recv in _chip_copies(src_ref, land_ref, send_sems_ref, recv_sems_ref, mode):
            send.wait_send()
            recv.wait_recv()

    hbm = lambda a: pltpu.HBM(a.shape, a.dtype)
    return pl.pallas_call(
        body, name=name, out_shape=(hbm(src), hbm(land)),
        in_specs=(HBM, HBM, SEM, SEM, ANY), out_specs=(HBM, HBM), input_output_aliases={0: 0, 1: 1},
        compiler_params=pltpu.CompilerParams(has_side_effects=EFFECT),
    )(src, land, send_sems, recv_sems, after)


def sibling_exchange(srcs, name):
    n = len(srcs)

    def body(*refs):
        src_refs, out_refs, send_sems, recv_sems = refs[:n], refs[n:2 * n], refs[2 * n], refs[2 * n + 1]
        x, y, c = lax.axis_index("x"), lax.axis_index("y"), lax.axis_index("c")
        copies = [pltpu.make_async_remote_copy(src_ref=s, dst_ref=o, send_sem=send_sems.at[k], recv_sem=recv_sems.at[k],
                                               device_id=(x, y, 1 - c), device_id_type=MESH)
                  for k, (s, o) in enumerate(zip(src_refs, out_refs))]
        for cp in copies:
            cp.start()
        for cp in copies:
            cp.wait()

    return pl.pallas_call(
        body, name=name, in_specs=[ANY] * n, out_specs=[ANY] * n,
        out_shape=[jax.ShapeDtypeStruct(s.shape, s.dtype) for s in srcs],
        scratch_shapes=[pltpu.SemaphoreType.DMA((n,)), pltpu.SemaphoreType.DMA((n,))],
    )(*srcs)


def sum_own_slabs(src, chip, land, name, tile=512):
    _, R, C = src.shape
    n = land.shape[0]
    tile = _tile(R, tile, 16)

    def body(chip_ref, o_ref, t_ref, out_ref):
        acc = o_ref[...].astype(F32)
        for s in range(n):
            acc = acc + t_ref[s].astype(F32)
        out_ref[...] = acc

    return pl.pallas_call(
        body, name=name,
        grid_spec=pltpu.PrefetchScalarGridSpec(
            num_scalar_prefetch=1, grid=(R // tile,),
            in_specs=[pl.BlockSpec((None, tile, C), lambda i, chip_ref: (chip_ref[0], i, 0)),
                      pl.BlockSpec((n, tile, C), lambda i, chip_ref: (0, i, 0))],
            out_specs=pl.BlockSpec((tile, C), lambda i, chip_ref: (i, 0))),
        out_shape=jax.ShapeDtypeStruct((R, C), F32),
        compiler_params=_cp("parallel"),
    )(jnp.reshape(chip, (1,)).astype(jnp.int32), src, land)


def sum_slabs(t, name, tile=512):
    n, R, C = t.shape
    tile = _tile(R, tile, 16)

    def body(t_ref, o_ref):
        acc = t_ref[0].astype(F32)
        for s in range(1, n):
            acc = acc + t_ref[s].astype(F32)
        o_ref[...] = acc

    return pl.pallas_call(
        body, name=name, grid=(R // tile,), in_specs=[pl.BlockSpec((n, tile, C), lambda i: (0, i, 0))],
        out_specs=pl.BlockSpec((tile, C), lambda i: (i, 0)), out_shape=jax.ShapeDtypeStruct((R, C), F32),
        compiler_params=_cp("parallel"),
    )(t)


def _pad_rows(a, mult=16):
    r = (-a.shape[0]) % mult
    return jnp.pad(a, ((0, r), (0, 0))) if r else a


BIG = ["ab_w_in", "c_w_qkv", "xa_w_kv", "f_w_gate", "f_w_up", "ab_w_out", "c_w_out", "xa_w_q", "xa_w_out",
       "f_w_down", "s5_w_glu"]


GROUPS = ("B", "A")
GROUP_ROW_MULT = 64


def group_spec(layer, grp):
    i = layer // 2
    if grp == "A":
        return [("xa_w_kv", layer, True, 512), ("xa_w_q", layer, False, 256), ("xa_w_out", layer, False, 256),
                ("f_w_gate", layer, True, 704), ("f_w_up", layer, True, 704), ("f_w_down", layer, False, 704)]
    if layer % 2 == 0:
        return [("ab_w_in", i, True, 642), ("ab_w_out", i, False, 256), ("s5_w_glu", i, False, 64)]
    return [("c_w_qkv", i, True, 768), ("c_w_out", i, False, 256)]


def _seg_rows(rows):
    return rows + ((-rows) % 16)


def _f32_rows(a):
    bits = lax.bitcast_convert_type(a.reshape(-1), BF16).reshape(-1)
    return jnp.pad(bits, (0, 16 * D - bits.shape[0])).reshape(16, D)


def pack_group_shards(p, layer, grp):
    segs = []
    for name, idx, transposed, rows in group_spec(layer, grp):
        w = p[name][idx]
        if transposed:
            w = w.T
        segs.append(_pad_rows(w.astype(BF16).reshape(-1, D)))
    if grp == "B":
        small = p["gdn_conv_w"] if layer % 2 == 0 else p["c_norm_g"]
        segs.append(_f32_rows(small[layer // 2]))
    return _pad_rows(jnp.concatenate(segs, axis=0), GROUP_ROW_MULT)


def unpack_group_gathered(g, layer, grp):
    out, off = {}, 0
    for name, idx, transposed, rows in group_spec(layer, grp):
        seg = g[:, off:off + rows]
        if name == "s5_w_glu":
            out[name] = seg.reshape(N_CHIPS * 128, 512)
        else:
            out[name] = seg.reshape(N_CHIPS * rows, D)
        off += _seg_rows(rows)
    if grp == "B":
        n = 4 * 384 if layer % 2 == 0 else 256
        bits = g[:, off:off + 16].reshape(N_CHIPS, -1)[:, :2 * n].reshape(N_CHIPS, n, 2)
        small = lax.bitcast_convert_type(bits, F32)
        if layer % 2 == 0:
            out["gdn_conv_w"] = jnp.swapaxes(small.reshape(N_CHIPS, 4, 384), 0, 1).reshape(4, 1536)
        else:
            out["c_norm_g"] = small.reshape(D)
    return out


def pack_group_grads(gr, layer, grp):
    segs = []
    for name, idx, transposed, rows in group_spec(layer, grp):
        w = gr[name].astype(BF16)
        seg = w.reshape(N_CHIPS, rows, D)
        r = (-rows) % 16
        if r:
            seg = jnp.pad(seg, ((0, 0), (0, r), (0, 0)))
        segs.append(seg)
    out = jnp.concatenate(segs, axis=1)
    return jnp.pad(out, ((0, 0), (0, (-out.shape[1]) % GROUP_ROW_MULT), (0, 0)))


def unpack_group_reduced(g, layer, grp):
    out, off = {}, 0
    for name, idx, transposed, rows in group_spec(layer, grp):
        seg = g[off:off + rows]
        if name == "s5_w_glu":
            out[name] = seg.reshape(128, 512)
        else:
            out[name] = seg.T if (transposed and name not in ADAM_TRANSPOSED) else seg
        off += _seg_rows(rows)
    return out


ADAM_TRANSPOSED = ("f_w_gate", "f_w_up")


SMALL = ["ab_norm_g", "s5_a_re", "s5_a_im", "s5_log_dt", "s5_b_re", "s5_b_im", "s5_c_re", "s5_c_im", "s5_d",
         "s5_b_glu", "gdn_conv_w", "gdn_a_log", "gdn_dt_bias", "gdn_out_norm_g", "c_norm_g", "c_q_norm_g",
         "c_k_norm_g", "c_rel_bias", "mem_norm_g", "xa_norm_g", "xa_q_norm_g", "xa_k_norm_g", "f_norm_g"]


def _lane_rows(a):
    flat = a.reshape(-1).astype(F32)
    return jnp.pad(flat, (0, (-flat.shape[0]) % 1024)).reshape(-1, 128)


def pack_small(d, extra=None):
    parts = [_lane_rows(d[n]) for n in SMALL]
    if extra is not None:
        parts.append(_lane_rows(extra))
    rows = jnp.concatenate(parts, axis=0)
    return jnp.pad(rows, ((0, (-rows.shape[0]) % 128), (0, 0)))


def unpack_small(rows, shapes):
    out, off = {}, 0
    for n in SMALL:
        sz = int(np.prod(shapes[n]))
        k = 8 * -(-sz // 1024)
        out[n] = rows[off:off + k].reshape(-1)[:sz].reshape(shapes[n])
        off += k
    return out, rows[off:]


def _s5_blockdiag_b(b):
    bt = jnp.swapaxes(b, 1, 2).reshape(S5_GB, 8, S5_C, S5_P)
    eye = jnp.eye(8, dtype=b.dtype)
    return jnp.einsum("bgcp,gh->bgchp", bt, eye).reshape(S5_GB, 8 * S5_C, 8 * S5_P)


def _s5_blockdiag_c(c):
    ct = jnp.swapaxes(c, 1, 2).reshape(S5_GB, 8, S5_P, S5_C)
    eye = jnp.eye(8, dtype=c.dtype)
    return jnp.einsum("bgpc,gh->bgphc", ct, eye).reshape(S5_GB, 8 * S5_P, 8 * S5_C)


def _s5_diag_b(db):
    t = db.reshape(S5_GB, 8, S5_C, 8, S5_P)
    t = jnp.transpose(t, (0, 2, 4, 1, 3)).reshape(S5_GB, S5_C, S5_P, 64)
    d = t[..., ::9]
    return jnp.transpose(d, (0, 3, 2, 1)).reshape(S5_G, S5_P, S5_C)


def _s5_diag_c(dc):
    t = dc.reshape(S5_GB, 8, S5_P, 8, S5_C)
    t = jnp.transpose(t, (0, 2, 4, 1, 3)).reshape(S5_GB, S5_P, S5_C, 64)
    d = t[..., ::9]
    return jnp.transpose(d, (0, 3, 2, 1)).reshape(S5_G, S5_C, S5_P)


def local_step(x, mem, target, p, wsrc, gsink, wpre=lambda layer, grp, carry: carry):
    S = x.shape[0]
    row2 = lambda a: a.reshape(1, -1)
    saved = []
    (mem_n,) = row_fwd("mem_norm", f_norm, [mem], [row2(p["mem_norm_g"])], [(D, BF16)])
    gs = {}

    for layer in range(DEPTH):
        i = layer // 2
        w = wsrc(layer, "B", x)
        sv = {"x0": x, "wB": w}
        if layer % 2 == 0:
            (h,) = row_fwd("norm", f_norm, [x], [row2(p["ab_norm_g"][i])], [(D, BF16)])
            w_in = w["ab_w_in"]
            pm = matmul("nt", h, w_in[:2560], "proj_main")
            abt = matmul("nt", w_in[2560:], h, "proj_ab")
            s5p = dict(
                are=p["s5_a_re"][i].reshape(1, -1), aim=p["s5_a_im"][i].reshape(1, -1),
                ldt=jnp.broadcast_to(p["s5_log_dt"][i][:, None], (S5_G, S5_P)).reshape(1, -1),
                b_re=_s5_blockdiag_b(p["s5_b_re"][i]), b_im=_s5_blockdiag_b(p["s5_b_im"][i]),
                c_re=_s5_blockdiag_c(p["s5_c_re"][i]), c_im=_s5_blockdiag_c(p["s5_c_im"][i]),
                dv=p["s5_d"][i].reshape(1, -1))
            y5, st5 = s5_fwd(pm, **s5p)
            (a_out,) = row_fwd("glu", f_glu, [y5], [w["s5_w_glu"], row2(p["s5_b_glu"][i])], [(S5_W, F32)])
            conv_w = w["gdn_conv_w"]
            qkvc = conv_fwd(pm, conv_w)
            alog = p["gdn_a_log"][i].reshape(GDN_H, 1, 1)
            dtb = p["gdn_dt_bias"][i].reshape(GDN_H, 1, 1)
            og = row2(p["gdn_out_norm_g"][i])
            b_out, stg, inv = gdn_fwd(qkvc, pm, abt, alog, dtb, og)
            cat = wpre(layer, "A", jnp.concatenate([a_out, b_out], axis=1))
            x, hq = matmul("nn", cat, w["ab_w_out"], "mix_out", add=x, norm_out=row2(p["xa_norm_g"][layer]))
            sv.update(h=h, pm=pm, s5p=s5p, y5=y5, st5=st5, qkvc=qkvc, abt=abt, alog=alog, dtb=dtb, og=og,
                      stg=stg, inv=inv, cat=cat, conv_w=conv_w)
        else:
            (h,) = row_fwd("norm", f_norm, [x], [row2(w["c_norm_g"])], [(D, BF16)])
            qkv = matmul("nt", h, w["c_w_qkv"], "proj_qkv")
            qg = jnp.tile(row2(p["c_q_norm_g"][i]), (1, CA_H))
            kg = jnp.tile(row2(p["c_k_norm_g"][i]), (1, CA_H))
            (qn,) = row_fwd("headnorm_q", f_headnorm, [(qkv, D, 0)], [qg], [(D, F32)])
            kp, vp = kv_prep(qkv, kg)
            bias = rel_bias_expand(p["c_rel_bias"][i]).reshape(CA_H * CHUNK, CA_BAND)
            o = wpre(layer, "A", cattn_fwd(qn, kp, vp, bias))
            x, hq = matmul("nn", o, w["c_w_out"], "mix_out", add=x, norm_out=row2(p["xa_norm_g"][layer]))
            sv.update(h=h, qkv=qkv, qg=qg, kg=kg, qn=qn, kp=kp, vp=vp, bias=bias, o=o)
        sv["x1"] = x
        w = wsrc(layer, "A", x)
        sv["wA"] = w
        qx = matmul("nn", hq, w["xa_w_q"], "xa_q")
        kv = matmul("nt", mem_n, w["xa_w_kv"], "xa_kv")
        xqg, xkg = row2(p["xa_q_norm_g"][layer]), row2(p["xa_k_norm_g"][layer])
        (ox,) = row_fwd("xattn", f_xattn, [qx], [kv, xqg, xkg], [(D, BF16)])
        x, hf = matmul("nn", ox, w["xa_w_out"], "xa_out", add=x, norm_out=row2(p["f_norm_g"][layer]))
        sv.update(hq=hq, qx=qx, kv=kv, ox=ox)
        sv["x2"] = x
        gate, up, act = ffn_in(hf, w["f_w_gate"], w["f_w_up"])
        if layer + 1 < DEPTH:
            act = wpre(layer + 1, "B", act)
        x = matmul("nn", act, w["f_w_down"], "ffn_down", add=x)
        sv.update(hf=hf, gate=gate, up=up, act=act)
        saved.append(sv)

    dx, loss_vec, dxb = loss_head(x, target)

    dmem_n = None
    for layer in reversed(range(DEPTH)):
        i = layer // 2
        sv = saved[layer]
        w, gw = sv["wA"], {}
        dgate, dup = ffn_dact(dxb, w["f_w_down"], sv["gate"], sv["up"])
        gw["f_w_down"] = matmul("tn", sv["act"], dxb, "ffn_dwd", out_dtype=BF16)
        gw["f_w_gate"] = matmul("tn", dgate, sv["hf"], "ffn_dwg", out_dtype=BF16)
        gw["f_w_up"] = matmul("tn", dup, sv["hf"], "ffn_dwu", out_dtype=BF16)
        dh = matmul("nn", dgate, w["f_w_gate"], "ffn_dhg")
        dx, dg, dxb = matmul("nn", dup, w["f_w_up"], "ffn_dhu", add=dh,
                        norm_bwd=(sv["x2"], row2(p["f_norm_g"][layer]), dx))
        gs.setdefault("f_norm_g", [None] * DEPTH)[layer] = dg[0]
        do = matmul("nt", dxb, w["xa_w_out"], "xa_do")
        gw["xa_w_out"] = matmul("tn", sv["ox"], dxb, "xa_dwo", out_dtype=BF16)
        xqg, xkg = row2(p["xa_q_norm_g"][layer]), row2(p["xa_k_norm_g"][layer])
        dqx, dkv, dqg, dkg = row_bwd("xattn_bwd", f_xattn, [sv["qx"]], [sv["kv"], xqg, xkg], [do],
                                     [0], [0, 1, 2])
        gs.setdefault("xa_q_norm_g", [None] * DEPTH)[layer] = dqg[0]
        gs.setdefault("xa_k_norm_g", [None] * DEPTH)[layer] = dkg[0]
        gw["xa_w_q"] = matmul("tn", sv["hq"], dqx, "xa_dwq", out_dtype=BF16)
        gw["xa_w_kv"] = matmul("tn", dkv, mem_n, "xa_dwkv", out_dtype=BF16)
        dx = gsink(layer, "A", gw, dx)
        dmem_n = matmul("nn", dkv, w["xa_w_kv"], "xa_dmem", add=dmem_n)
        dx, dg, dxb = matmul("nt", dqx, w["xa_w_q"], "xa_dhq", norm_bwd=(sv["x1"], row2(p["xa_norm_g"][layer]), dx))
        gs.setdefault("xa_norm_g", [None] * DEPTH)[layer] = dg[0]
        w, gw = sv["wB"], {}
        if layer % 2 == 0:
            dcat = matmul("nt", dxb, w["ab_w_out"], "mix_dcat")
            gw["ab_w_out"] = matmul("tn", sv["cat"], dxb, "mix_dwo", out_dtype=BF16)
            dy5, dwglu, dbglu = row_bwd("glu_bwd", f_glu, [sv["y5"]], [w["s5_w_glu"], row2(p["s5_b_glu"][i])],
                                        [(dcat, S5_W, 0)], [0], [0, 1])
            gw["s5_w_glu"] = dwglu
            gs.setdefault("s5_b_glu", [None] * 2)[i] = dbglu[0]
            s5p = sv["s5p"]
            du, dare, daim, dldt, dbre, dbim, dcre, dcim, ddv = s5_bwd(sv["pm"], sv["st5"], dy5, **s5p)
            (dldt_g,) = whole("s5_dt_sum", lambda t: (jnp.sum(t, axis=1, keepdims=True),),
                              [dldt.reshape(S5_G, S5_P)], [((S5_G, 1), F32)])
            for nme, val in (("s5_a_re", dare.reshape(S5_G, S5_P)), ("s5_a_im", daim.reshape(S5_G, S5_P)),
                             ("s5_log_dt", dldt_g[:, 0]), ("s5_b_re", _s5_diag_b(dbre)),
                             ("s5_b_im", _s5_diag_b(dbim)), ("s5_c_re", _s5_diag_c(dcre)),
                             ("s5_c_im", _s5_diag_c(dcim)), ("s5_d", ddv.reshape(S5_G, S5_C))):
                gs.setdefault(nme, [None] * 2)[i] = val
            dqkvc, dgate, dal, dbl, dalog, ddtb, dog = gdn_bwd(
                sv["qkvc"], sv["pm"], sv["abt"], sv["alog"], sv["dtb"], sv["og"], sv["stg"], sv["inv"],
                dcat[:, S5_W:])
            (dog_s,) = whole("gdn_og_sum", lambda t: (jnp.sum(t, axis=0, keepdims=True),),
                             [dog.reshape(GDN_H, GDN_D)], [((1, GDN_D), F32)])
            gs.setdefault("gdn_out_norm_g", [None] * 2)[i] = dog_s[0]
            gs.setdefault("gdn_a_log", [None] * 2)[i] = dalog.reshape(GDN_H)
            gs.setdefault("gdn_dt_bias", [None] * 2)[i] = ddtb.reshape(GDN_H)
            dqkv, dconv = conv_bwd(sv["pm"], sv["conv_w"], dqkvc)
            gs.setdefault("gdn_conv_w", [None] * 2)[i] = dconv
            dpm = jnp.concatenate([du, dqkv, dgate], axis=1).astype(BF16)
            dabt = jnp.concatenate([dal, dbl], axis=0)
            dw_main = matmul("tn", dpm, sv["h"], "proj_dw", out_dtype=BF16)
            dw_ab = matmul("nn", dabt, sv["h"], "proj_ab_dw", out_dtype=BF16)
            gw["ab_w_in"] = jnp.concatenate([dw_main, dw_ab], axis=0)
            dx = gsink(layer, "B", gw, dx)
            w_in = w["ab_w_in"]
            dh = matmul("nn", dpm, w_in[:2560], "proj_dh")
            dx, dg, dxb = matmul("tn", dabt, w_in[2560:], "proj_ab_dh", add=dh,
                            norm_bwd=(sv["x0"], row2(p["ab_norm_g"][i]), dx))
            gs.setdefault("ab_norm_g", [None] * 2)[i] = dg[0]
        else:
            do = matmul("nt", dxb, w["c_w_out"], "mix_dcat")
            gw["c_w_out"] = matmul("tn", sv["o"], dxb, "mix_dwo", out_dtype=BF16)
            dqn, dkp, dvp, dbias = cattn_bwd(sv["qn"], sv["kp"], sv["vp"], sv["bias"], do)
            gs.setdefault("c_rel_bias", [None] * 2)[i] = rel_bias_grad(dbias.reshape(CA_H, CHUNK, CA_BAND))
            dq, dqg = row_bwd("headnorm_bwd", f_headnorm, [(sv["qkv"], D, 0)], [sv["qg"]], [dqn], [0], [0])
            dk, dkg = row_bwd("headnorm_bwd", f_headnorm, [(sv["qkv"], D, 1)], [sv["kg"]],
                              [(dkp, D, 0, CA_PAD)], [0], [0])
            head_sum = lambda t: (jnp.sum(t, axis=0, keepdims=True),)
            (dqg,) = whole("headgain_sum", head_sum, [dqg.reshape(CA_H, CA_D)], [((1, CA_D), F32)])
            (dkg,) = whole("headgain_sum", head_sum, [dkg.reshape(CA_H, CA_D)], [((1, CA_D), F32)])
            gs.setdefault("c_q_norm_g", [None] * 2)[i] = dqg[0]
            gs.setdefault("c_k_norm_g", [None] * 2)[i] = dkg[0]
            dqkv = jnp.concatenate([dq, dk, dvp[CA_PAD:]], axis=1).astype(BF16)
            gw["c_w_qkv"] = matmul("tn", dqkv, sv["h"], "proj_qkv_dw", out_dtype=BF16)
            dx = gsink(layer, "B", gw, dx)
            dx, dg, dxb = matmul("nn", dqkv, w["c_w_qkv"], "proj_qkv_dh",
                            norm_bwd=(sv["x0"], row2(w["c_norm_g"]), dx))
            gs.setdefault("c_norm_g", [None] * 2)[i] = dg[0]
    (dmg,) = row_bwd("mem_norm_bwd", f_norm, [mem], [row2(p["mem_norm_g"])], [dmem_n], [], [0])
    small = {n: jnp.stack(v) for n, v in gs.items()}
    small["mem_norm_g"] = dmg[0]
    return loss_vec, dx, small


ADAM_BLOCK_BYTES = 3 << 19


def adam(w, g, m, v, name):
    shape = w.shape
    if w.ndim == 3:
        d0, n, d2 = shape
        fits = [t for t in range(8, n + 1, 8) if n % t == 0 and d0 * t * d2 * 4 <= ADAM_BLOCK_BYTES]
        return tuple(row_fwd(name, f_adam, [w, g, m, v], [], [((d0, d2), F32)] * 3, tile=max(fits)))
    cols = shape[-1]
    w2, g2, m2, v2 = (t.reshape(-1, cols) for t in (w, g, m, v))
    rows = w2.shape[0]
    tile = rows if rows <= 512 else _tile(rows, 512, 8)
    outs = row_fwd(name, f_adam, [w2, g2, m2, v2], [], [(cols, F32)] * 3, tile=tile)
    return tuple(o.reshape(shape) for o in outs)


WEIGHTS = ['ab_norm_g', 'ab_w_in', 'ab_w_out', 's5_a_re', 's5_a_im', 's5_log_dt', 's5_b_re', 's5_b_im', 's5_c_re',
           's5_c_im', 's5_d', 's5_w_glu', 's5_b_glu', 'gdn_conv_w', 'gdn_a_log', 'gdn_dt_bias', 'gdn_out_norm_g',
           'c_norm_g', 'c_w_qkv', 'c_w_out', 'c_q_norm_g', 'c_k_norm_g', 'c_rel_bias', 'mem_norm_g', 'xa_norm_g',
           'xa_w_q', 'xa_w_kv', 'xa_w_out', 'xa_q_norm_g', 'xa_k_norm_g', 'f_norm_g', 'f_w_gate', 'f_w_up',
           'f_w_down']
SHARDED_SMALL = {"gdn_conv_w": (2, 384), "c_norm_g": (1, 256)}


def kernel(x, mem, ab_norm_g, ab_w_in, ab_w_out, s5_a_re, s5_a_im, s5_log_dt, s5_b_re, s5_b_im, s5_c_re, s5_c_im, s5_d, s5_w_glu, s5_b_glu, gdn_conv_w, gdn_a_log, gdn_dt_bias, gdn_out_norm_g, c_norm_g, c_w_qkv, c_w_out, c_q_norm_g, c_k_norm_g, c_rel_bias, mem_norm_g, xa_norm_g, xa_w_q, xa_w_kv, xa_w_out, xa_q_norm_g, xa_k_norm_g, f_norm_g, f_w_gate, f_w_up, f_w_down, loss_target, m_ab_norm_g, m_ab_w_in, m_ab_w_out, m_s5_a_re, m_s5_a_im, m_s5_log_dt, m_s5_b_re, m_s5_b_im, m_s5_c_re, m_s5_c_im, m_s5_d, m_s5_w_glu, m_s5_b_glu, m_gdn_conv_w, m_gdn_a_log, m_gdn_dt_bias, m_gdn_out_norm_g, m_c_norm_g, m_c_w_qkv, m_c_w_out, m_c_q_norm_g, m_c_k_norm_g, m_c_rel_bias, m_mem_norm_g, m_xa_norm_g, m_xa_w_q, m_xa_w_kv, m_xa_w_out, m_xa_q_norm_g, m_xa_k_norm_g, m_f_norm_g, m_f_w_gate, m_f_w_up, m_f_w_down, v_ab_norm_g, v_ab_w_in, v_ab_w_out, v_s5_a_re, v_s5_a_im, v_s5_log_dt, v_s5_b_re, v_s5_b_im, v_s5_c_re, v_s5_c_im, v_s5_d, v_s5_w_glu, v_s5_b_glu, v_gdn_conv_w, v_gdn_a_log, v_gdn_dt_bias, v_gdn_out_norm_g, v_c_norm_g, v_c_w_qkv, v_c_w_out, v_c_q_norm_g, v_c_k_norm_g, v_c_rel_bias, v_mem_norm_g, v_xa_norm_g, v_xa_w_q, v_xa_w_kv, v_xa_w_out, v_xa_q_norm_g, v_xa_k_norm_g, v_f_norm_g, v_f_w_gate, v_f_w_up, v_f_w_down):
    args = locals()
    p = {n: args[n] for n in WEIGHTS}
    m = {n: args["m_" + n] for n in WEIGHTS}
    v = {n: args["v_" + n] for n in WEIGHTS}
    chip = 2 * lax.axis_index("x") + lax.axis_index("y")

    carry = x[0]
    gathers = {}
    for layer in range(DEPTH):
        for grp in GROUPS:
            src = pack_group_shards(p, layer, grp)
            land = lax.dynamic_update_slice(lax.empty((N_CHIPS,) + src.shape, BF16), src[None], (chip, 0, 0))
            send_sems, recv_sems, src, land, carry = exchange_start(
                src, land, carry, f"gather_start_{layer}{grp}", mode="gather_half")
            gathers[layer, grp] = (src, land, send_sems, recv_sems)
    forwards = {}

    def wpre(layer, grp, carry):
        if (layer, grp) not in forwards:
            src, land, send_sems, recv_sems = gathers[layer, grp]
            src, land = exchange_wait(src, land, send_sems, recv_sems, carry, f"gather_wait_{layer}{grp}",
                                      mode="gather_half")
            send_sems, recv_sems, src, land, carry = exchange_start(
                src, land, carry, f"forward_start_{layer}{grp}", mode="forward_half")
            forwards[layer, grp] = (src, land, send_sems, recv_sems)
        return carry

    def wsrc(layer, grp, after):
        wpre(layer, grp, after)
        src, land, send_sems, recv_sems = forwards[layer, grp]
        _, land = exchange_wait(src, land, send_sems, recv_sems, after, f"forward_wait_{layer}{grp}",
                                mode="forward_half")
        return unpack_group_gathered(land, layer, grp)

    scatters, siblings = [], []
    LAG = 2

    def finish(carry):
        layer, grp, src, land, send_sems, recv_sems = scatters[len(siblings)]
        src, land = exchange_wait(src, land, send_sems, recv_sems, carry, f"scatter_wait_{layer}{grp}", mode="scatter")
        part = sum_own_slabs(src, chip, land, "sum_chips")
        send_sems, recv_sems, part, other, carry = exchange_start(
            part, lax.empty(part.shape, F32), carry, f"sibling_start_{layer}{grp}", mode="sibling")
        siblings.append((layer, grp, part, other, send_sems, recv_sems))
        return carry

    def gsink(layer, grp, gw, carry):
        src = pack_group_grads(gw, layer, grp)
        land = lax.empty((3,) + src.shape[1:], BF16)
        send_sems, recv_sems, src, land, carry = exchange_start(
            src, land, carry, f"scatter_start_{layer}{grp}", mode="scatter")
        scatters.append((layer, grp, src, land, send_sems, recv_sems))
        if len(scatters) > LAG:
            carry = finish(carry)
        return carry

    carry = wpre(0, "B", carry)
    loss_vec, dx, g_small = local_step(carry, mem[0], loss_target[0], p, wsrc, gsink, wpre)

    full_shapes = {n: ((2, 4, 1536) if n == "gdn_conv_w" else (2, D) if n == "c_norm_g" else p[n].shape)
                   for n in SMALL}
    small_mine = pack_small(g_small, extra=loss_vec)
    (small_other,) = sibling_exchange([small_mine], "sibling_small")
    (small_chip,) = row_fwd("sum_small_cores", lambda a, b: (a + b,), [small_mine, small_other], [], [(128, F32)],
                            tile=128)
    small_land = lax.dynamic_update_slice(lax.empty((N_CHIPS,) + small_chip.shape, F32), small_chip[None], (chip, 0, 0))
    small_sems = exchange_start(small_chip, small_land, dx, "gather_small_start", mode="gather")
    dx = small_sems[4]

    while len(siblings) < len(scatters):
        dx = finish(dx)

    per_layer = {}
    for layer, grp, part, other, send_sems, recv_sems in siblings:
        part, other = exchange_wait(part, other, send_sems, recv_sems, dx, f"sibling_wait_{layer}{grp}",
                                    mode="sibling")
        (total,) = row_fwd("sum_cores", lambda a, b: (a + b,), [part, other], [], [(D, F32)],
                           tile=_tile(part.shape[0], 512, 16))
        for name, g in unpack_group_reduced(total, layer, grp).items():
            per_layer.setdefault(name, {})[layer] = g
    grads = {name: jnp.stack([d[k] for k in sorted(d)]) for name, d in per_layer.items()}

    send_sems, recv_sems, small_chip, small_land, _ = small_sems
    _, small_land = exchange_wait(small_chip, small_land, send_sems, recv_sems, dx, "gather_small_wait", mode="gather")
    g_s, rest = unpack_small(sum_slabs(small_land, "sum_small"), full_shapes)
    (loss11,) = whole("loss_sum", lambda t: (jnp.sum(jnp.sum(t, axis=1, keepdims=True), axis=0, keepdims=True),),
                      [rest[:D // 128]], [((1, 1), F32)])
    for n, (axis, width) in SHARDED_SMALL.items():
        g_s[n] = lax.dynamic_slice_in_dim(g_s[n], chip * width, width, axis=axis)
    grads.update(g_s)

    delta, new_m, new_v = {}, {}, {}
    for n in SMALL:
        shape = p[n].shape
        two = (1, shape[0]) if len(shape) == 1 else (int(np.prod(shape[:-1])), shape[-1])
        outs = whole("adam_" + n, f_adam, [t.reshape(two) for t in (p[n], g_s[n], m[n], v[n])], [(two, F32)] * 3)
        delta[n], new_m[n], new_v[n] = (o.reshape(shape) for o in outs)
    for name in BIG:
        if name in ADAM_TRANSPOSED:
            t = lambda a: jnp.swapaxes(a, 1, 2)
            outs = adam(t(p[name]), grads[name], t(m[name]), t(v[name]), "adam_" + name)
            delta[name], new_m[name], new_v[name] = (t(o) for o in outs)
            grads[name] = t(grads[name])
        else:
            delta[name], new_m[name], new_v[name] = adam(p[name], grads[name], m[name], v[name], "adam_" + name)

    return (loss11[0, 0], dx[None], *[grads[n] for n in WEIGHTS], *[delta[n] for n in WEIGHTS],
            *[new_m[n] for n in WEIGHTS], *[new_v[n] for n in WEIGHTS])
```

```python
import numpy as np
import jax
import jax.numpy as jnp
from jax import lax
from jax.experimental import pallas as pl
from jax.experimental.pallas import tpu as pltpu

F32 = jnp.float32
BF16 = jnp.bfloat16
MESH = pl.DeviceIdType.MESH

D = 1024
CHUNK = 64
EPS = 1e-6
S5_W = 512
S5_G = 32
S5_C = 16
S5_P = 64
S5_GB = 4
GDN_H = 4
GDN_D = 128
CA_H = 16
CA_D = 64
CA_LEFT = 8
CA_BAND = (CA_LEFT + 1) * CHUNK
CA_PAD = CA_LEFT * CHUNK
MAX_REL = 128
XA_H = 4
XA_D = 256
DEPTH = 4
N_CHIPS = 4
LR, B1, B2, AEPS, WD, STEP = 0.001, 0.9, 0.999, 1e-08, 0.01, 10

VMEM_LIMIT = 56 * 1024 * 1024
ROW_TILE = 512


def _cp(*sem):
    return pltpu.CompilerParams(dimension_semantics=sem, vmem_limit_bytes=VMEM_LIMIT)


def _dg(a, b, ca, cb):
    return lax.dot_general(a.astype(BF16), b.astype(BF16), (((ca,), (cb,)), ((), ())),
                           preferred_element_type=F32)


@jax.custom_vjp
def mm(a, b):
    return _dg(a, b, 1, 0)


@jax.custom_vjp
def mm_nt(a, b):
    return _dg(a, b, 1, 1)


@jax.custom_vjp
def mm_tn(a, b):
    return _dg(a, b, 0, 0)


mm.defvjp(lambda a, b: (mm(a, b), (a, b)), lambda r, g: (mm_nt(g, r[1]), mm_tn(r[0], g)))
mm_nt.defvjp(lambda a, b: (mm_nt(a, b), (a, b)), lambda r, g: (mm(g, r[1]), mm_tn(g, r[0])))
mm_tn.defvjp(lambda a, b: (mm_tn(a, b), (a, b)), lambda r, g: (mm_nt(r[1], g), mm(r[0], g)))


def _split(a):
    hi = a.astype(BF16)
    return hi, (a - hi.astype(F32)).astype(BF16)


def _tri_mm(v, upper):
    T = v.shape[0]
    r = lax.broadcasted_iota(jnp.int32, (T, T), 0)
    c = lax.broadcasted_iota(jnp.int32, (T, T), 1)
    m = ((c >= r) if upper else (r >= c)).astype(BF16)
    hi, lo = _split(v)
    d = lambda u: lax.dot_general(m, u, (((1,), (0,)), ((), ())), preferred_element_type=F32)
    return d(hi) + d(lo)


@jax.custom_vjp
def cumsum_rows(v):
    return _tri_mm(v, False)


cumsum_rows.defvjp(lambda v: (_tri_mm(v, False), None), lambda _, g: (_tri_mm(g, True),))


def _rms(x, g):
    return x * lax.rsqrt(jnp.mean(x * x, axis=-1, keepdims=True) + EPS) * g


def _softmax(s):
    e = jnp.exp(s - lax.stop_gradient(jnp.max(s, axis=-1, keepdims=True)))
    return e / jnp.sum(e, axis=-1, keepdims=True)


def _softplus(x):
    return jnp.maximum(x, 0.0) + jnp.log(1.0 + jnp.exp(-jnp.abs(x)))


def _tile(n, cap, align):
    if n <= cap:
        return n
    best = None
    for d in range(align, cap + 1, align):
        if n % d == 0:
            best = d
    assert best is not None, (n, cap, align)
    return best


def matmul(mode, a, b, name, out_dtype=F32, add=None, norm_out=None, norm_bwd=None):
    if mode == "nn":
        (M, K), (K2, N) = a.shape, b.shape
    elif mode == "nt":
        (M, K), (N, K2) = a.shape, b.shape
    else:
        (K, M), (K2, N) = a.shape, b.shape
    assert K == K2, (mode, a.shape, b.shape)
    if mode != "tn" and norm_out is None and norm_bwd is None:
        tm, tn, tk = _tile(M, 1024, 128), _tile(N, 1024, 128), _tile(K, 2816, 128)
    else:
        tm, tn, tk = _tile(M, 512, 128), _tile(N, 1536, 128), _tile(K, 2816, 128)
    nk = K // tk
    if mode == "nn":
        a_spec = pl.BlockSpec((tm, tk), lambda i, j, k: (i, k))
        b_spec = pl.BlockSpec((tk, tn), lambda i, j, k: (k, j))
        dn = (((1,), (0,)), ((), ()))
    elif mode == "nt":
        a_spec = pl.BlockSpec((tm, tk), lambda i, j, k: (i, k))
        b_spec = pl.BlockSpec((tn, tk), lambda i, j, k: (j, k))
        dn = (((1,), (1,)), ((), ()))
    else:
        a_spec = pl.BlockSpec((tk, tm), lambda i, j, k: (k, i))
        b_spec = pl.BlockSpec((tk, tn), lambda i, j, k: (k, j))
        dn = (((0,), (0,)), ((), ()))
    o_spec = pl.BlockSpec((tm, tn), lambda i, j, k: (i, j))
    has_add = add is not None
    g_spec = pl.BlockSpec((1, tn), lambda i, j, k: (0, j))
    extra, extra_specs, out_shapes, out_specs = [], [], [jax.ShapeDtypeStruct((M, N), out_dtype)], [o_spec]
    sem = ("parallel", "parallel", "arbitrary")
    if norm_out is not None:
        assert tn == N
        extra, extra_specs = [norm_out], [g_spec]
        out_shapes.append(jax.ShapeDtypeStruct((M, N), BF16))
        out_specs.append(o_spec)
    if norm_bwd is not None:
        assert tn == N
        x_in, g_in, res_in = norm_bwd
        extra, extra_specs = [x_in, g_in, res_in], [o_spec, g_spec, o_spec]
        out_shapes.append(jax.ShapeDtypeStruct((1, N), F32))
        out_specs.append(g_spec)
        sem = ("arbitrary", "arbitrary", "arbitrary")
    n_in = 2 + int(has_add) + len(extra)
    n_out = len(out_shapes)

    def body(*refs):
        a_ref, b_ref = refs[0], refs[1]
        add_ref = refs[2] if has_add else None
        extra_refs = refs[2 + int(has_add):n_in]
        o_ref = refs[n_in]
        acc_ref = refs[-1]
        i = pl.program_id(0)
        p = lax.dot_general(a_ref[...].astype(BF16), b_ref[...].astype(BF16), dn,
                            preferred_element_type=F32)

        def finish(total):
            if has_add:
                total = total + add_ref[...]
            if norm_out is not None:
                o_ref[...] = total.astype(o_ref.dtype)
                refs[n_in + 1][...] = _rms(total, extra_refs[0][...]).astype(BF16)
            elif norm_bwd is not None:
                x_ref, g_ref, res_ref = extra_refs
                _, vjp = jax.vjp(f_norm_res, x_ref[...], g_ref[...])
                dx, dg = vjp((total, res_ref[...]))
                o_ref[...] = dx
                dg_ref = refs[n_in + 1]

                @pl.when(i == 0)
                def _():
                    dg_ref[...] = dg

                @pl.when(i > 0)
                def _():
                    dg_ref[...] += dg
            else:
                o_ref[...] = total.astype(o_ref.dtype)

        if nk == 1:
            finish(p)
        else:
            k = pl.program_id(2)

            @pl.when(k == 0)
            def _():
                acc_ref[...] = p

            @pl.when(k > 0)
            def _():
                acc_ref[...] += p

            @pl.when(k == nk - 1)
            def _():
                finish(acc_ref[...])

    ins = [a, b] + ([add] if has_add else []) + extra
    specs = [a_spec, b_spec] + ([o_spec] if has_add else []) + extra_specs
    out = pl.pallas_call(
        body, name=name, grid=(M // tm, N // tn, nk), in_specs=specs, out_specs=out_specs,
        out_shape=out_shapes, scratch_shapes=[pltpu.VMEM((tm, tn), F32)],
        compiler_params=_cp(*sem),
    )(*ins)
    return out[0] if n_out == 1 else out


def ffn_in(h, wg, wu):
    (S, K), F = h.shape, wg.shape[0]
    tm, tn = _tile(S, 1024, 128), _tile(F, 1536, 128)
    dn = (((1,), (1,)), ((), ()))

    def body(h_ref, wg_ref, wu_ref, g_ref, u_ref, act_ref):
        a = h_ref[...].astype(BF16)
        g = lax.dot_general(a, wg_ref[...].astype(BF16), dn, preferred_element_type=F32)
        u = lax.dot_general(a, wu_ref[...].astype(BF16), dn, preferred_element_type=F32)
        g_ref[...] = g.astype(BF16)
        u_ref[...] = u.astype(BF16)
        act_ref[...] = f_swiglu(g, u)[0].astype(BF16)

    w_spec = pl.BlockSpec((tn, K), lambda i, j: (j, 0))
    o_spec = pl.BlockSpec((tm, tn), lambda i, j: (i, j))
    sd = jax.ShapeDtypeStruct
    return pl.pallas_call(
        body, name="ffn_in", grid=(S // tm, F // tn),
        in_specs=[pl.BlockSpec((tm, K), lambda i, j: (i, 0)), w_spec, w_spec], out_specs=[o_spec] * 3,
        out_shape=[sd((S, F), BF16), sd((S, F), BF16), sd((S, F), BF16)],
        compiler_params=_cp("parallel", "parallel"),
    )(h, wg, wu)


def ffn_dact(dy, wd, gate, up):
    (S, K), F = dy.shape, wd.shape[0]
    tm, tn = _tile(S, 1024, 128), _tile(F, 1536, 128)
    dn = (((1,), (1,)), ((), ()))

    def body(dy_ref, wd_ref, g_ref, u_ref, dg_ref, du_ref):
        dact = lax.dot_general(dy_ref[...].astype(BF16), wd_ref[...].astype(BF16), dn, preferred_element_type=F32)
        _, vjp = jax.vjp(f_swiglu, g_ref[...].astype(F32), u_ref[...].astype(F32))
        dg, du = vjp((dact,))
        dg_ref[...] = dg.astype(BF16)
        du_ref[...] = du.astype(BF16)

    o_spec = pl.BlockSpec((tm, tn), lambda i, j: (i, j))
    sd = jax.ShapeDtypeStruct
    return pl.pallas_call(
        body, name="ffn_dact", grid=(S // tm, F // tn),
        in_specs=[pl.BlockSpec((tm, K), lambda i, j: (i, 0)), pl.BlockSpec((tn, K), lambda i, j: (j, 0)),
                  o_spec, o_spec],
        out_specs=[o_spec] * 2, out_shape=[sd((S, F), BF16)] * 2,
        compiler_params=_cp("parallel", "parallel"),
    )(dy, wd, gate, up)


def _row_spec(arr, tile):
    if isinstance(arr, tuple):
        a, w, cb = arr[:3]
        ro = (arr[3] // tile) if len(arr) > 3 else 0
        assert len(arr) < 4 or arr[3] % tile == 0
        return a, pl.BlockSpec((tile, w), lambda i, cb=cb, ro=ro: (i + ro, cb)), (tile, w)
    if arr.ndim == 3:
        d0, _, d2 = arr.shape
        return arr, pl.BlockSpec((d0, tile, d2), lambda i: (0, i, 0)), (d0, tile, d2)
    return arr, pl.BlockSpec((tile, arr.shape[1]), lambda i: (i, 0)), (tile, arr.shape[1])


def _full_spec(arr):
    nd = arr.ndim
    return pl.BlockSpec(arr.shape, lambda i, nd=nd: (0,) * nd)


def _n_rows(arr):
    a = arr[0] if isinstance(arr, tuple) else arr
    return a.shape[1] if a.ndim == 3 else a.shape[0]


def _row_out_shape(shape_tail, n, dtype):
    if isinstance(shape_tail, tuple):
        d0, d2 = shape_tail
        return (jax.ShapeDtypeStruct((d0, n, d2), dtype),
                lambda tile: pl.BlockSpec((d0, tile, d2), lambda i: (0, i, 0)))
    return (jax.ShapeDtypeStruct((n, shape_tail), dtype),
            lambda tile: pl.BlockSpec((tile, shape_tail), lambda i: (i, 0)))


def _f32(v):
    return v.astype(F32) if v.dtype == BF16 else v


def row_fwd(name, fn, rows, fulls, outs, tile=ROW_TILE):
    n = _n_rows(rows[0])
    tile = min(tile, n)
    assert n % tile == 0, (name, n, tile)
    rs = [_row_spec(r, tile) for r in rows]
    os_ = [_row_out_shape(w, n, dt) for w, dt in outs]
    nr, nf = len(rows), len(fulls)

    def body(*refs):
        vals = [_f32(r[...]) for r in refs[:nr + nf]]
        res = fn(*vals)
        for r, v in zip(refs[nr + nf:], res):
            r[...] = v.astype(r.dtype)

    out = pl.pallas_call(
        body, name=name, grid=(n // tile,),
        in_specs=[s for _, s, _ in rs] + [_full_spec(f) for f in fulls],
        out_specs=[mk(tile) for _, mk in os_], out_shape=[sh for sh, _ in os_],
        compiler_params=_cp("parallel"),
    )(*[a for a, _, _ in rs], *fulls)
    return out


def row_bwd(name, fn, rows, fulls, cts, want_rows, want_fulls, row_dtypes=None, tile=ROW_TILE):
    n = _n_rows(rows[0])
    tile = min(tile, n)
    assert n % tile == 0, (name, n, tile)
    rs = [_row_spec(r, tile) for r in rows]
    cs = [_row_spec(c, tile) for c in cts]
    nr, nf, nc = len(rows), len(fulls), len(cts)
    row_dtypes = row_dtypes or [F32] * len(want_rows)
    out_shapes, out_specs = [], []
    for k, idx in enumerate(want_rows):
        a, _, blk = rs[idx]
        if len(blk) == 3:
            sh, mk = _row_out_shape((blk[0], blk[2]), n, row_dtypes[k])
        else:
            sh, mk = _row_out_shape(blk[1], n, row_dtypes[k])
        out_shapes.append(sh)
        out_specs.append(mk(tile))
    for idx in want_fulls:
        out_shapes.append(jax.ShapeDtypeStruct(fulls[idx].shape, F32))
        out_specs.append(_full_spec(fulls[idx]))
    n_wr = len(want_rows)

    def body(*refs):
        i = pl.program_id(0)
        vals = [_f32(r[...]) for r in refs[:nr + nf]]
        ct_vals = [_f32(r[...]) for r in refs[nr + nf:nr + nf + nc]]
        outs = refs[nr + nf + nc:]
        _, vjp = jax.vjp(fn, *vals)
        grads = vjp(tuple(ct_vals))
        for k, idx in enumerate(want_rows):
            outs[k][...] = grads[idx].astype(outs[k].dtype)
        for k, idx in enumerate(want_fulls):
            o = outs[n_wr + k]
            g = grads[nr + idx]

            @pl.when(i == 0)
            def _(o=o, g=g):
                o[...] = g

            @pl.when(i > 0)
            def _(o=o, g=g):
                o[...] += g

    out = pl.pallas_call(
        body, name=name, grid=(n // tile,),
        in_specs=[s for _, s, _ in rs] + [_full_spec(f) for f in fulls] + [s for _, s, _ in cs],
        out_specs=out_specs, out_shape=out_shapes,
        compiler_params=_cp("arbitrary"),
    )(*[a for a, _, _ in rs], *fulls, *[a for a, _, _ in cs])
    return out


def whole(name, fn, args, outs):
    def body(*refs):
        res = fn(*[r[...] for r in refs[:len(args)]])
        for r, v in zip(refs[len(args):], res):
            r[...] = v.astype(r.dtype)

    return pl.pallas_call(
        body, name=name, out_shape=[jax.ShapeDtypeStruct(s, d) for s, d in outs],
        compiler_params=pltpu.CompilerParams(vmem_limit_bytes=VMEM_LIMIT),
    )(*args)


def f_norm(x, g):
    return (_rms(x, g),)


def f_norm_res(x, g):
    return _rms(x, g), x


def f_swiglu(g, u):
    return (g * jax.nn.sigmoid(g) * u,)


def f_glu(y, w, b):
    h = jax.nn.gelu(y)
    return (h * jax.nn.sigmoid(mm(h, w) + b),)


def f_xattn(q, kv, qg, kg):
    outs = []
    for h in range(XA_H):
        sl = slice(h * XA_D, (h + 1) * XA_D)
        qn = _rms(q[:, sl], qg)
        kn = _rms(kv[:, sl], kg)
        vh = kv[:, D + h * XA_D:D + (h + 1) * XA_D]
        p = _softmax(mm_nt(qn, kn) * (XA_D ** -0.5))
        outs.append(mm(p, vh))
    return (jnp.concatenate(outs, axis=-1),)


def f_adam(w, g, m, v):
    m2 = B1 * m + (1.0 - B1) * g
    v2 = B2 * v + (1.0 - B2) * (g * g)
    m_hat = m2 / (1.0 - B1 ** STEP)
    v_hat = v2 / (1.0 - B2 ** STEP)
    delta = -LR * (m_hat / (jnp.sqrt(v_hat) + AEPS) + WD * w)
    return delta, m2, v2


S5_NTAB, S5_NROW = 4, 6


def _s5_tables(are, aim, ldt):
    T = CHUNK
    dt = jnp.exp(ldt)
    ar, ai = are * dt, aim * dt
    t = lax.broadcasted_iota(jnp.int32, (T, 1), 0).astype(F32)
    mag, inv = jnp.exp(t * ar), jnp.exp(-t * ar)
    cs, sn = jnp.cos(t * ai), jnp.sin(t * ai)
    e_re, e_im = mag * cs, mag * sn
    n_re, n_im = inv * cs, -inv * sn
    l_re, l_im = jnp.exp(ar) * jnp.cos(ai), jnp.exp(ar) * jnp.sin(ai)
    den = are * are + aim * aim
    k_re = ((l_re - 1.0) * are + l_im * aim) / den
    k_im = (l_im * are - (l_re - 1.0) * aim) / den
    tl = float(T - 1)
    m_re, m_im = jnp.exp(tl * ar) * jnp.cos(tl * ai), jnp.exp(tl * ar) * jnp.sin(tl * ai)
    return (e_re, e_im, n_re, n_im), (l_re, l_im, k_re, k_im, m_re, m_im)


def _s5_chunk(u, sre, sim, tabs, rows, b_re, b_im, c_re, c_im, dv):
    e_re, e_im, n_re, n_im = tabs
    l_re, l_im, k_re, k_im, m_re, m_im = rows
    x_re, x_im = mm(u, b_re), mm(u, b_im)
    bu_re = k_re * x_re - k_im * x_im
    bu_im = k_re * x_im + k_im * x_re
    v_re = bu_re * n_re - bu_im * n_im
    v_im = bu_re * n_im + bu_im * n_re
    p_re = l_re * sre - l_im * sim
    p_im = l_re * sim + l_im * sre
    w_re = cumsum_rows(v_re) + p_re
    w_im = cumsum_rows(v_im) + p_im
    s_re = e_re * w_re - e_im * w_im
    s_im = e_re * w_im + e_im * w_re
    y = mm(s_re, c_re) - mm(s_im, c_im) + dv * u
    z_re = jnp.sum(v_re, axis=0, keepdims=True) + p_re
    z_im = jnp.sum(v_im, axis=0, keepdims=True) + p_im
    return y, m_re * z_re - m_im * z_im, m_re * z_im + m_im * z_re


def _s5_fill_tables(are_ref, aim_ref, ldt_ref, tab, row):
    for g in range(S5_GB):
        ls = slice(512 * g, 512 * (g + 1))
        tabs, rows = _s5_tables(are_ref[:, ls], aim_ref[:, ls], ldt_ref[:, ls])
        for k, t in enumerate(tabs):
            tab[k, :, ls] = t
        for k, r in enumerate(rows):
            row[k:k + 1, ls] = r


def _s5_read_tables(tab, row, ls):
    return (tuple(tab[k, :, ls] for k in range(S5_NTAB)), tuple(row[k:k + 1, ls] for k in range(S5_NROW)))


def _s5_specs(nc, rev):
    T = CHUNK

    def ci(c):
        return nc - 1 - c if rev else c

    u_spec = pl.BlockSpec((T, S5_W), lambda c: (ci(c), 0))
    p_spec = pl.BlockSpec((1, S5_G * S5_P), lambda c: (0, 0))
    b_spec = pl.BlockSpec((S5_GB, 128, 512), lambda c: (0, 0, 0))
    c_spec = pl.BlockSpec((S5_GB, 512, 128), lambda c: (0, 0, 0))
    d_spec = pl.BlockSpec((1, S5_W), lambda c: (0, 0))
    st_spec = pl.BlockSpec((None, 2, S5_G * S5_P), lambda c: (ci(c), 0, 0))
    return u_spec, p_spec, b_spec, c_spec, d_spec, st_spec


def s5_fwd(pm, are, aim, ldt, b_re, b_im, c_re, c_im, dv):
    S = pm.shape[0]
    nc = S // CHUNK
    u_spec, p_spec, b_spec, c_spec, d_spec, st_spec = _s5_specs(nc, False)

    def body(u_ref, are_ref, aim_ref, ldt_ref, bre_ref, bim_ref, cre_ref, cim_ref, dv_ref,
             y_ref, st_ref, state, tab, row):
        c = pl.program_id(0)

        @pl.when(c == 0)
        def _():
            state[...] = jnp.zeros_like(state)
            _s5_fill_tables(are_ref, aim_ref, ldt_ref, tab, row)

        st_ref[...] = state[...]
        for g in range(S5_GB):
            lu, ls = slice(128 * g, 128 * (g + 1)), slice(512 * g, 512 * (g + 1))
            tabs, rows = _s5_read_tables(tab, row, ls)
            y, e_re, e_im = _s5_chunk(u_ref[:, lu], state[0:1, ls], state[1:2, ls], tabs, rows,
                                      bre_ref[g], bim_ref[g], cre_ref[g], cim_ref[g], dv_ref[:, lu])
            y_ref[:, lu] = y
            state[0:1, ls] = e_re
            state[1:2, ls] = e_im

    n_state = S5_G * S5_P
    return pl.pallas_call(
        body, name="s5_fwd", grid=(nc,),
        in_specs=[u_spec, p_spec, p_spec, p_spec, b_spec, b_spec, c_spec, c_spec, d_spec],
        out_specs=[u_spec, st_spec],
        out_shape=[jax.ShapeDtypeStruct((S, S5_W), F32), jax.ShapeDtypeStruct((nc, 2, n_state), F32)],
        scratch_shapes=[pltpu.VMEM((2, n_state), F32), pltpu.VMEM((S5_NTAB, CHUNK, n_state), F32),
                        pltpu.VMEM((8, n_state), F32)],
        compiler_params=_cp("arbitrary"),
    )(pm, are, aim, ldt, b_re, b_im, c_re, c_im, dv)


def s5_bwd(pm, st, dy, are, aim, ldt, b_re, b_im, c_re, c_im, dv):
    S = pm.shape[0]
    nc = S // CHUNK
    u_spec, p_spec, b_spec, c_spec, d_spec, st_spec = _s5_specs(nc, True)

    def body(u_ref, st_ref, dy_ref, are_ref, aim_ref, ldt_ref, bre_ref, bim_ref, cre_ref, cim_ref, dv_ref,
             du_ref, dare_ref, daim_ref, dldt_ref, dbre_ref, dbim_ref, dcre_ref, dcim_ref, ddv_ref,
             dstate, tab, row, dtab, drow):
        c = pl.program_id(0)

        @pl.when(c == 0)
        def _():
            dstate[...] = jnp.zeros_like(dstate)
            dtab[...] = jnp.zeros_like(dtab)
            drow[...] = jnp.zeros_like(drow)
            _s5_fill_tables(are_ref, aim_ref, ldt_ref, tab, row)

        for g in range(S5_GB):
            lu, ls = slice(128 * g, 128 * (g + 1)), slice(512 * g, 512 * (g + 1))
            every = slice(None)
            tabs, rows = _s5_read_tables(tab, row, ls)
            args = (u_ref[:, lu], st_ref[0:1, ls], st_ref[1:2, ls], tabs, rows,
                    bre_ref[g], bim_ref[g], cre_ref[g], cim_ref[g], dv_ref[:, lu])
            _, vjp = jax.vjp(_s5_chunk, *args)
            gr = vjp((dy_ref[:, lu], dstate[0:1, ls], dstate[1:2, ls]))
            du_ref[:, lu] = gr[0]
            dstate[0:1, ls] = gr[1]
            dstate[1:2, ls] = gr[2]
            for k, t in enumerate(gr[3]):
                dtab[k, :, ls] += t
            for k, r in enumerate(gr[4]):
                drow[k:k + 1, ls] += r
            accs = ((dbre_ref, (g,)), (dbim_ref, (g,)), (dcre_ref, (g,)), (dcim_ref, (g,)), (ddv_ref, (every, lu)))
            for (o, idx), gv in zip(accs, gr[5:]):
                @pl.when(c == 0)
                def _(o=o, idx=idx, gv=gv):
                    o[idx] = gv

                @pl.when(c > 0)
                def _(o=o, idx=idx, gv=gv):
                    o[idx] += gv

        @pl.when(c == nc - 1)
        def _():
            for g in range(S5_GB):
                ls = slice(512 * g, 512 * (g + 1))
                _, vjp = jax.vjp(_s5_tables, are_ref[:, ls], aim_ref[:, ls], ldt_ref[:, ls])
                dtabs, drows = _s5_read_tables(dtab, drow, ls)
                ga, gi, gl = vjp((dtabs, drows))
                dare_ref[:, ls] = ga
                daim_ref[:, ls] = gi
                dldt_ref[:, ls] = gl

    n_state = S5_G * S5_P
    return pl.pallas_call(
        body, name="s5_bwd", grid=(nc,),
        in_specs=[u_spec, st_spec, u_spec, p_spec, p_spec, p_spec, b_spec, b_spec, c_spec, c_spec, d_spec],
        out_specs=[u_spec, p_spec, p_spec, p_spec, b_spec, b_spec, c_spec, c_spec, d_spec],
        out_shape=[jax.ShapeDtypeStruct((S, S5_W), F32)] + [jax.ShapeDtypeStruct((1, n_state), F32)] * 3
        + [jax.ShapeDtypeStruct((S5_GB, 128, 512), F32)] * 2 + [jax.ShapeDtypeStruct((S5_GB, 512, 128), F32)] * 2
        + [jax.ShapeDtypeStruct((1, S5_W), F32)],
        scratch_shapes=[pltpu.VMEM((2, n_state), F32), pltpu.VMEM((S5_NTAB, CHUNK, n_state), F32),
                        pltpu.VMEM((8, n_state), F32), pltpu.VMEM((S5_NTAB, CHUNK, n_state), F32),
                        pltpu.VMEM((8, n_state), F32)],
        compiler_params=_cp("arbitrary"),
    )(pm, st, dy, are, aim, ldt, b_re, b_im, c_re, c_im, dv)


def conv_fwd(pm, w):
    S = pm.shape[0]

    def body(x_ref, w_ref, o_ref, pad):
        x = x_ref[...]
        pad[0:8, :] = jnp.zeros((8, 128), F32)
        pad[8:, :] = x
        y = (w_ref[3:4, :] * x + w_ref[2:3, :] * pad[7:7 + S, :] + w_ref[1:2, :] * pad[6:6 + S, :]
             + w_ref[0:1, :] * pad[5:5 + S, :])
        o_ref[...] = y * jax.nn.sigmoid(y)

    return pl.pallas_call(
        body, name="conv_fwd", grid=(12,),
        in_specs=[pl.BlockSpec((S, 128), lambda j: (0, 4 + j)), pl.BlockSpec((4, 128), lambda j: (0, j))],
        out_specs=pl.BlockSpec((S, 128), lambda j: (0, j)),
        out_shape=jax.ShapeDtypeStruct((S, 1536), F32),
        scratch_shapes=[pltpu.VMEM((S + 8, 128), F32)],
        compiler_params=_cp("parallel"),
    )(pm, w)


def conv_bwd(pm, w, dout):
    S = pm.shape[0]

    def body(x_ref, w_ref, do_ref, dx_ref, dw_ref, pad, dpad):
        x = x_ref[...]
        pad[0:8, :] = jnp.zeros((8, 128), F32)
        pad[8:, :] = x
        xs = [pad[5:5 + S, :], pad[6:6 + S, :], pad[7:7 + S, :], x]
        y = w_ref[0:1, :] * xs[0] + w_ref[1:2, :] * xs[1] + w_ref[2:3, :] * xs[2] + w_ref[3:4, :] * xs[3]
        sg = jax.nn.sigmoid(y)
        dy = do_ref[...] * (sg + y * sg * (1.0 - sg))
        dpad[0:S, :] = dy
        dpad[S:, :] = jnp.zeros((8, 128), F32)
        dx_ref[...] = (w_ref[3:4, :] * dy + w_ref[2:3, :] * dpad[1:1 + S, :] + w_ref[1:2, :] * dpad[2:2 + S, :]
                       + w_ref[0:1, :] * dpad[3:3 + S, :])
        for i in range(4):
            dw_ref[i:i + 1, :] = jnp.sum(dy * xs[i], axis=0, keepdims=True)

    return pl.pallas_call(
        body, name="conv_bwd", grid=(12,),
        in_specs=[pl.BlockSpec((S, 128), lambda j: (0, 4 + j)), pl.BlockSpec((4, 128), lambda j: (0, j)),
                  pl.BlockSpec((S, 128), lambda j: (0, j))],
        out_specs=[pl.BlockSpec((S, 128), lambda j: (0, j)), pl.BlockSpec((4, 128), lambda j: (0, j))],
        out_shape=[jax.ShapeDtypeStruct((S, 1536), F32), jax.ShapeDtypeStruct((4, 1536), F32)],
        scratch_shapes=[pltpu.VMEM((S + 8, 128), F32), pltpu.VMEM((S + 8, 128), F32)],
        compiler_params=_cp("parallel"),
    )(pm, w, dout)


GDN_SUP = 4
GDN_ROWS = GDN_SUP * CHUNK


@jax.custom_vjp
def _saved_inverse(a, x):
    return x


_saved_inverse.defvjp(lambda a, x: (x, x),
                      lambda x, g: (-mm_tn(x, mm_nt(g, x)), jnp.zeros_like(x)))


def _gdn_chunk(q, k, v, gate, al, bl, alog, dtb, og, state, inv=None, want_inv=False):
    R = q.shape[0]
    r = lax.broadcasted_iota(jnp.int32, (R, R), 0)
    c = lax.broadcasted_iota(jnp.int32, (R, R), 1)
    same = (r // CHUNK) == (c // CHUNK)
    eye = (r == c).astype(F32)
    strict, causal, upper = same & (r > c), same & (r >= c), same & (r <= c)
    qn = q * lax.rsqrt(jnp.sum(q * q, axis=-1, keepdims=True) + EPS) * (GDN_D ** -0.5)
    kn = k * lax.rsqrt(jnp.sum(k * k, axis=-1, keepdims=True) + EPS)
    beta = jnp.sum(eye * jax.nn.sigmoid(bl), axis=1, keepdims=True)
    g_row = -jnp.exp(alog) * _softplus(al + dtb)
    g = jnp.sum(eye * g_row, axis=1, keepdims=True)
    gc_col = jnp.sum(jnp.where(causal, g_row, 0.0), axis=1, keepdims=True)
    gc_row = jnp.sum(jnp.where(upper, g, 0.0), axis=0, keepdims=True)
    gtot = jnp.sum(jnp.where(same, g_row, 0.0), axis=1, keepdims=True)
    gamma = jnp.exp(gc_col)
    diff = gc_col - gc_row
    d_strict = jnp.where(strict, jnp.exp(jnp.where(strict, diff, 0.0)), 0.0)
    d_causal = jnp.where(causal, jnp.exp(jnp.where(causal, diff, 0.0)), 0.0)
    a = beta * mm_nt(kn, kn) * d_strict
    if inv is None:
        p = -a
        x = eye + p
        for _ in range(5):
            p = mm(p, p)
            x = x + mm(x, p)
    else:
        x = _saved_inverse(a, inv)
    u_new = mm(x, beta * v)
    w_k = mm(x, (beta * gamma) * kn)
    qk = mm_nt(qn, kn) * d_causal
    q_g = qn * gamma
    k_tail = kn * jnp.exp(gtot - gc_col)
    ws, os_ = [], []
    for i in range(R // CHUNK):
        rows = slice(CHUNK * i, CHUNK * (i + 1))
        w_i = u_new[rows] - mm(w_k[rows], state)
        os_.append(mm(q_g[rows], state))
        decay = jnp.exp(jnp.sum(g[rows], axis=0, keepdims=True))
        state = decay * state + mm_tn(k_tail[rows], w_i)
        ws.append(w_i)
    o = jnp.concatenate(os_, axis=0) + mm(qk, jnp.concatenate(ws, axis=0))
    out = _rms(o, og) * (gate * jax.nn.sigmoid(gate))
    return (out, state, x) if want_inv else (out, state)


def _gdn_specs(nc, rev):
    def ci(c):
        return nc - 1 - c if rev else c

    def blk(cb):
        return pl.BlockSpec((GDN_ROWS, 512), lambda c: (ci(c), cb))

    col = lambda n: pl.BlockSpec((n, GDN_ROWS), lambda c: (0, ci(c)))
    sc = pl.BlockSpec((GDN_H, 1, 1), lambda c: (0, 0, 0))
    og = pl.BlockSpec((1, 128), lambda c: (0, 0))
    st = pl.BlockSpec((GDN_H, None, 128, 128), lambda c: (0, ci(c), 0, 0))
    return blk, col, sc, og, st


def gdn_fwd(qkvc, pm, abt, alog, dtb, og):
    S = qkvc.shape[0]
    nc = S // GDN_ROWS
    blk, col, sc, ogs, st = _gdn_specs(nc, False)

    def body(q_ref, k_ref, v_ref, gate_ref, ab_ref, alog_ref, dtb_ref, og_ref, o_ref, st_ref, inv_ref, state):
        c = pl.program_id(0)

        @pl.when(c == 0)
        def _():
            state[...] = jnp.zeros_like(state)

        st_ref[...] = state[...]
        for h in range(GDN_H):
            sl = slice(GDN_D * h, GDN_D * (h + 1))
            out, new_state, inv = _gdn_chunk(
                q_ref[:, sl], k_ref[:, sl], v_ref[:, sl], gate_ref[:, sl], ab_ref[h:h + 1, :], ab_ref[GDN_H + h:GDN_H + h + 1, :],
                alog_ref[h], dtb_ref[h], og_ref[...], state[h], want_inv=True)
            o_ref[:, sl] = out
            state[h] = new_state
            inv_ref[h] = inv

    inv_spec = pl.BlockSpec((GDN_H, None, GDN_ROWS, GDN_ROWS), lambda c: (0, c, 0, 0))
    return pl.pallas_call(
        body, name="gdn_fwd", grid=(nc,),
        in_specs=[blk(0), blk(1), blk(2), blk(4), col(2 * GDN_H), sc, sc, ogs],
        out_specs=[blk(0), st, inv_spec],
        out_shape=[jax.ShapeDtypeStruct((S, 512), F32), jax.ShapeDtypeStruct((GDN_H, nc, 128, 128), F32),
                   jax.ShapeDtypeStruct((GDN_H, nc, GDN_ROWS, GDN_ROWS), F32)],
        scratch_shapes=[pltpu.VMEM((GDN_H, 128, 128), F32)],
        compiler_params=_cp("arbitrary"),
    )(qkvc, qkvc, qkvc, pm, abt, alog, dtb, og)


def gdn_bwd(qkvc, pm, abt, alog, dtb, og, st, inv, dout):
    S = qkvc.shape[0]
    nc = S // GDN_ROWS
    blk, col, sc, ogs, sts = _gdn_specs(nc, True)

    def body(q_ref, k_ref, v_ref, gate_ref, ab_ref, alog_ref, dtb_ref, og_ref, st_ref, inv_ref, do_ref,
             dqkv_ref, dgate_ref, dal_ref, dbl_ref, dalog_ref, ddtb_ref, dog_ref, dstate):
        c = pl.program_id(0)

        @pl.when(c == 0)
        def _():
            dstate[...] = jnp.zeros_like(dstate)

        for h in range(GDN_H):
            sl = slice(GDN_D * h, GDN_D * (h + 1))
            args = (q_ref[:, sl], k_ref[:, sl], v_ref[:, sl], gate_ref[:, sl], ab_ref[h:h + 1, :], ab_ref[GDN_H + h:GDN_H + h + 1, :],
                    alog_ref[h], dtb_ref[h], og_ref[...], st_ref[h])
            inv_h = inv_ref[h]
            _, vjp = jax.vjp(lambda *a, inv_h=inv_h: _gdn_chunk(*a, inv=inv_h), *args)
            g = vjp((do_ref[:, sl], dstate[h]))
            for part in range(3):
                dqkv_ref[:, slice(512 * part + sl.start, 512 * part + sl.stop)] = g[part]
            dgate_ref[:, sl] = g[3]
            dal_ref[h:h + 1, :] = g[4]
            dbl_ref[h:h + 1, :] = g[5]
            dstate[h] = g[9]
            for o, gv in zip((dalog_ref, ddtb_ref, dog_ref), g[6:9]):
                @pl.when(c == 0)
                def _(o=o, gv=gv, h=h):
                    o[h] = gv

                @pl.when(c > 0)
                def _(o=o, gv=gv, h=h):
                    o[h] += gv

    ogo = pl.BlockSpec((GDN_H, 1, 128), lambda c: (0, 0, 0))
    inv_spec = pl.BlockSpec((GDN_H, None, GDN_ROWS, GDN_ROWS), lambda c: (0, nc - 1 - c, 0, 0))
    sd = jax.ShapeDtypeStruct
    return pl.pallas_call(
        body, name="gdn_bwd", grid=(nc,),
        in_specs=[blk(0), blk(1), blk(2), blk(4), col(2 * GDN_H), sc, sc, ogs, sts, inv_spec, blk(0)],
        out_specs=[pl.BlockSpec((GDN_ROWS, 1536), lambda c: (nc - 1 - c, 0)), blk(0), col(GDN_H), col(GDN_H),
                   sc, sc, ogo],
        out_shape=[sd((S, 1536), F32), sd((S, 512), F32)] + [sd((GDN_H, S), F32)] * 2 + [sd((GDN_H, 1, 1), F32)] * 2
        + [sd((GDN_H, 1, 128), F32)],
        scratch_shapes=[pltpu.VMEM((GDN_H, 128, 128), F32)],
        compiler_params=_cp("arbitrary"),
    )(qkvc, qkvc, qkvc, pm, abt, alog, dtb, og, st, inv, dout)


HG = 8
HG_LANES = HG * CA_D


def _group_mean_raw(y):
    r = lax.broadcasted_iota(jnp.int32, (128, 128), 0)
    c = lax.broadcasted_iota(jnp.int32, (128, 128), 1)
    g = jnp.where((r // CA_D) == (c // CA_D), 1.0 / CA_D, 0.0).astype(BF16)
    d = lambda u: lax.dot_general(u, g, (((1,), (0,)), ((), ())), preferred_element_type=F32)
    outs = []
    for j in range(y.shape[1] // 128):
        hi, lo = _split(y[:, 128 * j:128 * (j + 1)])
        outs.append(d(hi) + d(lo))
    return jnp.concatenate(outs, axis=1)


@jax.custom_vjp
def group_mean(y):
    return _group_mean_raw(y)


group_mean.defvjp(lambda y: (_group_mean_raw(y), None), lambda _, g: (_group_mean_raw(g),))


def f_headnorm(t, g):
    return (t * lax.rsqrt(group_mean(t * t) + EPS) * g,)


def _cattn_chunk(q, kb, vb, bias, valid):
    lane = lax.broadcasted_iota(jnp.int32, (1, 128), 1)
    m0 = (lane < CA_D).astype(F32)
    m1 = 1.0 - m0
    pairs = range(HG // 2)
    sl = [slice(128 * p, 128 * (p + 1)) for p in pairs]
    q2 = [jnp.concatenate([q[:, s] * m0, q[:, s] * m1], axis=0) for s in sl]
    sc = [mm_nt(q2[p], kb[:, sl[p]]) * (CA_D ** -0.5) + bias[sl[p]] for p in pairs]
    pr = [_softmax(jnp.where(valid, s, -1e30)) for s in sc]
    o2 = [mm(pr[p], vb[:, sl[p]]) for p in pairs]
    return jnp.concatenate([o[:CHUNK] * m0 + o[CHUNK:] * m1 for o in o2], axis=1)


def _cattn_valid(c):
    pos = lax.broadcasted_iota(jnp.int32, (1, CA_BAND), 1) + c * CHUNK
    return pos >= CA_PAD


def _cattn_specs(S):
    q_spec = pl.BlockSpec((CHUNK, HG_LANES), lambda h, c: (c, h))
    kv_spec = pl.BlockSpec((S + CA_PAD, HG_LANES), lambda h, c: (0, h))
    b_spec = pl.BlockSpec((HG * CHUNK, CA_BAND), lambda h, c: (h, 0))
    return q_spec, kv_spec, b_spec


def cattn_fwd(qn, kp, vp, bias):
    S = qn.shape[0]
    nc = S // CHUNK
    q_spec, kv_spec, b_spec = _cattn_specs(S)

    def body(q_ref, k_ref, v_ref, b_ref, o_ref):
        c = pl.program_id(1)
        start = pl.multiple_of(c * CHUNK, CHUNK)
        kb = k_ref[pl.ds(start, CA_BAND), :]
        vb = v_ref[pl.ds(start, CA_BAND), :]
        o_ref[...] = _cattn_chunk(q_ref[...], kb, vb, b_ref[...], _cattn_valid(c)).astype(o_ref.dtype)

    return pl.pallas_call(
        body, name="cattn_fwd", grid=(CA_H // HG, nc), in_specs=[q_spec, kv_spec, kv_spec, b_spec],
        out_specs=q_spec, out_shape=jax.ShapeDtypeStruct((S, D), BF16),
        compiler_params=_cp("parallel", "arbitrary"),
    )(qn, kp, vp, bias)


def kv_prep(qkv, kg):
    S = qkv.shape[0]
    tile = ROW_TILE
    lead = CA_PAD // tile

    def body(k_ref, v_ref, g_ref, kp_ref, vp_ref):
        i = pl.program_id(0)

        @pl.when(i < lead)
        def _():
            kp_ref[...] = jnp.zeros_like(kp_ref)
            vp_ref[...] = jnp.zeros_like(vp_ref)

        @pl.when(i >= lead)
        def _():
            kp_ref[...] = f_headnorm(k_ref[...], g_ref[...])[0].astype(BF16)
            vp_ref[...] = v_ref[...].astype(BF16)

    src = lambda cb: pl.BlockSpec((tile, D), lambda i, cb=cb: (jnp.maximum(i - lead, 0), cb))
    out = pl.BlockSpec((tile, D), lambda i: (i, 0))
    return pl.pallas_call(
        body, name="kv_prep", grid=((S + CA_PAD) // tile,),
        in_specs=[src(1), src(2), pl.BlockSpec((1, D), lambda i: (0, 0))], out_specs=[out, out],
        out_shape=[jax.ShapeDtypeStruct((S + CA_PAD, D), BF16)] * 2,
        compiler_params=_cp("parallel"),
    )(qkv, qkv, kg)


def cattn_bwd(qn, kp, vp, bias, do):
    S = qn.shape[0]
    nc = S // CHUNK
    q_spec, kv_spec, b_spec = _cattn_specs(S)

    def body(q_ref, k_ref, v_ref, b_ref, do_ref, dq_ref, dk_ref, dv_ref, db_ref):
        c = pl.program_id(1)

        @pl.when(c == 0)
        def _():
            dk_ref[...] = jnp.zeros_like(dk_ref)
            dv_ref[...] = jnp.zeros_like(dv_ref)
            db_ref[...] = jnp.zeros_like(db_ref)

        start = pl.multiple_of(c * CHUNK, CHUNK)
        kb = k_ref[pl.ds(start, CA_BAND), :].astype(F32)
        vb = v_ref[pl.ds(start, CA_BAND), :].astype(F32)
        valid = _cattn_valid(c)
        _, vjp = jax.vjp(lambda q, k, v, b: _cattn_chunk(q, k, v, b, valid), q_ref[...], kb, vb, b_ref[...])
        dq, dk, dv, db = vjp(do_ref[...])
        dq_ref[...] = dq
        dk_ref[pl.ds(start, CA_BAND), :] += dk
        dv_ref[pl.ds(start, CA_BAND), :] += dv
        db_ref[...] += db

    sd = jax.ShapeDtypeStruct
    return pl.pallas_call(
        body, name="cattn_bwd", grid=(CA_H // HG, nc), in_specs=[q_spec, kv_spec, kv_spec, b_spec, q_spec],
        out_specs=[q_spec, kv_spec, kv_spec, b_spec],
        out_shape=[sd((S, D), F32), sd((S + CA_PAD, D), F32), sd((S + CA_PAD, D), F32),
                   sd((CA_H * CHUNK, CA_BAND), F32)],
        compiler_params=_cp("parallel", "arbitrary"),
    )(qn, kp, vp, bias, do)


SKEW_W = CA_BAND + CHUNK


def rel_bias_grad(dbias):
    padded = jnp.pad(dbias, ((0, 0), (0, 0), (CHUNK, 0)))
    flat = jnp.pad(padded.reshape(CA_H, CHUNK * SKEW_W), ((0, 0), (0, CHUNK)))
    skew = flat.reshape(CA_H, CHUNK, SKEW_W + 1)

    first_near = SKEW_W - CHUNK - MAX_REL

    def fn(t):
        colsum = jnp.sum(t, axis=1, keepdims=True)
        j = lax.broadcasted_iota(jnp.int32, colsum.shape, 2)
        far = jnp.sum(jnp.where(j < first_near, colsum, 0.0), axis=2, keepdims=True)
        return (colsum + jnp.where(j == first_near, far, 0.0),)

    (colsum,) = whole("relbias_sum", fn, [skew], [((CA_H, 1, SKEW_W + 1), F32)])
    near = colsum[:, 0, first_near:SKEW_W][:, ::-1]
    return jnp.concatenate([jnp.zeros((CA_H, CHUNK + 1), F32), near], axis=1)


def rel_bias_expand(rb):
    near = rb[:, CHUNK + 1:][:, ::-1]
    far = jnp.broadcast_to(rb[:, 2 * MAX_REL:], (CA_H, SKEW_W - CHUNK - MAX_REL))
    t = jnp.concatenate([far, near, jnp.zeros((CA_H, 1), rb.dtype)], axis=1)
    rows = jnp.tile(t, (1, CHUNK))[:, :CHUNK * SKEW_W].reshape(CA_H, CHUNK, SKEW_W)
    return rows[:, :, CHUNK:]


def loss_head(y, target):
    S = y.shape[0]
    tile = min(ROW_TILE, S)

    def body(y_ref, t_ref, dy_ref, acc_ref):
        i = pl.program_id(0)
        e = y_ref[...] - t_ref[...]
        dy_ref[...] = e * (1.0 / D)
        part = jnp.sum(e * e, axis=0, keepdims=True) * (0.5 / D)

        @pl.when(i == 0)
        def _():
            acc_ref[...] = part

        @pl.when(i > 0)
        def _():
            acc_ref[...] += part

    row = pl.BlockSpec((tile, D), lambda i: (i, 0))
    return pl.pallas_call(
        body, name="loss_head", grid=(S // tile,), in_specs=[row, row],
        out_specs=[row, pl.BlockSpec((1, D), lambda i: (0, 0))],
        out_shape=[jax.ShapeDtypeStruct((S, D), F32), jax.ShapeDtypeStruct((1, D), F32)],
        compiler_params=_cp("arbitrary"),
    )(y, target)


ANY = pl.BlockSpec(memory_space=pl.ANY)


HBM = pl.BlockSpec(memory_space=pltpu.HBM)
SEM = pl.BlockSpec(memory_space=pltpu.SEMAPHORE)
EFFECT = pltpu.SideEffectType.DATAFLOW_SIDE_EFFECTING


def _chip_copies(src_ref, land_ref, send_sems, recv_sems, mode):
    x, y, c = lax.axis_index("x"), lax.axis_index("y"), lax.axis_index("c")
    me = 2 * x + y

    def copy(j, s, d, dev):
        return pltpu.make_async_remote_copy(src_ref=s, dst_ref=d, send_sem=send_sems.at[j], recv_sem=recv_sems.at[j],
                                            device_id=dev, device_id_type=MESH)

    if mode == "sibling":
        cp = copy(0, src_ref, land_ref, (x, y, 1 - c))
        return [(cp, cp)]
    out = []
    for j, (px, py) in enumerate([(1 - x, y), (x, 1 - y), (1 - x, 1 - y)]):
        peer = 2 * px + py
        if mode == "scatter":
            pairs, dev = ((src_ref.at[peer], land_ref.at[j]), (src_ref.at[me], land_ref.at[j])), (px, py, c)
        elif mode == "gather":
            pairs, dev = ((src_ref, land_ref.at[me]), (src_ref, land_ref.at[peer])), (px, py, c)
        else:
            half = land_ref.shape[1] // 2
            mine = pl.ds(pl.multiple_of(c * half, 16), half)
            theirs = pl.ds(pl.multiple_of((1 - c) * half, 16), half)
            if mode == "gather_half":
                pairs = ((src_ref.at[mine], land_ref.at[me, mine]), (src_ref.at[mine], land_ref.at[peer, mine]))
                dev = (px, py, c)
            else:
                pairs = ((land_ref.at[peer, mine],) * 2, (land_ref.at[peer, theirs],) * 2)
                dev = (x, y, 1 - c)
        out.append(tuple(copy(j, s, d, dev) for s, d in pairs))
    return out


def exchange_start(src, land, carry, name, mode):
    def body(src_ref, land_ref, carry_ref, send_sems, recv_sems, src_out, land_out, carry_out):
        for send, _ in _chip_copies(src_ref, land_ref, send_sems, recv_sems, mode):
            send.start()

    hbm = lambda a: pltpu.HBM(a.shape, a.dtype)
    n = 1 if mode == "sibling" else 3
    return pl.pallas_call(
        body, name=name,
        out_shape=(pltpu.SemaphoreType.DMA((n,)), pltpu.SemaphoreType.DMA((n,)), hbm(src), hbm(land), hbm(carry)),
        in_specs=(HBM, HBM, HBM), out_specs=(SEM, SEM, HBM, HBM, HBM),
        input_output_aliases={0: 2, 1: 3, 2: 4},
        compiler_params=pltpu.CompilerParams(has_side_effects=EFFECT),
    )(pltpu.with_memory_space_constraint(src, pltpu.HBM), pltpu.with_memory_space_constraint(land, pltpu.HBM),
      pltpu.with_memory_space_constraint(carry, pltpu.HBM))


def exchange_wait(src, land, send_sems, recv_sems, after, name, mode):
    def body(src_ref, land_ref, send_sems_ref, recv_sems_ref, after_ref, src_out, land_out):
        for send, recv in _chip_copies(src_ref, land_ref, send_sems_ref, recv_sems_ref, mode):
            send.wait_send()
            recv.wait_recv()

    hbm = lambda a: pltpu.HBM(a.shape, a.dtype)
    return pl.pallas_call(
        body, name=name, out_shape=(hbm(src), hbm(land)),
        in_specs=(HBM, HBM, SEM, SEM, ANY), out_specs=(HBM, HBM), input_output_aliases={0: 0, 1: 1},
        compiler_params=pltpu.CompilerParams(has_side_effects=EFFECT),
    )(src, land, send_sems, recv_sems, after)


def sibling_exchange(srcs, name):
    n = len(srcs)

    def body(*refs):
        src_refs, out_refs, send_sems, recv_sems = refs[:n], refs[n:2 * n], refs[2 * n], refs[2 * n + 1]
        x, y, c = lax.axis_index("x"), lax.axis_index("y"), lax.axis_index("c")
        copies = [pltpu.make_async_remote_copy(src_ref=s, dst_ref=o, send_sem=send_sems.at[k], recv_sem=recv_sems.at[k],
                                               device_id=(x, y, 1 - c), device_id_type=MESH)
                  for k, (s, o) in enumerate(zip(src_refs, out_refs))]
        for cp in copies:
            cp.start()
        for cp in copies:
            cp.wait()

    return pl.pallas_call(
        body, name=name, in_specs=[ANY] * n, out_specs=[ANY] * n,
        out_shape=[jax.ShapeDtypeStruct(s.shape, s.dtype) for s in srcs],
        scratch_shapes=[pltpu.SemaphoreType.DMA((n,)), pltpu.SemaphoreType.DMA((n,))],
    )(*srcs)


def sum_own_slabs(src, chip, land, name, tile=512):
    _, R, C = src.shape
    n = land.shape[0]
    tile = _tile(R, tile, 16)

    def body(chip_ref, o_ref, t_ref, out_ref):
        acc = o_ref[...].astype(F32)
        for s in range(n):
            acc = acc + t_ref[s].astype(F32)
        out_ref[...] = acc

    return pl.pallas_call(
        body, name=name,
        grid_spec=pltpu.PrefetchScalarGridSpec(
            num_scalar_prefetch=1, grid=(R // tile,),
            in_specs=[pl.BlockSpec((None, tile, C), lambda i, chip_ref: (chip_ref[0], i, 0)),
                      pl.BlockSpec((n, tile, C), lambda i, chip_ref: (0, i, 0))],
            out_specs=pl.BlockSpec((tile, C), lambda i, chip_ref: (i, 0))),
        out_shape=jax.ShapeDtypeStruct((R, C), F32),
        compiler_params=_cp("parallel"),
    )(jnp.reshape(chip, (1,)).astype(jnp.int32), src, land)


def sum_slabs(t, name, tile=512):
    n, R, C = t.shape
    tile = _tile(R, tile, 16)

    def body(t_ref, o_ref):
        acc = t_ref[0].astype(F32)
        for s in range(1, n):
            acc = acc + t_ref[s].astype(F32)
        o_ref[...] = acc

    return pl.pallas_call(
        body, name=name, grid=(R // tile,), in_specs=[pl.BlockSpec((n, tile, C), lambda i: (0, i, 0))],
        out_specs=pl.BlockSpec((tile, C), lambda i: (i, 0)), out_shape=jax.ShapeDtypeStruct((R, C), F32),
        compiler_params=_cp("parallel"),
    )(t)


def _pad_rows(a, mult=16):
    r = (-a.shape[0]) % mult
    return jnp.pad(a, ((0, r), (0, 0))) if r else a


BIG = ["ab_w_in", "c_w_qkv", "xa_w_kv", "f_w_gate", "f_w_up", "ab_w_out", "c_w_out", "xa_w_q", "xa_w_out",
       "f_w_down", "s5_w_glu"]


GROUPS = ("B", "A")
GROUP_ROW_MULT = 64


def group_spec(layer, grp):
    i = layer // 2
    if grp == "A":
        return [("xa_w_kv", layer, True, 512), ("xa_w_q", layer, False, 256), ("xa_w_out", layer, False, 256),
                ("f_w_gate", layer, True, 704), ("f_w_up", layer, True, 704), ("f_w_down", layer, False, 704)]
    if layer % 2 == 0:
        return [("ab_w_in", i, True, 642), ("ab_w_out", i, False, 256), ("s5_w_glu", i, False, 64)]
    return [("c_w_qkv", i, True, 768), ("c_w_out", i, False, 256)]


def _seg_rows(rows):
    return rows + ((-rows) % 16)


def _f32_rows(a):
    bits = lax.bitcast_convert_type(a.reshape(-1), BF16).reshape(-1)
    return jnp.pad(bits, (0, 16 * D - bits.shape[0])).reshape(16, D)


def pack_group_shards(p, layer, grp):
    segs = []
    for name, idx, transposed, rows in group_spec(layer, grp):
        w = p[name][idx]
        if transposed:
            w = w.T
        segs.append(_pad_rows(w.astype(BF16).reshape(-1, D)))
    if grp == "B":
        small = p["gdn_conv_w"] if layer % 2 == 0 else p["c_norm_g"]
        segs.append(_f32_rows(small[layer // 2]))
    return _pad_rows(jnp.concatenate(segs, axis=0), GROUP_ROW_MULT)


def unpack_group_gathered(g, layer, grp):
    out, off = {}, 0
    for name, idx, transposed, rows in group_spec(layer, grp):
        seg = g[:, off:off + rows]
        if name == "s5_w_glu":
            out[name] = seg.reshape(N_CHIPS * 128, 512)
        else:
            out[name] = seg.reshape(N_CHIPS * rows, D)
        off += _seg_rows(rows)
    if grp == "B":
        n = 4 * 384 if layer % 2 == 0 else 256
        bits = g[:, off:off + 16].reshape(N_CHIPS, -1)[:, :2 * n].reshape(N_CHIPS, n, 2)
        small = lax.bitcast_convert_type(bits, F32)
        if layer % 2 == 0:
            out["gdn_conv_w"] = jnp.swapaxes(small.reshape(N_CHIPS, 4, 384), 0, 1).reshape(4, 1536)
        else:
            out["c_norm_g"] = small.reshape(D)
    return out


def pack_group_grads(gr, layer, grp):
    segs = []
    for name, idx, transposed, rows in group_spec(layer, grp):
        w = gr[name].astype(BF16)
        seg = w.reshape(N_CHIPS, rows, D)
        r = (-rows) % 16
        if r:
            seg = jnp.pad(seg, ((0, 0), (0, r), (0, 0)))
        segs.append(seg)
    out = jnp.concatenate(segs, axis=1)
    return jnp.pad(out, ((0, 0), (0, (-out.shape[1]) % GROUP_ROW_MULT), (0, 0)))


def unpack_group_reduced(g, layer, grp):
    out, off = {}, 0
    for name, idx, transposed, rows in group_spec(layer, grp):
        seg = g[off:off + rows]
        if name == "s5_w_glu":
            out[name] = seg.reshape(128, 512)
        else:
            out[name] = seg.T if (transposed and name not in ADAM_TRANSPOSED) else seg
        off += _seg_rows(rows)
    return out


ADAM_TRANSPOSED = ("f_w_gate", "f_w_up")


SMALL = ["ab_norm_g", "s5_a_re", "s5_a_im", "s5_log_dt", "s5_b_re", "s5_b_im", "s5_c_re", "s5_c_im", "s5_d",
         "s5_b_glu", "gdn_conv_w", "gdn_a_log", "gdn_dt_bias", "gdn_out_norm_g", "c_norm_g", "c_q_norm_g",
         "c_k_norm_g", "c_rel_bias", "mem_norm_g", "xa_norm_g", "xa_q_norm_g", "xa_k_norm_g", "f_norm_g"]


def _lane_rows(a):
    flat = a.reshape(-1).astype(F32)
    return jnp.pad(flat, (0, (-flat.shape[0]) % 1024)).reshape(-1, 128)


def pack_small(d, extra=None):
    parts = [_lane_rows(d[n]) for n in SMALL]
    if extra is not None:
        parts.append(_lane_rows(extra))
    rows = jnp.concatenate(parts, axis=0)
    return jnp.pad(rows, ((0, (-rows.shape[0]) % 128), (0, 0)))


def unpack_small(rows, shapes):
    out, off = {}, 0
    for n in SMALL:
        sz = int(np.prod(shapes[n]))
        k = 8 * -(-sz // 1024)
        out[n] = rows[off:off + k].reshape(-1)[:sz].reshape(shapes[n])
        off += k
    return out, rows[off:]


def _s5_blockdiag_b(b):
    bt = jnp.swapaxes(b, 1, 2).reshape(S5_GB, 8, S5_C, S5_P)
    eye = jnp.eye(8, dtype=b.dtype)
    return jnp.einsum("bgcp,gh->bgchp", bt, eye).reshape(S5_GB, 8 * S5_C, 8 * S5_P)


def _s5_blockdiag_c(c):
    ct = jnp.swapaxes(c, 1, 2).reshape(S5_GB, 8, S5_P, S5_C)
    eye = jnp.eye(8, dtype=c.dtype)
    return jnp.einsum("bgpc,gh->bgphc", ct, eye).reshape(S5_GB, 8 * S5_P, 8 * S5_C)


def _s5_diag_b(db):
    t = db.reshape(S5_GB, 8, S5_C, 8, S5_P)
    t = jnp.transpose(t, (0, 2, 4, 1, 3)).reshape(S5_GB, S5_C, S5_P, 64)
    d = t[..., ::9]
    return jnp.transpose(d, (0, 3, 2, 1)).reshape(S5_G, S5_P, S5_C)


def _s5_diag_c(dc):
    t = dc.reshape(S5_GB, 8, S5_P, 8, S5_C)
    t = jnp.transpose(t, (0, 2, 4, 1, 3)).reshape(S5_GB, S5_P, S5_C, 64)
    d = t[..., ::9]
    return jnp.transpose(d, (0, 3, 2, 1)).reshape(S5_G, S5_C, S5_P)


def local_step(x, mem, target, p, wsrc, gsink, wpre=lambda layer, grp, carry: carry):
    S = x.shape[0]
    row2 = lambda a: a.reshape(1, -1)
    saved = []
    (mem_n,) = row_fwd("mem_norm", f_norm, [mem], [row2(p["mem_norm_g"])], [(D, BF16)])
    gs = {}

    for layer in range(DEPTH):
        i = layer // 2
        w = wsrc(layer, "B", x)
        sv = {"x0": x, "wB": w}
        if layer % 2 == 0:
            (h,) = row_fwd("norm", f_norm, [x], [row2(p["ab_norm_g"][i])], [(D, BF16)])
            w_in = w["ab_w_in"]
            pm = matmul("nt", h, w_in[:2560], "proj_main")
            abt = matmul("nt", w_in[2560:], h, "proj_ab")
            s5p = dict(
                are=p["s5_a_re"][i].reshape(1, -1), aim=p["s5_a_im"][i].reshape(1, -1),
                ldt=jnp.broadcast_to(p["s5_log_dt"][i][:, None], (S5_G, S5_P)).reshape(1, -1),
                b_re=_s5_blockdiag_b(p["s5_b_re"][i]), b_im=_s5_blockdiag_b(p["s5_b_im"][i]),
                c_re=_s5_blockdiag_c(p["s5_c_re"][i]), c_im=_s5_blockdiag_c(p["s5_c_im"][i]),
                dv=p["s5_d"][i].reshape(1, -1))
            y5, st5 = s5_fwd(pm, **s5p)
            (a_out,) = row_fwd("glu", f_glu, [y5], [w["s5_w_glu"], row2(p["s5_b_glu"][i])], [(S5_W, F32)])
            conv_w = w["gdn_conv_w"]
            qkvc = conv_fwd(pm, conv_w)
            alog = p["gdn_a_log"][i].reshape(GDN_H, 1, 1)
            dtb = p["gdn_dt_bias"][i].reshape(GDN_H, 1, 1)
            og = row2(p["gdn_out_norm_g"][i])
            b_out, stg, inv = gdn_fwd(qkvc, pm, abt, alog, dtb, og)
            cat = wpre(layer, "A", jnp.concatenate([a_out, b_out], axis=1))
            x, hq = matmul("nn", cat, w["ab_w_out"], "mix_out", add=x, norm_out=row2(p["xa_norm_g"][layer]))
            sv.update(h=h, pm=pm, s5p=s5p, y5=y5, st5=st5, qkvc=qkvc, abt=abt, alog=alog, dtb=dtb, og=og,
                      stg=stg, inv=inv, cat=cat, conv_w=conv_w)
        else:
            (h,) = row_fwd("norm", f_norm, [x], [row2(w["c_norm_g"])], [(D, BF16)])
            qkv = matmul("nt", h, w["c_w_qkv"], "proj_qkv")
            qg = jnp.tile(row2(p["c_q_norm_g"][i]), (1, CA_H))
            kg = jnp.tile(row2(p["c_k_norm_g"][i]), (1, CA_H))
            (qn,) = row_fwd("headnorm_q", f_headnorm, [(qkv, D, 0)], [qg], [(D, F32)])
            kp, vp = kv_prep(qkv, kg)
            bias = rel_bias_expand(p["c_rel_bias"][i]).reshape(CA_H * CHUNK, CA_BAND)
            o = wpre(layer, "A", cattn_fwd(qn, kp, vp, bias))
            x, hq = matmul("nn", o, w["c_w_out"], "mix_out", add=x, norm_out=row2(p["xa_norm_g"][layer]))
            sv.update(h=h, qkv=qkv, qg=qg, kg=kg, qn=qn, kp=kp, vp=vp, bias=bias, o=o)
        sv["x1"] = x
        w = wsrc(layer, "A", x)
        sv["wA"] = w
        qx = matmul("nn", hq, w["xa_w_q"], "xa_q")
        kv = matmul("nt", mem_n, w["xa_w_kv"], "xa_kv")
        xqg, xkg = row2(p["xa_q_norm_g"][layer]), row2(p["xa_k_norm_g"][layer])
        (ox,) = row_fwd("xattn", f_xattn, [qx], [kv, xqg, xkg], [(D, BF16)])
        x, hf = matmul("nn", ox, w["xa_w_out"], "xa_out", add=x, norm_out=row2(p["f_norm_g"][layer]))
        sv.update(hq=hq, qx=qx, kv=kv, ox=ox)
        sv["x2"] = x
        gate, up, act = ffn_in(hf, w["f_w_gate"], w["f_w_up"])
        if layer + 1 < DEPTH:
            act = wpre(layer + 1, "B", act)
        x = matmul("nn", act, w["f_w_down"], "ffn_down", add=x)
        sv.update(hf=hf, gate=gate, up=up, act=act)
        saved.append(sv)

    dx, loss_vec = loss_head(x, target)

    dmem_n = None
    for layer in reversed(range(DEPTH)):
        i = layer // 2
        sv = saved[layer]
        w, gw = sv["wA"], {}
        dgate, dup = ffn_dact(dx, w["f_w_down"], sv["gate"], sv["up"])
        gw["f_w_down"] = matmul("tn", sv["act"], dx, "ffn_dwd", out_dtype=BF16)
        gw["f_w_gate"] = matmul("tn", dgate, sv["hf"], "ffn_dwg", out_dtype=BF16)
        gw["f_w_up"] = matmul("tn", dup, sv["hf"], "ffn_dwu", out_dtype=BF16)
        dh = matmul("nn", dgate, w["f_w_gate"], "ffn_dhg")
        dx, dg = matmul("nn", dup, w["f_w_up"], "ffn_dhu", add=dh,
                        norm_bwd=(sv["x2"], row2(p["f_norm_g"][layer]), dx))
        gs.setdefault("f_norm_g", [None] * DEPTH)[layer] = dg[0]
        do = matmul("nt", dx, w["xa_w_out"], "xa_do")
        gw["xa_w_out"] = matmul("tn", sv["ox"], dx, "xa_dwo", out_dtype=BF16)
        xqg, xkg = row2(p["xa_q_norm_g"][layer]), row2(p["xa_k_norm_g"][layer])
        dqx, dkv, dqg, dkg = row_bwd("xattn_bwd", f_xattn, [sv["qx"]], [sv["kv"], xqg, xkg], [do],
                                     [0], [0, 1, 2])
        gs.setdefault("xa_q_norm_g", [None] * DEPTH)[layer] = dqg[0]
        gs.setdefault("xa_k_norm_g", [None] * DEPTH)[layer] = dkg[0]
        gw["xa_w_q"] = matmul("tn", sv["hq"], dqx, "xa_dwq", out_dtype=BF16)
        gw["xa_w_kv"] = matmul("tn", dkv, mem_n, "xa_dwkv", out_dtype=BF16)
        dx = gsink(layer, "A", gw, dx)
        dmem_n = matmul("nn", dkv, w["xa_w_kv"], "xa_dmem", add=dmem_n)
        dx, dg = matmul("nt", dqx, w["xa_w_q"], "xa_dhq", norm_bwd=(sv["x1"], row2(p["xa_norm_g"][layer]), dx))
        gs.setdefault("xa_norm_g", [None] * DEPTH)[layer] = dg[0]
        w, gw = sv["wB"], {}
        if layer % 2 == 0:
            dcat = matmul("nt", dx, w["ab_w_out"], "mix_dcat")
            gw["ab_w_out"] = matmul("tn", sv["cat"], dx, "mix_dwo", out_dtype=BF16)
            dy5, dwglu, dbglu = row_bwd("glu_bwd", f_glu, [sv["y5"]], [w["s5_w_glu"], row2(p["s5_b_glu"][i])],
                                        [(dcat, S5_W, 0)], [0], [0, 1])
            gw["s5_w_glu"] = dwglu
            gs.setdefault("s5_b_glu", [None] * 2)[i] = dbglu[0]
            s5p = sv["s5p"]
            du, dare, daim, dldt, dbre, dbim, dcre, dcim, ddv = s5_bwd(sv["pm"], sv["st5"], dy5, **s5p)
            (dldt_g,) = whole("s5_dt_sum", lambda t: (jnp.sum(t, axis=1, keepdims=True),),
                              [dldt.reshape(S5_G, S5_P)], [((S5_G, 1), F32)])
            for nme, val in (("s5_a_re", dare.reshape(S5_G, S5_P)), ("s5_a_im", daim.reshape(S5_G, S5_P)),
                             ("s5_log_dt", dldt_g[:, 0]), ("s5_b_re", _s5_diag_b(dbre)),
                             ("s5_b_im", _s5_diag_b(dbim)), ("s5_c_re", _s5_diag_c(dcre)),
                             ("s5_c_im", _s5_diag_c(dcim)), ("s5_d", ddv.reshape(S5_G, S5_C))):
                gs.setdefault(nme, [None] * 2)[i] = val
            dqkvc, dgate, dal, dbl, dalog, ddtb, dog = gdn_bwd(
                sv["qkvc"], sv["pm"], sv["abt"], sv["alog"], sv["dtb"], sv["og"], sv["stg"], sv["inv"],
                dcat[:, S5_W:])
            (dog_s,) = whole("gdn_og_sum", lambda t: (jnp.sum(t, axis=0, keepdims=True),),
                             [dog.reshape(GDN_H, GDN_D)], [((1, GDN_D), F32)])
            gs.setdefault("gdn_out_norm_g", [None] * 2)[i] = dog_s[0]
            gs.setdefault("gdn_a_log", [None] * 2)[i] = dalog.reshape(GDN_H)
            gs.setdefault("gdn_dt_bias", [None] * 2)[i] = ddtb.reshape(GDN_H)
            dqkv, dconv = conv_bwd(sv["pm"], sv["conv_w"], dqkvc)
            gs.setdefault("gdn_conv_w", [None] * 2)[i] = dconv
            dpm = jnp.concatenate([du, dqkv, dgate], axis=1).astype(BF16)
            dabt = jnp.concatenate([dal, dbl], axis=0)
            dw_main = matmul("tn", dpm, sv["h"], "proj_dw", out_dtype=BF16)
            dw_ab = matmul("nn", dabt, sv["h"], "proj_ab_dw", out_dtype=BF16)
            gw["ab_w_in"] = jnp.concatenate([dw_main, dw_ab], axis=0)
            dx = gsink(layer, "B", gw, dx)
            w_in = w["ab_w_in"]
            dh = matmul("nn", dpm, w_in[:2560], "proj_dh")
            dx, dg = matmul("tn", dabt, w_in[2560:], "proj_ab_dh", add=dh,
                            norm_bwd=(sv["x0"], row2(p["ab_norm_g"][i]), dx))
            gs.setdefault("ab_norm_g", [None] * 2)[i] = dg[0]
        else:
            do = matmul("nt", dx, w["c_w_out"], "mix_dcat")
            gw["c_w_out"] = matmul("tn", sv["o"], dx, "mix_dwo", out_dtype=BF16)
            dqn, dkp, dvp, dbias = cattn_bwd(sv["qn"], sv["kp"], sv["vp"], sv["bias"], do)
            gs.setdefault("c_rel_bias", [None] * 2)[i] = rel_bias_grad(dbias.reshape(CA_H, CHUNK, CA_BAND))
            dq, dqg = row_bwd("headnorm_bwd", f_headnorm, [(sv["qkv"], D, 0)], [sv["qg"]], [dqn], [0], [0])
            dk, dkg = row_bwd("headnorm_bwd", f_headnorm, [(sv["qkv"], D, 1)], [sv["kg"]],
                              [(dkp, D, 0, CA_PAD)], [0], [0])
            head_sum = lambda t: (jnp.sum(t, axis=0, keepdims=True),)
            (dqg,) = whole("headgain_sum", head_sum, [dqg.reshape(CA_H, CA_D)], [((1, CA_D), F32)])
            (dkg,) = whole("headgain_sum", head_sum, [dkg.reshape(CA_H, CA_D)], [((1, CA_D), F32)])
            gs.setdefault("c_q_norm_g", [None] * 2)[i] = dqg[0]
            gs.setdefault("c_k_norm_g", [None] * 2)[i] = dkg[0]
            dqkv = jnp.concatenate([dq, dk, dvp[CA_PAD:]], axis=1).astype(BF16)
            gw["c_w_qkv"] = matmul("tn", dqkv, sv["h"], "proj_qkv_dw", out_dtype=BF16)
            dx = gsink(layer, "B", gw, dx)
            dx, dg = matmul("nn", dqkv, w["c_w_qkv"], "proj_qkv_dh",
                            norm_bwd=(sv["x0"], row2(w["c_norm_g"]), dx))
            gs.setdefault("c_norm_g", [None] * 2)[i] = dg[0]
    (dmg,) = row_bwd("mem_norm_bwd", f_norm, [mem], [row2(p["mem_norm_g"])], [dmem_n], [], [0])
    small = {n: jnp.stack(v) for n, v in gs.items()}
    small["mem_norm_g"] = dmg[0]
    return loss_vec, dx, small


ADAM_BLOCK_BYTES = 3 << 19


def adam(w, g, m, v, name):
    shape = w.shape
    if w.ndim == 3:
        d0, n, d2 = shape
        fits = [t for t in range(8, n + 1, 8) if n % t == 0 and d0 * t * d2 * 4 <= ADAM_BLOCK_BYTES]
        return tuple(row_fwd(name, f_adam, [w, g, m, v], [], [((d0, d2), F32)] * 3, tile=max(fits)))
    cols = shape[-1]
    w2, g2, m2, v2 = (t.reshape(-1, cols) for t in (w, g, m, v))
    rows = w2.shape[0]
    tile = rows if rows <= 512 else _tile(rows, 512, 8)
    outs = row_fwd(name, f_adam, [w2, g2, m2, v2], [], [(cols, F32)] * 3, tile=tile)
    return tuple(o.reshape(shape) for o in outs)


WEIGHTS = ['ab_norm_g', 'ab_w_in', 'ab_w_out', 's5_a_re', 's5_a_im', 's5_log_dt', 's5_b_re', 's5_b_im', 's5_c_re',
           's5_c_im', 's5_d', 's5_w_glu', 's5_b_glu', 'gdn_conv_w', 'gdn_a_log', 'gdn_dt_bias', 'gdn_out_norm_g',
           'c_norm_g', 'c_w_qkv', 'c_w_out', 'c_q_norm_g', 'c_k_norm_g', 'c_rel_bias', 'mem_norm_g', 'xa_norm_g',
           'xa_w_q', 'xa_w_kv', 'xa_w_out', 'xa_q_norm_g', 'xa_k_norm_g', 'f_norm_g', 'f_w_gate', 'f_w_up',
           'f_w_down']
SHARDED_SMALL = {"gdn_conv_w": (2, 384), "c_norm_g": (1, 256)}


def kernel(x, mem, ab_norm_g, ab_w_in, ab_w_out, s5_a_re, s5_a_im, s5_log_dt, s5_b_re, s5_b_im, s5_c_re, s5_c_im, s5_d, s5_w_glu, s5_b_glu, gdn_conv_w, gdn_a_log, gdn_dt_bias, gdn_out_norm_g, c_norm_g, c_w_qkv, c_w_out, c_q_norm_g, c_k_norm_g, c_rel_bias, mem_norm_g, xa_norm_g, xa_w_q, xa_w_kv, xa_w_out, xa_q_norm_g, xa_k_norm_g, f_norm_g, f_w_gate, f_w_up, f_w_down, loss_target, m_ab_norm_g, m_ab_w_in, m_ab_w_out, m_s5_a_re, m_s5_a_im, m_s5_log_dt, m_s5_b_re, m_s5_b_im, m_s5_c_re, m_s5_c_im, m_s5_d, m_s5_w_glu, m_s5_b_glu, m_gdn_conv_w, m_gdn_a_log, m_gdn_dt_bias, m_gdn_out_norm_g, m_c_norm_g, m_c_w_qkv, m_c_w_out, m_c_q_norm_g, m_c_k_norm_g, m_c_rel_bias, m_mem_norm_g, m_xa_norm_g, m_xa_w_q, m_xa_w_kv, m_xa_w_out, m_xa_q_norm_g, m_xa_k_norm_g, m_f_norm_g, m_f_w_gate, m_f_w_up, m_f_w_down, v_ab_norm_g, v_ab_w_in, v_ab_w_out, v_s5_a_re, v_s5_a_im, v_s5_log_dt, v_s5_b_re, v_s5_b_im, v_s5_c_re, v_s5_c_im, v_s5_d, v_s5_w_glu, v_s5_b_glu, v_gdn_conv_w, v_gdn_a_log, v_gdn_dt_bias, v_gdn_out_norm_g, v_c_norm_g, v_c_w_qkv, v_c_w_out, v_c_q_norm_g, v_c_k_norm_g, v_c_rel_bias, v_mem_norm_g, v_xa_norm_g, v_xa_w_q, v_xa_w_kv, v_xa_w_out, v_xa_q_norm_g, v_xa_k_norm_g, v_f_norm_g, v_f_w_gate, v_f_w_up, v_f_w_down):
    args = locals()
    p = {n: args[n] for n in WEIGHTS}
    m = {n: args["m_" + n] for n in WEIGHTS}
    v = {n: args["v_" + n] for n in WEIGHTS}
    chip = 2 * lax.axis_index("x") + lax.axis_index("y")

    carry = x[0]
    gathers = {}
    for layer in range(DEPTH):
        for grp in GROUPS:
            src = pack_group_shards(p, layer, grp)
            land = lax.dynamic_update_slice(lax.empty((N_CHIPS,) + src.shape, BF16), src[None], (chip, 0, 0))
            send_sems, recv_sems, src, land, carry = exchange_start(
                src, land, carry, f"gather_start_{layer}{grp}", mode="gather_half")
            gathers[layer, grp] = (src, land, send_sems, recv_sems)
    forwards = {}

    def wpre(layer, grp, carry):
        if (layer, grp) not in forwards:
            src, land, send_sems, recv_sems = gathers[layer, grp]
            src, land = exchange_wait(src, land, send_sems, recv_sems, carry, f"gather_wait_{layer}{grp}",
                                      mode="gather_half")
            send_sems, recv_sems, src, land, carry = exchange_start(
                src, land, carry, f"forward_start_{layer}{grp}", mode="forward_half")
            forwards[layer, grp] = (src, land, send_sems, recv_sems)
        return carry

    def wsrc(layer, grp, after):
        wpre(layer, grp, after)
        src, land, send_sems, recv_sems = forwards[layer, grp]
        _, land = exchange_wait(src, land, send_sems, recv_sems, after, f"forward_wait_{layer}{grp}",
                                mode="forward_half")
        return unpack_group_gathered(land, layer, grp)

    scatters, siblings = [], []
    LAG = 2

    def finish(carry):
        layer, grp, src, land, send_sems, recv_sems = scatters[len(siblings)]
        src, land = exchange_wait(src, land, send_sems, recv_sems, carry, f"scatter_wait_{layer}{grp}", mode="scatter")
        part = sum_own_slabs(src, chip, land, "sum_chips")
        send_sems, recv_sems, part, other, carry = exchange_start(
            part, lax.empty(part.shape, F32), carry, f"sibling_start_{layer}{grp}", mode="sibling")
        siblings.append((layer, grp, part, other, send_sems, recv_sems))
        return carry

    def gsink(layer, grp, gw, carry):
        src = pack_group_grads(gw, layer, grp)
        land = lax.empty((3,) + src.shape[1:], BF16)
        send_sems, recv_sems, src, land, carry = exchange_start(
            src, land, carry, f"scatter_start_{layer}{grp}", mode="scatter")
        scatters.append((layer, grp, src, land, send_sems, recv_sems))
        if len(scatters) > LAG:
            carry = finish(carry)
        return carry

    carry = wpre(0, "B", carry)
    loss_vec, dx, g_small = local_step(carry, mem[0], loss_target[0], p, wsrc, gsink, wpre)

    full_shapes = {n: ((2, 4, 1536) if n == "gdn_conv_w" else (2, D) if n == "c_norm_g" else p[n].shape)
                   for n in SMALL}
    small_mine = pack_small(g_small, extra=loss_vec)
    (small_other,) = sibling_exchange([small_mine], "sibling_small")
    (small_chip,) = row_fwd("sum_small_cores", lambda a, b: (a + b,), [small_mine, small_other], [], [(128, F32)],
                            tile=128)
    small_land = lax.dynamic_update_slice(lax.empty((N_CHIPS,) + small_chip.shape, F32), small_chip[None], (chip, 0, 0))
    small_sems = exchange_start(small_chip, small_land, dx, "gather_small_start", mode="gather")
    dx = small_sems[4]

    while len(siblings) < len(scatters):
        dx = finish(dx)

    per_layer = {}
    for layer, grp, part, other, send_sems, recv_sems in siblings:
        part, other = exchange_wait(part, other, send_sems, recv_sems, dx, f"sibling_wait_{layer}{grp}",
                                    mode="sibling")
        (total,) = row_fwd("sum_cores", lambda a, b: (a + b,), [part, other], [], [(D, F32)],
                           tile=_tile(part.shape[0], 512, 16))
        for name, g in unpack_group_reduced(total, layer, grp).items():
            per_layer.setdefault(name, {})[layer] = g
    grads = {name: jnp.stack([d[k] for k in sorted(d)]) for name, d in per_layer.items()}

    send_sems, recv_sems, small_chip, small_land, _ = small_sems
    _, small_land = exchange_wait(small_chip, small_land, send_sems, recv_sems, dx, "gather_small_wait", mode="gather")
    g_s, rest = unpack_small(sum_slabs(small_land, "sum_small"), full_shapes)
    (loss11,) = whole("loss_sum", lambda t: (jnp.sum(jnp.sum(t, axis=1, keepdims=True), axis=0, keepdims=True),),
                      [rest[:D // 128]], [((1, 1), F32)])
    for n, (axis, width) in SHARDED_SMALL.items():
        g_s[n] = lax.dynamic_slice_in_dim(g_s[n], chip * width, width, axis=axis)
    grads.update(g_s)

    delta, new_m, new_v = {}, {}, {}
    for n in SMALL:
        shape = p[n].shape
        two = (1, shape[0]) if len(shape) == 1 else (int(np.prod(shape[:-1])), shape[-1])
        outs = whole("adam_" + n, f_adam, [t.reshape(two) for t in (p[n], g_s[n], m[n], v[n])], [(two, F32)] * 3)
        delta[n], new_m[n], new_v[n] = (o.reshape(shape) for o in outs)
    for name in BIG:
        if name in ADAM_TRANSPOSED:
            t = lambda a: jnp.swapaxes(a, 1, 2)
            outs = adam(t(p[name]), grads[name], t(m[name]), t(v[name]), "adam_" + name)
            delta[name], new_m[name], new_v[name] = (t(o) for o in outs)
            grads[name] = t(grads[name])
        else:
            delta[name], new_m[name], new_v[name] = adam(p[name], grads[name], m[name], v[name], "adam_" + name)

    return (loss11[0, 0], dx[None], *[grads[n] for n in WEIGHTS], *[delta[n] for n in WEIGHTS],
            *[new_m[n] for n in WEIGHTS], *[new_v[n] for n in WEIGHTS])
```
